```python
import jax
import jax.numpy as jnp
from jax import lax
import numpy as np

D_MODEL = 1024
BATCH = 16
SEQ = 256
DEPTH = 2
DEC_BATCH = 2
DEC_SEQ = 2048
PAST_LEN = 256

GRID_W = 64
N_DIR = 2
HM = 4
DK_M = 256
DV_M = 256
DM_A = HM * DV_M
HR = 4
DK_R = 256
DV_R = 256
DM_B = HR * DV_R
N_EXPERTS = 16
EC_FACTOR = 2
D_FF = 2816
CHUNK = 128
ROPE_BASE = 10000.0
EPS = 1e-6
F_BIAS_LO = 3.0
F_BIAS_HI = 6.0

_IN_WIDTHS = (HM * DK_M, HM * DK_M, DM_A, DM_A, N_DIR * HM, N_DIR * HM,
              HR * DK_R, HR * DK_R, DM_B, DM_B, D_MODEL, D_MODEL)
N_IN = sum(_IN_WIDTHS)
_IN_SPLITS = tuple(int(s) for s in np.cumsum(_IN_WIDTHS)[:-1])

kernel_name = 'bidir_mlstm_retention_ec_diffusion_step'


def _rmsnorm(x, w):
    xf = x.astype(jnp.float32)
    y = xf * lax.rsqrt(jnp.mean(xf * xf, axis=-1, keepdims=True) + EPS)
    return (y * w.astype(jnp.float32)).astype(x.dtype)


def _head_rmsnorm(h, w):
    b, nh, t, d = h.shape
    y = h * lax.rsqrt(jnp.mean(h * h, axis=-1, keepdims=True) + EPS)
    y = y * w.astype(jnp.float32).reshape(nh, 1, d)
    return y.transpose(0, 2, 1, 3).reshape(b, t, nh * d)


def _adaln(cond, w_mod, b_mod):
    mod = jax.nn.silu(cond) @ w_mod + b_mod
    return [m[:, None, :] for m in jnp.split(mod, 6, axis=-1)]


def _to_heads(x, nh):
    b, t, _ = x.shape
    return x.reshape(b, t, nh, -1).transpose(0, 2, 1, 3).astype(jnp.float32)


def _rope_2d(t):
    rows = t // GRID_W
    row = jnp.repeat(jnp.arange(rows, dtype=jnp.float32), GRID_W)
    col = jnp.tile(jnp.arange(GRID_W, dtype=jnp.float32), rows)
    n_freq = DK_R // 4
    inv = ROPE_BASE ** (-jnp.arange(n_freq, dtype=jnp.float32) / n_freq)
    ang = jnp.concatenate([row[:, None] * inv, col[:, None] * inv], axis=-1)
    return jnp.cos(ang), jnp.sin(ang)


def _apply_rope(x, cos, sin):
    half = x.shape[-1] // 2
    x1, x2 = x[..., :half], x[..., half:]
    return jnp.concatenate([x1 * cos - x2 * sin, x1 * sin + x2 * cos], axis=-1)


def _chunks(a, nc):
    return jnp.moveaxis(a.reshape(a.shape[:2] + (nc, CHUNK) + a.shape[3:]), 2, 0)


def _unchunk(h):
    h = jnp.moveaxis(h, 0, 2)
    return h.reshape(h.shape[0], h.shape[1], -1, h.shape[-1])


def _flip(a):
    return jnp.flip(a, axis=2)


def _mlstm_scan(q, k, v, ig, lf, c0, n0, m0):
    nc = q.shape[2] // CHUNK
    causal = jnp.tril(jnp.ones((CHUNK, CHUNK), dtype=bool))

    def step(carry, inp):
        cm, n, m = carry
        qc, kc, vc, ic, fc = inp
        b = jnp.cumsum(fc, axis=-1)
        dmat = jnp.where(causal, b[..., :, None] - b[..., None, :] + ic[..., None, :], -jnp.inf)
        inter = b + m[..., None]
        m_row = jnp.maximum(dmat.max(axis=-1), inter)
        w = jnp.exp(dmat - m_row[..., None])
        w_inter = jnp.exp(inter - m_row)
        s = jnp.einsum('bhid,bhjd->bhij', qc, kc) * w
        num = jnp.einsum('bhij,bhje->bhie', s, vc) + w_inter[..., None] * jnp.einsum('bhid,bhde->bhie', qc, cm)
        den = s.sum(axis=-1) + w_inter * jnp.einsum('bhid,bhd->bhi', qc, n)
        h = num / jnp.maximum(jnp.abs(den), jnp.exp(-m_row))[..., None]
        b_last = b[..., -1]
        wj = b_last[..., None] - b + ic
        m_new = jnp.maximum(b_last + m, wj.max(axis=-1))
        decay = jnp.exp(b_last + m - m_new)
        wk = jnp.exp(wj - m_new[..., None])[..., None] * kc
        c_new = decay[..., None, None] * cm + jnp.einsum('bhjd,bhje->bhde', wk, vc)
        n_new = decay[..., None] * n + wk.sum(axis=2)
        return (c_new, n_new, m_new), h

    xs = (_chunks(q, nc), _chunks(k, nc), _chunks(v, nc), _chunks(ig, nc), _chunks(lf, nc))
    (cm, n, m), h = lax.scan(step, (c0, n0, m0), xs)
    return _unchunk(h), (cm, n, m)


def _mlstm_bidir(q, k, v, ig, lf, c0, n0, m0):
    h_f, (cf, nf, mf) = _mlstm_scan(q, k, v, ig[:, 0], lf[:, 0], c0[:, 0], n0[:, 0], m0[:, 0])
    h_b, (cb, nb, mb) = _mlstm_scan(_flip(q), _flip(k), _flip(v), _flip(ig[:, 1]), _flip(lf[:, 1]),
                                    c0[:, 1], n0[:, 1], m0[:, 1])
    h = h_f + _flip(h_b)
    return h, (jnp.stack([cf, cb], axis=1), jnp.stack([nf, nb], axis=1), jnp.stack([mf, mb], axis=1))


def _retention_scan(q, k, v, log_gamma, s0):
    nc = q.shape[2] // CHUNK
    pos = jnp.arange(CHUNK, dtype=jnp.float32)
    rel = pos[:, None] - pos[None, :]
    lg = log_gamma[:, None, None]
    dmat = jnp.where(rel >= 0, jnp.exp(jnp.maximum(rel, 0.0)[None] * lg), 0.0)
    q_decay = jnp.exp((pos + 1.0) * log_gamma[:, None])
    k_decay = jnp.exp((CHUNK - 1.0 - pos) * log_gamma[:, None])
    chunk_decay = jnp.exp(CHUNK * log_gamma)

    def step(s, inp):
        qc, kc, vc = inp
        sc = jnp.einsum('bhid,bhjd->bhij', qc, kc) * dmat
        o = jnp.einsum('bhij,bhje->bhie', sc, vc) + q_decay[..., None] * jnp.einsum('bhid,bhde->bhie', qc, s)
        s_new = chunk_decay[:, None, None] * s + jnp.einsum('bhjd,bhje->bhde', kc * k_decay[..., None], vc)
        return s_new, o

    s, o = lax.scan(step, s0, (_chunks(q, nc), _chunks(k, nc), _chunks(v, nc)))
    return _unchunk(o), s


def _ret_bidir(q, k, v, log_gamma, s0):
    o_f, sf = _retention_scan(q, k, v, log_gamma[0], s0[:, 0])
    o_b, sb = _retention_scan(_flip(q), _flip(k), _flip(v), log_gamma[1], s0[:, 1])
    return o_f + _flip(o_b), jnp.stack([sf, sb], axis=1)


def _token_mix(h, w_in, if_b, m_norm_w, decay_logit, r_norm_w, w_a, w_b, w_out, st, rope):
    bsz, t, _ = h.shape
    proj = h @ w_in
    mq, mk, mv, mo, mi, mf, rq, rk, rv, rg, ga, gb = jnp.split(proj, _IN_SPLITS, axis=-1)
    q = _to_heads(mq, HM)
    k = _to_heads(mk, HM) * (DK_M ** -0.5)
    v = _to_heads(mv, HM)
    if_b = if_b.astype(jnp.float32)
    ig = (mi.astype(jnp.float32).reshape(bsz, t, N_DIR, HM) + if_b[:, 0]).transpose(0, 2, 3, 1)
    lf = jax.nn.log_sigmoid(mf.astype(jnp.float32).reshape(bsz, t, N_DIR, HM) + if_b[:, 1]).transpose(0, 2, 3, 1)
    hm, (cm, n, m) = _mlstm_bidir(q, k, v, ig, lf, st[0], st[1], st[2])
    hm = jax.nn.sigmoid(mo.astype(jnp.float32)) * _head_rmsnorm(hm, m_norm_w)
    rqh = _to_heads(rq, HR)
    rkh = _to_heads(rk, HR)
    if rope is not None:
        rqh = _apply_rope(rqh, rope[0], rope[1])
        rkh = _apply_rope(rkh, rope[0], rope[1])
    rkh = rkh * (DK_R ** -0.5)
    rvh = _to_heads(rv, HR)
    log_gamma = jax.nn.log_sigmoid(decay_logit.astype(jnp.float32))
    hr, s = _ret_bidir(rqh, rkh, rvh, log_gamma, st[3])
    hr = jax.nn.silu(rg.astype(jnp.float32)) * _head_rmsnorm(hr, r_norm_w)
    dt = h.dtype
    ya = hm.astype(dt) @ w_a
    yb = hr.astype(dt) @ w_b
    merged = jax.nn.sigmoid(ga) * ya + jax.nn.sigmoid(gb) * yb
    return merged @ w_out, (cm, n, m, s)


def _expert_choice_ffn(h, router_w, w1, w3, w2):
    bsz, t, d = h.shape
    tok = h.reshape(bsz * t, d)
    cap = EC_FACTOR * (bsz * t) // N_EXPERTS
    affinity = jax.nn.softmax((tok @ router_w).astype(jnp.float32), axis=-1)
    gate, idx = lax.top_k(affinity.T, cap)
    xe = tok[idx]
    he = jax.nn.silu(jnp.einsum('ecd,edf->ecf', xe, w1)) * jnp.einsum('ecd,edf->ecf', xe, w3)
    ye = jnp.einsum('ecf,efd->ecd', he, w2) * gate[..., None].astype(h.dtype)
    out = jnp.zeros_like(tok).at[idx.reshape(-1)].add(ye.reshape(-1, d))
    return out.reshape(bsz, t, d)


def _layer(x, cond, w_mod, b_mod, norm1_w, norm2_w, w_in, if_b, m_norm_w, decay_logit, r_norm_w,
           w_a, w_b, w_out, router_w, w1, w3, w2, st, rope):
    sh1, sc1, g1, sh2, sc2, g2 = _adaln(cond, w_mod, b_mod)
    h = _rmsnorm(x, norm1_w) * (1.0 + sc1) + sh1
    y, new_st = _token_mix(h, w_in, if_b, m_norm_w, decay_logit, r_norm_w, w_a, w_b, w_out, st, rope)
    x = x + g1 * y
    h = _rmsnorm(x, norm2_w) * (1.0 + sc2) + sh2
    x = x + g2 * _expert_choice_ffn(h, router_w, w1, w3, w2)
    return x, new_st


def setup_inputs(seed: int = 0) -> dict:
    key = jax.random.key(seed)
    ks = jax.random.split(key, 26)

    def nrm(k, shape, scale):
        return jax.random.normal(k, shape, jnp.float32) * scale

    gam = 1.0 - 2.0 ** (-5.0 - np.arange(HR))
    decay_logit0 = jnp.asarray(np.log(gam / (1.0 - gam)), jnp.float32)
    f_bias = jnp.linspace(F_BIAS_LO, F_BIAS_HI, HM, dtype=jnp.float32)
    if_base = jnp.stack([jnp.zeros((HM,), jnp.float32), f_bias])[None, None]
    return {
        'x_prompt': nrm(ks[0], (BATCH, SEQ, D_MODEL), 1.0),
        'x_sample': nrm(ks[1], (DEC_BATCH, DEC_SEQ, D_MODEL), 1.0),
        'c': nrm(ks[2], (DEC_BATCH, D_MODEL), 1.0),
        'state_mlstm_C': nrm(ks[3], (DEC_BATCH, DEPTH, N_DIR, HM, DK_M, DV_M), 0.05),
        'state_mlstm_n': nrm(ks[4], (DEC_BATCH, DEPTH, N_DIR, HM, DK_M), 0.05),
        'state_mlstm_m': 3.0 + nrm(ks[5], (DEC_BATCH, DEPTH, N_DIR, HM), 0.5),
        'state_ret_S': nrm(ks[6], (DEC_BATCH, DEPTH, N_DIR, HR, DK_R, DV_R), 0.1),
        'c_ctx': nrm(ks[7], (D_MODEL,), 1.0),
        'w_mod': nrm(ks[8], (DEPTH, D_MODEL, 6 * D_MODEL), 0.5 * D_MODEL ** -0.5),
        'b_mod': nrm(ks[9], (DEPTH, 6 * D_MODEL), 0.02),
        'norm1_w': 1.0 + nrm(ks[10], (DEPTH, D_MODEL), 0.02),
        'norm2_w': 1.0 + nrm(ks[11], (DEPTH, D_MODEL), 0.02),
        'w_in': nrm(ks[12], (DEPTH, D_MODEL, N_IN), D_MODEL ** -0.5),
        'mlstm_if_b': if_base + nrm(ks[13], (DEPTH, N_DIR, 2, HM), 0.1),
        'mlstm_norm_w': 1.0 + nrm(ks[14], (DEPTH, DM_A), 0.02),
        'ret_decay_logit': decay_logit0 + nrm(ks[15], (DEPTH, N_DIR, HR), 0.05),
        'ret_norm_w': 1.0 + nrm(ks[16], (DEPTH, DM_B), 0.02),
        'w_branch_a': nrm(ks[17], (DEPTH, DM_A, D_MODEL), DM_A ** -0.5),
        'w_branch_b': nrm(ks[18], (DEPTH, DM_B, D_MODEL), DM_B ** -0.5),
        'w_out': nrm(ks[19], (DEPTH, D_MODEL, D_MODEL), D_MODEL ** -0.5),
        'router_w': nrm(ks[20], (DEPTH, D_MODEL, N_EXPERTS), D_MODEL ** -0.5),
        'ffn_w1': nrm(ks[21], (DEPTH, N_EXPERTS, D_MODEL, D_FF), D_MODEL ** -0.5),
        'ffn_w3': nrm(ks[22], (DEPTH, N_EXPERTS, D_MODEL, D_FF), D_MODEL ** -0.5),
        'ffn_w2': nrm(ks[23], (DEPTH, N_EXPERTS, D_FF, D_MODEL), D_FF ** -0.5),
        'final_norm_w': 1.0 + nrm(ks[24], (D_MODEL,), 0.02),
    }


def reference(x_prompt, x_sample, c, state_mlstm_C, state_mlstm_n, state_mlstm_m, state_ret_S,
              c_ctx, w_mod, b_mod, norm1_w, norm2_w, w_in, mlstm_if_b, mlstm_norm_w,
              ret_decay_logit, ret_norm_w, w_branch_a, w_branch_b, w_out, router_w,
              ffn_w1, ffn_w3, ffn_w2, final_norm_w):
    f32 = jnp.float32
    bp = x_prompt.shape[0]
    zero_st = (jnp.zeros((bp, N_DIR, HM, DK_M, DV_M), f32), jnp.zeros((bp, N_DIR, HM, DK_M), f32),
               jnp.zeros((bp, N_DIR, HM), f32), jnp.zeros((bp, N_DIR, HR, DK_R, DV_R), f32))
    new_c, new_n, new_m, new_s = [], [], [], []
    xp = x_prompt
    for l in range(DEPTH):
        xp, (cm, n, m, s) = _layer(xp, c_ctx[None, :], w_mod[l], b_mod[l], norm1_w[l], norm2_w[l],
                                   w_in[l], mlstm_if_b[l], mlstm_norm_w[l], ret_decay_logit[l],
                                   ret_norm_w[l], w_branch_a[l], w_branch_b[l], w_out[l], router_w[l],
                                   ffn_w1[l], ffn_w3[l], ffn_w2[l], zero_st, None)
        new_c.append(cm)
        new_n.append(n)
        new_m.append(m)
        new_s.append(s)
    y_prompt = _rmsnorm(xp, final_norm_w)

    rope = _rope_2d(x_sample.shape[1])
    xs = x_sample
    for l in range(DEPTH):
        st = (state_mlstm_C[:, l].astype(f32), state_mlstm_n[:, l].astype(f32),
              state_mlstm_m[:, l].astype(f32), state_ret_S[:, l].astype(f32))
        xs, _ = _layer(xs, c, w_mod[l], b_mod[l], norm1_w[l], norm2_w[l], w_in[l], mlstm_if_b[l],
                       mlstm_norm_w[l], ret_decay_logit[l], ret_norm_w[l], w_branch_a[l],
                       w_branch_b[l], w_out[l], router_w[l], ffn_w1[l], ffn_w3[l], ffn_w2[l], st, rope)
    y_sample = _rmsnorm(xs, final_norm_w)
    return (y_prompt, y_sample, jnp.stack(new_c, axis=1), jnp.stack(new_n, axis=1),
            jnp.stack(new_m, axis=1), jnp.stack(new_s, axis=1))
```

```python
import functools

import numpy as np
import jax
import jax.numpy as jnp
from jax import lax
from jax.experimental import pallas as pl
from jax.experimental.pallas import tpu as pltpu

F32 = jnp.float32
BF16 = jnp.bfloat16

GRID_W = 64
CHUNK = 128
LANES = 128
HEAD_DIM = 256
N_EXPERTS = 16
EC_FACTOR = 2
ROPE_BASE = 10000.0
EPS = 1e-6
V7X_VMEM_BYTES = 64 * 1024 * 1024
MIB = 1024 * 1024


def _cparams(semantics, vmem_mib):
    assert vmem_mib * MIB < V7X_VMEM_BYTES
    return pltpu.CompilerParams(dimension_semantics=semantics, vmem_limit_bytes=vmem_mib * MIB)


def _dot(a, b):
    return jnp.dot(a, b, preferred_element_type=F32)


def _dot_nt(a, b):
    return lax.dot_general(a, b, (((1,), (1,)), ((), ())), preferred_element_type=F32)


def _log_sigmoid(x):
    return -(jnp.maximum(-x, 0.0) + jnp.log1p(jnp.exp(-jnp.abs(x))))


def _split3(x):
    hi = x.astype(BF16)
    r1 = x - hi.astype(F32)
    mid = r1.astype(BF16)
    lo = (r1 - mid.astype(F32)).astype(BF16)
    return hi, mid, lo


def _dot3(x, m):
    hi, mid, lo = _split3(x)
    return _dot(hi, m) + _dot(mid, m) + _dot(lo, m)


def _dot3_left(m, x):
    hi, mid, lo = _split3(x)
    return _dot(m, hi) + _dot(m, mid) + _dot(m, lo)


def _iota(shape, dim):
    return lax.broadcasted_iota(jnp.int32, shape, dim)


def _mod_kernel(cond_ref, w_ref, b_ref, out_ref):
    c = cond_ref[...]
    s = (c * jax.nn.sigmoid(c)).astype(BF16)
    out_ref[...] = _dot(s, w_ref[...].astype(BF16)) + b_ref[...]


def _mod_call(cond8, w_mod, b_mod):
    depth, d, w6 = w_mod.shape
    tn = 1536
    return pl.pallas_call(
        _mod_kernel,
        grid=(depth, w6 // tn),
        in_specs=[
            pl.BlockSpec((8, d), lambda l, j: (0, 0)),
            pl.BlockSpec((None, d, tn), lambda l, j: (l, 0, j)),
            pl.BlockSpec((None, 1, tn), lambda l, j: (l, 0, j)),
        ],
        out_specs=pl.BlockSpec((None, 8, tn), lambda l, j: (l, 0, j)),
        out_shape=jax.ShapeDtypeStruct((depth, 8, w6), F32),
        compiler_params=_cparams(("parallel", "parallel"), 32),
        name="adaln_mod",
    )(cond8, w_mod, b_mod.reshape(depth, 1, w6))


def _group_of_tile(i, tm, n_prompt, dec_seq):
    return jnp.maximum(i * tm - n_prompt + dec_seq, 0) // dec_seq


def _resid_norm_kernel(*refs, has_delta, has_mod, gate_row, mod_rows, write_x):
    it = iter(refs)
    x_ref = next(it)
    delta_ref = next(it) if has_delta else None
    mod_ref = next(it) if (has_delta or has_mod) else None
    nw_ref = next(it)
    xo_ref = next(it) if write_x else None
    h_ref = next(it)
    x = x_ref[...]
    if has_delta:
        x = x + mod_ref[gate_row:gate_row + 1, :] * delta_ref[...]
    if write_x:
        xo_ref[...] = x
    y = x * lax.rsqrt(jnp.mean(x * x, axis=-1, keepdims=True) + EPS) * nw_ref[...]
    if has_mod:
        sh_row, sc_row = mod_rows
        y = y * (1.0 + mod_ref[sc_row:sc_row + 1, :]) + mod_ref[sh_row:sh_row + 1, :]
    h_ref[...] = y.astype(h_ref.dtype)


def _resid_norm_call(x, delta, mod_gate, mod_norm, norm_w, *, gate_row, mod_rows, h_dtype, write_x,
                     n_prompt, dec_seq, tm=512):
    n, d = x.shape
    has_delta = delta is not None
    has_mod = mod_norm is not None
    grp = functools.partial(_group_of_tile, tm=tm, n_prompt=n_prompt, dec_seq=dec_seq)
    tile = pl.BlockSpec((tm, d), lambda i: (i, 0))
    args, specs = [x], [tile]
    if has_delta:
        args.append(delta)
        specs.append(tile)
    if has_delta or has_mod:
        mg = mod_gate if has_delta else mod_norm
        mn = mod_norm if has_mod else mod_gate
        args.append(jnp.concatenate([mg, mn], axis=1))
        specs.append(pl.BlockSpec((None, 12, d), lambda i: (grp(i), 0, 0)))
    args.append(norm_w.reshape(1, d))
    specs.append(pl.BlockSpec((1, d), lambda i: (0, 0)))
    out_shape, out_specs = [], []
    if write_x:
        out_shape.append(jax.ShapeDtypeStruct((n, d), F32))
        out_specs.append(tile)
    out_shape.append(jax.ShapeDtypeStruct((n, d), h_dtype))
    out_specs.append(tile)
    kern = functools.partial(
        _resid_norm_kernel, has_delta=has_delta, has_mod=has_mod, gate_row=gate_row,
        mod_rows=None if mod_rows is None else (6 + mod_rows[0], 6 + mod_rows[1]), write_x=write_x)
    outs = pl.pallas_call(
        kern, grid=(n // tm,), in_specs=specs, out_specs=out_specs, out_shape=out_shape,
        compiler_params=_cparams(("parallel",), 32), name="resid_norm",
    )(*args)
    return outs if write_x else (None, outs[0])


def _rope_pair(x1, x2, cos, sin):
    return x1 * cos - x2 * sin, x1 * sin + x2 * cos


def _proj_kernel(h_ref, w_ref, *rest, tm, rope, n_prompt, dec_seq):
    if rope:
        cos_ref, sin_ref, out_ref, wb_ref = rest
    else:
        out_ref, wb_ref = rest
    i = pl.program_id(1)

    @pl.when(i == 0)
    def _():
        wb_ref[...] = w_ref[...].astype(BF16)

    row0 = pl.multiple_of(i * tm, tm)
    acc = _dot(h_ref[pl.ds(row0, tm), :], wb_ref[...])
    if not rope:
        out_ref[...] = acc.astype(out_ref.dtype)
        return

    @pl.when(row0 < n_prompt)
    def _():
        out_ref[...] = acc.astype(out_ref.dtype)

    @pl.when(row0 >= n_prompt)
    def _():
        pos0 = pl.multiple_of((row0 - n_prompt) % dec_seq, tm)
        cos = cos_ref[pl.ds(pos0, tm), :]
        sin = sin_ref[pl.ds(pos0, tm), :]
        half = HEAD_DIM // 2
        for hh in range(acc.shape[1] // HEAD_DIM):
            c0 = hh * HEAD_DIM
            y1, y2 = _rope_pair(acc[:, c0:c0 + half], acc[:, c0 + half:c0 + HEAD_DIM], cos, sin)
            out_ref[:, c0:c0 + half] = y1.astype(out_ref.dtype)
            out_ref[:, c0 + half:c0 + HEAD_DIM] = y2.astype(out_ref.dtype)


def _proj_call(h, w, layer, col_blocks, out_dtype, *, rope=None, n_prompt=0, dec_seq=1, tm=512, tn=1024):
    n, d = h.shape
    nj = len(col_blocks)
    step = col_blocks[1] - col_blocks[0] if nj > 1 else 0
    assert all(col_blocks[k] == col_blocks[0] + k * step for k in range(nj))
    cb0 = col_blocks[0]
    args = [h, w]
    specs = [
        pl.BlockSpec((n, d), lambda j, i: (0, 0)),
        pl.BlockSpec((None, d, tn), lambda j, i: (layer, 0, cb0 + j * step)),
    ]
    if rope is not None:
        args += list(rope)
        specs += [pl.BlockSpec(rope[0].shape, lambda j, i: (0, 0))] * 2
    kern = functools.partial(_proj_kernel, tm=tm, rope=rope is not None, n_prompt=n_prompt, dec_seq=dec_seq)
    return pl.pallas_call(
        kern, grid=(nj, n // tm), in_specs=specs,
        out_specs=pl.BlockSpec((tm, tn), lambda j, i: (i, j)),
        out_shape=jax.ShapeDtypeStruct((n, nj * tn), out_dtype),
        scratch_shapes=[pltpu.VMEM((d, tn), BF16)],
        compiler_params=_cparams(("parallel", "arbitrary"), 56), name="in_proj",
    )(*args)


def _proj_t_kernel(h_ref, w_ref, *rest, tm, rope, n_prompt, dec_seq):
    if rope:
        cos_ref, sin_ref, out_ref, wb_ref = rest
    else:
        out_ref, wb_ref = rest
    i = pl.program_id(1)

    @pl.when(i == 0)
    def _():
        wb_ref[...] = w_ref[...].astype(BF16)

    row0 = pl.multiple_of(i * tm, tm)
    acc = _dot_nt(wb_ref[...], h_ref[pl.ds(row0, tm), :])

    def store(val):
        for s in range(tm // CHUNK):
            out_ref[s] = val[:, s * CHUNK:(s + 1) * CHUNK]

    if not rope:
        store(acc)
        return

    @pl.when(row0 < n_prompt)
    def _():
        store(acc)

    @pl.when(row0 >= n_prompt)
    def _():
        blk = ((row0 - n_prompt) % dec_seq) // tm
        cos = cos_ref[blk]
        sin = sin_ref[blk]
        half = HEAD_DIM // 2
        parts = []
        for hh in range(acc.shape[0] // HEAD_DIM):
            r0 = hh * HEAD_DIM
            parts += list(_rope_pair(acc[r0:r0 + half, :], acc[r0 + half:r0 + HEAD_DIM, :], cos, sin))
        store(jnp.concatenate(parts, axis=0))


def _proj_t_call(h, w_t, layer, *, rope=None, n_prompt=0, dec_seq=1, tm=512):
    n, d = h.shape
    tn = w_t.shape[1]
    args = [h, w_t]
    specs = [
        pl.BlockSpec((n, d), lambda j, i: (0, 0)),
        pl.BlockSpec((None, tn, d), lambda j, i: (layer, 0, 0)),
    ]
    if rope is not None:
        args += list(rope)
        specs += [pl.BlockSpec(rope[0].shape, lambda j, i: (0, 0, 0))] * 2
    kern = functools.partial(_proj_t_kernel, tm=tm, rope=rope is not None, n_prompt=n_prompt, dec_seq=dec_seq)
    return pl.pallas_call(
        kern, grid=(1, n // tm), in_specs=specs,
        out_specs=pl.BlockSpec((tm // CHUNK, tn, CHUNK), lambda j, i: (i, 0, 0)),
        out_shape=jax.ShapeDtypeStruct((n // CHUNK, tn, CHUNK), F32),
        scratch_shapes=[pltpu.VMEM((tn, d), BF16)],
        compiler_params=_cparams(("parallel", "arbitrary"), 56), name="key_proj_t",
    )(*args)


def _gate_kernel(h_ref, wg_ref, bias_ref, out_ref, *, tm, n_heads):
    g = _dot_nt(wg_ref[...].astype(BF16), h_ref[...]) + bias_ref[...]
    nd = 2 * n_heads
    ig = g[0:nd, :]
    lf = _log_sigmoid(g[nd:2 * nd, :])
    r = _iota((CHUNK, CHUNK), 0)
    c = _iota((CHUNK, CHUNK), 1)
    upper = jnp.where(r <= c, 1.0, 0.0).astype(BF16)
    lower = jnp.where(r >= c, 1.0, 0.0).astype(BF16)
    is_fwd = _iota((nd, CHUNK), 0) < n_heads
    for s in range(tm // CHUNK):
        sl = slice(s * CHUNK, (s + 1) * CHUNK)
        lf_c = lf[:, sl]
        b = jnp.where(is_fwd, _dot3(lf_c, upper), _dot3(lf_c, lower))
        ig_c = ig[:, sl]
        for hh in range(n_heads):
            out_ref[hh, s, 0:1, :] = ig_c[hh:hh + 1, :]
            out_ref[hh, s, 1:2, :] = ig_c[n_heads + hh:n_heads + hh + 1, :]
            out_ref[hh, s, 2:3, :] = b[hh:hh + 1, :]
            out_ref[hh, s, 3:4, :] = b[n_heads + hh:n_heads + hh + 1, :]


def _gate_call(h, wg_t, bias_col, n_heads, tm=512):
    n, d = h.shape
    ng = 4 * n_heads
    return pl.pallas_call(
        functools.partial(_gate_kernel, tm=tm, n_heads=n_heads),
        grid=(n // tm,),
        in_specs=[
            pl.BlockSpec((tm, d), lambda i: (i, 0)),
            pl.BlockSpec((ng, d), lambda i: (0, 0)),
            pl.BlockSpec((ng, 1), lambda i: (0, 0)),
        ],
        out_specs=pl.BlockSpec((n_heads, tm // CHUNK, 4, CHUNK), lambda i: (0, i, 0, 0)),
        out_shape=jax.ShapeDtypeStruct((n_heads, n // CHUNK, 4, CHUNK), F32),
        compiler_params=_cparams(("parallel",), 32), name="mlstm_gates",
    )(h, wg_t, bias_col)


def _col_from_row(row, eye):
    return jnp.sum(jnp.where(eye, row, 0.0), axis=1, keepdims=True)


def _row_from_col(col, eye):
    return jnp.sum(jnp.where(eye, col, 0.0), axis=0, keepdims=True)


def _chunk_loop(nc, body, init):
    if nc <= 2:
        carry = init
        for ci in range(nc):
            carry = body(ci, carry)
        return carry
    return lax.fori_loop(0, nc, body, init)


def _mlstm_kernel(*refs, nc, has_init, write_state):
    it = iter(refs)
    q_ref, v_ref, kt_ref, mo_ref, gp_ref, nw_ref = (next(it) for _ in range(6))
    if has_init:
        c0_ref, n0_ref, m0_ref = next(it), next(it), next(it)
    hm_ref = next(it)
    if write_state:
        cout_ref, nout_ref, mout_ref = next(it), next(it), next(it)
    c_scr, n_scr, hf_scr = next(it), next(it), next(it)

    L = CHUNK
    row = _iota((L, L), 0)
    col = _iota((L, L), 1)
    eye = row == col
    k_scale = HEAD_DIM ** -0.5
    b_idx = pl.program_id(0)
    h_idx = pl.program_id(1)
    n_heads = pl.num_programs(1)

    for d in range(2):
        if has_init:
            c_scr[...] = c0_ref[d]
            n_scr[...] = n0_ref[d]
            m_init = jnp.full((1, 1), m0_ref[(b_idx * 2 + d) * n_heads + h_idx], F32)
        else:
            c_scr[...] = jnp.zeros_like(c_scr)
            n_scr[...] = jnp.zeros_like(n_scr)
            m_init = jnp.zeros((1, 1), F32)
        causal = (col <= row) if d == 0 else (col >= row)

        def body(ci, m, d=d, causal=causal):
            c = ci if d == 0 else nc - 1 - ci
            off = c * L if isinstance(c, int) else pl.multiple_of(c * L, L)
            q = q_ref[pl.ds(off, L), :]
            v = v_ref[pl.ds(off, L), :]
            kt = kt_ref[c]
            g = gp_ref[c]
            ig = g[d:d + 1, :]
            brow = g[2 + d:3 + d, :]
            bcol = _col_from_row(brow, eye)
            blast = brow[:, L - 1:L] if d == 0 else brow[:, 0:1]
            s_qk = _dot(q, kt.astype(BF16))
            dm = jnp.where(causal, bcol - brow + ig, -jnp.inf)
            inter = bcol + m
            m_row = jnp.maximum(jnp.max(dm, axis=1, keepdims=True), inter)
            w = jnp.exp(dm - m_row)
            w_inter = jnp.exp(inter - m_row)
            s = s_qk * (w * k_scale)
            n_row = n_scr[...]
            num = _dot(s.astype(BF16), v) + w_inter * _dot(q, c_scr[...].astype(BF16))
            den = (jnp.sum(s, axis=1, keepdims=True)
                   + w_inter * jnp.sum(q.astype(F32) * n_row, axis=1, keepdims=True))
            h_loc = num / jnp.maximum(jnp.abs(den), jnp.exp(-m_row))
            wj = blast - brow + ig
            m_new = jnp.maximum(blast + m, jnp.max(wj, axis=1, keepdims=True))
            decay = jnp.exp(blast + m - m_new)
            e = jnp.exp(wj - m_new) * k_scale
            kw = kt * e
            c_scr[...] = decay * c_scr[...] + _dot(kw.astype(BF16), v)
            n_add = jnp.sum(kw, axis=1, keepdims=True)
            n_add_row = jnp.concatenate(
                [_row_from_col(n_add[k * L:(k + 1) * L, :], eye) for k in range(HEAD_DIM // L)], axis=1)
            n_scr[...] = decay * n_row + n_add_row
            if d == 0:
                hf_scr[pl.ds(off, L), :] = h_loc
            else:
                h_tot = hf_scr[pl.ds(off, L), :] + h_loc
                y = h_tot * lax.rsqrt(jnp.mean(h_tot * h_tot, axis=1, keepdims=True) + EPS) * nw_ref[...]
                hm_ref[pl.ds(off, L), :] = (jax.nn.sigmoid(mo_ref[pl.ds(off, L), :]) * y).astype(hm_ref.dtype)
            return m_new

        m_fin = _chunk_loop(nc, body, m_init)
        if write_state:
            cout_ref[d] = c_scr[...]
            nout_ref[d] = n_scr[...]
            mout_ref[d:d + 1, :] = jnp.broadcast_to(m_fin, (1, LANES))


def _mlstm_call(qv, kt3, mo, gp, norm_w, init, *, n_seq, seq_len, tok_off, n_heads, layer):
    n = qv.shape[0]
    dk = HEAD_DIM
    nc = seq_len // CHUNK
    boff = tok_off // seq_len
    has_init = init is not None
    args = [qv, qv, kt3, mo, gp, norm_w.reshape(1, -1)]
    specs = [
        pl.BlockSpec((seq_len, dk), lambda b, h: (b + boff, h)),
        pl.BlockSpec((seq_len, dk), lambda b, h: (b + boff, n_heads + h)),
        pl.BlockSpec((nc, dk, CHUNK), lambda b, h: (b + boff, h, 0)),
        pl.BlockSpec((seq_len, dk), lambda b, h: (b + boff, h)),
        pl.BlockSpec((None, nc, 4, CHUNK), lambda b, h: (h, b + boff, 0, 0)),
        pl.BlockSpec((1, dk), lambda b, h: (0, h)),
    ]
    if has_init:
        c0, n0, m0 = init
        args += [c0, n0[:, layer][:, :, :, None, :], m0[:, layer].reshape(-1)]
        specs += [
            pl.BlockSpec((None, None, 2, None, dk, dk), lambda b, h: (b, layer, 0, h, 0, 0)),
            pl.BlockSpec((None, 2, None, 1, dk), lambda b, h: (b, 0, h, 0, 0)),
            pl.BlockSpec(memory_space=pltpu.SMEM),
        ]
    out_shape = [jax.ShapeDtypeStruct((n_seq * seq_len, n_heads * dk), BF16)]
    out_specs = [pl.BlockSpec((seq_len, dk), lambda b, h: (b, h))]
    if not has_init:
        out_shape += [
            jax.ShapeDtypeStruct((n_seq, 2, n_heads, dk, dk), F32),
            jax.ShapeDtypeStruct((n_seq, 2, n_heads, 1, dk), F32),
            jax.ShapeDtypeStruct((n_seq, n_heads, 2, LANES), F32),
        ]
        out_specs += [
            pl.BlockSpec((None, 2, None, dk, dk), lambda b, h: (b, 0, h, 0, 0)),
            pl.BlockSpec((None, 2, None, 1, dk), lambda b, h: (b, 0, h, 0, 0)),
            pl.BlockSpec((None, None, 2, LANES), lambda b, h: (b, h, 0, 0)),
        ]
    kern = functools.partial(_mlstm_kernel, nc=nc, has_init=has_init, write_state=not has_init)
    return pl.pallas_call(
        kern, grid=(n_seq, n_heads), in_specs=specs, out_specs=out_specs, out_shape=out_shape,
        scratch_shapes=[pltpu.VMEM((dk, dk), F32), pltpu.VMEM((1, dk), F32), pltpu.VMEM((seq_len, dk), F32)],
        compiler_params=_cparams(("parallel", "parallel"), 40), name="mlstm_mixer",
    )(*args)


def _ret_kernel(*refs, nc, has_init, write_state):
    it = iter(refs)
    dl_ref, q_ref, v_ref, kt_ref, rg_ref, nw_ref = (next(it) for _ in range(6))
    if has_init:
        s0_ref = next(it)
    hr_ref = next(it)
    if write_state:
        sout_ref = next(it)
    s_scr, of_scr = next(it), next(it)

    L = CHUNK
    rowi = _iota((L, L), 0)
    coli = _iota((L, L), 1)
    k_scale = HEAD_DIM ** -0.5
    h_idx = pl.program_id(1)
    n_heads = pl.num_programs(1)

    for d in range(2):
        lg = _log_sigmoid(jnp.full((1, 1), dl_ref[d * n_heads + h_idx], F32))
        rel = (rowi - coli if d == 0 else coli - rowi).astype(F32)
        dmat = jnp.where(rel >= 0, jnp.exp(jnp.maximum(rel, 0.0) * lg), 0.0) * k_scale
        pos_col = _iota((L, 1), 0).astype(F32)
        pos_row = _iota((1, L), 1).astype(F32)
        if d == 1:
            pos_col = (L - 1.0) - pos_col
            pos_row = (L - 1.0) - pos_row
        q_decay = jnp.exp((pos_col + 1.0) * lg)
        k_decay = jnp.exp((L - 1.0 - pos_row) * lg) * k_scale
        chunk_decay = jnp.exp(float(L) * lg)
        if has_init:
            s_scr[...] = s0_ref[d]
        else:
            s_scr[...] = jnp.zeros_like(s_scr)

        def body(ci, carry, d=d, dmat=dmat, q_decay=q_decay, k_decay=k_decay, chunk_decay=chunk_decay):
            c = ci if d == 0 else nc - 1 - ci
            off = c * L if isinstance(c, int) else pl.multiple_of(c * L, L)
            q = q_ref[pl.ds(off, L), :]
            v = v_ref[pl.ds(off, L), :]
            kt = kt_ref[c]
            sc = _dot(q, kt.astype(BF16)) * dmat
            o = _dot(sc.astype(BF16), v) + q_decay * _dot(q, s_scr[...].astype(BF16))
            s_scr[...] = chunk_decay * s_scr[...] + _dot((kt * k_decay).astype(BF16), v)
            if d == 0:
                of_scr[pl.ds(off, L), :] = o
            else:
                o_tot = of_scr[pl.ds(off, L), :] + o
                y = o_tot * lax.rsqrt(jnp.mean(o_tot * o_tot, axis=1, keepdims=True) + EPS) * nw_ref[...]
                rg = rg_ref[pl.ds(off, L), :]
                hr_ref[pl.ds(off, L), :] = (rg * jax.nn.sigmoid(rg) * y).astype(hr_ref.dtype)
            return carry

        _chunk_loop(nc, body, 0)
        if write_state:
            sout_ref[d] = s_scr[...]


def _ret_call(rq, rv, kt3, gates, decay_logit, norm_w, init, *, n_seq, seq_len, tok_off, n_heads, layer):
    n = rq.shape[0]
    dk = HEAD_DIM
    nc = seq_len // CHUNK
    boff = tok_off // seq_len
    has_init = init is not None
    args = [decay_logit.reshape(-1), rq, rv, kt3, gates, norm_w.reshape(1, -1)]
    specs = [
        pl.BlockSpec(memory_space=pltpu.SMEM),
        pl.BlockSpec((seq_len, dk), lambda b, h: (b + boff, h)),
        pl.BlockSpec((seq_len, dk), lambda b, h: (b + boff, h)),
        pl.BlockSpec((nc, dk, CHUNK), lambda b, h: (b + boff, h, 0)),
        pl.BlockSpec((seq_len, dk), lambda b, h: (b + boff, h)),
        pl.BlockSpec((1, dk), lambda b, h: (0, h)),
    ]
    if has_init:
        args.append(init)
        specs.append(pl.BlockSpec((None, None, 2, None, dk, dk), lambda b, h: (b, layer, 0, h, 0, 0)))
    out_shape = [jax.ShapeDtypeStruct((n_seq * seq_len, n_heads * dk), BF16)]
    out_specs = [pl.BlockSpec((seq_len, dk), lambda b, h: (b, h))]
    if not has_init:
        out_shape.append(jax.ShapeDtypeStruct((n_seq, 2, n_heads, dk, dk), F32))
        out_specs.append(pl.BlockSpec((None, 2, None, dk, dk), lambda b, h: (b, 0, h, 0, 0)))
    kern = functools.partial(_ret_kernel, nc=nc, has_init=has_init, write_state=not has_init)
    return pl.pallas_call(
        kern, grid=(n_seq, n_heads), in_specs=specs, out_specs=out_specs, out_shape=out_shape,
        scratch_shapes=[pltpu.VMEM((dk, dk), F32), pltpu.VMEM((seq_len, dk), F32)],
        compiler_params=_cparams(("parallel", "parallel"), 40), name="retention_mixer",
    )(*args)


def _merge_kernel(hm_ref, hr_ref, ga_ref, gb_ref, x_ref, mod_ref, nw_ref, wa_ref, wb_ref, wo_ref, wr_ref,
                  x1_ref, h2_ref, aff_ref, wab_scr, wbb_scr, wob_scr):
    @pl.when(pl.program_id(0) == 0)
    def _():
        wab_scr[...] = wa_ref[...].astype(BF16)
        wbb_scr[...] = wb_ref[...].astype(BF16)
        wob_scr[...] = wo_ref[...].astype(BF16)

    ya = _dot(hm_ref[...], wab_scr[...])
    yb = _dot(hr_ref[...], wbb_scr[...])
    merged = jax.nn.sigmoid(ga_ref[...]) * ya + jax.nn.sigmoid(gb_ref[...]) * yb
    y = _dot(merged.astype(BF16), wob_scr[...])
    x1 = x_ref[...] + mod_ref[2:3, :] * y
    x1_ref[...] = x1
    h2 = x1 * lax.rsqrt(jnp.mean(x1 * x1, axis=-1, keepdims=True) + EPS) * nw_ref[...]
    h2 = h2 * (1.0 + mod_ref[4:5, :]) + mod_ref[3:4, :]
    h2_ref[...] = h2
    logits = _dot_nt(wr_ref[...].astype(BF16), h2.astype(BF16))
    p = jnp.exp(logits - jnp.max(logits, axis=0, keepdims=True))
    aff_ref[...] = p / jnp.sum(p, axis=0, keepdims=True)


def _merge_call(hm, hr, gates, x, mod, norm_w, w_a, w_b, w_out, router_wt, *, n_prompt, dec_seq, tm=512):
    n, d = x.shape
    ne = router_wt.shape[0]
    grp = functools.partial(_group_of_tile, tm=tm, n_prompt=n_prompt, dec_seq=dec_seq)
    tile = pl.BlockSpec((tm, d), lambda i: (i, 0))
    full = pl.BlockSpec((d, d), lambda i: (0, 0))
    return pl.pallas_call(
        _merge_kernel, grid=(n // tm,),
        in_specs=[
            tile, tile,
            pl.BlockSpec((tm, d), lambda i: (i, 1)),
            pl.BlockSpec((tm, d), lambda i: (i, 2)),
            tile,
            pl.BlockSpec((None, 6, d), lambda i: (grp(i), 0, 0)),
            pl.BlockSpec((1, d), lambda i: (0, 0)),
            full, full, full,
            pl.BlockSpec((ne, d), lambda i: (0, 0)),
        ],
        out_specs=[tile, tile, pl.BlockSpec((ne, tm), lambda i: (0, i))],
        out_shape=[jax.ShapeDtypeStruct((n, d), F32), jax.ShapeDtypeStruct((n, d), F32),
                   jax.ShapeDtypeStruct((ne, n), F32)],
        scratch_shapes=[pltpu.VMEM((d, d), BF16)] * 3,
        compiler_params=_cparams(("arbitrary",), 56), name="merge_out_router",
    )(hm, hr, gates, gates, x, mod, norm_w.reshape(1, d), w_a, w_b, w_out, router_wt)


def _route_kernel(a_ref, at_ref, idx_ref, gate_ref, thr_scr, *, cap, n_tok):
    n_sets = a_ref.shape[0]
    nb = n_tok // LANES
    bits_all = pltpu.bitcast(a_ref[...], jnp.int32)

    def count_ge(cand):
        m = jnp.where(bits_all >= cand, 1.0, 0.0)
        return jnp.sum(jnp.sum(m, axis=2, keepdims=True), axis=1, keepdims=True)

    def bit_step(k, thr):
        cand = thr | lax.shift_left(jnp.int32(1), 30 - k)
        return jnp.where(count_ge(cand) >= cap, cand, thr)

    thr_scr[...] = lax.fori_loop(0, 31, bit_step, jnp.zeros((n_sets, 1, 1), jnp.int32))

    r128 = _iota((LANES, LANES), 0)
    c128 = _iota((LANES, LANES), 1)
    upper = jnp.where(r128 <= c128, 1.0, 0.0).astype(BF16)
    lower_t = jnp.where(r128 >= c128, 1.0, 0.0).astype(BF16)
    rb = _iota((nb, nb), 0)
    cb = _iota((nb, nb), 1)
    blk_before_rows = jnp.where(cb < rb, 1.0, 0.0).astype(BF16)
    blk_before_cols = jnp.where(rb < cb, 1.0, 0.0).astype(BF16)

    def incl_counts(mask):
        within = _dot(mask.astype(BF16), upper)
        before = _dot(blk_before_rows, within.astype(BF16))[:, LANES - 1:LANES]
        return within + before

    def incl_counts_t(mask_t):
        within = _dot(lower_t, mask_t.astype(BF16))
        before = _dot(within.astype(BF16), blk_before_cols)[LANES - 1:LANES, :]
        return within + before

    slot = _iota((1, cap), 1).astype(F32)
    blk_col = _iota((nb, 1), 0).astype(F32)
    sub_col = _iota((LANES, 1), 0).astype(F32)

    def per_set(s, carry):
        thr = thr_scr[s]
        a = a_ref[s]
        a_t = at_ref[s]
        bits = pltpu.bitcast(a, jnp.int32)
        bits_t = pltpu.bitcast(a_t, jnp.int32)
        gt = jnp.where(bits > thr, 1.0, 0.0)
        eq = jnp.where(bits == thr, 1.0, 0.0)
        gt_t = jnp.where(bits_t > thr, 1.0, 0.0)
        eq_t = jnp.where(bits_t == thr, 1.0, 0.0)
        n_gt = jnp.sum(jnp.sum(gt, axis=1, keepdims=True), axis=0, keepdims=True)
        need = cap - n_gt
        sel = gt + eq * jnp.where(incl_counts(eq) - eq < need, 1.0, 0.0)
        sel_t = gt_t + eq_t * jnp.where(incl_counts_t(eq_t) - eq_t < need, 1.0, 0.0)
        cnt = incl_counts(sel)
        cnt_t = incl_counts_t(sel_t)
        blk_end = cnt[:, LANES - 1:LANES]
        blk_of_slot = jnp.sum(jnp.where(blk_end <= slot, 1.0, 0.0), axis=0, keepdims=True)
        onehot_blk = jnp.where(blk_col == blk_of_slot, 1.0, 0.0).astype(BF16)
        cnt_rows = _dot3_left_t(cnt_t, onehot_blk)
        sub_of_slot = jnp.sum(jnp.where(cnt_rows <= slot, 1.0, 0.0), axis=0, keepdims=True)
        a_rows = _dot3_left_t(a_t, onehot_blk)
        gate = jnp.sum(jnp.where(sub_col == sub_of_slot, a_rows, 0.0), axis=0, keepdims=True)
        idx_ref[s] = (blk_of_slot * LANES + sub_of_slot).astype(jnp.int32)
        gate_ref[s] = gate
        return carry

    lax.fori_loop(0, n_sets, per_set, 0)


def _dot3_left_t(x, onehot):
    hi, mid, lo = _split3(x)
    return _dot(hi, onehot) + _dot(mid, onehot) + _dot(lo, onehot)


def _route_call(aff_t, n_pass, cap):
    ne, n = aff_t.shape
    n_tok = n // n_pass
    nb = n_tok // LANES
    a4 = aff_t.reshape(ne, n_pass, nb, LANES).transpose(1, 0, 2, 3).reshape(n_pass * ne, nb, LANES)
    a4_t = a4.transpose(0, 2, 1)
    n_sets = n_pass * ne
    return pl.pallas_call(
        functools.partial(_route_kernel, cap=cap, n_tok=n_tok),
        out_shape=[jax.ShapeDtypeStruct((n_sets, 1, cap), jnp.int32),
                   jax.ShapeDtypeStruct((n_sets, 1, cap), F32)],
        scratch_shapes=[pltpu.VMEM((n_sets, 1, 1), jnp.int32)],
        compiler_params=pltpu.CompilerParams(vmem_limit_bytes=32 * MIB), name="expert_choice_route",
    )(a4, a4_t)


def _moe_kernel(idx_ref, gate_ref, h_hbm, w1_ref, w3_ref, w2_ref, out_hbm,
                xe_scr, xb_scr, ye_scr, acc_scr, gsem, osem, *, rows):
    e = pl.program_id(0)
    f = pl.program_id(1)
    n_e = pl.num_programs(0)
    n_f = pl.num_programs(1)

    def start_gather(ee):
        def issue(s, carry):
            tok = idx_ref[ee * rows + s]
            pltpu.make_async_copy(h_hbm.at[pl.ds(tok, 1), :], xe_scr.at[pl.ds(s, 1), :], gsem).start()
            return carry
        lax.fori_loop(0, rows, issue, 0)

    @pl.when((e == 0) & (f == 0))
    def _():
        acc_scr[...] = jnp.zeros_like(acc_scr)
        start_gather(0)

    @pl.when(f == 0)
    def _():
        pltpu.make_async_copy(h_hbm.at[pl.ds(0, rows), :], xe_scr, gsem).wait()
        xb_scr[...] = xe_scr[...].astype(BF16)

        @pl.when(e + 1 < n_e)
        def _():
            start_gather(e + 1)

    xb = xb_scr[...]
    h1 = _dot(xb, w1_ref[...].astype(BF16))
    h3 = _dot(xb, w3_ref[...].astype(BF16))
    he = (h1 * jax.nn.sigmoid(h1) * h3).astype(BF16)
    part = _dot(he, w2_ref[...].astype(BF16))

    @pl.when(f == 0)
    def _():
        ye_scr[...] = part

    @pl.when(f > 0)
    def _():
        ye_scr[...] += part

    @pl.when(f == n_f - 1)
    def _():
        def scatter(s, carry):
            tok = idx_ref[e * rows + s]
            g = gate_ref[e * rows + s]
            acc_scr[pl.ds(tok, 1), :] += ye_scr[pl.ds(s, 1), :] * g
            return carry
        lax.fori_loop(0, rows, scatter, 0)

        @pl.when(e == n_e - 1)
        def _():
            cp = pltpu.make_async_copy(acc_scr, out_hbm, osem)
            cp.start()
            cp.wait()


def _moe_call(idx, gate, h2, w1, w3, w2, layer, tf=256):
    n, d = h2.shape
    ne = w1.shape[1]
    dff = w1.shape[3]
    rows = idx.shape[0] // ne
    grid_spec = pltpu.PrefetchScalarGridSpec(
        num_scalar_prefetch=1,
        grid=(ne, dff // tf),
        in_specs=[
            pl.BlockSpec(memory_space=pltpu.SMEM),
            pl.BlockSpec(memory_space=pl.ANY),
            pl.BlockSpec((None, None, d, tf), lambda e, f, idx: (layer, e, 0, f)),
            pl.BlockSpec((None, None, d, tf), lambda e, f, idx: (layer, e, 0, f)),
            pl.BlockSpec((None, None, tf, d), lambda e, f, idx: (layer, e, f, 0)),
        ],
        out_specs=pl.BlockSpec(memory_space=pl.ANY),
        scratch_shapes=[
            pltpu.VMEM((rows, d), F32), pltpu.VMEM((rows, d), BF16), pltpu.VMEM((rows, d), F32),
            pltpu.VMEM((n, d), F32), pltpu.SemaphoreType.DMA, pltpu.SemaphoreType.DMA,
        ],
    )
    return pl.pallas_call(
        functools.partial(_moe_kernel, rows=rows), grid_spec=grid_spec,
        out_shape=jax.ShapeDtypeStruct((n, d), F32),
        compiler_params=_cparams(("arbitrary", "arbitrary"), 60), name="expert_ffn",
    )(idx, gate, h2, w1, w3, w2)


def _rope_tables(t):
    rows = t // GRID_W
    row = jnp.repeat(jnp.arange(rows, dtype=F32), GRID_W)
    colp = jnp.tile(jnp.arange(GRID_W, dtype=F32), rows)
    n_freq = HEAD_DIM // 4
    inv = ROPE_BASE ** (-jnp.arange(n_freq, dtype=F32) / n_freq)
    ang = jnp.concatenate([row[:, None] * inv, colp[:, None] * inv], axis=-1)
    return jnp.cos(ang), jnp.sin(ang)


def kernel(x_prompt, x_sample, c, state_mlstm_C, state_mlstm_n, state_mlstm_m, state_ret_S, c_ctx, w_mod, b_mod,
           norm1_w, norm2_w, w_in, mlstm_if_b, mlstm_norm_w, ret_decay_logit, ret_norm_w, w_branch_a, w_branch_b,
           w_out, router_w, ffn_w1, ffn_w3, ffn_w2, final_norm_w):
    bp, seq, d = x_prompt.shape
    db, dec_seq, _ = x_sample.shape
    depth = w_mod.shape[0]
    hm = mlstm_if_b.shape[-1]
    hr = ret_decay_logit.shape[-1]
    ne = router_w.shape[-1]
    n_prompt = bp * seq
    n_sample = db * dec_seq
    assert n_prompt == n_sample and hm * HEAD_DIM == d and hr * HEAD_DIM == d
    cap = EC_FACTOR * n_prompt // ne
    tm = 512
    tn = d
    geo = dict(n_prompt=n_prompt, dec_seq=dec_seq)

    x = jnp.concatenate([x_prompt.reshape(n_prompt, d), x_sample.reshape(n_sample, d)], axis=0)
    cond8 = jnp.concatenate([c_ctx[None, :], c, jnp.zeros((8 - 1 - db, d), F32)], axis=0)
    mod = _mod_call(cond8, w_mod, b_mod)[:, :1 + db].reshape(depth, 1 + db, 6, d)

    n_m = 4 * d
    g0 = n_m + 4 * hm
    w_ret = w_in[:, :, g0:]
    wk_m_t = jnp.swapaxes(w_in[:, :, d:2 * d], 1, 2)
    wk_r_t = jnp.swapaxes(w_ret[:, :, d:2 * d], 1, 2)
    wg_t = jnp.swapaxes(w_in[:, :, n_m:g0], 1, 2)
    gate_bias = jnp.transpose(mlstm_if_b, (0, 2, 1, 3)).reshape(depth, 4 * hm, 1)
    router_wt = jnp.swapaxes(router_w, 1, 2)

    cos, sin = _rope_tables(dec_seq)
    cos_t = cos.T.reshape(HEAD_DIM // 2, dec_seq // tm, tm).transpose(1, 0, 2)
    sin_t = sin.T.reshape(HEAD_DIM // 2, dec_seq // tm, tm).transpose(1, 0, 2)

    _, h = _resid_norm_call(x, None, None, mod[0], norm1_w[0], gate_row=0, mod_rows=(0, 1), h_dtype=BF16,
                            write_x=False, **geo)
    new_c, new_n, new_m, new_s = [], [], [], []
    y_all = None
    for l in range(depth):
        qv_m = _proj_call(h, w_in, l, [0, 2], BF16, tm=tm, tn=tn)
        mo = _proj_call(h, w_in, l, [3], F32, tm=tm, tn=tn)
        kt_m = _proj_t_call(h, wk_m_t, l, tm=tm)
        rq = _proj_call(h, w_ret, l, [0], BF16, rope=(cos, sin), tm=tm, tn=tn, **geo)
        rv = _proj_call(h, w_ret, l, [2], BF16, tm=tm, tn=tn)
        kt_r = _proj_t_call(h, wk_r_t, l, rope=(cos_t, sin_t), tm=tm, **geo)
        gates = _proj_call(h, w_ret, l, [3, 4, 5], F32, tm=tm, tn=tn)
        gp = _gate_call(h, wg_t[l], gate_bias[l], hm, tm=tm)

        hm_p, c_fin, n_fin, m_fin = _mlstm_call(qv_m, kt_m, mo, gp, mlstm_norm_w[l], None, n_seq=bp, seq_len=seq,
                                                tok_off=0, n_heads=hm, layer=l)
        (hm_s,) = _mlstm_call(qv_m, kt_m, mo, gp, mlstm_norm_w[l],
                              (state_mlstm_C, state_mlstm_n, state_mlstm_m), n_seq=db, seq_len=dec_seq,
                              tok_off=n_prompt, n_heads=hm, layer=l)
        hr_p, s_fin = _ret_call(rq, rv, kt_r, gates, ret_decay_logit[l], ret_norm_w[l], None, n_seq=bp,
                                seq_len=seq, tok_off=0, n_heads=hr, layer=l)
        (hr_s,) = _ret_call(rq, rv, kt_r, gates, ret_decay_logit[l], ret_norm_w[l], state_ret_S, n_seq=db,
                            seq_len=dec_seq, tok_off=n_prompt, n_heads=hr, layer=l)
        hm_all = jnp.concatenate([hm_p, hm_s], axis=0)
        hr_all = jnp.concatenate([hr_p, hr_s], axis=0)
        new_c.append(c_fin)
        new_n.append(n_fin[:, :, :, 0, :])
        new_m.append(jnp.swapaxes(m_fin[..., 0], 1, 2))
        new_s.append(s_fin)

        x1, h2, aff_t = _merge_call(hm_all, hr_all, gates, x, mod[l], norm2_w[l], w_branch_a[l], w_branch_b[l],
                                    w_out[l], router_wt[l], tm=tm, **geo)
        idx, gate = _route_call(aff_t, 2, cap)
        idx = idx.reshape(2, ne, cap) + (jnp.arange(2, dtype=jnp.int32) * n_prompt)[:, None, None]
        idx = jnp.swapaxes(idx, 0, 1).reshape(-1)
        gate = jnp.swapaxes(gate.reshape(2, ne, cap), 0, 1).reshape(-1)
        moe = _moe_call(idx, gate, h2, ffn_w1, ffn_w3, ffn_w2, l)
        if l + 1 < depth:
            x, h = _resid_norm_call(x1, moe, mod[l], mod[l + 1], norm1_w[l + 1], gate_row=5, mod_rows=(0, 1),
                                    h_dtype=BF16, write_x=True, **geo)
        else:
            _, y_all = _resid_norm_call(x1, moe, mod[l], None, final_norm_w, gate_row=5, mod_rows=None,
                                        h_dtype=F32, write_x=False, **geo)

    y_prompt = y_all[:n_prompt].reshape(bp, seq, d)
    y_sample = y_all[n_prompt:].reshape(db, dec_seq, d)
    return (y_prompt, y_sample, jnp.stack(new_c, axis=1), jnp.stack(new_n, axis=1), jnp.stack(new_m, axis=1),
            jnp.stack(new_s, axis=1))
```

```python
import functools

import numpy as np
import jax
import jax.numpy as jnp
from jax import lax
from jax.experimental import pallas as pl
from jax.experimental.pallas import tpu as pltpu

F32 = jnp.float32
BF16 = jnp.bfloat16

GRID_W = 64
CHUNK = 128
LANES = 128
HEAD_DIM = 256
N_EXPERTS = 16
EC_FACTOR = 2
ROPE_BASE = 10000.0
EPS = 1e-6
V7X_VMEM_BYTES = 64 * 1024 * 1024
MIB = 1024 * 1024


def _cparams(semantics, vmem_mib):
    assert vmem_mib * MIB < V7X_VMEM_BYTES
    return pltpu.CompilerParams(dimension_semantics=semantics, vmem_limit_bytes=vmem_mib * MIB)


def _dot(a, b):
    return jnp.dot(a, b, preferred_element_type=F32)


def _dot_nt(a, b):
    return lax.dot_general(a, b, (((1,), (1,)), ((), ())), preferred_element_type=F32)


def _log_sigmoid(x):
    return -(jnp.maximum(-x, 0.0) + jnp.log1p(jnp.exp(-jnp.abs(x))))


def _split3(x):
    hi = x.astype(BF16)
    r1 = x - hi.astype(F32)
    mid = r1.astype(BF16)
    lo = (r1 - mid.astype(F32)).astype(BF16)
    return hi, mid, lo


def _dot3(x, m):
    hi, mid, lo = _split3(x)
    return _dot(hi, m) + _dot(mid, m) + _dot(lo, m)


def _iota(shape, dim):
    return lax.broadcasted_iota(jnp.int32, shape, dim)


def _mod_kernel(cond_ref, w_ref, b_ref, out_ref):
    c = cond_ref[...]
    s = (c * jax.nn.sigmoid(c)).astype(BF16)
    out_ref[...] = _dot(s, w_ref[...].astype(BF16)) + b_ref[...]


def _mod_call(cond8, w_mod, b_mod):
    depth, d, w6 = w_mod.shape
    tn = 1536
    return pl.pallas_call(
        _mod_kernel,
        grid=(depth, w6 // tn),
        in_specs=[
            pl.BlockSpec((8, d), lambda l, j: (0, 0)),
            pl.BlockSpec((None, d, tn), lambda l, j: (l, 0, j)),
            pl.BlockSpec((None, 1, tn), lambda l, j: (l, 0, j)),
        ],
        out_specs=pl.BlockSpec((None, 8, tn), lambda l, j: (l, 0, j)),
        out_shape=jax.ShapeDtypeStruct((depth, 8, w6), F32),
        compiler_params=_cparams(("parallel", "parallel"), 32),
        name="adaln_mod",
    )(cond8, w_mod, b_mod.reshape(depth, 1, w6))


def _group_of_tile(i, tm, n_prompt, dec_seq):
    return jnp.maximum(i * tm - n_prompt + dec_seq, 0) // dec_seq


def _resid_norm_kernel(*refs, has_delta, has_mod, gate_row, mod_rows, write_x):
    it = iter(refs)
    x_ref = next(it)
    delta_ref = next(it) if has_delta else None
    mod_ref = next(it) if (has_delta or has_mod) else None
    nw_ref = next(it)
    xo_ref = next(it) if write_x else None
    h_ref = next(it)
    x = x_ref[...]
    if has_delta:
        x = x + mod_ref[gate_row:gate_row + 1, :] * delta_ref[...]
    if write_x:
        xo_ref[...] = x
    y = x * lax.rsqrt(jnp.mean(x * x, axis=-1, keepdims=True) + EPS) * nw_ref[...]
    if has_mod:
        sh_row, sc_row = mod_rows
        y = y * (1.0 + mod_ref[sc_row:sc_row + 1, :]) + mod_ref[sh_row:sh_row + 1, :]
    h_ref[...] = y.astype(h_ref.dtype)


def _resid_norm_call(x, delta, mod_gate, mod_norm, norm_w, *, gate_row, mod_rows, h_dtype, write_x,
                     n_prompt, dec_seq, tm=512, row_off=0, n_rows=None):
    n, d = x.shape
    n_rows = n if n_rows is None else n_rows
    toff = row_off // tm
    has_delta = delta is not None
    has_mod = mod_norm is not None
    grp = functools.partial(_group_of_tile, tm=tm, n_prompt=n_prompt, dec_seq=dec_seq)
    tile_in = pl.BlockSpec((tm, d), lambda i: (i + toff, 0))
    tile_out = pl.BlockSpec((tm, d), lambda i: (i, 0))
    args, specs = [x], [tile_in]
    if has_delta:
        args.append(delta)
        specs.append(tile_in)
    if has_delta or has_mod:
        mg = mod_gate if has_delta else mod_norm
        mn = mod_norm if has_mod else mod_gate
        args.append(jnp.concatenate([mg, mn], axis=1))
        specs.append(pl.BlockSpec((None, 12, d), lambda i: (grp(i + toff), 0, 0)))
    args.append(norm_w.reshape(1, d))
    specs.append(pl.BlockSpec((1, d), lambda i: (0, 0)))
    out_shape, out_specs = [], []
    if write_x:
        out_shape.append(jax.ShapeDtypeStruct((n_rows, d), F32))
        out_specs.append(tile_out)
    out_shape.append(jax.ShapeDtypeStruct((n_rows, d), h_dtype))
    out_specs.append(tile_out)
    kern = functools.partial(
        _resid_norm_kernel, has_delta=has_delta, has_mod=has_mod, gate_row=gate_row,
        mod_rows=None if mod_rows is None else (6 + mod_rows[0], 6 + mod_rows[1]), write_x=write_x)
    outs = pl.pallas_call(
        kern, grid=(n_rows // tm,), in_specs=specs, out_specs=out_specs, out_shape=out_shape,
        compiler_params=_cparams(("parallel",), 32), name="resid_norm",
    )(*args)
    return outs if write_x else (None, outs[0])


def _rope_pair(x1, x2, cos, sin):
    return x1 * cos - x2 * sin, x1 * sin + x2 * cos


def _proj_kernel(h_ref, w_ref, cos_ref, sin_ref, out_ref, *, tm, rope_tile, n_prompt, dec_seq):
    j = pl.program_id(0)
    i = pl.program_id(1)
    row0 = pl.multiple_of(i * tm, tm)
    acc = _dot(h_ref[pl.ds(row0, tm), :], w_ref[...])
    use_rope = (j == rope_tile) & (row0 >= n_prompt)

    @pl.when(jnp.logical_not(use_rope))
    def _():
        out_ref[...] = acc.astype(out_ref.dtype)

    @pl.when(use_rope)
    def _():
        pos0 = pl.multiple_of((row0 - n_prompt) % dec_seq, tm)
        cos = cos_ref[pl.ds(pos0, tm), :]
        sin = sin_ref[pl.ds(pos0, tm), :]
        half = HEAD_DIM // 2
        for hh in range(acc.shape[1] // HEAD_DIM):
            c0 = hh * HEAD_DIM
            y1, y2 = _rope_pair(acc[:, c0:c0 + half], acc[:, c0 + half:c0 + HEAD_DIM], cos, sin)
            out_ref[:, c0:c0 + half] = y1.astype(out_ref.dtype)
            out_ref[:, c0 + half:c0 + HEAD_DIM] = y2.astype(out_ref.dtype)


def _proj_call(h, w, layer, rope, rope_tile, *, n_prompt, dec_seq, tm=512, tn=1024):
    n, d = h.shape
    width = w.shape[2]
    kern = functools.partial(_proj_kernel, tm=tm, rope_tile=rope_tile, n_prompt=n_prompt, dec_seq=dec_seq)
    return pl.pallas_call(
        kern, grid=(width // tn, n // tm),
        in_specs=[
            pl.BlockSpec((n, d), lambda j, i: (0, 0)),
            pl.BlockSpec((None, d, tn), lambda j, i: (layer, 0, j)),
            pl.BlockSpec(rope[0].shape, lambda j, i: (0, 0)),
            pl.BlockSpec(rope[1].shape, lambda j, i: (0, 0)),
        ],
        out_specs=pl.BlockSpec((tm, tn), lambda j, i: (i, j)),
        out_shape=jax.ShapeDtypeStruct((n, width), BF16),
        compiler_params=_cparams(("parallel", "arbitrary"), 56), name="in_proj",
    )(h, w, *rope)


def _proj_t_kernel(h_ref, w_ref, cos_ref, sin_ref, out_ref, *, tm, rope_tile, n_prompt, dec_seq):
    j = pl.program_id(0)
    i = pl.program_id(1)
    row0 = pl.multiple_of(i * tm, tm)
    acc = _dot_nt(w_ref[...], h_ref[pl.ds(row0, tm), :])
    use_rope = (j == rope_tile) & (row0 >= n_prompt)

    def store(val):
        for s in range(tm // CHUNK):
            out_ref[s] = val[:, s * CHUNK:(s + 1) * CHUNK].astype(out_ref.dtype)

    @pl.when(jnp.logical_not(use_rope))
    def _():
        store(acc)

    @pl.when(use_rope)
    def _():
        blk = ((row0 - n_prompt) % dec_seq) // tm
        cos = cos_ref[blk]
        sin = sin_ref[blk]
        half = HEAD_DIM // 2
        parts = []
        for hh in range(acc.shape[0] // HEAD_DIM):
            r0 = hh * HEAD_DIM
            parts += list(_rope_pair(acc[r0:r0 + half, :], acc[r0 + half:r0 + HEAD_DIM, :], cos, sin))
        store(jnp.concatenate(parts, axis=0))


def _proj_t_call(h, w_t, layer, rope_t, rope_tile, *, n_prompt, dec_seq, tm=512, tn=1024):
    n, d = h.shape
    width = w_t.shape[1]
    kern = functools.partial(_proj_t_kernel, tm=tm, rope_tile=rope_tile, n_prompt=n_prompt, dec_seq=dec_seq)
    return pl.pallas_call(
        kern, grid=(width // tn, n // tm),
        in_specs=[
            pl.BlockSpec((n, d), lambda j, i: (0, 0)),
            pl.BlockSpec((None, tn, d), lambda j, i: (layer, j, 0)),
            pl.BlockSpec(rope_t[0].shape, lambda j, i: (0, 0, 0)),
            pl.BlockSpec(rope_t[1].shape, lambda j, i: (0, 0, 0)),
        ],
        out_specs=pl.BlockSpec((tm // CHUNK, tn, CHUNK), lambda j, i: (i, j, 0)),
        out_shape=jax.ShapeDtypeStruct((n // CHUNK, width, CHUNK), BF16),
        compiler_params=_cparams(("parallel", "arbitrary"), 56), name="in_proj_t",
    )(h, w_t, *rope_t)


def _gate_kernel(h_ref, wg_ref, bias_ref, out_ref, *, tm, n_heads):
    g = _dot_nt(wg_ref[...].astype(BF16), h_ref[...]) + bias_ref[...]
    nd = 2 * n_heads
    ig = g[0:nd, :]
    lf = _log_sigmoid(g[nd:2 * nd, :])
    r = _iota((CHUNK, CHUNK), 0)
    c = _iota((CHUNK, CHUNK), 1)
    upper = jnp.where(r <= c, 1.0, 0.0).astype(BF16)
    lower = jnp.where(r >= c, 1.0, 0.0).astype(BF16)
    is_fwd = _iota((nd, CHUNK), 0) < n_heads
    for s in range(tm // CHUNK):
        sl = slice(s * CHUNK, (s + 1) * CHUNK)
        lf_c = lf[:, sl]
        b = jnp.where(is_fwd, _dot3(lf_c, upper), _dot3(lf_c, lower))
        ig_c = ig[:, sl]
        for hh in range(n_heads):
            out_ref[hh, s, 0:1, :] = ig_c[hh:hh + 1, :]
            out_ref[hh, s, 1:2, :] = ig_c[n_heads + hh:n_heads + hh + 1, :]
            out_ref[hh, s, 2:3, :] = b[hh:hh + 1, :]
            out_ref[hh, s, 3:4, :] = b[n_heads + hh:n_heads + hh + 1, :]


def _gate_call(h, wg_t, bias_col, n_heads, tm=512):
    n, d = h.shape
    ng = 4 * n_heads
    return pl.pallas_call(
        functools.partial(_gate_kernel, tm=tm, n_heads=n_heads),
        grid=(n // tm,),
        in_specs=[
            pl.BlockSpec((tm, d), lambda i: (i, 0)),
            pl.BlockSpec((ng, d), lambda i: (0, 0)),
            pl.BlockSpec((ng, 1), lambda i: (0, 0)),
        ],
        out_specs=pl.BlockSpec((n_heads, tm // CHUNK, 4, CHUNK), lambda i: (0, i, 0, 0)),
        out_shape=jax.ShapeDtypeStruct((n_heads, n // CHUNK, 4, CHUNK), F32),
        compiler_params=_cparams(("parallel",), 32), name="mlstm_gates",
    )(h, wg_t, bias_col)


def _scan_loop(nc, body, init):
    if nc <= 2:
        carry = init
        for ci in range(nc):
            carry = body(ci, carry)
        return carry
    return lax.fori_loop(0, nc, body, init)


def _chunk_off(c):
    return c * CHUNK if isinstance(c, int) else pl.multiple_of(c * CHUNK, CHUNK)


def _mlstm_kernel(*refs, nc, has_init, first_layer, write_state):
    it = iter(refs)
    qt_ref, vt_ref, ot_ref, k_ref, gp_ref, nw_ref = (next(it) for _ in range(6))
    if has_init:
        c0_ref, n0_ref, m0_ref = next(it), next(it), next(it)
    if write_state and not first_layer:
        next(it), next(it), next(it)
    hm_ref = next(it)
    if write_state:
        cout_ref, nout_ref, mout_ref = next(it), next(it), next(it)
    ct_scr, n_scr, hf_scr, hb_scr = next(it), next(it), next(it), next(it)

    L = CHUNK
    row = _iota((L, L), 0)
    col = _iota((L, L), 1)
    k_scale = HEAD_DIM ** -0.5
    b_idx = pl.program_id(0)
    h_idx = pl.program_id(1)
    n_heads = pl.num_programs(1)

    m_init = []
    for d in range(2):
        if has_init:
            ct_scr[d] = c0_ref[d].T
            n_scr[d] = jnp.broadcast_to(n0_ref[d], n_scr.shape[1:])
            m_init.append(jnp.full((1, 1), m0_ref[(b_idx * 2 + d) * n_heads + h_idx], F32))
        else:
            ct_scr[d] = jnp.zeros(ct_scr.shape[1:], F32)
            n_scr[d] = jnp.zeros(n_scr.shape[1:], F32)
            m_init.append(jnp.zeros((1, 1), F32))

    def step(d, c, m):
        off = _chunk_off(c)
        qt = qt_ref[c]
        vt = vt_ref[c]
        k = k_ref[pl.ds(off, L), :]
        g = gp_ref[c]
        ig = g[d:d + 1, :]
        brow = g[2 + d:3 + d, :]
        blast = brow[:, L - 1:L] if d == 0 else brow[:, 0:1]
        key_term = jnp.broadcast_to(ig - brow, (L, L)).T
        causal = (row <= col) if d == 0 else (row >= col)
        dm = jnp.where(causal, key_term + brow, -jnp.inf)
        inter = brow + m
        m_row = jnp.maximum(jnp.max(dm, axis=0, keepdims=True), inter)
        w = jnp.exp(dm - m_row)
        w_inter = jnp.exp(inter - m_row)
        s = _dot(k, qt) * (w * k_scale)
        ct = ct_scr[d]
        n8 = n_scr[d]
        num = _dot(vt, s.astype(BF16)) + w_inter * _dot(ct.astype(BF16), qt)
        nq = _dot(n8.astype(BF16), qt)[0:1, :]
        den = jnp.sum(s, axis=0, keepdims=True) + w_inter * nq
        h_t = num * (1.0 / jnp.maximum(jnp.abs(den), jnp.exp(-m_row)))
        wj = blast - brow + ig
        m_new = jnp.maximum(blast + m, jnp.max(wj, axis=1, keepdims=True))
        decay = jnp.exp(blast + m - m_new)
        e = jnp.exp(wj - m_new) * k_scale
        ct_scr[d] = decay * ct + _dot((vt.astype(F32) * e).astype(BF16), k)
        n_scr[d] = decay * n8 + _dot3(jnp.broadcast_to(e, (8, L)), k)
        if d == 0:
            hf_scr[c] = h_t
        else:
            hb_scr[c] = h_t
        return m_new

    def body(ci, ms):
        return step(0, ci, ms[0]), step(1, nc - 1 - ci, ms[1])

    m_fin = _scan_loop(nc, body, tuple(m_init))

    def finish(c, carry):
        off = _chunk_off(c)
        h_t = hf_scr[c] + hb_scr[c]
        y = h_t * lax.rsqrt(jnp.mean(h_t * h_t, axis=0, keepdims=True) + EPS) * nw_ref[...]
        o = jax.nn.sigmoid(ot_ref[c].astype(F32)) * y
        hm_ref[pl.ds(off, L), :] = o.T.astype(hm_ref.dtype)
        return carry

    _scan_loop(nc, finish, 0)

    if write_state:
        lsel = (lambda d: (0, d)) if first_layer else (lambda d: (d,))
        for d in range(2):
            cout_ref[lsel(d)] = ct_scr[d].T
            nout_ref[lsel(d)] = n_scr[d][0:1, :]
            mout_ref[lsel(d)] = jnp.broadcast_to(m_fin[d], (1, LANES))
        if first_layer:
            for l in range(1, cout_ref.shape[0]):
                cout_ref[l] = jnp.zeros(cout_ref.shape[1:], F32)
                nout_ref[l] = jnp.zeros(nout_ref.shape[1:], F32)
                mout_ref[l] = jnp.zeros(mout_ref.shape[1:], F32)


def _state_specs(shapes, layer, first_layer):
    specs = []
    for shp in shapes:
        tail = shp[4:]
        zeros = (0,) * len(tail)
        if first_layer:
            specs.append(pl.BlockSpec((None, shp[1], 2, None) + tail, lambda b, h, z=zeros: (b, 0, 0, h) + z))
        else:
            specs.append(pl.BlockSpec((None, None, 2, None) + tail, lambda b, h, z=zeros: (b, layer, 0, h) + z))
    return specs


def _mlstm_call(proj_t, proj_n, gp, nw_b, init, state_bufs, *, n_seq, seq_len, tok_off, n_heads, layer, depth):
    dk = HEAD_DIM
    nc = seq_len // CHUNK
    boff = tok_off // seq_len
    has_init = init is not None
    write_state = not has_init
    first_layer = layer == 0
    args = [proj_t, proj_t, proj_t, proj_n, gp, nw_b]
    specs = [
        pl.BlockSpec((nc, dk, CHUNK), lambda b, h: (b + boff, h, 0)),
        pl.BlockSpec((nc, dk, CHUNK), lambda b, h: (b + boff, n_heads + h, 0)),
        pl.BlockSpec((nc, dk, CHUNK), lambda b, h: (b + boff, 2 * n_heads + h, 0)),
        pl.BlockSpec((seq_len, dk), lambda b, h: (b + boff, h)),
        pl.BlockSpec((None, nc, 4, CHUNK), lambda b, h: (h, b + boff, 0, 0)),
        pl.BlockSpec((dk, LANES), lambda b, h: (h, 0)),
    ]
    aliases = {}
    if has_init:
        c0, n0, m0 = init
        args += [c0, n0[:, layer][:, :, :, None, :], m0[:, layer].reshape(-1)]
        specs += [
            pl.BlockSpec((None, None, 2, None, dk, dk), lambda b, h: (b, layer, 0, h, 0, 0)),
            pl.BlockSpec((None, 2, None, 1, dk), lambda b, h: (b, 0, h, 0, 0)),
            pl.BlockSpec(memory_space=pltpu.SMEM),
        ]
    out_shape = [jax.ShapeDtypeStruct((n_seq * seq_len, n_heads * dk), BF16)]
    out_specs = [pl.BlockSpec((seq_len, dk), lambda b, h: (b, h))]
    if write_state:
        shapes = [(n_seq, depth, 2, n_heads, dk, dk), (n_seq, depth, 2, n_heads, 1, dk),
                  (n_seq, depth, 2, n_heads, 1, LANES)]
        if not first_layer:
            for k_, buf in enumerate(state_bufs):
                aliases[len(args)] = 1 + k_
                args.append(buf)
                specs.append(pl.BlockSpec(memory_space=pl.ANY))
        out_shape += [jax.ShapeDtypeStruct(s, F32) for s in shapes]
        out_specs += _state_specs(shapes, layer, first_layer)
    kern = functools.partial(_mlstm_kernel, nc=nc, has_init=has_init, first_layer=first_layer,
                             write_state=write_state)
    return pl.pallas_call(
        kern, grid=(n_seq, n_heads), in_specs=specs, out_specs=out_specs, out_shape=out_shape,
        input_output_aliases=aliases,
        scratch_shapes=[pltpu.VMEM((2, dk, dk), F32), pltpu.VMEM((2, 8, dk), F32),
                        pltpu.VMEM((nc, dk, CHUNK), F32), pltpu.VMEM((nc, dk, CHUNK), F32)],
        compiler_params=_cparams(("parallel", "parallel"), 40), name="mlstm_mixer",
    )(*args)


def _ret_kernel(*refs, nc, has_init, first_layer, write_state):
    it = iter(refs)
    dl_ref, qt_ref, vt_ref, gt_ref, k_ref, nw_ref = (next(it) for _ in range(6))
    if has_init:
        s0_ref = next(it)
    if write_state and not first_layer:
        next(it)
    hr_ref = next(it)
    if write_state:
        sout_ref = next(it)
    st_scr, of_scr, ob_scr = next(it), next(it), next(it)

    L = CHUNK
    rowi = _iota((L, L), 0)
    coli = _iota((L, L), 1)
    k_scale = HEAD_DIM ** -0.5
    h_idx = pl.program_id(1)
    n_heads = pl.num_programs(1)

    consts = []
    for d in range(2):
        lg = _log_sigmoid(jnp.full((1, 1), dl_ref[d * n_heads + h_idx], F32))
        rel = (coli - rowi if d == 0 else rowi - coli).astype(F32)
        dmat_t = jnp.where(rel >= 0, jnp.exp(jnp.maximum(rel, 0.0) * lg), 0.0) * k_scale
        pos = _iota((1, L), 1).astype(F32)
        if d == 1:
            pos = (L - 1.0) - pos
        q_decay = jnp.exp((pos + 1.0) * lg)
        k_decay = jnp.exp((L - 1.0 - pos) * lg) * k_scale
        chunk_decay = jnp.exp(float(L) * lg)
        consts.append((dmat_t, q_decay, k_decay, chunk_decay))
        if has_init:
            st_scr[d] = s0_ref[d].T
        else:
            st_scr[d] = jnp.zeros(st_scr.shape[1:], F32)

    def step(d, c):
        dmat_t, q_decay, k_decay, chunk_decay = consts[d]
        off = _chunk_off(c)
        qt = qt_ref[c]
        vt = vt_ref[c]
        k = k_ref[pl.ds(off, L), :]
        sc = _dot(k, qt) * dmat_t
        st = st_scr[d]
        o_t = _dot(vt, sc.astype(BF16)) + q_decay * _dot(st.astype(BF16), qt)
        st_scr[d] = chunk_decay * st + _dot((vt.astype(F32) * k_decay).astype(BF16), k)
        if d == 0:
            of_scr[c] = o_t
        else:
            ob_scr[c] = o_t

    def body(ci, carry):
        step(0, ci)
        step(1, nc - 1 - ci)
        return carry

    _scan_loop(nc, body, 0)

    def finish(c, carry):
        off = _chunk_off(c)
        o_t = of_scr[c] + ob_scr[c]
        y = o_t * lax.rsqrt(jnp.mean(o_t * o_t, axis=0, keepdims=True) + EPS) * nw_ref[...]
        rg = gt_ref[c].astype(F32)
        hr_ref[pl.ds(off, L), :] = (rg * jax.nn.sigmoid(rg) * y).T.astype(hr_ref.dtype)
        return carry

    _scan_loop(nc, finish, 0)

    if write_state:
        for d in range(2):
            if first_layer:
                sout_ref[0, d] = st_scr[d].T
            else:
                sout_ref[d] = st_scr[d].T
        if first_layer:
            for l in range(1, sout_ref.shape[0]):
                sout_ref[l] = jnp.zeros(sout_ref.shape[1:], F32)


def _ret_call(proj_t, proj_n, decay_logit, nw_b, init, state_buf, *, n_seq, seq_len, tok_off, n_heads, layer,
              depth):
    dk = HEAD_DIM
    nc = seq_len // CHUNK
    boff = tok_off // seq_len
    has_init = init is not None
    write_state = not has_init
    first_layer = layer == 0
    args = [decay_logit.reshape(-1), proj_t, proj_t, proj_t, proj_n, nw_b]
    specs = [
        pl.BlockSpec(memory_space=pltpu.SMEM),
        pl.BlockSpec((nc, dk, CHUNK), lambda b, h: (b + boff, 3 * n_heads + h, 0)),
        pl.BlockSpec((nc, dk, CHUNK), lambda b, h: (b + boff, 4 * n_heads + h, 0)),
        pl.BlockSpec((nc, dk, CHUNK), lambda b, h: (b + boff, 5 * n_heads + h, 0)),
        pl.BlockSpec((seq_len, dk), lambda b, h: (b + boff, n_heads + h)),
        pl.BlockSpec((dk, LANES), lambda b, h: (h, 0)),
    ]
    aliases = {}
    if has_init:
        args.append(init)
        specs.append(pl.BlockSpec((None, None, 2, None, dk, dk), lambda b, h: (b, layer, 0, h, 0, 0)))
    out_shape = [jax.ShapeDtypeStruct((n_seq * seq_len, n_heads * dk), BF16)]
    out_specs = [pl.BlockSpec((seq_len, dk), lambda b, h: (b, h))]
    if write_state:
        shapes = [(n_seq, depth, 2, n_heads, dk, dk)]
        if not first_layer:
            aliases[len(args)] = 1
            args.append(state_buf)
            specs.append(pl.BlockSpec(memory_space=pl.ANY))
        out_shape += [jax.ShapeDtypeStruct(s, F32) for s in shapes]
        out_specs += _state_specs(shapes, layer, first_layer)
    kern = functools.partial(_ret_kernel, nc=nc, has_init=has_init, first_layer=first_layer,
                             write_state=write_state)
    return pl.pallas_call(
        kern, grid=(n_seq, n_heads), in_specs=specs, out_specs=out_specs, out_shape=out_shape,
        input_output_aliases=aliases,
        scratch_shapes=[pltpu.VMEM((2, dk, dk), F32), pltpu.VMEM((nc, dk, CHUNK), F32),
                        pltpu.VMEM((nc, dk, CHUNK), F32)],
        compiler_params=_cparams(("parallel", "parallel"), 40), name="retention_mixer",
    )(*args)


def _merge_kernel(hmp_ref, hms_ref, hrp_ref, hrs_ref, ga_ref, gb_ref, x_ref, mod_ref, nw_ref, wa_ref, wb_ref,
                  wo_ref, wr_ref, x1_ref, h2_ref, aff_ref, wab_scr, wbb_scr, wob_scr, *, n_prompt_tiles):
    i = pl.program_id(0)

    @pl.when(i == 0)
    def _():
        wab_scr[...] = wa_ref[...].astype(BF16)
        wbb_scr[...] = wb_ref[...].astype(BF16)
        wob_scr[...] = wo_ref[...].astype(BF16)

    is_prompt = i < n_prompt_tiles
    hm = jnp.where(is_prompt, hmp_ref[...], hms_ref[...])
    hr = jnp.where(is_prompt, hrp_ref[...], hrs_ref[...])
    ya = _dot(hm, wab_scr[...])
    yb = _dot(hr, wbb_scr[...])
    merged = (jax.nn.sigmoid(ga_ref[...].astype(F32)) * ya + jax.nn.sigmoid(gb_ref[...].astype(F32)) * yb)
    y = _dot(merged.astype(BF16), wob_scr[...])
    x1 = x_ref[...] + mod_ref[2:3, :] * y
    x1_ref[...] = x1
    h2 = x1 * lax.rsqrt(jnp.mean(x1 * x1, axis=-1, keepdims=True) + EPS) * nw_ref[...]
    h2 = h2 * (1.0 + mod_ref[4:5, :]) + mod_ref[3:4, :]
    h2_ref[...] = h2
    logits = _dot_nt(wr_ref[...].astype(BF16), h2.astype(BF16))
    p = jnp.exp(logits - jnp.max(logits, axis=0, keepdims=True))
    aff_ref[...] = p / jnp.sum(p, axis=0, keepdims=True)


def _merge_call(hm_p, hm_s, hr_p, hr_s, proj_n, x, mod, norm_w, w_a, w_b, w_out, router_wt, layer, *,
                n_prompt, dec_seq, tm=512):
    n, d = x.shape
    ne = router_wt.shape[1]
    npt = n_prompt // tm
    grp = functools.partial(_group_of_tile, tm=tm, n_prompt=n_prompt, dec_seq=dec_seq)
    tile = pl.BlockSpec((tm, d), lambda i: (i, 0))
    tile_p = pl.BlockSpec((tm, d), lambda i: (jnp.minimum(i, npt - 1), 0))
    tile_s = pl.BlockSpec((tm, d), lambda i: (jnp.maximum(i - npt, 0), 0))
    full = pl.BlockSpec((None, d, d), lambda i: (layer, 0, 0))
    return pl.pallas_call(
        functools.partial(_merge_kernel, n_prompt_tiles=npt), grid=(n // tm,),
        in_specs=[
            tile_p, tile_s, tile_p, tile_s,
            pl.BlockSpec((tm, d), lambda i: (i, 2)),
            pl.BlockSpec((tm, d), lambda i: (i, 3)),
            tile,
            pl.BlockSpec((None, 6, d), lambda i: (grp(i), 0, 0)),
            pl.BlockSpec((1, d), lambda i: (0, 0)),
            full, full, full,
            pl.BlockSpec((None, ne, d), lambda i: (layer, 0, 0)),
        ],
        out_specs=[tile, tile, pl.BlockSpec((ne, tm), lambda i: (0, i))],
        out_shape=[jax.ShapeDtypeStruct((n, d), F32), jax.ShapeDtypeStruct((n, d), F32),
                   jax.ShapeDtypeStruct((ne, n), F32)],
        scratch_shapes=[pltpu.VMEM((d, d), BF16)] * 3,
        compiler_params=_cparams(("arbitrary",), 56), name="merge_out_router",
    )(hm_p, hm_s, hr_p, hr_s, proj_n, proj_n, x, mod, norm_w.reshape(1, d), w_a, w_b, w_out, router_wt)


def _route_kernel(a_ref, at_ref, idx_ref, gate_ref, thr_scr, *, cap, n_tok):
    n_sets = a_ref.shape[0]
    nb = n_tok // LANES
    bits_all = pltpu.bitcast(a_ref[...], jnp.int32)

    def count_ge(cand):
        m = jnp.where(bits_all >= cand, 1.0, 0.0)
        return jnp.sum(jnp.sum(m, axis=2, keepdims=True), axis=1, keepdims=True)

    def bit_step(k, thr):
        cand = thr | lax.shift_left(jnp.int32(1), 30 - k)
        return jnp.where(count_ge(cand) >= cap, cand, thr)

    thr_scr[...] = lax.fori_loop(0, 31, bit_step, jnp.zeros((n_sets, 1, 1), jnp.int32))

    r128 = _iota((LANES, LANES), 0)
    c128 = _iota((LANES, LANES), 1)
    upper = jnp.where(r128 <= c128, 1.0, 0.0).astype(BF16)
    lower_t = jnp.where(r128 >= c128, 1.0, 0.0).astype(BF16)
    rb = _iota((nb, nb), 0)
    cb = _iota((nb, nb), 1)
    blk_before_rows = jnp.where(cb < rb, 1.0, 0.0).astype(BF16)
    blk_before_cols = jnp.where(rb < cb, 1.0, 0.0).astype(BF16)

    def incl_counts(mask):
        within = _dot(mask.astype(BF16), upper)
        before = _dot(blk_before_rows, within.astype(BF16))[:, LANES - 1:LANES]
        return within + before

    def incl_counts_t(mask_t):
        within = _dot(lower_t, mask_t.astype(BF16))
        before = _dot(within.astype(BF16), blk_before_cols)[LANES - 1:LANES, :]
        return within + before

    slot = _iota((1, cap), 1).astype(F32)
    blk_col = _iota((nb, 1), 0).astype(F32)
    sub_col = _iota((LANES, 1), 0).astype(F32)

    def per_set(s, carry):
        thr = thr_scr[s]
        a = a_ref[s]
        a_t = at_ref[s]
        bits = pltpu.bitcast(a, jnp.int32)
        bits_t = pltpu.bitcast(a_t, jnp.int32)
        gt = jnp.where(bits > thr, 1.0, 0.0)
        eq = jnp.where(bits == thr, 1.0, 0.0)
        gt_t = jnp.where(bits_t > thr, 1.0, 0.0)
        eq_t = jnp.where(bits_t == thr, 1.0, 0.0)
        n_gt = jnp.sum(jnp.sum(gt, axis=1, keepdims=True), axis=0, keepdims=True)
        need = cap - n_gt
        sel = gt + eq * jnp.where(incl_counts(eq) - eq < need, 1.0, 0.0)
        sel_t = gt_t + eq_t * jnp.where(incl_counts_t(eq_t) - eq_t < need, 1.0, 0.0)
        cnt = incl_counts(sel)
        cnt_t = incl_counts_t(sel_t)
        blk_end = cnt[:, LANES - 1:LANES]
        blk_of_slot = jnp.sum(jnp.where(blk_end <= slot, 1.0, 0.0), axis=0, keepdims=True)
        onehot_blk = jnp.where(blk_col == blk_of_slot, 1.0, 0.0).astype(BF16)
        cnt_rows = _dot3(cnt_t, onehot_blk)
        sub_of_slot = jnp.sum(jnp.where(cnt_rows <= slot, 1.0, 0.0), axis=0, keepdims=True)
        a_rows = _dot3(a_t, onehot_blk)
        gate = jnp.sum(jnp.where(sub_col == sub_of_slot, a_rows, 0.0), axis=0, keepdims=True)
        idx_ref[s] = (blk_of_slot * LANES + sub_of_slot).astype(jnp.int32)
        gate_ref[s] = gate
        return carry

    lax.fori_loop(0, n_sets, per_set, 0)


def _route_call(aff_t, n_pass, cap):
    ne, n = aff_t.shape
    n_tok = n // n_pass
    nb = n_tok // LANES
    a4 = aff_t.reshape(ne, n_pass, nb, LANES).transpose(1, 0, 2, 3).reshape(n_pass * ne, nb, LANES)
    a4_t = a4.transpose(0, 2, 1)
    n_sets = n_pass * ne
    return pl.pallas_call(
        functools.partial(_route_kernel, cap=cap, n_tok=n_tok),
        out_shape=[jax.ShapeDtypeStruct((n_sets, 1, cap), jnp.int32),
                   jax.ShapeDtypeStruct((n_sets, 1, cap), F32)],
        scratch_shapes=[pltpu.VMEM((n_sets, 1, 1), jnp.int32)],
        compiler_params=pltpu.CompilerParams(vmem_limit_bytes=32 * MIB), name="expert_choice_route",
    )(a4, a4_t)


ROW_LOOP_UNROLL = 8


def _moe_kernel(idx_ref, gate_ref, h_hbm, w1_ref, w3_ref, w2_ref, out_hbm,
                xe_scr, xb_scr, ye_scr, acc_scr, gsem, osem, *, rows):
    e = pl.program_id(0)
    f = pl.program_id(1)
    n_e = pl.num_programs(0)
    n_f = pl.num_programs(1)

    def start_gather(ee):
        def issue(s, carry):
            tok = idx_ref[ee * rows + s]
            pltpu.make_async_copy(h_hbm.at[pl.ds(tok, 1), :], xe_scr.at[pl.ds(s, 1), :], gsem).start()
            return carry
        lax.fori_loop(0, rows, issue, 0, unroll=ROW_LOOP_UNROLL)

    @pl.when((e == 0) & (f == 0))
    def _():
        acc_scr[...] = jnp.zeros_like(acc_scr)
        start_gather(0)

    @pl.when(f == 0)
    def _():
        pltpu.make_async_copy(h_hbm.at[pl.ds(0, rows), :], xe_scr, gsem).wait()
        xb_scr[...] = xe_scr[...].astype(BF16)
        ye_scr[...] = jnp.zeros_like(ye_scr)

        @pl.when(e + 1 < n_e)
        def _():
            start_gather(e + 1)

    xb = xb_scr[...]
    h1 = _dot(xb, w1_ref[...].astype(BF16))
    h3 = _dot(xb, w3_ref[...].astype(BF16))
    he = (h1 * jax.nn.sigmoid(h1) * h3).astype(BF16)
    ye_scr[...] += _dot(he, w2_ref[...].astype(BF16))

    @pl.when(f == n_f - 1)
    def _():
        def scatter(s, carry):
            tok = idx_ref[e * rows + s]
            g = gate_ref[e * rows + s]
            acc_scr[pl.ds(tok, 1), :] += ye_scr[pl.ds(s, 1), :] * g
            return carry
        lax.fori_loop(0, rows, scatter, 0, unroll=ROW_LOOP_UNROLL)

        @pl.when(e == n_e - 1)
        def _():
            cp = pltpu.make_async_copy(acc_scr, out_hbm, osem)
            cp.start()
            cp.wait()


def _moe_call(idx, gate, h2, w1, w3, w2, layer, tf=256):
    n, d = h2.shape
    ne = w1.shape[1]
    dff = w1.shape[3]
    rows = idx.shape[0] // ne
    grid_spec = pltpu.PrefetchScalarGridSpec(
        num_scalar_prefetch=1,
        grid=(ne, dff // tf),
        in_specs=[
            pl.BlockSpec(memory_space=pltpu.SMEM),
            pl.BlockSpec(memory_space=pl.ANY),
            pl.BlockSpec((None, None, d, tf), lambda e, f, idx: (layer, e, 0, f)),
            pl.BlockSpec((None, None, d, tf), lambda e, f, idx: (layer, e, 0, f)),
            pl.BlockSpec((None, None, tf, d), lambda e, f, idx: (layer, e, f, 0)),
        ],
        out_specs=pl.BlockSpec(memory_space=pl.ANY),
        scratch_shapes=[
            pltpu.VMEM((rows, d), F32), pltpu.VMEM((rows, d), BF16), pltpu.VMEM((rows, d), F32),
            pltpu.VMEM((n, d), F32), pltpu.SemaphoreType.DMA, pltpu.SemaphoreType.DMA,
        ],
    )
    return pl.pallas_call(
        functools.partial(_moe_kernel, rows=rows), grid_spec=grid_spec,
        out_shape=jax.ShapeDtypeStruct((n, d), F32),
        compiler_params=_cparams(("arbitrary", "arbitrary"), 60), name="expert_ffn",
    )(idx, gate, h2, w1, w3, w2)


def _rope_tables(t):
    rows = t // GRID_W
    row = jnp.repeat(jnp.arange(rows, dtype=F32), GRID_W)
    colp = jnp.tile(jnp.arange(GRID_W, dtype=F32), rows)
    n_freq = HEAD_DIM // 4
    inv = ROPE_BASE ** (-jnp.arange(n_freq, dtype=F32) / n_freq)
    ang = jnp.concatenate([row[:, None] * inv, colp[:, None] * inv], axis=-1)
    return jnp.cos(ang), jnp.sin(ang)


def kernel(x_prompt, x_sample, c, state_mlstm_C, state_mlstm_n, state_mlstm_m, state_ret_S, c_ctx, w_mod, b_mod,
           norm1_w, norm2_w, w_in, mlstm_if_b, mlstm_norm_w, ret_decay_logit, ret_norm_w, w_branch_a, w_branch_b,
           w_out, router_w, ffn_w1, ffn_w3, ffn_w2, final_norm_w):
    bp, seq, d = x_prompt.shape
    db, dec_seq, _ = x_sample.shape
    depth = w_mod.shape[0]
    hm = mlstm_if_b.shape[-1]
    hr = ret_decay_logit.shape[-1]
    ne = router_w.shape[-1]
    n_prompt = bp * seq
    n_sample = db * dec_seq
    assert n_prompt == n_sample and hm * HEAD_DIM == d and hr * HEAD_DIM == d and hm == hr
    cap = EC_FACTOR * n_prompt // ne
    tm = 512
    geo = dict(n_prompt=n_prompt, dec_seq=dec_seq)

    x = jnp.concatenate([x_prompt.reshape(n_prompt, d), x_sample.reshape(n_sample, d)], axis=0)
    cond8 = jnp.concatenate([c_ctx[None, :], c, jnp.zeros((8 - 1 - db, d), F32)], axis=0)
    mod = _mod_call(cond8, w_mod, b_mod)[:, :1 + db].reshape(depth, 1 + db, 6, d)

    n_m = 4 * d
    g0 = n_m + 4 * hm

    def cols(k):
        start = k * d if k < 4 else g0 + (k - 4) * d
        return w_in[:, :, start:start + d]

    mq, mk, mv, mo, rq, rk, rv, rg, ga, gb = (cols(k) for k in range(10))
    w_t = jnp.swapaxes(jnp.concatenate([mq, mv, mo, rq, rv, rg], axis=2), 1, 2).astype(BF16)
    w_n = jnp.concatenate([mk, rk, ga, gb], axis=2).astype(BF16)
    wg_t = jnp.swapaxes(w_in[:, :, n_m:g0], 1, 2)
    gate_bias = jnp.transpose(mlstm_if_b, (0, 2, 1, 3)).reshape(depth, 4 * hm, 1)
    router_wt = jnp.swapaxes(router_w, 1, 2)
    nw_m = jnp.broadcast_to(mlstm_norm_w[:, :, None], (depth, d, LANES))
    nw_r = jnp.broadcast_to(ret_norm_w[:, :, None], (depth, d, LANES))

    cos, sin = _rope_tables(dec_seq)
    cos_t = cos.T.reshape(HEAD_DIM // 2, dec_seq // tm, tm).transpose(1, 0, 2)
    sin_t = sin.T.reshape(HEAD_DIM // 2, dec_seq // tm, tm).transpose(1, 0, 2)

    _, h = _resid_norm_call(x, None, None, mod[0], norm1_w[0], gate_row=0, mod_rows=(0, 1), h_dtype=BF16,
                            write_x=False, **geo)
    m_bufs, s_buf = None, None
    y_prompt = y_sample = None
    for l in range(depth):
        proj_t = _proj_t_call(h, w_t, l, (cos_t, sin_t), 3, tm=tm, **geo)
        proj_n = _proj_call(h, w_n, l, (cos, sin), 1, tm=tm, **geo)
        gp = _gate_call(h, wg_t[l], gate_bias[l], hm, tm=tm)

        mix = dict(n_heads=hm, layer=l, depth=depth)
        hm_p, *m_bufs = _mlstm_call(proj_t, proj_n, gp, nw_m[l], None, m_bufs, n_seq=bp, seq_len=seq, tok_off=0,
                                    **mix)
        (hm_s,) = _mlstm_call(proj_t, proj_n, gp, nw_m[l], (state_mlstm_C, state_mlstm_n, state_mlstm_m), None,
                              n_seq=db, seq_len=dec_seq, tok_off=n_prompt, **mix)
        hr_p, s_buf = _ret_call(proj_t, proj_n, ret_decay_logit[l], nw_r[l], None, s_buf, n_seq=bp, seq_len=seq,
                                tok_off=0, **mix)
        (hr_s,) = _ret_call(proj_t, proj_n, ret_decay_logit[l], nw_r[l], state_ret_S, None, n_seq=db,
                            seq_len=dec_seq, tok_off=n_prompt, **mix)

        x1, h2, aff_t = _merge_call(hm_p, hm_s, hr_p, hr_s, proj_n, x, mod[l], norm2_w[l], w_branch_a, w_branch_b,
                                    w_out, router_wt, l, tm=tm, **geo)
        idx, gate = _route_call(aff_t, 2, cap)
        idx = idx.reshape(2, ne, cap) + (jnp.arange(2, dtype=jnp.int32) * n_prompt)[:, None, None]
        idx = jnp.swapaxes(idx, 0, 1).reshape(-1)
        gate = jnp.swapaxes(gate.reshape(2, ne, cap), 0, 1).reshape(-1)
        moe = _moe_call(idx, gate, h2, ffn_w1, ffn_w3, ffn_w2, l)
        if l + 1 < depth:
            x, h = _resid_norm_call(x1, moe, mod[l], mod[l + 1], norm1_w[l + 1], gate_row=5, mod_rows=(0, 1),
                                    h_dtype=BF16, write_x=True, **geo)
        else:
            fin = dict(gate_row=5, mod_rows=None, h_dtype=F32, write_x=False, **geo)
            _, y_prompt = _resid_norm_call(x1, moe, mod[l], None, final_norm_w, row_off=0, n_rows=n_prompt, **fin)
            _, y_sample = _resid_norm_call(x1, moe, mod[l], None, final_norm_w, row_off=n_prompt, n_rows=n_sample,
                                           **fin)

    c_buf, n_buf, m_buf = m_bufs
    return (y_prompt.reshape(bp, seq, d), y_sample.reshape(db, dec_seq, d), c_buf, n_buf[:, :, :, :, 0, :],
            m_buf[:, :, :, :, 0, 0], s_buf)
```

```python
import functools

import numpy as np
import jax
import jax.numpy as jnp
from jax import lax
from jax.experimental import pallas as pl
from jax.experimental.pallas import tpu as pltpu

F32 = jnp.float32
BF16 = jnp.bfloat16

GRID_W = 64
CHUNK = 128
LANES = 128
HEAD_DIM = 256
N_EXPERTS = 16
EC_FACTOR = 2
ROPE_BASE = 10000.0
EPS = 1e-6
V7X_VMEM_BYTES = 64 * 1024 * 1024
MIB = 1024 * 1024


def _cparams(semantics, vmem_mib):
    assert vmem_mib * MIB < V7X_VMEM_BYTES
    return pltpu.CompilerParams(dimension_semantics=semantics, vmem_limit_bytes=vmem_mib * MIB)


def _dot(a, b):
    return jnp.dot(a, b, preferred_element_type=F32)


def _dot_nt(a, b):
    return lax.dot_general(a, b, (((1,), (1,)), ((), ())), preferred_element_type=F32)


def _log_sigmoid(x):
    return -(jnp.maximum(-x, 0.0) + jnp.log1p(jnp.exp(-jnp.abs(x))))


def _split3(x):
    hi = x.astype(BF16)
    r1 = x - hi.astype(F32)
    mid = r1.astype(BF16)
    lo = (r1 - mid.astype(F32)).astype(BF16)
    return hi, mid, lo


def _dot3(x, m):
    hi, mid, lo = _split3(x)
    return _dot(hi, m) + _dot(mid, m) + _dot(lo, m)


def _iota(shape, dim):
    return lax.broadcasted_iota(jnp.int32, shape, dim)


def _mod_kernel(cond_ref, w_ref, b_ref, out_ref):
    c = cond_ref[...]
    s = (c * jax.nn.sigmoid(c)).astype(BF16)
    out_ref[...] = _dot(s, w_ref[...].astype(BF16)) + b_ref[...]


def _mod_call(cond8, w_mod, b_mod):
    depth, d, w6 = w_mod.shape
    tn = 1536
    return pl.pallas_call(
        _mod_kernel,
        grid=(depth, w6 // tn),
        in_specs=[
            pl.BlockSpec((8, d), lambda l, j: (0, 0)),
            pl.BlockSpec((None, d, tn), lambda l, j: (l, 0, j)),
            pl.BlockSpec((None, 1, tn), lambda l, j: (l, 0, j)),
        ],
        out_specs=pl.BlockSpec((None, 8, tn), lambda l, j: (l, 0, j)),
        out_shape=jax.ShapeDtypeStruct((depth, 8, w6), F32),
        compiler_params=_cparams(("parallel", "parallel"), 32),
        name="adaln_mod",
    )(cond8, w_mod, b_mod.reshape(depth, 1, w6))


def _group_of_tile(i, tm, n_prompt, dec_seq):
    return jnp.maximum(i * tm - n_prompt + dec_seq, 0) // dec_seq


def _resid_norm_kernel(*refs, has_delta, has_mod, gate_row, mod_rows, write_x):
    it = iter(refs)
    x_ref = next(it)
    delta_ref = next(it) if has_delta else None
    mod_ref = next(it) if (has_delta or has_mod) else None
    nw_ref = next(it)
    xo_ref = next(it) if write_x else None
    h_ref = next(it)
    x = x_ref[...]
    if has_delta:
        x = x + mod_ref[gate_row:gate_row + 1, :] * delta_ref[...]
    if write_x:
        xo_ref[...] = x
    y = x * lax.rsqrt(jnp.mean(x * x, axis=-1, keepdims=True) + EPS) * nw_ref[...]
    if has_mod:
        sh_row, sc_row = mod_rows
        y = y * (1.0 + mod_ref[sc_row:sc_row + 1, :]) + mod_ref[sh_row:sh_row + 1, :]
    h_ref[...] = y.astype(h_ref.dtype)


def _resid_norm_call(x, delta, mod_gate, mod_norm, norm_w, *, gate_row, mod_rows, h_dtype, write_x,
                     n_prompt, dec_seq, tm=512, row_off=0, n_rows=None):
    n, d = x.shape
    n_rows = n if n_rows is None else n_rows
    toff = row_off // tm
    has_delta = delta is not None
    has_mod = mod_norm is not None
    grp = functools.partial(_group_of_tile, tm=tm, n_prompt=n_prompt, dec_seq=dec_seq)
    tile_in = pl.BlockSpec((tm, d), lambda i: (i + toff, 0))
    tile_out = pl.BlockSpec((tm, d), lambda i: (i, 0))
    args, specs = [x], [tile_in]
    if has_delta:
        args.append(delta)
        specs.append(tile_in)
    if has_delta or has_mod:
        mg = mod_gate if has_delta else mod_norm
        mn = mod_norm if has_mod else mod_gate
        args.append(jnp.concatenate([mg, mn], axis=1))
        specs.append(pl.BlockSpec((None, 12, d), lambda i: (grp(i + toff), 0, 0)))
    args.append(norm_w.reshape(1, d))
    specs.append(pl.BlockSpec((1, d), lambda i: (0, 0)))
    out_shape, out_specs = [], []
    if write_x:
        out_shape.append(jax.ShapeDtypeStruct((n_rows, d), F32))
        out_specs.append(tile_out)
    out_shape.append(jax.ShapeDtypeStruct((n_rows, d), h_dtype))
    out_specs.append(tile_out)
    kern = functools.partial(
        _resid_norm_kernel, has_delta=has_delta, has_mod=has_mod, gate_row=gate_row,
        mod_rows=None if mod_rows is None else (6 + mod_rows[0], 6 + mod_rows[1]), write_x=write_x)
    outs = pl.pallas_call(
        kern, grid=(n_rows // tm,), in_specs=specs, out_specs=out_specs, out_shape=out_shape,
        compiler_params=_cparams(("parallel",), 32), name="resid_norm",
    )(*args)
    return outs if write_x else (None, outs[0])


def _rope_pair(x1, x2, cos, sin):
    return x1 * cos - x2 * sin, x1 * sin + x2 * cos


def _proj_kernel(h_ref, w_ref, cos_ref, sin_ref, out_ref, *, tm, rope_tile, n_prompt, dec_seq):
    j = pl.program_id(0)
    i = pl.program_id(1)
    row0 = pl.multiple_of(i * tm, tm)
    use_rope = (j == rope_tile) & (row0 >= n_prompt)
    pos0 = pl.multiple_of(jnp.where(use_rope, (row0 - n_prompt) % dec_seq, 0), tm)
    cos = cos_ref[pl.ds(pos0, tm), :]
    sin = sin_ref[pl.ds(pos0, tm), :]
    h = h_ref[pl.ds(row0, tm), :]
    half = HEAD_DIM // 2
    for hh in range(w_ref.shape[1] // HEAD_DIM):
        c0 = hh * HEAD_DIM
        acc = _dot(h, w_ref[:, c0:c0 + HEAD_DIM])
        x1, x2 = acc[:, :half], acc[:, half:]
        y1, y2 = _rope_pair(x1, x2, cos, sin)
        out_ref[:, c0:c0 + half] = jnp.where(use_rope, y1, x1).astype(out_ref.dtype)
        out_ref[:, c0 + half:c0 + HEAD_DIM] = jnp.where(use_rope, y2, x2).astype(out_ref.dtype)


def _proj_call(h, w, layer, rope, rope_tile, *, n_prompt, dec_seq, tm=512, tn=1024):
    n, d = h.shape
    width = w.shape[2]
    kern = functools.partial(_proj_kernel, tm=tm, rope_tile=rope_tile, n_prompt=n_prompt, dec_seq=dec_seq)
    return pl.pallas_call(
        kern, grid=(width // tn, n // tm),
        in_specs=[
            pl.BlockSpec((n, d), lambda j, i: (0, 0)),
            pl.BlockSpec((None, d, tn), lambda j, i: (layer, 0, j)),
            pl.BlockSpec(rope[0].shape, lambda j, i: (0, 0)),
            pl.BlockSpec(rope[1].shape, lambda j, i: (0, 0)),
        ],
        out_specs=pl.BlockSpec((tm, tn), lambda j, i: (i, j)),
        out_shape=jax.ShapeDtypeStruct((n, width), BF16),
        compiler_params=_cparams(("parallel", "arbitrary"), 56), name="in_proj",
    )(h, w, *rope)


def _proj_t_kernel(h_ref, w_ref, cos_ref, sin_ref, out_ref, *, tm, rope_tile, n_prompt, dec_seq):
    j = pl.program_id(0)
    i = pl.program_id(1)
    row0 = pl.multiple_of(i * tm, tm)
    use_rope = (j == rope_tile) & (row0 >= n_prompt)
    blk = jnp.where(use_rope, ((row0 - n_prompt) % dec_seq) // tm, 0)
    cos = cos_ref[blk]
    sin = sin_ref[blk]
    h = h_ref[pl.ds(row0, tm), :]
    half = HEAD_DIM // 2
    for hh in range(w_ref.shape[0] // HEAD_DIM):
        r0 = hh * HEAD_DIM
        acc = _dot_nt(w_ref[r0:r0 + HEAD_DIM, :], h)
        x1, x2 = acc[:half, :], acc[half:, :]
        y1, y2 = _rope_pair(x1, x2, cos, sin)
        y1 = jnp.where(use_rope, y1, x1).astype(out_ref.dtype)
        y2 = jnp.where(use_rope, y2, x2).astype(out_ref.dtype)
        for s in range(tm // CHUNK):
            out_ref[s, r0:r0 + half, :] = y1[:, s * CHUNK:(s + 1) * CHUNK]
            out_ref[s, r0 + half:r0 + HEAD_DIM, :] = y2[:, s * CHUNK:(s + 1) * CHUNK]


def _proj_t_call(h, w_t, layer, rope_t, rope_tile, *, n_prompt, dec_seq, tm=512, tn=1024):
    n, d = h.shape
    width = w_t.shape[1]
    kern = functools.partial(_proj_t_kernel, tm=tm, rope_tile=rope_tile, n_prompt=n_prompt, dec_seq=dec_seq)
    return pl.pallas_call(
        kern, grid=(width // tn, n // tm),
        in_specs=[
            pl.BlockSpec((n, d), lambda j, i: (0, 0)),
            pl.BlockSpec((None, tn, d), lambda j, i: (layer, j, 0)),
            pl.BlockSpec(rope_t[0].shape, lambda j, i: (0, 0, 0)),
            pl.BlockSpec(rope_t[1].shape, lambda j, i: (0, 0, 0)),
        ],
        out_specs=pl.BlockSpec((tm // CHUNK, tn, CHUNK), lambda j, i: (i, j, 0)),
        out_shape=jax.ShapeDtypeStruct((n // CHUNK, width, CHUNK), BF16),
        compiler_params=_cparams(("parallel", "arbitrary"), 56), name="in_proj_t",
    )(h, w_t, *rope_t)


def _gate_kernel(h_ref, wg_ref, bias_ref, out_ref, *, tm, n_heads):
    g = _dot_nt(wg_ref[...].astype(BF16), h_ref[...]) + bias_ref[...]
    nd = 2 * n_heads
    ig = g[0:nd, :]
    lf = _log_sigmoid(g[nd:2 * nd, :])
    r = _iota((CHUNK, CHUNK), 0)
    c = _iota((CHUNK, CHUNK), 1)
    upper = jnp.where(r <= c, 1.0, 0.0).astype(BF16)
    lower = jnp.where(r >= c, 1.0, 0.0).astype(BF16)
    is_fwd = _iota((nd, CHUNK), 0) < n_heads
    for s in range(tm // CHUNK):
        sl = slice(s * CHUNK, (s + 1) * CHUNK)
        lf_c = lf[:, sl]
        b = jnp.where(is_fwd, _dot3(lf_c, upper), _dot3(lf_c, lower))
        ig_c = ig[:, sl]
        for hh in range(n_heads):
            out_ref[hh, s, 0:1, :] = ig_c[hh:hh + 1, :]
            out_ref[hh, s, 1:2, :] = ig_c[n_heads + hh:n_heads + hh + 1, :]
            out_ref[hh, s, 2:3, :] = b[hh:hh + 1, :]
            out_ref[hh, s, 3:4, :] = b[n_heads + hh:n_heads + hh + 1, :]


def _gate_call(h, wg_t, bias_col, n_heads, tm=512):
    n, d = h.shape
    ng = 4 * n_heads
    return pl.pallas_call(
        functools.partial(_gate_kernel, tm=tm, n_heads=n_heads),
        grid=(n // tm,),
        in_specs=[
            pl.BlockSpec((tm, d), lambda i: (i, 0)),
            pl.BlockSpec((ng, d), lambda i: (0, 0)),
            pl.BlockSpec((ng, 1), lambda i: (0, 0)),
        ],
        out_specs=pl.BlockSpec((n_heads, tm // CHUNK, 4, CHUNK), lambda i: (0, i, 0, 0)),
        out_shape=jax.ShapeDtypeStruct((n_heads, n // CHUNK, 4, CHUNK), F32),
        compiler_params=_cparams(("parallel",), 32), name="mlstm_gates",
    )(h, wg_t, bias_col)


def _scan_loop(nc, body, init):
    if nc <= 2:
        carry = init
        for ci in range(nc):
            carry = body(ci, carry)
        return carry
    return lax.fori_loop(0, nc, body, init)


def _chunk_off(c):
    return c * CHUNK if isinstance(c, int) else pl.multiple_of(c * CHUNK, CHUNK)


def _mlstm_kernel(*refs, nc, has_init, first_layer, write_state):
    it = iter(refs)
    qt_ref, vt_ref, ot_ref, k_ref, gp_ref, nw_ref = (next(it) for _ in range(6))
    if has_init:
        c0_ref, n0_ref, m0_ref = next(it), next(it), next(it)
    if write_state and not first_layer:
        next(it), next(it), next(it)
    hm_ref = next(it)
    if write_state:
        cout_ref, nout_ref, mout_ref = next(it), next(it), next(it)
    ct_scr, n_scr, hf_scr, hb_scr = next(it), next(it), next(it), next(it)

    L = CHUNK
    row = _iota((L, L), 0)
    col = _iota((L, L), 1)
    k_scale = HEAD_DIM ** -0.5
    b_idx = pl.program_id(0)
    h_idx = pl.program_id(1)
    n_heads = pl.num_programs(1)

    m_init = []
    for d in range(2):
        if has_init:
            ct_scr[d] = c0_ref[d].T
            n_scr[d] = jnp.broadcast_to(n0_ref[d], n_scr.shape[1:])
            m_init.append(jnp.full((1, 1), m0_ref[(b_idx * 2 + d) * n_heads + h_idx], F32))
        else:
            ct_scr[d] = jnp.zeros(ct_scr.shape[1:], F32)
            n_scr[d] = jnp.zeros(n_scr.shape[1:], F32)
            m_init.append(jnp.zeros((1, 1), F32))

    def step(d, c, m):
        off = _chunk_off(c)
        qt = qt_ref[c]
        vt = vt_ref[c]
        k = k_ref[pl.ds(off, L), :]
        g = gp_ref[c]
        ig = g[d:d + 1, :]
        brow = g[2 + d:3 + d, :]
        blast = brow[:, L - 1:L] if d == 0 else brow[:, 0:1]
        key_term = jnp.broadcast_to(ig - brow, (L, L)).T
        causal = (row <= col) if d == 0 else (row >= col)
        dm = jnp.where(causal, key_term + brow, -jnp.inf)
        inter = brow + m
        m_row = jnp.maximum(jnp.max(dm, axis=0, keepdims=True), inter)
        w = jnp.exp(dm - m_row)
        w_inter = jnp.exp(inter - m_row)
        s = _dot(k, qt) * (w * k_scale)
        ct = ct_scr[d]
        n8 = n_scr[d]
        num = _dot(vt, s.astype(BF16)) + w_inter * _dot(ct.astype(BF16), qt)
        nq = _dot(n8.astype(BF16), qt)[0:1, :]
        den = jnp.sum(s, axis=0, keepdims=True) + w_inter * nq
        h_t = num * (1.0 / jnp.maximum(jnp.abs(den), jnp.exp(-m_row)))
        wj = blast - brow + ig
        m_new = jnp.maximum(blast + m, jnp.max(wj, axis=1, keepdims=True))
        decay = jnp.exp(blast + m - m_new)
        e = jnp.exp(wj - m_new) * k_scale
        ct_scr[d] = decay * ct + _dot((vt.astype(F32) * e).astype(BF16), k)
        n_scr[d] = decay * n8 + _dot3(jnp.broadcast_to(e, (8, L)), k)
        if d == 0:
            hf_scr[c] = h_t
        else:
            hb_scr[c] = h_t
        return m_new

    def body(ci, ms):
        return step(0, ci, ms[0]), step(1, nc - 1 - ci, ms[1])

    m_fin = _scan_loop(nc, body, tuple(m_init))

    def finish(c, carry):
        off = _chunk_off(c)
        h_t = hf_scr[c] + hb_scr[c]
        y = h_t * lax.rsqrt(jnp.mean(h_t * h_t, axis=0, keepdims=True) + EPS) * nw_ref[...]
        o = jax.nn.sigmoid(ot_ref[c].astype(F32)) * y
        hm_ref[pl.ds(off, L), :] = o.T.astype(hm_ref.dtype)
        return carry

    _scan_loop(nc, finish, 0)

    if write_state:
        lsel = (lambda d: (0, d)) if first_layer else (lambda d: (d,))
        for d in range(2):
            cout_ref[lsel(d)] = ct_scr[d].T
            nout_ref[lsel(d)] = n_scr[d][0:1, :]
            mout_ref[lsel(d)] = jnp.broadcast_to(m_fin[d], (1, LANES))
        if first_layer:
            for l in range(1, cout_ref.shape[0]):
                cout_ref[l] = jnp.zeros(cout_ref.shape[1:], F32)
                nout_ref[l] = jnp.zeros(nout_ref.shape[1:], F32)
                mout_ref[l] = jnp.zeros(mout_ref.shape[1:], F32)


def _state_specs(shapes, layer, first_layer):
    specs = []
    for shp in shapes:
        tail = shp[4:]
        zeros = (0,) * len(tail)
        if first_layer:
            specs.append(pl.BlockSpec((None, shp[1], 2, None) + tail, lambda b, h, z=zeros: (b, 0, 0, h) + z))
        else:
            specs.append(pl.BlockSpec((None, None, 2, None) + tail, lambda b, h, z=zeros: (b, layer, 0, h) + z))
    return specs


def _mlstm_call(proj_t, proj_n, gp, nw_b, init, state_bufs, *, n_seq, seq_len, tok_off, n_heads, layer, depth):
    dk = HEAD_DIM
    nc = seq_len // CHUNK
    boff = tok_off // seq_len
    has_init = init is not None
    write_state = not has_init
    first_layer = layer == 0
    args = [proj_t, proj_t, proj_t, proj_n, gp, nw_b]
    specs = [
        pl.BlockSpec((nc, dk, CHUNK), lambda b, h: (b + boff, h, 0)),
        pl.BlockSpec((nc, dk, CHUNK), lambda b, h: (b + boff, n_heads + h, 0)),
        pl.BlockSpec((nc, dk, CHUNK), lambda b, h: (b + boff, 2 * n_heads + h, 0)),
        pl.BlockSpec((seq_len, dk), lambda b, h: (b + boff, h)),
        pl.BlockSpec((None, nc, 4, CHUNK), lambda b, h: (h, b + boff, 0, 0)),
        pl.BlockSpec((dk, LANES), lambda b, h: (h, 0)),
    ]
    aliases = {}
    if has_init:
        c0, n0, m0 = init
        args += [c0, n0[:, layer][:, :, :, None, :], m0[:, layer].reshape(-1)]
        specs += [
            pl.BlockSpec((None, None, 2, None, dk, dk), lambda b, h: (b, layer, 0, h, 0, 0)),
            pl.BlockSpec((None, 2, None, 1, dk), lambda b, h: (b, 0, h, 0, 0)),
            pl.BlockSpec(memory_space=pltpu.SMEM),
        ]
    out_shape = [jax.ShapeDtypeStruct((n_seq * seq_len, n_heads * dk), BF16)]
    out_specs = [pl.BlockSpec((seq_len, dk), lambda b, h: (b, h))]
    if write_state:
        shapes = [(n_seq, depth, 2, n_heads, dk, dk), (n_seq, depth, 2, n_heads, 1, dk),
                  (n_seq, depth, 2, n_heads, 1, LANES)]
        if not first_layer:
            for k_, buf in enumerate(state_bufs):
                aliases[len(args)] = 1 + k_
                args.append(buf)
                specs.append(pl.BlockSpec(memory_space=pl.ANY))
        out_shape += [jax.ShapeDtypeStruct(s, F32) for s in shapes]
        out_specs += _state_specs(shapes, layer, first_layer)
    kern = functools.partial(_mlstm_kernel, nc=nc, has_init=has_init, first_layer=first_layer,
                             write_state=write_state)
    return pl.pallas_call(
        kern, grid=(n_seq, n_heads), in_specs=specs, out_specs=out_specs, out_shape=out_shape,
        input_output_aliases=aliases,
        scratch_shapes=[pltpu.VMEM((2, dk, dk), F32), pltpu.VMEM((2, 8, dk), F32),
                        pltpu.VMEM((nc, dk, CHUNK), F32), pltpu.VMEM((nc, dk, CHUNK), F32)],
        compiler_params=_cparams(("parallel", "parallel"), 40), name="mlstm_mixer",
    )(*args)


def _ret_kernel(*refs, nc, has_init, first_layer, write_state):
    it = iter(refs)
    dl_ref, qt_ref, vt_ref, gt_ref, k_ref, nw_ref = (next(it) for _ in range(6))
    if has_init:
        s0_ref = next(it)
    if write_state and not first_layer:
        next(it)
    hr_ref = next(it)
    if write_state:
        sout_ref = next(it)
    st_scr, of_scr, ob_scr = next(it), next(it), next(it)

    L = CHUNK
    rowi = _iota((L, L), 0)
    coli = _iota((L, L), 1)
    k_scale = HEAD_DIM ** -0.5
    h_idx = pl.program_id(1)
    n_heads = pl.num_programs(1)

    consts = []
    for d in range(2):
        lg = _log_sigmoid(jnp.full((1, 1), dl_ref[d * n_heads + h_idx], F32))
        rel = (coli - rowi if d == 0 else rowi - coli).astype(F32)
        dmat_t = jnp.where(rel >= 0, jnp.exp(jnp.maximum(rel, 0.0) * lg), 0.0) * k_scale
        pos = _iota((1, L), 1).astype(F32)
        if d == 1:
            pos = (L - 1.0) - pos
        q_decay = jnp.exp((pos + 1.0) * lg)
        k_decay = jnp.exp((L - 1.0 - pos) * lg) * k_scale
        chunk_decay = jnp.exp(float(L) * lg)
        consts.append((dmat_t, q_decay, k_decay, chunk_decay))
        if has_init:
            st_scr[d] = s0_ref[d].T
        else:
            st_scr[d] = jnp.zeros(st_scr.shape[1:], F32)

    def step(d, c):
        dmat_t, q_decay, k_decay, chunk_decay = consts[d]
        off = _chunk_off(c)
        qt = qt_ref[c]
        vt = vt_ref[c]
        k = k_ref[pl.ds(off, L), :]
        sc = _dot(k, qt) * dmat_t
        st = st_scr[d]
        o_t = _dot(vt, sc.astype(BF16)) + q_decay * _dot(st.astype(BF16), qt)
        st_scr[d] = chunk_decay * st + _dot((vt.astype(F32) * k_decay).astype(BF16), k)
        if d == 0:
            of_scr[c] = o_t
        else:
            ob_scr[c] = o_t

    def body(ci, carry):
        step(0, ci)
        step(1, nc - 1 - ci)
        return carry

    _scan_loop(nc, body, 0)

    def finish(c, carry):
        off = _chunk_off(c)
        o_t = of_scr[c] + ob_scr[c]
        y = o_t * lax.rsqrt(jnp.mean(o_t * o_t, axis=0, keepdims=True) + EPS) * nw_ref[...]
        rg = gt_ref[c].astype(F32)
        hr_ref[pl.ds(off, L), :] = (rg * jax.nn.sigmoid(rg) * y).T.astype(hr_ref.dtype)
        return carry

    _scan_loop(nc, finish, 0)

    if write_state:
        for d in range(2):
            if first_layer:
                sout_ref[0, d] = st_scr[d].T
            else:
                sout_ref[d] = st_scr[d].T
        if first_layer:
            for l in range(1, sout_ref.shape[0]):
                sout_ref[l] = jnp.zeros(sout_ref.shape[1:], F32)


def _ret_call(proj_t, proj_n, decay_logit, nw_b, init, state_buf, *, n_seq, seq_len, tok_off, n_heads, layer,
              depth):
    dk = HEAD_DIM
    nc = seq_len // CHUNK
    boff = tok_off // seq_len
    has_init = init is not None
    write_state = not has_init
    first_layer = layer == 0
    args = [decay_logit.reshape(-1), proj_t, proj_t, proj_t, proj_n, nw_b]
    specs = [
        pl.BlockSpec(memory_space=pltpu.SMEM),
        pl.BlockSpec((nc, dk, CHUNK), lambda b, h: (b + boff, 3 * n_heads + h, 0)),
        pl.BlockSpec((nc, dk, CHUNK), lambda b, h: (b + boff, 4 * n_heads + h, 0)),
        pl.BlockSpec((nc, dk, CHUNK), lambda b, h: (b + boff, 5 * n_heads + h, 0)),
        pl.BlockSpec((seq_len, dk), lambda b, h: (b + boff, n_heads + h)),
        pl.BlockSpec((dk, LANES), lambda b, h: (h, 0)),
    ]
    aliases = {}
    if has_init:
        args.append(init)
        specs.append(pl.BlockSpec((None, None, 2, None, dk, dk), lambda b, h: (b, layer, 0, h, 0, 0)))
    out_shape = [jax.ShapeDtypeStruct((n_seq * seq_len, n_heads * dk), BF16)]
    out_specs = [pl.BlockSpec((seq_len, dk), lambda b, h: (b, h))]
    if write_state:
        shapes = [(n_seq, depth, 2, n_heads, dk, dk)]
        if not first_layer:
            aliases[len(args)] = 1
            args.append(state_buf)
            specs.append(pl.BlockSpec(memory_space=pl.ANY))
        out_shape += [jax.ShapeDtypeStruct(s, F32) for s in shapes]
        out_specs += _state_specs(shapes, layer, first_layer)
    kern = functools.partial(_ret_kernel, nc=nc, has_init=has_init, first_layer=first_layer,
                             write_state=write_state)
    return pl.pallas_call(
        kern, grid=(n_seq, n_heads), in_specs=specs, out_specs=out_specs, out_shape=out_shape,
        input_output_aliases=aliases,
        scratch_shapes=[pltpu.VMEM((2, dk, dk), F32), pltpu.VMEM((nc, dk, CHUNK), F32),
                        pltpu.VMEM((nc, dk, CHUNK), F32)],
        compiler_params=_cparams(("parallel", "parallel"), 40), name="retention_mixer",
    )(*args)


def _merge_kernel(hmp_ref, hms_ref, hrp_ref, hrs_ref, ga_ref, gb_ref, x_ref, mod_ref, nw_ref, wa_ref, wb_ref,
                  wo_ref, wr_ref, x1_ref, h2_ref, aff_ref, wab_scr, wbb_scr, wob_scr, *, n_prompt_tiles):
    i = pl.program_id(0)

    @pl.when(i == 0)
    def _():
        wab_scr[...] = wa_ref[...].astype(BF16)
        wbb_scr[...] = wb_ref[...].astype(BF16)
        wob_scr[...] = wo_ref[...].astype(BF16)

    is_prompt = i < n_prompt_tiles
    hm = jnp.where(is_prompt, hmp_ref[...], hms_ref[...])
    hr = jnp.where(is_prompt, hrp_ref[...], hrs_ref[...])
    ya = _dot(hm, wab_scr[...])
    yb = _dot(hr, wbb_scr[...])
    merged = (jax.nn.sigmoid(ga_ref[...].astype(F32)) * ya + jax.nn.sigmoid(gb_ref[...].astype(F32)) * yb)
    y = _dot(merged.astype(BF16), wob_scr[...])
    x1 = x_ref[...] + mod_ref[2:3, :] * y
    x1_ref[...] = x1
    h2 = x1 * lax.rsqrt(jnp.mean(x1 * x1, axis=-1, keepdims=True) + EPS) * nw_ref[...]
    h2 = h2 * (1.0 + mod_ref[4:5, :]) + mod_ref[3:4, :]
    h2_ref[...] = h2
    logits = _dot_nt(wr_ref[...].astype(BF16), h2.astype(BF16))
    p = jnp.exp(logits - jnp.max(logits, axis=0, keepdims=True))
    aff_ref[...] = p / jnp.sum(p, axis=0, keepdims=True)


def _merge_call(hm_p, hm_s, hr_p, hr_s, proj_n, x, mod, norm_w, w_a, w_b, w_out, router_wt, layer, *,
                n_prompt, dec_seq, tm=512):
    n, d = x.shape
    ne = router_wt.shape[1]
    npt = n_prompt // tm
    grp = functools.partial(_group_of_tile, tm=tm, n_prompt=n_prompt, dec_seq=dec_seq)
    tile = pl.BlockSpec((tm, d), lambda i: (i, 0))
    tile_p = pl.BlockSpec((tm, d), lambda i: (jnp.minimum(i, npt - 1), 0))
    tile_s = pl.BlockSpec((tm, d), lambda i: (jnp.maximum(i - npt, 0), 0))
    full = pl.BlockSpec((None, d, d), lambda i: (layer, 0, 0))
    return pl.pallas_call(
        functools.partial(_merge_kernel, n_prompt_tiles=npt), grid=(n // tm,),
        in_specs=[
            tile_p, tile_s, tile_p, tile_s,
            pl.BlockSpec((tm, d), lambda i: (i, 2)),
            pl.BlockSpec((tm, d), lambda i: (i, 3)),
            tile,
            pl.BlockSpec((None, 6, d), lambda i: (grp(i), 0, 0)),
            pl.BlockSpec((1, d), lambda i: (0, 0)),
            full, full, full,
            pl.BlockSpec((None, ne, d), lambda i: (layer, 0, 0)),
        ],
        out_specs=[tile, tile, pl.BlockSpec((ne, tm), lambda i: (0, i))],
        out_shape=[jax.ShapeDtypeStruct((n, d), F32), jax.ShapeDtypeStruct((n, d), F32),
                   jax.ShapeDtypeStruct((ne, n), F32)],
        scratch_shapes=[pltpu.VMEM((d, d), BF16)] * 3,
        compiler_params=_cparams(("arbitrary",), 56), name="merge_out_router",
    )(hm_p, hm_s, hr_p, hr_s, proj_n, proj_n, x, mod, norm_w.reshape(1, d), w_a, w_b, w_out, router_wt)


def _route_kernel(a_ref, at_ref, idx_ref, gate_ref, thr_scr, *, cap, n_tok):
    n_sets = a_ref.shape[0]
    nb = n_tok // LANES
    a_all = a_ref[...]

    def as_f32(bits):
        return lax.bitcast_convert_type(bits, F32)

    def count_ge(cand):
        m = jnp.where(a_all >= cand, 1.0, 0.0)
        return jnp.sum(jnp.sum(m, axis=2, keepdims=True), axis=1, keepdims=True)

    def bit_step(k, thr):
        cand = thr | lax.shift_left(jnp.int32(1), 30 - k)
        return jnp.where(count_ge(as_f32(cand)) >= cap, cand, thr)

    thr_scr[...] = lax.fori_loop(0, 31, bit_step, jnp.zeros((n_sets, 1, 1), jnp.int32))

    r128 = _iota((LANES, LANES), 0)
    c128 = _iota((LANES, LANES), 1)
    upper = jnp.where(r128 <= c128, 1.0, 0.0).astype(BF16)
    lower_t = jnp.where(r128 >= c128, 1.0, 0.0).astype(BF16)
    rb = _iota((nb, nb), 0)
    cb = _iota((nb, nb), 1)
    blk_before_rows = jnp.where(cb < rb, 1.0, 0.0).astype(BF16)
    blk_before_cols = jnp.where(rb < cb, 1.0, 0.0).astype(BF16)

    def incl_counts(mask):
        within = _dot(mask.astype(BF16), upper)
        before = _dot(blk_before_rows, within.astype(BF16))[:, LANES - 1:LANES]
        return within + before

    def incl_counts_t(mask_t):
        within = _dot(lower_t, mask_t.astype(BF16))
        before = _dot(within.astype(BF16), blk_before_cols)[LANES - 1:LANES, :]
        return within + before

    slot = _iota((1, cap), 1).astype(F32)
    blk_col = _iota((nb, 1), 0).astype(F32)
    sub_col = _iota((LANES, 1), 0).astype(F32)

    def per_set(s, carry):
        thr_bits = thr_scr[s]
        thr = as_f32(thr_bits)
        nxt = as_f32(thr_bits + 1)
        a = a_ref[s]
        a_t = at_ref[s]
        gt = jnp.where(a >= nxt, 1.0, 0.0)
        eq = jnp.where((a >= thr) & (a < nxt), 1.0, 0.0)
        gt_t = jnp.where(a_t >= nxt, 1.0, 0.0)
        eq_t = jnp.where((a_t >= thr) & (a_t < nxt), 1.0, 0.0)
        n_gt = jnp.sum(jnp.sum(gt, axis=1, keepdims=True), axis=0, keepdims=True)
        need = cap - n_gt
        sel = gt + eq * jnp.where(incl_counts(eq) - eq < need, 1.0, 0.0)
        sel_t = gt_t + eq_t * jnp.where(incl_counts_t(eq_t) - eq_t < need, 1.0, 0.0)
        cnt = incl_counts(sel)
        cnt_t = incl_counts_t(sel_t)
        blk_end = cnt[:, LANES - 1:LANES]
        blk_of_slot = jnp.sum(jnp.where(blk_end <= slot, 1.0, 0.0), axis=0, keepdims=True)
        onehot_blk = jnp.where(blk_col == blk_of_slot, 1.0, 0.0).astype(BF16)
        cnt_rows = _dot3(cnt_t, onehot_blk)
        sub_of_slot = jnp.sum(jnp.where(cnt_rows <= slot, 1.0, 0.0), axis=0, keepdims=True)
        a_rows = _dot3(a_t, onehot_blk)
        gate = jnp.sum(jnp.where(sub_col == sub_of_slot, a_rows, 0.0), axis=0, keepdims=True)
        idx_ref[s] = (blk_of_slot * LANES + sub_of_slot).astype(jnp.int32)
        gate_ref[s] = gate
        return carry

    lax.fori_loop(0, n_sets, per_set, 0)


def _route_call(aff_t, n_pass, cap):
    ne, n = aff_t.shape
    n_tok = n // n_pass
    nb = n_tok // LANES
    a4 = aff_t.reshape(ne, n_pass, nb, LANES).transpose(1, 0, 2, 3).reshape(n_pass * ne, nb, LANES)
    a4_t = a4.transpose(0, 2, 1)
    n_sets = n_pass * ne
    return pl.pallas_call(
        functools.partial(_route_kernel, cap=cap, n_tok=n_tok),
        out_shape=[jax.ShapeDtypeStruct((n_sets, 1, cap), jnp.int32),
                   jax.ShapeDtypeStruct((n_sets, 1, cap), F32)],
        scratch_shapes=[pltpu.VMEM((n_sets, 1, 1), jnp.int32)],
        compiler_params=pltpu.CompilerParams(vmem_limit_bytes=32 * MIB), name="expert_choice_route",
    )(a4, a4_t)


ROW_LOOP_UNROLL = 8


def _moe_kernel(idx_ref, gate_ref, h_hbm, w1_ref, w3_ref, w2_ref, out_hbm,
                xe_scr, xb_scr, ye_a, ye_b, acc_scr, gsem, osem, *, rows, rows_pad):
    e = pl.program_id(0)
    f = pl.program_id(1)
    n_e = pl.num_programs(0)
    n_f = pl.num_programs(1)
    chunk = rows_pad // n_f

    def gather_row(slot, s):
        tok = idx_ref[slot * rows_pad + s]
        pltpu.make_async_copy(h_hbm.at[pl.ds(tok, 1), :], xe_scr.at[pl.ds(s, 1), :], gsem).start()

    def scatter_row(slot, s, ye_ref):
        tok = idx_ref[slot * rows_pad + s]
        g = gate_ref[slot * rows_pad + s]
        acc_scr[pl.ds(tok, 1), :] += ye_ref[pl.ds(s, 1), :] * g

    def wait_gather():
        pltpu.make_async_copy(h_hbm.at[pl.ds(0, rows_pad), :], xe_scr, gsem).wait()

    @pl.when((e == 0) & (f == 0))
    def _():
        acc_scr[...] = jnp.zeros_like(acc_scr)
        ye_a[...] = jnp.zeros_like(ye_a)
        ye_b[...] = jnp.zeros_like(ye_b)

        def issue(s, carry):
            gather_row(1, s)
            return carry
        lax.fori_loop(0, rows_pad, issue, 0, unroll=ROW_LOOP_UNROLL)

    def step(ye_cur, ye_prev):
        @pl.when(f == 0)
        def _():
            wait_gather()
            xb_scr[...] = xe_scr[0:rows, :].astype(BF16)
            ye_cur[0:rows, :] = jnp.zeros((rows, ye_cur.shape[1]), F32)

        xb = xb_scr[...]
        h1 = _dot(xb, w1_ref[...].astype(BF16))
        h3 = _dot(xb, w3_ref[...].astype(BF16))
        he = (h1 * jax.nn.sigmoid(h1) * h3).astype(BF16)
        ye_cur[0:rows, :] += _dot(he, w2_ref[...].astype(BF16))
        base = f * chunk
        for r in range(chunk):
            gather_row(e + 2, base + r)
        for r in range(chunk):
            scatter_row(e, base + r, ye_prev)

        @pl.when((e == n_e - 1) & (f == n_f - 1))
        def _():
            def scatter(s, carry):
                scatter_row(n_e, s, ye_cur)
                return carry
            lax.fori_loop(0, rows, scatter, 0, unroll=ROW_LOOP_UNROLL)
            wait_gather()
            cp = pltpu.make_async_copy(acc_scr, out_hbm, osem)
            cp.start()
            cp.wait()

    @pl.when(e % 2 == 0)
    def _():
        step(ye_a, ye_b)

    @pl.when(e % 2 == 1)
    def _():
        step(ye_b, ye_a)


def _moe_call(idx, gate, h2, w1, w3, w2, layer, tf=256):
    n, d = h2.shape
    ne, rows = idx.shape
    dff = w1.shape[3]
    nf = dff // tf
    rows_pad = -(-rows // (8 * nf)) * 8 * nf
    pad = ((1, 1), (0, rows_pad - rows))
    idx_all = jnp.pad(idx, pad).reshape(-1)
    gate_all = jnp.pad(gate, pad).reshape(-1)
    grid_spec = pltpu.PrefetchScalarGridSpec(
        num_scalar_prefetch=1,
        grid=(ne, nf),
        in_specs=[
            pl.BlockSpec(memory_space=pltpu.SMEM),
            pl.BlockSpec(memory_space=pl.ANY),
            pl.BlockSpec((None, None, d, tf), lambda e, f, idx: (layer, e, 0, f)),
            pl.BlockSpec((None, None, d, tf), lambda e, f, idx: (layer, e, 0, f)),
            pl.BlockSpec((None, None, tf, d), lambda e, f, idx: (layer, e, f, 0)),
        ],
        out_specs=pl.BlockSpec(memory_space=pl.ANY),
        scratch_shapes=[
            pltpu.VMEM((rows_pad, d), F32), pltpu.VMEM((rows, d), BF16),
            pltpu.VMEM((rows_pad, d), F32), pltpu.VMEM((rows_pad, d), F32),
            pltpu.VMEM((n, d), F32), pltpu.SemaphoreType.DMA, pltpu.SemaphoreType.DMA,
        ],
    )
    return pl.pallas_call(
        functools.partial(_moe_kernel, rows=rows, rows_pad=rows_pad), grid_spec=grid_spec,
        out_shape=jax.ShapeDtypeStruct((n, d), F32),
        compiler_params=_cparams(("arbitrary", "arbitrary"), 62), name="expert_ffn",
    )(idx_all, gate_all, h2, w1, w3, w2)


def _rope_tables(t):
    rows = t // GRID_W
    row = jnp.repeat(jnp.arange(rows, dtype=F32), GRID_W)
    colp = jnp.tile(jnp.arange(GRID_W, dtype=F32), rows)
    n_freq = HEAD_DIM // 4
    inv = ROPE_BASE ** (-jnp.arange(n_freq, dtype=F32) / n_freq)
    ang = jnp.concatenate([row[:, None] * inv, colp[:, None] * inv], axis=-1)
    return jnp.cos(ang), jnp.sin(ang)


def kernel(x_prompt, x_sample, c, state_mlstm_C, state_mlstm_n, state_mlstm_m, state_ret_S, c_ctx, w_mod, b_mod,
           norm1_w, norm2_w, w_in, mlstm_if_b, mlstm_norm_w, ret_decay_logit, ret_norm_w, w_branch_a, w_branch_b,
           w_out, router_w, ffn_w1, ffn_w3, ffn_w2, final_norm_w):
    bp, seq, d = x_prompt.shape
    db, dec_seq, _ = x_sample.shape
    depth = w_mod.shape[0]
    hm = mlstm_if_b.shape[-1]
    hr = ret_decay_logit.shape[-1]
    ne = router_w.shape[-1]
    n_prompt = bp * seq
    n_sample = db * dec_seq
    assert n_prompt == n_sample and hm * HEAD_DIM == d and hr * HEAD_DIM == d and hm == hr
    cap = EC_FACTOR * n_prompt // ne
    tm = 512
    tp = 1024
    geo = dict(n_prompt=n_prompt, dec_seq=dec_seq)

    x = jnp.concatenate([x_prompt.reshape(n_prompt, d), x_sample.reshape(n_sample, d)], axis=0)
    cond8 = jnp.concatenate([c_ctx[None, :], c, jnp.zeros((8 - 1 - db, d), F32)], axis=0)
    mod = _mod_call(cond8, w_mod, b_mod)[:, :1 + db].reshape(depth, 1 + db, 6, d)

    n_m = 4 * d
    g0 = n_m + 4 * hm

    def cols(k):
        start = k * d if k < 4 else g0 + (k - 4) * d
        return w_in[:, :, start:start + d]

    mq, mk, mv, mo, rq, rk, rv, rg, ga, gb = (cols(k) for k in range(10))
    w_t = jnp.swapaxes(jnp.concatenate([mq, mv, mo, rq, rv, rg], axis=2), 1, 2).astype(BF16)
    w_n = jnp.concatenate([mk, rk, ga, gb], axis=2).astype(BF16)
    wg_t = jnp.swapaxes(w_in[:, :, n_m:g0], 1, 2)
    gate_bias = jnp.transpose(mlstm_if_b, (0, 2, 1, 3)).reshape(depth, 4 * hm, 1)
    router_wt = jnp.swapaxes(router_w, 1, 2)
    nw_m = jnp.broadcast_to(mlstm_norm_w[:, :, None], (depth, d, LANES))
    nw_r = jnp.broadcast_to(ret_norm_w[:, :, None], (depth, d, LANES))

    cos, sin = _rope_tables(dec_seq)
    cos_t = cos.T.reshape(HEAD_DIM // 2, dec_seq // tp, tp).transpose(1, 0, 2)
    sin_t = sin.T.reshape(HEAD_DIM // 2, dec_seq // tp, tp).transpose(1, 0, 2)

    _, h = _resid_norm_call(x, None, None, mod[0], norm1_w[0], gate_row=0, mod_rows=(0, 1), h_dtype=BF16,
                            write_x=False, **geo)
    m_bufs, s_buf = None, None
    y_prompt = y_sample = None
    for l in range(depth):
        proj_t = _proj_t_call(h, w_t, l, (cos_t, sin_t), 3, tm=tp, **geo)
        proj_n = _proj_call(h, w_n, l, (cos, sin), 1, tm=tp, **geo)
        gp = _gate_call(h, wg_t[l], gate_bias[l], hm, tm=tm)

        mix = dict(n_heads=hm, layer=l, depth=depth)
        hm_p, *m_bufs = _mlstm_call(proj_t, proj_n, gp, nw_m[l], None, m_bufs, n_seq=bp, seq_len=seq, tok_off=0,
                                    **mix)
        (hm_s,) = _mlstm_call(proj_t, proj_n, gp, nw_m[l], (state_mlstm_C, state_mlstm_n, state_mlstm_m), None,
                              n_seq=db, seq_len=dec_seq, tok_off=n_prompt, **mix)
        hr_p, s_buf = _ret_call(proj_t, proj_n, ret_decay_logit[l], nw_r[l], None, s_buf, n_seq=bp, seq_len=seq,
                                tok_off=0, **mix)
        (hr_s,) = _ret_call(proj_t, proj_n, ret_decay_logit[l], nw_r[l], state_ret_S, None, n_seq=db,
                            seq_len=dec_seq, tok_off=n_prompt, **mix)

        x1, h2, aff_t = _merge_call(hm_p, hm_s, hr_p, hr_s, proj_n, x, mod[l], norm2_w[l], w_branch_a, w_branch_b,
                                    w_out, router_wt, l, tm=tm, **geo)
        idx, gate = _route_call(aff_t, 2, cap)
        idx = idx.reshape(2, ne, cap) + (jnp.arange(2, dtype=jnp.int32) * n_prompt)[:, None, None]
        idx = jnp.swapaxes(idx, 0, 1).reshape(ne, 2 * cap)
        gate = jnp.swapaxes(gate.reshape(2, ne, cap), 0, 1).reshape(ne, 2 * cap)
        moe = _moe_call(idx, gate, h2, ffn_w1, ffn_w3, ffn_w2, l)
        if l + 1 < depth:
            x, h = _resid_norm_call(x1, moe, mod[l], mod[l + 1], norm1_w[l + 1], gate_row=5, mod_rows=(0, 1),
                                    h_dtype=BF16, write_x=True, **geo)
        else:
            fin = dict(gate_row=5, mod_rows=None, h_dtype=F32, write_x=False, **geo)
            _, y_prompt = _resid_norm_call(x1, moe, mod[l], None, final_norm_w, row_off=0, n_rows=n_prompt, **fin)
            _, y_sample = _resid_norm_call(x1, moe, mod[l], None, final_norm_w, row_off=n_prompt, n_rows=n_sample,
                                           **fin)

    c_buf, n_buf, m_buf = m_bufs
    return (y_prompt.reshape(bp, seq, d), y_sample.reshape(db, dec_seq, d), c_buf, n_buf[:, :, :, :, 0, :],
            m_buf[:, :, :, :, 0, 0], s_buf)
```

```python
import functools

import numpy as np
import jax
import jax.numpy as jnp
from jax import lax
from jax.experimental import pallas as pl
from jax.experimental.pallas import tpu as pltpu

F32 = jnp.float32
BF16 = jnp.bfloat16

GRID_W = 64
CHUNK = 128
LANES = 128
HEAD_DIM = 256
N_EXPERTS = 16
EC_FACTOR = 2
ROPE_BASE = 10000.0
EPS = 1e-6
V7X_VMEM_BYTES = 64 * 1024 * 1024
MIB = 1024 * 1024


def _cparams(semantics, vmem_mib):
    assert vmem_mib * MIB < V7X_VMEM_BYTES
    return pltpu.CompilerParams(dimension_semantics=semantics, vmem_limit_bytes=vmem_mib * MIB)


def _dot(a, b):
    return jnp.dot(a, b, preferred_element_type=F32)


def _dot_nt(a, b):
    return lax.dot_general(a, b, (((1,), (1,)), ((), ())), preferred_element_type=F32)


def _log_sigmoid(x):
    return -(jnp.maximum(-x, 0.0) + jnp.log1p(jnp.exp(-jnp.abs(x))))


def _split3(x):
    hi = x.astype(BF16)
    r1 = x - hi.astype(F32)
    mid = r1.astype(BF16)
    lo = (r1 - mid.astype(F32)).astype(BF16)
    return hi, mid, lo


def _dot3(x, m):
    hi, mid, lo = _split3(x)
    return _dot(hi, m) + _dot(mid, m) + _dot(lo, m)


def _iota(shape, dim):
    return lax.broadcasted_iota(jnp.int32, shape, dim)


def _mod_kernel(cond_ref, w_ref, b_ref, out_ref):
    c = cond_ref[...]
    s = (c * jax.nn.sigmoid(c)).astype(BF16)
    out_ref[...] = _dot(s, w_ref[...].astype(BF16)) + b_ref[...]


def _mod_call(cond8, w_mod, b_mod):
    depth, d, w6 = w_mod.shape
    tn = 1536
    return pl.pallas_call(
        _mod_kernel,
        grid=(depth, w6 // tn),
        in_specs=[
            pl.BlockSpec((8, d), lambda l, j: (0, 0)),
            pl.BlockSpec((None, d, tn), lambda l, j: (l, 0, j)),
            pl.BlockSpec((None, 1, tn), lambda l, j: (l, 0, j)),
        ],
        out_specs=pl.BlockSpec((None, 8, tn), lambda l, j: (l, 0, j)),
        out_shape=jax.ShapeDtypeStruct((depth, 8, w6), F32),
        compiler_params=_cparams(("parallel", "parallel"), 32),
        name="adaln_mod",
    )(cond8, w_mod, b_mod.reshape(depth, 1, w6))


def _group_of_tile(i, tm, n_prompt, dec_seq):
    return jnp.maximum(i * tm - n_prompt + dec_seq, 0) // dec_seq


def _resid_norm_kernel(*refs, has_delta, has_mod, gate_row, mod_rows, write_x):
    it = iter(refs)
    x_ref = next(it)
    delta_ref = next(it) if has_delta else None
    mod_ref = next(it) if (has_delta or has_mod) else None
    nw_ref = next(it)
    xo_ref = next(it) if write_x else None
    h_ref = next(it)
    x = x_ref[...]
    if has_delta:
        x = x + mod_ref[gate_row:gate_row + 1, :] * delta_ref[...]
    if write_x:
        xo_ref[...] = x
    y = x * lax.rsqrt(jnp.mean(x * x, axis=-1, keepdims=True) + EPS) * nw_ref[...]
    if has_mod:
        sh_row, sc_row = mod_rows
        y = y * (1.0 + mod_ref[sc_row:sc_row + 1, :]) + mod_ref[sh_row:sh_row + 1, :]
    h_ref[...] = y.astype(h_ref.dtype)


def _resid_norm_call(x, delta, mod_gate, mod_norm, norm_w, *, gate_row, mod_rows, h_dtype, write_x,
                     n_prompt, dec_seq, tm=512, row_off=0, n_rows=None):
    n, d = x.shape
    n_rows = n if n_rows is None else n_rows
    toff = row_off // tm
    has_delta = delta is not None
    has_mod = mod_norm is not None
    grp = functools.partial(_group_of_tile, tm=tm, n_prompt=n_prompt, dec_seq=dec_seq)
    tile_in = pl.BlockSpec((tm, d), lambda i: (i + toff, 0))
    tile_out = pl.BlockSpec((tm, d), lambda i: (i, 0))
    args, specs = [x], [tile_in]
    if has_delta:
        args.append(delta)
        specs.append(tile_in)
    if has_delta or has_mod:
        mg = mod_gate if has_delta else mod_norm
        mn = mod_norm if has_mod else mod_gate
        args.append(jnp.concatenate([mg, mn], axis=1))
        specs.append(pl.BlockSpec((None, 12, d), lambda i: (grp(i + toff), 0, 0)))
    args.append(norm_w.reshape(1, d))
    specs.append(pl.BlockSpec((1, d), lambda i: (0, 0)))
    out_shape, out_specs = [], []
    if write_x:
        out_shape.append(jax.ShapeDtypeStruct((n_rows, d), F32))
        out_specs.append(tile_out)
    out_shape.append(jax.ShapeDtypeStruct((n_rows, d), h_dtype))
    out_specs.append(tile_out)
    kern = functools.partial(
        _resid_norm_kernel, has_delta=has_delta, has_mod=has_mod, gate_row=gate_row,
        mod_rows=None if mod_rows is None else (6 + mod_rows[0], 6 + mod_rows[1]), write_x=write_x)
    outs = pl.pallas_call(
        kern, grid=(n_rows // tm,), in_specs=specs, out_specs=out_specs, out_shape=out_shape,
        compiler_params=_cparams(("parallel",), 32), name="resid_norm",
    )(*args)
    return outs if write_x else (None, outs[0])


def _rope_pair(x1, x2, cos, sin):
    return x1 * cos - x2 * sin, x1 * sin + x2 * cos


def _proj_kernel(h_ref, w_ref, cos_ref, sin_ref, out_ref, *, tm, rope_tile, n_prompt, dec_seq):
    j = pl.program_id(0)
    i = pl.program_id(1)
    row0 = pl.multiple_of(i * tm, tm)
    use_rope = (j == rope_tile) & (row0 >= n_prompt)
    pos0 = pl.multiple_of(jnp.where(use_rope, (row0 - n_prompt) % dec_seq, 0), tm)
    cos = cos_ref[pl.ds(pos0, tm), :]
    sin = sin_ref[pl.ds(pos0, tm), :]
    h = h_ref[pl.ds(row0, tm), :]
    half = HEAD_DIM // 2
    for hh in range(w_ref.shape[1] // HEAD_DIM):
        c0 = hh * HEAD_DIM
        acc = _dot(h, w_ref[:, c0:c0 + HEAD_DIM])
        x1, x2 = acc[:, :half], acc[:, half:]
        y1, y2 = _rope_pair(x1, x2, cos, sin)
        out_ref[:, c0:c0 + half] = jnp.where(use_rope, y1, x1).astype(out_ref.dtype)
        out_ref[:, c0 + half:c0 + HEAD_DIM] = jnp.where(use_rope, y2, x2).astype(out_ref.dtype)


def _proj_call(h, w, layer, rope, rope_tile, *, n_prompt, dec_seq, tm=512, tn=1024):
    n, d = h.shape
    width = w.shape[2]
    kern = functools.partial(_proj_kernel, tm=tm, rope_tile=rope_tile, n_prompt=n_prompt, dec_seq=dec_seq)
    return pl.pallas_call(
        kern, grid=(width // tn, n // tm),
        in_specs=[
            pl.BlockSpec((n, d), lambda j, i: (0, 0)),
            pl.BlockSpec((None, d, tn), lambda j, i: (layer, 0, j)),
            pl.BlockSpec(rope[0].shape, lambda j, i: (0, 0)),
            pl.BlockSpec(rope[1].shape, lambda j, i: (0, 0)),
        ],
        out_specs=pl.BlockSpec((tm, tn), lambda j, i: (i, j)),
        out_shape=jax.ShapeDtypeStruct((n, width), BF16),
        compiler_params=_cparams(("parallel", "arbitrary"), 56), name="in_proj",
    )(h, w, *rope)


def _proj_t_kernel(h_ref, w_ref, cos_ref, sin_ref, out_ref, *, tm, rope_tile, n_prompt, dec_seq):
    j = pl.program_id(0)
    i = pl.program_id(1)
    row0 = pl.multiple_of(i * tm, tm)
    use_rope = (j == rope_tile) & (row0 >= n_prompt)
    blk = jnp.where(use_rope, ((row0 - n_prompt) % dec_seq) // tm, 0)
    cos = cos_ref[blk]
    sin = sin_ref[blk]
    h = h_ref[pl.ds(row0, tm), :]
    half = HEAD_DIM // 2
    for hh in range(w_ref.shape[0] // HEAD_DIM):
        r0 = hh * HEAD_DIM
        acc = _dot_nt(w_ref[r0:r0 + HEAD_DIM, :], h)
        x1, x2 = acc[:half, :], acc[half:, :]
        y1, y2 = _rope_pair(x1, x2, cos, sin)
        y1 = jnp.where(use_rope, y1, x1).astype(out_ref.dtype)
        y2 = jnp.where(use_rope, y2, x2).astype(out_ref.dtype)
        for s in range(tm // CHUNK):
            out_ref[s, r0:r0 + half, :] = y1[:, s * CHUNK:(s + 1) * CHUNK]
            out_ref[s, r0 + half:r0 + HEAD_DIM, :] = y2[:, s * CHUNK:(s + 1) * CHUNK]


def _proj_t_call(h, w_t, layer, rope_t, rope_tile, *, n_prompt, dec_seq, tm=512, tn=1024):
    n, d = h.shape
    width = w_t.shape[1]
    kern = functools.partial(_proj_t_kernel, tm=tm, rope_tile=rope_tile, n_prompt=n_prompt, dec_seq=dec_seq)
    return pl.pallas_call(
        kern, grid=(width // tn, n // tm),
        in_specs=[
            pl.BlockSpec((n, d), lambda j, i: (0, 0)),
            pl.BlockSpec((None, tn, d), lambda j, i: (layer, j, 0)),
            pl.BlockSpec(rope_t[0].shape, lambda j, i: (0, 0, 0)),
            pl.BlockSpec(rope_t[1].shape, lambda j, i: (0, 0, 0)),
        ],
        out_specs=pl.BlockSpec((tm // CHUNK, tn, CHUNK), lambda j, i: (i, j, 0)),
        out_shape=jax.ShapeDtypeStruct((n // CHUNK, width, CHUNK), BF16),
        compiler_params=_cparams(("parallel", "arbitrary"), 56), name="in_proj_t",
    )(h, w_t, *rope_t)


def _gate_kernel(h_ref, wg_ref, bias_ref, out_ref, *, tm, n_heads):
    g = _dot_nt(wg_ref[...].astype(BF16), h_ref[...]) + bias_ref[...]
    nd = 2 * n_heads
    ig = g[0:nd, :]
    lf = _log_sigmoid(g[nd:2 * nd, :])
    r = _iota((CHUNK, CHUNK), 0)
    c = _iota((CHUNK, CHUNK), 1)
    upper = jnp.where(r <= c, 1.0, 0.0).astype(BF16)
    lower = jnp.where(r >= c, 1.0, 0.0).astype(BF16)
    is_fwd = _iota((nd, CHUNK), 0) < n_heads
    for s in range(tm // CHUNK):
        sl = slice(s * CHUNK, (s + 1) * CHUNK)
        lf_c = lf[:, sl]
        b = jnp.where(is_fwd, _dot3(lf_c, upper), _dot3(lf_c, lower))
        ig_c = ig[:, sl]
        for hh in range(n_heads):
            out_ref[hh, s, 0:1, :] = ig_c[hh:hh + 1, :]
            out_ref[hh, s, 1:2, :] = ig_c[n_heads + hh:n_heads + hh + 1, :]
            out_ref[hh, s, 2:3, :] = b[hh:hh + 1, :]
            out_ref[hh, s, 3:4, :] = b[n_heads + hh:n_heads + hh + 1, :]


def _gate_call(h, wg_t, bias_col, n_heads, tm=512):
    n, d = h.shape
    ng = 4 * n_heads
    return pl.pallas_call(
        functools.partial(_gate_kernel, tm=tm, n_heads=n_heads),
        grid=(n // tm,),
        in_specs=[
            pl.BlockSpec((tm, d), lambda i: (i, 0)),
            pl.BlockSpec((ng, d), lambda i: (0, 0)),
            pl.BlockSpec((ng, 1), lambda i: (0, 0)),
        ],
        out_specs=pl.BlockSpec((n_heads, tm // CHUNK, 4, CHUNK), lambda i: (0, i, 0, 0)),
        out_shape=jax.ShapeDtypeStruct((n_heads, n // CHUNK, 4, CHUNK), F32),
        compiler_params=_cparams(("parallel",), 32), name="mlstm_gates",
    )(h, wg_t, bias_col)


def _scan_loop(nc, body, init):
    if nc <= 2:
        carry = init
        for ci in range(nc):
            carry = body(ci, carry)
        return carry
    return lax.fori_loop(0, nc, body, init)


GROUP = 2
STRIP = 32


def _group_loop(n_groups, body):
    if n_groups == 1:
        body(0, 0)
    else:
        lax.fori_loop(0, n_groups, body, 0)


def _chunk_off(c):
    return c * CHUNK if isinstance(c, int) else pl.multiple_of(c * CHUNK, CHUNK)


def _mlstm_kernel(*refs, nc, has_init, first_layer, write_state):
    it = iter(refs)
    qt_ref, vt_ref, ot_ref, k_ref, gp_ref, nw_ref = (next(it) for _ in range(6))
    if has_init:
        c0_ref, n0_ref, m0_ref = next(it), next(it), next(it)
    if write_state and not first_layer:
        next(it), next(it), next(it)
    hm_ref = next(it)
    if write_state:
        cout_ref, nout_ref, mout_ref = next(it), next(it), next(it)
    ct_scr, n_scr, part_scr, inc_scr, ninc_scr, rows_scr, cst_scr, nst_scr, coef_scr = (next(it) for _ in range(9))

    L = CHUNK
    row = _iota((L, L), 0)
    col = _iota((L, L), 1)
    k_scale = HEAD_DIM ** -0.5
    b_idx = pl.program_id(0)
    h_idx = pl.program_id(1)
    n_heads = pl.num_programs(1)

    m_init = []
    for d in range(2):
        if has_init:
            ct_scr[d] = c0_ref[d].T
            n_scr[d] = jnp.broadcast_to(n0_ref[d], n_scr.shape[1:])
            m_init.append(jnp.full((1, L), m0_ref[(b_idx * 2 + d) * n_heads + h_idx], F32))
        else:
            ct_scr[d] = jnp.zeros(ct_scr.shape[1:], F32)
            n_scr[d] = jnp.zeros(n_scr.shape[1:], F32)
            m_init.append(jnp.zeros((1, L), F32))

    def gates(d, c):
        g = gp_ref[c]
        ig = g[d:d + 1, :]
        brow = g[2 + d:3 + d, :]
        blast = brow[:, L - 1:L] if d == 0 else brow[:, 0:1]
        return ig, brow, blast

    def local_group(g, carry):
        cs = [g * GROUP + u for u in range(GROUP)]
        pairs = [(u, d) for u in range(GROUP) for d in range(2)]
        qt = [qt_ref[c] for c in cs]
        vt = [vt_ref[c] for c in cs]
        k = [k_ref[pl.ds(_chunk_off(c), L), :] for c in cs]
        qk = [_dot(k[u], qt[u]) for u in range(GROUP)]
        vf = [v.astype(F32) for v in vt]
        gts = {(u, d): gates(d, cs[u]) for u, d in pairs}
        for u, d in pairs:
            c = cs[u]
            ig, brow, blast = gts[u, d]
            wj = blast - brow + ig
            mloc2 = jnp.max(wj, axis=1, keepdims=True)
            e = jnp.exp(wj - mloc2) * k_scale
            inc_scr[d, c] = _dot((vf[u] * e).astype(BF16), k[u])
            ninc_scr[d, c] = _dot3(jnp.broadcast_to(e, (8, L)), k[u])
            rows_scr[d, c, 2:3, :] = jnp.broadcast_to(mloc2, (1, L))
            rows_scr[d, c, 4:5, :] = jnp.broadcast_to(blast, (1, L))
        for u, d in pairs:
            c = cs[u]
            ig, brow, blast = gts[u, d]
            key_term = jnp.broadcast_to(ig - brow, (L, L)).T
            causal = (row <= col) if d == 0 else (row >= col)
            dm = jnp.where(causal, key_term + brow, -jnp.inf)
            mloc = jnp.max(dm, axis=0, keepdims=True)
            s = qk[u] * (jnp.exp(dm - mloc) * k_scale)
            part_scr[d, c] = _dot(vt[u], s.astype(BF16))
            rows_scr[d, c, 0:1, :] = jnp.sum(s, axis=0, keepdims=True)
            rows_scr[d, c, 1:2, :] = mloc
        return carry

    _group_loop(nc // GROUP, local_group)

    def m_step(d, c, m):
        blast = rows_scr[d, c, 4:5, :]
        mloc2 = rows_scr[d, c, 2:3, :]
        rows_scr[d, c, 3:4, :] = m
        m_new = jnp.maximum(blast + m, mloc2)
        decay = jnp.exp(blast + m - m_new)
        w_inc = jnp.exp(mloc2 - m_new)
        coef_scr[d, c, 0:1, :] = jnp.concatenate([decay] * (HEAD_DIM // L), axis=1)
        coef_scr[d, c, 1:2, :] = jnp.concatenate([w_inc] * (HEAD_DIM // L), axis=1)
        return m_new

    def m_body(ci, ms):
        return m_step(0, ci, ms[0]), m_step(1, nc - 1 - ci, ms[1])

    m_fin = _scan_loop(nc, m_body, tuple(m_init))

    def body(ci, carry):
        for d, c in ((0, ci), (1, nc - 1 - ci)):
            decay = coef_scr[d, c, 0:1, :]
            w_inc = coef_scr[d, c, 1:2, :]
            n8 = n_scr[d]
            nst_scr[d, c] = n8
            for r0 in range(0, HEAD_DIM, STRIP):
                ct = ct_scr[d, r0:r0 + STRIP, :]
                cst_scr[d, c, r0:r0 + STRIP, :] = ct.astype(BF16)
                ct_scr[d, r0:r0 + STRIP, :] = decay * ct + w_inc * inc_scr[d, c, r0:r0 + STRIP, :]
            n_scr[d] = decay * n8 + w_inc * ninc_scr[d, c]
        return carry

    _scan_loop(nc, body, 0)

    def finish_group(g, carry):
        cs = [g * GROUP + u for u in range(GROUP)]
        qt = [qt_ref[c] for c in cs]
        cq = [[_dot(cst_scr[d, c], qt[u]) for d in range(2)] for u, c in enumerate(cs)]
        nq = [[_dot(nst_scr[d, c].astype(BF16), qt[u])[0:1, :] for d in range(2)] for u, c in enumerate(cs)]
        for u, c in enumerate(cs):
            coef = []
            for d in range(2):
                _, brow, _ = gates(d, c)
                den_loc = rows_scr[d, c, 0:1, :]
                mloc = rows_scr[d, c, 1:2, :]
                inter = brow + rows_scr[d, c, 3:4, :]
                m_row = jnp.maximum(mloc, inter)
                w_loc = jnp.exp(mloc - m_row)
                w_inter = jnp.exp(inter - m_row)
                den = w_loc * den_loc + w_inter * nq[u][d]
                r_den = 1.0 / jnp.maximum(jnp.abs(den), jnp.exp(-m_row))
                coef.append((w_loc * r_den, w_inter * r_den))
            ssq = jnp.zeros((1, L), F32)
            for r0 in range(0, HEAD_DIM, STRIP):
                h_s = None
                for d in range(2):
                    h_d = coef[d][0] * part_scr[d, c, r0:r0 + STRIP, :] + coef[d][1] * cq[u][d][r0:r0 + STRIP, :]
                    h_s = h_d if h_s is None else h_s + h_d
                part_scr[0, c, r0:r0 + STRIP, :] = h_s
                ssq = ssq + jnp.sum(h_s * h_s, axis=0, keepdims=True)
            r_norm = lax.rsqrt(ssq * (1.0 / HEAD_DIM) + EPS)
            for r0 in range(0, HEAD_DIM, L):
                y = part_scr[0, c, r0:r0 + L, :] * r_norm * nw_ref[r0:r0 + L, :]
                o = jax.nn.sigmoid(ot_ref[c, r0:r0 + L, :].astype(F32)) * y
                hm_ref[pl.ds(_chunk_off(c), L), r0:r0 + L] = o.T.astype(hm_ref.dtype)
        return carry

    _group_loop(nc // GROUP, finish_group)

    if write_state:
        lsel = (lambda d: (0, d)) if first_layer else (lambda d: (d,))
        for d in range(2):
            cout_ref[lsel(d)] = ct_scr[d].T
            nout_ref[lsel(d)] = n_scr[d][0:1, :]
            mout_ref[lsel(d)] = m_fin[d]
        if first_layer:
            for l in range(1, cout_ref.shape[0]):
                cout_ref[l] = jnp.zeros(cout_ref.shape[1:], F32)
                nout_ref[l] = jnp.zeros(nout_ref.shape[1:], F32)
                mout_ref[l] = jnp.zeros(mout_ref.shape[1:], F32)


def _state_specs(shapes, layer, first_layer):
    specs = []
    for shp in shapes:
        tail = shp[4:]
        zeros = (0,) * len(tail)
        if first_layer:
            specs.append(pl.BlockSpec((None, shp[1], 2, None) + tail, lambda b, h, z=zeros: (b, 0, 0, h) + z))
        else:
            specs.append(pl.BlockSpec((None, None, 2, None) + tail, lambda b, h, z=zeros: (b, layer, 0, h) + z))
    return specs


def _mlstm_call(proj_t, proj_n, gp, nw_b, init, state_bufs, *, n_seq, seq_len, tok_off, n_heads, layer, depth):
    dk = HEAD_DIM
    nc = seq_len // CHUNK
    boff = tok_off // seq_len
    has_init = init is not None
    write_state = not has_init
    first_layer = layer == 0
    args = [proj_t, proj_t, proj_t, proj_n, gp, nw_b]
    specs = [
        pl.BlockSpec((nc, dk, CHUNK), lambda b, h: (b + boff, h, 0)),
        pl.BlockSpec((nc, dk, CHUNK), lambda b, h: (b + boff, n_heads + h, 0)),
        pl.BlockSpec((nc, dk, CHUNK), lambda b, h: (b + boff, 2 * n_heads + h, 0)),
        pl.BlockSpec((seq_len, dk), lambda b, h: (b + boff, h)),
        pl.BlockSpec((None, nc, 4, CHUNK), lambda b, h: (h, b + boff, 0, 0)),
        pl.BlockSpec((dk, LANES), lambda b, h: (h, 0)),
    ]
    aliases = {}
    if has_init:
        c0, n0, m0 = init
        args += [c0, n0[:, layer][:, :, :, None, :], m0[:, layer].reshape(-1)]
        specs += [
            pl.BlockSpec((None, None, 2, None, dk, dk), lambda b, h: (b, layer, 0, h, 0, 0)),
            pl.BlockSpec((None, 2, None, 1, dk), lambda b, h: (b, 0, h, 0, 0)),
            pl.BlockSpec(memory_space=pltpu.SMEM),
        ]
    out_shape = [jax.ShapeDtypeStruct((n_seq * seq_len, n_heads * dk), BF16)]
    out_specs = [pl.BlockSpec((seq_len, dk), lambda b, h: (b, h))]
    if write_state:
        shapes = [(n_seq, depth, 2, n_heads, dk, dk), (n_seq, depth, 2, n_heads, 1, dk),
                  (n_seq, depth, 2, n_heads, 1, LANES)]
        if not first_layer:
            for k_, buf in enumerate(state_bufs):
                aliases[len(args)] = 1 + k_
                args.append(buf)
                specs.append(pl.BlockSpec(memory_space=pl.ANY))
        out_shape += [jax.ShapeDtypeStruct(s, F32) for s in shapes]
        out_specs += _state_specs(shapes, layer, first_layer)
    kern = functools.partial(_mlstm_kernel, nc=nc, has_init=has_init, first_layer=first_layer,
                             write_state=write_state)
    return pl.pallas_call(
        kern, grid=(n_seq, n_heads), in_specs=specs, out_specs=out_specs, out_shape=out_shape,
        input_output_aliases=aliases,
        scratch_shapes=[pltpu.VMEM((2, dk, dk), F32), pltpu.VMEM((2, 8, dk), F32),
                        pltpu.VMEM((2, nc, dk, CHUNK), F32), pltpu.VMEM((2, nc, dk, dk), F32),
                        pltpu.VMEM((2, nc, 8, dk), F32), pltpu.VMEM((2, nc, 8, CHUNK), F32),
                        pltpu.VMEM((2, nc, dk, dk), BF16), pltpu.VMEM((2, nc, 8, dk), F32),
                        pltpu.VMEM((2, nc, 8, dk), F32)],
        compiler_params=_cparams(("parallel", "parallel"), 48), name="mlstm_mixer",
    )(*args)


def _ret_kernel(*refs, nc, has_init, first_layer, write_state):
    it = iter(refs)
    dl_ref, qt_ref, vt_ref, gt_ref, k_ref, nw_ref = (next(it) for _ in range(6))
    if has_init:
        s0_ref = next(it)
    if write_state and not first_layer:
        next(it)
    hr_ref = next(it)
    if write_state:
        sout_ref = next(it)
    st_scr, part_scr, inc_scr, sst_scr = next(it), next(it), next(it), next(it)

    L = CHUNK
    rowi = _iota((L, L), 0)
    coli = _iota((L, L), 1)
    k_scale = HEAD_DIM ** -0.5
    h_idx = pl.program_id(1)
    n_heads = pl.num_programs(1)

    consts = []
    for d in range(2):
        lg = _log_sigmoid(jnp.full((1, 1), dl_ref[d * n_heads + h_idx], F32))
        rel = (coli - rowi if d == 0 else rowi - coli).astype(F32)
        dmat_t = jnp.where(rel >= 0, jnp.exp(jnp.maximum(rel, 0.0) * lg), 0.0) * k_scale
        pos = _iota((1, L), 1).astype(F32)
        if d == 1:
            pos = (L - 1.0) - pos
        q_decay = jnp.exp((pos + 1.0) * lg)
        k_decay = jnp.exp((L - 1.0 - pos) * lg) * k_scale
        chunk_decay = jnp.exp(float(L) * lg)
        consts.append((dmat_t, q_decay, k_decay, chunk_decay))
        if has_init:
            st_scr[d] = s0_ref[d].T
        else:
            st_scr[d] = jnp.zeros(st_scr.shape[1:], F32)

    def local_group(g, carry):
        cs = [g * GROUP + u for u in range(GROUP)]
        qt = [qt_ref[c] for c in cs]
        vt = [vt_ref[c] for c in cs]
        k = [k_ref[pl.ds(_chunk_off(c), L), :] for c in cs]
        qk = [_dot(k[u], qt[u]) for u in range(GROUP)]
        for u, c in enumerate(cs):
            vf = vt[u].astype(F32)
            for d in range(2):
                inc_scr[d, c] = _dot((vf * consts[d][2]).astype(BF16), k[u])
        for u, c in enumerate(cs):
            for d in range(2):
                part_scr[d, c] = _dot(vt[u], (qk[u] * consts[d][0]).astype(BF16))
        return carry

    _group_loop(nc // GROUP, local_group)

    def body(ci, carry):
        for d, c in ((0, ci), (1, nc - 1 - ci)):
            for r0 in range(0, HEAD_DIM, STRIP):
                st = st_scr[d, r0:r0 + STRIP, :]
                sst_scr[d, c, r0:r0 + STRIP, :] = st.astype(BF16)
                st_scr[d, r0:r0 + STRIP, :] = consts[d][3] * st + inc_scr[d, c, r0:r0 + STRIP, :]
        return carry

    _scan_loop(nc, body, 0)

    def finish_group(g, carry):
        cs = [g * GROUP + u for u in range(GROUP)]
        qt = [qt_ref[c] for c in cs]
        sq = [[_dot(sst_scr[d, c], qt[u]) for d in range(2)] for u, c in enumerate(cs)]
        for u, c in enumerate(cs):
            ssq = jnp.zeros((1, L), F32)
            for r0 in range(0, HEAD_DIM, STRIP):
                sl = slice(r0, r0 + STRIP)
                o_s = ((part_scr[0, c, sl, :] + part_scr[1, c, sl, :])
                       + (consts[0][1] * sq[u][0][sl, :] + consts[1][1] * sq[u][1][sl, :]))
                part_scr[0, c, sl, :] = o_s
                ssq = ssq + jnp.sum(o_s * o_s, axis=0, keepdims=True)
            r_norm = lax.rsqrt(ssq * (1.0 / HEAD_DIM) + EPS)
            for r0 in range(0, HEAD_DIM, L):
                y = part_scr[0, c, r0:r0 + L, :] * r_norm * nw_ref[r0:r0 + L, :]
                rg = gt_ref[c, r0:r0 + L, :].astype(F32)
                hr_ref[pl.ds(_chunk_off(c), L), r0:r0 + L] = (rg * jax.nn.sigmoid(rg) * y).T.astype(hr_ref.dtype)
        return carry

    _group_loop(nc // GROUP, finish_group)

    if write_state:
        for d in range(2):
            if first_layer:
                sout_ref[0, d] = st_scr[d].T
            else:
                sout_ref[d] = st_scr[d].T
        if first_layer:
            for l in range(1, sout_ref.shape[0]):
                sout_ref[l] = jnp.zeros(sout_ref.shape[1:], F32)


def _ret_call(proj_t, proj_n, decay_logit, nw_b, init, state_buf, *, n_seq, seq_len, tok_off, n_heads, layer,
              depth):
    dk = HEAD_DIM
    nc = seq_len // CHUNK
    boff = tok_off // seq_len
    has_init = init is not None
    write_state = not has_init
    first_layer = layer == 0
    args = [decay_logit.reshape(-1), proj_t, proj_t, proj_t, proj_n, nw_b]
    specs = [
        pl.BlockSpec(memory_space=pltpu.SMEM),
        pl.BlockSpec((nc, dk, CHUNK), lambda b, h: (b + boff, 3 * n_heads + h, 0)),
        pl.BlockSpec((nc, dk, CHUNK), lambda b, h: (b + boff, 4 * n_heads + h, 0)),
        pl.BlockSpec((nc, dk, CHUNK), lambda b, h: (b + boff, 5 * n_heads + h, 0)),
        pl.BlockSpec((seq_len, dk), lambda b, h: (b + boff, n_heads + h)),
        pl.BlockSpec((dk, LANES), lambda b, h: (h, 0)),
    ]
    aliases = {}
    if has_init:
        args.append(init)
        specs.append(pl.BlockSpec((None, None, 2, None, dk, dk), lambda b, h: (b, layer, 0, h, 0, 0)))
    out_shape = [jax.ShapeDtypeStruct((n_seq * seq_len, n_heads * dk), BF16)]
    out_specs = [pl.BlockSpec((seq_len, dk), lambda b, h: (b, h))]
    if write_state:
        shapes = [(n_seq, depth, 2, n_heads, dk, dk)]
        if not first_layer:
            aliases[len(args)] = 1
            args.append(state_buf)
            specs.append(pl.BlockSpec(memory_space=pl.ANY))
        out_shape += [jax.ShapeDtypeStruct(s, F32) for s in shapes]
        out_specs += _state_specs(shapes, layer, first_layer)
    kern = functools.partial(_ret_kernel, nc=nc, has_init=has_init, first_layer=first_layer,
                             write_state=write_state)
    return pl.pallas_call(
        kern, grid=(n_seq, n_heads), in_specs=specs, out_specs=out_specs, out_shape=out_shape,
        input_output_aliases=aliases,
        scratch_shapes=[pltpu.VMEM((2, dk, dk), F32), pltpu.VMEM((2, nc, dk, CHUNK), F32),
                        pltpu.VMEM((2, nc, dk, dk), F32), pltpu.VMEM((2, nc, dk, dk), BF16)],
        compiler_params=_cparams(("parallel", "parallel"), 48), name="retention_mixer",
    )(*args)


def _merge_kernel(hmp_ref, hms_ref, hrp_ref, hrs_ref, ga_ref, gb_ref, x_ref, mod_ref, nw_ref, wa_ref, wb_ref,
                  wo_ref, wr_ref, x1_ref, h2_ref, aff_ref, wab_scr, wbb_scr, wob_scr, *, n_prompt_tiles):
    i = pl.program_id(0)

    @pl.when(i == 0)
    def _():
        wab_scr[...] = wa_ref[...].astype(BF16)
        wbb_scr[...] = wb_ref[...].astype(BF16)
        wob_scr[...] = wo_ref[...].astype(BF16)

    is_prompt = i < n_prompt_tiles
    hm = jnp.where(is_prompt, hmp_ref[...], hms_ref[...])
    hr = jnp.where(is_prompt, hrp_ref[...], hrs_ref[...])
    ya = _dot(hm, wab_scr[...])
    yb = _dot(hr, wbb_scr[...])
    merged = (jax.nn.sigmoid(ga_ref[...].astype(F32)) * ya + jax.nn.sigmoid(gb_ref[...].astype(F32)) * yb)
    y = _dot(merged.astype(BF16), wob_scr[...])
    x1 = x_ref[...] + mod_ref[2:3, :] * y
    x1_ref[...] = x1
    h2 = x1 * lax.rsqrt(jnp.mean(x1 * x1, axis=-1, keepdims=True) + EPS) * nw_ref[...]
    h2 = h2 * (1.0 + mod_ref[4:5, :]) + mod_ref[3:4, :]
    h2_ref[...] = h2
    logits = _dot_nt(wr_ref[...].astype(BF16), h2.astype(BF16))
    p = jnp.exp(logits - jnp.max(logits, axis=0, keepdims=True))
    aff_ref[...] = p / jnp.sum(p, axis=0, keepdims=True)


def _merge_call(hm_p, hm_s, hr_p, hr_s, proj_n, x, mod, norm_w, w_a, w_b, w_out, router_wt, layer, *,
                n_prompt, dec_seq, tm=512):
    n, d = x.shape
    ne = router_wt.shape[1]
    npt = n_prompt // tm
    grp = functools.partial(_group_of_tile, tm=tm, n_prompt=n_prompt, dec_seq=dec_seq)
    tile = pl.BlockSpec((tm, d), lambda i: (i, 0))
    tile_p = pl.BlockSpec((tm, d), lambda i: (jnp.minimum(i, npt - 1), 0))
    tile_s = pl.BlockSpec((tm, d), lambda i: (jnp.maximum(i - npt, 0), 0))
    full = pl.BlockSpec((None, d, d), lambda i: (layer, 0, 0))
    return pl.pallas_call(
        functools.partial(_merge_kernel, n_prompt_tiles=npt), grid=(n // tm,),
        in_specs=[
            tile_p, tile_s, tile_p, tile_s,
            pl.BlockSpec((tm, d), lambda i: (i, 2)),
            pl.BlockSpec((tm, d), lambda i: (i, 3)),
            tile,
            pl.BlockSpec((None, 6, d), lambda i: (grp(i), 0, 0)),
            pl.BlockSpec((1, d), lambda i: (0, 0)),
            full, full, full,
            pl.BlockSpec((None, ne, d), lambda i: (layer, 0, 0)),
        ],
        out_specs=[tile, tile, pl.BlockSpec((ne, tm), lambda i: (0, i))],
        out_shape=[jax.ShapeDtypeStruct((n, d), F32), jax.ShapeDtypeStruct((n, d), F32),
                   jax.ShapeDtypeStruct((ne, n), F32)],
        scratch_shapes=[pltpu.VMEM((d, d), BF16)] * 3,
        compiler_params=_cparams(("arbitrary",), 56), name="merge_out_router",
    )(hm_p, hm_s, hr_p, hr_s, proj_n, proj_n, x, mod, norm_w.reshape(1, d), w_a, w_b, w_out, router_wt)


def _route_kernel(a_ref, at_ref, idx_ref, gate_ref, thr_scr, *, cap, n_tok):
    n_sets = a_ref.shape[0]
    nb = n_tok // LANES
    a_all = a_ref[...]

    def as_f32(bits):
        return lax.bitcast_convert_type(bits, F32)

    def count_ge(cand):
        m = jnp.where(a_all >= cand, 1.0, 0.0)
        return jnp.sum(jnp.sum(m, axis=2, keepdims=True), axis=1, keepdims=True)

    def bit_step(k, thr):
        cand = thr | lax.shift_left(jnp.int32(1), 30 - k)
        return jnp.where(count_ge(as_f32(cand)) >= cap, cand, thr)

    thr_scr[...] = lax.fori_loop(0, 31, bit_step, jnp.zeros((n_sets, 1, 1), jnp.int32))

    r128 = _iota((LANES, LANES), 0)
    c128 = _iota((LANES, LANES), 1)
    upper = jnp.where(r128 <= c128, 1.0, 0.0).astype(BF16)
    lower_t = jnp.where(r128 >= c128, 1.0, 0.0).astype(BF16)
    rb = _iota((nb, nb), 0)
    cb = _iota((nb, nb), 1)
    blk_before_rows = jnp.where(cb < rb, 1.0, 0.0).astype(BF16)
    blk_before_cols = jnp.where(rb < cb, 1.0, 0.0).astype(BF16)

    def incl_counts(mask):
        within = _dot(mask.astype(BF16), upper)
        before = _dot(blk_before_rows, within.astype(BF16))[:, LANES - 1:LANES]
        return within + before

    def incl_counts_t(mask_t):
        within = _dot(lower_t, mask_t.astype(BF16))
        before = _dot(within.astype(BF16), blk_before_cols)[LANES - 1:LANES, :]
        return within + before

    slot = _iota((1, cap), 1).astype(F32)
    blk_col = _iota((nb, 1), 0).astype(F32)
    sub_col = _iota((LANES, 1), 0).astype(F32)

    def per_set(s, carry):
        thr_bits = thr_scr[s]
        thr = as_f32(thr_bits)
        nxt = as_f32(thr_bits + 1)
        a = a_ref[s]
        a_t = at_ref[s]
        gt = jnp.where(a >= nxt, 1.0, 0.0)
        eq = jnp.where((a >= thr) & (a < nxt), 1.0, 0.0)
        gt_t = jnp.where(a_t >= nxt, 1.0, 0.0)
        eq_t = jnp.where((a_t >= thr) & (a_t < nxt), 1.0, 0.0)
        n_gt = jnp.sum(jnp.sum(gt, axis=1, keepdims=True), axis=0, keepdims=True)
        need = cap - n_gt
        sel = gt + eq * jnp.where(incl_counts(eq) - eq < need, 1.0, 0.0)
        sel_t = gt_t + eq_t * jnp.where(incl_counts_t(eq_t) - eq_t < need, 1.0, 0.0)
        cnt = incl_counts(sel)
        cnt_t = incl_counts_t(sel_t)
        blk_end = cnt[:, LANES - 1:LANES]
        blk_of_slot = jnp.sum(jnp.where(blk_end <= slot, 1.0, 0.0), axis=0, keepdims=True)
        onehot_blk = jnp.where(blk_col == blk_of_slot, 1.0, 0.0).astype(BF16)
        cnt_rows = _dot3(cnt_t, onehot_blk)
        sub_of_slot = jnp.sum(jnp.where(cnt_rows <= slot, 1.0, 0.0), axis=0, keepdims=True)
        a_rows = _dot3(a_t, onehot_blk)
        gate = jnp.sum(jnp.where(sub_col == sub_of_slot, a_rows, 0.0), axis=0, keepdims=True)
        idx_ref[s] = (blk_of_slot * LANES + sub_of_slot).astype(jnp.int32)
        gate_ref[s] = gate
        return carry

    lax.fori_loop(0, n_sets, per_set, 0)


def _route_call(aff_t, n_pass, cap):
    ne, n = aff_t.shape
    n_tok = n // n_pass
    nb = n_tok // LANES
    a4 = aff_t.reshape(ne, n_pass, nb, LANES).transpose(1, 0, 2, 3).reshape(n_pass * ne, nb, LANES)
    a4_t = a4.transpose(0, 2, 1)
    n_sets = n_pass * ne
    return pl.pallas_call(
        functools.partial(_route_kernel, cap=cap, n_tok=n_tok),
        out_shape=[jax.ShapeDtypeStruct((n_sets, 1, cap), jnp.int32),
                   jax.ShapeDtypeStruct((n_sets, 1, cap), F32)],
        scratch_shapes=[pltpu.VMEM((n_sets, 1, 1), jnp.int32)],
        compiler_params=pltpu.CompilerParams(vmem_limit_bytes=32 * MIB), name="expert_choice_route",
    )(a4, a4_t)


ROW_LOOP_UNROLL = 8


def _moe_kernel(idx_ref, gate_ref, h_hbm, w1_ref, w3_ref, w2_ref, out_hbm,
                xe_scr, xb_scr, ye_a, ye_b, acc_scr, gsem, osem, *, rows, rows_pad):
    e = pl.program_id(0)
    f = pl.program_id(1)
    n_e = pl.num_programs(0)
    n_f = pl.num_programs(1)
    chunk = rows_pad // n_f

    def gather_row(slot, s):
        tok = idx_ref[slot * rows_pad + s]
        pltpu.make_async_copy(h_hbm.at[pl.ds(tok, 1), :], xe_scr.at[pl.ds(s, 1), :], gsem).start()

    def scatter_row(slot, s, ye_ref):
        tok = idx_ref[slot * rows_pad + s]
        g = gate_ref[slot * rows_pad + s]
        acc_scr[pl.ds(tok, 1), :] += ye_ref[pl.ds(s, 1), :] * g

    def wait_gather():
        pltpu.make_async_copy(h_hbm.at[pl.ds(0, rows_pad), :], xe_scr, gsem).wait()

    @pl.when((e == 0) & (f == 0))
    def _():
        acc_scr[...] = jnp.zeros_like(acc_scr)
        ye_a[...] = jnp.zeros_like(ye_a)
        ye_b[...] = jnp.zeros_like(ye_b)

        def issue(s, carry):
            gather_row(1, s)
            return carry
        lax.fori_loop(0, rows_pad, issue, 0, unroll=ROW_LOOP_UNROLL)

    def step(ye_cur, ye_prev):
        @pl.when(f == 0)
        def _():
            wait_gather()
            xb_scr[...] = xe_scr[0:rows, :].astype(BF16)
            ye_cur[0:rows, :] = jnp.zeros((rows, ye_cur.shape[1]), F32)

        xb = xb_scr[...]
        h1 = _dot(xb, w1_ref[...].astype(BF16))
        h3 = _dot(xb, w3_ref[...].astype(BF16))
        he = (h1 * jax.nn.sigmoid(h1) * h3).astype(BF16)
        ye_cur[0:rows, :] += _dot(he, w2_ref[...].astype(BF16))
        base = f * chunk
        for r in range(chunk):
            gather_row(e + 2, base + r)
        for r in range(chunk):
            scatter_row(e, base + r, ye_prev)

        @pl.when((e == n_e - 1) & (f == n_f - 1))
        def _():
            def scatter(s, carry):
                scatter_row(n_e, s, ye_cur)
                return carry
            lax.fori_loop(0, rows, scatter, 0, unroll=ROW_LOOP_UNROLL)
            wait_gather()
            cp = pltpu.make_async_copy(acc_scr, out_hbm, osem)
            cp.start()
            cp.wait()

    @pl.when(e % 2 == 0)
    def _():
        step(ye_a, ye_b)

    @pl.when(e % 2 == 1)
    def _():
        step(ye_b, ye_a)


def _moe_call(idx, gate, h2, w1, w3, w2, layer, tf=256):
    n, d = h2.shape
    ne, rows = idx.shape
    dff = w1.shape[3]
    nf = dff // tf
    rows_pad = -(-rows // (8 * nf)) * 8 * nf
    pad = ((1, 1), (0, rows_pad - rows))
    idx_all = jnp.pad(idx, pad).reshape(-1)
    gate_all = jnp.pad(gate, pad).reshape(-1)
    grid_spec = pltpu.PrefetchScalarGridSpec(
        num_scalar_prefetch=1,
        grid=(ne, nf),
        in_specs=[
            pl.BlockSpec(memory_space=pltpu.SMEM),
            pl.BlockSpec(memory_space=pl.ANY),
            pl.BlockSpec((None, None, d, tf), lambda e, f, idx: (layer, e, 0, f)),
            pl.BlockSpec((None, None, d, tf), lambda e, f, idx: (layer, e, 0, f)),
            pl.BlockSpec((None, None, tf, d), lambda e, f, idx: (layer, e, f, 0)),
        ],
        out_specs=pl.BlockSpec(memory_space=pl.ANY),
        scratch_shapes=[
            pltpu.VMEM((rows_pad, d), F32), pltpu.VMEM((rows, d), BF16),
            pltpu.VMEM((rows_pad, d), F32), pltpu.VMEM((rows_pad, d), F32),
            pltpu.VMEM((n, d), F32), pltpu.SemaphoreType.DMA, pltpu.SemaphoreType.DMA,
        ],
    )
    return pl.pallas_call(
        functools.partial(_moe_kernel, rows=rows, rows_pad=rows_pad), grid_spec=grid_spec,
        out_shape=jax.ShapeDtypeStruct((n, d), F32),
        compiler_params=_cparams(("arbitrary", "arbitrary"), 62), name="expert_ffn",
    )(idx_all, gate_all, h2, w1, w3, w2)


def _rope_tables(t):
    rows = t // GRID_W
    row = jnp.repeat(jnp.arange(rows, dtype=F32), GRID_W)
    colp = jnp.tile(jnp.arange(GRID_W, dtype=F32), rows)
    n_freq = HEAD_DIM // 4
    inv = ROPE_BASE ** (-jnp.arange(n_freq, dtype=F32) / n_freq)
    ang = jnp.concatenate([row[:, None] * inv, colp[:, None] * inv], axis=-1)
    return jnp.cos(ang), jnp.sin(ang)


def kernel(x_prompt, x_sample, c, state_mlstm_C, state_mlstm_n, state_mlstm_m, state_ret_S, c_ctx, w_mod, b_mod,
           norm1_w, norm2_w, w_in, mlstm_if_b, mlstm_norm_w, ret_decay_logit, ret_norm_w, w_branch_a, w_branch_b,
           w_out, router_w, ffn_w1, ffn_w3, ffn_w2, final_norm_w):
    bp, seq, d = x_prompt.shape
    db, dec_seq, _ = x_sample.shape
    depth = w_mod.shape[0]
    hm = mlstm_if_b.shape[-1]
    hr = ret_decay_logit.shape[-1]
    ne = router_w.shape[-1]
    n_prompt = bp * seq
    n_sample = db * dec_seq
    assert n_prompt == n_sample and hm * HEAD_DIM == d and hr * HEAD_DIM == d and hm == hr
    cap = EC_FACTOR * n_prompt // ne
    tm = 512
    tp = 1024
    geo = dict(n_prompt=n_prompt, dec_seq=dec_seq)

    x = jnp.concatenate([x_prompt.reshape(n_prompt, d), x_sample.reshape(n_sample, d)], axis=0)
    cond8 = jnp.concatenate([c_ctx[None, :], c, jnp.zeros((8 - 1 - db, d), F32)], axis=0)
    mod = _mod_call(cond8, w_mod, b_mod)[:, :1 + db].reshape(depth, 1 + db, 6, d)

    n_m = 4 * d
    g0 = n_m + 4 * hm

    def cols(k):
        start = k * d if k < 4 else g0 + (k - 4) * d
        return w_in[:, :, start:start + d]

    mq, mk, mv, mo, rq, rk, rv, rg, ga, gb = (cols(k) for k in range(10))
    w_t = jnp.swapaxes(jnp.concatenate([mq, mv, mo, rq, rv, rg], axis=2), 1, 2).astype(BF16)
    w_n = jnp.concatenate([mk, rk, ga, gb], axis=2).astype(BF16)
    wg_t = jnp.swapaxes(w_in[:, :, n_m:g0], 1, 2)
    gate_bias = jnp.transpose(mlstm_if_b, (0, 2, 1, 3)).reshape(depth, 4 * hm, 1)
    router_wt = jnp.swapaxes(router_w, 1, 2)
    nw_m = jnp.broadcast_to(mlstm_norm_w[:, :, None], (depth, d, LANES))
    nw_r = jnp.broadcast_to(ret_norm_w[:, :, None], (depth, d, LANES))

    cos, sin = _rope_tables(dec_seq)
    cos_t = cos.T.reshape(HEAD_DIM // 2, dec_seq // tp, tp).transpose(1, 0, 2)
    sin_t = sin.T.reshape(HEAD_DIM // 2, dec_seq // tp, tp).transpose(1, 0, 2)

    _, h = _resid_norm_call(x, None, None, mod[0], norm1_w[0], gate_row=0, mod_rows=(0, 1), h_dtype=BF16,
                            write_x=False, **geo)
    m_bufs, s_buf = None, None
    y_prompt = y_sample = None
    for l in range(depth):
        proj_t = _proj_t_call(h, w_t, l, (cos_t, sin_t), 3, tm=tp, **geo)
        proj_n = _proj_call(h, w_n, l, (cos, sin), 1, tm=tp, **geo)
        gp = _gate_call(h, wg_t[l], gate_bias[l], hm, tm=tm)

        mix = dict(n_heads=hm, layer=l, depth=depth)
        hm_p, *m_bufs = _mlstm_call(proj_t, proj_n, gp, nw_m[l], None, m_bufs, n_seq=bp, seq_len=seq, tok_off=0,
                                    **mix)
        (hm_s,) = _mlstm_call(proj_t, proj_n, gp, nw_m[l], (state_mlstm_C, state_mlstm_n, state_mlstm_m), None,
                              n_seq=db, seq_len=dec_seq, tok_off=n_prompt, **mix)
        hr_p, s_buf = _ret_call(proj_t, proj_n, ret_decay_logit[l], nw_r[l], None, s_buf, n_seq=bp, seq_len=seq,
                                tok_off=0, **mix)
        (hr_s,) = _ret_call(proj_t, proj_n, ret_decay_logit[l], nw_r[l], state_ret_S, None, n_seq=db,
                            seq_len=dec_seq, tok_off=n_prompt, **mix)

        x1, h2, aff_t = _merge_call(hm_p, hm_s, hr_p, hr_s, proj_n, x, mod[l], norm2_w[l], w_branch_a, w_branch_b,
                                    w_out, router_wt, l, tm=tm, **geo)
        idx, gate = _route_call(aff_t, 2, cap)
        idx = idx.reshape(2, ne, cap) + (jnp.arange(2, dtype=jnp.int32) * n_prompt)[:, None, None]
        idx = jnp.swapaxes(idx, 0, 1).reshape(ne, 2 * cap)
        gate = jnp.swapaxes(gate.reshape(2, ne, cap), 0, 1).reshape(ne, 2 * cap)
        moe = _moe_call(idx, gate, h2, ffn_w1, ffn_w3, ffn_w2, l)
        if l + 1 < depth:
            x, h = _resid_norm_call(x1, moe, mod[l], mod[l + 1], norm1_w[l + 1], gate_row=5, mod_rows=(0, 1),
                                    h_dtype=BF16, write_x=True, **geo)
        else:
            fin = dict(gate_row=5, mod_rows=None, h_dtype=F32, write_x=False, **geo)
            _, y_prompt = _resid_norm_call(x1, moe, mod[l], None, final_norm_w, row_off=0, n_rows=n_prompt, **fin)
            _, y_sample = _resid_norm_call(x1, moe, mod[l], None, final_norm_w, row_off=n_prompt, n_rows=n_sample,
                                           **fin)

    c_buf, n_buf, m_buf = m_bufs
    return (y_prompt.reshape(bp, seq, d), y_sample.reshape(db, dec_seq, d), c_buf, n_buf[:, :, :, :, 0, :],
            m_buf[:, :, :, :, 0, 0], s_buf)
```

```python
import functools

import numpy as np
import jax
import jax.numpy as jnp
from jax import lax
from jax.experimental import pallas as pl
from jax.experimental.pallas import tpu as pltpu

F32 = jnp.float32
BF16 = jnp.bfloat16

GRID_W = 64
CHUNK = 128
LANES = 128
HEAD_DIM = 256
N_EXPERTS = 16
EC_FACTOR = 2
ROPE_BASE = 10000.0
EPS = 1e-6
V7X_VMEM_BYTES = 64 * 1024 * 1024
MIB = 1024 * 1024


def _cparams(semantics, vmem_mib):
    assert vmem_mib * MIB < V7X_VMEM_BYTES
    return pltpu.CompilerParams(dimension_semantics=semantics, vmem_limit_bytes=vmem_mib * MIB)


def _dot(a, b):
    return jnp.dot(a, b, preferred_element_type=F32)


def _dot_nt(a, b):
    return lax.dot_general(a, b, (((1,), (1,)), ((), ())), preferred_element_type=F32)


def _log_sigmoid(x):
    return -(jnp.maximum(-x, 0.0) + jnp.log1p(jnp.exp(-jnp.abs(x))))


def _split3(x):
    hi = x.astype(BF16)
    r1 = x - hi.astype(F32)
    mid = r1.astype(BF16)
    lo = (r1 - mid.astype(F32)).astype(BF16)
    return hi, mid, lo


def _dot3(x, m):
    hi, mid, lo = _split3(x)
    return _dot(hi, m) + _dot(mid, m) + _dot(lo, m)


def _iota(shape, dim):
    return lax.broadcasted_iota(jnp.int32, shape, dim)


def _mod_kernel(cond_ref, w_ref, b_ref, out_ref):
    c = cond_ref[...]
    s = (c * jax.nn.sigmoid(c)).astype(BF16)
    out_ref[...] = _dot(s, w_ref[...].astype(BF16)) + b_ref[...]


def _mod_call(cond8, w_mod, b_mod):
    depth, d, w6 = w_mod.shape
    tn = 1536
    return pl.pallas_call(
        _mod_kernel,
        grid=(depth, w6 // tn),
        in_specs=[
            pl.BlockSpec((8, d), lambda l, j: (0, 0)),
            pl.BlockSpec((None, d, tn), lambda l, j: (l, 0, j)),
            pl.BlockSpec((None, 1, tn), lambda l, j: (l, 0, j)),
        ],
        out_specs=pl.BlockSpec((None, 8, tn), lambda l, j: (l, 0, j)),
        out_shape=jax.ShapeDtypeStruct((depth, 8, w6), F32),
        compiler_params=_cparams(("parallel", "parallel"), 32),
        name="adaln_mod",
    )(cond8, w_mod, b_mod.reshape(depth, 1, w6))


def _group_of_tile(i, tm, n_prompt, dec_seq):
    return jnp.maximum(i * tm - n_prompt + dec_seq, 0) // dec_seq


def _resid_norm_kernel(*refs, has_delta, has_mod, gate_row, mod_rows, write_x):
    it = iter(refs)
    x_ref = next(it)
    delta_ref = next(it) if has_delta else None
    mod_ref = next(it) if (has_delta or has_mod) else None
    nw_ref = next(it)
    xo_ref = next(it) if write_x else None
    h_ref = next(it)
    x = x_ref[...]
    if has_delta:
        x = x + mod_ref[gate_row:gate_row + 1, :] * delta_ref[...]
    if write_x:
        xo_ref[...] = x
    y = x * lax.rsqrt(jnp.mean(x * x, axis=-1, keepdims=True) + EPS) * nw_ref[...]
    if has_mod:
        sh_row, sc_row = mod_rows
        y = y * (1.0 + mod_ref[sc_row:sc_row + 1, :]) + mod_ref[sh_row:sh_row + 1, :]
    h_ref[...] = y.astype(h_ref.dtype)


def _resid_norm_call(x, delta, mod_gate, mod_norm, norm_w, *, gate_row, mod_rows, h_dtype, write_x,
                     n_prompt, dec_seq, tm=512, row_off=0, n_rows=None):
    n, d = x.shape
    n_rows = n if n_rows is None else n_rows
    toff = row_off // tm
    has_delta = delta is not None
    has_mod = mod_norm is not None
    grp = functools.partial(_group_of_tile, tm=tm, n_prompt=n_prompt, dec_seq=dec_seq)
    tile_in = pl.BlockSpec((tm, d), lambda i: (i + toff, 0))
    tile_out = pl.BlockSpec((tm, d), lambda i: (i, 0))
    args, specs = [x], [tile_in]
    if has_delta:
        args.append(delta)
        specs.append(tile_in)
    if has_delta or has_mod:
        mg = mod_gate if has_delta else mod_norm
        mn = mod_norm if has_mod else mod_gate
        args.append(jnp.concatenate([mg, mn], axis=1))
        specs.append(pl.BlockSpec((None, 12, d), lambda i: (grp(i + toff), 0, 0)))
    args.append(norm_w.reshape(1, d))
    specs.append(pl.BlockSpec((1, d), lambda i: (0, 0)))
    out_shape, out_specs = [], []
    if write_x:
        out_shape.append(jax.ShapeDtypeStruct((n_rows, d), F32))
        out_specs.append(tile_out)
    out_shape.append(jax.ShapeDtypeStruct((n_rows, d), h_dtype))
    out_specs.append(tile_out)
    kern = functools.partial(
        _resid_norm_kernel, has_delta=has_delta, has_mod=has_mod, gate_row=gate_row,
        mod_rows=None if mod_rows is None else (6 + mod_rows[0], 6 + mod_rows[1]), write_x=write_x)
    outs = pl.pallas_call(
        kern, grid=(n_rows // tm,), in_specs=specs, out_specs=out_specs, out_shape=out_shape,
        compiler_params=_cparams(("parallel",), 32), name="resid_norm",
    )(*args)
    return outs if write_x else (None, outs[0])


def _rope_pair(x1, x2, cos, sin):
    return x1 * cos - x2 * sin, x1 * sin + x2 * cos


def _weight_rows_spec(layer, row_starts, tn, d):
    def index_map(j, i):
        start = sum(jnp.where(j == k, s, 0) for k, s in enumerate(row_starts))
        return layer, pl.multiple_of(start, 8), 0
    assert all(s % 8 == 0 for s in row_starts)
    return pl.BlockSpec((pl.Element(1), pl.Element(tn), pl.Element(d)), index_map)


def _proj_kernel(h_ref, w_ref, cos_ref, sin_ref, out_ref, wb_scr, *, tm, rope_tile, n_prompt, dec_seq):
    j = pl.program_id(0)
    i = pl.program_id(1)

    @pl.when(i == 0)
    def _():
        wb_scr[...] = w_ref[0].astype(BF16)

    row0 = pl.multiple_of(i * tm, tm)
    use_rope = (j == rope_tile) & (row0 >= n_prompt)
    pos0 = pl.multiple_of(jnp.where(use_rope, (row0 - n_prompt) % dec_seq, 0), tm)
    cos = cos_ref[pl.ds(pos0, tm), :]
    sin = sin_ref[pl.ds(pos0, tm), :]
    h = h_ref[pl.ds(row0, tm), :]
    half = HEAD_DIM // 2
    for hh in range(wb_scr.shape[0] // HEAD_DIM):
        c0 = hh * HEAD_DIM
        acc = _dot_nt(h, wb_scr[c0:c0 + HEAD_DIM, :])
        x1, x2 = acc[:, :half], acc[:, half:]
        y1, y2 = _rope_pair(x1, x2, cos, sin)
        out_ref[:, c0:c0 + half] = jnp.where(use_rope, y1, x1).astype(out_ref.dtype)
        out_ref[:, c0 + half:c0 + HEAD_DIM] = jnp.where(use_rope, y2, x2).astype(out_ref.dtype)


def _proj_call(h, w_t, layer, row_starts, rope, rope_tile, *, n_prompt, dec_seq, tm=512, tn=1024):
    n, d = h.shape
    nj = len(row_starts)
    kern = functools.partial(_proj_kernel, tm=tm, rope_tile=rope_tile, n_prompt=n_prompt, dec_seq=dec_seq)
    return pl.pallas_call(
        kern, grid=(nj, n // tm),
        in_specs=[
            pl.BlockSpec((n, d), lambda j, i: (0, 0)),
            _weight_rows_spec(layer, row_starts, tn, d),
            pl.BlockSpec(rope[0].shape, lambda j, i: (0, 0)),
            pl.BlockSpec(rope[1].shape, lambda j, i: (0, 0)),
        ],
        out_specs=pl.BlockSpec((tm, tn), lambda j, i: (i, j)),
        out_shape=jax.ShapeDtypeStruct((n, nj * tn), BF16),
        scratch_shapes=[pltpu.VMEM((tn, d), BF16)],
        compiler_params=_cparams(("parallel", "arbitrary"), 56), name="in_proj",
    )(h, w_t, *rope)


def _proj_t_kernel(h_ref, w_ref, cos_ref, sin_ref, out_ref, wb_scr, *, tm, rope_tile, n_prompt, dec_seq):
    j = pl.program_id(0)
    i = pl.program_id(1)

    @pl.when(i == 0)
    def _():
        wb_scr[...] = w_ref[0].astype(BF16)

    row0 = pl.multiple_of(i * tm, tm)
    use_rope = (j == rope_tile) & (row0 >= n_prompt)
    blk = jnp.where(use_rope, ((row0 - n_prompt) % dec_seq) // tm, 0)
    cos = cos_ref[blk]
    sin = sin_ref[blk]
    h = h_ref[pl.ds(row0, tm), :]
    half = HEAD_DIM // 2
    for hh in range(wb_scr.shape[0] // HEAD_DIM):
        r0 = hh * HEAD_DIM
        acc = _dot_nt(wb_scr[r0:r0 + HEAD_DIM, :], h)
        x1, x2 = acc[:half, :], acc[half:, :]
        y1, y2 = _rope_pair(x1, x2, cos, sin)
        y1 = jnp.where(use_rope, y1, x1).astype(out_ref.dtype)
        y2 = jnp.where(use_rope, y2, x2).astype(out_ref.dtype)
        for s in range(tm // CHUNK):
            out_ref[s, r0:r0 + half, :] = y1[:, s * CHUNK:(s + 1) * CHUNK]
            out_ref[s, r0 + half:r0 + HEAD_DIM, :] = y2[:, s * CHUNK:(s + 1) * CHUNK]


def _proj_t_call(h, w_t, layer, row_starts, rope_t, rope_tile, *, n_prompt, dec_seq, tm=512, tn=1024):
    n, d = h.shape
    nj = len(row_starts)
    kern = functools.partial(_proj_t_kernel, tm=tm, rope_tile=rope_tile, n_prompt=n_prompt, dec_seq=dec_seq)
    return pl.pallas_call(
        kern, grid=(nj, n // tm),
        in_specs=[
            pl.BlockSpec((n, d), lambda j, i: (0, 0)),
            _weight_rows_spec(layer, row_starts, tn, d),
            pl.BlockSpec(rope_t[0].shape, lambda j, i: (0, 0, 0)),
            pl.BlockSpec(rope_t[1].shape, lambda j, i: (0, 0, 0)),
        ],
        out_specs=pl.BlockSpec((tm // CHUNK, tn, CHUNK), lambda j, i: (i, j, 0)),
        out_shape=jax.ShapeDtypeStruct((n // CHUNK, nj * tn, CHUNK), BF16),
        scratch_shapes=[pltpu.VMEM((tn, d), BF16)],
        compiler_params=_cparams(("parallel", "arbitrary"), 56), name="in_proj_t",
    )(h, w_t, *rope_t)


def _gate_kernel(h_ref, wg_ref, bias_ref, out_ref, *, tm, n_heads):
    g = _dot_nt(wg_ref[...].astype(BF16), h_ref[...]) + bias_ref[...]
    nd = 2 * n_heads
    ig = g[0:nd, :]
    lf = _log_sigmoid(g[nd:2 * nd, :])
    r = _iota((CHUNK, CHUNK), 0)
    c = _iota((CHUNK, CHUNK), 1)
    upper = jnp.where(r <= c, 1.0, 0.0).astype(BF16)
    lower = jnp.where(r >= c, 1.0, 0.0).astype(BF16)
    is_fwd = _iota((nd, CHUNK), 0) < n_heads
    for s in range(tm // CHUNK):
        sl = slice(s * CHUNK, (s + 1) * CHUNK)
        lf_c = lf[:, sl]
        b = jnp.where(is_fwd, _dot3(lf_c, upper), _dot3(lf_c, lower))
        ig_c = ig[:, sl]
        for hh in range(n_heads):
            out_ref[hh, s, 0:1, :] = ig_c[hh:hh + 1, :]
            out_ref[hh, s, 1:2, :] = ig_c[n_heads + hh:n_heads + hh + 1, :]
            out_ref[hh, s, 2:3, :] = b[hh:hh + 1, :]
            out_ref[hh, s, 3:4, :] = b[n_heads + hh:n_heads + hh + 1, :]


def _gate_call(h, wg_t, bias_col, n_heads, tm=512):
    n, d = h.shape
    ng = 4 * n_heads
    return pl.pallas_call(
        functools.partial(_gate_kernel, tm=tm, n_heads=n_heads),
        grid=(n // tm,),
        in_specs=[
            pl.BlockSpec((tm, d), lambda i: (i, 0)),
            pl.BlockSpec((ng, d), lambda i: (0, 0)),
            pl.BlockSpec((ng, 1), lambda i: (0, 0)),
        ],
        out_specs=pl.BlockSpec((n_heads, tm // CHUNK, 4, CHUNK), lambda i: (0, i, 0, 0)),
        out_shape=jax.ShapeDtypeStruct((n_heads, n // CHUNK, 4, CHUNK), F32),
        compiler_params=_cparams(("parallel",), 32), name="mlstm_gates",
    )(h, wg_t, bias_col)


def _scan_loop(nc, body, init):
    if nc <= 2:
        carry = init
        for ci in range(nc):
            carry = body(ci, carry)
        return carry
    return lax.fori_loop(0, nc, body, init)


GROUP = 2
STRIP = 32


def _group_loop(n_groups, body):
    if n_groups == 1:
        body(0, 0)
    else:
        lax.fori_loop(0, n_groups, body, 0)


def _chunk_off(c):
    return c * CHUNK if isinstance(c, int) else pl.multiple_of(c * CHUNK, CHUNK)


def _mlstm_kernel(*refs, nc, has_init, first_layer, write_state):
    it = iter(refs)
    qt_ref, vt_ref, ot_ref, k_ref, gp_ref, nw_ref = (next(it) for _ in range(6))
    if has_init:
        c0_ref, n0_ref, m0_ref = next(it), next(it), next(it)
    if write_state and not first_layer:
        next(it), next(it), next(it)
    hm_ref = next(it)
    if write_state:
        cout_ref, nout_ref, mout_ref = next(it), next(it), next(it)
    ct_scr, n_scr, part_scr, inc_scr, ninc_scr, rows_scr, cst_scr, nst_scr, coef_scr = (next(it) for _ in range(9))

    L = CHUNK
    row = _iota((L, L), 0)
    col = _iota((L, L), 1)
    k_scale = HEAD_DIM ** -0.5
    b_idx = pl.program_id(0)
    h_idx = pl.program_id(1)
    n_heads = pl.num_programs(1)

    m_init = []
    for d in range(2):
        if has_init:
            ct_scr[d] = c0_ref[d].T
            n_scr[d] = jnp.broadcast_to(n0_ref[d], n_scr.shape[1:])
            m_init.append(jnp.full((1, L), m0_ref[(b_idx * 2 + d) * n_heads + h_idx], F32))
        else:
            ct_scr[d] = jnp.zeros(ct_scr.shape[1:], F32)
            n_scr[d] = jnp.zeros(n_scr.shape[1:], F32)
            m_init.append(jnp.zeros((1, L), F32))

    def gates(d, c):
        g = gp_ref[c]
        ig = g[d:d + 1, :]
        brow = g[2 + d:3 + d, :]
        blast = brow[:, L - 1:L] if d == 0 else brow[:, 0:1]
        return ig, brow, blast

    def local_group(g, carry):
        cs = [g * GROUP + u for u in range(GROUP)]
        pairs = [(u, d) for u in range(GROUP) for d in range(2)]
        qt = [qt_ref[c] for c in cs]
        vt = [vt_ref[c] for c in cs]
        k = [k_ref[pl.ds(_chunk_off(c), L), :] for c in cs]
        qk = [_dot(k[u], qt[u]) for u in range(GROUP)]
        vf = [v.astype(F32) for v in vt]
        gts = {(u, d): gates(d, cs[u]) for u, d in pairs}
        for u, d in pairs:
            c = cs[u]
            ig, brow, blast = gts[u, d]
            wj = blast - brow + ig
            mloc2 = jnp.max(wj, axis=1, keepdims=True)
            e = jnp.exp(wj - mloc2) * k_scale
            inc_scr[d, c] = _dot((vf[u] * e).astype(BF16), k[u])
            ninc_scr[d, c] = _dot3(jnp.broadcast_to(e, (8, L)), k[u])
            rows_scr[d, c, 2:3, :] = jnp.broadcast_to(mloc2, (1, L))
            rows_scr[d, c, 4:5, :] = jnp.broadcast_to(blast, (1, L))
        for u, d in pairs:
            c = cs[u]
            ig, brow, blast = gts[u, d]
            key_term = jnp.broadcast_to(ig - brow, (L, L)).T
            causal = (row <= col) if d == 0 else (row >= col)
            dm = jnp.where(causal, key_term + brow, -jnp.inf)
            mloc = jnp.max(dm, axis=0, keepdims=True)
            s = qk[u] * (jnp.exp(dm - mloc) * k_scale)
            part_scr[d, c] = _dot(vt[u], s.astype(BF16))
            rows_scr[d, c, 0:1, :] = jnp.sum(s, axis=0, keepdims=True)
            rows_scr[d, c, 1:2, :] = mloc
        return carry

    _group_loop(nc // GROUP, local_group)

    def m_step(d, c, m):
        blast = rows_scr[d, c, 4:5, :]
        mloc2 = rows_scr[d, c, 2:3, :]
        rows_scr[d, c, 3:4, :] = m
        m_new = jnp.maximum(blast + m, mloc2)
        decay = jnp.exp(blast + m - m_new)
        w_inc = jnp.exp(mloc2 - m_new)
        coef_scr[d, c, 0:1, :] = jnp.concatenate([decay] * (HEAD_DIM // L), axis=1)
        coef_scr[d, c, 1:2, :] = jnp.concatenate([w_inc] * (HEAD_DIM // L), axis=1)
        return m_new

    def m_body(ci, ms):
        return m_step(0, ci, ms[0]), m_step(1, nc - 1 - ci, ms[1])

    m_fin = _scan_loop(nc, m_body, tuple(m_init))

    def body(ci, carry):
        for d, c in ((0, ci), (1, nc - 1 - ci)):
            decay = coef_scr[d, c, 0:1, :]
            w_inc = coef_scr[d, c, 1:2, :]
            n8 = n_scr[d]
            nst_scr[d, c] = n8
            for r0 in range(0, HEAD_DIM, STRIP):
                ct = ct_scr[d, r0:r0 + STRIP, :]
                cst_scr[d, c, r0:r0 + STRIP, :] = ct.astype(BF16)
                ct_scr[d, r0:r0 + STRIP, :] = decay * ct + w_inc * inc_scr[d, c, r0:r0 + STRIP, :]
            n_scr[d] = decay * n8 + w_inc * ninc_scr[d, c]
        return carry

    _scan_loop(nc, body, 0)

    def finish_group(g, carry):
        cs = [g * GROUP + u for u in range(GROUP)]
        qt = [qt_ref[c] for c in cs]
        cq = [[_dot(cst_scr[d, c], qt[u]) for d in range(2)] for u, c in enumerate(cs)]
        nq = [[_dot(nst_scr[d, c].astype(BF16), qt[u])[0:1, :] for d in range(2)] for u, c in enumerate(cs)]
        for u, c in enumerate(cs):
            coef = []
            for d in range(2):
                _, brow, _ = gates(d, c)
                den_loc = rows_scr[d, c, 0:1, :]
                mloc = rows_scr[d, c, 1:2, :]
                inter = brow + rows_scr[d, c, 3:4, :]
                m_row = jnp.maximum(mloc, inter)
                w_loc = jnp.exp(mloc - m_row)
                w_inter = jnp.exp(inter - m_row)
                den = w_loc * den_loc + w_inter * nq[u][d]
                r_den = 1.0 / jnp.maximum(jnp.abs(den), jnp.exp(-m_row))
                coef.append((w_loc * r_den, w_inter * r_den))
            ssq = jnp.zeros((1, L), F32)
            for r0 in range(0, HEAD_DIM, STRIP):
                h_s = None
                for d in range(2):
                    h_d = coef[d][0] * part_scr[d, c, r0:r0 + STRIP, :] + coef[d][1] * cq[u][d][r0:r0 + STRIP, :]
                    h_s = h_d if h_s is None else h_s + h_d
                part_scr[0, c, r0:r0 + STRIP, :] = h_s
                ssq = ssq + jnp.sum(h_s * h_s, axis=0, keepdims=True)
            r_norm = lax.rsqrt(ssq * (1.0 / HEAD_DIM) + EPS)
            for r0 in range(0, HEAD_DIM, L):
                y = part_scr[0, c, r0:r0 + L, :] * r_norm * nw_ref[r0:r0 + L, :]
                o = jax.nn.sigmoid(ot_ref[c, r0:r0 + L, :].astype(F32)) * y
                hm_ref[pl.ds(_chunk_off(c), L), r0:r0 + L] = o.T.astype(hm_ref.dtype)
        return carry

    _group_loop(nc // GROUP, finish_group)

    if write_state:
        lsel = (lambda d: (0, d)) if first_layer else (lambda d: (d,))
        for d in range(2):
            cout_ref[lsel(d)] = ct_scr[d].T
            nout_ref[lsel(d)] = n_scr[d][0:1, :]
            mout_ref[lsel(d)] = m_fin[d]
        if first_layer:
            for l in range(1, cout_ref.shape[0]):
                cout_ref[l] = jnp.zeros(cout_ref.shape[1:], F32)
                nout_ref[l] = jnp.zeros(nout_ref.shape[1:], F32)
                mout_ref[l] = jnp.zeros(mout_ref.shape[1:], F32)


def _state_specs(shapes, layer, first_layer):
    specs = []
    for shp in shapes:
        tail = shp[4:]
        zeros = (0,) * len(tail)
        if first_layer:
            specs.append(pl.BlockSpec((None, shp[1], 2, None) + tail, lambda b, h, z=zeros: (b, 0, 0, h) + z))
        else:
            specs.append(pl.BlockSpec((None, None, 2, None) + tail, lambda b, h, z=zeros: (b, layer, 0, h) + z))
    return specs


def _mlstm_call(proj_t, proj_n, gp, nw_b, init, state_bufs, *, n_seq, seq_len, tok_off, n_heads, layer, depth):
    dk = HEAD_DIM
    nc = seq_len // CHUNK
    boff = tok_off // seq_len
    has_init = init is not None
    write_state = not has_init
    first_layer = layer == 0
    args = [proj_t, proj_t, proj_t, proj_n, gp, nw_b]
    specs = [
        pl.BlockSpec((nc, dk, CHUNK), lambda b, h: (b + boff, h, 0)),
        pl.BlockSpec((nc, dk, CHUNK), lambda b, h: (b + boff, n_heads + h, 0)),
        pl.BlockSpec((nc, dk, CHUNK), lambda b, h: (b + boff, 2 * n_heads + h, 0)),
        pl.BlockSpec((seq_len, dk), lambda b, h: (b + boff, h)),
        pl.BlockSpec((None, nc, 4, CHUNK), lambda b, h: (h, b + boff, 0, 0)),
        pl.BlockSpec((dk, LANES), lambda b, h: (h, 0)),
    ]
    aliases = {}
    if has_init:
        c0, n0, m0 = init
        args += [c0, n0[:, layer][:, :, :, None, :], m0[:, layer].reshape(-1)]
        specs += [
            pl.BlockSpec((None, None, 2, None, dk, dk), lambda b, h: (b, layer, 0, h, 0, 0)),
            pl.BlockSpec((None, 2, None, 1, dk), lambda b, h: (b, 0, h, 0, 0)),
            pl.BlockSpec(memory_space=pltpu.SMEM),
        ]
    out_shape = [jax.ShapeDtypeStruct((n_seq * seq_len, n_heads * dk), BF16)]
    out_specs = [pl.BlockSpec((seq_len, dk), lambda b, h: (b, h))]
    if write_state:
        shapes = [(n_seq, depth, 2, n_heads, dk, dk), (n_seq, depth, 2, n_heads, 1, dk),
                  (n_seq, depth, 2, n_heads, 1, LANES)]
        if not first_layer:
            for k_, buf in enumerate(state_bufs):
                aliases[len(args)] = 1 + k_
                args.append(buf)
                specs.append(pl.BlockSpec(memory_space=pl.ANY))
        out_shape += [jax.ShapeDtypeStruct(s, F32) for s in shapes]
        out_specs += _state_specs(shapes, layer, first_layer)
    kern = functools.partial(_mlstm_kernel, nc=nc, has_init=has_init, first_layer=first_layer,
                             write_state=write_state)
    return pl.pallas_call(
        kern, grid=(n_seq, n_heads), in_specs=specs, out_specs=out_specs, out_shape=out_shape,
        input_output_aliases=aliases,
        scratch_shapes=[pltpu.VMEM((2, dk, dk), F32), pltpu.VMEM((2, 8, dk), F32),
                        pltpu.VMEM((2, nc, dk, CHUNK), F32), pltpu.VMEM((2, nc, dk, dk), F32),
                        pltpu.VMEM((2, nc, 8, dk), F32), pltpu.VMEM((2, nc, 8, CHUNK), F32),
                        pltpu.VMEM((2, nc, dk, dk), BF16), pltpu.VMEM((2, nc, 8, dk), F32),
                        pltpu.VMEM((2, nc, 8, dk), F32)],
        compiler_params=_cparams(("parallel", "parallel"), 48), name="mlstm_mixer",
    )(*args)


def _ret_kernel(*refs, nc, has_init, first_layer, write_state):
    it = iter(refs)
    dl_ref, qt_ref, vt_ref, gt_ref, k_ref, nw_ref = (next(it) for _ in range(6))
    if has_init:
        s0_ref = next(it)
    if write_state and not first_layer:
        next(it)
    hr_ref = next(it)
    if write_state:
        sout_ref = next(it)
    st_scr, part_scr, inc_scr, sst_scr = next(it), next(it), next(it), next(it)

    L = CHUNK
    rowi = _iota((L, L), 0)
    coli = _iota((L, L), 1)
    k_scale = HEAD_DIM ** -0.5
    h_idx = pl.program_id(1)
    n_heads = pl.num_programs(1)

    consts = []
    for d in range(2):
        lg = _log_sigmoid(jnp.full((1, 1), dl_ref[d * n_heads + h_idx], F32))
        rel = (coli - rowi if d == 0 else rowi - coli).astype(F32)
        dmat_t = jnp.where(rel >= 0, jnp.exp(jnp.maximum(rel, 0.0) * lg), 0.0) * k_scale
        pos = _iota((1, L), 1).astype(F32)
        if d == 1:
            pos = (L - 1.0) - pos
        q_decay = jnp.exp((pos + 1.0) * lg)
        k_decay = jnp.exp((L - 1.0 - pos) * lg) * k_scale
        chunk_decay = jnp.exp(float(L) * lg)
        consts.append((dmat_t, q_decay, k_decay, chunk_decay))
        if has_init:
            st_scr[d] = s0_ref[d].T
        else:
            st_scr[d] = jnp.zeros(st_scr.shape[1:], F32)

    def local_group(g, carry):
        cs = [g * GROUP + u for u in range(GROUP)]
        qt = [qt_ref[c] for c in cs]
        vt = [vt_ref[c] for c in cs]
        k = [k_ref[pl.ds(_chunk_off(c), L), :] for c in cs]
        qk = [_dot(k[u], qt[u]) for u in range(GROUP)]
        for u, c in enumerate(cs):
            vf = vt[u].astype(F32)
            for d in range(2):
                inc_scr[d, c] = _dot((vf * consts[d][2]).astype(BF16), k[u])
        for u, c in enumerate(cs):
            for d in range(2):
                part_scr[d, c] = _dot(vt[u], (qk[u] * consts[d][0]).astype(BF16))
        return carry

    _group_loop(nc // GROUP, local_group)

    def body(ci, carry):
        for d, c in ((0, ci), (1, nc - 1 - ci)):
            for r0 in range(0, HEAD_DIM, STRIP):
                st = st_scr[d, r0:r0 + STRIP, :]
                sst_scr[d, c, r0:r0 + STRIP, :] = st.astype(BF16)
                st_scr[d, r0:r0 + STRIP, :] = consts[d][3] * st + inc_scr[d, c, r0:r0 + STRIP, :]
        return carry

    _scan_loop(nc, body, 0)

    def finish_group(g, carry):
        cs = [g * GROUP + u for u in range(GROUP)]
        qt = [qt_ref[c] for c in cs]
        sq = [[_dot(sst_scr[d, c], qt[u]) for d in range(2)] for u, c in enumerate(cs)]
        for u, c in enumerate(cs):
            ssq = jnp.zeros((1, L), F32)
            for r0 in range(0, HEAD_DIM, STRIP):
                sl = slice(r0, r0 + STRIP)
                o_s = ((part_scr[0, c, sl, :] + part_scr[1, c, sl, :])
                       + (consts[0][1] * sq[u][0][sl, :] + consts[1][1] * sq[u][1][sl, :]))
                part_scr[0, c, sl, :] = o_s
                ssq = ssq + jnp.sum(o_s * o_s, axis=0, keepdims=True)
            r_norm = lax.rsqrt(ssq * (1.0 / HEAD_DIM) + EPS)
            for r0 in range(0, HEAD_DIM, L):
                y = part_scr[0, c, r0:r0 + L, :] * r_norm * nw_ref[r0:r0 + L, :]
                rg = gt_ref[c, r0:r0 + L, :].astype(F32)
                hr_ref[pl.ds(_chunk_off(c), L), r0:r0 + L] = (rg * jax.nn.sigmoid(rg) * y).T.astype(hr_ref.dtype)
        return carry

    _group_loop(nc // GROUP, finish_group)

    if write_state:
        for d in range(2):
            if first_layer:
                sout_ref[0, d] = st_scr[d].T
            else:
                sout_ref[d] = st_scr[d].T
        if first_layer:
            for l in range(1, sout_ref.shape[0]):
                sout_ref[l] = jnp.zeros(sout_ref.shape[1:], F32)


def _ret_call(proj_t, proj_n, decay_logit, nw_b, init, state_buf, *, n_seq, seq_len, tok_off, n_heads, layer,
              depth):
    dk = HEAD_DIM
    nc = seq_len // CHUNK
    boff = tok_off // seq_len
    has_init = init is not None
    write_state = not has_init
    first_layer = layer == 0
    args = [decay_logit.reshape(-1), proj_t, proj_t, proj_t, proj_n, nw_b]
    specs = [
        pl.BlockSpec(memory_space=pltpu.SMEM),
        pl.BlockSpec((nc, dk, CHUNK), lambda b, h: (b + boff, 3 * n_heads + h, 0)),
        pl.BlockSpec((nc, dk, CHUNK), lambda b, h: (b + boff, 4 * n_heads + h, 0)),
        pl.BlockSpec((nc, dk, CHUNK), lambda b, h: (b + boff, 5 * n_heads + h, 0)),
        pl.BlockSpec((seq_len, dk), lambda b, h: (b + boff, n_heads + h)),
        pl.BlockSpec((dk, LANES), lambda b, h: (h, 0)),
    ]
    aliases = {}
    if has_init:
        args.append(init)
        specs.append(pl.BlockSpec((None, None, 2, None, dk, dk), lambda b, h: (b, layer, 0, h, 0, 0)))
    out_shape = [jax.ShapeDtypeStruct((n_seq * seq_len, n_heads * dk), BF16)]
    out_specs = [pl.BlockSpec((seq_len, dk), lambda b, h: (b, h))]
    if write_state:
        shapes = [(n_seq, depth, 2, n_heads, dk, dk)]
        if not first_layer:
            aliases[len(args)] = 1
            args.append(state_buf)
            specs.append(pl.BlockSpec(memory_space=pl.ANY))
        out_shape += [jax.ShapeDtypeStruct(s, F32) for s in shapes]
        out_specs += _state_specs(shapes, layer, first_layer)
    kern = functools.partial(_ret_kernel, nc=nc, has_init=has_init, first_layer=first_layer,
                             write_state=write_state)
    return pl.pallas_call(
        kern, grid=(n_seq, n_heads), in_specs=specs, out_specs=out_specs, out_shape=out_shape,
        input_output_aliases=aliases,
        scratch_shapes=[pltpu.VMEM((2, dk, dk), F32), pltpu.VMEM((2, nc, dk, CHUNK), F32),
                        pltpu.VMEM((2, nc, dk, dk), F32), pltpu.VMEM((2, nc, dk, dk), BF16)],
        compiler_params=_cparams(("parallel", "parallel"), 48), name="retention_mixer",
    )(*args)


def _merge_kernel(hmp_ref, hms_ref, hrp_ref, hrs_ref, ga_ref, gb_ref, x_ref, mod_ref, nw_ref, wa_ref, wb_ref,
                  wo_ref, wr_ref, x1_ref, h2_ref, aff_ref, wab_scr, wbb_scr, wob_scr, *, n_prompt_tiles):
    i = pl.program_id(0)

    @pl.when(i == 0)
    def _():
        wab_scr[...] = wa_ref[...].astype(BF16)
        wbb_scr[...] = wb_ref[...].astype(BF16)
        wob_scr[...] = wo_ref[...].astype(BF16)

    is_prompt = i < n_prompt_tiles
    hm = jnp.where(is_prompt, hmp_ref[...], hms_ref[...])
    hr = jnp.where(is_prompt, hrp_ref[...], hrs_ref[...])
    ya = _dot(hm, wab_scr[...])
    yb = _dot(hr, wbb_scr[...])
    merged = (jax.nn.sigmoid(ga_ref[...].astype(F32)) * ya + jax.nn.sigmoid(gb_ref[...].astype(F32)) * yb)
    y = _dot(merged.astype(BF16), wob_scr[...])
    x1 = x_ref[...] + mod_ref[2:3, :] * y
    x1_ref[...] = x1
    h2 = x1 * lax.rsqrt(jnp.mean(x1 * x1, axis=-1, keepdims=True) + EPS) * nw_ref[...]
    h2 = h2 * (1.0 + mod_ref[4:5, :]) + mod_ref[3:4, :]
    h2_ref[...] = h2
    logits = _dot_nt(wr_ref[...].astype(BF16), h2.astype(BF16))
    p = jnp.exp(logits - jnp.max(logits, axis=0, keepdims=True))
    aff_ref[...] = p / jnp.sum(p, axis=0, keepdims=True)


def _merge_call(hm_p, hm_s, hr_p, hr_s, proj_n, x, mod, norm_w, w_a, w_b, w_out, router_wt, layer, *,
                n_prompt, dec_seq, tm=512):
    n, d = x.shape
    ne = router_wt.shape[1]
    npt = n_prompt // tm
    grp = functools.partial(_group_of_tile, tm=tm, n_prompt=n_prompt, dec_seq=dec_seq)
    tile = pl.BlockSpec((tm, d), lambda i: (i, 0))
    tile_p = pl.BlockSpec((tm, d), lambda i: (jnp.minimum(i, npt - 1), 0))
    tile_s = pl.BlockSpec((tm, d), lambda i: (jnp.maximum(i - npt, 0), 0))
    full = pl.BlockSpec((None, d, d), lambda i: (layer, 0, 0))
    return pl.pallas_call(
        functools.partial(_merge_kernel, n_prompt_tiles=npt), grid=(n // tm,),
        in_specs=[
            tile_p, tile_s, tile_p, tile_s,
            pl.BlockSpec((tm, d), lambda i: (i, 2)),
            pl.BlockSpec((tm, d), lambda i: (i, 3)),
            tile,
            pl.BlockSpec((None, 6, d), lambda i: (grp(i), 0, 0)),
            pl.BlockSpec((1, d), lambda i: (0, 0)),
            full, full, full,
            pl.BlockSpec((None, ne, d), lambda i: (layer, 0, 0)),
        ],
        out_specs=[tile, tile, pl.BlockSpec((ne, tm), lambda i: (0, i))],
        out_shape=[jax.ShapeDtypeStruct((n, d), F32), jax.ShapeDtypeStruct((n, d), F32),
                   jax.ShapeDtypeStruct((ne, n), F32)],
        scratch_shapes=[pltpu.VMEM((d, d), BF16)] * 3,
        compiler_params=_cparams(("arbitrary",), 56), name="merge_out_router",
    )(hm_p, hm_s, hr_p, hr_s, proj_n, proj_n, x, mod, norm_w.reshape(1, d), w_a, w_b, w_out, router_wt)


def _route_kernel(a_ref, at_ref, idx_ref, gate_ref, thr_scr, *, cap, n_tok):
    n_sets = a_ref.shape[0]
    nb = n_tok // LANES
    a_all = a_ref[...]

    def as_f32(bits):
        return lax.bitcast_convert_type(bits, F32)

    def count_ge(cand):
        m = jnp.where(a_all >= cand, 1.0, 0.0)
        return jnp.sum(jnp.sum(m, axis=2, keepdims=True), axis=1, keepdims=True)

    def bit_step(k, thr):
        cand = thr | lax.shift_left(jnp.int32(1), 30 - k)
        return jnp.where(count_ge(as_f32(cand)) >= cap, cand, thr)

    thr_scr[...] = lax.fori_loop(0, 31, bit_step, jnp.zeros((n_sets, 1, 1), jnp.int32))

    r128 = _iota((LANES, LANES), 0)
    c128 = _iota((LANES, LANES), 1)
    upper = jnp.where(r128 <= c128, 1.0, 0.0).astype(BF16)
    lower_t = jnp.where(r128 >= c128, 1.0, 0.0).astype(BF16)
    rb = _iota((nb, nb), 0)
    cb = _iota((nb, nb), 1)
    blk_before_rows = jnp.where(cb < rb, 1.0, 0.0).astype(BF16)
    blk_before_cols = jnp.where(rb < cb, 1.0, 0.0).astype(BF16)

    def incl_counts(mask):
        within = _dot(mask.astype(BF16), upper)
        before = _dot(blk_before_rows, within.astype(BF16))[:, LANES - 1:LANES]
        return within + before

    def incl_counts_t(mask_t):
        within = _dot(lower_t, mask_t.astype(BF16))
        before = _dot(within.astype(BF16), blk_before_cols)[LANES - 1:LANES, :]
        return within + before

    slot = _iota((1, cap), 1).astype(F32)
    blk_col = _iota((nb, 1), 0).astype(F32)
    sub_col = _iota((LANES, 1), 0).astype(F32)

    def per_set(s, carry):
        thr_bits = thr_scr[s]
        thr = as_f32(thr_bits)
        nxt = as_f32(thr_bits + 1)
        a = a_ref[s]
        a_t = at_ref[s]
        gt = jnp.where(a >= nxt, 1.0, 0.0)
        eq = jnp.where((a >= thr) & (a < nxt), 1.0, 0.0)
        gt_t = jnp.where(a_t >= nxt, 1.0, 0.0)
        eq_t = jnp.where((a_t >= thr) & (a_t < nxt), 1.0, 0.0)
        n_gt = jnp.sum(jnp.sum(gt, axis=1, keepdims=True), axis=0, keepdims=True)
        need = cap - n_gt
        sel = gt + eq * jnp.where(incl_counts(eq) - eq < need, 1.0, 0.0)
        sel_t = gt_t + eq_t * jnp.where(incl_counts_t(eq_t) - eq_t < need, 1.0, 0.0)
        cnt = incl_counts(sel)
        cnt_t = incl_counts_t(sel_t)
        blk_end = cnt[:, LANES - 1:LANES]
        blk_of_slot = jnp.sum(jnp.where(blk_end <= slot, 1.0, 0.0), axis=0, keepdims=True)
        onehot_blk = jnp.where(blk_col == blk_of_slot, 1.0, 0.0).astype(BF16)
        cnt_rows = _dot3(cnt_t, onehot_blk)
        sub_of_slot = jnp.sum(jnp.where(cnt_rows <= slot, 1.0, 0.0), axis=0, keepdims=True)
        a_rows = _dot3(a_t, onehot_blk)
        gate = jnp.sum(jnp.where(sub_col == sub_of_slot, a_rows, 0.0), axis=0, keepdims=True)
        idx_ref[s] = (blk_of_slot * LANES + sub_of_slot).astype(jnp.int32)
        gate_ref[s] = gate
        return carry

    lax.fori_loop(0, n_sets, per_set, 0)


def _route_call(aff_t, n_pass, cap):
    ne, n = aff_t.shape
    n_tok = n // n_pass
    nb = n_tok // LANES
    a4 = aff_t.reshape(ne, n_pass, nb, LANES).transpose(1, 0, 2, 3).reshape(n_pass * ne, nb, LANES)
    a4_t = a4.transpose(0, 2, 1)
    n_sets = n_pass * ne
    return pl.pallas_call(
        functools.partial(_route_kernel, cap=cap, n_tok=n_tok),
        out_shape=[jax.ShapeDtypeStruct((n_sets, 1, cap), jnp.int32),
                   jax.ShapeDtypeStruct((n_sets, 1, cap), F32)],
        scratch_shapes=[pltpu.VMEM((n_sets, 1, 1), jnp.int32)],
        compiler_params=pltpu.CompilerParams(vmem_limit_bytes=32 * MIB), name="expert_choice_route",
    )(a4, a4_t)


ROW_LOOP_UNROLL = 8


def _moe_kernel(idx_ref, gate_ref, h_hbm, w1_ref, w3_ref, w2_ref, out_hbm,
                xe_scr, xb_scr, ye_a, ye_b, acc_scr, gsem, osem, *, rows, rows_pad):
    e = pl.program_id(0)
    f = pl.program_id(1)
    n_e = pl.num_programs(0)
    n_f = pl.num_programs(1)
    chunk = rows_pad // n_f

    def gather_row(slot, s):
        tok = idx_ref[slot * rows_pad + s]
        pltpu.make_async_copy(h_hbm.at[pl.ds(tok, 1), :], xe_scr.at[pl.ds(s, 1), :], gsem).start()

    def scatter_row(slot, s, ye_ref):
        tok = idx_ref[slot * rows_pad + s]
        g = gate_ref[slot * rows_pad + s]
        acc_scr[pl.ds(tok, 1), :] += ye_ref[pl.ds(s, 1), :] * g

    def wait_gather():
        pltpu.make_async_copy(h_hbm.at[pl.ds(0, rows_pad), :], xe_scr, gsem).wait()

    @pl.when((e == 0) & (f == 0))
    def _():
        acc_scr[...] = jnp.zeros_like(acc_scr)
        ye_a[...] = jnp.zeros_like(ye_a)
        ye_b[...] = jnp.zeros_like(ye_b)

        def issue(s, carry):
            gather_row(1, s)
            return carry
        lax.fori_loop(0, rows_pad, issue, 0, unroll=ROW_LOOP_UNROLL)

    def step(ye_cur, ye_prev):
        @pl.when(f == 0)
        def _():
            wait_gather()
            xb_scr[...] = xe_scr[0:rows, :].astype(BF16)
            ye_cur[0:rows, :] = jnp.zeros((rows, ye_cur.shape[1]), F32)

        xb = xb_scr[...]
        h1 = _dot(xb, w1_ref[...].astype(BF16))
        h3 = _dot(xb, w3_ref[...].astype(BF16))
        he = (h1 * jax.nn.sigmoid(h1) * h3).astype(BF16)
        ye_cur[0:rows, :] += _dot(he, w2_ref[...].astype(BF16))
        base = f * chunk
        for r in range(chunk):
            gather_row(e + 2, base + r)
        for r in range(chunk):
            scatter_row(e, base + r, ye_prev)

        @pl.when((e == n_e - 1) & (f == n_f - 1))
        def _():
            def scatter(s, carry):
                scatter_row(n_e, s, ye_cur)
                return carry
            lax.fori_loop(0, rows, scatter, 0, unroll=ROW_LOOP_UNROLL)
            wait_gather()
            cp = pltpu.make_async_copy(acc_scr, out_hbm, osem)
            cp.start()
            cp.wait()

    @pl.when(e % 2 == 0)
    def _():
        step(ye_a, ye_b)

    @pl.when(e % 2 == 1)
    def _():
        step(ye_b, ye_a)


def _moe_call(idx, gate, h2, w1, w3, w2, layer, tf=256):
    n, d = h2.shape
    ne, rows = idx.shape
    dff = w1.shape[3]
    nf = dff // tf
    rows_pad = -(-rows // (8 * nf)) * 8 * nf
    pad = ((1, 1), (0, rows_pad - rows))
    idx_all = jnp.pad(idx, pad).reshape(-1)
    gate_all = jnp.pad(gate, pad).reshape(-1)
    grid_spec = pltpu.PrefetchScalarGridSpec(
        num_scalar_prefetch=1,
        grid=(ne, nf),
        in_specs=[
            pl.BlockSpec(memory_space=pltpu.SMEM),
            pl.BlockSpec(memory_space=pl.ANY),
            pl.BlockSpec((None, None, d, tf), lambda e, f, idx: (layer, e, 0, f)),
            pl.BlockSpec((None, None, d, tf), lambda e, f, idx: (layer, e, 0, f)),
            pl.BlockSpec((None, None, tf, d), lambda e, f, idx: (layer, e, f, 0)),
        ],
        out_specs=pl.BlockSpec(memory_space=pl.ANY),
        scratch_shapes=[
            pltpu.VMEM((rows_pad, d), F32), pltpu.VMEM((rows, d), BF16),
            pltpu.VMEM((rows_pad, d), F32), pltpu.VMEM((rows_pad, d), F32),
            pltpu.VMEM((n, d), F32), pltpu.SemaphoreType.DMA, pltpu.SemaphoreType.DMA,
        ],
    )
    return pl.pallas_call(
        functools.partial(_moe_kernel, rows=rows, rows_pad=rows_pad), grid_spec=grid_spec,
        out_shape=jax.ShapeDtypeStruct((n, d), F32),
        compiler_params=_cparams(("arbitrary", "arbitrary"), 62), name="expert_ffn",
    )(idx_all, gate_all, h2, w1, w3, w2)


def _rope_tables(t):
    rows = t // GRID_W
    row = jnp.repeat(jnp.arange(rows, dtype=F32), GRID_W)
    colp = jnp.tile(jnp.arange(GRID_W, dtype=F32), rows)
    n_freq = HEAD_DIM // 4
    inv = ROPE_BASE ** (-jnp.arange(n_freq, dtype=F32) / n_freq)
    ang = jnp.concatenate([row[:, None] * inv, colp[:, None] * inv], axis=-1)
    return jnp.cos(ang), jnp.sin(ang)


def kernel(x_prompt, x_sample, c, state_mlstm_C, state_mlstm_n, state_mlstm_m, state_ret_S, c_ctx, w_mod, b_mod,
           norm1_w, norm2_w, w_in, mlstm_if_b, mlstm_norm_w, ret_decay_logit, ret_norm_w, w_branch_a, w_branch_b,
           w_out, router_w, ffn_w1, ffn_w3, ffn_w2, final_norm_w):
    bp, seq, d = x_prompt.shape
    db, dec_seq, _ = x_sample.shape
    depth = w_mod.shape[0]
    hm = mlstm_if_b.shape[-1]
    hr = ret_decay_logit.shape[-1]
    ne = router_w.shape[-1]
    n_prompt = bp * seq
    n_sample = db * dec_seq
    assert n_prompt == n_sample and hm * HEAD_DIM == d and hr * HEAD_DIM == d and hm == hr
    cap = EC_FACTOR * n_prompt // ne
    tm = 512
    tp = 1024
    geo = dict(n_prompt=n_prompt, dec_seq=dec_seq)

    x = jnp.concatenate([x_prompt.reshape(n_prompt, d), x_sample.reshape(n_sample, d)], axis=0)
    cond8 = jnp.concatenate([c_ctx[None, :], c, jnp.zeros((8 - 1 - db, d), F32)], axis=0)
    mod = _mod_call(cond8, w_mod, b_mod)[:, :1 + db].reshape(depth, 1 + db, 6, d)

    n_m = 4 * d
    g0 = n_m + 4 * hm

    w_t = jnp.swapaxes(w_in, 1, 2)
    mq, mk, mv, mo = (k * d for k in range(4))
    rq, rk, rv, rg, ga, gb = (g0 + k * d for k in range(6))
    wg_t = w_t[:, n_m:g0]
    gate_bias = jnp.transpose(mlstm_if_b, (0, 2, 1, 3)).reshape(depth, 4 * hm, 1)
    router_wt = jnp.swapaxes(router_w, 1, 2)
    nw_m = jnp.broadcast_to(mlstm_norm_w[:, :, None], (depth, d, LANES))
    nw_r = jnp.broadcast_to(ret_norm_w[:, :, None], (depth, d, LANES))

    cos, sin = _rope_tables(dec_seq)
    cos_t = cos.T.reshape(HEAD_DIM // 2, dec_seq // tp, tp).transpose(1, 0, 2)
    sin_t = sin.T.reshape(HEAD_DIM // 2, dec_seq // tp, tp).transpose(1, 0, 2)

    _, h = _resid_norm_call(x, None, None, mod[0], norm1_w[0], gate_row=0, mod_rows=(0, 1), h_dtype=BF16,
                            write_x=False, **geo)
    m_bufs, s_buf = None, None
    y_prompt = y_sample = None
    for l in range(depth):
        proj_t = _proj_t_call(h, w_t, l, (mq, mv, mo, rq, rv, rg), (cos_t, sin_t), 3, tm=tp, **geo)
        proj_n = _proj_call(h, w_t, l, (mk, rk, ga, gb), (cos, sin), 1, tm=tp, **geo)
        gp = _gate_call(h, wg_t[l], gate_bias[l], hm, tm=tm)

        mix = dict(n_heads=hm, layer=l, depth=depth)
        hm_p, *m_bufs = _mlstm_call(proj_t, proj_n, gp, nw_m[l], None, m_bufs, n_seq=bp, seq_len=seq, tok_off=0,
                                    **mix)
        (hm_s,) = _mlstm_call(proj_t, proj_n, gp, nw_m[l], (state_mlstm_C, state_mlstm_n, state_mlstm_m), None,
                              n_seq=db, seq_len=dec_seq, tok_off=n_prompt, **mix)
        hr_p, s_buf = _ret_call(proj_t, proj_n, ret_decay_logit[l], nw_r[l], None, s_buf, n_seq=bp, seq_len=seq,
                                tok_off=0, **mix)
        (hr_s,) = _ret_call(proj_t, proj_n, ret_decay_logit[l], nw_r[l], state_ret_S, None, n_seq=db,
                            seq_len=dec_seq, tok_off=n_prompt, **mix)

        x1, h2, aff_t = _merge_call(hm_p, hm_s, hr_p, hr_s, proj_n, x, mod[l], norm2_w[l], w_branch_a, w_branch_b,
                                    w_out, router_wt, l, tm=tm, **geo)
        idx, gate = _route_call(aff_t, 2, cap)
        idx = idx.reshape(2, ne, cap) + (jnp.arange(2, dtype=jnp.int32) * n_prompt)[:, None, None]
        idx = jnp.swapaxes(idx, 0, 1).reshape(ne, 2 * cap)
        gate = jnp.swapaxes(gate.reshape(2, ne, cap), 0, 1).reshape(ne, 2 * cap)
        moe = _moe_call(idx, gate, h2, ffn_w1, ffn_w3, ffn_w2, l)
        if l + 1 < depth:
            x, h = _resid_norm_call(x1, moe, mod[l], mod[l + 1], norm1_w[l + 1], gate_row=5, mod_rows=(0, 1),
                                    h_dtype=BF16, write_x=True, **geo)
        else:
            fin = dict(gate_row=5, mod_rows=None, h_dtype=F32, write_x=False, **geo)
            _, y_prompt = _resid_norm_call(x1, moe, mod[l], None, final_norm_w, row_off=0, n_rows=n_prompt, **fin)
            _, y_sample = _resid_norm_call(x1, moe, mod[l], None, final_norm_w, row_off=n_prompt, n_rows=n_sample,
                                           **fin)

    c_buf, n_buf, m_buf = m_bufs
    return (y_prompt.reshape(bp, seq, d), y_sample.reshape(db, dec_seq, d), c_buf, n_buf[:, :, :, :, 0, :],
            m_buf[:, :, :, :, 0, 0], s_buf)
```

```python
import functools

import numpy as np
import jax
import jax.numpy as jnp
from jax import lax
from jax.experimental import pallas as pl
from jax.experimental.pallas import tpu as pltpu

F32 = jnp.float32
BF16 = jnp.bfloat16

GRID_W = 64
CHUNK = 128
LANES = 128
HEAD_DIM = 256
N_EXPERTS = 16
EC_FACTOR = 2
ROPE_BASE = 10000.0
EPS = 1e-6
V7X_VMEM_BYTES = 64 * 1024 * 1024
MIB = 1024 * 1024


def _cparams(semantics, vmem_mib):
    assert vmem_mib * MIB < V7X_VMEM_BYTES
    return pltpu.CompilerParams(dimension_semantics=semantics, vmem_limit_bytes=vmem_mib * MIB)


def _dot(a, b):
    return jnp.dot(a, b, preferred_element_type=F32)


def _dot_nt(a, b):
    return lax.dot_general(a, b, (((1,), (1,)), ((), ())), preferred_element_type=F32)


def _log_sigmoid(x):
    return -(jnp.maximum(-x, 0.0) + jnp.log1p(jnp.exp(-jnp.abs(x))))


def _split3(x):
    hi = x.astype(BF16)
    r1 = x - hi.astype(F32)
    mid = r1.astype(BF16)
    lo = (r1 - mid.astype(F32)).astype(BF16)
    return hi, mid, lo


def _dot3(x, m):
    hi, mid, lo = _split3(x)
    return _dot(hi, m) + _dot(mid, m) + _dot(lo, m)


def _iota(shape, dim):
    return lax.broadcasted_iota(jnp.int32, shape, dim)


def _mod_kernel(cond_ref, w_ref, b_ref, out_ref):
    c = cond_ref[...]
    s = (c * jax.nn.sigmoid(c)).astype(BF16)
    out_ref[...] = _dot(s, w_ref[...].astype(BF16)) + b_ref[...]


def _mod_call(cond8, w_mod, b_mod):
    depth, d, w6 = w_mod.shape
    tn = 1536
    return pl.pallas_call(
        _mod_kernel,
        grid=(depth, w6 // tn),
        in_specs=[
            pl.BlockSpec((8, d), lambda l, j: (0, 0)),
            pl.BlockSpec((None, d, tn), lambda l, j: (l, 0, j)),
            pl.BlockSpec((None, 1, tn), lambda l, j: (l, 0, j)),
        ],
        out_specs=pl.BlockSpec((None, 8, tn), lambda l, j: (l, 0, j)),
        out_shape=jax.ShapeDtypeStruct((depth, 8, w6), F32),
        compiler_params=_cparams(("parallel", "parallel"), 32),
        name="adaln_mod",
    )(cond8, w_mod, b_mod.reshape(depth, 1, w6))


def _group_of_tile(i, tm, n_prompt, dec_seq):
    return jnp.maximum(i * tm - n_prompt + dec_seq, 0) // dec_seq


def _resid_norm_kernel(*refs, has_delta, has_mod, gate_row, mod_rows, write_x):
    it = iter(refs)
    x_ref = next(it)
    delta_ref = next(it) if has_delta else None
    mod_ref = next(it) if (has_delta or has_mod) else None
    nw_ref = next(it)
    xo_ref = next(it) if write_x else None
    h_ref = next(it)
    x = x_ref[...]
    if has_delta:
        x = x + mod_ref[gate_row:gate_row + 1, :] * delta_ref[...]
    if write_x:
        xo_ref[...] = x
    y = x * lax.rsqrt(jnp.mean(x * x, axis=-1, keepdims=True) + EPS) * nw_ref[...]
    if has_mod:
        sh_row, sc_row = mod_rows
        y = y * (1.0 + mod_ref[sc_row:sc_row + 1, :]) + mod_ref[sh_row:sh_row + 1, :]
    h_ref[...] = y.astype(h_ref.dtype)


def _resid_norm_call(x, delta, mod_gate, mod_norm, norm_w, *, gate_row, mod_rows, h_dtype, write_x,
                     n_prompt, dec_seq, tm=512, row_off=0, n_rows=None):
    n, d = x.shape
    n_rows = n if n_rows is None else n_rows
    toff = row_off // tm
    has_delta = delta is not None
    has_mod = mod_norm is not None
    grp = functools.partial(_group_of_tile, tm=tm, n_prompt=n_prompt, dec_seq=dec_seq)
    tile_in = pl.BlockSpec((tm, d), lambda i: (i + toff, 0))
    tile_out = pl.BlockSpec((tm, d), lambda i: (i, 0))
    args, specs = [x], [tile_in]
    if has_delta:
        args.append(delta)
        specs.append(tile_in)
    if has_delta or has_mod:
        mg = mod_gate if has_delta else mod_norm
        mn = mod_norm if has_mod else mod_gate
        args.append(jnp.concatenate([mg, mn], axis=1))
        specs.append(pl.BlockSpec((None, 12, d), lambda i: (grp(i + toff), 0, 0)))
    args.append(norm_w.reshape(1, d))
    specs.append(pl.BlockSpec((1, d), lambda i: (0, 0)))
    out_shape, out_specs = [], []
    if write_x:
        out_shape.append(jax.ShapeDtypeStruct((n_rows, d), F32))
        out_specs.append(tile_out)
    out_shape.append(jax.ShapeDtypeStruct((n_rows, d), h_dtype))
    out_specs.append(tile_out)
    kern = functools.partial(
        _resid_norm_kernel, has_delta=has_delta, has_mod=has_mod, gate_row=gate_row,
        mod_rows=None if mod_rows is None else (6 + mod_rows[0], 6 + mod_rows[1]), write_x=write_x)
    outs = pl.pallas_call(
        kern, grid=(n_rows // tm,), in_specs=specs, out_specs=out_specs, out_shape=out_shape,
        compiler_params=_cparams(("parallel",), 32), name="resid_norm",
    )(*args)
    return outs if write_x else (None, outs[0])


def _rope_pair(x1, x2, cos, sin):
    return x1 * cos - x2 * sin, x1 * sin + x2 * cos


def _weight_rows_spec(layer, row_starts, tn, d):
    def index_map(j, i):
        start = sum(jnp.where(j == k, s, 0) for k, s in enumerate(row_starts))
        return layer, pl.multiple_of(start, 8), 0
    assert all(s % 8 == 0 for s in row_starts)
    return pl.BlockSpec((pl.Element(1), pl.Element(tn), pl.Element(d)), index_map)


def _proj_kernel(h_ref, w_ref, cos_ref, sin_ref, out_ref, wb_scr, *, tm, rope_tile, n_prompt, dec_seq):
    j = pl.program_id(0)
    i = pl.program_id(1)

    @pl.when(i == 0)
    def _():
        wb_scr[...] = w_ref[0].astype(BF16)

    row0 = pl.multiple_of(i * tm, tm)
    use_rope = (j == rope_tile) & (row0 >= n_prompt)
    pos0 = pl.multiple_of(jnp.where(use_rope, (row0 - n_prompt) % dec_seq, 0), tm)
    cos = cos_ref[pl.ds(pos0, tm), :]
    sin = sin_ref[pl.ds(pos0, tm), :]
    h = h_ref[pl.ds(row0, tm), :]
    half = HEAD_DIM // 2
    for hh in range(wb_scr.shape[0] // HEAD_DIM):
        c0 = hh * HEAD_DIM
        acc = _dot_nt(h, wb_scr[c0:c0 + HEAD_DIM, :])
        x1, x2 = acc[:, :half], acc[:, half:]
        y1, y2 = _rope_pair(x1, x2, cos, sin)
        out_ref[:, c0:c0 + half] = jnp.where(use_rope, y1, x1).astype(out_ref.dtype)
        out_ref[:, c0 + half:c0 + HEAD_DIM] = jnp.where(use_rope, y2, x2).astype(out_ref.dtype)


def _proj_call(h, w_t, layer, row_starts, rope, rope_tile, *, n_prompt, dec_seq, tm=512, tn=1024):
    n, d = h.shape
    nj = len(row_starts)
    kern = functools.partial(_proj_kernel, tm=tm, rope_tile=rope_tile, n_prompt=n_prompt, dec_seq=dec_seq)
    return pl.pallas_call(
        kern, grid=(nj, n // tm),
        in_specs=[
            pl.BlockSpec((n, d), lambda j, i: (0, 0)),
            _weight_rows_spec(layer, row_starts, tn, d),
            pl.BlockSpec(rope[0].shape, lambda j, i: (0, 0)),
            pl.BlockSpec(rope[1].shape, lambda j, i: (0, 0)),
        ],
        out_specs=pl.BlockSpec((tm, tn), lambda j, i: (i, j)),
        out_shape=jax.ShapeDtypeStruct((n, nj * tn), BF16),
        scratch_shapes=[pltpu.VMEM((tn, d), BF16)],
        compiler_params=_cparams(("parallel", "arbitrary"), 56), name="in_proj",
    )(h, w_t, *rope)


def _proj_t_kernel(h_ref, w_ref, cos_ref, sin_ref, out_ref, wb_scr, *, tm, rope_tile, n_prompt, dec_seq):
    j = pl.program_id(0)
    i = pl.program_id(1)

    @pl.when(i == 0)
    def _():
        wb_scr[...] = w_ref[0].astype(BF16)

    row0 = pl.multiple_of(i * tm, tm)
    use_rope = (j == rope_tile) & (row0 >= n_prompt)
    blk = jnp.where(use_rope, ((row0 - n_prompt) % dec_seq) // tm, 0)
    cos = cos_ref[blk]
    sin = sin_ref[blk]
    h = h_ref[pl.ds(row0, tm), :]
    half = HEAD_DIM // 2
    for hh in range(wb_scr.shape[0] // HEAD_DIM):
        r0 = hh * HEAD_DIM
        acc = _dot_nt(wb_scr[r0:r0 + HEAD_DIM, :], h)
        x1, x2 = acc[:half, :], acc[half:, :]
        y1, y2 = _rope_pair(x1, x2, cos, sin)
        y1 = jnp.where(use_rope, y1, x1).astype(out_ref.dtype)
        y2 = jnp.where(use_rope, y2, x2).astype(out_ref.dtype)
        for s in range(tm // CHUNK):
            out_ref[s, r0:r0 + half, :] = y1[:, s * CHUNK:(s + 1) * CHUNK]
            out_ref[s, r0 + half:r0 + HEAD_DIM, :] = y2[:, s * CHUNK:(s + 1) * CHUNK]


def _proj_t_call(h, w_t, layer, row_starts, rope_t, rope_tile, *, n_prompt, dec_seq, tm=512, tn=1024):
    n, d = h.shape
    nj = len(row_starts)
    kern = functools.partial(_proj_t_kernel, tm=tm, rope_tile=rope_tile, n_prompt=n_prompt, dec_seq=dec_seq)
    return pl.pallas_call(
        kern, grid=(nj, n // tm),
        in_specs=[
            pl.BlockSpec((n, d), lambda j, i: (0, 0)),
            _weight_rows_spec(layer, row_starts, tn, d),
            pl.BlockSpec(rope_t[0].shape, lambda j, i: (0, 0, 0)),
            pl.BlockSpec(rope_t[1].shape, lambda j, i: (0, 0, 0)),
        ],
        out_specs=pl.BlockSpec((tm // CHUNK, tn, CHUNK), lambda j, i: (i, j, 0)),
        out_shape=jax.ShapeDtypeStruct((n // CHUNK, nj * tn, CHUNK), BF16),
        scratch_shapes=[pltpu.VMEM((tn, d), BF16)],
        compiler_params=_cparams(("parallel", "arbitrary"), 56), name="in_proj_t",
    )(h, w_t, *rope_t)


def _gate_kernel(h_ref, wg_ref, bias_ref, out_ref, *, tm, n_heads):
    g = _dot_nt(wg_ref[...].astype(BF16), h_ref[...]) + bias_ref[...]
    nd = 2 * n_heads
    ig = g[0:nd, :]
    lf = _log_sigmoid(g[nd:2 * nd, :])
    r = _iota((CHUNK, CHUNK), 0)
    c = _iota((CHUNK, CHUNK), 1)
    upper = jnp.where(r <= c, 1.0, 0.0).astype(BF16)
    lower = jnp.where(r >= c, 1.0, 0.0).astype(BF16)
    is_fwd = _iota((nd, CHUNK), 0) < n_heads
    for s in range(tm // CHUNK):
        sl = slice(s * CHUNK, (s + 1) * CHUNK)
        lf_c = lf[:, sl]
        b = jnp.where(is_fwd, _dot3(lf_c, upper), _dot3(lf_c, lower))
        ig_c = ig[:, sl]
        for hh in range(n_heads):
            out_ref[hh, s, 0:1, :] = ig_c[hh:hh + 1, :]
            out_ref[hh, s, 1:2, :] = ig_c[n_heads + hh:n_heads + hh + 1, :]
            out_ref[hh, s, 2:3, :] = b[hh:hh + 1, :]
            out_ref[hh, s, 3:4, :] = b[n_heads + hh:n_heads + hh + 1, :]


def _gate_call(h, wg_t, bias_col, n_heads, tm=512):
    n, d = h.shape
    ng = 4 * n_heads
    return pl.pallas_call(
        functools.partial(_gate_kernel, tm=tm, n_heads=n_heads),
        grid=(n // tm,),
        in_specs=[
            pl.BlockSpec((tm, d), lambda i: (i, 0)),
            pl.BlockSpec((ng, d), lambda i: (0, 0)),
            pl.BlockSpec((ng, 1), lambda i: (0, 0)),
        ],
        out_specs=pl.BlockSpec((n_heads, tm // CHUNK, 4, CHUNK), lambda i: (0, i, 0, 0)),
        out_shape=jax.ShapeDtypeStruct((n_heads, n // CHUNK, 4, CHUNK), F32),
        compiler_params=_cparams(("parallel",), 32), name="mlstm_gates",
    )(h, wg_t, bias_col)


def _scan_loop(nc, body, init):
    if nc <= 2:
        carry = init
        for ci in range(nc):
            carry = body(ci, carry)
        return carry
    return lax.fori_loop(0, nc, body, init)


MAX_GROUP = 8
STRIP = 32


def _seqs_per_step(n_seq, nc):
    ns = max(1, MAX_GROUP // nc)
    while n_seq % ns:
        ns -= 1
    return ns


def _group_loop(n_groups, body):
    if n_groups == 1:
        body(0, 0)
    else:
        lax.fori_loop(0, n_groups, body, 0)


def _chunk_off(c):
    return c * CHUNK if isinstance(c, int) else pl.multiple_of(c * CHUNK, CHUNK)


def _mlstm_kernel(*refs, nc, ns, has_init, first_layer, write_state):
    it = iter(refs)
    qt_ref, vt_ref, ot_ref, k_ref, gp_ref, nw_ref = (next(it) for _ in range(6))
    if has_init:
        c0_ref, n0_ref, m0_ref = next(it), next(it), next(it)
    if write_state and not first_layer:
        next(it), next(it), next(it)
    hm_ref = next(it)
    if write_state:
        cout_ref, nout_ref, mout_ref = next(it), next(it), next(it)
    ct_scr, n_scr, part_scr, inc_scr, ninc_scr, rows_scr, cst_scr, nst_scr, coef_scr = (next(it) for _ in range(9))

    L = CHUNK
    grp = min(MAX_GROUP, ns * nc)
    row = _iota((L, L), 0)
    col = _iota((L, L), 1)
    k_scale = HEAD_DIM ** -0.5
    b_idx = pl.program_id(0)
    h_idx = pl.program_id(1)
    n_heads = pl.num_programs(1)

    def gates(d, c):
        g = gp_ref[c]
        ig = g[d:d + 1, :]
        brow = g[2 + d:3 + d, :]
        blast = brow[:, L - 1:L] if d == 0 else brow[:, 0:1]
        return ig, brow, blast

    def local_group(g, carry):
        cs = [g * grp + u for u in range(grp)]
        pairs = [(u, d) for u in range(grp) for d in range(2)]
        qt = [qt_ref[c] for c in cs]
        vt = [vt_ref[c] for c in cs]
        k = [k_ref[pl.ds(_chunk_off(c), L), :] for c in cs]
        qk = [_dot(k[u], qt[u]) for u in range(grp)]
        vf = [v.astype(F32) for v in vt]
        gts = {(u, d): gates(d, cs[u]) for u, d in pairs}
        for u, d in pairs:
            c = cs[u]
            ig, brow, blast = gts[u, d]
            wj = blast - brow + ig
            mloc2 = jnp.max(wj, axis=1, keepdims=True)
            e = jnp.exp(wj - mloc2) * k_scale
            inc_scr[d, c] = _dot((vf[u] * e).astype(BF16), k[u])
            ninc_scr[d, c] = _dot3(jnp.broadcast_to(e, (8, L)), k[u])
            rows_scr[d, c, 2:3, :] = jnp.broadcast_to(mloc2, (1, L))
            rows_scr[d, c, 4:5, :] = jnp.broadcast_to(blast, (1, L))
        for u, d in pairs:
            c = cs[u]
            ig, brow, blast = gts[u, d]
            key_term = jnp.broadcast_to(ig - brow, (L, L)).T
            causal = (row <= col) if d == 0 else (row >= col)
            dm = jnp.where(causal, key_term + brow, -jnp.inf)
            mloc = jnp.max(dm, axis=0, keepdims=True)
            s = qk[u] * (jnp.exp(dm - mloc) * k_scale)
            part_scr[d, c] = _dot(vt[u], s.astype(BF16))
            rows_scr[d, c, 0:1, :] = jnp.sum(s, axis=0, keepdims=True)
            rows_scr[d, c, 1:2, :] = mloc
        return carry

    _group_loop(ns * nc // grp, local_group)

    def m_step(d, c, m):
        blast = rows_scr[d, c, 4:5, :]
        mloc2 = rows_scr[d, c, 2:3, :]
        rows_scr[d, c, 3:4, :] = m
        m_new = jnp.maximum(blast + m, mloc2)
        decay = jnp.exp(blast + m - m_new)
        w_inc = jnp.exp(mloc2 - m_new)
        coef_scr[d, c, 0:1, :] = jnp.concatenate([decay] * (HEAD_DIM // L), axis=1)
        coef_scr[d, c, 1:2, :] = jnp.concatenate([w_inc] * (HEAD_DIM // L), axis=1)
        return m_new

    for sq in range(ns):
        c_lo = sq * nc
        m_init = []
        for d in range(2):
            if has_init:
                ct_scr[d] = c0_ref[d].T
                n_scr[d] = jnp.broadcast_to(n0_ref[d], n_scr.shape[1:])
                m_init.append(jnp.full((1, L), m0_ref[(b_idx * 2 + d) * n_heads + h_idx], F32))
            else:
                ct_scr[d] = jnp.zeros(ct_scr.shape[1:], F32)
                n_scr[d] = jnp.zeros(n_scr.shape[1:], F32)
                m_init.append(jnp.zeros((1, L), F32))

        def m_body(ci, ms, c_lo=c_lo):
            return m_step(0, c_lo + ci, ms[0]), m_step(1, c_lo + nc - 1 - ci, ms[1])

        m_fin = _scan_loop(nc, m_body, tuple(m_init))

        def body(ci, carry, c_lo=c_lo):
            for d, c in ((0, c_lo + ci), (1, c_lo + nc - 1 - ci)):
                decay = coef_scr[d, c, 0:1, :]
                w_inc = coef_scr[d, c, 1:2, :]
                n8 = n_scr[d]
                nst_scr[d, c] = n8
                for r0 in range(0, HEAD_DIM, STRIP):
                    ct = ct_scr[d, r0:r0 + STRIP, :]
                    cst_scr[d, c, r0:r0 + STRIP, :] = ct.astype(BF16)
                    ct_scr[d, r0:r0 + STRIP, :] = decay * ct + w_inc * inc_scr[d, c, r0:r0 + STRIP, :]
                n_scr[d] = decay * n8 + w_inc * ninc_scr[d, c]
            return carry

        _scan_loop(nc, body, 0)

        if write_state:
            lsel = (lambda d, sq=sq: (sq, 0, d)) if first_layer else (lambda d, sq=sq: (sq, d))
            for d in range(2):
                cout_ref[lsel(d)] = ct_scr[d].T
                nout_ref[lsel(d)] = n_scr[d][0:1, :]
                mout_ref[lsel(d)] = m_fin[d]
            if first_layer:
                for l in range(1, cout_ref.shape[1]):
                    cout_ref[sq, l] = jnp.zeros(cout_ref.shape[2:], F32)
                    nout_ref[sq, l] = jnp.zeros(nout_ref.shape[2:], F32)
                    mout_ref[sq, l] = jnp.zeros(mout_ref.shape[2:], F32)

    def finish_group(g, carry):
        cs = [g * grp + u for u in range(grp)]
        qt = [qt_ref[c] for c in cs]
        cq = [[_dot(cst_scr[d, c], qt[u]) for d in range(2)] for u, c in enumerate(cs)]
        nq = [[_dot(nst_scr[d, c].astype(BF16), qt[u])[0:1, :] for d in range(2)] for u, c in enumerate(cs)]
        for u, c in enumerate(cs):
            coef = []
            for d in range(2):
                _, brow, _ = gates(d, c)
                den_loc = rows_scr[d, c, 0:1, :]
                mloc = rows_scr[d, c, 1:2, :]
                inter = brow + rows_scr[d, c, 3:4, :]
                m_row = jnp.maximum(mloc, inter)
                w_loc = jnp.exp(mloc - m_row)
                w_inter = jnp.exp(inter - m_row)
                den = w_loc * den_loc + w_inter * nq[u][d]
                r_den = 1.0 / jnp.maximum(jnp.abs(den), jnp.exp(-m_row))
                coef.append((w_loc * r_den, w_inter * r_den))
            ssq = jnp.zeros((1, L), F32)
            for r0 in range(0, HEAD_DIM, STRIP):
                h_s = None
                for d in range(2):
                    h_d = coef[d][0] * part_scr[d, c, r0:r0 + STRIP, :] + coef[d][1] * cq[u][d][r0:r0 + STRIP, :]
                    h_s = h_d if h_s is None else h_s + h_d
                part_scr[0, c, r0:r0 + STRIP, :] = h_s
                ssq = ssq + jnp.sum(h_s * h_s, axis=0, keepdims=True)
            r_norm = lax.rsqrt(ssq * (1.0 / HEAD_DIM) + EPS)
            for r0 in range(0, HEAD_DIM, L):
                y = part_scr[0, c, r0:r0 + L, :] * r_norm * nw_ref[r0:r0 + L, :]
                o = jax.nn.sigmoid(ot_ref[c, r0:r0 + L, :].astype(F32)) * y
                hm_ref[pl.ds(_chunk_off(c), L), r0:r0 + L] = o.T.astype(hm_ref.dtype)
        return carry

    _group_loop(ns * nc // grp, finish_group)


def _state_specs(shapes, layer, first_layer, ns):
    specs = []
    for shp in shapes:
        tail = shp[4:]
        zeros = (0,) * len(tail)
        if first_layer:
            specs.append(pl.BlockSpec((ns, shp[1], 2, None) + tail, lambda b, h, z=zeros: (b, 0, 0, h) + z))
        else:
            specs.append(pl.BlockSpec((ns, None, 2, None) + tail, lambda b, h, z=zeros: (b, layer, 0, h) + z))
    return specs


def _mlstm_call(proj_t, proj_n, gp, nw_b, init, state_bufs, *, n_seq, seq_len, tok_off, n_heads, layer, depth):
    dk = HEAD_DIM
    nc = seq_len // CHUNK
    has_init = init is not None
    ns = 1 if has_init else _seqs_per_step(n_seq, nc)
    nct = ns * nc
    boff = tok_off // (ns * seq_len)
    write_state = not has_init
    first_layer = layer == 0
    args = [proj_t, proj_t, proj_t, proj_n, gp, nw_b]
    specs = [
        pl.BlockSpec((nct, dk, CHUNK), lambda b, h: (b + boff, h, 0)),
        pl.BlockSpec((nct, dk, CHUNK), lambda b, h: (b + boff, n_heads + h, 0)),
        pl.BlockSpec((nct, dk, CHUNK), lambda b, h: (b + boff, 2 * n_heads + h, 0)),
        pl.BlockSpec((ns * seq_len, dk), lambda b, h: (b + boff, h)),
        pl.BlockSpec((None, nct, 4, CHUNK), lambda b, h: (h, b + boff, 0, 0)),
        pl.BlockSpec((dk, LANES), lambda b, h: (h, 0)),
    ]
    aliases = {}
    if has_init:
        c0, n0, m0 = init
        args += [c0, n0[:, layer][:, :, :, None, :], m0[:, layer].reshape(-1)]
        specs += [
            pl.BlockSpec((None, None, 2, None, dk, dk), lambda b, h: (b, layer, 0, h, 0, 0)),
            pl.BlockSpec((None, 2, None, 1, dk), lambda b, h: (b, 0, h, 0, 0)),
            pl.BlockSpec(memory_space=pltpu.SMEM),
        ]
    out_shape = [jax.ShapeDtypeStruct((n_seq * seq_len, n_heads * dk), BF16)]
    out_specs = [pl.BlockSpec((ns * seq_len, dk), lambda b, h: (b, h))]
    if write_state:
        shapes = [(n_seq, depth, 2, n_heads, dk, dk), (n_seq, depth, 2, n_heads, 1, dk),
                  (n_seq, depth, 2, n_heads, 1, LANES)]
        if not first_layer:
            for k_, buf in enumerate(state_bufs):
                aliases[len(args)] = 1 + k_
                args.append(buf)
                specs.append(pl.BlockSpec(memory_space=pl.ANY))
        out_shape += [jax.ShapeDtypeStruct(s, F32) for s in shapes]
        out_specs += _state_specs(shapes, layer, first_layer, ns)
    kern = functools.partial(_mlstm_kernel, nc=nc, ns=ns, has_init=has_init, first_layer=first_layer,
                             write_state=write_state)
    return pl.pallas_call(
        kern, grid=(n_seq // ns, n_heads), in_specs=specs, out_specs=out_specs, out_shape=out_shape,
        input_output_aliases=aliases,
        scratch_shapes=[pltpu.VMEM((2, dk, dk), F32), pltpu.VMEM((2, 8, dk), F32),
                        pltpu.VMEM((2, nct, dk, CHUNK), F32), pltpu.VMEM((2, nct, dk, dk), F32),
                        pltpu.VMEM((2, nct, 8, dk), F32), pltpu.VMEM((2, nct, 8, CHUNK), F32),
                        pltpu.VMEM((2, nct, dk, dk), BF16), pltpu.VMEM((2, nct, 8, dk), F32),
                        pltpu.VMEM((2, nct, 8, dk), F32)],
        compiler_params=_cparams(("parallel", "parallel"), 48), name="mlstm_mixer",
    )(*args)


def _ret_kernel(*refs, nc, ns, has_init, first_layer, write_state):
    it = iter(refs)
    dl_ref, qt_ref, vt_ref, gt_ref, k_ref, nw_ref = (next(it) for _ in range(6))
    if has_init:
        s0_ref = next(it)
    if write_state and not first_layer:
        next(it)
    hr_ref = next(it)
    if write_state:
        sout_ref = next(it)
    st_scr, part_scr, inc_scr, sst_scr = next(it), next(it), next(it), next(it)

    L = CHUNK
    grp = min(MAX_GROUP, ns * nc)
    rowi = _iota((L, L), 0)
    coli = _iota((L, L), 1)
    k_scale = HEAD_DIM ** -0.5
    h_idx = pl.program_id(1)
    n_heads = pl.num_programs(1)

    consts = []
    for d in range(2):
        lg = _log_sigmoid(jnp.full((1, 1), dl_ref[d * n_heads + h_idx], F32))
        rel = (coli - rowi if d == 0 else rowi - coli).astype(F32)
        dmat_t = jnp.where(rel >= 0, jnp.exp(jnp.maximum(rel, 0.0) * lg), 0.0) * k_scale
        pos = _iota((1, L), 1).astype(F32)
        if d == 1:
            pos = (L - 1.0) - pos
        q_decay = jnp.exp((pos + 1.0) * lg)
        k_decay = jnp.exp((L - 1.0 - pos) * lg) * k_scale
        chunk_decay = jnp.exp(float(L) * lg)
        consts.append((dmat_t, q_decay, k_decay, chunk_decay))

    def local_group(g, carry):
        cs = [g * grp + u for u in range(grp)]
        qt = [qt_ref[c] for c in cs]
        vt = [vt_ref[c] for c in cs]
        k = [k_ref[pl.ds(_chunk_off(c), L), :] for c in cs]
        qk = [_dot(k[u], qt[u]) for u in range(grp)]
        for u, c in enumerate(cs):
            vf = vt[u].astype(F32)
            for d in range(2):
                inc_scr[d, c] = _dot((vf * consts[d][2]).astype(BF16), k[u])
        for u, c in enumerate(cs):
            for d in range(2):
                part_scr[d, c] = _dot(vt[u], (qk[u] * consts[d][0]).astype(BF16))
        return carry

    _group_loop(ns * nc // grp, local_group)

    for sq in range(ns):
        c_lo = sq * nc
        for d in range(2):
            if has_init:
                st_scr[d] = s0_ref[d].T
            else:
                st_scr[d] = jnp.zeros(st_scr.shape[1:], F32)

        def body(ci, carry, c_lo=c_lo):
            for d, c in ((0, c_lo + ci), (1, c_lo + nc - 1 - ci)):
                for r0 in range(0, HEAD_DIM, STRIP):
                    st = st_scr[d, r0:r0 + STRIP, :]
                    sst_scr[d, c, r0:r0 + STRIP, :] = st.astype(BF16)
                    st_scr[d, r0:r0 + STRIP, :] = consts[d][3] * st + inc_scr[d, c, r0:r0 + STRIP, :]
            return carry

        _scan_loop(nc, body, 0)

        if write_state:
            for d in range(2):
                if first_layer:
                    sout_ref[sq, 0, d] = st_scr[d].T
                else:
                    sout_ref[sq, d] = st_scr[d].T
            if first_layer:
                for l in range(1, sout_ref.shape[1]):
                    sout_ref[sq, l] = jnp.zeros(sout_ref.shape[2:], F32)

    def finish_group(g, carry):
        cs = [g * grp + u for u in range(grp)]
        qt = [qt_ref[c] for c in cs]
        sq = [[_dot(sst_scr[d, c], qt[u]) for d in range(2)] for u, c in enumerate(cs)]
        for u, c in enumerate(cs):
            ssq = jnp.zeros((1, L), F32)
            for r0 in range(0, HEAD_DIM, STRIP):
                sl = slice(r0, r0 + STRIP)
                o_s = ((part_scr[0, c, sl, :] + part_scr[1, c, sl, :])
                       + (consts[0][1] * sq[u][0][sl, :] + consts[1][1] * sq[u][1][sl, :]))
                part_scr[0, c, sl, :] = o_s
                ssq = ssq + jnp.sum(o_s * o_s, axis=0, keepdims=True)
            r_norm = lax.rsqrt(ssq * (1.0 / HEAD_DIM) + EPS)
            for r0 in range(0, HEAD_DIM, L):
                y = part_scr[0, c, r0:r0 + L, :] * r_norm * nw_ref[r0:r0 + L, :]
                rg = gt_ref[c, r0:r0 + L, :].astype(F32)
                hr_ref[pl.ds(_chunk_off(c), L), r0:r0 + L] = (rg * jax.nn.sigmoid(rg) * y).T.astype(hr_ref.dtype)
        return carry

    _group_loop(ns * nc // grp, finish_group)


def _ret_call(proj_t, proj_n, decay_logit, nw_b, init, state_buf, *, n_seq, seq_len, tok_off, n_heads, layer,
              depth):
    dk = HEAD_DIM
    nc = seq_len // CHUNK
    has_init = init is not None
    ns = 1 if has_init else _seqs_per_step(n_seq, nc)
    nct = ns * nc
    boff = tok_off // (ns * seq_len)
    write_state = not has_init
    first_layer = layer == 0
    args = [decay_logit.reshape(-1), proj_t, proj_t, proj_t, proj_n, nw_b]
    specs = [
        pl.BlockSpec(memory_space=pltpu.SMEM),
        pl.BlockSpec((nct, dk, CHUNK), lambda b, h: (b + boff, 3 * n_heads + h, 0)),
        pl.BlockSpec((nct, dk, CHUNK), lambda b, h: (b + boff, 4 * n_heads + h, 0)),
        pl.BlockSpec((nct, dk, CHUNK), lambda b, h: (b + boff, 5 * n_heads + h, 0)),
        pl.BlockSpec((ns * seq_len, dk), lambda b, h: (b + boff, n_heads + h)),
        pl.BlockSpec((dk, LANES), lambda b, h: (h, 0)),
    ]
    aliases = {}
    if has_init:
        args.append(init)
        specs.append(pl.BlockSpec((None, None, 2, None, dk, dk), lambda b, h: (b, layer, 0, h, 0, 0)))
    out_shape = [jax.ShapeDtypeStruct((n_seq * seq_len, n_heads * dk), BF16)]
    out_specs = [pl.BlockSpec((ns * seq_len, dk), lambda b, h: (b, h))]
    if write_state:
        shapes = [(n_seq, depth, 2, n_heads, dk, dk)]
        if not first_layer:
            aliases[len(args)] = 1
            args.append(state_buf)
            specs.append(pl.BlockSpec(memory_space=pl.ANY))
        out_shape += [jax.ShapeDtypeStruct(s, F32) for s in shapes]
        out_specs += _state_specs(shapes, layer, first_layer, ns)
    kern = functools.partial(_ret_kernel, nc=nc, ns=ns, has_init=has_init, first_layer=first_layer,
                             write_state=write_state)
    return pl.pallas_call(
        kern, grid=(n_seq // ns, n_heads), in_specs=specs, out_specs=out_specs, out_shape=out_shape,
        input_output_aliases=aliases,
        scratch_shapes=[pltpu.VMEM((2, dk, dk), F32), pltpu.VMEM((2, nct, dk, CHUNK), F32),
                        pltpu.VMEM((2, nct, dk, dk), F32), pltpu.VMEM((2, nct, dk, dk), BF16)],
        compiler_params=_cparams(("parallel", "parallel"), 48), name="retention_mixer",
    )(*args)


def _merge_kernel(hmp_ref, hms_ref, hrp_ref, hrs_ref, ga_ref, gb_ref, x_ref, mod_ref, nw_ref, wa_ref, wb_ref,
                  wo_ref, wr_ref, x1_ref, h2_ref, aff_ref, wab_scr, wbb_scr, wob_scr, *, n_prompt_tiles):
    i = pl.program_id(0)

    @pl.when(i == 0)
    def _():
        wab_scr[...] = wa_ref[...].astype(BF16)
        wbb_scr[...] = wb_ref[...].astype(BF16)
        wob_scr[...] = wo_ref[...].astype(BF16)

    is_prompt = i < n_prompt_tiles
    hm = jnp.where(is_prompt, hmp_ref[...], hms_ref[...])
    hr = jnp.where(is_prompt, hrp_ref[...], hrs_ref[...])
    ya = _dot(hm, wab_scr[...])
    yb = _dot(hr, wbb_scr[...])
    merged = (jax.nn.sigmoid(ga_ref[...].astype(F32)) * ya + jax.nn.sigmoid(gb_ref[...].astype(F32)) * yb)
    y = _dot(merged.astype(BF16), wob_scr[...])
    x1 = x_ref[...] + mod_ref[2:3, :] * y
    x1_ref[...] = x1
    h2 = x1 * lax.rsqrt(jnp.mean(x1 * x1, axis=-1, keepdims=True) + EPS) * nw_ref[...]
    h2 = h2 * (1.0 + mod_ref[4:5, :]) + mod_ref[3:4, :]
    h2_ref[...] = h2
    logits = _dot_nt(wr_ref[...].astype(BF16), h2.astype(BF16))
    p = jnp.exp(logits - jnp.max(logits, axis=0, keepdims=True))
    aff_ref[...] = p / jnp.sum(p, axis=0, keepdims=True)


def _merge_call(hm_p, hm_s, hr_p, hr_s, proj_n, x, mod, norm_w, w_a, w_b, w_out, router_wt, layer, *,
                n_prompt, dec_seq, tm=512):
    n, d = x.shape
    ne = router_wt.shape[1]
    npt = n_prompt // tm
    grp = functools.partial(_group_of_tile, tm=tm, n_prompt=n_prompt, dec_seq=dec_seq)
    tile = pl.BlockSpec((tm, d), lambda i: (i, 0))
    tile_p = pl.BlockSpec((tm, d), lambda i: (jnp.minimum(i, npt - 1), 0))
    tile_s = pl.BlockSpec((tm, d), lambda i: (jnp.maximum(i - npt, 0), 0))
    full = pl.BlockSpec((None, d, d), lambda i: (layer, 0, 0))
    return pl.pallas_call(
        functools.partial(_merge_kernel, n_prompt_tiles=npt), grid=(n // tm,),
        in_specs=[
            tile_p, tile_s, tile_p, tile_s,
            pl.BlockSpec((tm, d), lambda i: (i, 2)),
            pl.BlockSpec((tm, d), lambda i: (i, 3)),
            tile,
            pl.BlockSpec((None, 6, d), lambda i: (grp(i), 0, 0)),
            pl.BlockSpec((1, d), lambda i: (0, 0)),
            full, full, full,
            pl.BlockSpec((None, ne, d), lambda i: (layer, 0, 0)),
        ],
        out_specs=[tile, tile, pl.BlockSpec((ne, tm), lambda i: (0, i))],
        out_shape=[jax.ShapeDtypeStruct((n, d), F32), jax.ShapeDtypeStruct((n, d), F32),
                   jax.ShapeDtypeStruct((ne, n), F32)],
        scratch_shapes=[pltpu.VMEM((d, d), BF16)] * 3,
        compiler_params=_cparams(("arbitrary",), 56), name="merge_out_router",
    )(hm_p, hm_s, hr_p, hr_s, proj_n, proj_n, x, mod, norm_w.reshape(1, d), w_a, w_b, w_out, router_wt)


def _route_kernel(a_ref, at_ref, idx_ref, gate_ref, thr_scr, *, cap, n_tok):
    n_sets = a_ref.shape[0]
    nb = n_tok // LANES
    a_all = a_ref[...]

    def as_f32(bits):
        return lax.bitcast_convert_type(bits, F32)

    def count_ge(cand):
        m = jnp.where(a_all >= cand, 1.0, 0.0)
        return jnp.sum(jnp.sum(m, axis=2, keepdims=True), axis=1, keepdims=True)

    def bit_step(k, thr):
        cand = thr | lax.shift_left(jnp.int32(1), 30 - k)
        return jnp.where(count_ge(as_f32(cand)) >= cap, cand, thr)

    thr_scr[...] = lax.fori_loop(0, 31, bit_step, jnp.zeros((n_sets, 1, 1), jnp.int32))

    r128 = _iota((LANES, LANES), 0)
    c128 = _iota((LANES, LANES), 1)
    upper = jnp.where(r128 <= c128, 1.0, 0.0).astype(BF16)
    lower_t = jnp.where(r128 >= c128, 1.0, 0.0).astype(BF16)
    rb = _iota((nb, nb), 0)
    cb = _iota((nb, nb), 1)
    blk_before_rows = jnp.where(cb < rb, 1.0, 0.0).astype(BF16)
    blk_before_cols = jnp.where(rb < cb, 1.0, 0.0).astype(BF16)

    def incl_counts(mask):
        within = _dot(mask.astype(BF16), upper)
        before = _dot(blk_before_rows, within.astype(BF16))[:, LANES - 1:LANES]
        return within + before

    def incl_counts_t(mask_t):
        within = _dot(lower_t, mask_t.astype(BF16))
        before = _dot(within.astype(BF16), blk_before_cols)[LANES - 1:LANES, :]
        return within + before

    slot = _iota((1, cap), 1).astype(F32)
    blk_col = _iota((nb, 1), 0).astype(F32)
    sub_col = _iota((LANES, 1), 0).astype(F32)

    def per_set(s, carry):
        thr_bits = thr_scr[s]
        thr = as_f32(thr_bits)
        nxt = as_f32(thr_bits + 1)
        a = a_ref[s]
        a_t = at_ref[s]
        gt = jnp.where(a >= nxt, 1.0, 0.0)
        eq = jnp.where((a >= thr) & (a < nxt), 1.0, 0.0)
        gt_t = jnp.where(a_t >= nxt, 1.0, 0.0)
        eq_t = jnp.where((a_t >= thr) & (a_t < nxt), 1.0, 0.0)
        n_gt = jnp.sum(jnp.sum(gt, axis=1, keepdims=True), axis=0, keepdims=True)
        need = cap - n_gt
        sel = gt + eq * jnp.where(incl_counts(eq) - eq < need, 1.0, 0.0)
        sel_t = gt_t + eq_t * jnp.where(incl_counts_t(eq_t) - eq_t < need, 1.0, 0.0)
        cnt = incl_counts(sel)
        cnt_t = incl_counts_t(sel_t)
        blk_end = cnt[:, LANES - 1:LANES]
        blk_of_slot = jnp.sum(jnp.where(blk_end <= slot, 1.0, 0.0), axis=0, keepdims=True)
        onehot_blk = jnp.where(blk_col == blk_of_slot, 1.0, 0.0).astype(BF16)
        cnt_rows = _dot3(cnt_t, onehot_blk)
        sub_of_slot = jnp.sum(jnp.where(cnt_rows <= slot, 1.0, 0.0), axis=0, keepdims=True)
        a_rows = _dot3(a_t, onehot_blk)
        gate = jnp.sum(jnp.where(sub_col == sub_of_slot, a_rows, 0.0), axis=0, keepdims=True)
        idx_ref[s] = (blk_of_slot * LANES + sub_of_slot).astype(jnp.int32)
        gate_ref[s] = gate
        return carry

    lax.fori_loop(0, n_sets, per_set, 0)


def _route_call(aff_t, n_pass, cap):
    ne, n = aff_t.shape
    n_tok = n // n_pass
    nb = n_tok // LANES
    a4 = aff_t.reshape(ne, n_pass, nb, LANES).transpose(1, 0, 2, 3).reshape(n_pass * ne, nb, LANES)
    a4_t = a4.transpose(0, 2, 1)
    n_sets = n_pass * ne
    return pl.pallas_call(
        functools.partial(_route_kernel, cap=cap, n_tok=n_tok),
        out_shape=[jax.ShapeDtypeStruct((n_sets, 1, cap), jnp.int32),
                   jax.ShapeDtypeStruct((n_sets, 1, cap), F32)],
        scratch_shapes=[pltpu.VMEM((n_sets, 1, 1), jnp.int32)],
        compiler_params=pltpu.CompilerParams(vmem_limit_bytes=32 * MIB), name="expert_choice_route",
    )(a4, a4_t)


ROW_LOOP_UNROLL = 8


def _moe_kernel(idx_ref, gate_ref, h_hbm, w1_ref, w3_ref, w2_ref, out_hbm,
                xe_scr, xb_scr, ye_a, ye_b, acc_scr, gsem, osem, *, rows, rows_pad):
    e = pl.program_id(0)
    f = pl.program_id(1)
    n_e = pl.num_programs(0)
    n_f = pl.num_programs(1)
    chunk = rows_pad // n_f

    def gather_row(slot, s):
        tok = idx_ref[slot * rows_pad + s]
        pltpu.make_async_copy(h_hbm.at[pl.ds(tok, 1), :], xe_scr.at[pl.ds(s, 1), :], gsem).start()

    def scatter_row(slot, s, ye_ref):
        tok = idx_ref[slot * rows_pad + s]
        g = gate_ref[slot * rows_pad + s]
        acc_scr[pl.ds(tok, 1), :] += ye_ref[pl.ds(s, 1), :] * g

    def wait_gather():
        pltpu.make_async_copy(h_hbm.at[pl.ds(0, rows_pad), :], xe_scr, gsem).wait()

    @pl.when((e == 0) & (f == 0))
    def _():
        acc_scr[...] = jnp.zeros_like(acc_scr)
        ye_a[...] = jnp.zeros_like(ye_a)
        ye_b[...] = jnp.zeros_like(ye_b)

        def issue(s, carry):
            gather_row(1, s)
            return carry
        lax.fori_loop(0, rows_pad, issue, 0, unroll=ROW_LOOP_UNROLL)

    def step(ye_cur, ye_prev):
        @pl.when(f == 0)
        def _():
            wait_gather()
            xb_scr[...] = xe_scr[0:rows, :].astype(BF16)
            ye_cur[0:rows, :] = jnp.zeros((rows, ye_cur.shape[1]), F32)

        xb = xb_scr[...]
        h1 = _dot(xb, w1_ref[...].astype(BF16))
        h3 = _dot(xb, w3_ref[...].astype(BF16))
        he = (h1 * jax.nn.sigmoid(h1) * h3).astype(BF16)
        ye_cur[0:rows, :] += _dot(he, w2_ref[...].astype(BF16))
        base = f * chunk
        for r in range(chunk):
            gather_row(e + 2, base + r)
        for r in range(chunk):
            scatter_row(e, base + r, ye_prev)

        @pl.when((e == n_e - 1) & (f == n_f - 1))
        def _():
            def scatter(s, carry):
                scatter_row(n_e, s, ye_cur)
                return carry
            lax.fori_loop(0, rows, scatter, 0, unroll=ROW_LOOP_UNROLL)
            wait_gather()
            cp = pltpu.make_async_copy(acc_scr, out_hbm, osem)
            cp.start()
            cp.wait()

    @pl.when(e % 2 == 0)
    def _():
        step(ye_a, ye_b)

    @pl.when(e % 2 == 1)
    def _():
        step(ye_b, ye_a)


def _moe_call(idx, gate, h2, w1, w3, w2, layer, tf=256):
    n, d = h2.shape
    ne, rows = idx.shape
    dff = w1.shape[3]
    nf = dff // tf
    rows_pad = -(-rows // (8 * nf)) * 8 * nf
    pad = ((1, 1), (0, rows_pad - rows))
    idx_all = jnp.pad(idx, pad).reshape(-1)
    gate_all = jnp.pad(gate, pad).reshape(-1)
    grid_spec = pltpu.PrefetchScalarGridSpec(
        num_scalar_prefetch=1,
        grid=(ne, nf),
        in_specs=[
            pl.BlockSpec(memory_space=pltpu.SMEM),
            pl.BlockSpec(memory_space=pl.ANY),
            pl.BlockSpec((None, None, d, tf), lambda e, f, idx: (layer, e, 0, f)),
            pl.BlockSpec((None, None, d, tf), lambda e, f, idx: (layer, e, 0, f)),
            pl.BlockSpec((None, None, tf, d), lambda e, f, idx: (layer, e, f, 0)),
        ],
        out_specs=pl.BlockSpec(memory_space=pl.ANY),
        scratch_shapes=[
            pltpu.VMEM((rows_pad, d), F32), pltpu.VMEM((rows, d), BF16),
            pltpu.VMEM((rows_pad, d), F32), pltpu.VMEM((rows_pad, d), F32),
            pltpu.VMEM((n, d), F32), pltpu.SemaphoreType.DMA, pltpu.SemaphoreType.DMA,
        ],
    )
    return pl.pallas_call(
        functools.partial(_moe_kernel, rows=rows, rows_pad=rows_pad), grid_spec=grid_spec,
        out_shape=jax.ShapeDtypeStruct((n, d), F32),
        compiler_params=_cparams(("arbitrary", "arbitrary"), 62), name="expert_ffn",
    )(idx_all, gate_all, h2, w1, w3, w2)


def _rope_tables(t):
    rows = t // GRID_W
    row = jnp.repeat(jnp.arange(rows, dtype=F32), GRID_W)
    colp = jnp.tile(jnp.arange(GRID_W, dtype=F32), rows)
    n_freq = HEAD_DIM // 4
    inv = ROPE_BASE ** (-jnp.arange(n_freq, dtype=F32) / n_freq)
    ang = jnp.concatenate([row[:, None] * inv, colp[:, None] * inv], axis=-1)
    return jnp.cos(ang), jnp.sin(ang)


def kernel(x_prompt, x_sample, c, state_mlstm_C, state_mlstm_n, state_mlstm_m, state_ret_S, c_ctx, w_mod, b_mod,
           norm1_w, norm2_w, w_in, mlstm_if_b, mlstm_norm_w, ret_decay_logit, ret_norm_w, w_branch_a, w_branch_b,
           w_out, router_w, ffn_w1, ffn_w3, ffn_w2, final_norm_w):
    bp, seq, d = x_prompt.shape
    db, dec_seq, _ = x_sample.shape
    depth = w_mod.shape[0]
    hm = mlstm_if_b.shape[-1]
    hr = ret_decay_logit.shape[-1]
    ne = router_w.shape[-1]
    n_prompt = bp * seq
    n_sample = db * dec_seq
    assert n_prompt == n_sample and hm * HEAD_DIM == d and hr * HEAD_DIM == d and hm == hr
    cap = EC_FACTOR * n_prompt // ne
    tm = 512
    tp = 1024
    geo = dict(n_prompt=n_prompt, dec_seq=dec_seq)

    x = jnp.concatenate([x_prompt.reshape(n_prompt, d), x_sample.reshape(n_sample, d)], axis=0)
    cond8 = jnp.concatenate([c_ctx[None, :], c, jnp.zeros((8 - 1 - db, d), F32)], axis=0)
    mod = _mod_call(cond8, w_mod, b_mod)[:, :1 + db].reshape(depth, 1 + db, 6, d)

    n_m = 4 * d
    g0 = n_m + 4 * hm

    w_t = jnp.swapaxes(w_in, 1, 2)
    mq, mk, mv, mo = (k * d for k in range(4))
    rq, rk, rv, rg, ga, gb = (g0 + k * d for k in range(6))
    wg_t = w_t[:, n_m:g0]
    gate_bias = jnp.transpose(mlstm_if_b, (0, 2, 1, 3)).reshape(depth, 4 * hm, 1)
    router_wt = jnp.swapaxes(router_w, 1, 2)
    nw_m = jnp.broadcast_to(mlstm_norm_w[:, :, None], (depth, d, LANES))
    nw_r = jnp.broadcast_to(ret_norm_w[:, :, None], (depth, d, LANES))

    cos, sin = _rope_tables(dec_seq)
    cos_t = cos.T.reshape(HEAD_DIM // 2, dec_seq // tp, tp).transpose(1, 0, 2)
    sin_t = sin.T.reshape(HEAD_DIM // 2, dec_seq // tp, tp).transpose(1, 0, 2)

    _, h = _resid_norm_call(x, None, None, mod[0], norm1_w[0], gate_row=0, mod_rows=(0, 1), h_dtype=BF16,
                            write_x=False, **geo)
    m_bufs, s_buf = None, None
    y_prompt = y_sample = None
    for l in range(depth):
        proj_t = _proj_t_call(h, w_t, l, (mq, mv, mo, rq, rv, rg), (cos_t, sin_t), 3, tm=tp, **geo)
        proj_n = _proj_call(h, w_t, l, (mk, rk, ga, gb), (cos, sin), 1, tm=tp, **geo)
        gp = _gate_call(h, wg_t[l], gate_bias[l], hm, tm=tm)

        mix = dict(n_heads=hm, layer=l, depth=depth)
        hm_p, *m_bufs = _mlstm_call(proj_t, proj_n, gp, nw_m[l], None, m_bufs, n_seq=bp, seq_len=seq, tok_off=0,
                                    **mix)
        (hm_s,) = _mlstm_call(proj_t, proj_n, gp, nw_m[l], (state_mlstm_C, state_mlstm_n, state_mlstm_m), None,
                              n_seq=db, seq_len=dec_seq, tok_off=n_prompt, **mix)
        hr_p, s_buf = _ret_call(proj_t, proj_n, ret_decay_logit[l], nw_r[l], None, s_buf, n_seq=bp, seq_len=seq,
                                tok_off=0, **mix)
        (hr_s,) = _ret_call(proj_t, proj_n, ret_decay_logit[l], nw_r[l], state_ret_S, None, n_seq=db,
                            seq_len=dec_seq, tok_off=n_prompt, **mix)

        x1, h2, aff_t = _merge_call(hm_p, hm_s, hr_p, hr_s, proj_n, x, mod[l], norm2_w[l], w_branch_a, w_branch_b,
                                    w_out, router_wt, l, tm=tm, **geo)
        idx, gate = _route_call(aff_t, 2, cap)
        idx = idx.reshape(2, ne, cap) + (jnp.arange(2, dtype=jnp.int32) * n_prompt)[:, None, None]
        idx = jnp.swapaxes(idx, 0, 1).reshape(ne, 2 * cap)
        gate = jnp.swapaxes(gate.reshape(2, ne, cap), 0, 1).reshape(ne, 2 * cap)
        moe = _moe_call(idx, gate, h2, ffn_w1, ffn_w3, ffn_w2, l)
        if l + 1 < depth:
            x, h = _resid_norm_call(x1, moe, mod[l], mod[l + 1], norm1_w[l + 1], gate_row=5, mod_rows=(0, 1),
                                    h_dtype=BF16, write_x=True, **geo)
        else:
            fin = dict(gate_row=5, mod_rows=None, h_dtype=F32, write_x=False, **geo)
            _, y_prompt = _resid_norm_call(x1, moe, mod[l], None, final_norm_w, row_off=0, n_rows=n_prompt, **fin)
            _, y_sample = _resid_norm_call(x1, moe, mod[l], None, final_norm_w, row_off=n_prompt, n_rows=n_sample,
                                           **fin)

    c_buf, n_buf, m_buf = m_bufs
    return (y_prompt.reshape(bp, seq, d), y_sample.reshape(db, dec_seq, d), c_buf, n_buf[:, :, :, :, 0, :],
            m_buf[:, :, :, :, 0, 0], s_buf)
```

```python
import functools

import numpy as np
import jax
import jax.numpy as jnp
from jax import lax
from jax.experimental import pallas as pl
from jax.experimental.pallas import tpu as pltpu

F32 = jnp.float32
BF16 = jnp.bfloat16

GRID_W = 64
CHUNK = 128
LANES = 128
HEAD_DIM = 256
N_EXPERTS = 16
EC_FACTOR = 2
ROPE_BASE = 10000.0
EPS = 1e-6
V7X_VMEM_BYTES = 64 * 1024 * 1024
MIB = 1024 * 1024


def _cparams(semantics, vmem_mib):
    assert vmem_mib * MIB < V7X_VMEM_BYTES
    return pltpu.CompilerParams(dimension_semantics=semantics, vmem_limit_bytes=vmem_mib * MIB)


def _dot(a, b):
    return jnp.dot(a, b, preferred_element_type=F32)


def _dot_nt(a, b):
    return lax.dot_general(a, b, (((1,), (1,)), ((), ())), preferred_element_type=F32)


def _log_sigmoid(x):
    return -(jnp.maximum(-x, 0.0) + jnp.log1p(jnp.exp(-jnp.abs(x))))


def _split3(x):
    hi = x.astype(BF16)
    r1 = x - hi.astype(F32)
    mid = r1.astype(BF16)
    lo = (r1 - mid.astype(F32)).astype(BF16)
    return hi, mid, lo


def _dot3(x, m):
    hi, mid, lo = _split3(x)
    return _dot(hi, m) + _dot(mid, m) + _dot(lo, m)


def _iota(shape, dim):
    return lax.broadcasted_iota(jnp.int32, shape, dim)


def _mod_kernel(cond_ref, w_ref, b_ref, out_ref):
    c = cond_ref[...]
    s = (c * jax.nn.sigmoid(c)).astype(BF16)
    out_ref[...] = _dot(s, w_ref[...].astype(BF16)) + b_ref[...]


def _mod_call(cond8, w_mod, b_mod):
    depth, d, w6 = w_mod.shape
    tn = 1536
    return pl.pallas_call(
        _mod_kernel,
        grid=(depth, w6 // tn),
        in_specs=[
            pl.BlockSpec((8, d), lambda l, j: (0, 0)),
            pl.BlockSpec((None, d, tn), lambda l, j: (l, 0, j)),
            pl.BlockSpec((None, 1, tn), lambda l, j: (l, 0, j)),
        ],
        out_specs=pl.BlockSpec((None, 8, tn), lambda l, j: (l, 0, j)),
        out_shape=jax.ShapeDtypeStruct((depth, 8, w6), F32),
        compiler_params=_cparams(("parallel", "parallel"), 32),
        name="adaln_mod",
    )(cond8, w_mod, b_mod.reshape(depth, 1, w6))


def _group_of_tile(i, tm, n_prompt, dec_seq):
    return jnp.maximum(i * tm - n_prompt + dec_seq, 0) // dec_seq


def _resid_norm_kernel(*refs, has_delta, has_mod, gate_row, mod_rows, write_x, n_first_tiles):
    it = iter(refs)
    x_ref = next(it)
    x2_ref = next(it) if n_first_tiles else None
    delta_ref = next(it) if has_delta else None
    mod_ref = next(it) if (has_delta or has_mod) else None
    nw_ref = next(it)
    xo_ref = next(it) if write_x else None
    h_ref = next(it)
    x = x_ref[...]
    if n_first_tiles:
        x = jnp.where(pl.program_id(0) < n_first_tiles, x, x2_ref[...])
    if has_delta:
        x = x + mod_ref[gate_row:gate_row + 1, :] * delta_ref[...]
    if write_x:
        xo_ref[...] = x
    y = x * lax.rsqrt(jnp.mean(x * x, axis=-1, keepdims=True) + EPS) * nw_ref[...]
    if has_mod:
        sh_row, sc_row = mod_rows
        y = y * (1.0 + mod_ref[sc_row:sc_row + 1, :]) + mod_ref[sh_row:sh_row + 1, :]
    h_ref[...] = y.astype(h_ref.dtype)


def _resid_norm_call(x, delta, mod_gate, mod_norm, norm_w, *, gate_row, mod_rows, h_dtype, write_x,
                     n_prompt, dec_seq, tm=512, row_off=0, n_rows=None, x_tail=None):
    n, d = x.shape
    has_delta = delta is not None
    has_mod = mod_norm is not None
    grp = functools.partial(_group_of_tile, tm=tm, n_prompt=n_prompt, dec_seq=dec_seq)
    if x_tail is None:
        n_rows = n if n_rows is None else n_rows
        toff = row_off // tm
        nft = 0
        tile_in = pl.BlockSpec((tm, d), lambda i: (i + toff, 0))
        args, specs = [x], [tile_in]
    else:
        assert row_off == 0 and n_rows is None and not has_delta
        n_rows, toff, nft = n + x_tail.shape[0], 0, n // tm
        args = [x, x_tail]
        specs = [pl.BlockSpec((tm, d), lambda i: (jnp.minimum(i, nft - 1), 0)),
                 pl.BlockSpec((tm, d), lambda i: (jnp.maximum(i - nft, 0), 0))]
    tile_out = pl.BlockSpec((tm, d), lambda i: (i, 0))
    if has_delta:
        args.append(delta)
        specs.append(tile_in)
    if has_delta or has_mod:
        mg = mod_gate if has_delta else mod_norm
        mn = mod_norm if has_mod else mod_gate
        args.append(jnp.concatenate([mg, mn], axis=1))
        specs.append(pl.BlockSpec((None, 12, d), lambda i: (grp(i + toff), 0, 0)))
    args.append(norm_w.reshape(1, d))
    specs.append(pl.BlockSpec((1, d), lambda i: (0, 0)))
    out_shape, out_specs = [], []
    if write_x:
        out_shape.append(jax.ShapeDtypeStruct((n_rows, d), F32))
        out_specs.append(tile_out)
    out_shape.append(jax.ShapeDtypeStruct((n_rows, d), h_dtype))
    out_specs.append(tile_out)
    kern = functools.partial(
        _resid_norm_kernel, has_delta=has_delta, has_mod=has_mod, gate_row=gate_row,
        mod_rows=None if mod_rows is None else (6 + mod_rows[0], 6 + mod_rows[1]), write_x=write_x,
        n_first_tiles=nft)
    outs = pl.pallas_call(
        kern, grid=(n_rows // tm,), in_specs=specs, out_specs=out_specs, out_shape=out_shape,
        compiler_params=_cparams(("parallel",), 32), name="resid_norm",
    )(*args)
    return outs if write_x else (None, outs[0])


def _rope_pair(x1, x2, cos, sin):
    return x1 * cos - x2 * sin, x1 * sin + x2 * cos


def _weight_rows_spec(layer, row_starts, tn, d):
    def index_map(j, i):
        start = sum(jnp.where(j == k, s, 0) for k, s in enumerate(row_starts))
        return layer, pl.multiple_of(start, 8), 0
    assert all(s % 8 == 0 for s in row_starts)
    return pl.BlockSpec((pl.Element(1), pl.Element(tn), pl.Element(d)), index_map)


def _proj_kernel(h_ref, w_ref, cos_ref, sin_ref, out_ref, wb_scr, *, tm, rope_tile, n_prompt, dec_seq):
    j = pl.program_id(0)
    i = pl.program_id(1)

    @pl.when(i == 0)
    def _():
        wb_scr[...] = w_ref[0].astype(BF16)

    row0 = pl.multiple_of(i * tm, tm)
    use_rope = (j == rope_tile) & (row0 >= n_prompt)
    pos0 = pl.multiple_of(jnp.where(use_rope, (row0 - n_prompt) % dec_seq, 0), tm)
    cos = cos_ref[pl.ds(pos0, tm), :]
    sin = sin_ref[pl.ds(pos0, tm), :]
    h = h_ref[pl.ds(row0, tm), :]
    half = HEAD_DIM // 2
    for hh in range(wb_scr.shape[0] // HEAD_DIM):
        c0 = hh * HEAD_DIM
        acc = _dot_nt(h, wb_scr[c0:c0 + HEAD_DIM, :])
        x1, x2 = acc[:, :half], acc[:, half:]
        y1, y2 = _rope_pair(x1, x2, cos, sin)
        out_ref[:, c0:c0 + half] = jnp.where(use_rope, y1, x1).astype(out_ref.dtype)
        out_ref[:, c0 + half:c0 + HEAD_DIM] = jnp.where(use_rope, y2, x2).astype(out_ref.dtype)


def _proj_call(h, w_t, layer, row_starts, rope, rope_tile, *, n_prompt, dec_seq, tm=512, tn=1024):
    n, d = h.shape
    nj = len(row_starts)
    kern = functools.partial(_proj_kernel, tm=tm, rope_tile=rope_tile, n_prompt=n_prompt, dec_seq=dec_seq)
    return pl.pallas_call(
        kern, grid=(nj, n // tm),
        in_specs=[
            pl.BlockSpec((n, d), lambda j, i: (0, 0)),
            _weight_rows_spec(layer, row_starts, tn, d),
            pl.BlockSpec(rope[0].shape, lambda j, i: (0, 0)),
            pl.BlockSpec(rope[1].shape, lambda j, i: (0, 0)),
        ],
        out_specs=pl.BlockSpec((tm, tn), lambda j, i: (i, j)),
        out_shape=jax.ShapeDtypeStruct((n, nj * tn), BF16),
        scratch_shapes=[pltpu.VMEM((tn, d), BF16)],
        compiler_params=_cparams(("parallel", "arbitrary"), 56), name="in_proj",
    )(h, w_t, *rope)


def _proj_t_kernel(h_ref, w_ref, cos_ref, sin_ref, out_ref, wb_scr, *, tm, rope_tile, n_prompt, dec_seq):
    j = pl.program_id(0)
    i = pl.program_id(1)

    @pl.when(i == 0)
    def _():
        wb_scr[...] = w_ref[0].astype(BF16)

    row0 = pl.multiple_of(i * tm, tm)
    use_rope = (j == rope_tile) & (row0 >= n_prompt)
    blk = jnp.where(use_rope, ((row0 - n_prompt) % dec_seq) // tm, 0)
    cos = cos_ref[blk]
    sin = sin_ref[blk]
    h = h_ref[pl.ds(row0, tm), :]
    half = HEAD_DIM // 2
    for hh in range(wb_scr.shape[0] // HEAD_DIM):
        r0 = hh * HEAD_DIM
        acc = _dot_nt(wb_scr[r0:r0 + HEAD_DIM, :], h)
        x1, x2 = acc[:half, :], acc[half:, :]
        y1, y2 = _rope_pair(x1, x2, cos, sin)
        y1 = jnp.where(use_rope, y1, x1).astype(out_ref.dtype)
        y2 = jnp.where(use_rope, y2, x2).astype(out_ref.dtype)
        for s in range(tm // CHUNK):
            out_ref[s, r0:r0 + half, :] = y1[:, s * CHUNK:(s + 1) * CHUNK]
            out_ref[s, r0 + half:r0 + HEAD_DIM, :] = y2[:, s * CHUNK:(s + 1) * CHUNK]


def _proj_t_call(h, w_t, layer, row_starts, rope_t, rope_tile, *, n_prompt, dec_seq, tm=512, tn=1024):
    n, d = h.shape
    nj = len(row_starts)
    kern = functools.partial(_proj_t_kernel, tm=tm, rope_tile=rope_tile, n_prompt=n_prompt, dec_seq=dec_seq)
    return pl.pallas_call(
        kern, grid=(nj, n // tm),
        in_specs=[
            pl.BlockSpec((n, d), lambda j, i: (0, 0)),
            _weight_rows_spec(layer, row_starts, tn, d),
            pl.BlockSpec(rope_t[0].shape, lambda j, i: (0, 0, 0)),
            pl.BlockSpec(rope_t[1].shape, lambda j, i: (0, 0, 0)),
        ],
        out_specs=pl.BlockSpec((tm // CHUNK, tn, CHUNK), lambda j, i: (i, j, 0)),
        out_shape=jax.ShapeDtypeStruct((n // CHUNK, nj * tn, CHUNK), BF16),
        scratch_shapes=[pltpu.VMEM((tn, d), BF16)],
        compiler_params=_cparams(("parallel", "arbitrary"), 56), name="in_proj_t",
    )(h, w_t, *rope_t)


def _gate_kernel(h_ref, wg_ref, bias_ref, out_ref, *, tm, n_heads):
    g = _dot_nt(wg_ref[...].astype(BF16), h_ref[...]) + bias_ref[...]
    nd = 2 * n_heads
    ig = g[0:nd, :]
    lf = _log_sigmoid(g[nd:2 * nd, :])
    r = _iota((CHUNK, CHUNK), 0)
    c = _iota((CHUNK, CHUNK), 1)
    upper = jnp.where(r <= c, 1.0, 0.0).astype(BF16)
    lower = jnp.where(r >= c, 1.0, 0.0).astype(BF16)
    is_fwd = _iota((nd, CHUNK), 0) < n_heads
    for s in range(tm // CHUNK):
        sl = slice(s * CHUNK, (s + 1) * CHUNK)
        lf_c = lf[:, sl]
        b = jnp.where(is_fwd, _dot3(lf_c, upper), _dot3(lf_c, lower))
        ig_c = ig[:, sl]
        for hh in range(n_heads):
            out_ref[hh, s, 0:1, :] = ig_c[hh:hh + 1, :]
            out_ref[hh, s, 1:2, :] = ig_c[n_heads + hh:n_heads + hh + 1, :]
            out_ref[hh, s, 2:3, :] = b[hh:hh + 1, :]
            out_ref[hh, s, 3:4, :] = b[n_heads + hh:n_heads + hh + 1, :]


def _gate_call(h, wg_t, bias_col, n_heads, tm=512):
    n, d = h.shape
    ng = 4 * n_heads
    return pl.pallas_call(
        functools.partial(_gate_kernel, tm=tm, n_heads=n_heads),
        grid=(n // tm,),
        in_specs=[
            pl.BlockSpec((tm, d), lambda i: (i, 0)),
            pl.BlockSpec((ng, d), lambda i: (0, 0)),
            pl.BlockSpec((ng, 1), lambda i: (0, 0)),
        ],
        out_specs=pl.BlockSpec((n_heads, tm // CHUNK, 4, CHUNK), lambda i: (0, i, 0, 0)),
        out_shape=jax.ShapeDtypeStruct((n_heads, n // CHUNK, 4, CHUNK), F32),
        compiler_params=_cparams(("parallel",), 32), name="mlstm_gates",
    )(h, wg_t, bias_col)


def _scan_loop(nc, body, init):
    if nc <= 2:
        carry = init
        for ci in range(nc):
            carry = body(ci, carry)
        return carry
    return lax.fori_loop(0, nc, body, init)


MAX_GROUP = 16
STRIP = 32


def _seqs_per_step(n_seq, nc):
    ns = max(1, MAX_GROUP // nc)
    while n_seq % ns:
        ns -= 1
    return ns


def _group_loop(n_groups, body):
    if n_groups == 1:
        body(0, 0)
    else:
        lax.fori_loop(0, n_groups, body, 0)


def _chunk_off(c):
    return c * CHUNK if isinstance(c, int) else pl.multiple_of(c * CHUNK, CHUNK)


def _mlstm_kernel(*refs, nc, ns, has_init, first_layer, write_state):
    it = iter(refs)
    qt_ref, vt_ref, ot_ref, k_ref, gp_ref, nw_ref = (next(it) for _ in range(6))
    if has_init:
        c0_ref, n0_ref, m0_ref = next(it), next(it), next(it)
    if write_state and not first_layer:
        next(it), next(it), next(it)
    hm_ref = next(it)
    if write_state:
        cout_ref, nout_ref, mout_ref = next(it), next(it), next(it)
    ct_scr, n_scr, part_scr, inc_scr, ninc_scr, rows_scr, cst_scr, nst_scr, coef_scr = (next(it) for _ in range(9))

    L = CHUNK
    grp = min(MAX_GROUP, ns * nc)
    row = _iota((L, L), 0)
    col = _iota((L, L), 1)
    k_scale = HEAD_DIM ** -0.5
    b_idx = pl.program_id(0)
    h_idx = pl.program_id(1)
    n_heads = pl.num_programs(1)

    def gates(d, c):
        g = gp_ref[c]
        ig = g[d:d + 1, :]
        brow = g[2 + d:3 + d, :]
        blast = brow[:, L - 1:L] if d == 0 else brow[:, 0:1]
        return ig, brow, blast

    def local_group(g, carry):
        cs = [g * grp + u for u in range(grp)]
        pairs = [(u, d) for u in range(grp) for d in range(2)]
        qt = [qt_ref[c] for c in cs]
        vt = [vt_ref[c] for c in cs]
        k = [k_ref[pl.ds(_chunk_off(c), L), :] for c in cs]
        qk = [_dot(k[u], qt[u]) for u in range(grp)]
        vf = [v.astype(F32) for v in vt]
        gts = {(u, d): gates(d, cs[u]) for u, d in pairs}
        for u, d in pairs:
            c = cs[u]
            ig, brow, blast = gts[u, d]
            wj = blast - brow + ig
            mloc2 = jnp.max(wj, axis=1, keepdims=True)
            e = jnp.exp(wj - mloc2) * k_scale
            inc_scr[d, c] = _dot((vf[u] * e).astype(BF16), k[u])
            ninc_scr[d, c] = _dot3(jnp.broadcast_to(e, (8, L)), k[u])
            rows_scr[d, c, 2:3, :] = jnp.broadcast_to(mloc2, (1, L))
            rows_scr[d, c, 4:5, :] = jnp.broadcast_to(blast, (1, L))
        for u, d in pairs:
            c = cs[u]
            ig, brow, blast = gts[u, d]
            key_term = jnp.broadcast_to(ig - brow, (L, L)).T
            causal = (row <= col) if d == 0 else (row >= col)
            dm = jnp.where(causal, key_term + brow, -jnp.inf)
            mloc = jnp.max(dm, axis=0, keepdims=True)
            s = qk[u] * (jnp.exp(dm - mloc) * k_scale)
            part_scr[d, c] = _dot(vt[u], s.astype(BF16))
            rows_scr[d, c, 0:1, :] = jnp.sum(s, axis=0, keepdims=True)
            rows_scr[d, c, 1:2, :] = mloc
        return carry

    _group_loop(ns * nc // grp, local_group)

    def m_step(d, c, m):
        blast = rows_scr[d, c, 4:5, :]
        mloc2 = rows_scr[d, c, 2:3, :]
        rows_scr[d, c, 3:4, :] = m
        m_new = jnp.maximum(blast + m, mloc2)
        decay = jnp.exp(blast + m - m_new)
        w_inc = jnp.exp(mloc2 - m_new)
        coef_scr[d, c, 0:1, :] = jnp.concatenate([decay] * (HEAD_DIM // L), axis=1)
        coef_scr[d, c, 1:2, :] = jnp.concatenate([w_inc] * (HEAD_DIM // L), axis=1)
        return m_new

    for sq in range(ns):
        c_lo = sq * nc
        m_init = []
        for d in range(2):
            if has_init:
                ct_scr[d] = c0_ref[d].T
                n_scr[d] = jnp.broadcast_to(n0_ref[d], n_scr.shape[1:])
                m_init.append(jnp.full((1, L), m0_ref[(b_idx * 2 + d) * n_heads + h_idx], F32))
            else:
                ct_scr[d] = jnp.zeros(ct_scr.shape[1:], F32)
                n_scr[d] = jnp.zeros(n_scr.shape[1:], F32)
                m_init.append(jnp.zeros((1, L), F32))

        def m_body(ci, ms, c_lo=c_lo):
            return m_step(0, c_lo + ci, ms[0]), m_step(1, c_lo + nc - 1 - ci, ms[1])

        m_fin = _scan_loop(nc, m_body, tuple(m_init))

        def body(ci, carry, c_lo=c_lo):
            for d, c in ((0, c_lo + ci), (1, c_lo + nc - 1 - ci)):
                decay = coef_scr[d, c, 0:1, :]
                w_inc = coef_scr[d, c, 1:2, :]
                n8 = n_scr[d]
                nst_scr[d, c] = n8
                for r0 in range(0, HEAD_DIM, STRIP):
                    ct = ct_scr[d, r0:r0 + STRIP, :]
                    cst_scr[d, c, r0:r0 + STRIP, :] = ct.astype(BF16)
                    ct_scr[d, r0:r0 + STRIP, :] = decay * ct + w_inc * inc_scr[d, c, r0:r0 + STRIP, :]
                n_scr[d] = decay * n8 + w_inc * ninc_scr[d, c]
            return carry

        _scan_loop(nc, body, 0)

        if write_state:
            lsel = (lambda d, sq=sq: (sq, 0, d)) if first_layer else (lambda d, sq=sq: (sq, d))
            for d in range(2):
                cout_ref[lsel(d)] = ct_scr[d].T
                nout_ref[lsel(d)] = n_scr[d][0:1, :]
                mout_ref[lsel(d)] = m_fin[d]
            if first_layer:
                for l in range(1, cout_ref.shape[1]):
                    cout_ref[sq, l] = jnp.zeros(cout_ref.shape[2:], F32)
                    nout_ref[sq, l] = jnp.zeros(nout_ref.shape[2:], F32)
                    mout_ref[sq, l] = jnp.zeros(mout_ref.shape[2:], F32)

    def finish_group(g, carry):
        cs = [g * grp + u for u in range(grp)]
        qt = [qt_ref[c] for c in cs]
        cq = [[_dot(cst_scr[d, c], qt[u]) for d in range(2)] for u, c in enumerate(cs)]
        nq = [[_dot(nst_scr[d, c].astype(BF16), qt[u])[0:1, :] for d in range(2)] for u, c in enumerate(cs)]
        for u, c in enumerate(cs):
            coef = []
            for d in range(2):
                _, brow, _ = gates(d, c)
                den_loc = rows_scr[d, c, 0:1, :]
                mloc = rows_scr[d, c, 1:2, :]
                inter = brow + rows_scr[d, c, 3:4, :]
                m_row = jnp.maximum(mloc, inter)
                w_loc = jnp.exp(mloc - m_row)
                w_inter = jnp.exp(inter - m_row)
                den = w_loc * den_loc + w_inter * nq[u][d]
                r_den = 1.0 / jnp.maximum(jnp.abs(den), jnp.exp(-m_row))
                coef.append((w_loc * r_den, w_inter * r_den))
            ssq = jnp.zeros((1, L), F32)
            for r0 in range(0, HEAD_DIM, STRIP):
                h_s = None
                for d in range(2):
                    h_d = coef[d][0] * part_scr[d, c, r0:r0 + STRIP, :] + coef[d][1] * cq[u][d][r0:r0 + STRIP, :]
                    h_s = h_d if h_s is None else h_s + h_d
                part_scr[0, c, r0:r0 + STRIP, :] = h_s
                ssq = ssq + jnp.sum(h_s * h_s, axis=0, keepdims=True)
            r_norm = lax.rsqrt(ssq * (1.0 / HEAD_DIM) + EPS)
            for r0 in range(0, HEAD_DIM, L):
                y = part_scr[0, c, r0:r0 + L, :] * r_norm * nw_ref[r0:r0 + L, :]
                o = jax.nn.sigmoid(ot_ref[c, r0:r0 + L, :].astype(F32)) * y
                hm_ref[pl.ds(_chunk_off(c), L), r0:r0 + L] = o.T.astype(hm_ref.dtype)
        return carry

    _group_loop(ns * nc // grp, finish_group)


def _state_specs(shapes, layer, first_layer, ns):
    specs = []
    for shp in shapes:
        tail = shp[4:]
        zeros = (0,) * len(tail)
        if first_layer:
            specs.append(pl.BlockSpec((ns, shp[1], 2, None) + tail, lambda b, h, z=zeros: (b, 0, 0, h) + z))
        else:
            specs.append(pl.BlockSpec((ns, None, 2, None) + tail, lambda b, h, z=zeros: (b, layer, 0, h) + z))
    return specs


def _mlstm_call(proj_t, proj_n, gp, nw_b, init, state_bufs, *, n_seq, seq_len, tok_off, n_heads, layer, depth):
    dk = HEAD_DIM
    nc = seq_len // CHUNK
    has_init = init is not None
    ns = 1 if has_init else _seqs_per_step(n_seq, nc)
    nct = ns * nc
    boff = tok_off // (ns * seq_len)
    write_state = not has_init
    first_layer = layer == 0
    args = [proj_t, proj_t, proj_t, proj_n, gp, nw_b]
    specs = [
        pl.BlockSpec((nct, dk, CHUNK), lambda b, h: (b + boff, h, 0)),
        pl.BlockSpec((nct, dk, CHUNK), lambda b, h: (b + boff, n_heads + h, 0)),
        pl.BlockSpec((nct, dk, CHUNK), lambda b, h: (b + boff, 2 * n_heads + h, 0)),
        pl.BlockSpec((ns * seq_len, dk), lambda b, h: (b + boff, h)),
        pl.BlockSpec((None, nct, 4, CHUNK), lambda b, h: (h, b + boff, 0, 0)),
        pl.BlockSpec((dk, LANES), lambda b, h: (h, 0)),
    ]
    aliases = {}
    if has_init:
        c0, n0, m0 = init
        args += [c0, n0[:, layer][:, :, :, None, :], m0[:, layer].reshape(-1)]
        specs += [
            pl.BlockSpec((None, None, 2, None, dk, dk), lambda b, h: (b, layer, 0, h, 0, 0)),
            pl.BlockSpec((None, 2, None, 1, dk), lambda b, h: (b, 0, h, 0, 0)),
            pl.BlockSpec(memory_space=pltpu.SMEM),
        ]
    out_shape = [jax.ShapeDtypeStruct((n_seq * seq_len, n_heads * dk), BF16)]
    out_specs = [pl.BlockSpec((ns * seq_len, dk), lambda b, h: (b, h))]
    if write_state:
        shapes = [(n_seq, depth, 2, n_heads, dk, dk), (n_seq, depth, 2, n_heads, 1, dk),
                  (n_seq, depth, 2, n_heads, 1, LANES)]
        if not first_layer:
            for k_, buf in enumerate(state_bufs):
                aliases[len(args)] = 1 + k_
                args.append(buf)
                specs.append(pl.BlockSpec(memory_space=pl.ANY))
        out_shape += [jax.ShapeDtypeStruct(s, F32) for s in shapes]
        out_specs += _state_specs(shapes, layer, first_layer, ns)
    kern = functools.partial(_mlstm_kernel, nc=nc, ns=ns, has_init=has_init, first_layer=first_layer,
                             write_state=write_state)
    return pl.pallas_call(
        kern, grid=(n_seq // ns, n_heads), in_specs=specs, out_specs=out_specs, out_shape=out_shape,
        input_output_aliases=aliases,
        scratch_shapes=[pltpu.VMEM((2, dk, dk), F32), pltpu.VMEM((2, 8, dk), F32),
                        pltpu.VMEM((2, nct, dk, CHUNK), F32), pltpu.VMEM((2, nct, dk, dk), F32),
                        pltpu.VMEM((2, nct, 8, dk), F32), pltpu.VMEM((2, nct, 8, CHUNK), F32),
                        pltpu.VMEM((2, nct, dk, dk), BF16), pltpu.VMEM((2, nct, 8, dk), F32),
                        pltpu.VMEM((2, nct, 8, dk), F32)],
        compiler_params=_cparams(("parallel", "parallel"), 48), name="mlstm_mixer",
    )(*args)


def _ret_kernel(*refs, nc, ns, has_init, first_layer, write_state):
    it = iter(refs)
    dl_ref, qt_ref, vt_ref, gt_ref, k_ref, nw_ref = (next(it) for _ in range(6))
    if has_init:
        s0_ref = next(it)
    if write_state and not first_layer:
        next(it)
    hr_ref = next(it)
    if write_state:
        sout_ref = next(it)
    st_scr, part_scr, inc_scr, sst_scr = next(it), next(it), next(it), next(it)

    L = CHUNK
    grp = min(MAX_GROUP, ns * nc)
    rowi = _iota((L, L), 0)
    coli = _iota((L, L), 1)
    k_scale = HEAD_DIM ** -0.5
    h_idx = pl.program_id(1)
    n_heads = pl.num_programs(1)

    consts = []
    for d in range(2):
        lg = _log_sigmoid(jnp.full((1, 1), dl_ref[d * n_heads + h_idx], F32))
        rel = (coli - rowi if d == 0 else rowi - coli).astype(F32)
        dmat_t = jnp.where(rel >= 0, jnp.exp(jnp.maximum(rel, 0.0) * lg), 0.0) * k_scale
        pos = _iota((1, L), 1).astype(F32)
        if d == 1:
            pos = (L - 1.0) - pos
        q_decay = jnp.exp((pos + 1.0) * lg)
        k_decay = jnp.exp((L - 1.0 - pos) * lg) * k_scale
        chunk_decay = jnp.exp(float(L) * lg)
        consts.append((dmat_t, q_decay, k_decay, chunk_decay))

    def local_group(g, carry):
        cs = [g * grp + u for u in range(grp)]
        qt = [qt_ref[c] for c in cs]
        vt = [vt_ref[c] for c in cs]
        k = [k_ref[pl.ds(_chunk_off(c), L), :] for c in cs]
        qk = [_dot(k[u], qt[u]) for u in range(grp)]
        for u, c in enumerate(cs):
            vf = vt[u].astype(F32)
            for d in range(2):
                inc_scr[d, c] = _dot((vf * consts[d][2]).astype(BF16), k[u])
        for u, c in enumerate(cs):
            for d in range(2):
                part_scr[d, c] = _dot(vt[u], (qk[u] * consts[d][0]).astype(BF16))
        return carry

    _group_loop(ns * nc // grp, local_group)

    for sq in range(ns):
        c_lo = sq * nc
        for d in range(2):
            if has_init:
                st_scr[d] = s0_ref[d].T
            else:
                st_scr[d] = jnp.zeros(st_scr.shape[1:], F32)

        def body(ci, carry, c_lo=c_lo):
            for d, c in ((0, c_lo + ci), (1, c_lo + nc - 1 - ci)):
                for r0 in range(0, HEAD_DIM, STRIP):
                    st = st_scr[d, r0:r0 + STRIP, :]
                    sst_scr[d, c, r0:r0 + STRIP, :] = st.astype(BF16)
                    st_scr[d, r0:r0 + STRIP, :] = consts[d][3] * st + inc_scr[d, c, r0:r0 + STRIP, :]
            return carry

        _scan_loop(nc, body, 0)

        if write_state:
            for d in range(2):
                if first_layer:
                    sout_ref[sq, 0, d] = st_scr[d].T
                else:
                    sout_ref[sq, d] = st_scr[d].T
            if first_layer:
                for l in range(1, sout_ref.shape[1]):
                    sout_ref[sq, l] = jnp.zeros(sout_ref.shape[2:], F32)

    def finish_group(g, carry):
        cs = [g * grp + u for u in range(grp)]
        qt = [qt_ref[c] for c in cs]
        sq = [[_dot(sst_scr[d, c], qt[u]) for d in range(2)] for u, c in enumerate(cs)]
        for u, c in enumerate(cs):
            ssq = jnp.zeros((1, L), F32)
            for r0 in range(0, HEAD_DIM, STRIP):
                sl = slice(r0, r0 + STRIP)
                o_s = ((part_scr[0, c, sl, :] + part_scr[1, c, sl, :])
                       + (consts[0][1] * sq[u][0][sl, :] + consts[1][1] * sq[u][1][sl, :]))
                part_scr[0, c, sl, :] = o_s
                ssq = ssq + jnp.sum(o_s * o_s, axis=0, keepdims=True)
            r_norm = lax.rsqrt(ssq * (1.0 / HEAD_DIM) + EPS)
            for r0 in range(0, HEAD_DIM, L):
                y = part_scr[0, c, r0:r0 + L, :] * r_norm * nw_ref[r0:r0 + L, :]
                rg = gt_ref[c, r0:r0 + L, :].astype(F32)
                hr_ref[pl.ds(_chunk_off(c), L), r0:r0 + L] = (rg * jax.nn.sigmoid(rg) * y).T.astype(hr_ref.dtype)
        return carry

    _group_loop(ns * nc // grp, finish_group)


def _ret_call(proj_t, proj_n, decay_logit, nw_b, init, state_buf, *, n_seq, seq_len, tok_off, n_heads, layer,
              depth):
    dk = HEAD_DIM
    nc = seq_len // CHUNK
    has_init = init is not None
    ns = 1 if has_init else _seqs_per_step(n_seq, nc)
    nct = ns * nc
    boff = tok_off // (ns * seq_len)
    write_state = not has_init
    first_layer = layer == 0
    args = [decay_logit.reshape(-1), proj_t, proj_t, proj_t, proj_n, nw_b]
    specs = [
        pl.BlockSpec(memory_space=pltpu.SMEM),
        pl.BlockSpec((nct, dk, CHUNK), lambda b, h: (b + boff, 3 * n_heads + h, 0)),
        pl.BlockSpec((nct, dk, CHUNK), lambda b, h: (b + boff, 4 * n_heads + h, 0)),
        pl.BlockSpec((nct, dk, CHUNK), lambda b, h: (b + boff, 5 * n_heads + h, 0)),
        pl.BlockSpec((ns * seq_len, dk), lambda b, h: (b + boff, n_heads + h)),
        pl.BlockSpec((dk, LANES), lambda b, h: (h, 0)),
    ]
    aliases = {}
    if has_init:
        args.append(init)
        specs.append(pl.BlockSpec((None, None, 2, None, dk, dk), lambda b, h: (b, layer, 0, h, 0, 0)))
    out_shape = [jax.ShapeDtypeStruct((n_seq * seq_len, n_heads * dk), BF16)]
    out_specs = [pl.BlockSpec((ns * seq_len, dk), lambda b, h: (b, h))]
    if write_state:
        shapes = [(n_seq, depth, 2, n_heads, dk, dk)]
        if not first_layer:
            aliases[len(args)] = 1
            args.append(state_buf)
            specs.append(pl.BlockSpec(memory_space=pl.ANY))
        out_shape += [jax.ShapeDtypeStruct(s, F32) for s in shapes]
        out_specs += _state_specs(shapes, layer, first_layer, ns)
    kern = functools.partial(_ret_kernel, nc=nc, ns=ns, has_init=has_init, first_layer=first_layer,
                             write_state=write_state)
    return pl.pallas_call(
        kern, grid=(n_seq // ns, n_heads), in_specs=specs, out_specs=out_specs, out_shape=out_shape,
        input_output_aliases=aliases,
        scratch_shapes=[pltpu.VMEM((2, dk, dk), F32), pltpu.VMEM((2, nct, dk, CHUNK), F32),
                        pltpu.VMEM((2, nct, dk, dk), F32), pltpu.VMEM((2, nct, dk, dk), BF16)],
        compiler_params=_cparams(("parallel", "parallel"), 48), name="retention_mixer",
    )(*args)


def _merge_kernel(hmp_ref, hms_ref, hrp_ref, hrs_ref, ga_ref, gb_ref, x_ref, mod_ref, nw_ref, wa_ref, wb_ref,
                  wo_ref, wr_ref, x1_ref, h2_ref, aff_ref, wab_scr, wbb_scr, wob_scr, *, n_prompt_tiles):
    i = pl.program_id(0)

    @pl.when(i == 0)
    def _():
        wab_scr[...] = wa_ref[...].astype(BF16)
        wbb_scr[...] = wb_ref[...].astype(BF16)
        wob_scr[...] = wo_ref[...].astype(BF16)

    is_prompt = i < n_prompt_tiles
    hm = jnp.where(is_prompt, hmp_ref[...], hms_ref[...])
    hr = jnp.where(is_prompt, hrp_ref[...], hrs_ref[...])
    ya = _dot(hm, wab_scr[...])
    yb = _dot(hr, wbb_scr[...])
    merged = (jax.nn.sigmoid(ga_ref[...].astype(F32)) * ya + jax.nn.sigmoid(gb_ref[...].astype(F32)) * yb)
    y = _dot(merged.astype(BF16), wob_scr[...])
    x1 = x_ref[...] + mod_ref[2:3, :] * y
    x1_ref[...] = x1
    h2 = x1 * lax.rsqrt(jnp.mean(x1 * x1, axis=-1, keepdims=True) + EPS) * nw_ref[...]
    h2 = h2 * (1.0 + mod_ref[4:5, :]) + mod_ref[3:4, :]
    h2_ref[...] = h2
    logits = _dot_nt(wr_ref[...].astype(BF16), h2.astype(BF16))
    p = jnp.exp(logits - jnp.max(logits, axis=0, keepdims=True))
    aff_ref[...] = p / jnp.sum(p, axis=0, keepdims=True)


def _merge_call(hm_p, hm_s, hr_p, hr_s, proj_n, x, mod, norm_w, w_a, w_b, w_out, router_wt, layer, *,
                n_prompt, dec_seq, tm=512):
    n, d = x.shape
    ne = router_wt.shape[1]
    npt = n_prompt // tm
    grp = functools.partial(_group_of_tile, tm=tm, n_prompt=n_prompt, dec_seq=dec_seq)
    tile = pl.BlockSpec((tm, d), lambda i: (i, 0))
    tile_p = pl.BlockSpec((tm, d), lambda i: (jnp.minimum(i, npt - 1), 0))
    tile_s = pl.BlockSpec((tm, d), lambda i: (jnp.maximum(i - npt, 0), 0))
    full = pl.BlockSpec((None, d, d), lambda i: (layer, 0, 0))
    return pl.pallas_call(
        functools.partial(_merge_kernel, n_prompt_tiles=npt), grid=(n // tm,),
        in_specs=[
            tile_p, tile_s, tile_p, tile_s,
            pl.BlockSpec((tm, d), lambda i: (i, 2)),
            pl.BlockSpec((tm, d), lambda i: (i, 3)),
            tile,
            pl.BlockSpec((None, 6, d), lambda i: (grp(i), 0, 0)),
            pl.BlockSpec((1, d), lambda i: (0, 0)),
            full, full, full,
            pl.BlockSpec((None, ne, d), lambda i: (layer, 0, 0)),
        ],
        out_specs=[tile, tile, pl.BlockSpec((ne, tm), lambda i: (0, i))],
        out_shape=[jax.ShapeDtypeStruct((n, d), F32), jax.ShapeDtypeStruct((n, d), F32),
                   jax.ShapeDtypeStruct((ne, n), F32)],
        scratch_shapes=[pltpu.VMEM((d, d), BF16)] * 3,
        compiler_params=_cparams(("arbitrary",), 56), name="merge_out_router",
    )(hm_p, hm_s, hr_p, hr_s, proj_n, proj_n, x, mod, norm_w.reshape(1, d), w_a, w_b, w_out, router_wt)


ROUTE_GROUP = 4


def _route_kernel(a_ref, at_ref, idx_ref, gate_ref, thr_scr, *, cap, n_tok):
    n_sets = a_ref.shape[0]
    nb = n_tok // LANES
    a_all = a_ref[...]

    def as_f32(bits):
        return lax.bitcast_convert_type(bits, F32)

    def count_ge(cand):
        m = jnp.where(a_all >= cand, 1.0, 0.0)
        return jnp.sum(jnp.sum(m, axis=1, keepdims=True), axis=2, keepdims=True)

    def bit_step(k, thr):
        cand = thr | lax.shift_left(jnp.int32(1), 30 - k)
        return jnp.where(count_ge(as_f32(cand)) >= cap, cand, thr)

    thr_scr[...] = lax.fori_loop(0, 31, bit_step, jnp.zeros((n_sets, 1, 1), jnp.int32))

    r128 = _iota((LANES, LANES), 0)
    c128 = _iota((LANES, LANES), 1)
    upper = jnp.where(r128 <= c128, 1.0, 0.0).astype(BF16)
    lower_t = jnp.where(r128 >= c128, 1.0, 0.0).astype(BF16)
    rb = _iota((nb, nb), 0)
    cb = _iota((nb, nb), 1)
    blk_before_rows = jnp.where(cb < rb, 1.0, 0.0).astype(BF16)
    blk_before_cols = jnp.where(rb < cb, 1.0, 0.0).astype(BF16)

    def incl_counts(mask):
        within = _dot(mask.astype(BF16), upper)
        before = _dot(blk_before_rows, within.astype(BF16))[:, LANES - 1:LANES]
        return within + before

    def incl_counts_t(mask_t):
        within = _dot(lower_t, mask_t.astype(BF16))
        before = _dot(within.astype(BF16), blk_before_cols)[LANES - 1:LANES, :]
        return within + before

    slot = _iota((1, cap), 1).astype(F32)
    blk_col = _iota((nb, 1), 0).astype(F32)
    sub_col = _iota((LANES, 1), 0).astype(F32)

    def per_group(g, carry):
        sets = [g + u * (n_sets // ROUTE_GROUP) for u in range(ROUTE_GROUP)]
        stage = []
        for s in sets:
            thr_bits = thr_scr[s]
            thr = as_f32(thr_bits)
            nxt = as_f32(thr_bits + 1)
            a = a_ref[s]
            a_t = at_ref[s]
            gt = jnp.where(a >= nxt, 1.0, 0.0)
            eq = jnp.where((a >= thr) & (a < nxt), 1.0, 0.0)
            gt_t = jnp.where(a_t >= nxt, 1.0, 0.0)
            eq_t = jnp.where((a_t >= thr) & (a_t < nxt), 1.0, 0.0)
            n_gt = jnp.sum(jnp.sum(gt, axis=1, keepdims=True), axis=0, keepdims=True)
            stage.append((a_t, gt, eq, gt_t, eq_t, cap - n_gt))
        ties = [(incl_counts(eq), incl_counts_t(eq_t)) for _, _, eq, _, eq_t, _ in stage]
        sels = []
        for (a_t, gt, eq, gt_t, eq_t, need), (c_eq, c_eq_t) in zip(stage, ties):
            sels.append((gt + eq * jnp.where(c_eq - eq < need, 1.0, 0.0),
                         gt_t + eq_t * jnp.where(c_eq_t - eq_t < need, 1.0, 0.0)))
        cnts = [(incl_counts(sel), incl_counts_t(sel_t)) for sel, sel_t in sels]
        picks = []
        for (a_t, *_), (cnt, cnt_t) in zip(stage, cnts):
            blk_end = cnt[:, LANES - 1:LANES]
            blk_of_slot = jnp.sum(jnp.where(blk_end <= slot, 1.0, 0.0), axis=0, keepdims=True)
            onehot_blk = jnp.where(blk_col == blk_of_slot, 1.0, 0.0).astype(BF16)
            cnt_rows = _dot3(cnt_t, onehot_blk)
            a_rows = _dot3(a_t, onehot_blk)
            picks.append((blk_of_slot, cnt_rows, a_rows))
        for s, (blk_of_slot, cnt_rows, a_rows) in zip(sets, picks):
            sub_of_slot = jnp.sum(jnp.where(cnt_rows <= slot, 1.0, 0.0), axis=0, keepdims=True)
            gate = jnp.sum(jnp.where(sub_col == sub_of_slot, a_rows, 0.0), axis=0, keepdims=True)
            idx_ref[s] = (blk_of_slot * LANES + sub_of_slot).astype(jnp.int32)
            gate_ref[s] = gate
        return carry

    lax.fori_loop(0, n_sets // ROUTE_GROUP, per_group, 0)


def _route_call(aff_t, n_pass, cap):
    ne, n = aff_t.shape
    n_tok = n // n_pass
    nb = n_tok // LANES
    a4 = aff_t.reshape(ne, n_pass, nb, LANES).transpose(1, 0, 2, 3).reshape(n_pass * ne, nb, LANES)
    a4_t = a4.transpose(0, 2, 1)
    n_sets = n_pass * ne
    return pl.pallas_call(
        functools.partial(_route_kernel, cap=cap, n_tok=n_tok),
        out_shape=[jax.ShapeDtypeStruct((n_sets, 1, cap), jnp.int32),
                   jax.ShapeDtypeStruct((n_sets, 1, cap), F32)],
        scratch_shapes=[pltpu.VMEM((n_sets, 1, 1), jnp.int32)],
        compiler_params=pltpu.CompilerParams(vmem_limit_bytes=32 * MIB), name="expert_choice_route",
    )(a4, a4_t)


ROW_LOOP_UNROLL = 8


def _moe_kernel(idx_ref, gate_ref, h_hbm, w1_ref, w3_ref, w2_ref, out_hbm,
                xe_scr, xb_scr, ye_a, ye_b, acc_scr, gsem, osem, *, rows, rows_pad):
    e = pl.program_id(0)
    f = pl.program_id(1)
    n_e = pl.num_programs(0)
    n_f = pl.num_programs(1)
    chunk = rows_pad // n_f

    def gather_row(slot, s):
        tok = idx_ref[slot * rows_pad + s]
        pltpu.make_async_copy(h_hbm.at[pl.ds(tok, 1), :], xe_scr.at[pl.ds(s, 1), :], gsem).start()

    def scatter_row(slot, s, ye_ref):
        tok = idx_ref[slot * rows_pad + s]
        g = gate_ref[slot * rows_pad + s]
        acc_scr[pl.ds(tok, 1), :] += ye_ref[pl.ds(s, 1), :] * g

    def wait_gather():
        pltpu.make_async_copy(h_hbm.at[pl.ds(0, rows_pad), :], xe_scr, gsem).wait()

    @pl.when((e == 0) & (f == 0))
    def _():
        acc_scr[...] = jnp.zeros_like(acc_scr)
        ye_a[...] = jnp.zeros_like(ye_a)
        ye_b[...] = jnp.zeros_like(ye_b)

        def issue(s, carry):
            gather_row(1, s)
            return carry
        lax.fori_loop(0, rows_pad, issue, 0, unroll=ROW_LOOP_UNROLL)

    def step(ye_cur, ye_prev):
        @pl.when(f == 0)
        def _():
            wait_gather()
            xb_scr[...] = xe_scr[0:rows, :].astype(BF16)
            ye_cur[0:rows, :] = jnp.zeros((rows, ye_cur.shape[1]), F32)

        xb = xb_scr[...]
        h1 = _dot(xb, w1_ref[...].astype(BF16))
        h3 = _dot(xb, w3_ref[...].astype(BF16))
        he = (h1 * jax.nn.sigmoid(h1) * h3).astype(BF16)
        ye_cur[0:rows, :] += _dot(he, w2_ref[...].astype(BF16))
        base = f * chunk
        for r in range(chunk):
            gather_row(e + 2, base + r)
        for r in range(chunk):
            scatter_row(e, base + r, ye_prev)

        @pl.when((e == n_e - 1) & (f == n_f - 1))
        def _():
            def scatter(s, carry):
                scatter_row(n_e, s, ye_cur)
                return carry
            lax.fori_loop(0, rows, scatter, 0, unroll=ROW_LOOP_UNROLL)
            wait_gather()
            cp = pltpu.make_async_copy(acc_scr, out_hbm, osem)
            cp.start()
            cp.wait()

    @pl.when(e % 2 == 0)
    def _():
        step(ye_a, ye_b)

    @pl.when(e % 2 == 1)
    def _():
        step(ye_b, ye_a)


def _moe_call(idx, gate, h2, w1, w3, w2, layer, tf=256):
    n, d = h2.shape
    ne, rows = idx.shape
    dff = w1.shape[3]
    nf = dff // tf
    rows_pad = -(-rows // (8 * nf)) * 8 * nf
    pad = ((1, 1), (0, rows_pad - rows))
    idx_all = jnp.pad(idx, pad).reshape(-1)
    gate_all = jnp.pad(gate, pad).reshape(-1)
    grid_spec = pltpu.PrefetchScalarGridSpec(
        num_scalar_prefetch=1,
        grid=(ne, nf),
        in_specs=[
            pl.BlockSpec(memory_space=pltpu.SMEM),
            pl.BlockSpec(memory_space=pl.ANY),
            pl.BlockSpec((None, None, d, tf), lambda e, f, idx: (layer, e, 0, f)),
            pl.BlockSpec((None, None, d, tf), lambda e, f, idx: (layer, e, 0, f)),
            pl.BlockSpec((None, None, tf, d), lambda e, f, idx: (layer, e, f, 0)),
        ],
        out_specs=pl.BlockSpec(memory_space=pl.ANY),
        scratch_shapes=[
            pltpu.VMEM((rows_pad, d), F32), pltpu.VMEM((rows, d), BF16),
            pltpu.VMEM((rows_pad, d), F32), pltpu.VMEM((rows_pad, d), F32),
            pltpu.VMEM((n, d), F32), pltpu.SemaphoreType.DMA, pltpu.SemaphoreType.DMA,
        ],
    )
    return pl.pallas_call(
        functools.partial(_moe_kernel, rows=rows, rows_pad=rows_pad), grid_spec=grid_spec,
        out_shape=jax.ShapeDtypeStruct((n, d), F32),
        compiler_params=_cparams(("arbitrary", "arbitrary"), 62), name="expert_ffn",
    )(idx_all, gate_all, h2, w1, w3, w2)


def _rope_tables(t):
    rows = t // GRID_W
    row = jnp.repeat(jnp.arange(rows, dtype=F32), GRID_W)
    colp = jnp.tile(jnp.arange(GRID_W, dtype=F32), rows)
    n_freq = HEAD_DIM // 4
    inv = ROPE_BASE ** (-jnp.arange(n_freq, dtype=F32) / n_freq)
    ang = jnp.concatenate([row[:, None] * inv, colp[:, None] * inv], axis=-1)
    return jnp.cos(ang), jnp.sin(ang)


def kernel(x_prompt, x_sample, c, state_mlstm_C, state_mlstm_n, state_mlstm_m, state_ret_S, c_ctx, w_mod, b_mod,
           norm1_w, norm2_w, w_in, mlstm_if_b, mlstm_norm_w, ret_decay_logit, ret_norm_w, w_branch_a, w_branch_b,
           w_out, router_w, ffn_w1, ffn_w3, ffn_w2, final_norm_w):
    bp, seq, d = x_prompt.shape
    db, dec_seq, _ = x_sample.shape
    depth = w_mod.shape[0]
    hm = mlstm_if_b.shape[-1]
    hr = ret_decay_logit.shape[-1]
    ne = router_w.shape[-1]
    n_prompt = bp * seq
    n_sample = db * dec_seq
    assert n_prompt == n_sample and hm * HEAD_DIM == d and hr * HEAD_DIM == d and hm == hr
    cap = EC_FACTOR * n_prompt // ne
    tm = 512
    tp = 2048
    geo = dict(n_prompt=n_prompt, dec_seq=dec_seq)

    cond8 = jnp.concatenate([c_ctx[None, :], c, jnp.zeros((8 - 1 - db, d), F32)], axis=0)
    mod = _mod_call(cond8, w_mod, b_mod)[:, :1 + db].reshape(depth, 1 + db, 6, d)

    n_m = 4 * d
    g0 = n_m + 4 * hm

    w_t = jnp.swapaxes(w_in, 1, 2)
    mq, mk, mv, mo = (k * d for k in range(4))
    rq, rk, rv, rg, ga, gb = (g0 + k * d for k in range(6))
    wg_t = w_t[:, n_m:g0]
    gate_bias = jnp.transpose(mlstm_if_b, (0, 2, 1, 3)).reshape(depth, 4 * hm, 1)
    router_wt = jnp.swapaxes(router_w, 1, 2)
    nw_m = jnp.broadcast_to(mlstm_norm_w[:, :, None], (depth, d, LANES))
    nw_r = jnp.broadcast_to(ret_norm_w[:, :, None], (depth, d, LANES))

    cos, sin = _rope_tables(dec_seq)
    cos_t = cos.T.reshape(HEAD_DIM // 2, dec_seq // tp, tp).transpose(1, 0, 2)
    sin_t = sin.T.reshape(HEAD_DIM // 2, dec_seq // tp, tp).transpose(1, 0, 2)

    x, h = _resid_norm_call(x_prompt.reshape(n_prompt, d), None, None, mod[0], norm1_w[0], gate_row=0,
                            mod_rows=(0, 1), h_dtype=BF16, write_x=True, x_tail=x_sample.reshape(n_sample, d),
                            **geo)
    m_bufs, s_buf = None, None
    y_prompt = y_sample = None
    for l in range(depth):
        proj_t = _proj_t_call(h, w_t, l, (mq, mv, mo, rq, rv, rg), (cos_t, sin_t), 3, tm=tp, **geo)
        proj_n = _proj_call(h, w_t, l, (mk, rk, ga, gb), (cos, sin), 1, tm=tp, **geo)
        gp = _gate_call(h, wg_t[l], gate_bias[l], hm, tm=tm)

        mix = dict(n_heads=hm, layer=l, depth=depth)
        hm_p, *m_bufs = _mlstm_call(proj_t, proj_n, gp, nw_m[l], None, m_bufs, n_seq=bp, seq_len=seq, tok_off=0,
                                    **mix)
        (hm_s,) = _mlstm_call(proj_t, proj_n, gp, nw_m[l], (state_mlstm_C, state_mlstm_n, state_mlstm_m), None,
                              n_seq=db, seq_len=dec_seq, tok_off=n_prompt, **mix)
        hr_p, s_buf = _ret_call(proj_t, proj_n, ret_decay_logit[l], nw_r[l], None, s_buf, n_seq=bp, seq_len=seq,
                                tok_off=0, **mix)
        (hr_s,) = _ret_call(proj_t, proj_n, ret_decay_logit[l], nw_r[l], state_ret_S, None, n_seq=db,
                            seq_len=dec_seq, tok_off=n_prompt, **mix)

        x1, h2, aff_t = _merge_call(hm_p, hm_s, hr_p, hr_s, proj_n, x, mod[l], norm2_w[l], w_branch_a, w_branch_b,
                                    w_out, router_wt, l, tm=tm, **geo)
        idx, gate = _route_call(aff_t, 2, cap)
        idx = idx.reshape(2, ne, cap) + (jnp.arange(2, dtype=jnp.int32) * n_prompt)[:, None, None]
        idx = jnp.swapaxes(idx, 0, 1).reshape(ne, 2 * cap)
        gate = jnp.swapaxes(gate.reshape(2, ne, cap), 0, 1).reshape(ne, 2 * cap)
        moe = _moe_call(idx, gate, h2, ffn_w1, ffn_w3, ffn_w2, l)
        if l + 1 < depth:
            x, h = _resid_norm_call(x1, moe, mod[l], mod[l + 1], norm1_w[l + 1], gate_row=5, mod_rows=(0, 1),
                                    h_dtype=BF16, write_x=True, **geo)
        else:
            fin = dict(gate_row=5, mod_rows=None, h_dtype=F32, write_x=False, **geo)
            _, y_prompt = _resid_norm_call(x1, moe, mod[l], None, final_norm_w, row_off=0, n_rows=n_prompt, **fin)
            _, y_sample = _resid_norm_call(x1, moe, mod[l], None, final_norm_w, row_off=n_prompt, n_rows=n_sample,
                                           **fin)

    c_buf, n_buf, m_buf = m_bufs
    return (y_prompt.reshape(bp, seq, d), y_sample.reshape(db, dec_seq, d), c_buf, n_buf[:, :, :, :, 0, :],
            m_buf[:, :, :, :, 0, 0], s_buf)
```

```python
import functools

import numpy as np
import jax
import jax.numpy as jnp
from jax import lax
from jax.experimental import pallas as pl
from jax.experimental.pallas import tpu as pltpu

F32 = jnp.float32
BF16 = jnp.bfloat16

GRID_W = 64
CHUNK = 128
LANES = 128
HEAD_DIM = 256
N_EXPERTS = 16
EC_FACTOR = 2
ROPE_BASE = 10000.0
EPS = 1e-6
V7X_VMEM_BYTES = 64 * 1024 * 1024
MIB = 1024 * 1024


def _cparams(semantics, vmem_mib):
    assert vmem_mib * MIB < V7X_VMEM_BYTES
    return pltpu.CompilerParams(dimension_semantics=semantics, vmem_limit_bytes=vmem_mib * MIB)


def _dot(a, b):
    return jnp.dot(a, b, preferred_element_type=F32)


def _dot_nt(a, b):
    return lax.dot_general(a, b, (((1,), (1,)), ((), ())), preferred_element_type=F32)


def _log_sigmoid(x):
    return -(jnp.maximum(-x, 0.0) + jnp.log1p(jnp.exp(-jnp.abs(x))))


def _split3(x):
    hi = x.astype(BF16)
    r1 = x - hi.astype(F32)
    mid = r1.astype(BF16)
    lo = (r1 - mid.astype(F32)).astype(BF16)
    return hi, mid, lo


def _dot3(x, m):
    hi, mid, lo = _split3(x)
    return _dot(hi, m) + _dot(mid, m) + _dot(lo, m)


def _iota(shape, dim):
    return lax.broadcasted_iota(jnp.int32, shape, dim)


def _mod_kernel(cond_ref, w_ref, b_ref, out_ref):
    c = cond_ref[...]
    s = (c * jax.nn.sigmoid(c)).astype(BF16)
    out_ref[...] = _dot(s, w_ref[...].astype(BF16)) + b_ref[...]


def _mod_call(cond8, w_mod, b_mod):
    depth, d, w6 = w_mod.shape
    tn = 1536
    return pl.pallas_call(
        _mod_kernel,
        grid=(depth, w6 // tn),
        in_specs=[
            pl.BlockSpec((8, d), lambda l, j: (0, 0)),
            pl.BlockSpec((None, d, tn), lambda l, j: (l, 0, j)),
            pl.BlockSpec((None, 1, tn), lambda l, j: (l, 0, j)),
        ],
        out_specs=pl.BlockSpec((None, 8, tn), lambda l, j: (l, 0, j)),
        out_shape=jax.ShapeDtypeStruct((depth, 8, w6), F32),
        compiler_params=_cparams(("parallel", "parallel"), 32),
        name="adaln_mod",
    )(cond8, w_mod, b_mod.reshape(depth, 1, w6))


def _group_of_tile(i, tm, n_prompt, dec_seq):
    return jnp.maximum(i * tm - n_prompt + dec_seq, 0) // dec_seq


def _resid_norm_kernel(*refs, has_delta, has_mod, gate_row, mod_rows, write_x, n_first_tiles):
    it = iter(refs)
    x_ref = next(it)
    x2_ref = next(it) if n_first_tiles else None
    delta_ref = next(it) if has_delta else None
    mod_ref = next(it) if (has_delta or has_mod) else None
    nw_ref = next(it)
    xo_ref = next(it) if write_x else None
    h_ref = next(it)
    x = x_ref[...]
    if n_first_tiles:
        x = jnp.where(pl.program_id(0) < n_first_tiles, x, x2_ref[...])
    if has_delta:
        x = x + mod_ref[gate_row:gate_row + 1, :] * delta_ref[...]
    if write_x:
        xo_ref[...] = x
    y = x * lax.rsqrt(jnp.mean(x * x, axis=-1, keepdims=True) + EPS) * nw_ref[...]
    if has_mod:
        sh_row, sc_row = mod_rows
        y = y * (1.0 + mod_ref[sc_row:sc_row + 1, :]) + mod_ref[sh_row:sh_row + 1, :]
    h_ref[...] = y.astype(h_ref.dtype)


def _resid_norm_call(x, delta, mod_gate, mod_norm, norm_w, *, gate_row, mod_rows, h_dtype, write_x,
                     n_prompt, dec_seq, tm=512, row_off=0, n_rows=None, x_tail=None):
    n, d = x.shape
    has_delta = delta is not None
    has_mod = mod_norm is not None
    grp = functools.partial(_group_of_tile, tm=tm, n_prompt=n_prompt, dec_seq=dec_seq)
    if x_tail is None:
        n_rows = n if n_rows is None else n_rows
        toff = row_off // tm
        nft = 0
        tile_in = pl.BlockSpec((tm, d), lambda i: (i + toff, 0))
        args, specs = [x], [tile_in]
    else:
        assert row_off == 0 and n_rows is None and not has_delta
        n_rows, toff, nft = n + x_tail.shape[0], 0, n // tm
        args = [x, x_tail]
        specs = [pl.BlockSpec((tm, d), lambda i: (jnp.minimum(i, nft - 1), 0)),
                 pl.BlockSpec((tm, d), lambda i: (jnp.maximum(i - nft, 0), 0))]
    tile_out = pl.BlockSpec((tm, d), lambda i: (i, 0))
    if has_delta:
        args.append(delta)
        specs.append(tile_in)
    if has_delta or has_mod:
        mg = mod_gate if has_delta else mod_norm
        mn = mod_norm if has_mod else mod_gate
        args.append(jnp.concatenate([mg, mn], axis=1))
        specs.append(pl.BlockSpec((None, 12, d), lambda i: (grp(i + toff), 0, 0)))
    args.append(norm_w.reshape(1, d))
    specs.append(pl.BlockSpec((1, d), lambda i: (0, 0)))
    out_shape, out_specs = [], []
    if write_x:
        out_shape.append(jax.ShapeDtypeStruct((n_rows, d), F32))
        out_specs.append(tile_out)
    out_shape.append(jax.ShapeDtypeStruct((n_rows, d), h_dtype))
    out_specs.append(tile_out)
    kern = functools.partial(
        _resid_norm_kernel, has_delta=has_delta, has_mod=has_mod, gate_row=gate_row,
        mod_rows=None if mod_rows is None else (6 + mod_rows[0], 6 + mod_rows[1]), write_x=write_x,
        n_first_tiles=nft)
    outs = pl.pallas_call(
        kern, grid=(n_rows // tm,), in_specs=specs, out_specs=out_specs, out_shape=out_shape,
        compiler_params=_cparams(("parallel",), 32), name="resid_norm",
    )(*args)
    return outs if write_x else (None, outs[0])


def _rope_pair(x1, x2, cos, sin):
    return x1 * cos - x2 * sin, x1 * sin + x2 * cos


def _weight_rows_spec(layer, row_starts, tn, d):
    def index_map(j, i):
        start = sum(jnp.where(j == k, s, 0) for k, s in enumerate(row_starts))
        return layer, pl.multiple_of(start, 8), 0
    assert all(s % 8 == 0 for s in row_starts)
    return pl.BlockSpec((pl.Element(1), pl.Element(tn), pl.Element(d)), index_map)


def _proj_kernel(h_ref, w_ref, cos_ref, sin_ref, out_ref, wb_scr, *, tm, rope_tile, n_prompt, dec_seq):
    j = pl.program_id(0)
    i = pl.program_id(1)

    @pl.when(i == 0)
    def _():
        wb_scr[...] = w_ref[0].astype(BF16)

    row0 = pl.multiple_of(i * tm, tm)
    use_rope = (j == rope_tile) & (row0 >= n_prompt)
    half = HEAD_DIM // 2

    def heads(rope):
        h = h_ref[pl.ds(row0, tm), :]
        if rope:
            pos0 = pl.multiple_of((row0 - n_prompt) % dec_seq, tm)
            cos = cos_ref[pl.ds(pos0, tm), :]
            sin = sin_ref[pl.ds(pos0, tm), :]
        for hh in range(wb_scr.shape[0] // HEAD_DIM):
            c0 = hh * HEAD_DIM
            acc = _dot_nt(h, wb_scr[c0:c0 + HEAD_DIM, :])
            if rope:
                y1, y2 = _rope_pair(acc[:, :half], acc[:, half:], cos, sin)
                out_ref[:, c0:c0 + half] = y1.astype(out_ref.dtype)
                out_ref[:, c0 + half:c0 + HEAD_DIM] = y2.astype(out_ref.dtype)
            else:
                out_ref[:, c0:c0 + HEAD_DIM] = acc.astype(out_ref.dtype)

    pl.when(use_rope)(functools.partial(heads, True))
    pl.when(jnp.logical_not(use_rope))(functools.partial(heads, False))


def _proj_call(h, w_t, layer, row_starts, rope, rope_tile, *, n_prompt, dec_seq, tm=512, tn=1024):
    n, d = h.shape
    nj = len(row_starts)
    kern = functools.partial(_proj_kernel, tm=tm, rope_tile=rope_tile, n_prompt=n_prompt, dec_seq=dec_seq)
    return pl.pallas_call(
        kern, grid=(nj, n // tm),
        in_specs=[
            pl.BlockSpec((n, d), lambda j, i: (0, 0)),
            _weight_rows_spec(layer, row_starts, tn, d),
            pl.BlockSpec(rope[0].shape, lambda j, i: (0, 0)),
            pl.BlockSpec(rope[1].shape, lambda j, i: (0, 0)),
        ],
        out_specs=pl.BlockSpec((tm, tn), lambda j, i: (i, j)),
        out_shape=jax.ShapeDtypeStruct((n, nj * tn), BF16),
        scratch_shapes=[pltpu.VMEM((tn, d), BF16)],
        compiler_params=_cparams(("parallel", "arbitrary"), 56), name="in_proj",
    )(h, w_t, *rope)


def _proj_t_kernel(h_ref, w_ref, cos_ref, sin_ref, out_ref, wb_scr, *, tm, rope_tile, n_prompt, dec_seq):
    j = pl.program_id(0)
    i = pl.program_id(1)

    @pl.when(i == 0)
    def _():
        wb_scr[...] = w_ref[0].astype(BF16)

    row0 = pl.multiple_of(i * tm, tm)
    use_rope = (j == rope_tile) & (row0 >= n_prompt)
    half = HEAD_DIM // 2

    def heads(rope):
        h = h_ref[pl.ds(row0, tm), :]
        if rope:
            blk = ((row0 - n_prompt) % dec_seq) // tm
            cos = cos_ref[blk]
            sin = sin_ref[blk]
        for hh in range(wb_scr.shape[0] // HEAD_DIM):
            r0 = hh * HEAD_DIM
            acc = _dot_nt(wb_scr[r0:r0 + HEAD_DIM, :], h)
            if rope:
                y1, y2 = _rope_pair(acc[:half, :], acc[half:, :], cos, sin)
                y = jnp.concatenate([y1, y2], axis=0).astype(out_ref.dtype)
            else:
                y = acc.astype(out_ref.dtype)
            for s in range(tm // CHUNK):
                out_ref[s, r0:r0 + HEAD_DIM, :] = y[:, s * CHUNK:(s + 1) * CHUNK]

    pl.when(use_rope)(functools.partial(heads, True))
    pl.when(jnp.logical_not(use_rope))(functools.partial(heads, False))


def _proj_t_call(h, w_t, layer, row_starts, rope_t, rope_tile, *, n_prompt, dec_seq, tm=512, tn=1024):
    n, d = h.shape
    nj = len(row_starts)
    kern = functools.partial(_proj_t_kernel, tm=tm, rope_tile=rope_tile, n_prompt=n_prompt, dec_seq=dec_seq)
    return pl.pallas_call(
        kern, grid=(nj, n // tm),
        in_specs=[
            pl.BlockSpec((n, d), lambda j, i: (0, 0)),
            _weight_rows_spec(layer, row_starts, tn, d),
            pl.BlockSpec(rope_t[0].shape, lambda j, i: (0, 0, 0)),
            pl.BlockSpec(rope_t[1].shape, lambda j, i: (0, 0, 0)),
        ],
        out_specs=pl.BlockSpec((tm // CHUNK, tn, CHUNK), lambda j, i: (i, j, 0)),
        out_shape=jax.ShapeDtypeStruct((n // CHUNK, nj * tn, CHUNK), BF16),
        scratch_shapes=[pltpu.VMEM((tn, d), BF16)],
        compiler_params=_cparams(("parallel", "arbitrary"), 56), name="in_proj_t",
    )(h, w_t, *rope_t)


def _gate_kernel(h_ref, wg_ref, bias_ref, out_ref, *, tm, n_heads):
    g = _dot_nt(wg_ref[...].astype(BF16), h_ref[...]) + bias_ref[...]
    nd = 2 * n_heads
    ig = g[0:nd, :]
    lf = _log_sigmoid(g[nd:2 * nd, :])
    r = _iota((CHUNK, CHUNK), 0)
    c = _iota((CHUNK, CHUNK), 1)
    upper = jnp.where(r <= c, 1.0, 0.0).astype(BF16)
    lower = jnp.where(r >= c, 1.0, 0.0).astype(BF16)
    is_fwd = _iota((nd, CHUNK), 0) < n_heads
    for s in range(tm // CHUNK):
        sl = slice(s * CHUNK, (s + 1) * CHUNK)
        lf_c = lf[:, sl]
        b = jnp.where(is_fwd, _dot3(lf_c, upper), _dot3(lf_c, lower))
        ig_c = ig[:, sl]
        for hh in range(n_heads):
            out_ref[hh, s, 0:1, :] = ig_c[hh:hh + 1, :]
            out_ref[hh, s, 1:2, :] = ig_c[n_heads + hh:n_heads + hh + 1, :]
            out_ref[hh, s, 2:3, :] = b[hh:hh + 1, :]
            out_ref[hh, s, 3:4, :] = b[n_heads + hh:n_heads + hh + 1, :]


def _gate_call(h, wg_t, bias_col, n_heads, tm=512):
    n, d = h.shape
    ng = 4 * n_heads
    return pl.pallas_call(
        functools.partial(_gate_kernel, tm=tm, n_heads=n_heads),
        grid=(n // tm,),
        in_specs=[
            pl.BlockSpec((tm, d), lambda i: (i, 0)),
            pl.BlockSpec((ng, d), lambda i: (0, 0)),
            pl.BlockSpec((ng, 1), lambda i: (0, 0)),
        ],
        out_specs=pl.BlockSpec((n_heads, tm // CHUNK, 4, CHUNK), lambda i: (0, i, 0, 0)),
        out_shape=jax.ShapeDtypeStruct((n_heads, n // CHUNK, 4, CHUNK), F32),
        compiler_params=_cparams(("parallel",), 32), name="mlstm_gates",
    )(h, wg_t, bias_col)


def _scan_loop(nc, body, init):
    if nc <= 2:
        carry = init
        for ci in range(nc):
            carry = body(ci, carry)
        return carry
    return lax.fori_loop(0, nc, body, init)


MAX_GROUP = 16
STRIP = 32


def _seqs_per_step(n_seq, nc):
    ns = max(1, MAX_GROUP // nc)
    while n_seq % ns:
        ns -= 1
    return ns


def _group_loop(n_groups, body):
    if n_groups == 1:
        body(0, 0)
    else:
        lax.fori_loop(0, n_groups, body, 0)


def _chunk_off(c):
    return c * CHUNK if isinstance(c, int) else pl.multiple_of(c * CHUNK, CHUNK)


def _mlstm_kernel(*refs, nc, ns, has_init, first_layer, write_state):
    it = iter(refs)
    qt_ref, vt_ref, ot_ref, k_ref, gp_ref, nw_ref = (next(it) for _ in range(6))
    if has_init:
        c0_ref, n0_ref, m0_ref = next(it), next(it), next(it)
    if write_state and not first_layer:
        next(it), next(it), next(it)
    hm_ref = next(it)
    if write_state:
        cout_ref, nout_ref, mout_ref = next(it), next(it), next(it)
    ct_scr, n_scr, part_scr, inc_scr, ninc_scr, rows_scr, cst_scr, nst_scr, coef_scr = (next(it) for _ in range(9))

    L = CHUNK
    grp = min(MAX_GROUP, ns * nc)
    row = _iota((L, L), 0)
    col = _iota((L, L), 1)
    k_scale = HEAD_DIM ** -0.5
    b_idx = pl.program_id(0)
    h_idx = pl.program_id(1)
    n_heads = pl.num_programs(1)

    def gates(d, c):
        g = gp_ref[c]
        ig = g[d:d + 1, :]
        brow = g[2 + d:3 + d, :]
        blast = brow[:, L - 1:L] if d == 0 else brow[:, 0:1]
        return ig, brow, blast

    def local_group(g, carry):
        cs = [g * grp + u for u in range(grp)]
        pairs = [(u, d) for u in range(grp) for d in range(2)]
        qt = [qt_ref[c] for c in cs]
        vt = [vt_ref[c] for c in cs]
        k = [k_ref[pl.ds(_chunk_off(c), L), :] for c in cs]
        qk = [_dot(k[u], qt[u]) for u in range(grp)]
        vf = [v.astype(F32) for v in vt]
        gts = {(u, d): gates(d, cs[u]) for u, d in pairs}
        for u, d in pairs:
            c = cs[u]
            ig, brow, blast = gts[u, d]
            wj = blast - brow + ig
            mloc2 = jnp.max(wj, axis=1, keepdims=True)
            e = jnp.exp(wj - mloc2) * k_scale
            inc_scr[d, c] = _dot((vf[u] * e).astype(BF16), k[u])
            ninc_scr[d, c] = _dot3(jnp.broadcast_to(e, (8, L)), k[u])
            rows_scr[d, c, 2:3, :] = jnp.broadcast_to(mloc2, (1, L))
            rows_scr[d, c, 4:5, :] = jnp.broadcast_to(blast, (1, L))
        for u, d in pairs:
            c = cs[u]
            ig, brow, blast = gts[u, d]
            key_term = jnp.broadcast_to(ig - brow, (L, L)).T
            causal = (row <= col) if d == 0 else (row >= col)
            dm = jnp.where(causal, key_term + brow, -jnp.inf)
            mloc = jnp.max(dm, axis=0, keepdims=True)
            s = qk[u] * (jnp.exp(dm - mloc) * k_scale)
            part_scr[d, c] = _dot(vt[u], s.astype(BF16))
            rows_scr[d, c, 0:1, :] = jnp.sum(s, axis=0, keepdims=True)
            rows_scr[d, c, 1:2, :] = mloc
        return carry

    _group_loop(ns * nc // grp, local_group)

    def m_step(d, c, m):
        blast = rows_scr[d, c, 4:5, :]
        mloc2 = rows_scr[d, c, 2:3, :]
        rows_scr[d, c, 3:4, :] = m
        m_new = jnp.maximum(blast + m, mloc2)
        decay = jnp.exp(blast + m - m_new)
        w_inc = jnp.exp(mloc2 - m_new)
        coef_scr[d, c, 0:1, :] = jnp.concatenate([decay] * (HEAD_DIM // L), axis=1)
        coef_scr[d, c, 1:2, :] = jnp.concatenate([w_inc] * (HEAD_DIM // L), axis=1)
        return m_new

    for sq in range(ns):
        c_lo = sq * nc
        m_init = []
        for d in range(2):
            if has_init:
                ct_scr[d] = c0_ref[d].T
                n_scr[d] = jnp.broadcast_to(n0_ref[d], n_scr.shape[1:])
                m_init.append(jnp.full((1, L), m0_ref[(b_idx * 2 + d) * n_heads + h_idx], F32))
            else:
                ct_scr[d] = jnp.zeros(ct_scr.shape[1:], F32)
                n_scr[d] = jnp.zeros(n_scr.shape[1:], F32)
                m_init.append(jnp.zeros((1, L), F32))

        def m_body(ci, ms, c_lo=c_lo):
            return m_step(0, c_lo + ci, ms[0]), m_step(1, c_lo + nc - 1 - ci, ms[1])

        m_fin = _scan_loop(nc, m_body, tuple(m_init))

        def body(ci, carry, c_lo=c_lo):
            for d, c in ((0, c_lo + ci), (1, c_lo + nc - 1 - ci)):
                decay = coef_scr[d, c, 0:1, :]
                w_inc = coef_scr[d, c, 1:2, :]
                n8 = n_scr[d]
                nst_scr[d, c] = n8
                for r0 in range(0, HEAD_DIM, STRIP):
                    ct = ct_scr[d, r0:r0 + STRIP, :]
                    cst_scr[d, c, r0:r0 + STRIP, :] = ct.astype(BF16)
                    ct_scr[d, r0:r0 + STRIP, :] = decay * ct + w_inc * inc_scr[d, c, r0:r0 + STRIP, :]
                n_scr[d] = decay * n8 + w_inc * ninc_scr[d, c]
            return carry

        _scan_loop(nc, body, 0)

        if write_state:
            lsel = (lambda d, sq=sq: (sq, 0, d)) if first_layer else (lambda d, sq=sq: (sq, d))
            for d in range(2):
                cout_ref[lsel(d)] = ct_scr[d].T
                nout_ref[lsel(d)] = n_scr[d][0:1, :]
                mout_ref[lsel(d)] = m_fin[d]
            if first_layer:
                for l in range(1, cout_ref.shape[1]):
                    cout_ref[sq, l] = jnp.zeros(cout_ref.shape[2:], F32)
                    nout_ref[sq, l] = jnp.zeros(nout_ref.shape[2:], F32)
                    mout_ref[sq, l] = jnp.zeros(mout_ref.shape[2:], F32)

    def finish_group(g, carry):
        cs = [g * grp + u for u in range(grp)]
        qt = [qt_ref[c] for c in cs]
        cq = [[_dot(cst_scr[d, c], qt[u]) for d in range(2)] for u, c in enumerate(cs)]
        nq = [[_dot(nst_scr[d, c].astype(BF16), qt[u])[0:1, :] for d in range(2)] for u, c in enumerate(cs)]
        for u, c in enumerate(cs):
            coef = []
            for d in range(2):
                _, brow, _ = gates(d, c)
                den_loc = rows_scr[d, c, 0:1, :]
                mloc = rows_scr[d, c, 1:2, :]
                inter = brow + rows_scr[d, c, 3:4, :]
                m_row = jnp.maximum(mloc, inter)
                w_loc = jnp.exp(mloc - m_row)
                w_inter = jnp.exp(inter - m_row)
                den = w_loc * den_loc + w_inter * nq[u][d]
                r_den = 1.0 / jnp.maximum(jnp.abs(den), jnp.exp(-m_row))
                coef.append((w_loc * r_den, w_inter * r_den))
            ssq = jnp.zeros((1, L), F32)
            for r0 in range(0, HEAD_DIM, STRIP):
                h_s = None
                for d in range(2):
                    h_d = coef[d][0] * part_scr[d, c, r0:r0 + STRIP, :] + coef[d][1] * cq[u][d][r0:r0 + STRIP, :]
                    h_s = h_d if h_s is None else h_s + h_d
                part_scr[0, c, r0:r0 + STRIP, :] = h_s
                ssq = ssq + jnp.sum(h_s * h_s, axis=0, keepdims=True)
            r_norm = lax.rsqrt(ssq * (1.0 / HEAD_DIM) + EPS)
            for r0 in range(0, HEAD_DIM, L):
                y = part_scr[0, c, r0:r0 + L, :] * r_norm * nw_ref[r0:r0 + L, :]
                o = jax.nn.sigmoid(ot_ref[c, r0:r0 + L, :].astype(F32)) * y
                hm_ref[pl.ds(_chunk_off(c), L), r0:r0 + L] = o.T.astype(hm_ref.dtype)
        return carry

    _group_loop(ns * nc // grp, finish_group)


def _state_specs(shapes, layer, first_layer, ns):
    specs = []
    for shp in shapes:
        tail = shp[4:]
        zeros = (0,) * len(tail)
        if first_layer:
            specs.append(pl.BlockSpec((ns, shp[1], 2, None) + tail, lambda b, h, z=zeros: (b, 0, 0, h) + z))
        else:
            specs.append(pl.BlockSpec((ns, None, 2, None) + tail, lambda b, h, z=zeros: (b, layer, 0, h) + z))
    return specs


def _mlstm_call(proj_t, proj_n, gp, nw_b, init, state_bufs, *, n_seq, seq_len, tok_off, n_heads, layer, depth):
    dk = HEAD_DIM
    nc = seq_len // CHUNK
    has_init = init is not None
    ns = 1 if has_init else _seqs_per_step(n_seq, nc)
    nct = ns * nc
    boff = tok_off // (ns * seq_len)
    write_state = not has_init
    first_layer = layer == 0
    args = [proj_t, proj_t, proj_t, proj_n, gp, nw_b]
    specs = [
        pl.BlockSpec((nct, dk, CHUNK), lambda b, h: (b + boff, h, 0)),
        pl.BlockSpec((nct, dk, CHUNK), lambda b, h: (b + boff, n_heads + h, 0)),
        pl.BlockSpec((nct, dk, CHUNK), lambda b, h: (b + boff, 2 * n_heads + h, 0)),
        pl.BlockSpec((ns * seq_len, dk), lambda b, h: (b + boff, h)),
        pl.BlockSpec((None, nct, 4, CHUNK), lambda b, h: (h, b + boff, 0, 0)),
        pl.BlockSpec((dk, LANES), lambda b, h: (h, 0)),
    ]
    aliases = {}
    if has_init:
        c0, n0, m0 = init
        args += [c0, n0[:, layer][:, :, :, None, :], m0[:, layer].reshape(-1)]
        specs += [
            pl.BlockSpec((None, None, 2, None, dk, dk), lambda b, h: (b, layer, 0, h, 0, 0)),
            pl.BlockSpec((None, 2, None, 1, dk), lambda b, h: (b, 0, h, 0, 0)),
            pl.BlockSpec(memory_space=pltpu.SMEM),
        ]
    out_shape = [jax.ShapeDtypeStruct((n_seq * seq_len, n_heads * dk), BF16)]
    out_specs = [pl.BlockSpec((ns * seq_len, dk), lambda b, h: (b, h))]
    if write_state:
        shapes = [(n_seq, depth, 2, n_heads, dk, dk), (n_seq, depth, 2, n_heads, 1, dk),
                  (n_seq, depth, 2, n_heads, 1, LANES)]
        if not first_layer:
            for k_, buf in enumerate(state_bufs):
                aliases[len(args)] = 1 + k_
                args.append(buf)
                specs.append(pl.BlockSpec(memory_space=pl.ANY))
        out_shape += [jax.ShapeDtypeStruct(s, F32) for s in shapes]
        out_specs += _state_specs(shapes, layer, first_layer, ns)
    kern = functools.partial(_mlstm_kernel, nc=nc, ns=ns, has_init=has_init, first_layer=first_layer,
                             write_state=write_state)
    return pl.pallas_call(
        kern, grid=(n_seq // ns, n_heads), in_specs=specs, out_specs=out_specs, out_shape=out_shape,
        input_output_aliases=aliases,
        scratch_shapes=[pltpu.VMEM((2, dk, dk), F32), pltpu.VMEM((2, 8, dk), F32),
                        pltpu.VMEM((2, nct, dk, CHUNK), F32), pltpu.VMEM((2, nct, dk, dk), F32),
                        pltpu.VMEM((2, nct, 8, dk), F32), pltpu.VMEM((2, nct, 8, CHUNK), F32),
                        pltpu.VMEM((2, nct, dk, dk), BF16), pltpu.VMEM((2, nct, 8, dk), F32),
                        pltpu.VMEM((2, nct, 8, dk), F32)],
        compiler_params=_cparams(("parallel", "parallel"), 48), name="mlstm_mixer",
    )(*args)


def _ret_kernel(*refs, nc, ns, has_init, first_layer, write_state):
    it = iter(refs)
    dl_ref, qt_ref, vt_ref, gt_ref, k_ref, nw_ref = (next(it) for _ in range(6))
    if has_init:
        s0_ref = next(it)
    if write_state and not first_layer:
        next(it)
    hr_ref = next(it)
    if write_state:
        sout_ref = next(it)
    st_scr, part_scr, inc_scr, sst_scr = next(it), next(it), next(it), next(it)

    L = CHUNK
    grp = min(MAX_GROUP, ns * nc)
    rowi = _iota((L, L), 0)
    coli = _iota((L, L), 1)
    k_scale = HEAD_DIM ** -0.5
    h_idx = pl.program_id(1)
    n_heads = pl.num_programs(1)

    consts = []
    for d in range(2):
        lg = _log_sigmoid(jnp.full((1, 1), dl_ref[d * n_heads + h_idx], F32))
        rel = (coli - rowi if d == 0 else rowi - coli).astype(F32)
        dmat_t = jnp.where(rel >= 0, jnp.exp(jnp.maximum(rel, 0.0) * lg), 0.0) * k_scale
        pos = _iota((1, L), 1).astype(F32)
        if d == 1:
            pos = (L - 1.0) - pos
        q_decay = jnp.exp((pos + 1.0) * lg)
        k_decay = jnp.exp((L - 1.0 - pos) * lg) * k_scale
        chunk_decay = jnp.exp(float(L) * lg)
        consts.append((dmat_t, q_decay, k_decay, chunk_decay))

    def local_group(g, carry):
        cs = [g * grp + u for u in range(grp)]
        qt = [qt_ref[c] for c in cs]
        vt = [vt_ref[c] for c in cs]
        k = [k_ref[pl.ds(_chunk_off(c), L), :] for c in cs]
        qk = [_dot(k[u], qt[u]) for u in range(grp)]
        for u, c in enumerate(cs):
            vf = vt[u].astype(F32)
            for d in range(2):
                inc_scr[d, c] = _dot((vf * consts[d][2]).astype(BF16), k[u])
        for u, c in enumerate(cs):
            for d in range(2):
                part_scr[d, c] = _dot(vt[u], (qk[u] * consts[d][0]).astype(BF16))
        return carry

    _group_loop(ns * nc // grp, local_group)

    for sq in range(ns):
        c_lo = sq * nc
        for d in range(2):
            if has_init:
                st_scr[d] = s0_ref[d].T
            else:
                st_scr[d] = jnp.zeros(st_scr.shape[1:], F32)

        def body(ci, carry, c_lo=c_lo):
            for d, c in ((0, c_lo + ci), (1, c_lo + nc - 1 - ci)):
                for r0 in range(0, HEAD_DIM, STRIP):
                    st = st_scr[d, r0:r0 + STRIP, :]
                    sst_scr[d, c, r0:r0 + STRIP, :] = st.astype(BF16)
                    st_scr[d, r0:r0 + STRIP, :] = consts[d][3] * st + inc_scr[d, c, r0:r0 + STRIP, :]
            return carry

        _scan_loop(nc, body, 0)

        if write_state:
            for d in range(2):
                if first_layer:
                    sout_ref[sq, 0, d] = st_scr[d].T
                else:
                    sout_ref[sq, d] = st_scr[d].T
            if first_layer:
                for l in range(1, sout_ref.shape[1]):
                    sout_ref[sq, l] = jnp.zeros(sout_ref.shape[2:], F32)

    def finish_group(g, carry):
        cs = [g * grp + u for u in range(grp)]
        qt = [qt_ref[c] for c in cs]
        sq = [[_dot(sst_scr[d, c], qt[u]) for d in range(2)] for u, c in enumerate(cs)]
        for u, c in enumerate(cs):
            ssq = jnp.zeros((1, L), F32)
            for r0 in range(0, HEAD_DIM, STRIP):
                sl = slice(r0, r0 + STRIP)
                o_s = ((part_scr[0, c, sl, :] + part_scr[1, c, sl, :])
                       + (consts[0][1] * sq[u][0][sl, :] + consts[1][1] * sq[u][1][sl, :]))
                part_scr[0, c, sl, :] = o_s
                ssq = ssq + jnp.sum(o_s * o_s, axis=0, keepdims=True)
            r_norm = lax.rsqrt(ssq * (1.0 / HEAD_DIM) + EPS)
            for r0 in range(0, HEAD_DIM, L):
                y = part_scr[0, c, r0:r0 + L, :] * r_norm * nw_ref[r0:r0 + L, :]
                rg = gt_ref[c, r0:r0 + L, :].astype(F32)
                hr_ref[pl.ds(_chunk_off(c), L), r0:r0 + L] = (rg * jax.nn.sigmoid(rg) * y).T.astype(hr_ref.dtype)
        return carry

    _group_loop(ns * nc // grp, finish_group)


def _ret_call(proj_t, proj_n, decay_logit, nw_b, init, state_buf, *, n_seq, seq_len, tok_off, n_heads, layer,
              depth):
    dk = HEAD_DIM
    nc = seq_len // CHUNK
    has_init = init is not None
    ns = 1 if has_init else _seqs_per_step(n_seq, nc)
    nct = ns * nc
    boff = tok_off // (ns * seq_len)
    write_state = not has_init
    first_layer = layer == 0
    args = [decay_logit.reshape(-1), proj_t, proj_t, proj_t, proj_n, nw_b]
    specs = [
        pl.BlockSpec(memory_space=pltpu.SMEM),
        pl.BlockSpec((nct, dk, CHUNK), lambda b, h: (b + boff, 3 * n_heads + h, 0)),
        pl.BlockSpec((nct, dk, CHUNK), lambda b, h: (b + boff, 4 * n_heads + h, 0)),
        pl.BlockSpec((nct, dk, CHUNK), lambda b, h: (b + boff, 5 * n_heads + h, 0)),
        pl.BlockSpec((ns * seq_len, dk), lambda b, h: (b + boff, n_heads + h)),
        pl.BlockSpec((dk, LANES), lambda b, h: (h, 0)),
    ]
    aliases = {}
    if has_init:
        args.append(init)
        specs.append(pl.BlockSpec((None, None, 2, None, dk, dk), lambda b, h: (b, layer, 0, h, 0, 0)))
    out_shape = [jax.ShapeDtypeStruct((n_seq * seq_len, n_heads * dk), BF16)]
    out_specs = [pl.BlockSpec((ns * seq_len, dk), lambda b, h: (b, h))]
    if write_state:
        shapes = [(n_seq, depth, 2, n_heads, dk, dk)]
        if not first_layer:
            aliases[len(args)] = 1
            args.append(state_buf)
            specs.append(pl.BlockSpec(memory_space=pl.ANY))
        out_shape += [jax.ShapeDtypeStruct(s, F32) for s in shapes]
        out_specs += _state_specs(shapes, layer, first_layer, ns)
    kern = functools.partial(_ret_kernel, nc=nc, ns=ns, has_init=has_init, first_layer=first_layer,
                             write_state=write_state)
    return pl.pallas_call(
        kern, grid=(n_seq // ns, n_heads), in_specs=specs, out_specs=out_specs, out_shape=out_shape,
        input_output_aliases=aliases,
        scratch_shapes=[pltpu.VMEM((2, dk, dk), F32), pltpu.VMEM((2, nct, dk, CHUNK), F32),
                        pltpu.VMEM((2, nct, dk, dk), F32), pltpu.VMEM((2, nct, dk, dk), BF16)],
        compiler_params=_cparams(("parallel", "parallel"), 48), name="retention_mixer",
    )(*args)


def _merge_kernel(hmp_ref, hms_ref, hrp_ref, hrs_ref, ga_ref, gb_ref, x_ref, mod_ref, nw_ref, wa_ref, wb_ref,
                  wo_ref, wr_ref, x1_ref, h2_ref, aff_ref, wab_scr, wbb_scr, wob_scr, *, n_prompt_tiles):
    i = pl.program_id(0)

    @pl.when(i == 0)
    def _():
        wab_scr[...] = wa_ref[...].astype(BF16)
        wbb_scr[...] = wb_ref[...].astype(BF16)
        wob_scr[...] = wo_ref[...].astype(BF16)

    is_prompt = i < n_prompt_tiles
    hm = jnp.where(is_prompt, hmp_ref[...], hms_ref[...])
    hr = jnp.where(is_prompt, hrp_ref[...], hrs_ref[...])
    ya = _dot(hm, wab_scr[...])
    yb = _dot(hr, wbb_scr[...])
    merged = (jax.nn.sigmoid(ga_ref[...].astype(F32)) * ya + jax.nn.sigmoid(gb_ref[...].astype(F32)) * yb)
    y = _dot(merged.astype(BF16), wob_scr[...])
    x1 = x_ref[...] + mod_ref[2:3, :] * y
    x1_ref[...] = x1
    h2 = x1 * lax.rsqrt(jnp.mean(x1 * x1, axis=-1, keepdims=True) + EPS) * nw_ref[...]
    h2 = h2 * (1.0 + mod_ref[4:5, :]) + mod_ref[3:4, :]
    h2_ref[...] = h2
    logits = _dot_nt(wr_ref[...].astype(BF16), h2.astype(BF16))
    p = jnp.exp(logits - jnp.max(logits, axis=0, keepdims=True))
    aff_ref[...] = p / jnp.sum(p, axis=0, keepdims=True)


def _merge_call(hm_p, hm_s, hr_p, hr_s, proj_n, x, mod, norm_w, w_a, w_b, w_out, router_wt, layer, *,
                n_prompt, dec_seq, tm=512):
    n, d = x.shape
    ne = router_wt.shape[1]
    npt = n_prompt // tm
    grp = functools.partial(_group_of_tile, tm=tm, n_prompt=n_prompt, dec_seq=dec_seq)
    tile = pl.BlockSpec((tm, d), lambda i: (i, 0))
    tile_p = pl.BlockSpec((tm, d), lambda i: (jnp.minimum(i, npt - 1), 0))
    tile_s = pl.BlockSpec((tm, d), lambda i: (jnp.maximum(i - npt, 0), 0))
    full = pl.BlockSpec((None, d, d), lambda i: (layer, 0, 0))
    return pl.pallas_call(
        functools.partial(_merge_kernel, n_prompt_tiles=npt), grid=(n // tm,),
        in_specs=[
            tile_p, tile_s, tile_p, tile_s,
            pl.BlockSpec((tm, d), lambda i: (i, 2)),
            pl.BlockSpec((tm, d), lambda i: (i, 3)),
            tile,
            pl.BlockSpec((None, 6, d), lambda i: (grp(i), 0, 0)),
            pl.BlockSpec((1, d), lambda i: (0, 0)),
            full, full, full,
            pl.BlockSpec((None, ne, d), lambda i: (layer, 0, 0)),
        ],
        out_specs=[tile, tile, pl.BlockSpec((ne, tm), lambda i: (0, i))],
        out_shape=[jax.ShapeDtypeStruct((n, d), F32), jax.ShapeDtypeStruct((n, d), F32),
                   jax.ShapeDtypeStruct((ne, n), F32)],
        scratch_shapes=[pltpu.VMEM((d, d), BF16)] * 3,
        compiler_params=_cparams(("arbitrary",), 56), name="merge_out_router",
    )(hm_p, hm_s, hr_p, hr_s, proj_n, proj_n, x, mod, norm_w.reshape(1, d), w_a, w_b, w_out, router_wt)


ROUTE_GROUP = 4


def _route_kernel(a_ref, at_ref, idx_ref, gate_ref, thr_scr, *, cap, n_tok):
    n_sets = a_ref.shape[0]
    nb = n_tok // LANES
    a_all = a_ref[...]

    def as_f32(bits):
        return lax.bitcast_convert_type(bits, F32)

    def count_ge(cand):
        m = jnp.where(a_all >= cand, 1.0, 0.0)
        return jnp.sum(jnp.sum(m, axis=1, keepdims=True), axis=2, keepdims=True)

    def bit_step(k, thr):
        cand = thr | lax.shift_left(jnp.int32(1), 30 - k)
        return jnp.where(count_ge(as_f32(cand)) >= cap, cand, thr)

    thr_scr[...] = lax.fori_loop(0, 31, bit_step, jnp.zeros((n_sets, 1, 1), jnp.int32))

    r128 = _iota((LANES, LANES), 0)
    c128 = _iota((LANES, LANES), 1)
    upper = jnp.where(r128 <= c128, 1.0, 0.0).astype(BF16)
    lower_t = jnp.where(r128 >= c128, 1.0, 0.0).astype(BF16)
    rb = _iota((nb, nb), 0)
    cb = _iota((nb, nb), 1)
    blk_before_rows = jnp.where(cb < rb, 1.0, 0.0).astype(BF16)
    blk_before_cols = jnp.where(rb < cb, 1.0, 0.0).astype(BF16)

    def incl_counts(mask):
        within = _dot(mask.astype(BF16), upper)
        before = _dot(blk_before_rows, within.astype(BF16))[:, LANES - 1:LANES]
        return within + before

    def incl_counts_t(mask_t):
        within = _dot(lower_t, mask_t.astype(BF16))
        before = _dot(within.astype(BF16), blk_before_cols)[LANES - 1:LANES, :]
        return within + before

    slot = _iota((1, cap), 1).astype(F32)
    blk_col = _iota((nb, 1), 0).astype(F32)
    sub_col = _iota((LANES, 1), 0).astype(F32)

    def per_group(g, carry):
        sets = [g + u * (n_sets // ROUTE_GROUP) for u in range(ROUTE_GROUP)]
        stage = []
        for s in sets:
            thr_bits = thr_scr[s]
            thr = as_f32(thr_bits)
            nxt = as_f32(thr_bits + 1)
            a = a_ref[s]
            a_t = at_ref[s]
            gt = jnp.where(a >= nxt, 1.0, 0.0)
            eq = jnp.where((a >= thr) & (a < nxt), 1.0, 0.0)
            gt_t = jnp.where(a_t >= nxt, 1.0, 0.0)
            eq_t = jnp.where((a_t >= thr) & (a_t < nxt), 1.0, 0.0)
            n_gt = jnp.sum(jnp.sum(gt, axis=1, keepdims=True), axis=0, keepdims=True)
            stage.append((a_t, gt, eq, gt_t, eq_t, cap - n_gt))
        ties = [(incl_counts(eq), incl_counts_t(eq_t)) for _, _, eq, _, eq_t, _ in stage]
        sels = []
        for (a_t, gt, eq, gt_t, eq_t, need), (c_eq, c_eq_t) in zip(stage, ties):
            sels.append((gt + eq * jnp.where(c_eq - eq < need, 1.0, 0.0),
                         gt_t + eq_t * jnp.where(c_eq_t - eq_t < need, 1.0, 0.0)))
        cnts = [(incl_counts(sel), incl_counts_t(sel_t)) for sel, sel_t in sels]
        picks = []
        for (a_t, *_), (cnt, cnt_t) in zip(stage, cnts):
            blk_end = cnt[:, LANES - 1:LANES]
            blk_of_slot = jnp.sum(jnp.where(blk_end <= slot, 1.0, 0.0), axis=0, keepdims=True)
            onehot_blk = jnp.where(blk_col == blk_of_slot, 1.0, 0.0).astype(BF16)
            cnt_rows = _dot3(cnt_t, onehot_blk)
            a_rows = _dot3(a_t, onehot_blk)
            picks.append((blk_of_slot, cnt_rows, a_rows))
        for s, (blk_of_slot, cnt_rows, a_rows) in zip(sets, picks):
            sub_of_slot = jnp.sum(jnp.where(cnt_rows <= slot, 1.0, 0.0), axis=0, keepdims=True)
            gate = jnp.sum(jnp.where(sub_col == sub_of_slot, a_rows, 0.0), axis=0, keepdims=True)
            idx_ref[s] = (blk_of_slot * LANES + sub_of_slot).astype(jnp.int32)
            gate_ref[s] = gate
        return carry

    lax.fori_loop(0, n_sets // ROUTE_GROUP, per_group, 0)


def _route_call(aff_t, n_pass, cap):
    ne, n = aff_t.shape
    n_tok = n // n_pass
    nb = n_tok // LANES
    a4 = aff_t.reshape(ne, n_pass, nb, LANES).transpose(1, 0, 2, 3).reshape(n_pass * ne, nb, LANES)
    a4_t = a4.transpose(0, 2, 1)
    n_sets = n_pass * ne
    return pl.pallas_call(
        functools.partial(_route_kernel, cap=cap, n_tok=n_tok),
        out_shape=[jax.ShapeDtypeStruct((n_sets, 1, cap), jnp.int32),
                   jax.ShapeDtypeStruct((n_sets, 1, cap), F32)],
        scratch_shapes=[pltpu.VMEM((n_sets, 1, 1), jnp.int32)],
        compiler_params=pltpu.CompilerParams(vmem_limit_bytes=32 * MIB), name="expert_choice_route",
    )(a4, a4_t)


ROW_LOOP_UNROLL = 8


def _moe_kernel(idx_ref, gate_ref, h_hbm, w1_ref, w3_ref, w2_ref, out_hbm,
                xe_scr, xb_scr, ye_a, ye_b, acc_scr, gsem, osem, *, rows, rows_pad):
    e = pl.program_id(0)
    f = pl.program_id(1)
    n_e = pl.num_programs(0)
    n_f = pl.num_programs(1)
    chunk = rows_pad // n_f

    def gather_row(slot, s):
        tok = idx_ref[slot * rows_pad + s]
        pltpu.make_async_copy(h_hbm.at[pl.ds(tok, 1), :], xe_scr.at[pl.ds(s, 1), :], gsem).start()

    def scatter_row(slot, s, ye_ref):
        tok = idx_ref[slot * rows_pad + s]
        g = gate_ref[slot * rows_pad + s]
        acc_scr[pl.ds(tok, 1), :] += ye_ref[pl.ds(s, 1), :] * g

    def wait_gather():
        pltpu.make_async_copy(h_hbm.at[pl.ds(0, rows_pad), :], xe_scr, gsem).wait()

    @pl.when((e == 0) & (f == 0))
    def _():
        acc_scr[...] = jnp.zeros_like(acc_scr)
        ye_a[...] = jnp.zeros_like(ye_a)
        ye_b[...] = jnp.zeros_like(ye_b)

        def issue(s, carry):
            gather_row(1, s)
            return carry
        lax.fori_loop(0, rows_pad, issue, 0, unroll=ROW_LOOP_UNROLL)

    def step(ye_cur, ye_prev):
        @pl.when(f == 0)
        def _():
            wait_gather()
            xb_scr[...] = xe_scr[0:rows, :].astype(BF16)
            ye_cur[0:rows, :] = jnp.zeros((rows, ye_cur.shape[1]), F32)

        xb = xb_scr[...]
        h1 = _dot(xb, w1_ref[...].astype(BF16))
        h3 = _dot(xb, w3_ref[...].astype(BF16))
        he = (h1 * jax.nn.sigmoid(h1) * h3).astype(BF16)
        ye_cur[0:rows, :] += _dot(he, w2_ref[...].astype(BF16))
        base = f * chunk
        for r in range(chunk):
            gather_row(e + 2, base + r)
        for r in range(chunk):
            scatter_row(e, base + r, ye_prev)

        @pl.when((e == n_e - 1) & (f == n_f - 1))
        def _():
            def scatter(s, carry):
                scatter_row(n_e, s, ye_cur)
                return carry
            lax.fori_loop(0, rows, scatter, 0, unroll=ROW_LOOP_UNROLL)
            wait_gather()
            cp = pltpu.make_async_copy(acc_scr, out_hbm, osem)
            cp.start()
            cp.wait()

    @pl.when(e % 2 == 0)
    def _():
        step(ye_a, ye_b)

    @pl.when(e % 2 == 1)
    def _():
        step(ye_b, ye_a)


def _moe_call(idx, gate, h2, w1, w3, w2, layer, tf=256):
    n, d = h2.shape
    ne, rows = idx.shape
    dff = w1.shape[3]
    nf = dff // tf
    rows_pad = -(-rows // (8 * nf)) * 8 * nf
    pad = ((1, 1), (0, rows_pad - rows))
    idx_all = jnp.pad(idx, pad).reshape(-1)
    gate_all = jnp.pad(gate, pad).reshape(-1)
    grid_spec = pltpu.PrefetchScalarGridSpec(
        num_scalar_prefetch=1,
        grid=(ne, nf),
        in_specs=[
            pl.BlockSpec(memory_space=pltpu.SMEM),
            pl.BlockSpec(memory_space=pl.ANY),
            pl.BlockSpec((None, None, d, tf), lambda e, f, idx: (layer, e, 0, f)),
            pl.BlockSpec((None, None, d, tf), lambda e, f, idx: (layer, e, 0, f)),
            pl.BlockSpec((None, None, tf, d), lambda e, f, idx: (layer, e, f, 0)),
        ],
        out_specs=pl.BlockSpec(memory_space=pl.ANY),
        scratch_shapes=[
            pltpu.VMEM((rows_pad, d), F32), pltpu.VMEM((rows, d), BF16),
            pltpu.VMEM((rows_pad, d), F32), pltpu.VMEM((rows_pad, d), F32),
            pltpu.VMEM((n, d), F32), pltpu.SemaphoreType.DMA, pltpu.SemaphoreType.DMA,
        ],
    )
    return pl.pallas_call(
        functools.partial(_moe_kernel, rows=rows, rows_pad=rows_pad), grid_spec=grid_spec,
        out_shape=jax.ShapeDtypeStruct((n, d), F32),
        compiler_params=_cparams(("arbitrary", "arbitrary"), 62), name="expert_ffn",
    )(idx_all, gate_all, h2, w1, w3, w2)


def _rope_tables(t):
    rows = t // GRID_W
    row = jnp.repeat(jnp.arange(rows, dtype=F32), GRID_W)
    colp = jnp.tile(jnp.arange(GRID_W, dtype=F32), rows)
    n_freq = HEAD_DIM // 4
    inv = ROPE_BASE ** (-jnp.arange(n_freq, dtype=F32) / n_freq)
    ang = jnp.concatenate([row[:, None] * inv, colp[:, None] * inv], axis=-1)
    return jnp.cos(ang), jnp.sin(ang)


def kernel(x_prompt, x_sample, c, state_mlstm_C, state_mlstm_n, state_mlstm_m, state_ret_S, c_ctx, w_mod, b_mod,
           norm1_w, norm2_w, w_in, mlstm_if_b, mlstm_norm_w, ret_decay_logit, ret_norm_w, w_branch_a, w_branch_b,
           w_out, router_w, ffn_w1, ffn_w3, ffn_w2, final_norm_w):
    bp, seq, d = x_prompt.shape
    db, dec_seq, _ = x_sample.shape
    depth = w_mod.shape[0]
    hm = mlstm_if_b.shape[-1]
    hr = ret_decay_logit.shape[-1]
    ne = router_w.shape[-1]
    n_prompt = bp * seq
    n_sample = db * dec_seq
    assert n_prompt == n_sample and hm * HEAD_DIM == d and hr * HEAD_DIM == d and hm == hr
    cap = EC_FACTOR * n_prompt // ne
    tm = 512
    tp = 2048
    geo = dict(n_prompt=n_prompt, dec_seq=dec_seq)

    cond8 = jnp.concatenate([c_ctx[None, :], c, jnp.zeros((8 - 1 - db, d), F32)], axis=0)
    mod = _mod_call(cond8, w_mod, b_mod)[:, :1 + db].reshape(depth, 1 + db, 6, d)

    n_m = 4 * d
    g0 = n_m + 4 * hm

    w_t = jnp.swapaxes(w_in, 1, 2)
    mq, mk, mv, mo = (k * d for k in range(4))
    rq, rk, rv, rg, ga, gb = (g0 + k * d for k in range(6))
    wg_t = w_t[:, n_m:g0]
    gate_bias = jnp.transpose(mlstm_if_b, (0, 2, 1, 3)).reshape(depth, 4 * hm, 1)
    router_wt = jnp.swapaxes(router_w, 1, 2)
    nw_m = jnp.broadcast_to(mlstm_norm_w[:, :, None], (depth, d, LANES))
    nw_r = jnp.broadcast_to(ret_norm_w[:, :, None], (depth, d, LANES))

    cos, sin = _rope_tables(dec_seq)
    cos_t = cos.T.reshape(HEAD_DIM // 2, dec_seq // tp, tp).transpose(1, 0, 2)
    sin_t = sin.T.reshape(HEAD_DIM // 2, dec_seq // tp, tp).transpose(1, 0, 2)

    x, h = _resid_norm_call(x_prompt.reshape(n_prompt, d), None, None, mod[0], norm1_w[0], gate_row=0,
                            mod_rows=(0, 1), h_dtype=BF16, write_x=True, x_tail=x_sample.reshape(n_sample, d),
                            **geo)
    m_bufs, s_buf = None, None
    y_prompt = y_sample = None
    for l in range(depth):
        proj_t = _proj_t_call(h, w_t, l, (mq, mv, mo, rq, rv, rg), (cos_t, sin_t), 3, tm=tp, **geo)
        proj_n = _proj_call(h, w_t, l, (mk, rk, ga, gb), (cos, sin), 1, tm=tp, **geo)
        gp = _gate_call(h, wg_t[l], gate_bias[l], hm, tm=tm)

        mix = dict(n_heads=hm, layer=l, depth=depth)
        hm_p, *m_bufs = _mlstm_call(proj_t, proj_n, gp, nw_m[l], None, m_bufs, n_seq=bp, seq_len=seq, tok_off=0,
                                    **mix)
        (hm_s,) = _mlstm_call(proj_t, proj_n, gp, nw_m[l], (state_mlstm_C, state_mlstm_n, state_mlstm_m), None,
                              n_seq=db, seq_len=dec_seq, tok_off=n_prompt, **mix)
        hr_p, s_buf = _ret_call(proj_t, proj_n, ret_decay_logit[l], nw_r[l], None, s_buf, n_seq=bp, seq_len=seq,
                                tok_off=0, **mix)
        (hr_s,) = _ret_call(proj_t, proj_n, ret_decay_logit[l], nw_r[l], state_ret_S, None, n_seq=db,
                            seq_len=dec_seq, tok_off=n_prompt, **mix)

        x1, h2, aff_t = _merge_call(hm_p, hm_s, hr_p, hr_s, proj_n, x, mod[l], norm2_w[l], w_branch_a, w_branch_b,
                                    w_out, router_wt, l, tm=tm, **geo)
        idx, gate = _route_call(aff_t, 2, cap)
        idx = idx.reshape(2, ne, cap) + (jnp.arange(2, dtype=jnp.int32) * n_prompt)[:, None, None]
        idx = jnp.swapaxes(idx, 0, 1).reshape(ne, 2 * cap)
        gate = jnp.swapaxes(gate.reshape(2, ne, cap), 0, 1).reshape(ne, 2 * cap)
        moe = _moe_call(idx, gate, h2, ffn_w1, ffn_w3, ffn_w2, l)
        if l + 1 < depth:
            x, h = _resid_norm_call(x1, moe, mod[l], mod[l + 1], norm1_w[l + 1], gate_row=5, mod_rows=(0, 1),
                                    h_dtype=BF16, write_x=True, **geo)
        else:
            fin = dict(gate_row=5, mod_rows=None, h_dtype=F32, write_x=False, **geo)
            _, y_prompt = _resid_norm_call(x1, moe, mod[l], None, final_norm_w, row_off=0, n_rows=n_prompt, **fin)
            _, y_sample = _resid_norm_call(x1, moe, mod[l], None, final_norm_w, row_off=n_prompt, n_rows=n_sample,
                                           **fin)

    c_buf, n_buf, m_buf = m_bufs
    return (y_prompt.reshape(bp, seq, d), y_sample.reshape(db, dec_seq, d), c_buf, n_buf[:, :, :, :, 0, :],
            m_buf[:, :, :, :, 0, 0], s_buf)
```

```python
import functools

import numpy as np
import jax
import jax.numpy as jnp
from jax import lax
from jax.experimental import pallas as pl
from jax.experimental.pallas import tpu as pltpu

F32 = jnp.float32
BF16 = jnp.bfloat16

GRID_W = 64
CHUNK = 128
LANES = 128
HEAD_DIM = 256
N_EXPERTS = 16
EC_FACTOR = 2
ROPE_BASE = 10000.0
EPS = 1e-6
V7X_VMEM_BYTES = 64 * 1024 * 1024
MIB = 1024 * 1024


def _cparams(semantics, vmem_mib):
    assert vmem_mib * MIB < V7X_VMEM_BYTES
    return pltpu.CompilerParams(dimension_semantics=semantics, vmem_limit_bytes=vmem_mib * MIB)


def _dot(a, b):
    return jnp.dot(a, b, preferred_element_type=F32)


def _dot_nt(a, b):
    return lax.dot_general(a, b, (((1,), (1,)), ((), ())), preferred_element_type=F32)


def _log_sigmoid(x):
    return -(jnp.maximum(-x, 0.0) + jnp.log1p(jnp.exp(-jnp.abs(x))))


def _split3(x):
    hi = x.astype(BF16)
    r1 = x - hi.astype(F32)
    mid = r1.astype(BF16)
    lo = (r1 - mid.astype(F32)).astype(BF16)
    return hi, mid, lo


def _dot3(x, m):
    hi, mid, lo = _split3(x)
    return _dot(hi, m) + _dot(mid, m) + _dot(lo, m)


def _iota(shape, dim):
    return lax.broadcasted_iota(jnp.int32, shape, dim)


def _tile_rows(ref, fb, n_tok, n_blk):
    return ref.at[pl.ds(fb, n_tok, stride=n_blk), :]


def _mod_kernel(cond_ref, w_ref, b_ref, out_ref):
    c = cond_ref[...]
    s = (c * jax.nn.sigmoid(c)).astype(BF16)
    out_ref[...] = _dot(s, w_ref[...].astype(BF16)) + b_ref[...]


def _mod_call(cond8, w_mod, b_mod):
    depth, d, w6 = w_mod.shape
    tn = 1536
    return pl.pallas_call(
        _mod_kernel,
        grid=(depth, w6 // tn),
        in_specs=[
            pl.BlockSpec((8, d), lambda l, j: (0, 0)),
            pl.BlockSpec((None, d, tn), lambda l, j: (l, 0, j)),
            pl.BlockSpec((None, 1, tn), lambda l, j: (l, 0, j)),
        ],
        out_specs=pl.BlockSpec((None, 8, tn), lambda l, j: (l, 0, j)),
        out_shape=jax.ShapeDtypeStruct((depth, 8, w6), F32),
        compiler_params=_cparams(("parallel", "parallel"), 32),
        name="adaln_mod",
    )(cond8, w_mod, b_mod.reshape(depth, 1, w6))


def _group_of_tile(i, tm, n_prompt, dec_seq):
    return jnp.maximum(i * tm - n_prompt + dec_seq, 0) // dec_seq


def _resid_norm_kernel(*refs, has_delta, has_mod, gate_row, mod_rows, write_x, n_first_tiles):
    it = iter(refs)
    x_ref = next(it)
    x2_ref = next(it) if n_first_tiles else None
    delta_ref = next(it) if has_delta else None
    mod_ref = next(it) if (has_delta or has_mod) else None
    nw_ref = next(it)
    xo_ref = next(it) if write_x else None
    h_ref = next(it)
    x = x_ref[...]
    if n_first_tiles:
        x = jnp.where(pl.program_id(0) < n_first_tiles, x, x2_ref[...])
    if has_delta:
        x = x + mod_ref[gate_row:gate_row + 1, :] * delta_ref[...]
    if write_x:
        xo_ref[...] = x
    y = x * lax.rsqrt(jnp.mean(x * x, axis=-1, keepdims=True) + EPS) * nw_ref[...]
    if has_mod:
        sh_row, sc_row = mod_rows
        y = y * (1.0 + mod_ref[sc_row:sc_row + 1, :]) + mod_ref[sh_row:sh_row + 1, :]
    h_ref[...] = y.astype(h_ref.dtype)


def _resid_norm_call(x, delta, mod_gate, mod_norm, norm_w, *, gate_row, mod_rows, h_dtype, write_x,
                     n_prompt, dec_seq, tm=512, row_off=0, n_rows=None, x_tail=None):
    n, d = x.shape
    has_delta = delta is not None
    has_mod = mod_norm is not None
    grp = functools.partial(_group_of_tile, tm=tm, n_prompt=n_prompt, dec_seq=dec_seq)
    if x_tail is None:
        n_rows = n if n_rows is None else n_rows
        toff = row_off // tm
        nft = 0
        tile_in = pl.BlockSpec((tm, d), lambda i: (i + toff, 0))
        args, specs = [x], [tile_in]
    else:
        assert row_off == 0 and n_rows is None and not has_delta
        n_rows, toff, nft = n + x_tail.shape[0], 0, n // tm
        args = [x, x_tail]
        specs = [pl.BlockSpec((tm, d), lambda i: (jnp.minimum(i, nft - 1), 0)),
                 pl.BlockSpec((tm, d), lambda i: (jnp.maximum(i - nft, 0), 0))]
    tile_out = pl.BlockSpec((tm, d), lambda i: (i, 0))
    if has_delta:
        args.append(delta)
        specs.append(tile_in)
    if has_delta or has_mod:
        mg = mod_gate if has_delta else mod_norm
        mn = mod_norm if has_mod else mod_gate
        args.append(jnp.concatenate([mg, mn], axis=1))
        specs.append(pl.BlockSpec((None, 12, d), lambda i: (grp(i + toff), 0, 0)))
    args.append(norm_w.reshape(1, d))
    specs.append(pl.BlockSpec((1, d), lambda i: (0, 0)))
    out_shape, out_specs = [], []
    if write_x:
        out_shape.append(jax.ShapeDtypeStruct((n_rows, d), F32))
        out_specs.append(tile_out)
    out_shape.append(jax.ShapeDtypeStruct((n_rows, d), h_dtype))
    out_specs.append(tile_out)
    kern = functools.partial(
        _resid_norm_kernel, has_delta=has_delta, has_mod=has_mod, gate_row=gate_row,
        mod_rows=None if mod_rows is None else (6 + mod_rows[0], 6 + mod_rows[1]), write_x=write_x,
        n_first_tiles=nft)
    outs = pl.pallas_call(
        kern, grid=(n_rows // tm,), in_specs=specs, out_specs=out_specs, out_shape=out_shape,
        compiler_params=_cparams(("parallel",), 32), name="resid_norm",
    )(*args)
    return outs if write_x else (None, outs[0])


def _rope_pair(x1, x2, cos, sin):
    return x1 * cos - x2 * sin, x1 * sin + x2 * cos


def _weight_rows_spec(layer, row_starts, tn, d):
    def index_map(j, i):
        start = sum(jnp.where(j == k, s, 0) for k, s in enumerate(row_starts))
        return layer, pl.multiple_of(start, 8), 0
    assert all(s % 8 == 0 for s in row_starts)
    return pl.BlockSpec((pl.Element(1), pl.Element(tn), pl.Element(d)), index_map)


def _proj_kernel(h_ref, w_ref, cos_ref, sin_ref, out_ref, wb_scr, *, tm, rope_tile, n_prompt, dec_seq):
    j = pl.program_id(0)
    i = pl.program_id(1)

    @pl.when(i == 0)
    def _():
        wb_scr[...] = w_ref[0].astype(BF16)

    row0 = pl.multiple_of(i * tm, tm)
    use_rope = (j == rope_tile) & (row0 >= n_prompt)
    half = HEAD_DIM // 2

    def heads(rope):
        h = h_ref[pl.ds(row0, tm), :]
        if rope:
            pos0 = pl.multiple_of((row0 - n_prompt) % dec_seq, tm)
            cos = cos_ref[pl.ds(pos0, tm), :]
            sin = sin_ref[pl.ds(pos0, tm), :]
        for hh in range(wb_scr.shape[0] // HEAD_DIM):
            c0 = hh * HEAD_DIM
            acc = _dot_nt(h, wb_scr[c0:c0 + HEAD_DIM, :])
            if rope:
                y1, y2 = _rope_pair(acc[:, :half], acc[:, half:], cos, sin)
                out_ref[:, c0:c0 + half] = y1.astype(out_ref.dtype)
                out_ref[:, c0 + half:c0 + HEAD_DIM] = y2.astype(out_ref.dtype)
            else:
                out_ref[:, c0:c0 + HEAD_DIM] = acc.astype(out_ref.dtype)

    pl.when(use_rope)(functools.partial(heads, True))
    pl.when(jnp.logical_not(use_rope))(functools.partial(heads, False))


def _proj_call(h, w_t, layer, row_starts, rope, rope_tile, *, n_prompt, dec_seq, tm=512, tn=1024):
    n, d = h.shape
    nj = len(row_starts)
    kern = functools.partial(_proj_kernel, tm=tm, rope_tile=rope_tile, n_prompt=n_prompt, dec_seq=dec_seq)
    return pl.pallas_call(
        kern, grid=(nj, n // tm),
        in_specs=[
            pl.BlockSpec((n, d), lambda j, i: (0, 0)),
            _weight_rows_spec(layer, row_starts, tn, d),
            pl.BlockSpec(rope[0].shape, lambda j, i: (0, 0)),
            pl.BlockSpec(rope[1].shape, lambda j, i: (0, 0)),
        ],
        out_specs=pl.BlockSpec((tm, tn), lambda j, i: (i, j)),
        out_shape=jax.ShapeDtypeStruct((n, nj * tn), BF16),
        scratch_shapes=[pltpu.VMEM((tn, d), BF16)],
        compiler_params=_cparams(("parallel", "arbitrary"), 56), name="in_proj",
    )(h, w_t, *rope)


def _proj_t_kernel(h_ref, w_ref, cos_ref, sin_ref, out_ref, wb_scr, *, tm, rope_tile, n_prompt, dec_seq):
    j = pl.program_id(0)
    i = pl.program_id(1)

    @pl.when(i == 0)
    def _():
        wb_scr[...] = w_ref[0].astype(BF16)

    row0 = pl.multiple_of(i * tm, tm)
    use_rope = (j == rope_tile) & (row0 >= n_prompt)
    half = HEAD_DIM // 2

    def heads(rope):
        h = h_ref[pl.ds(row0, tm), :]
        if rope:
            blk = ((row0 - n_prompt) % dec_seq) // tm
            cos = cos_ref[blk]
            sin = sin_ref[blk]
        for hh in range(wb_scr.shape[0] // HEAD_DIM):
            r0 = hh * HEAD_DIM
            acc = _dot_nt(wb_scr[r0:r0 + HEAD_DIM, :], h)
            if rope:
                y1, y2 = _rope_pair(acc[:half, :], acc[half:, :], cos, sin)
                y = jnp.concatenate([y1, y2], axis=0).astype(out_ref.dtype)
            else:
                y = acc.astype(out_ref.dtype)
            for s in range(tm // CHUNK):
                out_ref[s, r0:r0 + HEAD_DIM, :] = y[:, s * CHUNK:(s + 1) * CHUNK]

    pl.when(use_rope)(functools.partial(heads, True))
    pl.when(jnp.logical_not(use_rope))(functools.partial(heads, False))


def _proj_t_call(h, w_t, layer, row_starts, rope_t, rope_tile, *, n_prompt, dec_seq, tm=512, tn=1024):
    n, d = h.shape
    nj = len(row_starts)
    kern = functools.partial(_proj_t_kernel, tm=tm, rope_tile=rope_tile, n_prompt=n_prompt, dec_seq=dec_seq)
    return pl.pallas_call(
        kern, grid=(nj, n // tm),
        in_specs=[
            pl.BlockSpec((n, d), lambda j, i: (0, 0)),
            _weight_rows_spec(layer, row_starts, tn, d),
            pl.BlockSpec(rope_t[0].shape, lambda j, i: (0, 0, 0)),
            pl.BlockSpec(rope_t[1].shape, lambda j, i: (0, 0, 0)),
        ],
        out_specs=pl.BlockSpec((tm // CHUNK, tn, CHUNK), lambda j, i: (i, j, 0)),
        out_shape=jax.ShapeDtypeStruct((n // CHUNK, nj * tn, CHUNK), BF16),
        scratch_shapes=[pltpu.VMEM((tn, d), BF16)],
        compiler_params=_cparams(("parallel", "arbitrary"), 56), name="in_proj_t",
    )(h, w_t, *rope_t)


def _gate_kernel(h_ref, wg_ref, bias_ref, out_ref, *, tm, n_heads):
    g = _dot_nt(wg_ref[...].astype(BF16), h_ref[...]) + bias_ref[...]
    nd = 2 * n_heads
    ig = g[0:nd, :]
    lf = _log_sigmoid(g[nd:2 * nd, :])
    r = _iota((CHUNK, CHUNK), 0)
    c = _iota((CHUNK, CHUNK), 1)
    upper = jnp.where(r <= c, 1.0, 0.0).astype(BF16)
    lower = jnp.where(r >= c, 1.0, 0.0).astype(BF16)
    is_fwd = _iota((nd, CHUNK), 0) < n_heads
    for s in range(tm // CHUNK):
        sl = slice(s * CHUNK, (s + 1) * CHUNK)
        lf_c = lf[:, sl]
        b = jnp.where(is_fwd, _dot3(lf_c, upper), _dot3(lf_c, lower))
        ig_c = ig[:, sl]
        for hh in range(n_heads):
            out_ref[hh, s, 0:1, :] = ig_c[hh:hh + 1, :]
            out_ref[hh, s, 1:2, :] = ig_c[n_heads + hh:n_heads + hh + 1, :]
            out_ref[hh, s, 2:3, :] = b[hh:hh + 1, :]
            out_ref[hh, s, 3:4, :] = b[n_heads + hh:n_heads + hh + 1, :]


def _gate_call(h, wg_t, bias_col, n_heads, tm=512):
    n, d = h.shape
    ng = 4 * n_heads
    return pl.pallas_call(
        functools.partial(_gate_kernel, tm=tm, n_heads=n_heads),
        grid=(n // tm,),
        in_specs=[
            pl.BlockSpec((tm, d), lambda i: (i, 0)),
            pl.BlockSpec((ng, d), lambda i: (0, 0)),
            pl.BlockSpec((ng, 1), lambda i: (0, 0)),
        ],
        out_specs=pl.BlockSpec((n_heads, tm // CHUNK, 4, CHUNK), lambda i: (0, i, 0, 0)),
        out_shape=jax.ShapeDtypeStruct((n_heads, n // CHUNK, 4, CHUNK), F32),
        compiler_params=_cparams(("parallel",), 32), name="mlstm_gates",
    )(h, wg_t, bias_col)


def _scan_loop(nc, body, init):
    if nc <= 2:
        carry = init
        for ci in range(nc):
            carry = body(ci, carry)
        return carry
    return lax.fori_loop(0, nc, body, init)


MAX_GROUP = 16
STRIP = 32


def _seqs_per_step(n_seq, nc):
    ns = max(1, MAX_GROUP // nc)
    while n_seq % ns:
        ns -= 1
    return ns


def _group_loop(n_groups, body):
    if n_groups == 1:
        body(0, 0)
    else:
        lax.fori_loop(0, n_groups, body, 0)


def _chunk_off(c):
    return c * CHUNK if isinstance(c, int) else pl.multiple_of(c * CHUNK, CHUNK)


def _mlstm_kernel(*refs, nc, ns, has_init, first_layer, write_state):
    it = iter(refs)
    qt_ref, vt_ref, ot_ref, k_ref, gp_ref, nw_ref = (next(it) for _ in range(6))
    if has_init:
        c0_ref, n0_ref, m0_ref = next(it), next(it), next(it)
    if write_state and not first_layer:
        next(it), next(it), next(it)
    hm_ref = next(it)
    if write_state:
        cout_ref, nout_ref, mout_ref = next(it), next(it), next(it)
    ct_scr, n_scr, part_scr, inc_scr, ninc_scr, rows_scr, cst_scr, nst_scr, coef_scr = (next(it) for _ in range(9))

    L = CHUNK
    grp = min(MAX_GROUP, ns * nc)
    row = _iota((L, L), 0)
    col = _iota((L, L), 1)
    k_scale = HEAD_DIM ** -0.5
    b_idx = pl.program_id(0)
    h_idx = pl.program_id(1)
    n_heads = pl.num_programs(1)

    def gates(d, c):
        g = gp_ref[c]
        ig = g[d:d + 1, :]
        brow = g[2 + d:3 + d, :]
        blast = brow[:, L - 1:L] if d == 0 else brow[:, 0:1]
        return ig, brow, blast

    def local_group(g, carry):
        cs = [g * grp + u for u in range(grp)]
        pairs = [(u, d) for u in range(grp) for d in range(2)]
        qt = [qt_ref[c] for c in cs]
        vt = [vt_ref[c] for c in cs]
        k = [k_ref[pl.ds(_chunk_off(c), L), :] for c in cs]
        qk = [_dot(k[u], qt[u]) for u in range(grp)]
        vf = [v.astype(F32) for v in vt]
        gts = {(u, d): gates(d, cs[u]) for u, d in pairs}
        for u, d in pairs:
            c = cs[u]
            ig, brow, blast = gts[u, d]
            wj = blast - brow + ig
            mloc2 = jnp.max(wj, axis=1, keepdims=True)
            e = jnp.exp(wj - mloc2) * k_scale
            inc_scr[d, c] = _dot((vf[u] * e).astype(BF16), k[u])
            ninc_scr[d, c] = _dot3(jnp.broadcast_to(e, (8, L)), k[u])
            rows_scr[d, c, 2:3, :] = jnp.broadcast_to(mloc2, (1, L))
            rows_scr[d, c, 4:5, :] = jnp.broadcast_to(blast, (1, L))
        for u, d in pairs:
            c = cs[u]
            ig, brow, blast = gts[u, d]
            key_term = jnp.broadcast_to(ig - brow, (L, L)).T
            causal = (row <= col) if d == 0 else (row >= col)
            dm = jnp.where(causal, key_term + brow, -jnp.inf)
            mloc = jnp.max(dm, axis=0, keepdims=True)
            s = qk[u] * (jnp.exp(dm - mloc) * k_scale)
            part_scr[d, c] = _dot(vt[u], s.astype(BF16))
            rows_scr[d, c, 0:1, :] = jnp.sum(s, axis=0, keepdims=True)
            rows_scr[d, c, 1:2, :] = mloc
        return carry

    _group_loop(ns * nc // grp, local_group)

    def m_step(d, c, m):
        blast = rows_scr[d, c, 4:5, :]
        mloc2 = rows_scr[d, c, 2:3, :]
        rows_scr[d, c, 3:4, :] = m
        m_new = jnp.maximum(blast + m, mloc2)
        decay = jnp.exp(blast + m - m_new)
        w_inc = jnp.exp(mloc2 - m_new)
        coef_scr[d, c, 0:1, :] = jnp.concatenate([decay] * (HEAD_DIM // L), axis=1)
        coef_scr[d, c, 1:2, :] = jnp.concatenate([w_inc] * (HEAD_DIM // L), axis=1)
        return m_new

    for sq in range(ns):
        c_lo = sq * nc
        m_init = []
        for d in range(2):
            if has_init:
                ct_scr[d] = c0_ref[d].T
                n_scr[d] = jnp.broadcast_to(n0_ref[d], n_scr.shape[1:])
                m_init.append(jnp.full((1, L), m0_ref[(b_idx * 2 + d) * n_heads + h_idx], F32))
            else:
                ct_scr[d] = jnp.zeros(ct_scr.shape[1:], F32)
                n_scr[d] = jnp.zeros(n_scr.shape[1:], F32)
                m_init.append(jnp.zeros((1, L), F32))

        def m_body(ci, ms, c_lo=c_lo):
            return m_step(0, c_lo + ci, ms[0]), m_step(1, c_lo + nc - 1 - ci, ms[1])

        m_fin = _scan_loop(nc, m_body, tuple(m_init))

        def body(ci, carry, c_lo=c_lo):
            for d, c in ((0, c_lo + ci), (1, c_lo + nc - 1 - ci)):
                decay = coef_scr[d, c, 0:1, :]
                w_inc = coef_scr[d, c, 1:2, :]
                n8 = n_scr[d]
                nst_scr[d, c] = n8
                for r0 in range(0, HEAD_DIM, STRIP):
                    ct = ct_scr[d, r0:r0 + STRIP, :]
                    cst_scr[d, c, r0:r0 + STRIP, :] = ct.astype(BF16)
                    ct_scr[d, r0:r0 + STRIP, :] = decay * ct + w_inc * inc_scr[d, c, r0:r0 + STRIP, :]
                n_scr[d] = decay * n8 + w_inc * ninc_scr[d, c]
            return carry

        _scan_loop(nc, body, 0)

        if write_state:
            lsel = (lambda d, sq=sq: (sq, 0, d)) if first_layer else (lambda d, sq=sq: (sq, d))
            for d in range(2):
                cout_ref[lsel(d)] = ct_scr[d].T
                nout_ref[lsel(d)] = n_scr[d][0:1, :]
                mout_ref[lsel(d)] = m_fin[d]
            if first_layer:
                for l in range(1, cout_ref.shape[1]):
                    cout_ref[sq, l] = jnp.zeros(cout_ref.shape[2:], F32)
                    nout_ref[sq, l] = jnp.zeros(nout_ref.shape[2:], F32)
                    mout_ref[sq, l] = jnp.zeros(mout_ref.shape[2:], F32)

    def finish_group(g, carry):
        cs = [g * grp + u for u in range(grp)]
        qt = [qt_ref[c] for c in cs]
        cq = [[_dot(cst_scr[d, c], qt[u]) for d in range(2)] for u, c in enumerate(cs)]
        nq = [[_dot(nst_scr[d, c].astype(BF16), qt[u])[0:1, :] for d in range(2)] for u, c in enumerate(cs)]
        for u, c in enumerate(cs):
            coef = []
            for d in range(2):
                _, brow, _ = gates(d, c)
                den_loc = rows_scr[d, c, 0:1, :]
                mloc = rows_scr[d, c, 1:2, :]
                inter = brow + rows_scr[d, c, 3:4, :]
                m_row = jnp.maximum(mloc, inter)
                w_loc = jnp.exp(mloc - m_row)
                w_inter = jnp.exp(inter - m_row)
                den = w_loc * den_loc + w_inter * nq[u][d]
                r_den = 1.0 / jnp.maximum(jnp.abs(den), jnp.exp(-m_row))
                coef.append((w_loc * r_den, w_inter * r_den))
            ssq = jnp.zeros((1, L), F32)
            for r0 in range(0, HEAD_DIM, STRIP):
                h_s = None
                for d in range(2):
                    h_d = coef[d][0] * part_scr[d, c, r0:r0 + STRIP, :] + coef[d][1] * cq[u][d][r0:r0 + STRIP, :]
                    h_s = h_d if h_s is None else h_s + h_d
                part_scr[0, c, r0:r0 + STRIP, :] = h_s
                ssq = ssq + jnp.sum(h_s * h_s, axis=0, keepdims=True)
            r_norm = lax.rsqrt(ssq * (1.0 / HEAD_DIM) + EPS)
            for r0 in range(0, HEAD_DIM, L):
                y = part_scr[0, c, r0:r0 + L, :] * r_norm * nw_ref[r0:r0 + L, :]
                o = jax.nn.sigmoid(ot_ref[c, r0:r0 + L, :].astype(F32)) * y
                hm_ref[pl.ds(_chunk_off(c), L), r0:r0 + L] = o.T.astype(hm_ref.dtype)
        return carry

    _group_loop(ns * nc // grp, finish_group)


def _state_specs(shapes, layer, first_layer, ns):
    specs = []
    for shp in shapes:
        tail = shp[4:]
        zeros = (0,) * len(tail)
        if first_layer:
            specs.append(pl.BlockSpec((ns, shp[1], 2, None) + tail, lambda b, h, z=zeros: (b, 0, 0, h) + z))
        else:
            specs.append(pl.BlockSpec((ns, None, 2, None) + tail, lambda b, h, z=zeros: (b, layer, 0, h) + z))
    return specs


def _mlstm_call(proj_t, proj_n, gp, nw_b, init, state_bufs, *, n_seq, seq_len, tok_off, n_heads, layer, depth):
    dk = HEAD_DIM
    nc = seq_len // CHUNK
    has_init = init is not None
    ns = 1 if has_init else _seqs_per_step(n_seq, nc)
    nct = ns * nc
    boff = tok_off // (ns * seq_len)
    write_state = not has_init
    first_layer = layer == 0
    args = [proj_t, proj_t, proj_t, proj_n, gp, nw_b]
    specs = [
        pl.BlockSpec((nct, dk, CHUNK), lambda b, h: (b + boff, h, 0)),
        pl.BlockSpec((nct, dk, CHUNK), lambda b, h: (b + boff, n_heads + h, 0)),
        pl.BlockSpec((nct, dk, CHUNK), lambda b, h: (b + boff, 2 * n_heads + h, 0)),
        pl.BlockSpec((ns * seq_len, dk), lambda b, h: (b + boff, h)),
        pl.BlockSpec((None, nct, 4, CHUNK), lambda b, h: (h, b + boff, 0, 0)),
        pl.BlockSpec((dk, LANES), lambda b, h: (h, 0)),
    ]
    aliases = {}
    if has_init:
        c0, n0, m0 = init
        args += [c0, n0[:, layer][:, :, :, None, :], m0[:, layer].reshape(-1)]
        specs += [
            pl.BlockSpec((None, None, 2, None, dk, dk), lambda b, h: (b, layer, 0, h, 0, 0)),
            pl.BlockSpec((None, 2, None, 1, dk), lambda b, h: (b, 0, h, 0, 0)),
            pl.BlockSpec(memory_space=pltpu.SMEM),
        ]
    out_shape = [jax.ShapeDtypeStruct((n_seq * seq_len, n_heads * dk), BF16)]
    out_specs = [pl.BlockSpec((ns * seq_len, dk), lambda b, h: (b, h))]
    if write_state:
        shapes = [(n_seq, depth, 2, n_heads, dk, dk), (n_seq, depth, 2, n_heads, 1, dk),
                  (n_seq, depth, 2, n_heads, 1, LANES)]
        if not first_layer:
            for k_, buf in enumerate(state_bufs):
                aliases[len(args)] = 1 + k_
                args.append(buf)
                specs.append(pl.BlockSpec(memory_space=pl.ANY))
        out_shape += [jax.ShapeDtypeStruct(s, F32) for s in shapes]
        out_specs += _state_specs(shapes, layer, first_layer, ns)
    kern = functools.partial(_mlstm_kernel, nc=nc, ns=ns, has_init=has_init, first_layer=first_layer,
                             write_state=write_state)
    return pl.pallas_call(
        kern, grid=(n_seq // ns, n_heads), in_specs=specs, out_specs=out_specs, out_shape=out_shape,
        input_output_aliases=aliases,
        scratch_shapes=[pltpu.VMEM((2, dk, dk), F32), pltpu.VMEM((2, 8, dk), F32),
                        pltpu.VMEM((2, nct, dk, CHUNK), F32), pltpu.VMEM((2, nct, dk, dk), F32),
                        pltpu.VMEM((2, nct, 8, dk), F32), pltpu.VMEM((2, nct, 8, CHUNK), F32),
                        pltpu.VMEM((2, nct, dk, dk), BF16), pltpu.VMEM((2, nct, 8, dk), F32),
                        pltpu.VMEM((2, nct, 8, dk), F32)],
        compiler_params=_cparams(("parallel", "parallel"), 48), name="mlstm_mixer",
    )(*args)


def _ret_kernel(*refs, nc, ns, has_init, first_layer, write_state):
    it = iter(refs)
    dl_ref, qt_ref, vt_ref, gt_ref, k_ref, nw_ref = (next(it) for _ in range(6))
    if has_init:
        s0_ref = next(it)
    if write_state and not first_layer:
        next(it)
    hr_ref = next(it)
    if write_state:
        sout_ref = next(it)
    st_scr, part_scr, inc_scr, sst_scr = next(it), next(it), next(it), next(it)

    L = CHUNK
    grp = min(MAX_GROUP, ns * nc)
    rowi = _iota((L, L), 0)
    coli = _iota((L, L), 1)
    k_scale = HEAD_DIM ** -0.5
    h_idx = pl.program_id(1)
    n_heads = pl.num_programs(1)

    consts = []
    for d in range(2):
        lg = _log_sigmoid(jnp.full((1, 1), dl_ref[d * n_heads + h_idx], F32))
        rel = (coli - rowi if d == 0 else rowi - coli).astype(F32)
        dmat_t = jnp.where(rel >= 0, jnp.exp(jnp.maximum(rel, 0.0) * lg), 0.0) * k_scale
        pos = _iota((1, L), 1).astype(F32)
        if d == 1:
            pos = (L - 1.0) - pos
        q_decay = jnp.exp((pos + 1.0) * lg)
        k_decay = jnp.exp((L - 1.0 - pos) * lg) * k_scale
        chunk_decay = jnp.exp(float(L) * lg)
        consts.append((dmat_t, q_decay, k_decay, chunk_decay))

    def local_group(g, carry):
        cs = [g * grp + u for u in range(grp)]
        qt = [qt_ref[c] for c in cs]
        vt = [vt_ref[c] for c in cs]
        k = [k_ref[pl.ds(_chunk_off(c), L), :] for c in cs]
        qk = [_dot(k[u], qt[u]) for u in range(grp)]
        for u, c in enumerate(cs):
            vf = vt[u].astype(F32)
            for d in range(2):
                inc_scr[d, c] = _dot((vf * consts[d][2]).astype(BF16), k[u])
        for u, c in enumerate(cs):
            for d in range(2):
                part_scr[d, c] = _dot(vt[u], (qk[u] * consts[d][0]).astype(BF16))
        return carry

    _group_loop(ns * nc // grp, local_group)

    for sq in range(ns):
        c_lo = sq * nc
        for d in range(2):
            if has_init:
                st_scr[d] = s0_ref[d].T
            else:
                st_scr[d] = jnp.zeros(st_scr.shape[1:], F32)

        def body(ci, carry, c_lo=c_lo):
            for d, c in ((0, c_lo + ci), (1, c_lo + nc - 1 - ci)):
                for r0 in range(0, HEAD_DIM, STRIP):
                    st = st_scr[d, r0:r0 + STRIP, :]
                    sst_scr[d, c, r0:r0 + STRIP, :] = st.astype(BF16)
                    st_scr[d, r0:r0 + STRIP, :] = consts[d][3] * st + inc_scr[d, c, r0:r0 + STRIP, :]
            return carry

        _scan_loop(nc, body, 0)

        if write_state:
            for d in range(2):
                if first_layer:
                    sout_ref[sq, 0, d] = st_scr[d].T
                else:
                    sout_ref[sq, d] = st_scr[d].T
            if first_layer:
                for l in range(1, sout_ref.shape[1]):
                    sout_ref[sq, l] = jnp.zeros(sout_ref.shape[2:], F32)

    def finish_group(g, carry):
        cs = [g * grp + u for u in range(grp)]
        qt = [qt_ref[c] for c in cs]
        sq = [[_dot(sst_scr[d, c], qt[u]) for d in range(2)] for u, c in enumerate(cs)]
        for u, c in enumerate(cs):
            ssq = jnp.zeros((1, L), F32)
            for r0 in range(0, HEAD_DIM, STRIP):
                sl = slice(r0, r0 + STRIP)
                o_s = ((part_scr[0, c, sl, :] + part_scr[1, c, sl, :])
                       + (consts[0][1] * sq[u][0][sl, :] + consts[1][1] * sq[u][1][sl, :]))
                part_scr[0, c, sl, :] = o_s
                ssq = ssq + jnp.sum(o_s * o_s, axis=0, keepdims=True)
            r_norm = lax.rsqrt(ssq * (1.0 / HEAD_DIM) + EPS)
            for r0 in range(0, HEAD_DIM, L):
                y = part_scr[0, c, r0:r0 + L, :] * r_norm * nw_ref[r0:r0 + L, :]
                rg = gt_ref[c, r0:r0 + L, :].astype(F32)
                hr_ref[pl.ds(_chunk_off(c), L), r0:r0 + L] = (rg * jax.nn.sigmoid(rg) * y).T.astype(hr_ref.dtype)
        return carry

    _group_loop(ns * nc // grp, finish_group)


def _ret_call(proj_t, proj_n, decay_logit, nw_b, init, state_buf, *, n_seq, seq_len, tok_off, n_heads, layer,
              depth):
    dk = HEAD_DIM
    nc = seq_len // CHUNK
    has_init = init is not None
    ns = 1 if has_init else _seqs_per_step(n_seq, nc)
    nct = ns * nc
    boff = tok_off // (ns * seq_len)
    write_state = not has_init
    first_layer = layer == 0
    args = [decay_logit.reshape(-1), proj_t, proj_t, proj_t, proj_n, nw_b]
    specs = [
        pl.BlockSpec(memory_space=pltpu.SMEM),
        pl.BlockSpec((nct, dk, CHUNK), lambda b, h: (b + boff, 3 * n_heads + h, 0)),
        pl.BlockSpec((nct, dk, CHUNK), lambda b, h: (b + boff, 4 * n_heads + h, 0)),
        pl.BlockSpec((nct, dk, CHUNK), lambda b, h: (b + boff, 5 * n_heads + h, 0)),
        pl.BlockSpec((ns * seq_len, dk), lambda b, h: (b + boff, n_heads + h)),
        pl.BlockSpec((dk, LANES), lambda b, h: (h, 0)),
    ]
    aliases = {}
    if has_init:
        args.append(init)
        specs.append(pl.BlockSpec((None, None, 2, None, dk, dk), lambda b, h: (b, layer, 0, h, 0, 0)))
    out_shape = [jax.ShapeDtypeStruct((n_seq * seq_len, n_heads * dk), BF16)]
    out_specs = [pl.BlockSpec((ns * seq_len, dk), lambda b, h: (b, h))]
    if write_state:
        shapes = [(n_seq, depth, 2, n_heads, dk, dk)]
        if not first_layer:
            aliases[len(args)] = 1
            args.append(state_buf)
            specs.append(pl.BlockSpec(memory_space=pl.ANY))
        out_shape += [jax.ShapeDtypeStruct(s, F32) for s in shapes]
        out_specs += _state_specs(shapes, layer, first_layer, ns)
    kern = functools.partial(_ret_kernel, nc=nc, ns=ns, has_init=has_init, first_layer=first_layer,
                             write_state=write_state)
    return pl.pallas_call(
        kern, grid=(n_seq // ns, n_heads), in_specs=specs, out_specs=out_specs, out_shape=out_shape,
        input_output_aliases=aliases,
        scratch_shapes=[pltpu.VMEM((2, dk, dk), F32), pltpu.VMEM((2, nct, dk, CHUNK), F32),
                        pltpu.VMEM((2, nct, dk, dk), F32), pltpu.VMEM((2, nct, dk, dk), BF16)],
        compiler_params=_cparams(("parallel", "parallel"), 48), name="retention_mixer",
    )(*args)


def _merge_kernel(hmp_ref, hms_ref, hrp_ref, hrs_ref, ga_ref, gb_ref, x_ref, mod_ref, nw_ref, wa_ref, wb_ref,
                  wo_ref, wr_ref, x1_ref, h2_ref, aff_ref, wab_scr, wbb_scr, wob_scr, *, n_prompt_tiles):
    i = pl.program_id(0)

    @pl.when(i == 0)
    def _():
        wab_scr[...] = wa_ref[...].astype(BF16)
        wbb_scr[...] = wb_ref[...].astype(BF16)
        wob_scr[...] = wo_ref[...].astype(BF16)

    is_prompt = i < n_prompt_tiles
    hm = jnp.where(is_prompt, hmp_ref[...], hms_ref[...])
    hr = jnp.where(is_prompt, hrp_ref[...], hrs_ref[...])
    ya = _dot(hm, wab_scr[...])
    yb = _dot(hr, wbb_scr[...])
    merged = (jax.nn.sigmoid(ga_ref[...].astype(F32)) * ya + jax.nn.sigmoid(gb_ref[...].astype(F32)) * yb)
    y = _dot(merged.astype(BF16), wob_scr[...])
    x1 = x_ref[...] + mod_ref[2:3, :] * y
    x1_ref[...] = x1
    h2 = x1 * lax.rsqrt(jnp.mean(x1 * x1, axis=-1, keepdims=True) + EPS) * nw_ref[...]
    h2 = h2 * (1.0 + mod_ref[4:5, :]) + mod_ref[3:4, :]
    n_blk = h2.shape[1] // LANES
    for fb in range(n_blk):
        _tile_rows(h2_ref, fb, h2.shape[0], n_blk)[...] = h2[:, fb * LANES:(fb + 1) * LANES]
    logits = _dot_nt(wr_ref[...].astype(BF16), h2.astype(BF16))
    p = jnp.exp(logits - jnp.max(logits, axis=0, keepdims=True))
    aff_ref[...] = p / jnp.sum(p, axis=0, keepdims=True)


def _merge_call(hm_p, hm_s, hr_p, hr_s, proj_n, x, mod, norm_w, w_a, w_b, w_out, router_wt, layer, *,
                n_prompt, dec_seq, tm=512):
    n, d = x.shape
    ne = router_wt.shape[1]
    npt = n_prompt // tm
    grp = functools.partial(_group_of_tile, tm=tm, n_prompt=n_prompt, dec_seq=dec_seq)
    tile = pl.BlockSpec((tm, d), lambda i: (i, 0))
    tile_p = pl.BlockSpec((tm, d), lambda i: (jnp.minimum(i, npt - 1), 0))
    tile_s = pl.BlockSpec((tm, d), lambda i: (jnp.maximum(i - npt, 0), 0))
    full = pl.BlockSpec((None, d, d), lambda i: (layer, 0, 0))
    return pl.pallas_call(
        functools.partial(_merge_kernel, n_prompt_tiles=npt), grid=(n // tm,),
        in_specs=[
            tile_p, tile_s, tile_p, tile_s,
            pl.BlockSpec((tm, d), lambda i: (i, 2)),
            pl.BlockSpec((tm, d), lambda i: (i, 3)),
            tile,
            pl.BlockSpec((None, 6, d), lambda i: (grp(i), 0, 0)),
            pl.BlockSpec((1, d), lambda i: (0, 0)),
            full, full, full,
            pl.BlockSpec((None, ne, d), lambda i: (layer, 0, 0)),
        ],
        out_specs=[tile, pl.BlockSpec((tm * (d // LANES), LANES), lambda i: (i, 0)),
                   pl.BlockSpec((ne, tm), lambda i: (0, i))],
        out_shape=[jax.ShapeDtypeStruct((n, d), F32), jax.ShapeDtypeStruct((n * (d // LANES), LANES), F32),
                   jax.ShapeDtypeStruct((ne, n), F32)],
        scratch_shapes=[pltpu.VMEM((d, d), BF16)] * 3,
        compiler_params=_cparams(("arbitrary",), 56), name="merge_out_router",
    )(hm_p, hm_s, hr_p, hr_s, proj_n, proj_n, x, mod, norm_w.reshape(1, d), w_a, w_b, w_out, router_wt)


ROUTE_GROUP = 4


def _route_kernel(a_ref, at_ref, idx_ref, gate_ref, thr_scr, *, cap, n_tok):
    n_sets = a_ref.shape[0]
    nb = n_tok // LANES
    a_all = a_ref[...]

    def as_f32(bits):
        return lax.bitcast_convert_type(bits, F32)

    def count_ge(cand):
        m = jnp.where(a_all >= cand, 1.0, 0.0)
        return jnp.sum(jnp.sum(m, axis=1, keepdims=True), axis=2, keepdims=True)

    def bit_step(k, thr):
        cand = thr | lax.shift_left(jnp.int32(1), 30 - k)
        return jnp.where(count_ge(as_f32(cand)) >= cap, cand, thr)

    thr_scr[...] = lax.fori_loop(0, 31, bit_step, jnp.zeros((n_sets, 1, 1), jnp.int32))

    r128 = _iota((LANES, LANES), 0)
    c128 = _iota((LANES, LANES), 1)
    upper = jnp.where(r128 <= c128, 1.0, 0.0).astype(BF16)
    lower_t = jnp.where(r128 >= c128, 1.0, 0.0).astype(BF16)
    rb = _iota((nb, nb), 0)
    cb = _iota((nb, nb), 1)
    blk_before_rows = jnp.where(cb < rb, 1.0, 0.0).astype(BF16)
    blk_before_cols = jnp.where(rb < cb, 1.0, 0.0).astype(BF16)

    def incl_counts(mask):
        within = _dot(mask.astype(BF16), upper)
        before = _dot(blk_before_rows, within.astype(BF16))[:, LANES - 1:LANES]
        return within + before

    def incl_counts_t(mask_t):
        within = _dot(lower_t, mask_t.astype(BF16))
        before = _dot(within.astype(BF16), blk_before_cols)[LANES - 1:LANES, :]
        return within + before

    slot = _iota((1, cap), 1).astype(F32)
    blk_col = _iota((nb, 1), 0).astype(F32)
    sub_col = _iota((LANES, 1), 0).astype(F32)

    def per_group(g, carry):
        sets = [g + u * (n_sets // ROUTE_GROUP) for u in range(ROUTE_GROUP)]
        stage = []
        for s in sets:
            thr_bits = thr_scr[s]
            thr = as_f32(thr_bits)
            nxt = as_f32(thr_bits + 1)
            a = a_ref[s]
            a_t = at_ref[s]
            gt = jnp.where(a >= nxt, 1.0, 0.0)
            eq = jnp.where((a >= thr) & (a < nxt), 1.0, 0.0)
            gt_t = jnp.where(a_t >= nxt, 1.0, 0.0)
            eq_t = jnp.where((a_t >= thr) & (a_t < nxt), 1.0, 0.0)
            n_gt = jnp.sum(jnp.sum(gt, axis=1, keepdims=True), axis=0, keepdims=True)
            stage.append((a_t, gt, eq, gt_t, eq_t, cap - n_gt))
        ties = [(incl_counts(eq), incl_counts_t(eq_t)) for _, _, eq, _, eq_t, _ in stage]
        sels = []
        for (a_t, gt, eq, gt_t, eq_t, need), (c_eq, c_eq_t) in zip(stage, ties):
            sels.append((gt + eq * jnp.where(c_eq - eq < need, 1.0, 0.0),
                         gt_t + eq_t * jnp.where(c_eq_t - eq_t < need, 1.0, 0.0)))
        cnts = [(incl_counts(sel), incl_counts_t(sel_t)) for sel, sel_t in sels]
        picks = []
        for (a_t, *_), (cnt, cnt_t) in zip(stage, cnts):
            blk_end = cnt[:, LANES - 1:LANES]
            blk_of_slot = jnp.sum(jnp.where(blk_end <= slot, 1.0, 0.0), axis=0, keepdims=True)
            onehot_blk = jnp.where(blk_col == blk_of_slot, 1.0, 0.0).astype(BF16)
            cnt_rows = _dot3(cnt_t, onehot_blk)
            a_rows = _dot3(a_t, onehot_blk)
            picks.append((blk_of_slot, cnt_rows, a_rows))
        for s, (blk_of_slot, cnt_rows, a_rows) in zip(sets, picks):
            sub_of_slot = jnp.sum(jnp.where(cnt_rows <= slot, 1.0, 0.0), axis=0, keepdims=True)
            gate = jnp.sum(jnp.where(sub_col == sub_of_slot, a_rows, 0.0), axis=0, keepdims=True)
            idx_ref[s] = (blk_of_slot * LANES + sub_of_slot).astype(jnp.int32)
            gate_ref[s] = gate
        return carry

    lax.fori_loop(0, n_sets // ROUTE_GROUP, per_group, 0)


def _route_call(aff_t, n_pass, cap):
    ne, n = aff_t.shape
    n_tok = n // n_pass
    nb = n_tok // LANES
    a4 = aff_t.reshape(ne, n_pass, nb, LANES).transpose(1, 0, 2, 3).reshape(n_pass * ne, nb, LANES)
    a4_t = a4.transpose(0, 2, 1)
    n_sets = n_pass * ne
    return pl.pallas_call(
        functools.partial(_route_kernel, cap=cap, n_tok=n_tok),
        out_shape=[jax.ShapeDtypeStruct((n_sets, 1, cap), jnp.int32),
                   jax.ShapeDtypeStruct((n_sets, 1, cap), F32)],
        scratch_shapes=[pltpu.VMEM((n_sets, 1, 1), jnp.int32)],
        compiler_params=pltpu.CompilerParams(vmem_limit_bytes=32 * MIB), name="expert_choice_route",
    )(a4, a4_t)


ROW_LOOP_UNROLL = 8


def _moe_kernel(idx_ref, gate_ref, h_hbm, w1_ref, w3_ref, w2_ref, out_hbm,
                xe_scr, xb_scr, ye_a, ye_b, acc_scr, gsem, osem, *, rows, rows_pad):
    e = pl.program_id(0)
    f = pl.program_id(1)
    n_e = pl.num_programs(0)
    n_f = pl.num_programs(1)
    chunk = rows_pad // n_f

    n_blk = xb_scr.shape[1] // LANES

    def tile_of(t):
        return pl.ds(pl.multiple_of(t * n_blk, n_blk), n_blk)

    def gather_row(slot, s):
        tok = idx_ref[slot * rows_pad + s]
        pltpu.make_async_copy(h_hbm.at[tile_of(tok), :], xe_scr.at[tile_of(s), :], gsem).start()

    def scatter_row(slot, s, ye_ref):
        tok = idx_ref[slot * rows_pad + s]
        g = gate_ref[slot * rows_pad + s]
        acc_scr[pl.ds(tok, 1), :] += ye_ref[pl.ds(s, 1), :] * g

    def wait_gather():
        pltpu.make_async_copy(h_hbm.at[pl.ds(0, rows_pad * n_blk), :], xe_scr, gsem).wait()

    @pl.when((e == 0) & (f == 0))
    def _():
        acc_scr[...] = jnp.zeros_like(acc_scr)
        ye_a[...] = jnp.zeros_like(ye_a)
        ye_b[...] = jnp.zeros_like(ye_b)

        def issue(s, carry):
            gather_row(1, s)
            return carry
        lax.fori_loop(0, rows_pad, issue, 0, unroll=ROW_LOOP_UNROLL)

    def step(ye_cur, ye_prev):
        @pl.when(f == 0)
        def _():
            wait_gather()
            for fb in range(n_blk):
                xb_scr[:, fb * LANES:(fb + 1) * LANES] = _tile_rows(xe_scr, fb, rows, n_blk)[...].astype(BF16)
            ye_cur[0:rows, :] = jnp.zeros((rows, ye_cur.shape[1]), F32)

        xb = xb_scr[...]
        h1 = _dot(xb, w1_ref[...].astype(BF16))
        h3 = _dot(xb, w3_ref[...].astype(BF16))
        he = (h1 * jax.nn.sigmoid(h1) * h3).astype(BF16)
        ye_cur[0:rows, :] += _dot(he, w2_ref[...].astype(BF16))
        base = f * chunk
        for r in range(chunk):
            gather_row(e + 2, base + r)
        for r in range(chunk):
            scatter_row(e, base + r, ye_prev)

        @pl.when((e == n_e - 1) & (f == n_f - 1))
        def _():
            def scatter(s, carry):
                scatter_row(n_e, s, ye_cur)
                return carry
            lax.fori_loop(0, rows, scatter, 0, unroll=ROW_LOOP_UNROLL)
            wait_gather()
            cp = pltpu.make_async_copy(acc_scr, out_hbm, osem)
            cp.start()
            cp.wait()

    @pl.when(e % 2 == 0)
    def _():
        step(ye_a, ye_b)

    @pl.when(e % 2 == 1)
    def _():
        step(ye_b, ye_a)


def _moe_call(idx, gate, h2, w1, w3, w2, layer, tf=256):
    d = w1.shape[2]
    n_blk = d // LANES
    n = h2.shape[0] // n_blk
    ne, rows = idx.shape
    dff = w1.shape[3]
    nf = dff // tf
    rows_pad = -(-rows // (8 * nf)) * 8 * nf
    pad = ((1, 1), (0, rows_pad - rows))
    idx_all = jnp.pad(idx, pad).reshape(-1)
    gate_all = jnp.pad(gate, pad).reshape(-1)
    grid_spec = pltpu.PrefetchScalarGridSpec(
        num_scalar_prefetch=1,
        grid=(ne, nf),
        in_specs=[
            pl.BlockSpec(memory_space=pltpu.SMEM),
            pl.BlockSpec(memory_space=pl.ANY),
            pl.BlockSpec((None, None, d, tf), lambda e, f, idx: (layer, e, 0, f)),
            pl.BlockSpec((None, None, d, tf), lambda e, f, idx: (layer, e, 0, f)),
            pl.BlockSpec((None, None, tf, d), lambda e, f, idx: (layer, e, f, 0)),
        ],
        out_specs=pl.BlockSpec(memory_space=pl.ANY),
        scratch_shapes=[
            pltpu.VMEM((rows_pad * n_blk, LANES), F32), pltpu.VMEM((rows, d), BF16),
            pltpu.VMEM((rows_pad, d), F32), pltpu.VMEM((rows_pad, d), F32),
            pltpu.VMEM((n, d), F32), pltpu.SemaphoreType.DMA, pltpu.SemaphoreType.DMA,
        ],
    )
    return pl.pallas_call(
        functools.partial(_moe_kernel, rows=rows, rows_pad=rows_pad), grid_spec=grid_spec,
        out_shape=jax.ShapeDtypeStruct((n, d), F32),
        compiler_params=_cparams(("arbitrary", "arbitrary"), 62), name="expert_ffn",
    )(idx_all, gate_all, h2, w1, w3, w2)


def _rope_tables(t):
    rows = t // GRID_W
    row = jnp.repeat(jnp.arange(rows, dtype=F32), GRID_W)
    colp = jnp.tile(jnp.arange(GRID_W, dtype=F32), rows)
    n_freq = HEAD_DIM // 4
    inv = ROPE_BASE ** (-jnp.arange(n_freq, dtype=F32) / n_freq)
    ang = jnp.concatenate([row[:, None] * inv, colp[:, None] * inv], axis=-1)
    return jnp.cos(ang), jnp.sin(ang)


def kernel(x_prompt, x_sample, c, state_mlstm_C, state_mlstm_n, state_mlstm_m, state_ret_S, c_ctx, w_mod, b_mod,
           norm1_w, norm2_w, w_in, mlstm_if_b, mlstm_norm_w, ret_decay_logit, ret_norm_w, w_branch_a, w_branch_b,
           w_out, router_w, ffn_w1, ffn_w3, ffn_w2, final_norm_w):
    bp, seq, d = x_prompt.shape
    db, dec_seq, _ = x_sample.shape
    depth = w_mod.shape[0]
    hm = mlstm_if_b.shape[-1]
    hr = ret_decay_logit.shape[-1]
    ne = router_w.shape[-1]
    n_prompt = bp * seq
    n_sample = db * dec_seq
    assert n_prompt == n_sample and hm * HEAD_DIM == d and hr * HEAD_DIM == d and hm == hr
    cap = EC_FACTOR * n_prompt // ne
    tm = 512
    tp = 2048
    geo = dict(n_prompt=n_prompt, dec_seq=dec_seq)

    cond8 = jnp.concatenate([c_ctx[None, :], c, jnp.zeros((8 - 1 - db, d), F32)], axis=0)
    mod = _mod_call(cond8, w_mod, b_mod)[:, :1 + db].reshape(depth, 1 + db, 6, d)

    n_m = 4 * d
    g0 = n_m + 4 * hm

    w_t = jnp.swapaxes(w_in, 1, 2)
    mq, mk, mv, mo = (k * d for k in range(4))
    rq, rk, rv, rg, ga, gb = (g0 + k * d for k in range(6))
    wg_t = w_t[:, n_m:g0]
    gate_bias = jnp.transpose(mlstm_if_b, (0, 2, 1, 3)).reshape(depth, 4 * hm, 1)
    router_wt = jnp.swapaxes(router_w, 1, 2)
    nw_m = jnp.broadcast_to(mlstm_norm_w[:, :, None], (depth, d, LANES))
    nw_r = jnp.broadcast_to(ret_norm_w[:, :, None], (depth, d, LANES))

    cos, sin = _rope_tables(dec_seq)
    cos_t = cos.T.reshape(HEAD_DIM // 2, dec_seq // tp, tp).transpose(1, 0, 2)
    sin_t = sin.T.reshape(HEAD_DIM // 2, dec_seq // tp, tp).transpose(1, 0, 2)

    x, h = _resid_norm_call(x_prompt.reshape(n_prompt, d), None, None, mod[0], norm1_w[0], gate_row=0,
                            mod_rows=(0, 1), h_dtype=BF16, write_x=True, x_tail=x_sample.reshape(n_sample, d),
                            **geo)
    m_bufs, s_buf = None, None
    y_prompt = y_sample = None
    for l in range(depth):
        proj_t = _proj_t_call(h, w_t, l, (mq, mv, mo, rq, rv, rg), (cos_t, sin_t), 3, tm=tp, **geo)
        proj_n = _proj_call(h, w_t, l, (mk, rk, ga, gb), (cos, sin), 1, tm=tp, **geo)
        gp = _gate_call(h, wg_t[l], gate_bias[l], hm, tm=tm)

        mix = dict(n_heads=hm, layer=l, depth=depth)
        hm_p, *m_bufs = _mlstm_call(proj_t, proj_n, gp, nw_m[l], None, m_bufs, n_seq=bp, seq_len=seq, tok_off=0,
                                    **mix)
        (hm_s,) = _mlstm_call(proj_t, proj_n, gp, nw_m[l], (state_mlstm_C, state_mlstm_n, state_mlstm_m), None,
                              n_seq=db, seq_len=dec_seq, tok_off=n_prompt, **mix)
        hr_p, s_buf = _ret_call(proj_t, proj_n, ret_decay_logit[l], nw_r[l], None, s_buf, n_seq=bp, seq_len=seq,
                                tok_off=0, **mix)
        (hr_s,) = _ret_call(proj_t, proj_n, ret_decay_logit[l], nw_r[l], state_ret_S, None, n_seq=db,
                            seq_len=dec_seq, tok_off=n_prompt, **mix)

        x1, h2, aff_t = _merge_call(hm_p, hm_s, hr_p, hr_s, proj_n, x, mod[l], norm2_w[l], w_branch_a, w_branch_b,
                                    w_out, router_wt, l, tm=tm, **geo)
        idx, gate = _route_call(aff_t, 2, cap)
        idx = idx.reshape(2, ne, cap) + (jnp.arange(2, dtype=jnp.int32) * n_prompt)[:, None, None]
        idx = jnp.swapaxes(idx, 0, 1).reshape(ne, 2 * cap)
        gate = jnp.swapaxes(gate.reshape(2, ne, cap), 0, 1).reshape(ne, 2 * cap)
        moe = _moe_call(idx, gate, h2, ffn_w1, ffn_w3, ffn_w2, l)
        if l + 1 < depth:
            x, h = _resid_norm_call(x1, moe, mod[l], mod[l + 1], norm1_w[l + 1], gate_row=5, mod_rows=(0, 1),
                                    h_dtype=BF16, write_x=True, **geo)
        else:
            fin = dict(gate_row=5, mod_rows=None, h_dtype=F32, write_x=False, **geo)
            _, y_prompt = _resid_norm_call(x1, moe, mod[l], None, final_norm_w, row_off=0, n_rows=n_prompt, **fin)
            _, y_sample = _resid_norm_call(x1, moe, mod[l], None, final_norm_w, row_off=n_prompt, n_rows=n_sample,
                                           **fin)

    c_buf, n_buf, m_buf = m_bufs
    return (y_prompt.reshape(bp, seq, d), y_sample.reshape(db, dec_seq, d), c_buf, n_buf[:, :, :, :, 0, :],
            m_buf[:, :, :, :, 0, 0], s_buf)
```

```python
import functools

import numpy as np
import jax
import jax.numpy as jnp
from jax import lax
from jax.experimental import pallas as pl
from jax.experimental.pallas import tpu as pltpu

F32 = jnp.float32
BF16 = jnp.bfloat16

GRID_W = 64
CHUNK = 128
LANES = 128
HEAD_DIM = 256
N_EXPERTS = 16
EC_FACTOR = 2
ROPE_BASE = 10000.0
EPS = 1e-6
V7X_VMEM_BYTES = 64 * 1024 * 1024
MIB = 1024 * 1024


def _cparams(semantics, vmem_mib):
    assert vmem_mib * MIB < V7X_VMEM_BYTES
    return pltpu.CompilerParams(dimension_semantics=semantics, vmem_limit_bytes=vmem_mib * MIB)


def _dot(a, b):
    return jnp.dot(a, b, preferred_element_type=F32)


def _dot_nt(a, b):
    return lax.dot_general(a, b, (((1,), (1,)), ((), ())), preferred_element_type=F32)


def _log_sigmoid(x):
    return -(jnp.maximum(-x, 0.0) + jnp.log1p(jnp.exp(-jnp.abs(x))))


def _split3(x):
    hi = x.astype(BF16)
    r1 = x - hi.astype(F32)
    mid = r1.astype(BF16)
    lo = (r1 - mid.astype(F32)).astype(BF16)
    return hi, mid, lo


def _dot3(x, m):
    hi, mid, lo = _split3(x)
    return _dot(hi, m) + _dot(mid, m) + _dot(lo, m)


def _iota(shape, dim):
    return lax.broadcasted_iota(jnp.int32, shape, dim)


def _tile_rows(ref, fb, n_tok, n_blk):
    return ref.at[pl.ds(fb, n_tok, stride=n_blk), :]


def _mod_kernel(cond_ref, w_ref, b_ref, out_ref):
    c = cond_ref[...]
    s = (c * jax.nn.sigmoid(c)).astype(BF16)
    out_ref[...] = _dot(s, w_ref[...].astype(BF16)) + b_ref[...]


def _mod_call(cond8, w_mod, b_mod):
    depth, d, w6 = w_mod.shape
    tn = 1536
    return pl.pallas_call(
        _mod_kernel,
        grid=(depth, w6 // tn),
        in_specs=[
            pl.BlockSpec((8, d), lambda l, j: (0, 0)),
            pl.BlockSpec((None, d, tn), lambda l, j: (l, 0, j)),
            pl.BlockSpec((None, 1, tn), lambda l, j: (l, 0, j)),
        ],
        out_specs=pl.BlockSpec((None, 8, tn), lambda l, j: (l, 0, j)),
        out_shape=jax.ShapeDtypeStruct((depth, 8, w6), F32),
        compiler_params=_cparams(("parallel", "parallel"), 32),
        name="adaln_mod",
    )(cond8, w_mod, b_mod.reshape(depth, 1, w6))


def _group_of_tile(i, tm, n_prompt, dec_seq):
    return jnp.maximum(i * tm - n_prompt + dec_seq, 0) // dec_seq


def _resid_norm_kernel(*refs, has_delta, has_mod, gate_row, mod_rows, write_x, n_first_tiles):
    it = iter(refs)
    x_ref = next(it)
    x2_ref = next(it) if n_first_tiles else None
    delta_ref = next(it) if has_delta else None
    mod_ref = next(it) if (has_delta or has_mod) else None
    nw_ref = next(it)
    xo_ref = next(it) if write_x else None
    h_ref = next(it)
    x = x_ref[...]
    if n_first_tiles:
        x = jnp.where(pl.program_id(0) < n_first_tiles, x, x2_ref[...])
    if has_delta:
        x = x + mod_ref[gate_row:gate_row + 1, :] * delta_ref[...]
    if write_x:
        xo_ref[...] = x
    y = x * lax.rsqrt(jnp.mean(x * x, axis=-1, keepdims=True) + EPS) * nw_ref[...]
    if has_mod:
        sh_row, sc_row = mod_rows
        y = y * (1.0 + mod_ref[sc_row:sc_row + 1, :]) + mod_ref[sh_row:sh_row + 1, :]
    h_ref[...] = y.astype(h_ref.dtype)


def _resid_norm_call(x, delta, mod_gate, mod_norm, norm_w, *, gate_row, mod_rows, h_dtype, write_x,
                     n_prompt, dec_seq, tm=512, row_off=0, n_rows=None, x_tail=None):
    n, d = x.shape
    has_delta = delta is not None
    has_mod = mod_norm is not None
    grp = functools.partial(_group_of_tile, tm=tm, n_prompt=n_prompt, dec_seq=dec_seq)
    if x_tail is None:
        n_rows = n if n_rows is None else n_rows
        toff = row_off // tm
        nft = 0
        tile_in = pl.BlockSpec((tm, d), lambda i: (i + toff, 0))
        args, specs = [x], [tile_in]
    else:
        assert row_off == 0 and n_rows is None and not has_delta
        n_rows, toff, nft = n + x_tail.shape[0], 0, n // tm
        args = [x, x_tail]
        specs = [pl.BlockSpec((tm, d), lambda i: (jnp.minimum(i, nft - 1), 0)),
                 pl.BlockSpec((tm, d), lambda i: (jnp.maximum(i - nft, 0), 0))]
    tile_out = pl.BlockSpec((tm, d), lambda i: (i, 0))
    if has_delta:
        args.append(delta)
        specs.append(tile_in)
    if has_delta or has_mod:
        mg = mod_gate if has_delta else mod_norm
        mn = mod_norm if has_mod else mod_gate
        args.append(jnp.concatenate([mg, mn], axis=1))
        specs.append(pl.BlockSpec((None, 12, d), lambda i: (grp(i + toff), 0, 0)))
    args.append(norm_w.reshape(1, d))
    specs.append(pl.BlockSpec((1, d), lambda i: (0, 0)))
    out_shape, out_specs = [], []
    if write_x:
        out_shape.append(jax.ShapeDtypeStruct((n_rows, d), F32))
        out_specs.append(tile_out)
    out_shape.append(jax.ShapeDtypeStruct((n_rows, d), h_dtype))
    out_specs.append(tile_out)
    kern = functools.partial(
        _resid_norm_kernel, has_delta=has_delta, has_mod=has_mod, gate_row=gate_row,
        mod_rows=None if mod_rows is None else (6 + mod_rows[0], 6 + mod_rows[1]), write_x=write_x,
        n_first_tiles=nft)
    outs = pl.pallas_call(
        kern, grid=(n_rows // tm,), in_specs=specs, out_specs=out_specs, out_shape=out_shape,
        compiler_params=_cparams(("parallel",), 32), name="resid_norm",
    )(*args)
    return outs if write_x else (None, outs[0])


def _rope_pair(x1, x2, cos, sin):
    return x1 * cos - x2 * sin, x1 * sin + x2 * cos


def _weight_rows_spec(layer, row_starts, tn, d):
    def index_map(j, i):
        start = sum(jnp.where(j == k, s, 0) for k, s in enumerate(row_starts))
        return layer, pl.multiple_of(start, 8), 0
    assert all(s % 8 == 0 for s in row_starts)
    return pl.BlockSpec((pl.Element(1), pl.Element(tn), pl.Element(d)), index_map)


def _proj_kernel(h_ref, w_ref, cos_ref, sin_ref, out_ref, wb_scr, *, tm, rope_tile, n_prompt, dec_seq):
    j = pl.program_id(0)
    i = pl.program_id(1)

    @pl.when(i == 0)
    def _():
        wb_scr[...] = w_ref[0].astype(BF16)

    row0 = pl.multiple_of(i * tm, tm)
    use_rope = (j == rope_tile) & (row0 >= n_prompt)
    half = HEAD_DIM // 2

    def heads(rope):
        h = h_ref[...]
        if rope:
            pos0 = pl.multiple_of((row0 - n_prompt) % dec_seq, tm)
            cos = cos_ref[pl.ds(pos0, tm), :]
            sin = sin_ref[pl.ds(pos0, tm), :]
        for hh in range(wb_scr.shape[0] // HEAD_DIM):
            c0 = hh * HEAD_DIM
            acc = _dot_nt(h, wb_scr[c0:c0 + HEAD_DIM, :])
            if rope:
                y1, y2 = _rope_pair(acc[:, :half], acc[:, half:], cos, sin)
                out_ref[:, c0:c0 + half] = y1.astype(out_ref.dtype)
                out_ref[:, c0 + half:c0 + HEAD_DIM] = y2.astype(out_ref.dtype)
            else:
                out_ref[:, c0:c0 + HEAD_DIM] = acc.astype(out_ref.dtype)

    pl.when(use_rope)(functools.partial(heads, True))
    pl.when(jnp.logical_not(use_rope))(functools.partial(heads, False))


def _proj_call(h, w_t, layer, row_starts, rope, rope_tile, *, n_prompt, dec_seq, tm=512, tn=1024):
    n, d = h.shape
    nj = len(row_starts)
    kern = functools.partial(_proj_kernel, tm=tm, rope_tile=rope_tile, n_prompt=n_prompt, dec_seq=dec_seq)
    return pl.pallas_call(
        kern, grid=(nj, n // tm),
        in_specs=[
            pl.BlockSpec((tm, d), lambda j, i: (i, 0)),
            _weight_rows_spec(layer, row_starts, tn, d),
            pl.BlockSpec(rope[0].shape, lambda j, i: (0, 0)),
            pl.BlockSpec(rope[1].shape, lambda j, i: (0, 0)),
        ],
        out_specs=pl.BlockSpec((tm, tn), lambda j, i: (i, j)),
        out_shape=jax.ShapeDtypeStruct((n, nj * tn), BF16),
        scratch_shapes=[pltpu.VMEM((tn, d), BF16)],
        compiler_params=_cparams(("parallel", "arbitrary"), 56), name="in_proj",
    )(h, w_t, *rope)


def _proj_t_kernel(h_ref, w_ref, cos_ref, sin_ref, out_ref, wb_scr, *, tm, rope_tile, n_prompt, dec_seq):
    j = pl.program_id(0)
    i = pl.program_id(1)

    @pl.when(i == 0)
    def _():
        wb_scr[...] = w_ref[0].astype(BF16)

    row0 = pl.multiple_of(i * tm, tm)
    use_rope = (j == rope_tile) & (row0 >= n_prompt)
    half = HEAD_DIM // 2

    def heads(rope):
        h = h_ref[...]
        if rope:
            blk = ((row0 - n_prompt) % dec_seq) // tm
            cos = cos_ref[blk]
            sin = sin_ref[blk]
        for hh in range(wb_scr.shape[0] // HEAD_DIM):
            r0 = hh * HEAD_DIM
            acc = _dot_nt(wb_scr[r0:r0 + HEAD_DIM, :], h)
            if rope:
                y1, y2 = _rope_pair(acc[:half, :], acc[half:, :], cos, sin)
                y = jnp.concatenate([y1, y2], axis=0).astype(out_ref.dtype)
            else:
                y = acc.astype(out_ref.dtype)
            for s in range(tm // CHUNK):
                out_ref[s, r0:r0 + HEAD_DIM, :] = y[:, s * CHUNK:(s + 1) * CHUNK]

    pl.when(use_rope)(functools.partial(heads, True))
    pl.when(jnp.logical_not(use_rope))(functools.partial(heads, False))


def _proj_t_call(h, w_t, layer, row_starts, rope_t, rope_tile, *, n_prompt, dec_seq, tm=512, tn=1024):
    n, d = h.shape
    nj = len(row_starts)
    kern = functools.partial(_proj_t_kernel, tm=tm, rope_tile=rope_tile, n_prompt=n_prompt, dec_seq=dec_seq)
    return pl.pallas_call(
        kern, grid=(nj, n // tm),
        in_specs=[
            pl.BlockSpec((tm, d), lambda j, i: (i, 0)),
            _weight_rows_spec(layer, row_starts, tn, d),
            pl.BlockSpec(rope_t[0].shape, lambda j, i: (0, 0, 0)),
            pl.BlockSpec(rope_t[1].shape, lambda j, i: (0, 0, 0)),
        ],
        out_specs=pl.BlockSpec((tm // CHUNK, tn, CHUNK), lambda j, i: (i, j, 0)),
        out_shape=jax.ShapeDtypeStruct((n // CHUNK, nj * tn, CHUNK), BF16),
        scratch_shapes=[pltpu.VMEM((tn, d), BF16)],
        compiler_params=_cparams(("parallel", "arbitrary"), 56), name="in_proj_t",
    )(h, w_t, *rope_t)


def _gate_kernel(h_ref, wg_ref, bias_ref, out_ref, *, tm, n_heads):
    g = _dot_nt(wg_ref[...].astype(BF16), h_ref[...]) + bias_ref[...]
    nd = 2 * n_heads
    ig = g[0:nd, :]
    lf = _log_sigmoid(g[nd:2 * nd, :])
    r = _iota((CHUNK, CHUNK), 0)
    c = _iota((CHUNK, CHUNK), 1)
    upper = jnp.where(r <= c, 1.0, 0.0).astype(BF16)
    lower = jnp.where(r >= c, 1.0, 0.0).astype(BF16)
    is_fwd = _iota((nd, CHUNK), 0) < n_heads
    for s in range(tm // CHUNK):
        sl = slice(s * CHUNK, (s + 1) * CHUNK)
        lf_c = lf[:, sl]
        b = jnp.where(is_fwd, _dot3(lf_c, upper), _dot3(lf_c, lower))
        ig_c = ig[:, sl]
        for hh in range(n_heads):
            out_ref[hh, s, 0:1, :] = ig_c[hh:hh + 1, :]
            out_ref[hh, s, 1:2, :] = ig_c[n_heads + hh:n_heads + hh + 1, :]
            out_ref[hh, s, 2:3, :] = b[hh:hh + 1, :]
            out_ref[hh, s, 3:4, :] = b[n_heads + hh:n_heads + hh + 1, :]


def _gate_call(h, wg_t, bias_col, n_heads, tm=512):
    n, d = h.shape
    ng = 4 * n_heads
    return pl.pallas_call(
        functools.partial(_gate_kernel, tm=tm, n_heads=n_heads),
        grid=(n // tm,),
        in_specs=[
            pl.BlockSpec((tm, d), lambda i: (i, 0)),
            pl.BlockSpec((ng, d), lambda i: (0, 0)),
            pl.BlockSpec((ng, 1), lambda i: (0, 0)),
        ],
        out_specs=pl.BlockSpec((n_heads, tm // CHUNK, 4, CHUNK), lambda i: (0, i, 0, 0)),
        out_shape=jax.ShapeDtypeStruct((n_heads, n // CHUNK, 4, CHUNK), F32),
        compiler_params=_cparams(("parallel",), 32), name="mlstm_gates",
    )(h, wg_t, bias_col)


def _scan_loop(nc, body, init):
    if nc <= 2:
        carry = init
        for ci in range(nc):
            carry = body(ci, carry)
        return carry
    return lax.fori_loop(0, nc, body, init)


MAX_GROUP = 16
STRIP = 32


def _seqs_per_step(n_seq, nc):
    ns = max(1, MAX_GROUP // nc)
    while n_seq % ns:
        ns -= 1
    return ns


def _group_loop(n_groups, body):
    if n_groups == 1:
        body(0, 0)
    else:
        lax.fori_loop(0, n_groups, body, 0)


def _chunk_off(c):
    return c * CHUNK if isinstance(c, int) else pl.multiple_of(c * CHUNK, CHUNK)


def _mlstm_kernel(*refs, nc, ns, has_init, first_layer, write_state):
    it = iter(refs)
    qt_ref, vt_ref, ot_ref, k_ref, gp_ref, nw_ref = (next(it) for _ in range(6))
    if has_init:
        c0_ref, n0_ref, m0_ref = next(it), next(it), next(it)
    if write_state and not first_layer:
        next(it), next(it), next(it)
    hm_ref = next(it)
    if write_state:
        cout_ref, nout_ref, mout_ref = next(it), next(it), next(it)
    ct_scr, n_scr, part_scr, inc_scr, ninc_scr, rows_scr, cst_scr, nst_scr, coef_scr = (next(it) for _ in range(9))

    L = CHUNK
    grp = min(MAX_GROUP, ns * nc)
    row = _iota((L, L), 0)
    col = _iota((L, L), 1)
    k_scale = HEAD_DIM ** -0.5
    b_idx = pl.program_id(0)
    h_idx = pl.program_id(1)
    n_heads = pl.num_programs(1)

    def gates(d, c):
        g = gp_ref[c]
        ig = g[d:d + 1, :]
        brow = g[2 + d:3 + d, :]
        blast = brow[:, L - 1:L] if d == 0 else brow[:, 0:1]
        return ig, brow, blast

    def local_group(g, carry):
        cs = [g * grp + u for u in range(grp)]
        pairs = [(u, d) for u in range(grp) for d in range(2)]
        qt = [qt_ref[c] for c in cs]
        vt = [vt_ref[c] for c in cs]
        k = [k_ref[pl.ds(_chunk_off(c), L), :] for c in cs]
        qk = [_dot(k[u], qt[u]) for u in range(grp)]
        vf = [v.astype(F32) for v in vt]
        gts = {(u, d): gates(d, cs[u]) for u, d in pairs}
        for u, d in pairs:
            c = cs[u]
            ig, brow, blast = gts[u, d]
            wj = blast - brow + ig
            mloc2 = jnp.max(wj, axis=1, keepdims=True)
            e = jnp.exp(wj - mloc2) * k_scale
            inc_scr[d, c] = _dot((vf[u] * e).astype(BF16), k[u])
            ninc_scr[d, c] = _dot3(jnp.broadcast_to(e, (8, L)), k[u])
            rows_scr[d, c, 2:3, :] = jnp.broadcast_to(mloc2, (1, L))
            rows_scr[d, c, 4:5, :] = jnp.broadcast_to(blast, (1, L))
        for u, d in pairs:
            c = cs[u]
            ig, brow, blast = gts[u, d]
            key_term = jnp.broadcast_to(ig - brow, (L, L)).T
            causal = (row <= col) if d == 0 else (row >= col)
            dm = jnp.where(causal, key_term + brow, -jnp.inf)
            mloc = jnp.max(dm, axis=0, keepdims=True)
            s = qk[u] * (jnp.exp(dm - mloc) * k_scale)
            part_scr[d, c] = _dot(vt[u], s.astype(BF16))
            rows_scr[d, c, 0:1, :] = jnp.sum(s, axis=0, keepdims=True)
            rows_scr[d, c, 1:2, :] = mloc
        return carry

    _group_loop(ns * nc // grp, local_group)

    def m_step(d, c, m):
        blast = rows_scr[d, c, 4:5, :]
        mloc2 = rows_scr[d, c, 2:3, :]
        rows_scr[d, c, 3:4, :] = m
        m_new = jnp.maximum(blast + m, mloc2)
        decay = jnp.exp(blast + m - m_new)
        w_inc = jnp.exp(mloc2 - m_new)
        coef_scr[d, c, 0:1, :] = jnp.concatenate([decay] * (HEAD_DIM // L), axis=1)
        coef_scr[d, c, 1:2, :] = jnp.concatenate([w_inc] * (HEAD_DIM // L), axis=1)
        return m_new

    for sq in range(ns):
        c_lo = sq * nc
        m_init = []
        for d in range(2):
            if has_init:
                ct_scr[d] = c0_ref[d].T
                n_scr[d] = jnp.broadcast_to(n0_ref[d], n_scr.shape[1:])
                m_init.append(jnp.full((1, L), m0_ref[(b_idx * 2 + d) * n_heads + h_idx], F32))
            else:
                ct_scr[d] = jnp.zeros(ct_scr.shape[1:], F32)
                n_scr[d] = jnp.zeros(n_scr.shape[1:], F32)
                m_init.append(jnp.zeros((1, L), F32))

        def m_body(ci, ms, c_lo=c_lo):
            return m_step(0, c_lo + ci, ms[0]), m_step(1, c_lo + nc - 1 - ci, ms[1])

        m_fin = _scan_loop(nc, m_body, tuple(m_init))

        def body(ci, carry, c_lo=c_lo):
            for d, c in ((0, c_lo + ci), (1, c_lo + nc - 1 - ci)):
                decay = coef_scr[d, c, 0:1, :]
                w_inc = coef_scr[d, c, 1:2, :]
                n8 = n_scr[d]
                nst_scr[d, c] = n8
                for r0 in range(0, HEAD_DIM, STRIP):
                    ct = ct_scr[d, r0:r0 + STRIP, :]
                    cst_scr[d, c, r0:r0 + STRIP, :] = ct.astype(BF16)
                    ct_scr[d, r0:r0 + STRIP, :] = decay * ct + w_inc * inc_scr[d, c, r0:r0 + STRIP, :]
                n_scr[d] = decay * n8 + w_inc * ninc_scr[d, c]
            return carry

        _scan_loop(nc, body, 0)

        if write_state:
            lsel = (lambda d, sq=sq: (sq, 0, d)) if first_layer else (lambda d, sq=sq: (sq, d))
            for d in range(2):
                cout_ref[lsel(d)] = ct_scr[d].T
                nout_ref[lsel(d)] = n_scr[d][0:1, :]
                mout_ref[lsel(d)] = m_fin[d]
            if first_layer:
                for l in range(1, cout_ref.shape[1]):
                    cout_ref[sq, l] = jnp.zeros(cout_ref.shape[2:], F32)
                    nout_ref[sq, l] = jnp.zeros(nout_ref.shape[2:], F32)
                    mout_ref[sq, l] = jnp.zeros(mout_ref.shape[2:], F32)

    def finish_group(g, carry):
        cs = [g * grp + u for u in range(grp)]
        qt = [qt_ref[c] for c in cs]
        cq = [[_dot(cst_scr[d, c], qt[u]) for d in range(2)] for u, c in enumerate(cs)]
        nq = [[_dot(nst_scr[d, c].astype(BF16), qt[u])[0:1, :] for d in range(2)] for u, c in enumerate(cs)]
        for u, c in enumerate(cs):
            coef = []
            for d in range(2):
                _, brow, _ = gates(d, c)
                den_loc = rows_scr[d, c, 0:1, :]
                mloc = rows_scr[d, c, 1:2, :]
                inter = brow + rows_scr[d, c, 3:4, :]
                m_row = jnp.maximum(mloc, inter)
                w_loc = jnp.exp(mloc - m_row)
                w_inter = jnp.exp(inter - m_row)
                den = w_loc * den_loc + w_inter * nq[u][d]
                r_den = 1.0 / jnp.maximum(jnp.abs(den), jnp.exp(-m_row))
                coef.append((w_loc * r_den, w_inter * r_den))
            ssq = jnp.zeros((1, L), F32)
            for r0 in range(0, HEAD_DIM, STRIP):
                h_s = None
                for d in range(2):
                    h_d = coef[d][0] * part_scr[d, c, r0:r0 + STRIP, :] + coef[d][1] * cq[u][d][r0:r0 + STRIP, :]
                    h_s = h_d if h_s is None else h_s + h_d
                part_scr[0, c, r0:r0 + STRIP, :] = h_s
                ssq = ssq + jnp.sum(h_s * h_s, axis=0, keepdims=True)
            r_norm = lax.rsqrt(ssq * (1.0 / HEAD_DIM) + EPS)
            for r0 in range(0, HEAD_DIM, L):
                y = part_scr[0, c, r0:r0 + L, :] * r_norm * nw_ref[r0:r0 + L, :]
                o = jax.nn.sigmoid(ot_ref[c, r0:r0 + L, :].astype(F32)) * y
                hm_ref[pl.ds(_chunk_off(c), L), r0:r0 + L] = o.T.astype(hm_ref.dtype)
        return carry

    _group_loop(ns * nc // grp, finish_group)


def _state_specs(shapes, layer, first_layer, ns):
    specs = []
    for shp in shapes:
        tail = shp[4:]
        zeros = (0,) * len(tail)
        if first_layer:
            specs.append(pl.BlockSpec((ns, shp[1], 2, None) + tail, lambda b, h, z=zeros: (b, 0, 0, h) + z))
        else:
            specs.append(pl.BlockSpec((ns, None, 2, None) + tail, lambda b, h, z=zeros: (b, layer, 0, h) + z))
    return specs


def _mlstm_call(proj_t, proj_n, gp, nw_b, init, state_bufs, *, n_seq, seq_len, tok_off, n_heads, layer, depth):
    dk = HEAD_DIM
    nc = seq_len // CHUNK
    has_init = init is not None
    ns = 1 if has_init else _seqs_per_step(n_seq, nc)
    nct = ns * nc
    boff = tok_off // (ns * seq_len)
    write_state = not has_init
    first_layer = layer == 0
    args = [proj_t, proj_t, proj_t, proj_n, gp, nw_b]
    specs = [
        pl.BlockSpec((nct, dk, CHUNK), lambda b, h: (b + boff, h, 0)),
        pl.BlockSpec((nct, dk, CHUNK), lambda b, h: (b + boff, n_heads + h, 0)),
        pl.BlockSpec((nct, dk, CHUNK), lambda b, h: (b + boff, 2 * n_heads + h, 0)),
        pl.BlockSpec((ns * seq_len, dk), lambda b, h: (b + boff, h)),
        pl.BlockSpec((None, nct, 4, CHUNK), lambda b, h: (h, b + boff, 0, 0)),
        pl.BlockSpec((dk, LANES), lambda b, h: (h, 0)),
    ]
    aliases = {}
    if has_init:
        c0, n0, m0 = init
        args += [c0, n0[:, layer][:, :, :, None, :], m0[:, layer].reshape(-1)]
        specs += [
            pl.BlockSpec((None, None, 2, None, dk, dk), lambda b, h: (b, layer, 0, h, 0, 0)),
            pl.BlockSpec((None, 2, None, 1, dk), lambda b, h: (b, 0, h, 0, 0)),
            pl.BlockSpec(memory_space=pltpu.SMEM),
        ]
    out_shape = [jax.ShapeDtypeStruct((n_seq * seq_len, n_heads * dk), BF16)]
    out_specs = [pl.BlockSpec((ns * seq_len, dk), lambda b, h: (b, h))]
    if write_state:
        shapes = [(n_seq, depth, 2, n_heads, dk, dk), (n_seq, depth, 2, n_heads, 1, dk),
                  (n_seq, depth, 2, n_heads, 1, LANES)]
        if not first_layer:
            for k_, buf in enumerate(state_bufs):
                aliases[len(args)] = 1 + k_
                args.append(buf)
                specs.append(pl.BlockSpec(memory_space=pl.ANY))
        out_shape += [jax.ShapeDtypeStruct(s, F32) for s in shapes]
        out_specs += _state_specs(shapes, layer, first_layer, ns)
    kern = functools.partial(_mlstm_kernel, nc=nc, ns=ns, has_init=has_init, first_layer=first_layer,
                             write_state=write_state)
    return pl.pallas_call(
        kern, grid=(n_seq // ns, n_heads), in_specs=specs, out_specs=out_specs, out_shape=out_shape,
        input_output_aliases=aliases,
        scratch_shapes=[pltpu.VMEM((2, dk, dk), F32), pltpu.VMEM((2, 8, dk), F32),
                        pltpu.VMEM((2, nct, dk, CHUNK), F32), pltpu.VMEM((2, nct, dk, dk), F32),
                        pltpu.VMEM((2, nct, 8, dk), F32), pltpu.VMEM((2, nct, 8, CHUNK), F32),
                        pltpu.VMEM((2, nct, dk, dk), BF16), pltpu.VMEM((2, nct, 8, dk), F32),
                        pltpu.VMEM((2, nct, 8, dk), F32)],
        compiler_params=_cparams(("parallel", "parallel"), 48), name="mlstm_mixer",
    )(*args)


def _ret_kernel(*refs, nc, ns, has_init, first_layer, write_state):
    it = iter(refs)
    dl_ref, qt_ref, vt_ref, gt_ref, k_ref, nw_ref = (next(it) for _ in range(6))
    if has_init:
        s0_ref = next(it)
    if write_state and not first_layer:
        next(it)
    hr_ref = next(it)
    if write_state:
        sout_ref = next(it)
    st_scr, part_scr, inc_scr, sst_scr = next(it), next(it), next(it), next(it)

    L = CHUNK
    grp = min(MAX_GROUP, ns * nc)
    rowi = _iota((L, L), 0)
    coli = _iota((L, L), 1)
    k_scale = HEAD_DIM ** -0.5
    h_idx = pl.program_id(1)
    n_heads = pl.num_programs(1)

    consts = []
    for d in range(2):
        lg = _log_sigmoid(jnp.full((1, 1), dl_ref[d * n_heads + h_idx], F32))
        rel = (coli - rowi if d == 0 else rowi - coli).astype(F32)
        dmat_t = jnp.where(rel >= 0, jnp.exp(jnp.maximum(rel, 0.0) * lg), 0.0) * k_scale
        pos = _iota((1, L), 1).astype(F32)
        if d == 1:
            pos = (L - 1.0) - pos
        q_decay = jnp.exp((pos + 1.0) * lg)
        k_decay = jnp.exp((L - 1.0 - pos) * lg) * k_scale
        chunk_decay = jnp.exp(float(L) * lg)
        consts.append((dmat_t, q_decay, k_decay, chunk_decay))

    def local_group(g, carry):
        cs = [g * grp + u for u in range(grp)]
        qt = [qt_ref[c] for c in cs]
        vt = [vt_ref[c] for c in cs]
        k = [k_ref[pl.ds(_chunk_off(c), L), :] for c in cs]
        qk = [_dot(k[u], qt[u]) for u in range(grp)]
        for u, c in enumerate(cs):
            vf = vt[u].astype(F32)
            for d in range(2):
                inc_scr[d, c] = _dot((vf * consts[d][2]).astype(BF16), k[u])
        for u, c in enumerate(cs):
            for d in range(2):
                part_scr[d, c] = _dot(vt[u], (qk[u] * consts[d][0]).astype(BF16))
        return carry

    _group_loop(ns * nc // grp, local_group)

    for sq in range(ns):
        c_lo = sq * nc
        for d in range(2):
            if has_init:
                st_scr[d] = s0_ref[d].T
            else:
                st_scr[d] = jnp.zeros(st_scr.shape[1:], F32)

        def body(ci, carry, c_lo=c_lo):
            for d, c in ((0, c_lo + ci), (1, c_lo + nc - 1 - ci)):
                for r0 in range(0, HEAD_DIM, STRIP):
                    st = st_scr[d, r0:r0 + STRIP, :]
                    sst_scr[d, c, r0:r0 + STRIP, :] = st.astype(BF16)
                    st_scr[d, r0:r0 + STRIP, :] = consts[d][3] * st + inc_scr[d, c, r0:r0 + STRIP, :]
            return carry

        _scan_loop(nc, body, 0)

        if write_state:
            for d in range(2):
                if first_layer:
                    sout_ref[sq, 0, d] = st_scr[d].T
                else:
                    sout_ref[sq, d] = st_scr[d].T
            if first_layer:
                for l in range(1, sout_ref.shape[1]):
                    sout_ref[sq, l] = jnp.zeros(sout_ref.shape[2:], F32)

    def finish_group(g, carry):
        cs = [g * grp + u for u in range(grp)]
        qt = [qt_ref[c] for c in cs]
        sq = [[_dot(sst_scr[d, c], qt[u]) for d in range(2)] for u, c in enumerate(cs)]
        for u, c in enumerate(cs):
            ssq = jnp.zeros((1, L), F32)
            for r0 in range(0, HEAD_DIM, STRIP):
                sl = slice(r0, r0 + STRIP)
                o_s = ((part_scr[0, c, sl, :] + part_scr[1, c, sl, :])
                       + (consts[0][1] * sq[u][0][sl, :] + consts[1][1] * sq[u][1][sl, :]))
                part_scr[0, c, sl, :] = o_s
                ssq = ssq + jnp.sum(o_s * o_s, axis=0, keepdims=True)
            r_norm = lax.rsqrt(ssq * (1.0 / HEAD_DIM) + EPS)
            for r0 in range(0, HEAD_DIM, L):
                y = part_scr[0, c, r0:r0 + L, :] * r_norm * nw_ref[r0:r0 + L, :]
                rg = gt_ref[c, r0:r0 + L, :].astype(F32)
                hr_ref[pl.ds(_chunk_off(c), L), r0:r0 + L] = (rg * jax.nn.sigmoid(rg) * y).T.astype(hr_ref.dtype)
        return carry

    _group_loop(ns * nc // grp, finish_group)


def _ret_call(proj_t, proj_n, decay_logit, nw_b, init, state_buf, *, n_seq, seq_len, tok_off, n_heads, layer,
              depth):
    dk = HEAD_DIM
    nc = seq_len // CHUNK
    has_init = init is not None
    ns = 1 if has_init else _seqs_per_step(n_seq, nc)
    nct = ns * nc
    boff = tok_off // (ns * seq_len)
    write_state = not has_init
    first_layer = layer == 0
    args = [decay_logit.reshape(-1), proj_t, proj_t, proj_t, proj_n, nw_b]
    specs = [
        pl.BlockSpec(memory_space=pltpu.SMEM),
        pl.BlockSpec((nct, dk, CHUNK), lambda b, h: (b + boff, 3 * n_heads + h, 0)),
        pl.BlockSpec((nct, dk, CHUNK), lambda b, h: (b + boff, 4 * n_heads + h, 0)),
        pl.BlockSpec((nct, dk, CHUNK), lambda b, h: (b + boff, 5 * n_heads + h, 0)),
        pl.BlockSpec((ns * seq_len, dk), lambda b, h: (b + boff, n_heads + h)),
        pl.BlockSpec((dk, LANES), lambda b, h: (h, 0)),
    ]
    aliases = {}
    if has_init:
        args.append(init)
        specs.append(pl.BlockSpec((None, None, 2, None, dk, dk), lambda b, h: (b, layer, 0, h, 0, 0)))
    out_shape = [jax.ShapeDtypeStruct((n_seq * seq_len, n_heads * dk), BF16)]
    out_specs = [pl.BlockSpec((ns * seq_len, dk), lambda b, h: (b, h))]
    if write_state:
        shapes = [(n_seq, depth, 2, n_heads, dk, dk)]
        if not first_layer:
            aliases[len(args)] = 1
            args.append(state_buf)
            specs.append(pl.BlockSpec(memory_space=pl.ANY))
        out_shape += [jax.ShapeDtypeStruct(s, F32) for s in shapes]
        out_specs += _state_specs(shapes, layer, first_layer, ns)
    kern = functools.partial(_ret_kernel, nc=nc, ns=ns, has_init=has_init, first_layer=first_layer,
                             write_state=write_state)
    return pl.pallas_call(
        kern, grid=(n_seq // ns, n_heads), in_specs=specs, out_specs=out_specs, out_shape=out_shape,
        input_output_aliases=aliases,
        scratch_shapes=[pltpu.VMEM((2, dk, dk), F32), pltpu.VMEM((2, nct, dk, CHUNK), F32),
                        pltpu.VMEM((2, nct, dk, dk), F32), pltpu.VMEM((2, nct, dk, dk), BF16)],
        compiler_params=_cparams(("parallel", "parallel"), 48), name="retention_mixer",
    )(*args)


def _merge_kernel(hmp_ref, hms_ref, hrp_ref, hrs_ref, ga_ref, gb_ref, x_ref, mod_ref, nw_ref, wa_ref, wb_ref,
                  wo_ref, wr_ref, x1_ref, h2_ref, aff_ref, wab_scr, wbb_scr, wob_scr, *, n_prompt_tiles):
    i = pl.program_id(0)

    @pl.when(i == 0)
    def _():
        wab_scr[...] = wa_ref[...].astype(BF16)
        wbb_scr[...] = wb_ref[...].astype(BF16)
        wob_scr[...] = wo_ref[...].astype(BF16)

    is_prompt = i < n_prompt_tiles
    hm = jnp.where(is_prompt, hmp_ref[...], hms_ref[...])
    hr = jnp.where(is_prompt, hrp_ref[...], hrs_ref[...])
    ya = _dot(hm, wab_scr[...])
    yb = _dot(hr, wbb_scr[...])
    merged = (jax.nn.sigmoid(ga_ref[...].astype(F32)) * ya + jax.nn.sigmoid(gb_ref[...].astype(F32)) * yb)
    y = _dot(merged.astype(BF16), wob_scr[...])
    x1 = x_ref[...] + mod_ref[2:3, :] * y
    x1_ref[...] = x1
    h2 = x1 * lax.rsqrt(jnp.mean(x1 * x1, axis=-1, keepdims=True) + EPS) * nw_ref[...]
    h2 = h2 * (1.0 + mod_ref[4:5, :]) + mod_ref[3:4, :]
    n_blk = h2.shape[1] // LANES
    for fb in range(n_blk):
        _tile_rows(h2_ref, fb, h2.shape[0], n_blk)[...] = h2[:, fb * LANES:(fb + 1) * LANES]
    logits = _dot_nt(wr_ref[...].astype(BF16), h2.astype(BF16))
    p = jnp.exp(logits - jnp.max(logits, axis=0, keepdims=True))
    aff_ref[...] = p / jnp.sum(p, axis=0, keepdims=True)


def _merge_call(hm_p, hm_s, hr_p, hr_s, proj_n, x, mod, norm_w, w_a, w_b, w_out, router_wt, layer, *,
                n_prompt, dec_seq, tm=512):
    n, d = x.shape
    ne = router_wt.shape[1]
    npt = n_prompt // tm
    grp = functools.partial(_group_of_tile, tm=tm, n_prompt=n_prompt, dec_seq=dec_seq)
    tile = pl.BlockSpec((tm, d), lambda i: (i, 0))
    tile_p = pl.BlockSpec((tm, d), lambda i: (jnp.minimum(i, npt - 1), 0))
    tile_s = pl.BlockSpec((tm, d), lambda i: (jnp.maximum(i - npt, 0), 0))
    full = pl.BlockSpec((None, d, d), lambda i: (layer, 0, 0))
    return pl.pallas_call(
        functools.partial(_merge_kernel, n_prompt_tiles=npt), grid=(n // tm,),
        in_specs=[
            tile_p, tile_s, tile_p, tile_s,
            pl.BlockSpec((tm, d), lambda i: (i, 2)),
            pl.BlockSpec((tm, d), lambda i: (i, 3)),
            tile,
            pl.BlockSpec((None, 6, d), lambda i: (grp(i), 0, 0)),
            pl.BlockSpec((1, d), lambda i: (0, 0)),
            full, full, full,
            pl.BlockSpec((None, ne, d), lambda i: (layer, 0, 0)),
        ],
        out_specs=[tile, pl.BlockSpec((tm * (d // LANES), LANES), lambda i: (i, 0)),
                   pl.BlockSpec((ne, tm), lambda i: (0, i))],
        out_shape=[jax.ShapeDtypeStruct((n, d), F32), jax.ShapeDtypeStruct((n * (d // LANES), LANES), F32),
                   jax.ShapeDtypeStruct((ne, n), F32)],
        scratch_shapes=[pltpu.VMEM((d, d), BF16)] * 3,
        compiler_params=_cparams(("arbitrary",), 56), name="merge_out_router",
    )(hm_p, hm_s, hr_p, hr_s, proj_n, proj_n, x, mod, norm_w.reshape(1, d), w_a, w_b, w_out, router_wt)


ROUTE_GROUP = 4


def _route_kernel(a_ref, at_ref, idx_ref, gate_ref, thr_scr, *, cap, n_tok):
    n_sets = a_ref.shape[0]
    nb = n_tok // LANES
    a_all = a_ref[...]

    def as_f32(bits):
        return lax.bitcast_convert_type(bits, F32)

    def count_ge(cand):
        m = jnp.where(a_all >= cand, 1.0, 0.0)
        return jnp.sum(jnp.sum(m, axis=1, keepdims=True), axis=2, keepdims=True)

    def bit_step(k, thr):
        cand = thr | lax.shift_left(jnp.int32(1), 30 - k)
        return jnp.where(count_ge(as_f32(cand)) >= cap, cand, thr)

    thr_scr[...] = lax.fori_loop(0, 31, bit_step, jnp.zeros((n_sets, 1, 1), jnp.int32))

    r128 = _iota((LANES, LANES), 0)
    c128 = _iota((LANES, LANES), 1)
    upper = jnp.where(r128 <= c128, 1.0, 0.0).astype(BF16)
    lower_t = jnp.where(r128 >= c128, 1.0, 0.0).astype(BF16)
    rb = _iota((nb, nb), 0)
    cb = _iota((nb, nb), 1)
    blk_before_rows = jnp.where(cb < rb, 1.0, 0.0).astype(BF16)
    blk_before_cols = jnp.where(rb < cb, 1.0, 0.0).astype(BF16)

    def incl_counts(mask):
        within = _dot(mask.astype(BF16), upper)
        before = _dot(blk_before_rows, within.astype(BF16))[:, LANES - 1:LANES]
        return within + before

    def incl_counts_t(mask_t):
        within = _dot(lower_t, mask_t.astype(BF16))
        before = _dot(within.astype(BF16), blk_before_cols)[LANES - 1:LANES, :]
        return within + before

    slot = _iota((1, cap), 1).astype(F32)
    blk_col = _iota((nb, 1), 0).astype(F32)
    sub_col = _iota((LANES, 1), 0).astype(F32)

    def per_group(g, carry):
        sets = [g + u * (n_sets // ROUTE_GROUP) for u in range(ROUTE_GROUP)]
        stage = []
        for s in sets:
            thr_bits = thr_scr[s]
            thr = as_f32(thr_bits)
            nxt = as_f32(thr_bits + 1)
            a = a_ref[s]
            a_t = at_ref[s]
            gt = jnp.where(a >= nxt, 1.0, 0.0)
            eq = jnp.where((a >= thr) & (a < nxt), 1.0, 0.0)
            gt_t = jnp.where(a_t >= nxt, 1.0, 0.0)
            eq_t = jnp.where((a_t >= thr) & (a_t < nxt), 1.0, 0.0)
            n_gt = jnp.sum(jnp.sum(gt, axis=1, keepdims=True), axis=0, keepdims=True)
            stage.append((a_t, gt, eq, gt_t, eq_t, cap - n_gt))
        ties = [(incl_counts(eq), incl_counts_t(eq_t)) for _, _, eq, _, eq_t, _ in stage]
        sels = []
        for (a_t, gt, eq, gt_t, eq_t, need), (c_eq, c_eq_t) in zip(stage, ties):
            sels.append((gt + eq * jnp.where(c_eq - eq < need, 1.0, 0.0),
                         gt_t + eq_t * jnp.where(c_eq_t - eq_t < need, 1.0, 0.0)))
        cnts = [(incl_counts(sel), incl_counts_t(sel_t)) for sel, sel_t in sels]
        picks = []
        for (a_t, *_), (cnt, cnt_t) in zip(stage, cnts):
            blk_end = cnt[:, LANES - 1:LANES]
            blk_of_slot = jnp.sum(jnp.where(blk_end <= slot, 1.0, 0.0), axis=0, keepdims=True)
            onehot_blk = jnp.where(blk_col == blk_of_slot, 1.0, 0.0).astype(BF16)
            cnt_rows = _dot3(cnt_t, onehot_blk)
            a_rows = _dot3(a_t, onehot_blk)
            picks.append((blk_of_slot, cnt_rows, a_rows))
        for s, (blk_of_slot, cnt_rows, a_rows) in zip(sets, picks):
            sub_of_slot = jnp.sum(jnp.where(cnt_rows <= slot, 1.0, 0.0), axis=0, keepdims=True)
            gate = jnp.sum(jnp.where(sub_col == sub_of_slot, a_rows, 0.0), axis=0, keepdims=True)
            idx_ref[s] = (blk_of_slot * LANES + sub_of_slot).astype(jnp.int32)
            gate_ref[s] = gate
        return carry

    lax.fori_loop(0, n_sets // ROUTE_GROUP, per_group, 0)


def _route_call(aff_t, n_pass, cap):
    ne, n = aff_t.shape
    n_tok = n // n_pass
    nb = n_tok // LANES
    a4 = aff_t.reshape(ne, n_pass, nb, LANES).transpose(1, 0, 2, 3).reshape(n_pass * ne, nb, LANES)
    a4_t = a4.transpose(0, 2, 1)
    n_sets = n_pass * ne
    return pl.pallas_call(
        functools.partial(_route_kernel, cap=cap, n_tok=n_tok),
        out_shape=[jax.ShapeDtypeStruct((n_sets, 1, cap), jnp.int32),
                   jax.ShapeDtypeStruct((n_sets, 1, cap), F32)],
        scratch_shapes=[pltpu.VMEM((n_sets, 1, 1), jnp.int32)],
        compiler_params=pltpu.CompilerParams(vmem_limit_bytes=32 * MIB), name="expert_choice_route",
    )(a4, a4_t)


ROW_LOOP_UNROLL = 8


def _moe_kernel(idx_ref, gate_ref, h_hbm, w1_ref, w3_ref, w2_ref, out_hbm,
                xe_scr, xb_scr, ye_a, ye_b, acc_scr, gsem, osem, *, rows, rows_pad):
    e = pl.program_id(0)
    f = pl.program_id(1)
    n_e = pl.num_programs(0)
    n_f = pl.num_programs(1)
    chunk = rows_pad // n_f

    n_blk = xb_scr.shape[1] // LANES

    def tile_of(t):
        return pl.ds(pl.multiple_of(t * n_blk, n_blk), n_blk)

    def gather_row(slot, s):
        tok = idx_ref[slot * rows_pad + s]
        pltpu.make_async_copy(h_hbm.at[tile_of(tok), :], xe_scr.at[tile_of(s), :], gsem).start()

    def scatter_row(slot, s, ye_ref):
        tok = idx_ref[slot * rows_pad + s]
        g = gate_ref[slot * rows_pad + s]
        acc_scr[pl.ds(tok, 1), :] += ye_ref[pl.ds(s, 1), :] * g

    def wait_gather():
        pltpu.make_async_copy(h_hbm.at[pl.ds(0, rows_pad * n_blk), :], xe_scr, gsem).wait()

    @pl.when((e == 0) & (f == 0))
    def _():
        acc_scr[...] = jnp.zeros_like(acc_scr)
        ye_a[...] = jnp.zeros_like(ye_a)
        ye_b[...] = jnp.zeros_like(ye_b)

        def issue(s, carry):
            gather_row(1, s)
            return carry
        lax.fori_loop(0, rows_pad, issue, 0, unroll=ROW_LOOP_UNROLL)

    def step(ye_cur, ye_prev):
        @pl.when(f == 0)
        def _():
            wait_gather()
            for fb in range(n_blk):
                xb_scr[:, fb * LANES:(fb + 1) * LANES] = _tile_rows(xe_scr, fb, rows, n_blk)[...].astype(BF16)
            ye_cur[0:rows, :] = jnp.zeros((rows, ye_cur.shape[1]), F32)

        xb = xb_scr[...]
        h1 = _dot(xb, w1_ref[...].astype(BF16))
        h3 = _dot(xb, w3_ref[...].astype(BF16))
        he = (h1 * jax.nn.sigmoid(h1) * h3).astype(BF16)
        ye_cur[0:rows, :] += _dot(he, w2_ref[...].astype(BF16))
        base = f * chunk
        for r in range(chunk):
            gather_row(e + 2, base + r)
        for r in range(chunk):
            scatter_row(e, base + r, ye_prev)

        @pl.when((e == n_e - 1) & (f == n_f - 1))
        def _():
            def scatter(s, carry):
                scatter_row(n_e, s, ye_cur)
                return carry
            lax.fori_loop(0, rows, scatter, 0, unroll=ROW_LOOP_UNROLL)
            wait_gather()
            cp = pltpu.make_async_copy(acc_scr, out_hbm, osem)
            cp.start()
            cp.wait()

    @pl.when(e % 2 == 0)
    def _():
        step(ye_a, ye_b)

    @pl.when(e % 2 == 1)
    def _():
        step(ye_b, ye_a)


def _moe_call(idx, gate, h2, w1, w3, w2, layer, tf=256):
    d = w1.shape[2]
    n_blk = d // LANES
    n = h2.shape[0] // n_blk
    ne, rows = idx.shape
    dff = w1.shape[3]
    nf = dff // tf
    rows_pad = -(-rows // (8 * nf)) * 8 * nf
    pad = ((1, 1), (0, rows_pad - rows))
    idx_all = jnp.pad(idx, pad).reshape(-1)
    gate_all = jnp.pad(gate, pad).reshape(-1)
    grid_spec = pltpu.PrefetchScalarGridSpec(
        num_scalar_prefetch=1,
        grid=(ne, nf),
        in_specs=[
            pl.BlockSpec(memory_space=pltpu.SMEM),
            pl.BlockSpec(memory_space=pl.ANY),
            pl.BlockSpec((None, None, d, tf), lambda e, f, idx: (layer, e, 0, f)),
            pl.BlockSpec((None, None, d, tf), lambda e, f, idx: (layer, e, 0, f)),
            pl.BlockSpec((None, None, tf, d), lambda e, f, idx: (layer, e, f, 0)),
        ],
        out_specs=pl.BlockSpec(memory_space=pl.ANY),
        scratch_shapes=[
            pltpu.VMEM((rows_pad * n_blk, LANES), F32), pltpu.VMEM((rows, d), BF16),
            pltpu.VMEM((rows_pad, d), F32), pltpu.VMEM((rows_pad, d), F32),
            pltpu.VMEM((n, d), F32), pltpu.SemaphoreType.DMA, pltpu.SemaphoreType.DMA,
        ],
    )
    return pl.pallas_call(
        functools.partial(_moe_kernel, rows=rows, rows_pad=rows_pad), grid_spec=grid_spec,
        out_shape=jax.ShapeDtypeStruct((n, d), F32),
        compiler_params=_cparams(("arbitrary", "arbitrary"), 62), name="expert_ffn",
    )(idx_all, gate_all, h2, w1, w3, w2)


def _rope_tables(t):
    rows = t // GRID_W
    row = jnp.repeat(jnp.arange(rows, dtype=F32), GRID_W)
    colp = jnp.tile(jnp.arange(GRID_W, dtype=F32), rows)
    n_freq = HEAD_DIM // 4
    inv = ROPE_BASE ** (-jnp.arange(n_freq, dtype=F32) / n_freq)
    ang = jnp.concatenate([row[:, None] * inv, colp[:, None] * inv], axis=-1)
    return jnp.cos(ang), jnp.sin(ang)


def kernel(x_prompt, x_sample, c, state_mlstm_C, state_mlstm_n, state_mlstm_m, state_ret_S, c_ctx, w_mod, b_mod,
           norm1_w, norm2_w, w_in, mlstm_if_b, mlstm_norm_w, ret_decay_logit, ret_norm_w, w_branch_a, w_branch_b,
           w_out, router_w, ffn_w1, ffn_w3, ffn_w2, final_norm_w):
    bp, seq, d = x_prompt.shape
    db, dec_seq, _ = x_sample.shape
    depth = w_mod.shape[0]
    hm = mlstm_if_b.shape[-1]
    hr = ret_decay_logit.shape[-1]
    ne = router_w.shape[-1]
    n_prompt = bp * seq
    n_sample = db * dec_seq
    assert n_prompt == n_sample and hm * HEAD_DIM == d and hr * HEAD_DIM == d and hm == hr
    cap = EC_FACTOR * n_prompt // ne
    tm = 512
    tp = 2048
    geo = dict(n_prompt=n_prompt, dec_seq=dec_seq)

    cond8 = jnp.concatenate([c_ctx[None, :], c, jnp.zeros((8 - 1 - db, d), F32)], axis=0)
    mod = _mod_call(cond8, w_mod, b_mod)[:, :1 + db].reshape(depth, 1 + db, 6, d)

    n_m = 4 * d
    g0 = n_m + 4 * hm

    w_t = jnp.swapaxes(w_in, 1, 2)
    mq, mk, mv, mo = (k * d for k in range(4))
    rq, rk, rv, rg, ga, gb = (g0 + k * d for k in range(6))
    wg_t = w_t[:, n_m:g0]
    gate_bias = jnp.transpose(mlstm_if_b, (0, 2, 1, 3)).reshape(depth, 4 * hm, 1)
    router_wt = jnp.swapaxes(router_w, 1, 2)
    nw_m = jnp.broadcast_to(mlstm_norm_w[:, :, None], (depth, d, LANES))
    nw_r = jnp.broadcast_to(ret_norm_w[:, :, None], (depth, d, LANES))

    cos, sin = _rope_tables(dec_seq)
    cos_t = cos.T.reshape(HEAD_DIM // 2, dec_seq // tp, tp).transpose(1, 0, 2)
    sin_t = sin.T.reshape(HEAD_DIM // 2, dec_seq // tp, tp).transpose(1, 0, 2)

    x, h = _resid_norm_call(x_prompt.reshape(n_prompt, d), None, None, mod[0], norm1_w[0], gate_row=0,
                            mod_rows=(0, 1), h_dtype=BF16, write_x=True, x_tail=x_sample.reshape(n_sample, d),
                            **geo)
    m_bufs, s_buf = None, None
    y_prompt = y_sample = None
    for l in range(depth):
        proj_t = _proj_t_call(h, w_t, l, (mq, mv, mo, rq, rv, rg), (cos_t, sin_t), 3, tm=tp, **geo)
        proj_n = _proj_call(h, w_t, l, (mk, rk, ga, gb), (cos, sin), 1, tm=tp, **geo)
        gp = _gate_call(h, wg_t[l], gate_bias[l], hm, tm=tm)

        mix = dict(n_heads=hm, layer=l, depth=depth)
        hm_p, *m_bufs = _mlstm_call(proj_t, proj_n, gp, nw_m[l], None, m_bufs, n_seq=bp, seq_len=seq, tok_off=0,
                                    **mix)
        (hm_s,) = _mlstm_call(proj_t, proj_n, gp, nw_m[l], (state_mlstm_C, state_mlstm_n, state_mlstm_m), None,
                              n_seq=db, seq_len=dec_seq, tok_off=n_prompt, **mix)
        hr_p, s_buf = _ret_call(proj_t, proj_n, ret_decay_logit[l], nw_r[l], None, s_buf, n_seq=bp, seq_len=seq,
                                tok_off=0, **mix)
        (hr_s,) = _ret_call(proj_t, proj_n, ret_decay_logit[l], nw_r[l], state_ret_S, None, n_seq=db,
                            seq_len=dec_seq, tok_off=n_prompt, **mix)

        x1, h2, aff_t = _merge_call(hm_p, hm_s, hr_p, hr_s, proj_n, x, mod[l], norm2_w[l], w_branch_a, w_branch_b,
                                    w_out, router_wt, l, tm=tm, **geo)
        idx, gate = _route_call(aff_t, 2, cap)
        idx = idx.reshape(2, ne, cap) + (jnp.arange(2, dtype=jnp.int32) * n_prompt)[:, None, None]
        idx = jnp.swapaxes(idx, 0, 1).reshape(ne, 2 * cap)
        gate = jnp.swapaxes(gate.reshape(2, ne, cap), 0, 1).reshape(ne, 2 * cap)
        moe = _moe_call(idx, gate, h2, ffn_w1, ffn_w3, ffn_w2, l)
        if l + 1 < depth:
            x, h = _resid_norm_call(x1, moe, mod[l], mod[l + 1], norm1_w[l + 1], gate_row=5, mod_rows=(0, 1),
                                    h_dtype=BF16, write_x=True, **geo)
        else:
            fin = dict(gate_row=5, mod_rows=None, h_dtype=F32, write_x=False, **geo)
            _, y_prompt = _resid_norm_call(x1, moe, mod[l], None, final_norm_w, row_off=0, n_rows=n_prompt, **fin)
            _, y_sample = _resid_norm_call(x1, moe, mod[l], None, final_norm_w, row_off=n_prompt, n_rows=n_sample,
                                           **fin)

    c_buf, n_buf, m_buf = m_bufs
    return (y_prompt.reshape(bp, seq, d), y_sample.reshape(db, dec_seq, d), c_buf, n_buf[:, :, :, :, 0, :],
            m_buf[:, :, :, :, 0, 0], s_buf)
```

```python
import functools

import numpy as np
import jax
import jax.numpy as jnp
from jax import lax
from jax.experimental import pallas as pl
from jax.experimental.pallas import tpu as pltpu

F32 = jnp.float32
BF16 = jnp.bfloat16

GRID_W = 64
CHUNK = 128
LANES = 128
HEAD_DIM = 256
N_EXPERTS = 16
EC_FACTOR = 2
ROPE_BASE = 10000.0
EPS = 1e-6
V7X_VMEM_BYTES = 64 * 1024 * 1024
MIB = 1024 * 1024


def _cparams(semantics, vmem_mib):
    assert vmem_mib * MIB < V7X_VMEM_BYTES
    return pltpu.CompilerParams(dimension_semantics=semantics, vmem_limit_bytes=vmem_mib * MIB)


def _dot(a, b):
    return jnp.dot(a, b, preferred_element_type=F32)


def _dot_nt(a, b):
    return lax.dot_general(a, b, (((1,), (1,)), ((), ())), preferred_element_type=F32)


def _log_sigmoid(x):
    return -(jnp.maximum(-x, 0.0) + jnp.log1p(jnp.exp(-jnp.abs(x))))


def _split3(x):
    hi = x.astype(BF16)
    r1 = x - hi.astype(F32)
    mid = r1.astype(BF16)
    lo = (r1 - mid.astype(F32)).astype(BF16)
    return hi, mid, lo


def _dot3(x, m):
    hi, mid, lo = _split3(x)
    return _dot(hi, m) + _dot(mid, m) + _dot(lo, m)


def _iota(shape, dim):
    return lax.broadcasted_iota(jnp.int32, shape, dim)


def _tile_rows(ref, fb, n_tok, n_blk):
    return ref.at[pl.ds(fb, n_tok, stride=n_blk), :]


def _mod_kernel(cond_ref, w_ref, b_ref, out_ref):
    c = cond_ref[...]
    s = (c * jax.nn.sigmoid(c)).astype(BF16)
    out_ref[...] = _dot(s, w_ref[...].astype(BF16)) + b_ref[...]


def _mod_call(cond8, w_mod, b_mod):
    depth, d, w6 = w_mod.shape
    tn = 1536
    return pl.pallas_call(
        _mod_kernel,
        grid=(depth, w6 // tn),
        in_specs=[
            pl.BlockSpec((8, d), lambda l, j: (0, 0)),
            pl.BlockSpec((None, d, tn), lambda l, j: (l, 0, j)),
            pl.BlockSpec((None, 1, tn), lambda l, j: (l, 0, j)),
        ],
        out_specs=pl.BlockSpec((None, 8, tn), lambda l, j: (l, 0, j)),
        out_shape=jax.ShapeDtypeStruct((depth, 8, w6), F32),
        compiler_params=_cparams(("parallel", "parallel"), 32),
        name="adaln_mod",
    )(cond8, w_mod, b_mod.reshape(depth, 1, w6))


def _group_of_tile(i, tm, n_prompt, dec_seq):
    return jnp.maximum(i * tm - n_prompt + dec_seq, 0) // dec_seq


def _resid_norm_kernel(*refs, has_delta, has_mod, gate_row, mod_rows, write_x, n_first_tiles):
    it = iter(refs)
    x_ref = next(it)
    x2_ref = next(it) if n_first_tiles else None
    delta_ref = next(it) if has_delta else None
    mod_ref = next(it) if (has_delta or has_mod) else None
    nw_ref = next(it)
    xo_ref = next(it) if write_x else None
    h_ref = next(it)
    x = x_ref[...]
    if n_first_tiles:
        x = jnp.where(pl.program_id(0) < n_first_tiles, x, x2_ref[...])
    if has_delta:
        x = x + mod_ref[gate_row:gate_row + 1, :] * delta_ref[...]
    if write_x:
        xo_ref[...] = x
    y = x * lax.rsqrt(jnp.mean(x * x, axis=-1, keepdims=True) + EPS) * nw_ref[...]
    if has_mod:
        sh_row, sc_row = mod_rows
        y = y * (1.0 + mod_ref[sc_row:sc_row + 1, :]) + mod_ref[sh_row:sh_row + 1, :]
    h_ref[...] = y.astype(h_ref.dtype)


def _resid_norm_call(x, delta, mod_gate, mod_norm, norm_w, *, gate_row, mod_rows, h_dtype, write_x,
                     n_prompt, dec_seq, tm=512, row_off=0, n_rows=None, x_tail=None):
    n, d = x.shape
    has_delta = delta is not None
    has_mod = mod_norm is not None
    grp = functools.partial(_group_of_tile, tm=tm, n_prompt=n_prompt, dec_seq=dec_seq)
    if x_tail is None:
        n_rows = n if n_rows is None else n_rows
        toff = row_off // tm
        nft = 0
        tile_in = pl.BlockSpec((tm, d), lambda i: (i + toff, 0))
        args, specs = [x], [tile_in]
    else:
        assert row_off == 0 and n_rows is None and not has_delta
        n_rows, toff, nft = n + x_tail.shape[0], 0, n // tm
        args = [x, x_tail]
        specs = [pl.BlockSpec((tm, d), lambda i: (jnp.minimum(i, nft - 1), 0)),
                 pl.BlockSpec((tm, d), lambda i: (jnp.maximum(i - nft, 0), 0))]
    tile_out = pl.BlockSpec((tm, d), lambda i: (i, 0))
    if has_delta:
        args.append(delta)
        specs.append(tile_in)
    if has_delta or has_mod:
        mg = mod_gate if has_delta else mod_norm
        mn = mod_norm if has_mod else mod_gate
        args.append(jnp.concatenate([mg, mn], axis=1))
        specs.append(pl.BlockSpec((None, 12, d), lambda i: (grp(i + toff), 0, 0)))
    args.append(norm_w.reshape(1, d))
    specs.append(pl.BlockSpec((1, d), lambda i: (0, 0)))
    out_shape, out_specs = [], []
    if write_x:
        out_shape.append(jax.ShapeDtypeStruct((n_rows, d), F32))
        out_specs.append(tile_out)
    out_shape.append(jax.ShapeDtypeStruct((n_rows, d), h_dtype))
    out_specs.append(tile_out)
    kern = functools.partial(
        _resid_norm_kernel, has_delta=has_delta, has_mod=has_mod, gate_row=gate_row,
        mod_rows=None if mod_rows is None else (6 + mod_rows[0], 6 + mod_rows[1]), write_x=write_x,
        n_first_tiles=nft)
    outs = pl.pallas_call(
        kern, grid=(n_rows // tm,), in_specs=specs, out_specs=out_specs, out_shape=out_shape,
        compiler_params=_cparams(("parallel",), 32), name="resid_norm",
    )(*args)
    return outs if write_x else (None, outs[0])


def _rope_pair(x1, x2, cos, sin):
    return x1 * cos - x2 * sin, x1 * sin + x2 * cos


def _weight_rows_spec(layer, row_starts, tn, d):
    def index_map(j, i):
        start = sum(jnp.where(j == k, s, 0) for k, s in enumerate(row_starts))
        return layer, pl.multiple_of(start, 8), 0
    assert all(s % 8 == 0 for s in row_starts)
    return pl.BlockSpec((pl.Element(1), pl.Element(tn), pl.Element(d)), index_map)


def _proj_kernel(h_ref, w_ref, cos_ref, sin_ref, out_ref, wb_scr, *, tm, rope_tile, n_prompt, dec_seq):
    j = pl.program_id(0)
    i = pl.program_id(1)

    @pl.when(i == 0)
    def _():
        wb_scr[...] = w_ref[0].astype(BF16)

    row0 = pl.multiple_of(i * tm, tm)
    use_rope = (j == rope_tile) & (row0 >= n_prompt)
    half = HEAD_DIM // 2

    def heads(rope):
        h = h_ref[...]
        if rope:
            pos0 = pl.multiple_of((row0 - n_prompt) % dec_seq, tm)
            cos = cos_ref[pl.ds(pos0, tm), :]
            sin = sin_ref[pl.ds(pos0, tm), :]
        for hh in range(wb_scr.shape[0] // HEAD_DIM):
            c0 = hh * HEAD_DIM
            acc = _dot_nt(h, wb_scr[c0:c0 + HEAD_DIM, :])
            if rope:
                y1, y2 = _rope_pair(acc[:, :half], acc[:, half:], cos, sin)
                out_ref[:, c0:c0 + half] = y1.astype(out_ref.dtype)
                out_ref[:, c0 + half:c0 + HEAD_DIM] = y2.astype(out_ref.dtype)
            else:
                out_ref[:, c0:c0 + HEAD_DIM] = acc.astype(out_ref.dtype)

    pl.when(use_rope)(functools.partial(heads, True))
    pl.when(jnp.logical_not(use_rope))(functools.partial(heads, False))


def _proj_call(h, w_t, layer, row_starts, rope, rope_tile, *, n_prompt, dec_seq, tm=512, tn=1024):
    n, d = h.shape
    nj = len(row_starts)
    kern = functools.partial(_proj_kernel, tm=tm, rope_tile=rope_tile, n_prompt=n_prompt, dec_seq=dec_seq)
    return pl.pallas_call(
        kern, grid=(nj, n // tm),
        in_specs=[
            pl.BlockSpec((tm, d), lambda j, i: (i, 0)),
            _weight_rows_spec(layer, row_starts, tn, d),
            pl.BlockSpec(rope[0].shape, lambda j, i: (0, 0)),
            pl.BlockSpec(rope[1].shape, lambda j, i: (0, 0)),
        ],
        out_specs=pl.BlockSpec((tm, tn), lambda j, i: (i, j)),
        out_shape=jax.ShapeDtypeStruct((n, nj * tn), BF16),
        scratch_shapes=[pltpu.VMEM((tn, d), BF16)],
        compiler_params=_cparams(("parallel", "arbitrary"), 56), name="in_proj",
    )(h, w_t, *rope)


def _proj_t_kernel(h_ref, w_ref, cos_ref, sin_ref, out_ref, wb_scr, *, tm, rope_tile, n_prompt, dec_seq):
    j = pl.program_id(0)
    i = pl.program_id(1)

    @pl.when(i == 0)
    def _():
        wb_scr[...] = w_ref[0].astype(BF16)

    row0 = pl.multiple_of(i * tm, tm)
    use_rope = (j == rope_tile) & (row0 >= n_prompt)
    half = HEAD_DIM // 2

    def heads(rope):
        h = h_ref[...]
        if rope:
            blk = ((row0 - n_prompt) % dec_seq) // tm
            cos = cos_ref[blk]
            sin = sin_ref[blk]
        for hh in range(wb_scr.shape[0] // HEAD_DIM):
            r0 = hh * HEAD_DIM
            acc = _dot_nt(wb_scr[r0:r0 + HEAD_DIM, :], h)
            if rope:
                y1, y2 = _rope_pair(acc[:half, :], acc[half:, :], cos, sin)
                y = jnp.concatenate([y1, y2], axis=0).astype(out_ref.dtype)
            else:
                y = acc.astype(out_ref.dtype)
            for s in range(tm // CHUNK):
                out_ref[s, r0:r0 + HEAD_DIM, :] = y[:, s * CHUNK:(s + 1) * CHUNK]

    pl.when(use_rope)(functools.partial(heads, True))
    pl.when(jnp.logical_not(use_rope))(functools.partial(heads, False))


def _proj_t_call(h, w_t, layer, row_starts, rope_t, rope_tile, *, n_prompt, dec_seq, tm=512, tn=1024):
    n, d = h.shape
    nj = len(row_starts)
    kern = functools.partial(_proj_t_kernel, tm=tm, rope_tile=rope_tile, n_prompt=n_prompt, dec_seq=dec_seq)
    return pl.pallas_call(
        kern, grid=(nj, n // tm),
        in_specs=[
            pl.BlockSpec((tm, d), lambda j, i: (i, 0)),
            _weight_rows_spec(layer, row_starts, tn, d),
            pl.BlockSpec(rope_t[0].shape, lambda j, i: (0, 0, 0)),
            pl.BlockSpec(rope_t[1].shape, lambda j, i: (0, 0, 0)),
        ],
        out_specs=pl.BlockSpec((tm // CHUNK, tn, CHUNK), lambda j, i: (i, j, 0)),
        out_shape=jax.ShapeDtypeStruct((n // CHUNK, nj * tn, CHUNK), BF16),
        scratch_shapes=[pltpu.VMEM((tn, d), BF16)],
        compiler_params=_cparams(("parallel", "arbitrary"), 56), name="in_proj_t",
    )(h, w_t, *rope_t)


def _gate_kernel(h_ref, wg_ref, bias_ref, out_ref, *, tm, n_heads):
    g = _dot_nt(wg_ref[...].astype(BF16), h_ref[...]) + bias_ref[...]
    nd = 2 * n_heads
    ig = g[0:nd, :]
    lf = _log_sigmoid(g[nd:2 * nd, :])
    r = _iota((CHUNK, CHUNK), 0)
    c = _iota((CHUNK, CHUNK), 1)
    upper = jnp.where(r <= c, 1.0, 0.0).astype(BF16)
    lower = jnp.where(r >= c, 1.0, 0.0).astype(BF16)
    is_fwd = _iota((nd, CHUNK), 0) < n_heads
    for s in range(tm // CHUNK):
        sl = slice(s * CHUNK, (s + 1) * CHUNK)
        lf_c = lf[:, sl]
        b = jnp.where(is_fwd, _dot3(lf_c, upper), _dot3(lf_c, lower))
        ig_c = ig[:, sl]
        for hh in range(n_heads):
            out_ref[hh, s, 0:1, :] = ig_c[hh:hh + 1, :]
            out_ref[hh, s, 1:2, :] = ig_c[n_heads + hh:n_heads + hh + 1, :]
            out_ref[hh, s, 2:3, :] = b[hh:hh + 1, :]
            out_ref[hh, s, 3:4, :] = b[n_heads + hh:n_heads + hh + 1, :]


def _gate_call(h, wg_t, bias_col, n_heads, tm=512):
    n, d = h.shape
    ng = 4 * n_heads
    return pl.pallas_call(
        functools.partial(_gate_kernel, tm=tm, n_heads=n_heads),
        grid=(n // tm,),
        in_specs=[
            pl.BlockSpec((tm, d), lambda i: (i, 0)),
            pl.BlockSpec((ng, d), lambda i: (0, 0)),
            pl.BlockSpec((ng, 1), lambda i: (0, 0)),
        ],
        out_specs=pl.BlockSpec((n_heads, tm // CHUNK, 4, CHUNK), lambda i: (0, i, 0, 0)),
        out_shape=jax.ShapeDtypeStruct((n_heads, n // CHUNK, 4, CHUNK), F32),
        compiler_params=_cparams(("parallel",), 32), name="mlstm_gates",
    )(h, wg_t, bias_col)


def _scan_loop(nc, body, init):
    if nc <= 2:
        carry = init
        for ci in range(nc):
            carry = body(ci, carry)
        return carry
    return lax.fori_loop(0, nc, body, init)


MAX_GROUP = 16
BF16_ROWS = 16
STRIP = 32


def _seqs_per_step(n_seq, nc):
    ns = max(1, MAX_GROUP // nc)
    while n_seq % ns:
        ns -= 1
    return ns


def _group_loop(n_groups, body):
    if n_groups == 1:
        body(0, 0)
    else:
        lax.fori_loop(0, n_groups, body, 0)


def _chunk_off(c):
    return c * CHUNK if isinstance(c, int) else pl.multiple_of(c * CHUNK, CHUNK)


def _mlstm_kernel(*refs, nc, ns, has_init, first_layer, write_state):
    it = iter(refs)
    qt_ref, vt_ref, ot_ref, k_ref, gp_ref, nw_ref = (next(it) for _ in range(6))
    if has_init:
        c0_ref, n0_ref, m0_ref = next(it), next(it), next(it)
    if write_state and not first_layer:
        next(it), next(it), next(it)
    hm_ref = next(it)
    if write_state:
        cout_ref, nout_ref, mout_ref = next(it), next(it), next(it)
    ct_scr, n_scr, part_scr, inc_scr, ninc_scr, rows_scr, cst_scr, nst_scr, coef_scr = (next(it) for _ in range(9))

    L = CHUNK
    grp = min(MAX_GROUP, ns * nc)
    row = _iota((L, L), 0)
    col = _iota((L, L), 1)
    k_scale = HEAD_DIM ** -0.5
    b_idx = pl.program_id(0)
    h_idx = pl.program_id(1)
    n_heads = pl.num_programs(1)

    def gates(d, c):
        g = gp_ref[c]
        ig = g[d:d + 1, :]
        brow = g[2 + d:3 + d, :]
        blast = brow[:, L - 1:L] if d == 0 else brow[:, 0:1]
        return ig, brow, blast

    def local_group(g, carry):
        cs = [g * grp + u for u in range(grp)]
        pairs = [(u, d) for u in range(grp) for d in range(2)]
        qt = [qt_ref[c] for c in cs]
        vt = [vt_ref[c] for c in cs]
        k = [k_ref[pl.ds(_chunk_off(c), L), :] for c in cs]
        qk = [_dot(k[u], qt[u]) for u in range(grp)]
        vf = [v.astype(F32) for v in vt]
        gts = {(u, d): gates(d, cs[u]) for u, d in pairs}
        for u, c in enumerate(cs):
            lhs = []
            for d in range(2):
                ig, brow, blast = gts[u, d]
                wj = blast - brow + ig
                mloc2 = jnp.max(wj, axis=1, keepdims=True)
                e = jnp.exp(wj - mloc2) * k_scale
                rows_scr[d, c, 2:3, :] = jnp.broadcast_to(mloc2, (1, L))
                rows_scr[d, c, 4:5, :] = jnp.broadcast_to(blast, (1, L))
                lhs.append(((vf[u] * e).astype(BF16), _split3(jnp.broadcast_to(e, (BF16_ROWS, L)))))
            res = _dot(jnp.concatenate([lhs[0][0], lhs[1][0], *lhs[0][1], *lhs[1][1]], axis=0), k[u])
            for d in range(2):
                inc_scr[d, c] = res[d * HEAD_DIM:(d + 1) * HEAD_DIM, :]
                r0 = 2 * HEAD_DIM + 3 * BF16_ROWS * d
                ninc_scr[d, c] = (res[r0:r0 + 8, :] + res[r0 + BF16_ROWS:r0 + BF16_ROWS + 8, :]
                                  + res[r0 + 2 * BF16_ROWS:r0 + 2 * BF16_ROWS + 8, :])
        for u, d in pairs:
            c = cs[u]
            ig, brow, blast = gts[u, d]
            key_term = jnp.broadcast_to(ig - brow, (L, L)).T
            causal = (row <= col) if d == 0 else (row >= col)
            dm = jnp.where(causal, key_term + brow, -jnp.inf)
            mloc = jnp.max(dm, axis=0, keepdims=True)
            s = qk[u] * (jnp.exp(dm - mloc) * k_scale)
            part_scr[d, c] = _dot(vt[u], s.astype(BF16))
            rows_scr[d, c, 0:1, :] = jnp.sum(s, axis=0, keepdims=True)
            rows_scr[d, c, 1:2, :] = mloc
        return carry

    _group_loop(ns * nc // grp, local_group)

    def m_step(d, c, m):
        blast = rows_scr[d, c, 4:5, :]
        mloc2 = rows_scr[d, c, 2:3, :]
        rows_scr[d, c, 3:4, :] = m
        m_new = jnp.maximum(blast + m, mloc2)
        decay = jnp.exp(blast + m - m_new)
        w_inc = jnp.exp(mloc2 - m_new)
        coef_scr[d, c, 0:1, :] = jnp.concatenate([decay] * (HEAD_DIM // L), axis=1)
        coef_scr[d, c, 1:2, :] = jnp.concatenate([w_inc] * (HEAD_DIM // L), axis=1)
        return m_new

    for sq in range(ns):
        c_lo = sq * nc
        m_init = []
        for d in range(2):
            if has_init:
                ct_scr[d] = c0_ref[d].T
                n_scr[d] = jnp.broadcast_to(n0_ref[d], n_scr.shape[1:])
                m_init.append(jnp.full((1, L), m0_ref[(b_idx * 2 + d) * n_heads + h_idx], F32))
            else:
                ct_scr[d] = jnp.zeros(ct_scr.shape[1:], F32)
                n_scr[d] = jnp.zeros(n_scr.shape[1:], F32)
                m_init.append(jnp.zeros((1, L), F32))

        def m_body(ci, ms, c_lo=c_lo):
            return m_step(0, c_lo + ci, ms[0]), m_step(1, c_lo + nc - 1 - ci, ms[1])

        m_fin = _scan_loop(nc, m_body, tuple(m_init))

        def body(ci, carry, c_lo=c_lo):
            for d, c in ((0, c_lo + ci), (1, c_lo + nc - 1 - ci)):
                decay = coef_scr[d, c, 0:1, :]
                w_inc = coef_scr[d, c, 1:2, :]
                n8 = n_scr[d]
                nst_scr[d, c] = n8
                for r0 in range(0, HEAD_DIM, STRIP):
                    ct = ct_scr[d, r0:r0 + STRIP, :]
                    cst_scr[d, c, r0:r0 + STRIP, :] = ct.astype(BF16)
                    ct_scr[d, r0:r0 + STRIP, :] = decay * ct + w_inc * inc_scr[d, c, r0:r0 + STRIP, :]
                n_scr[d] = decay * n8 + w_inc * ninc_scr[d, c]
            return carry

        _scan_loop(nc, body, 0)

        if write_state:
            lsel = (lambda d, sq=sq: (sq, 0, d)) if first_layer else (lambda d, sq=sq: (sq, d))
            for d in range(2):
                cout_ref[lsel(d)] = ct_scr[d].T
                nout_ref[lsel(d)] = n_scr[d][0:1, :]
                mout_ref[lsel(d)] = m_fin[d]
            if first_layer:
                for l in range(1, cout_ref.shape[1]):
                    cout_ref[sq, l] = jnp.zeros(cout_ref.shape[2:], F32)
                    nout_ref[sq, l] = jnp.zeros(nout_ref.shape[2:], F32)
                    mout_ref[sq, l] = jnp.zeros(mout_ref.shape[2:], F32)

    def finish_group(g, carry):
        cs = [g * grp + u for u in range(grp)]
        qt = [qt_ref[c] for c in cs]
        cq, nq = [], []
        for u, c in enumerate(cs):
            n16 = [jnp.concatenate([nst_scr[d, c]] * (BF16_ROWS // 8), axis=0).astype(BF16) for d in range(2)]
            res = _dot(jnp.concatenate([cst_scr[0, c], cst_scr[1, c], n16[0], n16[1]], axis=0), qt[u])
            cq.append([res[d * HEAD_DIM:(d + 1) * HEAD_DIM, :] for d in range(2)])
            nq.append([res[2 * HEAD_DIM + d * BF16_ROWS:2 * HEAD_DIM + d * BF16_ROWS + 1, :] for d in range(2)])
        for u, c in enumerate(cs):
            coef = []
            for d in range(2):
                _, brow, _ = gates(d, c)
                den_loc = rows_scr[d, c, 0:1, :]
                mloc = rows_scr[d, c, 1:2, :]
                inter = brow + rows_scr[d, c, 3:4, :]
                m_row = jnp.maximum(mloc, inter)
                w_loc = jnp.exp(mloc - m_row)
                w_inter = jnp.exp(inter - m_row)
                den = w_loc * den_loc + w_inter * nq[u][d]
                r_den = 1.0 / jnp.maximum(jnp.abs(den), jnp.exp(-m_row))
                coef.append((w_loc * r_den, w_inter * r_den))
            ssq = jnp.zeros((1, L), F32)
            for r0 in range(0, HEAD_DIM, STRIP):
                h_s = None
                for d in range(2):
                    h_d = coef[d][0] * part_scr[d, c, r0:r0 + STRIP, :] + coef[d][1] * cq[u][d][r0:r0 + STRIP, :]
                    h_s = h_d if h_s is None else h_s + h_d
                part_scr[0, c, r0:r0 + STRIP, :] = h_s
                ssq = ssq + jnp.sum(h_s * h_s, axis=0, keepdims=True)
            r_norm = lax.rsqrt(ssq * (1.0 / HEAD_DIM) + EPS)
            for r0 in range(0, HEAD_DIM, L):
                y = part_scr[0, c, r0:r0 + L, :] * r_norm * nw_ref[r0:r0 + L, :]
                o = jax.nn.sigmoid(ot_ref[c, r0:r0 + L, :].astype(F32)) * y
                hm_ref[pl.ds(_chunk_off(c), L), r0:r0 + L] = o.T.astype(hm_ref.dtype)
        return carry

    _group_loop(ns * nc // grp, finish_group)


def _state_specs(shapes, layer, first_layer, ns):
    specs = []
    for shp in shapes:
        tail = shp[4:]
        zeros = (0,) * len(tail)
        if first_layer:
            specs.append(pl.BlockSpec((ns, shp[1], 2, None) + tail, lambda b, h, z=zeros: (b, 0, 0, h) + z))
        else:
            specs.append(pl.BlockSpec((ns, None, 2, None) + tail, lambda b, h, z=zeros: (b, layer, 0, h) + z))
    return specs


def _mlstm_call(proj_t, proj_n, gp, nw_b, init, state_bufs, *, n_seq, seq_len, tok_off, n_heads, layer, depth):
    dk = HEAD_DIM
    nc = seq_len // CHUNK
    has_init = init is not None
    ns = 1 if has_init else _seqs_per_step(n_seq, nc)
    nct = ns * nc
    boff = tok_off // (ns * seq_len)
    write_state = not has_init
    first_layer = layer == 0
    args = [proj_t, proj_t, proj_t, proj_n, gp, nw_b]
    specs = [
        pl.BlockSpec((nct, dk, CHUNK), lambda b, h: (b + boff, h, 0)),
        pl.BlockSpec((nct, dk, CHUNK), lambda b, h: (b + boff, n_heads + h, 0)),
        pl.BlockSpec((nct, dk, CHUNK), lambda b, h: (b + boff, 2 * n_heads + h, 0)),
        pl.BlockSpec((ns * seq_len, dk), lambda b, h: (b + boff, h)),
        pl.BlockSpec((None, nct, 4, CHUNK), lambda b, h: (h, b + boff, 0, 0)),
        pl.BlockSpec((dk, LANES), lambda b, h: (h, 0)),
    ]
    aliases = {}
    if has_init:
        c0, n0, m0 = init
        args += [c0, n0[:, layer][:, :, :, None, :], m0[:, layer].reshape(-1)]
        specs += [
            pl.BlockSpec((None, None, 2, None, dk, dk), lambda b, h: (b, layer, 0, h, 0, 0)),
            pl.BlockSpec((None, 2, None, 1, dk), lambda b, h: (b, 0, h, 0, 0)),
            pl.BlockSpec(memory_space=pltpu.SMEM),
        ]
    out_shape = [jax.ShapeDtypeStruct((n_seq * seq_len, n_heads * dk), BF16)]
    out_specs = [pl.BlockSpec((ns * seq_len, dk), lambda b, h: (b, h))]
    if write_state:
        shapes = [(n_seq, depth, 2, n_heads, dk, dk), (n_seq, depth, 2, n_heads, 1, dk),
                  (n_seq, depth, 2, n_heads, 1, LANES)]
        if not first_layer:
            for k_, buf in enumerate(state_bufs):
                aliases[len(args)] = 1 + k_
                args.append(buf)
                specs.append(pl.BlockSpec(memory_space=pl.ANY))
        out_shape += [jax.ShapeDtypeStruct(s, F32) for s in shapes]
        out_specs += _state_specs(shapes, layer, first_layer, ns)
    kern = functools.partial(_mlstm_kernel, nc=nc, ns=ns, has_init=has_init, first_layer=first_layer,
                             write_state=write_state)
    return pl.pallas_call(
        kern, grid=(n_seq // ns, n_heads), in_specs=specs, out_specs=out_specs, out_shape=out_shape,
        input_output_aliases=aliases,
        scratch_shapes=[pltpu.VMEM((2, dk, dk), F32), pltpu.VMEM((2, 8, dk), F32),
                        pltpu.VMEM((2, nct, dk, CHUNK), F32), pltpu.VMEM((2, nct, dk, dk), F32),
                        pltpu.VMEM((2, nct, 8, dk), F32), pltpu.VMEM((2, nct, 8, CHUNK), F32),
                        pltpu.VMEM((2, nct, dk, dk), BF16), pltpu.VMEM((2, nct, 8, dk), F32),
                        pltpu.VMEM((2, nct, 8, dk), F32)],
        compiler_params=_cparams(("parallel", "parallel"), 48), name="mlstm_mixer",
    )(*args)


def _ret_kernel(*refs, nc, ns, has_init, first_layer, write_state):
    it = iter(refs)
    dl_ref, qt_ref, vt_ref, gt_ref, k_ref, nw_ref = (next(it) for _ in range(6))
    if has_init:
        s0_ref = next(it)
    if write_state and not first_layer:
        next(it)
    hr_ref = next(it)
    if write_state:
        sout_ref = next(it)
    st_scr, part_scr, inc_scr, sst_scr = next(it), next(it), next(it), next(it)

    L = CHUNK
    grp = min(MAX_GROUP, ns * nc)
    rowi = _iota((L, L), 0)
    coli = _iota((L, L), 1)
    k_scale = HEAD_DIM ** -0.5
    h_idx = pl.program_id(1)
    n_heads = pl.num_programs(1)

    consts = []
    for d in range(2):
        lg = _log_sigmoid(jnp.full((1, 1), dl_ref[d * n_heads + h_idx], F32))
        rel = (coli - rowi if d == 0 else rowi - coli).astype(F32)
        dmat_t = jnp.where(rel >= 0, jnp.exp(jnp.maximum(rel, 0.0) * lg), 0.0) * k_scale
        pos = _iota((1, L), 1).astype(F32)
        if d == 1:
            pos = (L - 1.0) - pos
        q_decay = jnp.exp((pos + 1.0) * lg)
        k_decay = jnp.exp((L - 1.0 - pos) * lg) * k_scale
        chunk_decay = jnp.exp(float(L) * lg)
        consts.append((dmat_t, q_decay, k_decay, chunk_decay))
    dmat_both = consts[0][0] + consts[1][0]

    def local_group(g, carry):
        cs = [g * grp + u for u in range(grp)]
        qt = [qt_ref[c] for c in cs]
        vt = [vt_ref[c] for c in cs]
        k = [k_ref[pl.ds(_chunk_off(c), L), :] for c in cs]
        qk = [_dot(k[u], qt[u]) for u in range(grp)]
        for u, c in enumerate(cs):
            vf = vt[u].astype(F32)
            res = _dot(jnp.concatenate([(vf * consts[d][2]).astype(BF16) for d in range(2)], axis=0), k[u])
            for d in range(2):
                inc_scr[d, c] = res[d * HEAD_DIM:(d + 1) * HEAD_DIM, :]
        for u, c in enumerate(cs):
            part_scr[c] = _dot(vt[u], (qk[u] * dmat_both).astype(BF16))
        return carry

    _group_loop(ns * nc // grp, local_group)

    for sq in range(ns):
        c_lo = sq * nc
        for d in range(2):
            if has_init:
                st_scr[d] = s0_ref[d].T
            else:
                st_scr[d] = jnp.zeros(st_scr.shape[1:], F32)

        def body(ci, carry, c_lo=c_lo):
            for d, c in ((0, c_lo + ci), (1, c_lo + nc - 1 - ci)):
                for r0 in range(0, HEAD_DIM, STRIP):
                    st = st_scr[d, r0:r0 + STRIP, :]
                    sst_scr[d, c, r0:r0 + STRIP, :] = st.astype(BF16)
                    st_scr[d, r0:r0 + STRIP, :] = consts[d][3] * st + inc_scr[d, c, r0:r0 + STRIP, :]
            return carry

        _scan_loop(nc, body, 0)

        if write_state:
            for d in range(2):
                if first_layer:
                    sout_ref[sq, 0, d] = st_scr[d].T
                else:
                    sout_ref[sq, d] = st_scr[d].T
            if first_layer:
                for l in range(1, sout_ref.shape[1]):
                    sout_ref[sq, l] = jnp.zeros(sout_ref.shape[2:], F32)

    def finish_group(g, carry):
        cs = [g * grp + u for u in range(grp)]
        qt = [qt_ref[c] for c in cs]
        sq = [_dot(jnp.concatenate([sst_scr[0, c], sst_scr[1, c]], axis=0), qt[u]) for u, c in enumerate(cs)]
        for u, c in enumerate(cs):
            ssq = jnp.zeros((1, L), F32)
            for r0 in range(0, HEAD_DIM, STRIP):
                sl = slice(r0, r0 + STRIP)
                o_s = part_scr[c, sl, :] + (consts[0][1] * sq[u][r0:r0 + STRIP, :]
                                            + consts[1][1] * sq[u][HEAD_DIM + r0:HEAD_DIM + r0 + STRIP, :])
                part_scr[c, sl, :] = o_s
                ssq = ssq + jnp.sum(o_s * o_s, axis=0, keepdims=True)
            r_norm = lax.rsqrt(ssq * (1.0 / HEAD_DIM) + EPS)
            for r0 in range(0, HEAD_DIM, L):
                y = part_scr[c, r0:r0 + L, :] * r_norm * nw_ref[r0:r0 + L, :]
                rg = gt_ref[c, r0:r0 + L, :].astype(F32)
                hr_ref[pl.ds(_chunk_off(c), L), r0:r0 + L] = (rg * jax.nn.sigmoid(rg) * y).T.astype(hr_ref.dtype)
        return carry

    _group_loop(ns * nc // grp, finish_group)


def _ret_call(proj_t, proj_n, decay_logit, nw_b, init, state_buf, *, n_seq, seq_len, tok_off, n_heads, layer,
              depth):
    dk = HEAD_DIM
    nc = seq_len // CHUNK
    has_init = init is not None
    ns = 1 if has_init else _seqs_per_step(n_seq, nc)
    nct = ns * nc
    boff = tok_off // (ns * seq_len)
    write_state = not has_init
    first_layer = layer == 0
    args = [decay_logit.reshape(-1), proj_t, proj_t, proj_t, proj_n, nw_b]
    specs = [
        pl.BlockSpec(memory_space=pltpu.SMEM),
        pl.BlockSpec((nct, dk, CHUNK), lambda b, h: (b + boff, 3 * n_heads + h, 0)),
        pl.BlockSpec((nct, dk, CHUNK), lambda b, h: (b + boff, 4 * n_heads + h, 0)),
        pl.BlockSpec((nct, dk, CHUNK), lambda b, h: (b + boff, 5 * n_heads + h, 0)),
        pl.BlockSpec((ns * seq_len, dk), lambda b, h: (b + boff, n_heads + h)),
        pl.BlockSpec((dk, LANES), lambda b, h: (h, 0)),
    ]
    aliases = {}
    if has_init:
        args.append(init)
        specs.append(pl.BlockSpec((None, None, 2, None, dk, dk), lambda b, h: (b, layer, 0, h, 0, 0)))
    out_shape = [jax.ShapeDtypeStruct((n_seq * seq_len, n_heads * dk), BF16)]
    out_specs = [pl.BlockSpec((ns * seq_len, dk), lambda b, h: (b, h))]
    if write_state:
        shapes = [(n_seq, depth, 2, n_heads, dk, dk)]
        if not first_layer:
            aliases[len(args)] = 1
            args.append(state_buf)
            specs.append(pl.BlockSpec(memory_space=pl.ANY))
        out_shape += [jax.ShapeDtypeStruct(s, F32) for s in shapes]
        out_specs += _state_specs(shapes, layer, first_layer, ns)
    kern = functools.partial(_ret_kernel, nc=nc, ns=ns, has_init=has_init, first_layer=first_layer,
                             write_state=write_state)
    return pl.pallas_call(
        kern, grid=(n_seq // ns, n_heads), in_specs=specs, out_specs=out_specs, out_shape=out_shape,
        input_output_aliases=aliases,
        scratch_shapes=[pltpu.VMEM((2, dk, dk), F32), pltpu.VMEM((nct, dk, CHUNK), F32),
                        pltpu.VMEM((2, nct, dk, dk), F32), pltpu.VMEM((2, nct, dk, dk), BF16)],
        compiler_params=_cparams(("parallel", "parallel"), 48), name="retention_mixer",
    )(*args)


def _merge_kernel(hmp_ref, hms_ref, hrp_ref, hrs_ref, ga_ref, gb_ref, x_ref, mod_ref, nw_ref, wa_ref, wb_ref,
                  wo_ref, wr_ref, x1_ref, h2_ref, aff_ref, wab_scr, wbb_scr, wob_scr, *, n_prompt_tiles):
    i = pl.program_id(0)

    @pl.when(i == 0)
    def _():
        wab_scr[...] = wa_ref[...].astype(BF16)
        wbb_scr[...] = wb_ref[...].astype(BF16)
        wob_scr[...] = wo_ref[...].astype(BF16)

    is_prompt = i < n_prompt_tiles
    hm = jnp.where(is_prompt, hmp_ref[...], hms_ref[...])
    hr = jnp.where(is_prompt, hrp_ref[...], hrs_ref[...])
    ya = _dot(hm, wab_scr[...])
    yb = _dot(hr, wbb_scr[...])
    merged = (jax.nn.sigmoid(ga_ref[...].astype(F32)) * ya + jax.nn.sigmoid(gb_ref[...].astype(F32)) * yb)
    y = _dot(merged.astype(BF16), wob_scr[...])
    x1 = x_ref[...] + mod_ref[2:3, :] * y
    x1_ref[...] = x1
    h2 = x1 * lax.rsqrt(jnp.mean(x1 * x1, axis=-1, keepdims=True) + EPS) * nw_ref[...]
    h2 = h2 * (1.0 + mod_ref[4:5, :]) + mod_ref[3:4, :]
    n_blk = h2.shape[1] // LANES
    for fb in range(n_blk):
        _tile_rows(h2_ref, fb, h2.shape[0], n_blk)[...] = h2[:, fb * LANES:(fb + 1) * LANES]
    logits = _dot_nt(wr_ref[...].astype(BF16), h2.astype(BF16))
    p = jnp.exp(logits - jnp.max(logits, axis=0, keepdims=True))
    aff_ref[...] = p / jnp.sum(p, axis=0, keepdims=True)


def _merge_call(hm_p, hm_s, hr_p, hr_s, proj_n, x, mod, norm_w, w_a, w_b, w_out, router_wt, layer, *,
                n_prompt, dec_seq, tm=512):
    n, d = x.shape
    ne = router_wt.shape[1]
    npt = n_prompt // tm
    grp = functools.partial(_group_of_tile, tm=tm, n_prompt=n_prompt, dec_seq=dec_seq)
    tile = pl.BlockSpec((tm, d), lambda i: (i, 0))
    tile_p = pl.BlockSpec((tm, d), lambda i: (jnp.minimum(i, npt - 1), 0))
    tile_s = pl.BlockSpec((tm, d), lambda i: (jnp.maximum(i - npt, 0), 0))
    full = pl.BlockSpec((None, d, d), lambda i: (layer, 0, 0))
    return pl.pallas_call(
        functools.partial(_merge_kernel, n_prompt_tiles=npt), grid=(n // tm,),
        in_specs=[
            tile_p, tile_s, tile_p, tile_s,
            pl.BlockSpec((tm, d), lambda i: (i, 2)),
            pl.BlockSpec((tm, d), lambda i: (i, 3)),
            tile,
            pl.BlockSpec((None, 6, d), lambda i: (grp(i), 0, 0)),
            pl.BlockSpec((1, d), lambda i: (0, 0)),
            full, full, full,
            pl.BlockSpec((None, ne, d), lambda i: (layer, 0, 0)),
        ],
        out_specs=[tile, pl.BlockSpec((tm * (d // LANES), LANES), lambda i: (i, 0)),
                   pl.BlockSpec((ne, tm), lambda i: (0, i))],
        out_shape=[jax.ShapeDtypeStruct((n, d), F32), jax.ShapeDtypeStruct((n * (d // LANES), LANES), F32),
                   jax.ShapeDtypeStruct((ne, n), F32)],
        scratch_shapes=[pltpu.VMEM((d, d), BF16)] * 3,
        compiler_params=_cparams(("arbitrary",), 56), name="merge_out_router",
    )(hm_p, hm_s, hr_p, hr_s, proj_n, proj_n, x, mod, norm_w.reshape(1, d), w_a, w_b, w_out, router_wt)


ROUTE_GROUP = 4


def _route_kernel(a_ref, at_ref, idx_ref, gate_ref, thr_scr, *, cap, n_tok):
    n_sets = a_ref.shape[0]
    nb = n_tok // LANES
    a_all = a_ref[...]

    def as_f32(bits):
        return lax.bitcast_convert_type(bits, F32)

    def count_ge(cand):
        m = jnp.where(a_all >= cand, 1.0, 0.0)
        return jnp.sum(jnp.sum(m, axis=1, keepdims=True), axis=2, keepdims=True)

    def bit_step(k, thr):
        cand = thr | lax.shift_left(jnp.int32(1), 30 - k)
        return jnp.where(count_ge(as_f32(cand)) >= cap, cand, thr)

    thr_scr[...] = lax.fori_loop(0, 31, bit_step, jnp.zeros((n_sets, 1, 1), jnp.int32))

    r128 = _iota((LANES, LANES), 0)
    c128 = _iota((LANES, LANES), 1)
    upper = jnp.where(r128 <= c128, 1.0, 0.0).astype(BF16)
    lower_t = jnp.where(r128 >= c128, 1.0, 0.0).astype(BF16)
    rb = _iota((nb, nb), 0)
    cb = _iota((nb, nb), 1)
    blk_before_rows = jnp.where(cb < rb, 1.0, 0.0).astype(BF16)
    blk_before_cols = jnp.where(rb < cb, 1.0, 0.0).astype(BF16)

    def incl_counts(mask):
        within = _dot(mask.astype(BF16), upper)
        before = _dot(blk_before_rows, within.astype(BF16))[:, LANES - 1:LANES]
        return within + before

    def incl_counts_t(mask_t):
        within = _dot(lower_t, mask_t.astype(BF16))
        before = _dot(within.astype(BF16), blk_before_cols)[LANES - 1:LANES, :]
        return within + before

    slot = _iota((1, cap), 1).astype(F32)
    blk_col = _iota((nb, 1), 0).astype(F32)
    sub_col = _iota((LANES, 1), 0).astype(F32)

    def per_group(g, carry):
        sets = [g + u * (n_sets // ROUTE_GROUP) for u in range(ROUTE_GROUP)]
        stage = []
        for s in sets:
            thr_bits = thr_scr[s]
            thr = as_f32(thr_bits)
            nxt = as_f32(thr_bits + 1)
            a = a_ref[s]
            a_t = at_ref[s]
            gt = jnp.where(a >= nxt, 1.0, 0.0)
            eq = jnp.where((a >= thr) & (a < nxt), 1.0, 0.0)
            gt_t = jnp.where(a_t >= nxt, 1.0, 0.0)
            eq_t = jnp.where((a_t >= thr) & (a_t < nxt), 1.0, 0.0)
            n_gt = jnp.sum(jnp.sum(gt, axis=1, keepdims=True), axis=0, keepdims=True)
            stage.append((a_t, gt, eq, gt_t, eq_t, cap - n_gt))
        ties = [(incl_counts(eq), incl_counts_t(eq_t)) for _, _, eq, _, eq_t, _ in stage]
        sels = []
        for (a_t, gt, eq, gt_t, eq_t, need), (c_eq, c_eq_t) in zip(stage, ties):
            sels.append((gt + eq * jnp.where(c_eq - eq < need, 1.0, 0.0),
                         gt_t + eq_t * jnp.where(c_eq_t - eq_t < need, 1.0, 0.0)))
        cnts = [(incl_counts(sel), incl_counts_t(sel_t)) for sel, sel_t in sels]
        picks = []
        for (a_t, *_), (cnt, cnt_t) in zip(stage, cnts):
            blk_end = cnt[:, LANES - 1:LANES]
            blk_of_slot = jnp.sum(jnp.where(blk_end <= slot, 1.0, 0.0), axis=0, keepdims=True)
            onehot_blk = jnp.where(blk_col == blk_of_slot, 1.0, 0.0).astype(BF16)
            cnt_rows = _dot3(cnt_t, onehot_blk)
            a_rows = _dot3(a_t, onehot_blk)
            picks.append((blk_of_slot, cnt_rows, a_rows))
        for s, (blk_of_slot, cnt_rows, a_rows) in zip(sets, picks):
            sub_of_slot = jnp.sum(jnp.where(cnt_rows <= slot, 1.0, 0.0), axis=0, keepdims=True)
            gate = jnp.sum(jnp.where(sub_col == sub_of_slot, a_rows, 0.0), axis=0, keepdims=True)
            idx_ref[s] = (blk_of_slot * LANES + sub_of_slot).astype(jnp.int32)
            gate_ref[s] = gate
        return carry

    lax.fori_loop(0, n_sets // ROUTE_GROUP, per_group, 0)


def _route_call(aff_t, n_pass, cap):
    ne, n = aff_t.shape
    n_tok = n // n_pass
    nb = n_tok // LANES
    a4 = aff_t.reshape(ne, n_pass, nb, LANES).transpose(1, 0, 2, 3).reshape(n_pass * ne, nb, LANES)
    a4_t = a4.transpose(0, 2, 1)
    n_sets = n_pass * ne
    return pl.pallas_call(
        functools.partial(_route_kernel, cap=cap, n_tok=n_tok),
        out_shape=[jax.ShapeDtypeStruct((n_sets, 1, cap), jnp.int32),
                   jax.ShapeDtypeStruct((n_sets, 1, cap), F32)],
        scratch_shapes=[pltpu.VMEM((n_sets, 1, 1), jnp.int32)],
        compiler_params=pltpu.CompilerParams(vmem_limit_bytes=32 * MIB), name="expert_choice_route",
    )(a4, a4_t)


ROW_LOOP_UNROLL = 8


def _moe_kernel(idx_ref, gate_ref, h_hbm, w1_ref, w3_ref, w2_ref, out_hbm,
                xe_scr, xb_scr, ye_a, ye_b, acc_scr, gsem, osem, *, rows, rows_pad):
    e = pl.program_id(0)
    f = pl.program_id(1)
    n_e = pl.num_programs(0)
    n_f = pl.num_programs(1)
    chunk = rows_pad // n_f

    n_blk = xb_scr.shape[1] // LANES

    def tile_of(t):
        return pl.ds(pl.multiple_of(t * n_blk, n_blk), n_blk)

    def gather_row(slot, s):
        tok = idx_ref[slot * rows_pad + s]
        pltpu.make_async_copy(h_hbm.at[tile_of(tok), :], xe_scr.at[tile_of(s), :], gsem).start()

    def scatter_row(slot, s, ye_ref):
        tok = idx_ref[slot * rows_pad + s]
        g = gate_ref[slot * rows_pad + s]
        acc_scr[pl.ds(tok, 1), :] += ye_ref[pl.ds(s, 1), :] * g

    def wait_gather():
        pltpu.make_async_copy(h_hbm.at[pl.ds(0, rows_pad * n_blk), :], xe_scr, gsem).wait()

    @pl.when((e == 0) & (f == 0))
    def _():
        acc_scr[...] = jnp.zeros_like(acc_scr)
        ye_a[...] = jnp.zeros_like(ye_a)
        ye_b[...] = jnp.zeros_like(ye_b)

        def issue(s, carry):
            gather_row(1, s)
            return carry
        lax.fori_loop(0, rows_pad, issue, 0, unroll=ROW_LOOP_UNROLL)

    def step(ye_cur, ye_prev):
        @pl.when(f == 0)
        def _():
            wait_gather()
            for fb in range(n_blk):
                xb_scr[:, fb * LANES:(fb + 1) * LANES] = _tile_rows(xe_scr, fb, rows, n_blk)[...].astype(BF16)
            ye_cur[0:rows, :] = jnp.zeros((rows, ye_cur.shape[1]), F32)

        xb = xb_scr[...]
        h1 = _dot(xb, w1_ref[...].astype(BF16))
        h3 = _dot(xb, w3_ref[...].astype(BF16))
        he = (h1 * jax.nn.sigmoid(h1) * h3).astype(BF16)
        ye_cur[0:rows, :] += _dot(he, w2_ref[...].astype(BF16))
        base = f * chunk
        for r in range(chunk):
            gather_row(e + 2, base + r)
        for r in range(chunk):
            scatter_row(e, base + r, ye_prev)

        @pl.when((e == n_e - 1) & (f == n_f - 1))
        def _():
            def scatter(s, carry):
                scatter_row(n_e, s, ye_cur)
                return carry
            lax.fori_loop(0, rows, scatter, 0, unroll=ROW_LOOP_UNROLL)
            wait_gather()
            cp = pltpu.make_async_copy(acc_scr, out_hbm, osem)
            cp.start()
            cp.wait()

    @pl.when(e % 2 == 0)
    def _():
        step(ye_a, ye_b)

    @pl.when(e % 2 == 1)
    def _():
        step(ye_b, ye_a)


def _moe_call(idx, gate, h2, w1, w3, w2, layer, tf=256):
    d = w1.shape[2]
    n_blk = d // LANES
    n = h2.shape[0] // n_blk
    ne, rows = idx.shape
    dff = w1.shape[3]
    nf = dff // tf
    rows_pad = -(-rows // (8 * nf)) * 8 * nf
    pad = ((1, 1), (0, rows_pad - rows))
    idx_all = jnp.pad(idx, pad).reshape(-1)
    gate_all = jnp.pad(gate, pad).reshape(-1)
    grid_spec = pltpu.PrefetchScalarGridSpec(
        num_scalar_prefetch=1,
        grid=(ne, nf),
        in_specs=[
            pl.BlockSpec(memory_space=pltpu.SMEM),
            pl.BlockSpec(memory_space=pl.ANY),
            pl.BlockSpec((None, None, d, tf), lambda e, f, idx: (layer, e, 0, f)),
            pl.BlockSpec((None, None, d, tf), lambda e, f, idx: (layer, e, 0, f)),
            pl.BlockSpec((None, None, tf, d), lambda e, f, idx: (layer, e, f, 0)),
        ],
        out_specs=pl.BlockSpec(memory_space=pl.ANY),
        scratch_shapes=[
            pltpu.VMEM((rows_pad * n_blk, LANES), F32), pltpu.VMEM((rows, d), BF16),
            pltpu.VMEM((rows_pad, d), F32), pltpu.VMEM((rows_pad, d), F32),
            pltpu.VMEM((n, d), F32), pltpu.SemaphoreType.DMA, pltpu.SemaphoreType.DMA,
        ],
    )
    return pl.pallas_call(
        functools.partial(_moe_kernel, rows=rows, rows_pad=rows_pad), grid_spec=grid_spec,
        out_shape=jax.ShapeDtypeStruct((n, d), F32),
        compiler_params=_cparams(("arbitrary", "arbitrary"), 62), name="expert_ffn",
    )(idx_all, gate_all, h2, w1, w3, w2)


def _rope_tables(t):
    rows = t // GRID_W
    row = jnp.repeat(jnp.arange(rows, dtype=F32), GRID_W)
    colp = jnp.tile(jnp.arange(GRID_W, dtype=F32), rows)
    n_freq = HEAD_DIM // 4
    inv = ROPE_BASE ** (-jnp.arange(n_freq, dtype=F32) / n_freq)
    ang = jnp.concatenate([row[:, None] * inv, colp[:, None] * inv], axis=-1)
    return jnp.cos(ang), jnp.sin(ang)


def kernel(x_prompt, x_sample, c, state_mlstm_C, state_mlstm_n, state_mlstm_m, state_ret_S, c_ctx, w_mod, b_mod,
           norm1_w, norm2_w, w_in, mlstm_if_b, mlstm_norm_w, ret_decay_logit, ret_norm_w, w_branch_a, w_branch_b,
           w_out, router_w, ffn_w1, ffn_w3, ffn_w2, final_norm_w):
    bp, seq, d = x_prompt.shape
    db, dec_seq, _ = x_sample.shape
    depth = w_mod.shape[0]
    hm = mlstm_if_b.shape[-1]
    hr = ret_decay_logit.shape[-1]
    ne = router_w.shape[-1]
    n_prompt = bp * seq
    n_sample = db * dec_seq
    assert n_prompt == n_sample and hm * HEAD_DIM == d and hr * HEAD_DIM == d and hm == hr
    cap = EC_FACTOR * n_prompt // ne
    tm = 512
    tp = 2048
    geo = dict(n_prompt=n_prompt, dec_seq=dec_seq)

    cond8 = jnp.concatenate([c_ctx[None, :], c, jnp.zeros((8 - 1 - db, d), F32)], axis=0)
    mod = _mod_call(cond8, w_mod, b_mod)[:, :1 + db].reshape(depth, 1 + db, 6, d)

    n_m = 4 * d
    g0 = n_m + 4 * hm

    w_t = jnp.swapaxes(w_in, 1, 2)
    mq, mk, mv, mo = (k * d for k in range(4))
    rq, rk, rv, rg, ga, gb = (g0 + k * d for k in range(6))
    wg_t = w_t[:, n_m:g0]
    gate_bias = jnp.transpose(mlstm_if_b, (0, 2, 1, 3)).reshape(depth, 4 * hm, 1)
    router_wt = jnp.swapaxes(router_w, 1, 2)
    nw_m = jnp.broadcast_to(mlstm_norm_w[:, :, None], (depth, d, LANES))
    nw_r = jnp.broadcast_to(ret_norm_w[:, :, None], (depth, d, LANES))

    cos, sin = _rope_tables(dec_seq)
    cos_t = cos.T.reshape(HEAD_DIM // 2, dec_seq // tp, tp).transpose(1, 0, 2)
    sin_t = sin.T.reshape(HEAD_DIM // 2, dec_seq // tp, tp).transpose(1, 0, 2)

    x, h = _resid_norm_call(x_prompt.reshape(n_prompt, d), None, None, mod[0], norm1_w[0], gate_row=0,
                            mod_rows=(0, 1), h_dtype=BF16, write_x=True, x_tail=x_sample.reshape(n_sample, d),
                            **geo)
    m_bufs, s_buf = None, None
    y_prompt = y_sample = None
    for l in range(depth):
        proj_t = _proj_t_call(h, w_t, l, (mq, mv, mo, rq, rv, rg), (cos_t, sin_t), 3, tm=tp, **geo)
        proj_n = _proj_call(h, w_t, l, (mk, rk, ga, gb), (cos, sin), 1, tm=tp, **geo)
        gp = _gate_call(h, wg_t[l], gate_bias[l], hm, tm=tm)

        mix = dict(n_heads=hm, layer=l, depth=depth)
        hm_p, *m_bufs = _mlstm_call(proj_t, proj_n, gp, nw_m[l], None, m_bufs, n_seq=bp, seq_len=seq, tok_off=0,
                                    **mix)
        (hm_s,) = _mlstm_call(proj_t, proj_n, gp, nw_m[l], (state_mlstm_C, state_mlstm_n, state_mlstm_m), None,
                              n_seq=db, seq_len=dec_seq, tok_off=n_prompt, **mix)
        hr_p, s_buf = _ret_call(proj_t, proj_n, ret_decay_logit[l], nw_r[l], None, s_buf, n_seq=bp, seq_len=seq,
                                tok_off=0, **mix)
        (hr_s,) = _ret_call(proj_t, proj_n, ret_decay_logit[l], nw_r[l], state_ret_S, None, n_seq=db,
                            seq_len=dec_seq, tok_off=n_prompt, **mix)

        x1, h2, aff_t = _merge_call(hm_p, hm_s, hr_p, hr_s, proj_n, x, mod[l], norm2_w[l], w_branch_a, w_branch_b,
                                    w_out, router_wt, l, tm=tm, **geo)
        idx, gate = _route_call(aff_t, 2, cap)
        idx = idx.reshape(2, ne, cap) + (jnp.arange(2, dtype=jnp.int32) * n_prompt)[:, None, None]
        idx = jnp.swapaxes(idx, 0, 1).reshape(ne, 2 * cap)
        gate = jnp.swapaxes(gate.reshape(2, ne, cap), 0, 1).reshape(ne, 2 * cap)
        moe = _moe_call(idx, gate, h2, ffn_w1, ffn_w3, ffn_w2, l)
        if l + 1 < depth:
            x, h = _resid_norm_call(x1, moe, mod[l], mod[l + 1], norm1_w[l + 1], gate_row=5, mod_rows=(0, 1),
                                    h_dtype=BF16, write_x=True, **geo)
        else:
            fin = dict(gate_row=5, mod_rows=None, h_dtype=F32, write_x=False, **geo)
            _, y_prompt = _resid_norm_call(x1, moe, mod[l], None, final_norm_w, row_off=0, n_rows=n_prompt, **fin)
            _, y_sample = _resid_norm_call(x1, moe, mod[l], None, final_norm_w, row_off=n_prompt, n_rows=n_sample,
                                           **fin)

    c_buf, n_buf, m_buf = m_bufs
    return (y_prompt.reshape(bp, seq, d), y_sample.reshape(db, dec_seq, d), c_buf, n_buf[:, :, :, :, 0, :],
            m_buf[:, :, :, :, 0, 0], s_buf)
```

```python
import functools

import jax
import jax.numpy as jnp
from jax import lax
from jax.experimental import pallas as pl
from jax.experimental.pallas import tpu as pltpu

F32 = jnp.float32
BF16 = jnp.bfloat16

GRID_W = 64
CHUNK = 128
LANES = 128
HEAD_DIM = 256
EC_FACTOR = 2
ROPE_BASE = 10000.0
EPS = 1e-6
V7X_VMEM_BYTES = 64 * 1024 * 1024
MIB = 1024 * 1024


def _cparams(semantics, vmem_mib):
    assert vmem_mib * MIB < V7X_VMEM_BYTES
    return pltpu.CompilerParams(dimension_semantics=semantics, vmem_limit_bytes=vmem_mib * MIB)


def _dot(a, b):
    return jnp.dot(a, b, preferred_element_type=F32)


def _dot_nt(a, b):
    return lax.dot_general(a, b, (((1,), (1,)), ((), ())), preferred_element_type=F32)


def _log_sigmoid(x):
    return -(jnp.maximum(-x, 0.0) + jnp.log1p(jnp.exp(-jnp.abs(x))))


def _split3(x):
    hi = x.astype(BF16)
    r1 = x - hi.astype(F32)
    mid = r1.astype(BF16)
    lo = (r1 - mid.astype(F32)).astype(BF16)
    return hi, mid, lo


def _dot3(x, m):
    hi, mid, lo = _split3(x)
    return _dot(hi, m) + _dot(mid, m) + _dot(lo, m)


def _iota(shape, dim):
    return lax.broadcasted_iota(jnp.int32, shape, dim)


def _tile_rows(ref, fb, n_tok, n_blk):
    return ref.at[pl.ds(fb, n_tok, stride=n_blk), :]


def _mod_kernel(cond_ref, w_ref, b_ref, out_ref):
    c = cond_ref[...]
    s = (c * jax.nn.sigmoid(c)).astype(BF16)
    out_ref[...] = _dot(s, w_ref[...].astype(BF16)) + b_ref[...]


def _mod_call(cond8, w_mod, b_mod):
    depth, d, w6 = w_mod.shape
    tn = 1536
    return pl.pallas_call(
        _mod_kernel,
        grid=(depth, w6 // tn),
        in_specs=[
            pl.BlockSpec((8, d), lambda l, j: (0, 0)),
            pl.BlockSpec((None, d, tn), lambda l, j: (l, 0, j)),
            pl.BlockSpec((None, 1, tn), lambda l, j: (l, 0, j)),
        ],
        out_specs=pl.BlockSpec((None, 8, tn), lambda l, j: (l, 0, j)),
        out_shape=jax.ShapeDtypeStruct((depth, 8, w6), F32),
        compiler_params=_cparams(("parallel", "parallel"), 32),
        name="adaln_mod",
    )(cond8, w_mod, b_mod.reshape(depth, 1, w6))


def _group_of_tile(i, tm, n_prompt, dec_seq):
    return jnp.maximum(i * tm - n_prompt + dec_seq, 0) // dec_seq


def _resid_norm_kernel(*refs, has_delta, has_mod, gate_row, mod_rows, write_x, n_first_tiles):
    it = iter(refs)
    x_ref = next(it)
    x2_ref = next(it) if n_first_tiles else None
    delta_ref = next(it) if has_delta else None
    mod_ref = next(it) if (has_delta or has_mod) else None
    nw_ref = next(it)
    xo_ref = next(it) if write_x else None
    h_ref = next(it)
    x = x_ref[...]
    if n_first_tiles:
        x = jnp.where(pl.program_id(0) < n_first_tiles, x, x2_ref[...])
    if has_delta:
        x = x + mod_ref[gate_row:gate_row + 1, :] * delta_ref[...]
    if write_x:
        xo_ref[...] = x
    y = x * lax.rsqrt(jnp.mean(x * x, axis=-1, keepdims=True) + EPS) * nw_ref[...]
    if has_mod:
        sh_row, sc_row = mod_rows
        y = y * (1.0 + mod_ref[sc_row:sc_row + 1, :]) + mod_ref[sh_row:sh_row + 1, :]
    h_ref[...] = y.astype(h_ref.dtype)


def _resid_norm_call(x, delta, mod_gate, mod_norm, norm_w, *, gate_row, mod_rows, h_dtype, write_x,
                     n_prompt, dec_seq, tm=1024, row_off=0, n_rows=None, x_tail=None):
    n, d = x.shape
    has_delta = delta is not None
    has_mod = mod_norm is not None
    grp = functools.partial(_group_of_tile, tm=tm, n_prompt=n_prompt, dec_seq=dec_seq)
    if x_tail is None:
        n_rows = n if n_rows is None else n_rows
        toff = row_off // tm
        nft = 0
        tile_in = pl.BlockSpec((tm, d), lambda i: (i + toff, 0))
        args, specs = [x], [tile_in]
    else:
        assert row_off == 0 and n_rows is None and not has_delta
        n_rows, toff, nft = n + x_tail.shape[0], 0, n // tm
        args = [x, x_tail]
        specs = [pl.BlockSpec((tm, d), lambda i: (jnp.minimum(i, nft - 1), 0)),
                 pl.BlockSpec((tm, d), lambda i: (jnp.maximum(i - nft, 0), 0))]
    tile_out = pl.BlockSpec((tm, d), lambda i: (i, 0))
    if has_delta:
        args.append(delta)
        specs.append(tile_in)
    if has_delta or has_mod:
        mg = mod_gate if has_delta else mod_norm
        mn = mod_norm if has_mod else mod_gate
        args.append(jnp.concatenate([mg, mn], axis=1))
        specs.append(pl.BlockSpec((None, 12, d), lambda i: (grp(i + toff), 0, 0)))
    args.append(norm_w.reshape(1, d))
    specs.append(pl.BlockSpec((1, d), lambda i: (0, 0)))
    out_shape, out_specs = [], []
    if write_x:
        out_shape.append(jax.ShapeDtypeStruct((n_rows, d), F32))
        out_specs.append(tile_out)
    out_shape.append(jax.ShapeDtypeStruct((n_rows, d), h_dtype))
    out_specs.append(tile_out)
    kern = functools.partial(
        _resid_norm_kernel, has_delta=has_delta, has_mod=has_mod, gate_row=gate_row,
        mod_rows=None if mod_rows is None else (6 + mod_rows[0], 6 + mod_rows[1]), write_x=write_x,
        n_first_tiles=nft)
    outs = pl.pallas_call(
        kern, grid=(n_rows // tm,), in_specs=specs, out_specs=out_specs, out_shape=out_shape,
        compiler_params=_cparams(("parallel",), 48), name="resid_norm",
    )(*args)
    return outs if write_x else (None, outs[0])


def _rope_pair(x1, x2, cos, sin):
    return x1 * cos - x2 * sin, x1 * sin + x2 * cos


def _weight_rows_spec(layer, row_starts, tn, d):
    def index_map(j, i):
        start = sum(jnp.where(j == k, s, 0) for k, s in enumerate(row_starts))
        return layer, pl.multiple_of(start, 8), 0
    assert all(s % 8 == 0 for s in row_starts)
    return pl.BlockSpec((pl.Element(1), pl.Element(tn), pl.Element(d)), index_map)


def _proj_kernel(h_ref, w_ref, cos_ref, sin_ref, out_ref, wb_scr, *, tm, rope_tile, n_prompt, dec_seq):
    j = pl.program_id(0)
    i = pl.program_id(1)

    @pl.when(i == 0)
    def _():
        wb_scr[...] = w_ref[0].astype(BF16)

    row0 = pl.multiple_of(i * tm, tm)
    use_rope = (j == rope_tile) & (row0 >= n_prompt)
    half = HEAD_DIM // 2

    def heads(rope):
        h = h_ref[...]
        if rope:
            pos0 = pl.multiple_of((row0 - n_prompt) % dec_seq, tm)
            cos = cos_ref[pl.ds(pos0, tm), :]
            sin = sin_ref[pl.ds(pos0, tm), :]
        for hh in range(wb_scr.shape[0] // HEAD_DIM):
            c0 = hh * HEAD_DIM
            acc = _dot_nt(h, wb_scr[c0:c0 + HEAD_DIM, :])
            if rope:
                y1, y2 = _rope_pair(acc[:, :half], acc[:, half:], cos, sin)
                out_ref[:, c0:c0 + half] = y1.astype(out_ref.dtype)
                out_ref[:, c0 + half:c0 + HEAD_DIM] = y2.astype(out_ref.dtype)
            else:
                out_ref[:, c0:c0 + HEAD_DIM] = acc.astype(out_ref.dtype)

    pl.when(use_rope)(functools.partial(heads, True))
    pl.when(jnp.logical_not(use_rope))(functools.partial(heads, False))


def _proj_call(h, w_t, layer, row_starts, rope, rope_tile, *, n_prompt, dec_seq, tm=512, tn=1024):
    n, d = h.shape
    nj = len(row_starts)
    kern = functools.partial(_proj_kernel, tm=tm, rope_tile=rope_tile, n_prompt=n_prompt, dec_seq=dec_seq)
    return pl.pallas_call(
        kern, grid=(nj, n // tm),
        in_specs=[
            pl.BlockSpec((tm, d), lambda j, i: (i, 0)),
            _weight_rows_spec(layer, row_starts, tn, d),
            pl.BlockSpec(rope[0].shape, lambda j, i: (0, 0)),
            pl.BlockSpec(rope[1].shape, lambda j, i: (0, 0)),
        ],
        out_specs=pl.BlockSpec((tm, tn), lambda j, i: (i, j)),
        out_shape=jax.ShapeDtypeStruct((n, nj * tn), BF16),
        scratch_shapes=[pltpu.VMEM((tn, d), BF16)],
        compiler_params=_cparams(("parallel", "arbitrary"), 56), name="in_proj",
    )(h, w_t, *rope)


def _proj_t_kernel(h_ref, w_ref, cos_ref, sin_ref, out_ref, wb_scr, *, tm, rope_tile, n_prompt, dec_seq):
    j = pl.program_id(0)
    i = pl.program_id(1)

    @pl.when(i == 0)
    def _():
        wb_scr[...] = w_ref[0].astype(BF16)

    row0 = pl.multiple_of(i * tm, tm)
    use_rope = (j == rope_tile) & (row0 >= n_prompt)
    half = HEAD_DIM // 2

    def heads(rope):
        h = h_ref[...]
        if rope:
            blk = ((row0 - n_prompt) % dec_seq) // tm
            cos = cos_ref[blk]
            sin = sin_ref[blk]
        for hh in range(wb_scr.shape[0] // HEAD_DIM):
            r0 = hh * HEAD_DIM
            acc = _dot_nt(wb_scr[r0:r0 + HEAD_DIM, :], h)
            if rope:
                y1, y2 = _rope_pair(acc[:half, :], acc[half:, :], cos, sin)
                y = jnp.concatenate([y1, y2], axis=0).astype(out_ref.dtype)
            else:
                y = acc.astype(out_ref.dtype)
            for s in range(tm // CHUNK):
                out_ref[s, r0:r0 + HEAD_DIM, :] = y[:, s * CHUNK:(s + 1) * CHUNK]

    pl.when(use_rope)(functools.partial(heads, True))
    pl.when(jnp.logical_not(use_rope))(functools.partial(heads, False))


def _proj_t_call(h, w_t, layer, row_starts, rope_t, rope_tile, *, n_prompt, dec_seq, tm=512, tn=1024):
    n, d = h.shape
    nj = len(row_starts)
    kern = functools.partial(_proj_t_kernel, tm=tm, rope_tile=rope_tile, n_prompt=n_prompt, dec_seq=dec_seq)
    return pl.pallas_call(
        kern, grid=(nj, n // tm),
        in_specs=[
            pl.BlockSpec((tm, d), lambda j, i: (i, 0)),
            _weight_rows_spec(layer, row_starts, tn, d),
            pl.BlockSpec(rope_t[0].shape, lambda j, i: (0, 0, 0)),
            pl.BlockSpec(rope_t[1].shape, lambda j, i: (0, 0, 0)),
        ],
        out_specs=pl.BlockSpec((tm // CHUNK, tn, CHUNK), lambda j, i: (i, j, 0)),
        out_shape=jax.ShapeDtypeStruct((n // CHUNK, nj * tn, CHUNK), BF16),
        scratch_shapes=[pltpu.VMEM((tn, d), BF16)],
        compiler_params=_cparams(("parallel", "arbitrary"), 56), name="in_proj_t",
    )(h, w_t, *rope_t)


def _gate_kernel(h_ref, wg_ref, bias_ref, out_ref, *, tm, n_heads):
    g = _dot_nt(wg_ref[...].astype(BF16), h_ref[...]) + bias_ref[...]
    nd = 2 * n_heads
    ig = g[0:nd, :]
    lf = _log_sigmoid(g[nd:2 * nd, :])
    r = _iota((CHUNK, CHUNK), 0)
    c = _iota((CHUNK, CHUNK), 1)
    upper = jnp.where(r <= c, 1.0, 0.0).astype(BF16)
    lower = jnp.where(r >= c, 1.0, 0.0).astype(BF16)
    is_fwd = _iota((nd, CHUNK), 0) < n_heads
    for s in range(tm // CHUNK):
        sl = slice(s * CHUNK, (s + 1) * CHUNK)
        lf_c = lf[:, sl]
        b = jnp.where(is_fwd, _dot3(lf_c, upper), _dot3(lf_c, lower))
        ig_c = ig[:, sl]
        for hh in range(n_heads):
            out_ref[hh, s, 0:1, :] = ig_c[hh:hh + 1, :]
            out_ref[hh, s, 1:2, :] = ig_c[n_heads + hh:n_heads + hh + 1, :]
            out_ref[hh, s, 2:3, :] = b[hh:hh + 1, :]
            out_ref[hh, s, 3:4, :] = b[n_heads + hh:n_heads + hh + 1, :]


def _gate_call(h, wg_t, bias_col, n_heads, tm=512):
    n, d = h.shape
    ng = 4 * n_heads
    return pl.pallas_call(
        functools.partial(_gate_kernel, tm=tm, n_heads=n_heads),
        grid=(n // tm,),
        in_specs=[
            pl.BlockSpec((tm, d), lambda i: (i, 0)),
            pl.BlockSpec((ng, d), lambda i: (0, 0)),
            pl.BlockSpec((ng, 1), lambda i: (0, 0)),
        ],
        out_specs=pl.BlockSpec((n_heads, tm // CHUNK, 4, CHUNK), lambda i: (0, i, 0, 0)),
        out_shape=jax.ShapeDtypeStruct((n_heads, n // CHUNK, 4, CHUNK), F32),
        compiler_params=_cparams(("parallel",), 32), name="mlstm_gates",
    )(h, wg_t, bias_col)


def _scan_loop(nc, body, init):
    if nc <= 2:
        carry = init
        for ci in range(nc):
            carry = body(ci, carry)
        return carry
    return lax.fori_loop(0, nc, body, init)


MAX_GROUP = 16
BF16_ROWS = 16
STRIP = 32


def _seqs_per_step(n_seq, nc):
    ns = max(1, MAX_GROUP // nc)
    while n_seq % ns:
        ns -= 1
    return ns


def _group_loop(n_groups, body):
    if n_groups == 1:
        body(0, 0)
    else:
        lax.fori_loop(0, n_groups, body, 0)


def _chunk_off(c):
    return c * CHUNK if isinstance(c, int) else pl.multiple_of(c * CHUNK, CHUNK)


def _mlstm_kernel(*refs, nc, ns, has_init, first_layer, write_state):
    it = iter(refs)
    qt_ref, vt_ref, ot_ref, k_ref, gp_ref, nw_ref = (next(it) for _ in range(6))
    if has_init:
        c0_ref, n0_ref, m0_ref = next(it), next(it), next(it)
    if write_state and not first_layer:
        next(it), next(it), next(it)
    hm_ref = next(it)
    if write_state:
        cout_ref, nout_ref, mout_ref = next(it), next(it), next(it)
    ct_scr, n_scr, part_scr, inc_scr, ninc_scr, rows_scr, cst_scr, nst_scr, coef_scr = (next(it) for _ in range(9))

    L = CHUNK
    grp = min(MAX_GROUP, ns * nc)
    row = _iota((L, L), 0)
    col = _iota((L, L), 1)
    k_scale = HEAD_DIM ** -0.5
    b_idx = pl.program_id(0)
    h_idx = pl.program_id(1)
    n_heads = pl.num_programs(1)

    def gates(d, c):
        g = gp_ref[c]
        ig = g[d:d + 1, :]
        brow = g[2 + d:3 + d, :]
        blast = brow[:, L - 1:L] if d == 0 else brow[:, 0:1]
        return ig, brow, blast

    def local_group(g, carry):
        cs = [g * grp + u for u in range(grp)]
        pairs = [(u, d) for u in range(grp) for d in range(2)]
        qt = [qt_ref[c] for c in cs]
        vt = [vt_ref[c] for c in cs]
        k = [k_ref[pl.ds(_chunk_off(c), L), :] for c in cs]
        qk = [_dot(k[u], qt[u]) for u in range(grp)]
        vf = [v.astype(F32) for v in vt]
        gts = {(u, d): gates(d, cs[u]) for u, d in pairs}
        for u, c in enumerate(cs):
            lhs = []
            for d in range(2):
                ig, brow, blast = gts[u, d]
                wj = blast - brow + ig
                mloc2 = jnp.max(wj, axis=1, keepdims=True)
                e = jnp.exp(wj - mloc2) * k_scale
                rows_scr[d, c, 2:3, :] = jnp.broadcast_to(mloc2, (1, L))
                rows_scr[d, c, 4:5, :] = jnp.broadcast_to(blast, (1, L))
                lhs.append(((vf[u] * e).astype(BF16), _split3(jnp.broadcast_to(e, (BF16_ROWS, L)))))
            res = _dot(jnp.concatenate([lhs[0][0], lhs[1][0], *lhs[0][1], *lhs[1][1]], axis=0), k[u])
            for d in range(2):
                inc_scr[d, c] = res[d * HEAD_DIM:(d + 1) * HEAD_DIM, :]
                r0 = 2 * HEAD_DIM + 3 * BF16_ROWS * d
                ninc_scr[d, c] = (res[r0:r0 + 8, :] + res[r0 + BF16_ROWS:r0 + BF16_ROWS + 8, :]
                                  + res[r0 + 2 * BF16_ROWS:r0 + 2 * BF16_ROWS + 8, :])
        for u, d in pairs:
            c = cs[u]
            ig, brow, blast = gts[u, d]
            key_term = jnp.broadcast_to(ig - brow, (L, L)).T
            causal = (row <= col) if d == 0 else (row >= col)
            dm = jnp.where(causal, key_term + brow, -jnp.inf)
            mloc = jnp.max(dm, axis=0, keepdims=True)
            s = qk[u] * (jnp.exp(dm - mloc) * k_scale)
            part_scr[d, c] = _dot(vt[u], s.astype(BF16))
            rows_scr[d, c, 0:1, :] = jnp.sum(s, axis=0, keepdims=True)
            rows_scr[d, c, 1:2, :] = mloc
        return carry

    _group_loop(ns * nc // grp, local_group)

    def m_step(d, c, m):
        blast = rows_scr[d, c, 4:5, :]
        mloc2 = rows_scr[d, c, 2:3, :]
        rows_scr[d, c, 3:4, :] = m
        m_new = jnp.maximum(blast + m, mloc2)
        decay = jnp.exp(blast + m - m_new)
        w_inc = jnp.exp(mloc2 - m_new)
        coef_scr[d, c, 0:1, :] = jnp.concatenate([decay] * (HEAD_DIM // L), axis=1)
        coef_scr[d, c, 1:2, :] = jnp.concatenate([w_inc] * (HEAD_DIM // L), axis=1)
        return m_new

    for sq in range(ns):
        c_lo = sq * nc
        m_init = []
        for d in range(2):
            if has_init:
                ct_scr[d] = c0_ref[d].T
                n_scr[d] = jnp.broadcast_to(n0_ref[d], n_scr.shape[1:])
                m_init.append(jnp.full((1, L), m0_ref[(b_idx * 2 + d) * n_heads + h_idx], F32))
            else:
                ct_scr[d] = jnp.zeros(ct_scr.shape[1:], F32)
                n_scr[d] = jnp.zeros(n_scr.shape[1:], F32)
                m_init.append(jnp.zeros((1, L), F32))

        def m_body(ci, ms, c_lo=c_lo):
            return m_step(0, c_lo + ci, ms[0]), m_step(1, c_lo + nc - 1 - ci, ms[1])

        m_fin = _scan_loop(nc, m_body, tuple(m_init))

        def body(ci, carry, c_lo=c_lo):
            for d, c in ((0, c_lo + ci), (1, c_lo + nc - 1 - ci)):
                decay = coef_scr[d, c, 0:1, :]
                w_inc = coef_scr[d, c, 1:2, :]
                n8 = n_scr[d]
                nst_scr[d, c] = n8
                for r0 in range(0, HEAD_DIM, STRIP):
                    ct = ct_scr[d, r0:r0 + STRIP, :]
                    cst_scr[d, c, r0:r0 + STRIP, :] = ct.astype(BF16)
                    ct_scr[d, r0:r0 + STRIP, :] = decay * ct + w_inc * inc_scr[d, c, r0:r0 + STRIP, :]
                n_scr[d] = decay * n8 + w_inc * ninc_scr[d, c]
            return carry

        _scan_loop(nc, body, 0)

        if write_state:
            lsel = (lambda d, sq=sq: (sq, 0, d)) if first_layer else (lambda d, sq=sq: (sq, d))
            for d in range(2):
                cout_ref[lsel(d)] = ct_scr[d].T
                nout_ref[lsel(d)] = n_scr[d][0:1, :]
                mout_ref[lsel(d)] = m_fin[d]
            if first_layer:
                for l in range(1, cout_ref.shape[1]):
                    cout_ref[sq, l] = jnp.zeros(cout_ref.shape[2:], F32)
                    nout_ref[sq, l] = jnp.zeros(nout_ref.shape[2:], F32)
                    mout_ref[sq, l] = jnp.zeros(mout_ref.shape[2:], F32)

    def finish_group(g, carry):
        cs = [g * grp + u for u in range(grp)]
        qt = [qt_ref[c] for c in cs]
        cq, nq = [], []
        for u, c in enumerate(cs):
            n16 = [jnp.concatenate([nst_scr[d, c]] * (BF16_ROWS // 8), axis=0).astype(BF16) for d in range(2)]
            res = _dot(jnp.concatenate([cst_scr[0, c], cst_scr[1, c], n16[0], n16[1]], axis=0), qt[u])
            cq.append([res[d * HEAD_DIM:(d + 1) * HEAD_DIM, :] for d in range(2)])
            nq.append([res[2 * HEAD_DIM + d * BF16_ROWS:2 * HEAD_DIM + d * BF16_ROWS + 1, :] for d in range(2)])
        for u, c in enumerate(cs):
            coef = []
            for d in range(2):
                _, brow, _ = gates(d, c)
                den_loc = rows_scr[d, c, 0:1, :]
                mloc = rows_scr[d, c, 1:2, :]
                inter = brow + rows_scr[d, c, 3:4, :]
                m_row = jnp.maximum(mloc, inter)
                w_loc = jnp.exp(mloc - m_row)
                w_inter = jnp.exp(inter - m_row)
                den = w_loc * den_loc + w_inter * nq[u][d]
                r_den = 1.0 / jnp.maximum(jnp.abs(den), jnp.exp(-m_row))
                coef.append((w_loc * r_den, w_inter * r_den))
            ssq = jnp.zeros((1, L), F32)
            for r0 in range(0, HEAD_DIM, STRIP):
                h_s = None
                for d in range(2):
                    h_d = coef[d][0] * part_scr[d, c, r0:r0 + STRIP, :] + coef[d][1] * cq[u][d][r0:r0 + STRIP, :]
                    h_s = h_d if h_s is None else h_s + h_d
                part_scr[0, c, r0:r0 + STRIP, :] = h_s
                ssq = ssq + jnp.sum(h_s * h_s, axis=0, keepdims=True)
            r_norm = lax.rsqrt(ssq * (1.0 / HEAD_DIM) + EPS)
            for r0 in range(0, HEAD_DIM, L):
                y = part_scr[0, c, r0:r0 + L, :] * r_norm * nw_ref[r0:r0 + L, :]
                o = jax.nn.sigmoid(ot_ref[c, r0:r0 + L, :].astype(F32)) * y
                hm_ref[pl.ds(_chunk_off(c), L), r0:r0 + L] = o.T.astype(hm_ref.dtype)
        return carry

    _group_loop(ns * nc // grp, finish_group)


def _state_specs(shapes, layer, first_layer, ns):
    specs = []
    for shp in shapes:
        tail = shp[4:]
        zeros = (0,) * len(tail)
        if first_layer:
            specs.append(pl.BlockSpec((ns, shp[1], 2, None) + tail, lambda b, h, z=zeros: (b, 0, 0, h) + z))
        else:
            specs.append(pl.BlockSpec((ns, None, 2, None) + tail, lambda b, h, z=zeros: (b, layer, 0, h) + z))
    return specs


def _mlstm_call(proj_t, proj_n, gp, nw_b, init, state_bufs, *, n_seq, seq_len, tok_off, n_heads, layer, depth):
    dk = HEAD_DIM
    nc = seq_len // CHUNK
    has_init = init is not None
    ns = 1 if has_init else _seqs_per_step(n_seq, nc)
    nct = ns * nc
    boff = tok_off // (ns * seq_len)
    write_state = not has_init
    first_layer = layer == 0
    args = [proj_t, proj_t, proj_t, proj_n, gp, nw_b]
    specs = [
        pl.BlockSpec((nct, dk, CHUNK), lambda b, h: (b + boff, h, 0)),
        pl.BlockSpec((nct, dk, CHUNK), lambda b, h: (b + boff, n_heads + h, 0)),
        pl.BlockSpec((nct, dk, CHUNK), lambda b, h: (b + boff, 2 * n_heads + h, 0)),
        pl.BlockSpec((ns * seq_len, dk), lambda b, h: (b + boff, h)),
        pl.BlockSpec((None, nct, 4, CHUNK), lambda b, h: (h, b + boff, 0, 0)),
        pl.BlockSpec((dk, LANES), lambda b, h: (h, 0)),
    ]
    aliases = {}
    if has_init:
        c0, n0, m0 = init
        args += [c0, n0[:, layer][:, :, :, None, :], m0[:, layer].reshape(-1)]
        specs += [
            pl.BlockSpec((None, None, 2, None, dk, dk), lambda b, h: (b, layer, 0, h, 0, 0)),
            pl.BlockSpec((None, 2, None, 1, dk), lambda b, h: (b, 0, h, 0, 0)),
            pl.BlockSpec(memory_space=pltpu.SMEM),
        ]
    out_shape = [jax.ShapeDtypeStruct((n_seq * seq_len, n_heads * dk), BF16)]
    out_specs = [pl.BlockSpec((ns * seq_len, dk), lambda b, h: (b, h))]
    if write_state:
        shapes = [(n_seq, depth, 2, n_heads, dk, dk), (n_seq, depth, 2, n_heads, 1, dk),
                  (n_seq, depth, 2, n_heads, 1, LANES)]
        if not first_layer:
            for k_, buf in enumerate(state_bufs):
                aliases[len(args)] = 1 + k_
                args.append(buf)
                specs.append(pl.BlockSpec(memory_space=pl.ANY))
        out_shape += [jax.ShapeDtypeStruct(s, F32) for s in shapes]
        out_specs += _state_specs(shapes, layer, first_layer, ns)
    kern = functools.partial(_mlstm_kernel, nc=nc, ns=ns, has_init=has_init, first_layer=first_layer,
                             write_state=write_state)
    return pl.pallas_call(
        kern, grid=(n_seq // ns, n_heads), in_specs=specs, out_specs=out_specs, out_shape=out_shape,
        input_output_aliases=aliases,
        scratch_shapes=[pltpu.VMEM((2, dk, dk), F32), pltpu.VMEM((2, 8, dk), F32),
                        pltpu.VMEM((2, nct, dk, CHUNK), F32), pltpu.VMEM((2, nct, dk, dk), F32),
                        pltpu.VMEM((2, nct, 8, dk), F32), pltpu.VMEM((2, nct, 8, CHUNK), F32),
                        pltpu.VMEM((2, nct, dk, dk), BF16), pltpu.VMEM((2, nct, 8, dk), F32),
                        pltpu.VMEM((2, nct, 8, dk), F32)],
        compiler_params=_cparams(("parallel", "parallel"), 48), name="mlstm_mixer",
    )(*args)


def _ret_kernel(*refs, nc, ns, has_init, first_layer, write_state):
    it = iter(refs)
    dl_ref, qt_ref, vt_ref, gt_ref, k_ref, nw_ref = (next(it) for _ in range(6))
    if has_init:
        s0_ref = next(it)
    if write_state and not first_layer:
        next(it)
    hr_ref = next(it)
    if write_state:
        sout_ref = next(it)
    st_scr, part_scr, inc_scr, sst_scr = next(it), next(it), next(it), next(it)

    L = CHUNK
    grp = min(MAX_GROUP, ns * nc)
    rowi = _iota((L, L), 0)
    coli = _iota((L, L), 1)
    k_scale = HEAD_DIM ** -0.5
    h_idx = pl.program_id(1)
    n_heads = pl.num_programs(1)

    consts = []
    for d in range(2):
        lg = _log_sigmoid(jnp.full((1, 1), dl_ref[d * n_heads + h_idx], F32))
        rel = (coli - rowi if d == 0 else rowi - coli).astype(F32)
        dmat_t = jnp.where(rel >= 0, jnp.exp(jnp.maximum(rel, 0.0) * lg), 0.0) * k_scale
        pos = _iota((1, L), 1).astype(F32)
        if d == 1:
            pos = (L - 1.0) - pos
        q_decay = jnp.exp((pos + 1.0) * lg)
        k_decay = jnp.exp((L - 1.0 - pos) * lg) * k_scale
        chunk_decay = jnp.exp(float(L) * lg)
        consts.append((dmat_t, q_decay, k_decay, chunk_decay))
    dmat_both = consts[0][0] + consts[1][0]

    def local_group(g, carry):
        cs = [g * grp + u for u in range(grp)]
        qt = [qt_ref[c] for c in cs]
        vt = [vt_ref[c] for c in cs]
        k = [k_ref[pl.ds(_chunk_off(c), L), :] for c in cs]
        qk = [_dot(k[u], qt[u]) for u in range(grp)]
        for u, c in enumerate(cs):
            vf = vt[u].astype(F32)
            res = _dot(jnp.concatenate([(vf * consts[d][2]).astype(BF16) for d in range(2)], axis=0), k[u])
            for d in range(2):
                inc_scr[d, c] = res[d * HEAD_DIM:(d + 1) * HEAD_DIM, :]
        for u, c in enumerate(cs):
            part_scr[c] = _dot(vt[u], (qk[u] * dmat_both).astype(BF16))
        return carry

    _group_loop(ns * nc // grp, local_group)

    for sq in range(ns):
        c_lo = sq * nc
        for d in range(2):
            if has_init:
                st_scr[d] = s0_ref[d].T
            else:
                st_scr[d] = jnp.zeros(st_scr.shape[1:], F32)

        def body(ci, carry, c_lo=c_lo):
            for d, c in ((0, c_lo + ci), (1, c_lo + nc - 1 - ci)):
                for r0 in range(0, HEAD_DIM, STRIP):
                    st = st_scr[d, r0:r0 + STRIP, :]
                    sst_scr[d, c, r0:r0 + STRIP, :] = st.astype(BF16)
                    st_scr[d, r0:r0 + STRIP, :] = consts[d][3] * st + inc_scr[d, c, r0:r0 + STRIP, :]
            return carry

        _scan_loop(nc, body, 0)

        if write_state:
            for d in range(2):
                if first_layer:
                    sout_ref[sq, 0, d] = st_scr[d].T
                else:
                    sout_ref[sq, d] = st_scr[d].T
            if first_layer:
                for l in range(1, sout_ref.shape[1]):
                    sout_ref[sq, l] = jnp.zeros(sout_ref.shape[2:], F32)

    def finish_group(g, carry):
        cs = [g * grp + u for u in range(grp)]
        qt = [qt_ref[c] for c in cs]
        sq = [_dot(jnp.concatenate([sst_scr[0, c], sst_scr[1, c]], axis=0), qt[u]) for u, c in enumerate(cs)]
        for u, c in enumerate(cs):
            ssq = jnp.zeros((1, L), F32)
            for r0 in range(0, HEAD_DIM, STRIP):
                sl = slice(r0, r0 + STRIP)
                o_s = part_scr[c, sl, :] + (consts[0][1] * sq[u][r0:r0 + STRIP, :]
                                            + consts[1][1] * sq[u][HEAD_DIM + r0:HEAD_DIM + r0 + STRIP, :])
                part_scr[c, sl, :] = o_s
                ssq = ssq + jnp.sum(o_s * o_s, axis=0, keepdims=True)
            r_norm = lax.rsqrt(ssq * (1.0 / HEAD_DIM) + EPS)
            for r0 in range(0, HEAD_DIM, L):
                y = part_scr[c, r0:r0 + L, :] * r_norm * nw_ref[r0:r0 + L, :]
                rg = gt_ref[c, r0:r0 + L, :].astype(F32)
                hr_ref[pl.ds(_chunk_off(c), L), r0:r0 + L] = (rg * jax.nn.sigmoid(rg) * y).T.astype(hr_ref.dtype)
        return carry

    _group_loop(ns * nc // grp, finish_group)


def _ret_call(proj_t, proj_n, decay_logit, nw_b, init, state_buf, *, n_seq, seq_len, tok_off, n_heads, layer,
              depth):
    dk = HEAD_DIM
    nc = seq_len // CHUNK
    has_init = init is not None
    ns = 1 if has_init else _seqs_per_step(n_seq, nc)
    nct = ns * nc
    boff = tok_off // (ns * seq_len)
    write_state = not has_init
    first_layer = layer == 0
    args = [decay_logit.reshape(-1), proj_t, proj_t, proj_t, proj_n, nw_b]
    specs = [
        pl.BlockSpec(memory_space=pltpu.SMEM),
        pl.BlockSpec((nct, dk, CHUNK), lambda b, h: (b + boff, 3 * n_heads + h, 0)),
        pl.BlockSpec((nct, dk, CHUNK), lambda b, h: (b + boff, 4 * n_heads + h, 0)),
        pl.BlockSpec((nct, dk, CHUNK), lambda b, h: (b + boff, 5 * n_heads + h, 0)),
        pl.BlockSpec((ns * seq_len, dk), lambda b, h: (b + boff, n_heads + h)),
        pl.BlockSpec((dk, LANES), lambda b, h: (h, 0)),
    ]
    aliases = {}
    if has_init:
        args.append(init)
        specs.append(pl.BlockSpec((None, None, 2, None, dk, dk), lambda b, h: (b, layer, 0, h, 0, 0)))
    out_shape = [jax.ShapeDtypeStruct((n_seq * seq_len, n_heads * dk), BF16)]
    out_specs = [pl.BlockSpec((ns * seq_len, dk), lambda b, h: (b, h))]
    if write_state:
        shapes = [(n_seq, depth, 2, n_heads, dk, dk)]
        if not first_layer:
            aliases[len(args)] = 1
            args.append(state_buf)
            specs.append(pl.BlockSpec(memory_space=pl.ANY))
        out_shape += [jax.ShapeDtypeStruct(s, F32) for s in shapes]
        out_specs += _state_specs(shapes, layer, first_layer, ns)
    kern = functools.partial(_ret_kernel, nc=nc, ns=ns, has_init=has_init, first_layer=first_layer,
                             write_state=write_state)
    return pl.pallas_call(
        kern, grid=(n_seq // ns, n_heads), in_specs=specs, out_specs=out_specs, out_shape=out_shape,
        input_output_aliases=aliases,
        scratch_shapes=[pltpu.VMEM((2, dk, dk), F32), pltpu.VMEM((nct, dk, CHUNK), F32),
                        pltpu.VMEM((2, nct, dk, dk), F32), pltpu.VMEM((2, nct, dk, dk), BF16)],
        compiler_params=_cparams(("parallel", "parallel"), 48), name="retention_mixer",
    )(*args)


def _merge_kernel(hmp_ref, hms_ref, hrp_ref, hrs_ref, ga_ref, gb_ref, x_ref, mod_ref, nw_ref, wa_ref, wb_ref,
                  wo_ref, wr_ref, x1_ref, h2_ref, aff_ref, wab_scr, wbb_scr, wob_scr, *, n_prompt_tiles):
    i = pl.program_id(0)

    @pl.when(i == 0)
    def _():
        wab_scr[...] = wa_ref[...].astype(BF16)
        wbb_scr[...] = wb_ref[...].astype(BF16)
        wob_scr[...] = wo_ref[...].astype(BF16)

    is_prompt = i < n_prompt_tiles
    hm = jnp.where(is_prompt, hmp_ref[...], hms_ref[...])
    hr = jnp.where(is_prompt, hrp_ref[...], hrs_ref[...])
    ya = _dot(hm, wab_scr[...])
    yb = _dot(hr, wbb_scr[...])
    merged = (jax.nn.sigmoid(ga_ref[...].astype(F32)) * ya + jax.nn.sigmoid(gb_ref[...].astype(F32)) * yb)
    y = _dot(merged.astype(BF16), wob_scr[...])
    x1 = x_ref[...] + mod_ref[2:3, :] * y
    x1_ref[...] = x1
    h2 = x1 * lax.rsqrt(jnp.mean(x1 * x1, axis=-1, keepdims=True) + EPS) * nw_ref[...]
    h2 = h2 * (1.0 + mod_ref[4:5, :]) + mod_ref[3:4, :]
    n_blk = h2.shape[1] // LANES
    for fb in range(n_blk):
        _tile_rows(h2_ref, fb, h2.shape[0], n_blk)[...] = h2[:, fb * LANES:(fb + 1) * LANES]
    logits = _dot_nt(wr_ref[...].astype(BF16), h2.astype(BF16))
    p = jnp.exp(logits - jnp.max(logits, axis=0, keepdims=True))
    aff_ref[...] = p / jnp.sum(p, axis=0, keepdims=True)


def _merge_call(hm_p, hm_s, hr_p, hr_s, proj_n, x, mod, norm_w, w_a, w_b, w_out, router_wt, layer, *,
                n_prompt, dec_seq, tm=512):
    n, d = x.shape
    ne = router_wt.shape[1]
    npt = n_prompt // tm
    grp = functools.partial(_group_of_tile, tm=tm, n_prompt=n_prompt, dec_seq=dec_seq)
    tile = pl.BlockSpec((tm, d), lambda i: (i, 0))
    tile_p = pl.BlockSpec((tm, d), lambda i: (jnp.minimum(i, npt - 1), 0))
    tile_s = pl.BlockSpec((tm, d), lambda i: (jnp.maximum(i - npt, 0), 0))
    full = pl.BlockSpec((None, d, d), lambda i: (layer, 0, 0))
    return pl.pallas_call(
        functools.partial(_merge_kernel, n_prompt_tiles=npt), grid=(n // tm,),
        in_specs=[
            tile_p, tile_s, tile_p, tile_s,
            pl.BlockSpec((tm, d), lambda i: (i, 2)),
            pl.BlockSpec((tm, d), lambda i: (i, 3)),
            tile,
            pl.BlockSpec((None, 6, d), lambda i: (grp(i), 0, 0)),
            pl.BlockSpec((1, d), lambda i: (0, 0)),
            full, full, full,
            pl.BlockSpec((None, ne, d), lambda i: (layer, 0, 0)),
        ],
        out_specs=[tile, pl.BlockSpec((tm * (d // LANES), LANES), lambda i: (i, 0)),
                   pl.BlockSpec((ne, tm), lambda i: (0, i))],
        out_shape=[jax.ShapeDtypeStruct((n, d), F32), jax.ShapeDtypeStruct((n * (d // LANES), LANES), F32),
                   jax.ShapeDtypeStruct((ne, n), F32)],
        scratch_shapes=[pltpu.VMEM((d, d), BF16)] * 3,
        compiler_params=_cparams(("arbitrary",), 56), name="merge_out_router",
    )(hm_p, hm_s, hr_p, hr_s, proj_n, proj_n, x, mod, norm_w.reshape(1, d), w_a, w_b, w_out, router_wt)


ROUTE_GROUP = 4


def _route_kernel(a_ref, at_ref, idx_ref, gate_ref, thr_scr, *, cap, n_tok):
    n_sets = a_ref.shape[0]
    nb = n_tok // LANES
    a_all = a_ref[...]

    def as_f32(bits):
        return lax.bitcast_convert_type(bits, F32)

    def count_ge(cand):
        m = jnp.where(a_all >= cand, 1.0, 0.0)
        return jnp.sum(jnp.sum(m, axis=1, keepdims=True), axis=2, keepdims=True)

    def bit_step(k, thr):
        cand = thr | lax.shift_left(jnp.int32(1), 30 - k)
        return jnp.where(count_ge(as_f32(cand)) >= cap, cand, thr)

    thr_scr[...] = lax.fori_loop(0, 31, bit_step, jnp.zeros((n_sets, 1, 1), jnp.int32))

    r128 = _iota((LANES, LANES), 0)
    c128 = _iota((LANES, LANES), 1)
    upper = jnp.where(r128 <= c128, 1.0, 0.0).astype(BF16)
    lower_t = jnp.where(r128 >= c128, 1.0, 0.0).astype(BF16)
    rb = _iota((nb, nb), 0)
    cb = _iota((nb, nb), 1)
    blk_before_rows = jnp.where(cb < rb, 1.0, 0.0).astype(BF16)
    blk_before_cols = jnp.where(rb < cb, 1.0, 0.0).astype(BF16)

    def incl_counts(mask):
        within = _dot(mask.astype(BF16), upper)
        before = _dot(blk_before_rows, within.astype(BF16))[:, LANES - 1:LANES]
        return within + before

    def incl_counts_t(mask_t):
        within = _dot(lower_t, mask_t.astype(BF16))
        before = _dot(within.astype(BF16), blk_before_cols)[LANES - 1:LANES, :]
        return within + before

    slot = _iota((1, cap), 1).astype(F32)
    blk_col = _iota((nb, 1), 0).astype(F32)
    sub_col = _iota((LANES, 1), 0).astype(F32)

    def per_group(g, carry):
        sets = [g + u * (n_sets // ROUTE_GROUP) for u in range(ROUTE_GROUP)]
        stage = []
        for s in sets:
            thr_bits = thr_scr[s]
            thr = as_f32(thr_bits)
            nxt = as_f32(thr_bits + 1)
            a = a_ref[s]
            a_t = at_ref[s]
            gt = jnp.where(a >= nxt, 1.0, 0.0)
            eq = jnp.where((a >= thr) & (a < nxt), 1.0, 0.0)
            gt_t = jnp.where(a_t >= nxt, 1.0, 0.0)
            eq_t = jnp.where((a_t >= thr) & (a_t < nxt), 1.0, 0.0)
            n_gt = jnp.sum(jnp.sum(gt, axis=1, keepdims=True), axis=0, keepdims=True)
            stage.append((a_t, gt, eq, gt_t, eq_t, cap - n_gt))
        ties = [(incl_counts(eq), incl_counts_t(eq_t)) for _, _, eq, _, eq_t, _ in stage]
        sels = []
        for (a_t, gt, eq, gt_t, eq_t, need), (c_eq, c_eq_t) in zip(stage, ties):
            sels.append((gt + eq * jnp.where(c_eq - eq < need, 1.0, 0.0),
                         gt_t + eq_t * jnp.where(c_eq_t - eq_t < need, 1.0, 0.0)))
        cnts = [(incl_counts(sel), incl_counts_t(sel_t)) for sel, sel_t in sels]
        picks = []
        for (a_t, *_), (cnt, cnt_t) in zip(stage, cnts):
            blk_end = cnt[:, LANES - 1:LANES]
            blk_of_slot = jnp.sum(jnp.where(blk_end <= slot, 1.0, 0.0), axis=0, keepdims=True)
            onehot_blk = jnp.where(blk_col == blk_of_slot, 1.0, 0.0).astype(BF16)
            cnt_rows = _dot3(cnt_t, onehot_blk)
            a_rows = _dot3(a_t, onehot_blk)
            picks.append((blk_of_slot, cnt_rows, a_rows))
        for s, (blk_of_slot, cnt_rows, a_rows) in zip(sets, picks):
            sub_of_slot = jnp.sum(jnp.where(cnt_rows <= slot, 1.0, 0.0), axis=0, keepdims=True)
            gate = jnp.sum(jnp.where(sub_col == sub_of_slot, a_rows, 0.0), axis=0, keepdims=True)
            idx_ref[s] = (blk_of_slot * LANES + sub_of_slot).astype(jnp.int32)
            gate_ref[s] = gate
        return carry

    lax.fori_loop(0, n_sets // ROUTE_GROUP, per_group, 0)


def _route_call(aff_t, n_pass, cap):
    ne, n = aff_t.shape
    n_tok = n // n_pass
    nb = n_tok // LANES
    a4 = aff_t.reshape(ne, n_pass, nb, LANES).transpose(1, 0, 2, 3).reshape(n_pass * ne, nb, LANES)
    a4_t = a4.transpose(0, 2, 1)
    n_sets = n_pass * ne
    return pl.pallas_call(
        functools.partial(_route_kernel, cap=cap, n_tok=n_tok),
        out_shape=[jax.ShapeDtypeStruct((n_sets, 1, cap), jnp.int32),
                   jax.ShapeDtypeStruct((n_sets, 1, cap), F32)],
        scratch_shapes=[pltpu.VMEM((n_sets, 1, 1), jnp.int32)],
        compiler_params=pltpu.CompilerParams(vmem_limit_bytes=32 * MIB), name="expert_choice_route",
    )(a4, a4_t)


ROW_LOOP_UNROLL = 8


def _moe_kernel(idx_ref, gate_ref, h_hbm, w1_ref, w3_ref, w2_ref, out_hbm,
                xe_scr, xb_scr, ye_a, ye_b, acc_scr, gsem, osem, *, rows, rows_pad):
    e = pl.program_id(0)
    f = pl.program_id(1)
    n_e = pl.num_programs(0)
    n_f = pl.num_programs(1)
    chunk = rows_pad // n_f

    n_blk = xb_scr.shape[1] // LANES

    def tile_of(t):
        return pl.ds(pl.multiple_of(t * n_blk, n_blk), n_blk)

    def gather_row(slot, s):
        tok = idx_ref[slot * rows_pad + s]
        pltpu.make_async_copy(h_hbm.at[tile_of(tok), :], xe_scr.at[tile_of(s), :], gsem).start()

    def scatter_row(slot, s, ye_ref):
        tok = idx_ref[slot * rows_pad + s]
        g = gate_ref[slot * rows_pad + s]
        acc_scr[pl.ds(tok, 1), :] += ye_ref[pl.ds(s, 1), :] * g

    def wait_gather():
        pltpu.make_async_copy(h_hbm.at[pl.ds(0, rows_pad * n_blk), :], xe_scr, gsem).wait()

    @pl.when((e == 0) & (f == 0))
    def _():
        acc_scr[...] = jnp.zeros_like(acc_scr)
        ye_a[...] = jnp.zeros_like(ye_a)
        ye_b[...] = jnp.zeros_like(ye_b)

        def issue(s, carry):
            gather_row(1, s)
            return carry
        lax.fori_loop(0, rows_pad, issue, 0, unroll=ROW_LOOP_UNROLL)

    def step(ye_cur, ye_prev):
        @pl.when(f == 0)
        def _():
            wait_gather()
            for fb in range(n_blk):
                xb_scr[:, fb * LANES:(fb + 1) * LANES] = _tile_rows(xe_scr, fb, rows, n_blk)[...].astype(BF16)
            ye_cur[0:rows, :] = jnp.zeros((rows, ye_cur.shape[1]), F32)

        xb = xb_scr[...]
        h1 = _dot(xb, w1_ref[...].astype(BF16))
        h3 = _dot(xb, w3_ref[...].astype(BF16))
        he = (h1 * jax.nn.sigmoid(h1) * h3).astype(BF16)
        ye_cur[0:rows, :] += _dot(he, w2_ref[...].astype(BF16))
        base = f * chunk
        for r in range(chunk):
            gather_row(e + 2, base + r)
        for r in range(chunk):
            scatter_row(e, base + r, ye_prev)

        @pl.when((e == n_e - 1) & (f == n_f - 1))
        def _():
            def scatter(s, carry):
                scatter_row(n_e, s, ye_cur)
                return carry
            lax.fori_loop(0, rows, scatter, 0, unroll=ROW_LOOP_UNROLL)
            wait_gather()
            cp = pltpu.make_async_copy(acc_scr, out_hbm, osem)
            cp.start()
            cp.wait()

    @pl.when(e % 2 == 0)
    def _():
        step(ye_a, ye_b)

    @pl.when(e % 2 == 1)
    def _():
        step(ye_b, ye_a)


def _moe_call(idx, gate, h2, w1, w3, w2, layer, tf=256):
    d = w1.shape[2]
    n_blk = d // LANES
    n = h2.shape[0] // n_blk
    ne, rows = idx.shape
    dff = w1.shape[3]
    nf = dff // tf
    rows_pad = -(-rows // (8 * nf)) * 8 * nf
    pad = ((1, 1), (0, rows_pad - rows))
    idx_all = jnp.pad(idx, pad).reshape(-1)
    gate_all = jnp.pad(gate, pad).reshape(-1)
    grid_spec = pltpu.PrefetchScalarGridSpec(
        num_scalar_prefetch=1,
        grid=(ne, nf),
        in_specs=[
            pl.BlockSpec(memory_space=pltpu.SMEM),
            pl.BlockSpec(memory_space=pl.ANY),
            pl.BlockSpec((None, None, d, tf), lambda e, f, idx: (layer, e, 0, f)),
            pl.BlockSpec((None, None, d, tf), lambda e, f, idx: (layer, e, 0, f)),
            pl.BlockSpec((None, None, tf, d), lambda e, f, idx: (layer, e, f, 0)),
        ],
        out_specs=pl.BlockSpec(memory_space=pl.ANY),
        scratch_shapes=[
            pltpu.VMEM((rows_pad * n_blk, LANES), F32), pltpu.VMEM((rows, d), BF16),
            pltpu.VMEM((rows_pad, d), F32), pltpu.VMEM((rows_pad, d), F32),
            pltpu.VMEM((n, d), F32), pltpu.SemaphoreType.DMA, pltpu.SemaphoreType.DMA,
        ],
    )
    return pl.pallas_call(
        functools.partial(_moe_kernel, rows=rows, rows_pad=rows_pad), grid_spec=grid_spec,
        out_shape=jax.ShapeDtypeStruct((n, d), F32),
        compiler_params=_cparams(("arbitrary", "arbitrary"), 62), name="expert_ffn",
    )(idx_all, gate_all, h2, w1, w3, w2)


def _rope_tables(t):
    rows = t // GRID_W
    row = jnp.repeat(jnp.arange(rows, dtype=F32), GRID_W)
    colp = jnp.tile(jnp.arange(GRID_W, dtype=F32), rows)
    n_freq = HEAD_DIM // 4
    inv = ROPE_BASE ** (-jnp.arange(n_freq, dtype=F32) / n_freq)
    ang = jnp.concatenate([row[:, None] * inv, colp[:, None] * inv], axis=-1)
    return jnp.cos(ang), jnp.sin(ang)


def kernel(x_prompt, x_sample, c, state_mlstm_C, state_mlstm_n, state_mlstm_m, state_ret_S, c_ctx, w_mod, b_mod,
           norm1_w, norm2_w, w_in, mlstm_if_b, mlstm_norm_w, ret_decay_logit, ret_norm_w, w_branch_a, w_branch_b,
           w_out, router_w, ffn_w1, ffn_w3, ffn_w2, final_norm_w):
    bp, seq, d = x_prompt.shape
    db, dec_seq, _ = x_sample.shape
    depth = w_mod.shape[0]
    hm = mlstm_if_b.shape[-1]
    hr = ret_decay_logit.shape[-1]
    ne = router_w.shape[-1]
    n_prompt = bp * seq
    n_sample = db * dec_seq
    assert n_prompt == n_sample and hm * HEAD_DIM == d and hr * HEAD_DIM == d and hm == hr
    cap = EC_FACTOR * n_prompt // ne
    tm = 512
    tp = 2048
    geo = dict(n_prompt=n_prompt, dec_seq=dec_seq)

    cond8 = jnp.concatenate([c_ctx[None, :], c, jnp.zeros((8 - 1 - db, d), F32)], axis=0)
    mod = _mod_call(cond8, w_mod, b_mod)[:, :1 + db].reshape(depth, 1 + db, 6, d)

    n_m = 4 * d
    g0 = n_m + 4 * hm

    w_t = jnp.swapaxes(w_in, 1, 2)
    mq, mk, mv, mo = (k * d for k in range(4))
    rq, rk, rv, rg, ga, gb = (g0 + k * d for k in range(6))
    wg_t = w_t[:, n_m:g0]
    gate_bias = jnp.transpose(mlstm_if_b, (0, 2, 1, 3)).reshape(depth, 4 * hm, 1)
    router_wt = jnp.swapaxes(router_w, 1, 2)
    nw_m = jnp.broadcast_to(mlstm_norm_w[:, :, None], (depth, d, LANES))
    nw_r = jnp.broadcast_to(ret_norm_w[:, :, None], (depth, d, LANES))

    cos, sin = _rope_tables(dec_seq)
    cos_t = cos.T.reshape(HEAD_DIM // 2, dec_seq // tp, tp).transpose(1, 0, 2)
    sin_t = sin.T.reshape(HEAD_DIM // 2, dec_seq // tp, tp).transpose(1, 0, 2)

    x, h = _resid_norm_call(x_prompt.reshape(n_prompt, d), None, None, mod[0], norm1_w[0], gate_row=0,
                            mod_rows=(0, 1), h_dtype=BF16, write_x=True, x_tail=x_sample.reshape(n_sample, d),
                            **geo)
    m_bufs, s_buf = None, None
    y_prompt = y_sample = None
    for l in range(depth):
        proj_t = _proj_t_call(h, w_t, l, (mq, mv, mo, rq, rv, rg), (cos_t, sin_t), 3, tm=tp, **geo)
        proj_n = _proj_call(h, w_t, l, (mk, rk, ga, gb), (cos, sin), 1, tm=tp, **geo)
        gp = _gate_call(h, wg_t[l], gate_bias[l], hm, tm=2 * tm)

        mix = dict(n_heads=hm, layer=l, depth=depth)
        hm_p, *m_bufs = _mlstm_call(proj_t, proj_n, gp, nw_m[l], None, m_bufs, n_seq=bp, seq_len=seq, tok_off=0,
                                    **mix)
        (hm_s,) = _mlstm_call(proj_t, proj_n, gp, nw_m[l], (state_mlstm_C, state_mlstm_n, state_mlstm_m), None,
                              n_seq=db, seq_len=dec_seq, tok_off=n_prompt, **mix)
        hr_p, s_buf = _ret_call(proj_t, proj_n, ret_decay_logit[l], nw_r[l], None, s_buf, n_seq=bp, seq_len=seq,
                                tok_off=0, **mix)
        (hr_s,) = _ret_call(proj_t, proj_n, ret_decay_logit[l], nw_r[l], state_ret_S, None, n_seq=db,
                            seq_len=dec_seq, tok_off=n_prompt, **mix)

        x1, h2, aff_t = _merge_call(hm_p, hm_s, hr_p, hr_s, proj_n, x, mod[l], norm2_w[l], w_branch_a, w_branch_b,
                                    w_out, router_wt, l, tm=tm, **geo)
        idx, gate = _route_call(aff_t, 2, cap)
        idx = idx.reshape(2, ne, cap) + (jnp.arange(2, dtype=jnp.int32) * n_prompt)[:, None, None]
        idx = jnp.swapaxes(idx, 0, 1).reshape(ne, 2 * cap)
        gate = jnp.swapaxes(gate.reshape(2, ne, cap), 0, 1).reshape(ne, 2 * cap)
        moe = _moe_call(idx, gate, h2, ffn_w1, ffn_w3, ffn_w2, l)
        if l + 1 < depth:
            x, h = _resid_norm_call(x1, moe, mod[l], mod[l + 1], norm1_w[l + 1], gate_row=5, mod_rows=(0, 1),
                                    h_dtype=BF16, write_x=True, **geo)
        else:
            fin = dict(gate_row=5, mod_rows=None, h_dtype=F32, write_x=False, **geo)
            _, y_prompt = _resid_norm_call(x1, moe, mod[l], None, final_norm_w, row_off=0, n_rows=n_prompt, **fin)
            _, y_sample = _resid_norm_call(x1, moe, mod[l], None, final_norm_w, row_off=n_prompt, n_rows=n_sample,
                                           **fin)

    c_buf, n_buf, m_buf = m_bufs
    return (y_prompt.reshape(bp, seq, d), y_sample.reshape(db, dec_seq, d), c_buf, n_buf[:, :, :, :, 0, :],
            m_buf[:, :, :, :, 0, 0], s_buf)
```

```python
import functools

import jax
import jax.numpy as jnp
from jax import lax
from jax.experimental import pallas as pl
from jax.experimental.pallas import tpu as pltpu

F32 = jnp.float32
BF16 = jnp.bfloat16

GRID_W = 64
CHUNK = 128
LANES = 128
HEAD_DIM = 256
EC_FACTOR = 2
ROPE_BASE = 10000.0
EPS = 1e-6
V7X_VMEM_BYTES = 64 * 1024 * 1024
MIB = 1024 * 1024


def _cparams(semantics, vmem_mib):
    assert vmem_mib * MIB < V7X_VMEM_BYTES
    return pltpu.CompilerParams(dimension_semantics=semantics, vmem_limit_bytes=vmem_mib * MIB)


def _dot(a, b):
    return jnp.dot(a, b, preferred_element_type=F32)


def _dot_nt(a, b):
    return lax.dot_general(a, b, (((1,), (1,)), ((), ())), preferred_element_type=F32)


def _log_sigmoid(x):
    return -(jnp.maximum(-x, 0.0) + jnp.log1p(jnp.exp(-jnp.abs(x))))


def _split3(x):
    hi = x.astype(BF16)
    r1 = x - hi.astype(F32)
    mid = r1.astype(BF16)
    lo = (r1 - mid.astype(F32)).astype(BF16)
    return hi, mid, lo


def _dot3(x, m):
    hi, mid, lo = _split3(x)
    return _dot(hi, m) + _dot(mid, m) + _dot(lo, m)


def _iota(shape, dim):
    return lax.broadcasted_iota(jnp.int32, shape, dim)


def _tile_rows(ref, fb, n_tok, n_blk):
    return ref.at[pl.ds(fb, n_tok, stride=n_blk), :]


def _mod_kernel(cond_ref, w_ref, b_ref, out_ref):
    c = cond_ref[...]
    s = (c * jax.nn.sigmoid(c)).astype(BF16)
    out_ref[...] = _dot(s, w_ref[...].astype(BF16)) + b_ref[...]


def _mod_call(cond8, w_mod, b_mod):
    depth, d, w6 = w_mod.shape
    tn = 1536
    return pl.pallas_call(
        _mod_kernel,
        grid=(depth, w6 // tn),
        in_specs=[
            pl.BlockSpec((8, d), lambda l, j: (0, 0)),
            pl.BlockSpec((None, d, tn), lambda l, j: (l, 0, j)),
            pl.BlockSpec((None, 1, tn), lambda l, j: (l, 0, j)),
        ],
        out_specs=pl.BlockSpec((None, 8, tn), lambda l, j: (l, 0, j)),
        out_shape=jax.ShapeDtypeStruct((depth, 8, w6), F32),
        compiler_params=_cparams(("parallel", "parallel"), 32),
        name="adaln_mod",
    )(cond8, w_mod, b_mod.reshape(depth, 1, w6))


def _group_of_tile(i, tm, n_prompt, dec_seq):
    return jnp.maximum(i * tm - n_prompt + dec_seq, 0) // dec_seq


def _resid_norm_kernel(*refs, has_delta, has_mod, gate_row, mod_rows, write_x, n_first_tiles):
    it = iter(refs)
    x_ref = next(it)
    x2_ref = next(it) if n_first_tiles else None
    delta_ref = next(it) if has_delta else None
    mod_ref = next(it) if (has_delta or has_mod) else None
    nw_ref = next(it)
    xo_ref = next(it) if write_x else None
    h_ref = next(it)
    x = x_ref[...]
    if n_first_tiles:
        x = jnp.where(pl.program_id(0) < n_first_tiles, x, x2_ref[...])
    if has_delta:
        x = x + mod_ref[gate_row:gate_row + 1, :] * delta_ref[...]
    if write_x:
        xo_ref[...] = x
    y = x * lax.rsqrt(jnp.mean(x * x, axis=-1, keepdims=True) + EPS) * nw_ref[...]
    if has_mod:
        sh_row, sc_row = mod_rows
        y = y * (1.0 + mod_ref[sc_row:sc_row + 1, :]) + mod_ref[sh_row:sh_row + 1, :]
    h_ref[...] = y.astype(h_ref.dtype)


def _resid_norm_call(x, delta, mod_gate, mod_norm, norm_w, *, gate_row, mod_rows, h_dtype, write_x,
                     n_prompt, dec_seq, tm=1024, row_off=0, n_rows=None, x_tail=None):
    n, d = x.shape
    has_delta = delta is not None
    has_mod = mod_norm is not None
    grp = functools.partial(_group_of_tile, tm=tm, n_prompt=n_prompt, dec_seq=dec_seq)
    if x_tail is None:
        n_rows = n if n_rows is None else n_rows
        toff = row_off // tm
        nft = 0
        tile_in = pl.BlockSpec((tm, d), lambda i: (i + toff, 0))
        args, specs = [x], [tile_in]
    else:
        assert row_off == 0 and n_rows is None and not has_delta
        n_rows, toff, nft = n + x_tail.shape[0], 0, n // tm
        args = [x, x_tail]
        specs = [pl.BlockSpec((tm, d), lambda i: (jnp.minimum(i, nft - 1), 0)),
                 pl.BlockSpec((tm, d), lambda i: (jnp.maximum(i - nft, 0), 0))]
    tile_out = pl.BlockSpec((tm, d), lambda i: (i, 0))
    if has_delta:
        args.append(delta)
        specs.append(tile_in)
    if has_delta or has_mod:
        mg = mod_gate if has_delta else mod_norm
        mn = mod_norm if has_mod else mod_gate
        args.append(jnp.concatenate([mg, mn], axis=1))
        specs.append(pl.BlockSpec((None, 12, d), lambda i: (grp(i + toff), 0, 0)))
    args.append(norm_w.reshape(1, d))
    specs.append(pl.BlockSpec((1, d), lambda i: (0, 0)))
    out_shape, out_specs = [], []
    if write_x:
        out_shape.append(jax.ShapeDtypeStruct((n_rows, d), F32))
        out_specs.append(tile_out)
    out_shape.append(jax.ShapeDtypeStruct((n_rows, d), h_dtype))
    out_specs.append(tile_out)
    kern = functools.partial(
        _resid_norm_kernel, has_delta=has_delta, has_mod=has_mod, gate_row=gate_row,
        mod_rows=None if mod_rows is None else (6 + mod_rows[0], 6 + mod_rows[1]), write_x=write_x,
        n_first_tiles=nft)
    outs = pl.pallas_call(
        kern, grid=(n_rows // tm,), in_specs=specs, out_specs=out_specs, out_shape=out_shape,
        compiler_params=_cparams(("parallel",), 48), name="resid_norm",
    )(*args)
    return outs if write_x else (None, outs[0])


def _rope_pair(x1, x2, cos, sin):
    return x1 * cos - x2 * sin, x1 * sin + x2 * cos


def _weight_rows_spec(layer, row_starts, tn, d):
    def index_map(j, i):
        start = sum(jnp.where(j == k, s, 0) for k, s in enumerate(row_starts))
        return layer, pl.multiple_of(start, 8), 0
    assert all(s % 8 == 0 for s in row_starts)
    return pl.BlockSpec((pl.Element(1), pl.Element(tn), pl.Element(d)), index_map)


def _proj_kernel(h_ref, w_ref, cos_ref, sin_ref, out_ref, wb_scr, *, tm, rope_tile, n_prompt, dec_seq):
    j = pl.program_id(0)
    i = pl.program_id(1)

    @pl.when(i == 0)
    def _():
        wb_scr[...] = w_ref[0].astype(BF16)

    row0 = pl.multiple_of(i * tm, tm)
    use_rope = (j == rope_tile) & (row0 >= n_prompt)
    half = HEAD_DIM // 2

    def heads(rope):
        h = h_ref[...]
        if rope:
            pos0 = pl.multiple_of((row0 - n_prompt) % dec_seq, tm)
            cos = cos_ref[pl.ds(pos0, tm), :]
            sin = sin_ref[pl.ds(pos0, tm), :]
        for hh in range(wb_scr.shape[0] // HEAD_DIM):
            c0 = hh * HEAD_DIM
            acc = _dot_nt(h, wb_scr[c0:c0 + HEAD_DIM, :])
            if rope:
                y1, y2 = _rope_pair(acc[:, :half], acc[:, half:], cos, sin)
                out_ref[:, c0:c0 + half] = y1.astype(out_ref.dtype)
                out_ref[:, c0 + half:c0 + HEAD_DIM] = y2.astype(out_ref.dtype)
            else:
                out_ref[:, c0:c0 + HEAD_DIM] = acc.astype(out_ref.dtype)

    pl.when(use_rope)(functools.partial(heads, True))
    pl.when(jnp.logical_not(use_rope))(functools.partial(heads, False))


def _proj_call(h, w_t, layer, row_starts, rope, rope_tile, *, n_prompt, dec_seq, tm=512, tn=1024):
    n, d = h.shape
    nj = len(row_starts)
    kern = functools.partial(_proj_kernel, tm=tm, rope_tile=rope_tile, n_prompt=n_prompt, dec_seq=dec_seq)
    return pl.pallas_call(
        kern, grid=(nj, n // tm),
        in_specs=[
            pl.BlockSpec((tm, d), lambda j, i: (i, 0)),
            _weight_rows_spec(layer, row_starts, tn, d),
            pl.BlockSpec(rope[0].shape, lambda j, i: (0, 0)),
            pl.BlockSpec(rope[1].shape, lambda j, i: (0, 0)),
        ],
        out_specs=pl.BlockSpec((tm, tn), lambda j, i: (i, j)),
        out_shape=jax.ShapeDtypeStruct((n, nj * tn), BF16),
        scratch_shapes=[pltpu.VMEM((tn, d), BF16)],
        compiler_params=_cparams(("parallel", "arbitrary"), 56), name="in_proj",
    )(h, w_t, *rope)


def _proj_t_kernel(h_ref, w_ref, cos_ref, sin_ref, out_ref, wb_scr, *, tm, rope_tile, n_prompt, dec_seq):
    j = pl.program_id(0)
    i = pl.program_id(1)

    @pl.when(i == 0)
    def _():
        wb_scr[...] = w_ref[0].astype(BF16)

    row0 = pl.multiple_of(i * tm, tm)
    use_rope = (j == rope_tile) & (row0 >= n_prompt)
    half = HEAD_DIM // 2

    def heads(rope):
        h = h_ref[...]
        if rope:
            blk = ((row0 - n_prompt) % dec_seq) // tm
            cos = cos_ref[blk]
            sin = sin_ref[blk]
        for hh in range(wb_scr.shape[0] // HEAD_DIM):
            r0 = hh * HEAD_DIM
            acc = _dot_nt(wb_scr[r0:r0 + HEAD_DIM, :], h)
            if rope:
                y1, y2 = _rope_pair(acc[:half, :], acc[half:, :], cos, sin)
                y = jnp.concatenate([y1, y2], axis=0).astype(out_ref.dtype)
            else:
                y = acc.astype(out_ref.dtype)
            for s in range(tm // CHUNK):
                out_ref[s, r0:r0 + HEAD_DIM, :] = y[:, s * CHUNK:(s + 1) * CHUNK]

    pl.when(use_rope)(functools.partial(heads, True))
    pl.when(jnp.logical_not(use_rope))(functools.partial(heads, False))


def _proj_t_call(h, w_t, layer, row_starts, rope_t, rope_tile, *, n_prompt, dec_seq, tm=512, tn=1024):
    n, d = h.shape
    nj = len(row_starts)
    kern = functools.partial(_proj_t_kernel, tm=tm, rope_tile=rope_tile, n_prompt=n_prompt, dec_seq=dec_seq)
    return pl.pallas_call(
        kern, grid=(nj, n // tm),
        in_specs=[
            pl.BlockSpec((tm, d), lambda j, i: (i, 0)),
            _weight_rows_spec(layer, row_starts, tn, d),
            pl.BlockSpec(rope_t[0].shape, lambda j, i: (0, 0, 0)),
            pl.BlockSpec(rope_t[1].shape, lambda j, i: (0, 0, 0)),
        ],
        out_specs=pl.BlockSpec((tm // CHUNK, tn, CHUNK), lambda j, i: (i, j, 0)),
        out_shape=jax.ShapeDtypeStruct((n // CHUNK, nj * tn, CHUNK), BF16),
        scratch_shapes=[pltpu.VMEM((tn, d), BF16)],
        compiler_params=_cparams(("parallel", "arbitrary"), 56), name="in_proj_t",
    )(h, w_t, *rope_t)


def _gate_kernel(h_ref, wg_ref, bias_ref, out_ref, *, tm, n_heads):
    g = _dot_nt(wg_ref[...].astype(BF16), h_ref[...]) + bias_ref[...]
    nd = 2 * n_heads
    ig = g[0:nd, :]
    lf = _log_sigmoid(g[nd:2 * nd, :])
    r = _iota((CHUNK, CHUNK), 0)
    c = _iota((CHUNK, CHUNK), 1)
    upper = jnp.where(r <= c, 1.0, 0.0).astype(BF16)
    lower = jnp.where(r >= c, 1.0, 0.0).astype(BF16)
    is_fwd = _iota((nd, CHUNK), 0) < n_heads
    for s in range(tm // CHUNK):
        sl = slice(s * CHUNK, (s + 1) * CHUNK)
        lf_c = lf[:, sl]
        b = jnp.where(is_fwd, _dot3(lf_c, upper), _dot3(lf_c, lower))
        ig_c = ig[:, sl]
        for hh in range(n_heads):
            out_ref[hh, s, 0:1, :] = ig_c[hh:hh + 1, :]
            out_ref[hh, s, 1:2, :] = ig_c[n_heads + hh:n_heads + hh + 1, :]
            out_ref[hh, s, 2:3, :] = b[hh:hh + 1, :]
            out_ref[hh, s, 3:4, :] = b[n_heads + hh:n_heads + hh + 1, :]


def _gate_call(h, wg_t, bias_col, n_heads, tm=512):
    n, d = h.shape
    ng = 4 * n_heads
    return pl.pallas_call(
        functools.partial(_gate_kernel, tm=tm, n_heads=n_heads),
        grid=(n // tm,),
        in_specs=[
            pl.BlockSpec((tm, d), lambda i: (i, 0)),
            pl.BlockSpec((ng, d), lambda i: (0, 0)),
            pl.BlockSpec((ng, 1), lambda i: (0, 0)),
        ],
        out_specs=pl.BlockSpec((n_heads, tm // CHUNK, 4, CHUNK), lambda i: (0, i, 0, 0)),
        out_shape=jax.ShapeDtypeStruct((n_heads, n // CHUNK, 4, CHUNK), F32),
        compiler_params=_cparams(("parallel",), 32), name="mlstm_gates",
    )(h, wg_t, bias_col)


def _scan_loop(nc, body, init):
    if nc <= 2:
        carry = init
        for ci in range(nc):
            carry = body(ci, carry)
        return carry
    return lax.fori_loop(0, nc, body, init)


MAX_GROUP = 16
BF16_ROWS = 16
STRIP = 32


def _seqs_per_step(n_seq, nc):
    ns = max(1, MAX_GROUP // nc)
    while n_seq % ns:
        ns -= 1
    return ns


def _group_loop(n_groups, body):
    if n_groups == 1:
        body(0, 0)
    else:
        lax.fori_loop(0, n_groups, body, 0)


def _chunk_off(c):
    return c * CHUNK if isinstance(c, int) else pl.multiple_of(c * CHUNK, CHUNK)


def _mlstm_kernel(*refs, nc, ns, has_init, first_layer, write_state):
    it = iter(refs)
    qt_ref, vt_ref, ot_ref, k_ref, gp_ref, nw_ref = (next(it) for _ in range(6))
    if has_init:
        c0_ref, n0_ref, m0_ref = next(it), next(it), next(it)
    if write_state and not first_layer:
        next(it), next(it), next(it)
    hm_ref = next(it)
    if write_state:
        cout_ref, nout_ref, mout_ref = next(it), next(it), next(it)
    ct_scr, n_scr, part_scr, inc_scr, ninc_scr, rows_scr, cst_scr, nst_scr, coef_scr = (next(it) for _ in range(9))

    L = CHUNK
    grp = min(MAX_GROUP, ns * nc)
    row = _iota((L, L), 0)
    col = _iota((L, L), 1)
    k_scale = HEAD_DIM ** -0.5
    b_idx = pl.program_id(0)
    h_idx = pl.program_id(1)
    n_heads = pl.num_programs(1)

    def gates(d, c):
        g = gp_ref[c]
        ig = g[d:d + 1, :]
        brow = g[2 + d:3 + d, :]
        blast = brow[:, L - 1:L] if d == 0 else brow[:, 0:1]
        return ig, brow, blast

    def local_group(g, carry):
        cs = [g * grp + u for u in range(grp)]
        pairs = [(u, d) for u in range(grp) for d in range(2)]
        qt = [qt_ref[c] for c in cs]
        vt = [vt_ref[c] for c in cs]
        k = [k_ref[pl.ds(_chunk_off(c), L), :] for c in cs]
        qk = [_dot(k[u], qt[u]) for u in range(grp)]
        vf = [v.astype(F32) for v in vt]
        gts = {(u, d): gates(d, cs[u]) for u, d in pairs}
        for u, c in enumerate(cs):
            lhs = []
            for d in range(2):
                ig, brow, blast = gts[u, d]
                wj = blast - brow + ig
                mloc2 = jnp.max(wj, axis=1, keepdims=True)
                e = jnp.exp(wj - mloc2) * k_scale
                rows_scr[d, c, 2:3, :] = jnp.broadcast_to(mloc2, (1, L))
                rows_scr[d, c, 4:5, :] = jnp.broadcast_to(blast, (1, L))
                lhs.append(((vf[u] * e).astype(BF16), _split3(jnp.broadcast_to(e, (BF16_ROWS, L)))))
            res = _dot(jnp.concatenate([lhs[0][0], lhs[1][0], *lhs[0][1], *lhs[1][1]], axis=0), k[u])
            for d in range(2):
                inc_scr[d, c] = res[d * HEAD_DIM:(d + 1) * HEAD_DIM, :]
                r0 = 2 * HEAD_DIM + 3 * BF16_ROWS * d
                ninc_scr[d, c] = (res[r0:r0 + 8, :] + res[r0 + BF16_ROWS:r0 + BF16_ROWS + 8, :]
                                  + res[r0 + 2 * BF16_ROWS:r0 + 2 * BF16_ROWS + 8, :])
        for u, d in pairs:
            c = cs[u]
            ig, brow, blast = gts[u, d]
            key_term = jnp.broadcast_to(ig - brow, (L, L)).T
            causal = (row <= col) if d == 0 else (row >= col)
            dm = jnp.where(causal, key_term + brow, -jnp.inf)
            mloc = jnp.max(dm, axis=0, keepdims=True)
            s = qk[u] * (jnp.exp(dm - mloc) * k_scale)
            part_scr[d, c] = _dot(vt[u], s.astype(BF16))
            rows_scr[d, c, 0:1, :] = jnp.sum(s, axis=0, keepdims=True)
            rows_scr[d, c, 1:2, :] = mloc
        return carry

    _group_loop(ns * nc // grp, local_group)

    def m_step(d, c, m):
        blast = rows_scr[d, c, 4:5, :]
        mloc2 = rows_scr[d, c, 2:3, :]
        rows_scr[d, c, 3:4, :] = m
        m_new = jnp.maximum(blast + m, mloc2)
        decay = jnp.exp(blast + m - m_new)
        w_inc = jnp.exp(mloc2 - m_new)
        coef_scr[d, c, 0:1, :] = jnp.concatenate([decay] * (HEAD_DIM // L), axis=1)
        coef_scr[d, c, 1:2, :] = jnp.concatenate([w_inc] * (HEAD_DIM // L), axis=1)
        return m_new

    for sq in range(ns):
        c_lo = sq * nc
        m_init = []
        for d in range(2):
            if has_init:
                ct_scr[d] = c0_ref[d].T
                n_scr[d] = jnp.broadcast_to(n0_ref[d], n_scr.shape[1:])
                m_init.append(jnp.full((1, L), m0_ref[(b_idx * 2 + d) * n_heads + h_idx], F32))
            else:
                ct_scr[d] = jnp.zeros(ct_scr.shape[1:], F32)
                n_scr[d] = jnp.zeros(n_scr.shape[1:], F32)
                m_init.append(jnp.zeros((1, L), F32))

        def m_body(ci, ms, c_lo=c_lo):
            return m_step(0, c_lo + ci, ms[0]), m_step(1, c_lo + nc - 1 - ci, ms[1])

        m_fin = _scan_loop(nc, m_body, tuple(m_init))

        def body(ci, carry, c_lo=c_lo):
            for d, c in ((0, c_lo + ci), (1, c_lo + nc - 1 - ci)):
                decay = coef_scr[d, c, 0:1, :]
                w_inc = coef_scr[d, c, 1:2, :]
                n8 = n_scr[d]
                nst_scr[d, c] = n8
                for r0 in range(0, HEAD_DIM, STRIP):
                    ct = ct_scr[d, r0:r0 + STRIP, :]
                    cst_scr[d, c, r0:r0 + STRIP, :] = ct.astype(BF16)
                    ct_scr[d, r0:r0 + STRIP, :] = decay * ct + w_inc * inc_scr[d, c, r0:r0 + STRIP, :]
                n_scr[d] = decay * n8 + w_inc * ninc_scr[d, c]
            return carry

        _scan_loop(nc, body, 0)

        if write_state:
            lsel = (lambda d, sq=sq: (sq, 0, d)) if first_layer else (lambda d, sq=sq: (sq, d))
            for d in range(2):
                cout_ref[lsel(d)] = ct_scr[d].T
                nout_ref[lsel(d)] = n_scr[d][0:1, :]
                mout_ref[lsel(d)] = m_fin[d]
            if first_layer:
                for l in range(1, cout_ref.shape[1]):
                    cout_ref[sq, l] = jnp.zeros(cout_ref.shape[2:], F32)
                    nout_ref[sq, l] = jnp.zeros(nout_ref.shape[2:], F32)
                    mout_ref[sq, l] = jnp.zeros(mout_ref.shape[2:], F32)

    def finish_group(g, carry):
        cs = [g * grp + u for u in range(grp)]
        qt = [qt_ref[c] for c in cs]
        cq, nq = [], []
        for u, c in enumerate(cs):
            n16 = [jnp.concatenate([nst_scr[d, c]] * (BF16_ROWS // 8), axis=0).astype(BF16) for d in range(2)]
            res = _dot(jnp.concatenate([cst_scr[0, c], cst_scr[1, c], n16[0], n16[1]], axis=0), qt[u])
            cq.append([res[d * HEAD_DIM:(d + 1) * HEAD_DIM, :] for d in range(2)])
            nq.append([res[2 * HEAD_DIM + d * BF16_ROWS:2 * HEAD_DIM + d * BF16_ROWS + 1, :] for d in range(2)])
        for u, c in enumerate(cs):
            coef = []
            for d in range(2):
                _, brow, _ = gates(d, c)
                den_loc = rows_scr[d, c, 0:1, :]
                mloc = rows_scr[d, c, 1:2, :]
                inter = brow + rows_scr[d, c, 3:4, :]
                m_row = jnp.maximum(mloc, inter)
                w_loc = jnp.exp(mloc - m_row)
                w_inter = jnp.exp(inter - m_row)
                den = w_loc * den_loc + w_inter * nq[u][d]
                r_den = 1.0 / jnp.maximum(jnp.abs(den), jnp.exp(-m_row))
                coef.append((w_loc * r_den, w_inter * r_den))
            ssq = jnp.zeros((1, L), F32)
            for r0 in range(0, HEAD_DIM, STRIP):
                h_s = None
                for d in range(2):
                    h_d = coef[d][0] * part_scr[d, c, r0:r0 + STRIP, :] + coef[d][1] * cq[u][d][r0:r0 + STRIP, :]
                    h_s = h_d if h_s is None else h_s + h_d
                part_scr[0, c, r0:r0 + STRIP, :] = h_s
                ssq = ssq + jnp.sum(h_s * h_s, axis=0, keepdims=True)
            r_norm = lax.rsqrt(ssq * (1.0 / HEAD_DIM) + EPS)
            for r0 in range(0, HEAD_DIM, L):
                y = part_scr[0, c, r0:r0 + L, :] * r_norm * nw_ref[r0:r0 + L, :]
                o = jax.nn.sigmoid(ot_ref[c, r0:r0 + L, :].astype(F32)) * y
                hm_ref[pl.ds(_chunk_off(c), L), r0:r0 + L] = o.T.astype(hm_ref.dtype)
        return carry

    _group_loop(ns * nc // grp, finish_group)


def _state_specs(shapes, layer, first_layer, ns):
    specs = []
    for shp in shapes:
        tail = shp[4:]
        zeros = (0,) * len(tail)
        if first_layer:
            specs.append(pl.BlockSpec((ns, shp[1], 2, None) + tail, lambda b, h, z=zeros: (b, 0, 0, h) + z))
        else:
            specs.append(pl.BlockSpec((ns, None, 2, None) + tail, lambda b, h, z=zeros: (b, layer, 0, h) + z))
    return specs


def _mlstm_call(proj_t, proj_n, gp, nw_b, init, state_bufs, *, n_seq, seq_len, tok_off, n_heads, layer, depth):
    dk = HEAD_DIM
    nc = seq_len // CHUNK
    has_init = init is not None
    ns = 1 if has_init else _seqs_per_step(n_seq, nc)
    nct = ns * nc
    boff = tok_off // (ns * seq_len)
    write_state = not has_init
    first_layer = layer == 0
    args = [proj_t, proj_t, proj_t, proj_n, gp, nw_b]
    specs = [
        pl.BlockSpec((nct, dk, CHUNK), lambda b, h: (b + boff, h, 0)),
        pl.BlockSpec((nct, dk, CHUNK), lambda b, h: (b + boff, n_heads + h, 0)),
        pl.BlockSpec((nct, dk, CHUNK), lambda b, h: (b + boff, 2 * n_heads + h, 0)),
        pl.BlockSpec((ns * seq_len, dk), lambda b, h: (b + boff, h)),
        pl.BlockSpec((None, nct, 4, CHUNK), lambda b, h: (h, b + boff, 0, 0)),
        pl.BlockSpec((dk, LANES), lambda b, h: (h, 0)),
    ]
    aliases = {}
    if has_init:
        c0, n0, m0 = init
        args += [c0, n0[:, layer][:, :, :, None, :], m0[:, layer].reshape(-1)]
        specs += [
            pl.BlockSpec((None, None, 2, None, dk, dk), lambda b, h: (b, layer, 0, h, 0, 0)),
            pl.BlockSpec((None, 2, None, 1, dk), lambda b, h: (b, 0, h, 0, 0)),
            pl.BlockSpec(memory_space=pltpu.SMEM),
        ]
    out_shape = [jax.ShapeDtypeStruct((n_seq * seq_len, n_heads * dk), BF16)]
    out_specs = [pl.BlockSpec((ns * seq_len, dk), lambda b, h: (b, h))]
    if write_state:
        shapes = [(n_seq, depth, 2, n_heads, dk, dk), (n_seq, depth, 2, n_heads, 1, dk),
                  (n_seq, depth, 2, n_heads, 1, LANES)]
        if not first_layer:
            for k_, buf in enumerate(state_bufs):
                aliases[len(args)] = 1 + k_
                args.append(buf)
                specs.append(pl.BlockSpec(memory_space=pl.ANY))
        out_shape += [jax.ShapeDtypeStruct(s, F32) for s in shapes]
        out_specs += _state_specs(shapes, layer, first_layer, ns)
    kern = functools.partial(_mlstm_kernel, nc=nc, ns=ns, has_init=has_init, first_layer=first_layer,
                             write_state=write_state)
    return pl.pallas_call(
        kern, grid=(n_seq // ns, n_heads), in_specs=specs, out_specs=out_specs, out_shape=out_shape,
        input_output_aliases=aliases,
        scratch_shapes=[pltpu.VMEM((2, dk, dk), F32), pltpu.VMEM((2, 8, dk), F32),
                        pltpu.VMEM((2, nct, dk, CHUNK), F32), pltpu.VMEM((2, nct, dk, dk), F32),
                        pltpu.VMEM((2, nct, 8, dk), F32), pltpu.VMEM((2, nct, 8, CHUNK), F32),
                        pltpu.VMEM((2, nct, dk, dk), BF16), pltpu.VMEM((2, nct, 8, dk), F32),
                        pltpu.VMEM((2, nct, 8, dk), F32)],
        compiler_params=_cparams(("parallel", "parallel"), 48), name="mlstm_mixer",
    )(*args)


def _ret_kernel(*refs, nc, ns, has_init, first_layer, write_state):
    it = iter(refs)
    dl_ref, qt_ref, vt_ref, gt_ref, k_ref, nw_ref = (next(it) for _ in range(6))
    if has_init:
        s0_ref = next(it)
    if write_state and not first_layer:
        next(it)
    hr_ref = next(it)
    if write_state:
        sout_ref = next(it)
    st_scr, part_scr, inc_scr, sst_scr = next(it), next(it), next(it), next(it)

    L = CHUNK
    grp = min(MAX_GROUP, ns * nc)
    rowi = _iota((L, L), 0)
    coli = _iota((L, L), 1)
    k_scale = HEAD_DIM ** -0.5
    h_idx = pl.program_id(1)
    n_heads = pl.num_programs(1)

    consts = []
    for d in range(2):
        lg = _log_sigmoid(jnp.full((1, 1), dl_ref[d * n_heads + h_idx], F32))
        rel = (coli - rowi if d == 0 else rowi - coli).astype(F32)
        dmat_t = jnp.where(rel >= 0, jnp.exp(jnp.maximum(rel, 0.0) * lg), 0.0) * k_scale
        pos = _iota((1, L), 1).astype(F32)
        if d == 1:
            pos = (L - 1.0) - pos
        q_decay = jnp.exp((pos + 1.0) * lg)
        k_decay = jnp.exp((L - 1.0 - pos) * lg) * k_scale
        chunk_decay = jnp.exp(float(L) * lg)
        consts.append((dmat_t, q_decay, k_decay, chunk_decay))
    dmat_both = consts[0][0] + consts[1][0]

    def local_group(g, carry):
        cs = [g * grp + u for u in range(grp)]
        qt = [qt_ref[c] for c in cs]
        vt = [vt_ref[c] for c in cs]
        k = [k_ref[pl.ds(_chunk_off(c), L), :] for c in cs]
        qk = [_dot(k[u], qt[u]) for u in range(grp)]
        for u, c in enumerate(cs):
            vf = vt[u].astype(F32)
            res = _dot(jnp.concatenate([(vf * consts[d][2]).astype(BF16) for d in range(2)], axis=0), k[u])
            for d in range(2):
                inc_scr[d, c] = res[d * HEAD_DIM:(d + 1) * HEAD_DIM, :]
        for u, c in enumerate(cs):
            part_scr[c] = _dot(vt[u], (qk[u] * dmat_both).astype(BF16))
        return carry

    _group_loop(ns * nc // grp, local_group)

    for sq in range(ns):
        c_lo = sq * nc
        for d in range(2):
            if has_init:
                st_scr[d] = s0_ref[d].T
            else:
                st_scr[d] = jnp.zeros(st_scr.shape[1:], F32)

        def body(ci, carry, c_lo=c_lo):
            for d, c in ((0, c_lo + ci), (1, c_lo + nc - 1 - ci)):
                for r0 in range(0, HEAD_DIM, STRIP):
                    st = st_scr[d, r0:r0 + STRIP, :]
                    sst_scr[d, c, r0:r0 + STRIP, :] = st.astype(BF16)
                    st_scr[d, r0:r0 + STRIP, :] = consts[d][3] * st + inc_scr[d, c, r0:r0 + STRIP, :]
            return carry

        _scan_loop(nc, body, 0)

        if write_state:
            for d in range(2):
                if first_layer:
                    sout_ref[sq, 0, d] = st_scr[d].T
                else:
                    sout_ref[sq, d] = st_scr[d].T
            if first_layer:
                for l in range(1, sout_ref.shape[1]):
                    sout_ref[sq, l] = jnp.zeros(sout_ref.shape[2:], F32)

    def finish_group(g, carry):
        cs = [g * grp + u for u in range(grp)]
        qt = [qt_ref[c] for c in cs]
        sq = [_dot(jnp.concatenate([sst_scr[0, c], sst_scr[1, c]], axis=0), qt[u]) for u, c in enumerate(cs)]
        for u, c in enumerate(cs):
            ssq = jnp.zeros((1, L), F32)
            for r0 in range(0, HEAD_DIM, STRIP):
                sl = slice(r0, r0 + STRIP)
                o_s = part_scr[c, sl, :] + (consts[0][1] * sq[u][r0:r0 + STRIP, :]
                                            + consts[1][1] * sq[u][HEAD_DIM + r0:HEAD_DIM + r0 + STRIP, :])
                part_scr[c, sl, :] = o_s
                ssq = ssq + jnp.sum(o_s * o_s, axis=0, keepdims=True)
            r_norm = lax.rsqrt(ssq * (1.0 / HEAD_DIM) + EPS)
            for r0 in range(0, HEAD_DIM, L):
                y = part_scr[c, r0:r0 + L, :] * r_norm * nw_ref[r0:r0 + L, :]
                rg = gt_ref[c, r0:r0 + L, :].astype(F32)
                hr_ref[pl.ds(_chunk_off(c), L), r0:r0 + L] = (rg * jax.nn.sigmoid(rg) * y).T.astype(hr_ref.dtype)
        return carry

    _group_loop(ns * nc // grp, finish_group)


def _ret_call(proj_t, proj_n, decay_logit, nw_b, init, state_buf, *, n_seq, seq_len, tok_off, n_heads, layer,
              depth):
    dk = HEAD_DIM
    nc = seq_len // CHUNK
    has_init = init is not None
    ns = 1 if has_init else _seqs_per_step(n_seq, nc)
    nct = ns * nc
    boff = tok_off // (ns * seq_len)
    write_state = not has_init
    first_layer = layer == 0
    args = [decay_logit.reshape(-1), proj_t, proj_t, proj_t, proj_n, nw_b]
    specs = [
        pl.BlockSpec(memory_space=pltpu.SMEM),
        pl.BlockSpec((nct, dk, CHUNK), lambda b, h: (b + boff, 3 * n_heads + h, 0)),
        pl.BlockSpec((nct, dk, CHUNK), lambda b, h: (b + boff, 4 * n_heads + h, 0)),
        pl.BlockSpec((nct, dk, CHUNK), lambda b, h: (b + boff, 5 * n_heads + h, 0)),
        pl.BlockSpec((ns * seq_len, dk), lambda b, h: (b + boff, n_heads + h)),
        pl.BlockSpec((dk, LANES), lambda b, h: (h, 0)),
    ]
    aliases = {}
    if has_init:
        args.append(init)
        specs.append(pl.BlockSpec((None, None, 2, None, dk, dk), lambda b, h: (b, layer, 0, h, 0, 0)))
    out_shape = [jax.ShapeDtypeStruct((n_seq * seq_len, n_heads * dk), BF16)]
    out_specs = [pl.BlockSpec((ns * seq_len, dk), lambda b, h: (b, h))]
    if write_state:
        shapes = [(n_seq, depth, 2, n_heads, dk, dk)]
        if not first_layer:
            aliases[len(args)] = 1
            args.append(state_buf)
            specs.append(pl.BlockSpec(memory_space=pl.ANY))
        out_shape += [jax.ShapeDtypeStruct(s, F32) for s in shapes]
        out_specs += _state_specs(shapes, layer, first_layer, ns)
    kern = functools.partial(_ret_kernel, nc=nc, ns=ns, has_init=has_init, first_layer=first_layer,
                             write_state=write_state)
    return pl.pallas_call(
        kern, grid=(n_seq // ns, n_heads), in_specs=specs, out_specs=out_specs, out_shape=out_shape,
        input_output_aliases=aliases,
        scratch_shapes=[pltpu.VMEM((2, dk, dk), F32), pltpu.VMEM((nct, dk, CHUNK), F32),
                        pltpu.VMEM((2, nct, dk, dk), F32), pltpu.VMEM((2, nct, dk, dk), BF16)],
        compiler_params=_cparams(("parallel", "parallel"), 48), name="retention_mixer",
    )(*args)


def _merge_kernel(hmp_ref, hms_ref, hrp_ref, hrs_ref, ga_ref, gb_ref, xp_ref, xs_ref, mod_ref, nw_ref, wa_ref,
                  wb_ref, wo_ref, wr_ref, x1_ref, h2_ref, aff_ref, wab_scr, wbb_scr, wob_scr, *, n_prompt_tiles):
    i = pl.program_id(0)

    @pl.when(i == 0)
    def _():
        wab_scr[...] = wa_ref[...].astype(BF16)
        wbb_scr[...] = wb_ref[...].astype(BF16)
        wob_scr[...] = wo_ref[...].astype(BF16)

    is_prompt = i < n_prompt_tiles
    hm = jnp.where(is_prompt, hmp_ref[...], hms_ref[...])
    hr = jnp.where(is_prompt, hrp_ref[...], hrs_ref[...])
    ya = _dot(hm, wab_scr[...])
    yb = _dot(hr, wbb_scr[...])
    merged = (jax.nn.sigmoid(ga_ref[...].astype(F32)) * ya + jax.nn.sigmoid(gb_ref[...].astype(F32)) * yb)
    y = _dot(merged.astype(BF16), wob_scr[...])
    x1 = jnp.where(is_prompt, xp_ref[...], xs_ref[...]) + mod_ref[2:3, :] * y
    x1_ref[...] = x1
    h2 = x1 * lax.rsqrt(jnp.mean(x1 * x1, axis=-1, keepdims=True) + EPS) * nw_ref[...]
    h2 = h2 * (1.0 + mod_ref[4:5, :]) + mod_ref[3:4, :]
    n_blk = h2.shape[1] // LANES
    for fb in range(n_blk):
        _tile_rows(h2_ref, fb, h2.shape[0], n_blk)[...] = h2[:, fb * LANES:(fb + 1) * LANES]
    logits = _dot_nt(wr_ref[...].astype(BF16), h2.astype(BF16))
    p = jnp.exp(logits - jnp.max(logits, axis=0, keepdims=True))
    aff_ref[...] = p / jnp.sum(p, axis=0, keepdims=True)


def _merge_call(hm_p, hm_s, hr_p, hr_s, proj_n, x_p, x_s, x_s_row0, mod, norm_w, w_a, w_b, w_out, router_wt,
                layer, *, n_prompt, dec_seq, tm=512):
    n = proj_n.shape[0]
    d = x_p.shape[1]
    ne = router_wt.shape[1]
    npt = n_prompt // tm
    soff = x_s_row0 // tm
    grp = functools.partial(_group_of_tile, tm=tm, n_prompt=n_prompt, dec_seq=dec_seq)
    tile = pl.BlockSpec((tm, d), lambda i: (i, 0))
    tile_p = pl.BlockSpec((tm, d), lambda i: (jnp.minimum(i, npt - 1), 0))
    tile_s = pl.BlockSpec((tm, d), lambda i: (jnp.maximum(i - npt, 0), 0))
    full = pl.BlockSpec((None, d, d), lambda i: (layer, 0, 0))
    return pl.pallas_call(
        functools.partial(_merge_kernel, n_prompt_tiles=npt), grid=(n // tm,),
        in_specs=[
            tile_p, tile_s, tile_p, tile_s,
            pl.BlockSpec((tm, d), lambda i: (i, 2)),
            pl.BlockSpec((tm, d), lambda i: (i, 3)),
            tile_p,
            pl.BlockSpec((tm, d), lambda i: (jnp.maximum(i - npt, 0) + soff, 0)),
            pl.BlockSpec((None, 6, d), lambda i: (grp(i), 0, 0)),
            pl.BlockSpec((1, d), lambda i: (0, 0)),
            full, full, full,
            pl.BlockSpec((None, ne, d), lambda i: (layer, 0, 0)),
        ],
        out_specs=[tile, pl.BlockSpec((tm * (d // LANES), LANES), lambda i: (i, 0)),
                   pl.BlockSpec((ne, tm), lambda i: (0, i))],
        out_shape=[jax.ShapeDtypeStruct((n, d), F32), jax.ShapeDtypeStruct((n * (d // LANES), LANES), F32),
                   jax.ShapeDtypeStruct((ne, n), F32)],
        scratch_shapes=[pltpu.VMEM((d, d), BF16)] * 3,
        compiler_params=_cparams(("arbitrary",), 56), name="merge_out_router",
    )(hm_p, hm_s, hr_p, hr_s, proj_n, proj_n, x_p, x_s, mod, norm_w.reshape(1, d), w_a, w_b, w_out, router_wt)


ROUTE_GROUP = 4


def _route_kernel(a_ref, at_ref, idx_ref, gate_ref, thr_scr, *, cap, n_tok):
    n_sets = a_ref.shape[0]
    nb = n_tok // LANES
    a_all = a_ref[...]

    def as_f32(bits):
        return lax.bitcast_convert_type(bits, F32)

    def count_ge(cand):
        m = jnp.where(a_all >= cand, 1.0, 0.0)
        return jnp.sum(jnp.sum(m, axis=1, keepdims=True), axis=2, keepdims=True)

    def bit_step(k, thr):
        cand = thr | lax.shift_left(jnp.int32(1), 30 - k)
        return jnp.where(count_ge(as_f32(cand)) >= cap, cand, thr)

    thr_scr[...] = lax.fori_loop(0, 31, bit_step, jnp.zeros((n_sets, 1, 1), jnp.int32))

    r128 = _iota((LANES, LANES), 0)
    c128 = _iota((LANES, LANES), 1)
    upper = jnp.where(r128 <= c128, 1.0, 0.0).astype(BF16)
    lower_t = jnp.where(r128 >= c128, 1.0, 0.0).astype(BF16)
    rb = _iota((nb, nb), 0)
    cb = _iota((nb, nb), 1)
    blk_before_rows = jnp.where(cb < rb, 1.0, 0.0).astype(BF16)
    blk_before_cols = jnp.where(rb < cb, 1.0, 0.0).astype(BF16)

    def incl_counts(mask):
        within = _dot(mask.astype(BF16), upper)
        before = _dot(blk_before_rows, within.astype(BF16))[:, LANES - 1:LANES]
        return within + before

    def incl_counts_t(mask_t):
        within = _dot(lower_t, mask_t.astype(BF16))
        before = _dot(within.astype(BF16), blk_before_cols)[LANES - 1:LANES, :]
        return within + before

    slot = _iota((1, cap), 1).astype(F32)
    blk_col = _iota((nb, 1), 0).astype(F32)
    sub_col = _iota((LANES, 1), 0).astype(F32)

    def per_group(g, carry):
        sets = [g + u * (n_sets // ROUTE_GROUP) for u in range(ROUTE_GROUP)]
        stage = []
        for s in sets:
            thr_bits = thr_scr[s]
            thr = as_f32(thr_bits)
            nxt = as_f32(thr_bits + 1)
            a = a_ref[s]
            a_t = at_ref[s]
            gt = jnp.where(a >= nxt, 1.0, 0.0)
            eq = jnp.where((a >= thr) & (a < nxt), 1.0, 0.0)
            gt_t = jnp.where(a_t >= nxt, 1.0, 0.0)
            eq_t = jnp.where((a_t >= thr) & (a_t < nxt), 1.0, 0.0)
            n_gt = jnp.sum(jnp.sum(gt, axis=1, keepdims=True), axis=0, keepdims=True)
            stage.append((a_t, gt, eq, gt_t, eq_t, cap - n_gt))
        ties = [(incl_counts(eq), incl_counts_t(eq_t)) for _, _, eq, _, eq_t, _ in stage]
        sels = []
        for (a_t, gt, eq, gt_t, eq_t, need), (c_eq, c_eq_t) in zip(stage, ties):
            sels.append((gt + eq * jnp.where(c_eq - eq < need, 1.0, 0.0),
                         gt_t + eq_t * jnp.where(c_eq_t - eq_t < need, 1.0, 0.0)))
        cnts = [(incl_counts(sel), incl_counts_t(sel_t)) for sel, sel_t in sels]
        picks = []
        for (a_t, *_), (cnt, cnt_t) in zip(stage, cnts):
            blk_end = cnt[:, LANES - 1:LANES]
            blk_of_slot = jnp.sum(jnp.where(blk_end <= slot, 1.0, 0.0), axis=0, keepdims=True)
            onehot_blk = jnp.where(blk_col == blk_of_slot, 1.0, 0.0).astype(BF16)
            cnt_rows = _dot3(cnt_t, onehot_blk)
            a_rows = _dot3(a_t, onehot_blk)
            picks.append((blk_of_slot, cnt_rows, a_rows))
        for s, (blk_of_slot, cnt_rows, a_rows) in zip(sets, picks):
            sub_of_slot = jnp.sum(jnp.where(cnt_rows <= slot, 1.0, 0.0), axis=0, keepdims=True)
            gate = jnp.sum(jnp.where(sub_col == sub_of_slot, a_rows, 0.0), axis=0, keepdims=True)
            idx_ref[s] = (blk_of_slot * LANES + sub_of_slot).astype(jnp.int32)
            gate_ref[s] = gate
        return carry

    lax.fori_loop(0, n_sets // ROUTE_GROUP, per_group, 0)


def _route_call(aff_t, n_pass, cap):
    ne, n = aff_t.shape
    n_tok = n // n_pass
    nb = n_tok // LANES
    a4 = aff_t.reshape(ne, n_pass, nb, LANES).transpose(1, 0, 2, 3).reshape(n_pass * ne, nb, LANES)
    a4_t = a4.transpose(0, 2, 1)
    n_sets = n_pass * ne
    return pl.pallas_call(
        functools.partial(_route_kernel, cap=cap, n_tok=n_tok),
        out_shape=[jax.ShapeDtypeStruct((n_sets, 1, cap), jnp.int32),
                   jax.ShapeDtypeStruct((n_sets, 1, cap), F32)],
        scratch_shapes=[pltpu.VMEM((n_sets, 1, 1), jnp.int32)],
        compiler_params=pltpu.CompilerParams(vmem_limit_bytes=32 * MIB), name="expert_choice_route",
    )(a4, a4_t)


ROW_LOOP_UNROLL = 8


def _moe_kernel(idx_ref, gate_ref, h_hbm, w1_ref, w3_ref, w2_ref, out_hbm,
                xe_scr, xb_scr, ye_a, ye_b, acc_scr, gsem, osem, *, rows, rows_pad):
    e = pl.program_id(0)
    f = pl.program_id(1)
    n_e = pl.num_programs(0)
    n_f = pl.num_programs(1)
    chunk = rows_pad // n_f

    n_blk = xb_scr.shape[1] // LANES

    def tile_of(t):
        return pl.ds(pl.multiple_of(t * n_blk, n_blk), n_blk)

    def gather_row(slot, s):
        tok = idx_ref[slot * rows_pad + s]
        pltpu.make_async_copy(h_hbm.at[tile_of(tok), :], xe_scr.at[tile_of(s), :], gsem).start()

    def scatter_row(slot, s, ye_ref):
        tok = idx_ref[slot * rows_pad + s]
        g = gate_ref[slot * rows_pad + s]
        acc_scr[pl.ds(tok, 1), :] += ye_ref[pl.ds(s, 1), :] * g

    def wait_gather():
        pltpu.make_async_copy(h_hbm.at[pl.ds(0, rows_pad * n_blk), :], xe_scr, gsem).wait()

    @pl.when((e == 0) & (f == 0))
    def _():
        acc_scr[...] = jnp.zeros_like(acc_scr)
        ye_a[...] = jnp.zeros_like(ye_a)
        ye_b[...] = jnp.zeros_like(ye_b)

        def issue(s, carry):
            gather_row(1, s)
            return carry
        lax.fori_loop(0, rows_pad, issue, 0, unroll=ROW_LOOP_UNROLL)

    def step(ye_cur, ye_prev):
        @pl.when(f == 0)
        def _():
            wait_gather()
            for fb in range(n_blk):
                xb_scr[:, fb * LANES:(fb + 1) * LANES] = _tile_rows(xe_scr, fb, rows, n_blk)[...].astype(BF16)
            ye_cur[0:rows, :] = jnp.zeros((rows, ye_cur.shape[1]), F32)

        xb = xb_scr[...]
        h1 = _dot(xb, w1_ref[...].astype(BF16))
        h3 = _dot(xb, w3_ref[...].astype(BF16))
        he = (h1 * jax.nn.sigmoid(h1) * h3).astype(BF16)
        ye_cur[0:rows, :] += _dot(he, w2_ref[...].astype(BF16))
        base = f * chunk
        for r in range(chunk):
            gather_row(e + 2, base + r)
        for r in range(chunk):
            scatter_row(e, base + r, ye_prev)

        @pl.when((e == n_e - 1) & (f == n_f - 1))
        def _():
            def scatter(s, carry):
                scatter_row(n_e, s, ye_cur)
                return carry
            lax.fori_loop(0, rows, scatter, 0, unroll=ROW_LOOP_UNROLL)
            wait_gather()
            cp = pltpu.make_async_copy(acc_scr, out_hbm, osem)
            cp.start()
            cp.wait()

    @pl.when(e % 2 == 0)
    def _():
        step(ye_a, ye_b)

    @pl.when(e % 2 == 1)
    def _():
        step(ye_b, ye_a)


def _moe_call(idx, gate, h2, w1, w3, w2, layer, tf=256):
    d = w1.shape[2]
    n_blk = d // LANES
    n = h2.shape[0] // n_blk
    ne, rows = idx.shape
    dff = w1.shape[3]
    nf = dff // tf
    rows_pad = -(-rows // (8 * nf)) * 8 * nf
    pad = ((1, 1), (0, rows_pad - rows))
    idx_all = jnp.pad(idx, pad).reshape(-1)
    gate_all = jnp.pad(gate, pad).reshape(-1)
    grid_spec = pltpu.PrefetchScalarGridSpec(
        num_scalar_prefetch=1,
        grid=(ne, nf),
        in_specs=[
            pl.BlockSpec(memory_space=pltpu.SMEM),
            pl.BlockSpec(memory_space=pl.ANY),
            pl.BlockSpec((None, None, d, tf), lambda e, f, idx: (layer, e, 0, f)),
            pl.BlockSpec((None, None, d, tf), lambda e, f, idx: (layer, e, 0, f)),
            pl.BlockSpec((None, None, tf, d), lambda e, f, idx: (layer, e, f, 0)),
        ],
        out_specs=pl.BlockSpec(memory_space=pl.ANY),
        scratch_shapes=[
            pltpu.VMEM((rows_pad * n_blk, LANES), F32), pltpu.VMEM((rows, d), BF16),
            pltpu.VMEM((rows_pad, d), F32), pltpu.VMEM((rows_pad, d), F32),
            pltpu.VMEM((n, d), F32), pltpu.SemaphoreType.DMA, pltpu.SemaphoreType.DMA,
        ],
    )
    return pl.pallas_call(
        functools.partial(_moe_kernel, rows=rows, rows_pad=rows_pad), grid_spec=grid_spec,
        out_shape=jax.ShapeDtypeStruct((n, d), F32),
        compiler_params=_cparams(("arbitrary", "arbitrary"), 62), name="expert_ffn",
    )(idx_all, gate_all, h2, w1, w3, w2)


def _rope_tables(t):
    rows = t // GRID_W
    row = jnp.repeat(jnp.arange(rows, dtype=F32), GRID_W)
    colp = jnp.tile(jnp.arange(GRID_W, dtype=F32), rows)
    n_freq = HEAD_DIM // 4
    inv = ROPE_BASE ** (-jnp.arange(n_freq, dtype=F32) / n_freq)
    ang = jnp.concatenate([row[:, None] * inv, colp[:, None] * inv], axis=-1)
    return jnp.cos(ang), jnp.sin(ang)


def kernel(x_prompt, x_sample, c, state_mlstm_C, state_mlstm_n, state_mlstm_m, state_ret_S, c_ctx, w_mod, b_mod,
           norm1_w, norm2_w, w_in, mlstm_if_b, mlstm_norm_w, ret_decay_logit, ret_norm_w, w_branch_a, w_branch_b,
           w_out, router_w, ffn_w1, ffn_w3, ffn_w2, final_norm_w):
    bp, seq, d = x_prompt.shape
    db, dec_seq, _ = x_sample.shape
    depth = w_mod.shape[0]
    hm = mlstm_if_b.shape[-1]
    hr = ret_decay_logit.shape[-1]
    ne = router_w.shape[-1]
    n_prompt = bp * seq
    n_sample = db * dec_seq
    assert n_prompt == n_sample and hm * HEAD_DIM == d and hr * HEAD_DIM == d and hm == hr
    cap = EC_FACTOR * n_prompt // ne
    tm = 512
    tp = 2048
    geo = dict(n_prompt=n_prompt, dec_seq=dec_seq)

    cond8 = jnp.concatenate([c_ctx[None, :], c, jnp.zeros((8 - 1 - db, d), F32)], axis=0)
    mod = _mod_call(cond8, w_mod, b_mod)[:, :1 + db].reshape(depth, 1 + db, 6, d)

    n_m = 4 * d
    g0 = n_m + 4 * hm

    w_t = jnp.swapaxes(w_in, 1, 2)
    mq, mk, mv, mo = (k * d for k in range(4))
    rq, rk, rv, rg, ga, gb = (g0 + k * d for k in range(6))
    wg_t = w_t[:, n_m:g0]
    gate_bias = jnp.transpose(mlstm_if_b, (0, 2, 1, 3)).reshape(depth, 4 * hm, 1)
    router_wt = jnp.swapaxes(router_w, 1, 2)
    nw_m = jnp.broadcast_to(mlstm_norm_w[:, :, None], (depth, d, LANES))
    nw_r = jnp.broadcast_to(ret_norm_w[:, :, None], (depth, d, LANES))

    cos, sin = _rope_tables(dec_seq)
    cos_t = cos.T.reshape(HEAD_DIM // 2, dec_seq // tp, tp).transpose(1, 0, 2)
    sin_t = sin.T.reshape(HEAD_DIM // 2, dec_seq // tp, tp).transpose(1, 0, 2)

    x = (x_prompt.reshape(n_prompt, d), x_sample.reshape(n_sample, d), 0)
    _, h = _resid_norm_call(x[0], None, None, mod[0], norm1_w[0], gate_row=0, mod_rows=(0, 1), h_dtype=BF16,
                            write_x=False, x_tail=x[1], **geo)
    m_bufs, s_buf = None, None
    y_prompt = y_sample = None
    for l in range(depth):
        proj_t = _proj_t_call(h, w_t, l, (mq, mv, mo, rq, rv, rg), (cos_t, sin_t), 3, tm=tp, **geo)
        proj_n = _proj_call(h, w_t, l, (mk, rk, ga, gb), (cos, sin), 1, tm=tp, **geo)
        gp = _gate_call(h, wg_t[l], gate_bias[l], hm, tm=2 * tm)

        mix = dict(n_heads=hm, layer=l, depth=depth)
        hm_p, *m_bufs = _mlstm_call(proj_t, proj_n, gp, nw_m[l], None, m_bufs, n_seq=bp, seq_len=seq, tok_off=0,
                                    **mix)
        (hm_s,) = _mlstm_call(proj_t, proj_n, gp, nw_m[l], (state_mlstm_C, state_mlstm_n, state_mlstm_m), None,
                              n_seq=db, seq_len=dec_seq, tok_off=n_prompt, **mix)
        hr_p, s_buf = _ret_call(proj_t, proj_n, ret_decay_logit[l], nw_r[l], None, s_buf, n_seq=bp, seq_len=seq,
                                tok_off=0, **mix)
        (hr_s,) = _ret_call(proj_t, proj_n, ret_decay_logit[l], nw_r[l], state_ret_S, None, n_seq=db,
                            seq_len=dec_seq, tok_off=n_prompt, **mix)

        x1, h2, aff_t = _merge_call(hm_p, hm_s, hr_p, hr_s, proj_n, *x, mod[l], norm2_w[l], w_branch_a,
                                    w_branch_b, w_out, router_wt, l, tm=tm, **geo)
        idx, gate = _route_call(aff_t, 2, cap)
        idx = idx.reshape(2, ne, cap) + (jnp.arange(2, dtype=jnp.int32) * n_prompt)[:, None, None]
        idx = jnp.swapaxes(idx, 0, 1).reshape(ne, 2 * cap)
        gate = jnp.swapaxes(gate.reshape(2, ne, cap), 0, 1).reshape(ne, 2 * cap)
        moe = _moe_call(idx, gate, h2, ffn_w1, ffn_w3, ffn_w2, l)
        if l + 1 < depth:
            x_all, h = _resid_norm_call(x1, moe, mod[l], mod[l + 1], norm1_w[l + 1], gate_row=5, mod_rows=(0, 1),
                                        h_dtype=BF16, write_x=True, tm=tm, **geo)
            x = (x_all, x_all, n_prompt)
        else:
            fin = dict(gate_row=5, mod_rows=None, h_dtype=F32, write_x=False, **geo)
            _, y_prompt = _resid_norm_call(x1, moe, mod[l], None, final_norm_w, row_off=0, n_rows=n_prompt, **fin)
            _, y_sample = _resid_norm_call(x1, moe, mod[l], None, final_norm_w, row_off=n_prompt, n_rows=n_sample,
                                           **fin)

    c_buf, n_buf, m_buf = m_bufs
    return (y_prompt.reshape(bp, seq, d), y_sample.reshape(db, dec_seq, d), c_buf, n_buf[:, :, :, :, 0, :],
            m_buf[:, :, :, :, 0, 0], s_buf)
```

```python
import functools

import jax
import jax.numpy as jnp
from jax import lax
from jax.experimental import pallas as pl
from jax.experimental.pallas import tpu as pltpu

F32 = jnp.float32
BF16 = jnp.bfloat16

GRID_W = 64
CHUNK = 128
LANES = 128
HEAD_DIM = 256
EC_FACTOR = 2
ROPE_BASE = 10000.0
EPS = 1e-6
V7X_VMEM_BYTES = 64 * 1024 * 1024
MIB = 1024 * 1024


def _cparams(semantics, vmem_mib):
    assert vmem_mib * MIB < V7X_VMEM_BYTES
    return pltpu.CompilerParams(dimension_semantics=semantics, vmem_limit_bytes=vmem_mib * MIB)


def _dot(a, b):
    return jnp.dot(a, b, preferred_element_type=F32)


def _dot_nt(a, b):
    return lax.dot_general(a, b, (((1,), (1,)), ((), ())), preferred_element_type=F32)


def _log_sigmoid(x):
    return -(jnp.maximum(-x, 0.0) + jnp.log1p(jnp.exp(-jnp.abs(x))))


def _split3(x):
    hi = x.astype(BF16)
    r1 = x - hi.astype(F32)
    mid = r1.astype(BF16)
    lo = (r1 - mid.astype(F32)).astype(BF16)
    return hi, mid, lo


def _dot3(x, m):
    hi, mid, lo = _split3(x)
    return _dot(hi, m) + _dot(mid, m) + _dot(lo, m)


def _iota(shape, dim):
    return lax.broadcasted_iota(jnp.int32, shape, dim)


def _tile_rows(ref, fb, n_tok, n_blk):
    return ref.at[pl.ds(fb, n_tok, stride=n_blk), :]


def _mod_kernel(cond_ref, w_ref, b_ref, out_ref):
    c = cond_ref[...]
    s = (c * jax.nn.sigmoid(c)).astype(BF16)
    out_ref[...] = _dot(s, w_ref[...].astype(BF16)) + b_ref[...]


def _mod_call(cond8, w_mod, b_mod):
    depth, d, w6 = w_mod.shape
    tn = 1536
    return pl.pallas_call(
        _mod_kernel,
        grid=(depth, w6 // tn),
        in_specs=[
            pl.BlockSpec((8, d), lambda l, j: (0, 0)),
            pl.BlockSpec((None, d, tn), lambda l, j: (l, 0, j)),
            pl.BlockSpec((None, 1, tn), lambda l, j: (l, 0, j)),
        ],
        out_specs=pl.BlockSpec((None, 8, tn), lambda l, j: (l, 0, j)),
        out_shape=jax.ShapeDtypeStruct((depth, 8, w6), F32),
        compiler_params=_cparams(("parallel", "parallel"), 32),
        name="adaln_mod",
    )(cond8, w_mod, b_mod.reshape(depth, 1, w6))


def _group_of_tile(i, tm, n_prompt, dec_seq):
    return jnp.maximum(i * tm - n_prompt + dec_seq, 0) // dec_seq


def _resid_norm_kernel(*refs, has_delta, has_mod, gate_row, mod_rows, write_x, n_first_tiles):
    it = iter(refs)
    x_ref = next(it)
    x2_ref = next(it) if n_first_tiles else None
    delta_ref = next(it) if has_delta else None
    mod_ref = next(it) if (has_delta or has_mod) else None
    nw_ref = next(it)
    xo_ref = next(it) if write_x else None
    h_ref = next(it)
    x = x_ref[...]
    if n_first_tiles:
        x = jnp.where(pl.program_id(0) < n_first_tiles, x, x2_ref[...])
    if has_delta:
        x = x + mod_ref[gate_row:gate_row + 1, :] * delta_ref[...]
    if write_x:
        xo_ref[...] = x
    y = x * lax.rsqrt(jnp.mean(x * x, axis=-1, keepdims=True) + EPS) * nw_ref[...]
    if has_mod:
        sh_row, sc_row = mod_rows
        y = y * (1.0 + mod_ref[sc_row:sc_row + 1, :]) + mod_ref[sh_row:sh_row + 1, :]
    h_ref[...] = y.astype(h_ref.dtype)


def _resid_norm_call(x, delta, mod_gate, mod_norm, norm_w, *, gate_row, mod_rows, h_dtype, write_x,
                     n_prompt, dec_seq, tm=1024, row_off=0, n_rows=None, x_tail=None):
    n, d = x.shape
    has_delta = delta is not None
    has_mod = mod_norm is not None
    grp = functools.partial(_group_of_tile, tm=tm, n_prompt=n_prompt, dec_seq=dec_seq)
    if x_tail is None:
        n_rows = n if n_rows is None else n_rows
        toff = row_off // tm
        nft = 0
        tile_in = pl.BlockSpec((tm, d), lambda i: (i + toff, 0))
        args, specs = [x], [tile_in]
    else:
        assert row_off == 0 and n_rows is None and not has_delta
        n_rows, toff, nft = n + x_tail.shape[0], 0, n // tm
        args = [x, x_tail]
        specs = [pl.BlockSpec((tm, d), lambda i: (jnp.minimum(i, nft - 1), 0)),
                 pl.BlockSpec((tm, d), lambda i: (jnp.maximum(i - nft, 0), 0))]
    tile_out = pl.BlockSpec((tm, d), lambda i: (i, 0))
    if has_delta:
        args.append(delta)
        specs.append(tile_in)
    if has_delta or has_mod:
        mg = mod_gate if has_delta else mod_norm
        mn = mod_norm if has_mod else mod_gate
        args.append(jnp.concatenate([mg, mn], axis=1))
        specs.append(pl.BlockSpec((None, 12, d), lambda i: (grp(i + toff), 0, 0)))
    args.append(norm_w.reshape(1, d))
    specs.append(pl.BlockSpec((1, d), lambda i: (0, 0)))
    out_shape, out_specs = [], []
    if write_x:
        out_shape.append(jax.ShapeDtypeStruct((n_rows, d), F32))
        out_specs.append(tile_out)
    out_shape.append(jax.ShapeDtypeStruct((n_rows, d), h_dtype))
    out_specs.append(tile_out)
    kern = functools.partial(
        _resid_norm_kernel, has_delta=has_delta, has_mod=has_mod, gate_row=gate_row,
        mod_rows=None if mod_rows is None else (6 + mod_rows[0], 6 + mod_rows[1]), write_x=write_x,
        n_first_tiles=nft)
    outs = pl.pallas_call(
        kern, grid=(n_rows // tm,), in_specs=specs, out_specs=out_specs, out_shape=out_shape,
        compiler_params=_cparams(("parallel",), 48), name="resid_norm",
    )(*args)
    return outs if write_x else (None, outs[0])


def _rope_pair(x1, x2, cos, sin):
    return x1 * cos - x2 * sin, x1 * sin + x2 * cos


def _weight_rows_spec(layer, row_starts, tn, d):
    def index_map(j, i):
        start = sum(jnp.where(j == k, s, 0) for k, s in enumerate(row_starts))
        return layer, pl.multiple_of(start, 8), 0
    assert all(s % 8 == 0 for s in row_starts)
    return pl.BlockSpec((pl.Element(1), pl.Element(tn), pl.Element(d)), index_map)


def _proj_kernel(h_ref, w_ref, cos_ref, sin_ref, out_ref, wb_scr, *, tm, rope_tile, n_prompt, dec_seq):
    j = pl.program_id(0)
    i = pl.program_id(1)

    @pl.when(i == 0)
    def _():
        wb_scr[...] = w_ref[0].astype(BF16)

    row0 = pl.multiple_of(i * tm, tm)
    use_rope = (j == rope_tile) & (row0 >= n_prompt)
    half = HEAD_DIM // 2

    def heads(rope):
        h = h_ref[...]
        if rope:
            pos0 = pl.multiple_of((row0 - n_prompt) % dec_seq, tm)
            cos = cos_ref[pl.ds(pos0, tm), :]
            sin = sin_ref[pl.ds(pos0, tm), :]
        for hh in range(wb_scr.shape[0] // HEAD_DIM):
            c0 = hh * HEAD_DIM
            acc = _dot_nt(h, wb_scr[c0:c0 + HEAD_DIM, :])
            if rope:
                y1, y2 = _rope_pair(acc[:, :half], acc[:, half:], cos, sin)
                out_ref[:, c0:c0 + half] = y1.astype(out_ref.dtype)
                out_ref[:, c0 + half:c0 + HEAD_DIM] = y2.astype(out_ref.dtype)
            else:
                out_ref[:, c0:c0 + HEAD_DIM] = acc.astype(out_ref.dtype)

    pl.when(use_rope)(functools.partial(heads, True))
    pl.when(jnp.logical_not(use_rope))(functools.partial(heads, False))


def _proj_call(h, w_t, layer, row_starts, rope, rope_tile, *, n_prompt, dec_seq, tm=512, tn=1024):
    n, d = h.shape
    nj = len(row_starts)
    kern = functools.partial(_proj_kernel, tm=tm, rope_tile=rope_tile, n_prompt=n_prompt, dec_seq=dec_seq)
    return pl.pallas_call(
        kern, grid=(nj, n // tm),
        in_specs=[
            pl.BlockSpec((tm, d), lambda j, i: (i, 0)),
            _weight_rows_spec(layer, row_starts, tn, d),
            pl.BlockSpec(rope[0].shape, lambda j, i: (0, 0)),
            pl.BlockSpec(rope[1].shape, lambda j, i: (0, 0)),
        ],
        out_specs=pl.BlockSpec((tm, tn), lambda j, i: (i, j)),
        out_shape=jax.ShapeDtypeStruct((n, nj * tn), BF16),
        scratch_shapes=[pltpu.VMEM((tn, d), BF16)],
        compiler_params=_cparams(("parallel", "arbitrary"), 56), name="in_proj",
    )(h, w_t, *rope)


def _proj_t_kernel(h_ref, w_ref, cos_ref, sin_ref, out_ref, wb_scr, *, tm, rope_tile, n_prompt, dec_seq):
    j = pl.program_id(0)
    i = pl.program_id(1)

    @pl.when(i == 0)
    def _():
        wb_scr[...] = w_ref[0].astype(BF16)

    row0 = pl.multiple_of(i * tm, tm)
    use_rope = (j == rope_tile) & (row0 >= n_prompt)
    half = HEAD_DIM // 2

    def heads(rope):
        h = h_ref[...]
        if rope:
            blk = ((row0 - n_prompt) % dec_seq) // tm
            cos = cos_ref[blk]
            sin = sin_ref[blk]
        for hh in range(wb_scr.shape[0] // HEAD_DIM):
            r0 = hh * HEAD_DIM
            acc = _dot_nt(wb_scr[r0:r0 + HEAD_DIM, :], h)
            if rope:
                y1, y2 = _rope_pair(acc[:half, :], acc[half:, :], cos, sin)
                y = jnp.concatenate([y1, y2], axis=0).astype(out_ref.dtype)
            else:
                y = acc.astype(out_ref.dtype)
            for s in range(tm // CHUNK):
                out_ref[s, r0:r0 + HEAD_DIM, :] = y[:, s * CHUNK:(s + 1) * CHUNK]

    pl.when(use_rope)(functools.partial(heads, True))
    pl.when(jnp.logical_not(use_rope))(functools.partial(heads, False))


def _proj_t_call(h, w_t, layer, row_starts, rope_t, rope_tile, *, n_prompt, dec_seq, tm=512, tn=1024):
    n, d = h.shape
    nj = len(row_starts)
    kern = functools.partial(_proj_t_kernel, tm=tm, rope_tile=rope_tile, n_prompt=n_prompt, dec_seq=dec_seq)
    return pl.pallas_call(
        kern, grid=(nj, n // tm),
        in_specs=[
            pl.BlockSpec((tm, d), lambda j, i: (i, 0)),
            _weight_rows_spec(layer, row_starts, tn, d),
            pl.BlockSpec(rope_t[0].shape, lambda j, i: (0, 0, 0)),
            pl.BlockSpec(rope_t[1].shape, lambda j, i: (0, 0, 0)),
        ],
        out_specs=pl.BlockSpec((tm // CHUNK, tn, CHUNK), lambda j, i: (i, j, 0)),
        out_shape=jax.ShapeDtypeStruct((n // CHUNK, nj * tn, CHUNK), BF16),
        scratch_shapes=[pltpu.VMEM((tn, d), BF16)],
        compiler_params=_cparams(("parallel", "arbitrary"), 56), name="in_proj_t",
    )(h, w_t, *rope_t)


def _gate_kernel(h_ref, wg_ref, bias_ref, out_ref, *, tm, n_heads):
    g = _dot_nt(wg_ref[...].astype(BF16), h_ref[...]) + bias_ref[...]
    nd = 2 * n_heads
    ig = g[0:nd, :]
    lf = _log_sigmoid(g[nd:2 * nd, :])
    r = _iota((CHUNK, CHUNK), 0)
    c = _iota((CHUNK, CHUNK), 1)
    upper = jnp.where(r <= c, 1.0, 0.0).astype(BF16)
    lower = jnp.where(r >= c, 1.0, 0.0).astype(BF16)
    is_fwd = _iota((nd, CHUNK), 0) < n_heads
    for s in range(tm // CHUNK):
        sl = slice(s * CHUNK, (s + 1) * CHUNK)
        lf_c = lf[:, sl]
        b = jnp.where(is_fwd, _dot3(lf_c, upper), _dot3(lf_c, lower))
        ig_c = ig[:, sl]
        for hh in range(n_heads):
            out_ref[hh, s, 0:1, :] = ig_c[hh:hh + 1, :]
            out_ref[hh, s, 1:2, :] = ig_c[n_heads + hh:n_heads + hh + 1, :]
            out_ref[hh, s, 2:3, :] = b[hh:hh + 1, :]
            out_ref[hh, s, 3:4, :] = b[n_heads + hh:n_heads + hh + 1, :]


def _gate_call(h, wg_t, bias_col, n_heads, tm=512):
    n, d = h.shape
    ng = 4 * n_heads
    return pl.pallas_call(
        functools.partial(_gate_kernel, tm=tm, n_heads=n_heads),
        grid=(n // tm,),
        in_specs=[
            pl.BlockSpec((tm, d), lambda i: (i, 0)),
            pl.BlockSpec((ng, d), lambda i: (0, 0)),
            pl.BlockSpec((ng, 1), lambda i: (0, 0)),
        ],
        out_specs=pl.BlockSpec((n_heads, tm // CHUNK, 4, CHUNK), lambda i: (0, i, 0, 0)),
        out_shape=jax.ShapeDtypeStruct((n_heads, n // CHUNK, 4, CHUNK), F32),
        compiler_params=_cparams(("parallel",), 32), name="mlstm_gates",
    )(h, wg_t, bias_col)


def _scan_loop(nc, body, init):
    if nc <= 2:
        carry = init
        for ci in range(nc):
            carry = body(ci, carry)
        return carry
    return lax.fori_loop(0, nc, body, init)


MAX_GROUP = 16
BF16_ROWS = 16
STRIP = 64


def _seqs_per_step(n_seq, nc):
    ns = max(1, MAX_GROUP // nc)
    while n_seq % ns:
        ns -= 1
    return ns


def _group_loop(n_groups, body):
    if n_groups == 1:
        body(0, 0)
    else:
        lax.fori_loop(0, n_groups, body, 0)


def _chunk_off(c):
    return c * CHUNK if isinstance(c, int) else pl.multiple_of(c * CHUNK, CHUNK)


def _mlstm_kernel(*refs, nc, ns, has_init, first_layer, write_state):
    it = iter(refs)
    qt_ref, vt_ref, ot_ref, k_ref, gp_ref, nw_ref = (next(it) for _ in range(6))
    if has_init:
        c0_ref, n0_ref, m0_ref = next(it), next(it), next(it)
    if write_state and not first_layer:
        next(it), next(it), next(it)
    hm_ref = next(it)
    if write_state:
        cout_ref, nout_ref, mout_ref = next(it), next(it), next(it)
    ct_scr, n_scr, part_scr, inc_scr, ninc_scr, rows_scr, cst_scr, nst_scr, coef_scr = (next(it) for _ in range(9))

    L = CHUNK
    grp = min(MAX_GROUP, ns * nc)
    row = _iota((L, L), 0)
    col = _iota((L, L), 1)
    k_scale = HEAD_DIM ** -0.5
    b_idx = pl.program_id(0)
    h_idx = pl.program_id(1)
    n_heads = pl.num_programs(1)

    def gates(d, c):
        g = gp_ref[c]
        ig = g[d:d + 1, :]
        brow = g[2 + d:3 + d, :]
        blast = brow[:, L - 1:L] if d == 0 else brow[:, 0:1]
        return ig, brow, blast

    def local_group(g, carry):
        cs = [g * grp + u for u in range(grp)]
        pairs = [(u, d) for u in range(grp) for d in range(2)]
        qt = [qt_ref[c] for c in cs]
        vt = [vt_ref[c] for c in cs]
        k = [k_ref[pl.ds(_chunk_off(c), L), :] for c in cs]
        qk = [_dot(k[u], qt[u]) for u in range(grp)]
        vf = [v.astype(F32) for v in vt]
        gts = {(u, d): gates(d, cs[u]) for u, d in pairs}
        for u, c in enumerate(cs):
            lhs = []
            for d in range(2):
                ig, brow, blast = gts[u, d]
                wj = blast - brow + ig
                mloc2 = jnp.max(wj, axis=1, keepdims=True)
                e = jnp.exp(wj - mloc2) * k_scale
                rows_scr[d, c, 2:3, :] = jnp.broadcast_to(mloc2, (1, L))
                rows_scr[d, c, 4:5, :] = jnp.broadcast_to(blast, (1, L))
                lhs.append(((vf[u] * e).astype(BF16), _split3(jnp.broadcast_to(e, (BF16_ROWS, L)))))
            res = _dot(jnp.concatenate([lhs[0][0], lhs[1][0], *lhs[0][1], *lhs[1][1]], axis=0), k[u])
            for d in range(2):
                inc_scr[d, c] = res[d * HEAD_DIM:(d + 1) * HEAD_DIM, :]
                r0 = 2 * HEAD_DIM + 3 * BF16_ROWS * d
                ninc_scr[d, c] = (res[r0:r0 + 8, :] + res[r0 + BF16_ROWS:r0 + BF16_ROWS + 8, :]
                                  + res[r0 + 2 * BF16_ROWS:r0 + 2 * BF16_ROWS + 8, :])
        for u, d in pairs:
            c = cs[u]
            ig, brow, blast = gts[u, d]
            key_term = jnp.broadcast_to(ig - brow, (L, L)).T
            causal = (row <= col) if d == 0 else (row >= col)
            dm = jnp.where(causal, key_term + brow, -jnp.inf)
            mloc = jnp.max(dm, axis=0, keepdims=True)
            s = qk[u] * (jnp.exp(dm - mloc) * k_scale)
            part_scr[d, c] = _dot(vt[u], s.astype(BF16))
            rows_scr[d, c, 0:1, :] = jnp.sum(s, axis=0, keepdims=True)
            rows_scr[d, c, 1:2, :] = mloc
        return carry

    _group_loop(ns * nc // grp, local_group)

    def m_step(d, c, m):
        blast = rows_scr[d, c, 4:5, :]
        mloc2 = rows_scr[d, c, 2:3, :]
        rows_scr[d, c, 3:4, :] = m
        m_new = jnp.maximum(blast + m, mloc2)
        decay = jnp.exp(blast + m - m_new)
        w_inc = jnp.exp(mloc2 - m_new)
        coef_scr[d, c, 0:1, :] = jnp.concatenate([decay] * (HEAD_DIM // L), axis=1)
        coef_scr[d, c, 1:2, :] = jnp.concatenate([w_inc] * (HEAD_DIM // L), axis=1)
        return m_new

    for sq in range(ns):
        c_lo = sq * nc
        m_init = []
        for d in range(2):
            if has_init:
                ct_scr[d] = c0_ref[d].T
                n_scr[d] = jnp.broadcast_to(n0_ref[d], n_scr.shape[1:])
                m_init.append(jnp.full((1, L), m0_ref[(b_idx * 2 + d) * n_heads + h_idx], F32))
            else:
                ct_scr[d] = jnp.zeros(ct_scr.shape[1:], F32)
                n_scr[d] = jnp.zeros(n_scr.shape[1:], F32)
                m_init.append(jnp.zeros((1, L), F32))

        def m_body(ci, ms, c_lo=c_lo):
            return m_step(0, c_lo + ci, ms[0]), m_step(1, c_lo + nc - 1 - ci, ms[1])

        m_fin = _scan_loop(nc, m_body, tuple(m_init))

        def body(ci, carry, c_lo=c_lo):
            for d, c in ((0, c_lo + ci), (1, c_lo + nc - 1 - ci)):
                decay = coef_scr[d, c, 0:1, :]
                w_inc = coef_scr[d, c, 1:2, :]
                n8 = n_scr[d]
                nst_scr[d, c] = n8
                for r0 in range(0, HEAD_DIM, STRIP):
                    ct = ct_scr[d, r0:r0 + STRIP, :]
                    cst_scr[d, c, r0:r0 + STRIP, :] = ct.astype(BF16)
                    ct_scr[d, r0:r0 + STRIP, :] = decay * ct + w_inc * inc_scr[d, c, r0:r0 + STRIP, :]
                n_scr[d] = decay * n8 + w_inc * ninc_scr[d, c]
            return carry

        _scan_loop(nc, body, 0)

        if write_state:
            lsel = (lambda d, sq=sq: (sq, 0, d)) if first_layer else (lambda d, sq=sq: (sq, d))
            for d in range(2):
                cout_ref[lsel(d)] = ct_scr[d].T
                nout_ref[lsel(d)] = n_scr[d][0:1, :]
                mout_ref[lsel(d)] = m_fin[d]
            if first_layer:
                for l in range(1, cout_ref.shape[1]):
                    cout_ref[sq, l] = jnp.zeros(cout_ref.shape[2:], F32)
                    nout_ref[sq, l] = jnp.zeros(nout_ref.shape[2:], F32)
                    mout_ref[sq, l] = jnp.zeros(mout_ref.shape[2:], F32)

    def finish_group(g, carry):
        cs = [g * grp + u for u in range(grp)]
        qt = [qt_ref[c] for c in cs]
        cq, nq = [], []
        for u, c in enumerate(cs):
            n16 = [jnp.concatenate([nst_scr[d, c]] * (BF16_ROWS // 8), axis=0).astype(BF16) for d in range(2)]
            res = _dot(jnp.concatenate([cst_scr[0, c], cst_scr[1, c], n16[0], n16[1]], axis=0), qt[u])
            cq.append([res[d * HEAD_DIM:(d + 1) * HEAD_DIM, :] for d in range(2)])
            nq.append([res[2 * HEAD_DIM + d * BF16_ROWS:2 * HEAD_DIM + d * BF16_ROWS + 1, :] for d in range(2)])
        for u, c in enumerate(cs):
            coef = []
            for d in range(2):
                _, brow, _ = gates(d, c)
                den_loc = rows_scr[d, c, 0:1, :]
                mloc = rows_scr[d, c, 1:2, :]
                inter = brow + rows_scr[d, c, 3:4, :]
                m_row = jnp.maximum(mloc, inter)
                w_loc = jnp.exp(mloc - m_row)
                w_inter = jnp.exp(inter - m_row)
                den = w_loc * den_loc + w_inter * nq[u][d]
                r_den = 1.0 / jnp.maximum(jnp.abs(den), jnp.exp(-m_row))
                coef.append((w_loc * r_den, w_inter * r_den))
            ssq = jnp.zeros((1, L), F32)
            for r0 in range(0, HEAD_DIM, STRIP):
                h_s = None
                for d in range(2):
                    h_d = coef[d][0] * part_scr[d, c, r0:r0 + STRIP, :] + coef[d][1] * cq[u][d][r0:r0 + STRIP, :]
                    h_s = h_d if h_s is None else h_s + h_d
                part_scr[0, c, r0:r0 + STRIP, :] = h_s
                ssq = ssq + jnp.sum(h_s * h_s, axis=0, keepdims=True)
            r_norm = lax.rsqrt(ssq * (1.0 / HEAD_DIM) + EPS)
            for r0 in range(0, HEAD_DIM, L):
                y = part_scr[0, c, r0:r0 + L, :] * r_norm * nw_ref[r0:r0 + L, :]
                o = jax.nn.sigmoid(ot_ref[c, r0:r0 + L, :].astype(F32)) * y
                hm_ref[pl.ds(_chunk_off(c), L), r0:r0 + L] = o.T.astype(hm_ref.dtype)
        return carry

    _group_loop(ns * nc // grp, finish_group)


def _state_specs(shapes, layer, first_layer, ns):
    specs = []
    for shp in shapes:
        tail = shp[4:]
        zeros = (0,) * len(tail)
        if first_layer:
            specs.append(pl.BlockSpec((ns, shp[1], 2, None) + tail, lambda b, h, z=zeros: (b, 0, 0, h) + z))
        else:
            specs.append(pl.BlockSpec((ns, None, 2, None) + tail, lambda b, h, z=zeros: (b, layer, 0, h) + z))
    return specs


def _mlstm_call(proj_t, proj_n, gp, nw_b, init, state_bufs, *, n_seq, seq_len, tok_off, n_heads, layer, depth):
    dk = HEAD_DIM
    nc = seq_len // CHUNK
    has_init = init is not None
    ns = 1 if has_init else _seqs_per_step(n_seq, nc)
    nct = ns * nc
    boff = tok_off // (ns * seq_len)
    write_state = not has_init
    first_layer = layer == 0
    args = [proj_t, proj_t, proj_t, proj_n, gp, nw_b]
    specs = [
        pl.BlockSpec((nct, dk, CHUNK), lambda b, h: (b + boff, h, 0)),
        pl.BlockSpec((nct, dk, CHUNK), lambda b, h: (b + boff, n_heads + h, 0)),
        pl.BlockSpec((nct, dk, CHUNK), lambda b, h: (b + boff, 2 * n_heads + h, 0)),
        pl.BlockSpec((ns * seq_len, dk), lambda b, h: (b + boff, h)),
        pl.BlockSpec((None, nct, 4, CHUNK), lambda b, h: (h, b + boff, 0, 0)),
        pl.BlockSpec((dk, LANES), lambda b, h: (h, 0)),
    ]
    aliases = {}
    if has_init:
        c0, n0, m0 = init
        args += [c0, n0[:, layer][:, :, :, None, :], m0[:, layer].reshape(-1)]
        specs += [
            pl.BlockSpec((None, None, 2, None, dk, dk), lambda b, h: (b, layer, 0, h, 0, 0)),
            pl.BlockSpec((None, 2, None, 1, dk), lambda b, h: (b, 0, h, 0, 0)),
            pl.BlockSpec(memory_space=pltpu.SMEM),
        ]
    out_shape = [jax.ShapeDtypeStruct((n_seq * seq_len, n_heads * dk), BF16)]
    out_specs = [pl.BlockSpec((ns * seq_len, dk), lambda b, h: (b, h))]
    if write_state:
        shapes = [(n_seq, depth, 2, n_heads, dk, dk), (n_seq, depth, 2, n_heads, 1, dk),
                  (n_seq, depth, 2, n_heads, 1, LANES)]
        if not first_layer:
            for k_, buf in enumerate(state_bufs):
                aliases[len(args)] = 1 + k_
                args.append(buf)
                specs.append(pl.BlockSpec(memory_space=pl.ANY))
        out_shape += [jax.ShapeDtypeStruct(s, F32) for s in shapes]
        out_specs += _state_specs(shapes, layer, first_layer, ns)
    kern = functools.partial(_mlstm_kernel, nc=nc, ns=ns, has_init=has_init, first_layer=first_layer,
                             write_state=write_state)
    return pl.pallas_call(
        kern, grid=(n_seq // ns, n_heads), in_specs=specs, out_specs=out_specs, out_shape=out_shape,
        input_output_aliases=aliases,
        scratch_shapes=[pltpu.VMEM((2, dk, dk), F32), pltpu.VMEM((2, 8, dk), F32),
                        pltpu.VMEM((2, nct, dk, CHUNK), F32), pltpu.VMEM((2, nct, dk, dk), F32),
                        pltpu.VMEM((2, nct, 8, dk), F32), pltpu.VMEM((2, nct, 8, CHUNK), F32),
                        pltpu.VMEM((2, nct, dk, dk), BF16), pltpu.VMEM((2, nct, 8, dk), F32),
                        pltpu.VMEM((2, nct, 8, dk), F32)],
        compiler_params=_cparams(("parallel", "parallel"), 48), name="mlstm_mixer",
    )(*args)


def _ret_kernel(*refs, nc, ns, has_init, first_layer, write_state):
    it = iter(refs)
    dl_ref, qt_ref, vt_ref, gt_ref, k_ref, nw_ref = (next(it) for _ in range(6))
    if has_init:
        s0_ref = next(it)
    if write_state and not first_layer:
        next(it)
    hr_ref = next(it)
    if write_state:
        sout_ref = next(it)
    st_scr, part_scr, inc_scr, sst_scr = next(it), next(it), next(it), next(it)

    L = CHUNK
    grp = min(MAX_GROUP, ns * nc)
    rowi = _iota((L, L), 0)
    coli = _iota((L, L), 1)
    k_scale = HEAD_DIM ** -0.5
    h_idx = pl.program_id(1)
    n_heads = pl.num_programs(1)

    consts = []
    for d in range(2):
        lg = _log_sigmoid(jnp.full((1, 1), dl_ref[d * n_heads + h_idx], F32))
        rel = (coli - rowi if d == 0 else rowi - coli).astype(F32)
        dmat_t = jnp.where(rel >= 0, jnp.exp(jnp.maximum(rel, 0.0) * lg), 0.0) * k_scale
        pos = _iota((1, L), 1).astype(F32)
        if d == 1:
            pos = (L - 1.0) - pos
        q_decay = jnp.exp((pos + 1.0) * lg)
        k_decay = jnp.exp((L - 1.0 - pos) * lg) * k_scale
        chunk_decay = jnp.exp(float(L) * lg)
        consts.append((dmat_t, q_decay, k_decay, chunk_decay))
    dmat_both = consts[0][0] + consts[1][0]

    def local_group(g, carry):
        cs = [g * grp + u for u in range(grp)]
        qt = [qt_ref[c] for c in cs]
        vt = [vt_ref[c] for c in cs]
        k = [k_ref[pl.ds(_chunk_off(c), L), :] for c in cs]
        qk = [_dot(k[u], qt[u]) for u in range(grp)]
        for u, c in enumerate(cs):
            vf = vt[u].astype(F32)
            res = _dot(jnp.concatenate([(vf * consts[d][2]).astype(BF16) for d in range(2)], axis=0), k[u])
            for d in range(2):
                inc_scr[d, c] = res[d * HEAD_DIM:(d + 1) * HEAD_DIM, :]
        for u, c in enumerate(cs):
            part_scr[c] = _dot(vt[u], (qk[u] * dmat_both).astype(BF16))
        return carry

    _group_loop(ns * nc // grp, local_group)

    for sq in range(ns):
        c_lo = sq * nc
        for d in range(2):
            if has_init:
                st_scr[d] = s0_ref[d].T
            else:
                st_scr[d] = jnp.zeros(st_scr.shape[1:], F32)

        def body(ci, carry, c_lo=c_lo):
            for d, c in ((0, c_lo + ci), (1, c_lo + nc - 1 - ci)):
                for r0 in range(0, HEAD_DIM, STRIP):
                    st = st_scr[d, r0:r0 + STRIP, :]
                    sst_scr[d, c, r0:r0 + STRIP, :] = st.astype(BF16)
                    st_scr[d, r0:r0 + STRIP, :] = consts[d][3] * st + inc_scr[d, c, r0:r0 + STRIP, :]
            return carry

        _scan_loop(nc, body, 0)

        if write_state:
            for d in range(2):
                if first_layer:
                    sout_ref[sq, 0, d] = st_scr[d].T
                else:
                    sout_ref[sq, d] = st_scr[d].T
            if first_layer:
                for l in range(1, sout_ref.shape[1]):
                    sout_ref[sq, l] = jnp.zeros(sout_ref.shape[2:], F32)

    def finish_group(g, carry):
        cs = [g * grp + u for u in range(grp)]
        qt = [qt_ref[c] for c in cs]
        sq = [_dot(jnp.concatenate([sst_scr[0, c], sst_scr[1, c]], axis=0), qt[u]) for u, c in enumerate(cs)]
        for u, c in enumerate(cs):
            ssq = jnp.zeros((1, L), F32)
            for r0 in range(0, HEAD_DIM, STRIP):
                sl = slice(r0, r0 + STRIP)
                o_s = part_scr[c, sl, :] + (consts[0][1] * sq[u][r0:r0 + STRIP, :]
                                            + consts[1][1] * sq[u][HEAD_DIM + r0:HEAD_DIM + r0 + STRIP, :])
                part_scr[c, sl, :] = o_s
                ssq = ssq + jnp.sum(o_s * o_s, axis=0, keepdims=True)
            r_norm = lax.rsqrt(ssq * (1.0 / HEAD_DIM) + EPS)
            for r0 in range(0, HEAD_DIM, L):
                y = part_scr[c, r0:r0 + L, :] * r_norm * nw_ref[r0:r0 + L, :]
                rg = gt_ref[c, r0:r0 + L, :].astype(F32)
                hr_ref[pl.ds(_chunk_off(c), L), r0:r0 + L] = (rg * jax.nn.sigmoid(rg) * y).T.astype(hr_ref.dtype)
        return carry

    _group_loop(ns * nc // grp, finish_group)


def _ret_call(proj_t, proj_n, decay_logit, nw_b, init, state_buf, *, n_seq, seq_len, tok_off, n_heads, layer,
              depth):
    dk = HEAD_DIM
    nc = seq_len // CHUNK
    has_init = init is not None
    ns = 1 if has_init else _seqs_per_step(n_seq, nc)
    nct = ns * nc
    boff = tok_off // (ns * seq_len)
    write_state = not has_init
    first_layer = layer == 0
    args = [decay_logit.reshape(-1), proj_t, proj_t, proj_t, proj_n, nw_b]
    specs = [
        pl.BlockSpec(memory_space=pltpu.SMEM),
        pl.BlockSpec((nct, dk, CHUNK), lambda b, h: (b + boff, 3 * n_heads + h, 0)),
        pl.BlockSpec((nct, dk, CHUNK), lambda b, h: (b + boff, 4 * n_heads + h, 0)),
        pl.BlockSpec((nct, dk, CHUNK), lambda b, h: (b + boff, 5 * n_heads + h, 0)),
        pl.BlockSpec((ns * seq_len, dk), lambda b, h: (b + boff, n_heads + h)),
        pl.BlockSpec((dk, LANES), lambda b, h: (h, 0)),
    ]
    aliases = {}
    if has_init:
        args.append(init)
        specs.append(pl.BlockSpec((None, None, 2, None, dk, dk), lambda b, h: (b, layer, 0, h, 0, 0)))
    out_shape = [jax.ShapeDtypeStruct((n_seq * seq_len, n_heads * dk), BF16)]
    out_specs = [pl.BlockSpec((ns * seq_len, dk), lambda b, h: (b, h))]
    if write_state:
        shapes = [(n_seq, depth, 2, n_heads, dk, dk)]
        if not first_layer:
            aliases[len(args)] = 1
            args.append(state_buf)
            specs.append(pl.BlockSpec(memory_space=pl.ANY))
        out_shape += [jax.ShapeDtypeStruct(s, F32) for s in shapes]
        out_specs += _state_specs(shapes, layer, first_layer, ns)
    kern = functools.partial(_ret_kernel, nc=nc, ns=ns, has_init=has_init, first_layer=first_layer,
                             write_state=write_state)
    return pl.pallas_call(
        kern, grid=(n_seq // ns, n_heads), in_specs=specs, out_specs=out_specs, out_shape=out_shape,
        input_output_aliases=aliases,
        scratch_shapes=[pltpu.VMEM((2, dk, dk), F32), pltpu.VMEM((nct, dk, CHUNK), F32),
                        pltpu.VMEM((2, nct, dk, dk), F32), pltpu.VMEM((2, nct, dk, dk), BF16)],
        compiler_params=_cparams(("parallel", "parallel"), 48), name="retention_mixer",
    )(*args)


def _merge_kernel(hmp_ref, hms_ref, hrp_ref, hrs_ref, ga_ref, gb_ref, x_ref, mod_ref, nw_ref, wa_ref, wb_ref,
                  wo_ref, wr_ref, x1_ref, h2_ref, aff_ref, wab_scr, wbb_scr, wob_scr, *, n_prompt_tiles):
    i = pl.program_id(0)

    @pl.when(i == 0)
    def _():
        wab_scr[...] = wa_ref[...].astype(BF16)
        wbb_scr[...] = wb_ref[...].astype(BF16)
        wob_scr[...] = wo_ref[...].astype(BF16)

    is_prompt = i < n_prompt_tiles
    hm = jnp.where(is_prompt, hmp_ref[...], hms_ref[...])
    hr = jnp.where(is_prompt, hrp_ref[...], hrs_ref[...])
    ya = _dot(hm, wab_scr[...])
    yb = _dot(hr, wbb_scr[...])
    merged = (jax.nn.sigmoid(ga_ref[...].astype(F32)) * ya + jax.nn.sigmoid(gb_ref[...].astype(F32)) * yb)
    y = _dot(merged.astype(BF16), wob_scr[...])
    x1 = x_ref[...] + mod_ref[2:3, :] * y
    x1_ref[...] = x1
    h2 = x1 * lax.rsqrt(jnp.mean(x1 * x1, axis=-1, keepdims=True) + EPS) * nw_ref[...]
    h2 = h2 * (1.0 + mod_ref[4:5, :]) + mod_ref[3:4, :]
    n_blk = h2.shape[1] // LANES
    for fb in range(n_blk):
        _tile_rows(h2_ref, fb, h2.shape[0], n_blk)[...] = h2[:, fb * LANES:(fb + 1) * LANES]
    logits = _dot_nt(wr_ref[...].astype(BF16), h2.astype(BF16))
    p = jnp.exp(logits - jnp.max(logits, axis=0, keepdims=True))
    aff_ref[...] = p / jnp.sum(p, axis=0, keepdims=True)


def _merge_call(hm_p, hm_s, hr_p, hr_s, proj_n, x, mod, norm_w, w_a, w_b, w_out, router_wt, layer, *,
                n_prompt, dec_seq, tm=512):
    n, d = x.shape
    ne = router_wt.shape[1]
    npt = n_prompt // tm
    grp = functools.partial(_group_of_tile, tm=tm, n_prompt=n_prompt, dec_seq=dec_seq)
    tile = pl.BlockSpec((tm, d), lambda i: (i, 0))
    tile_p = pl.BlockSpec((tm, d), lambda i: (jnp.minimum(i, npt - 1), 0))
    tile_s = pl.BlockSpec((tm, d), lambda i: (jnp.maximum(i - npt, 0), 0))
    full = pl.BlockSpec((None, d, d), lambda i: (layer, 0, 0))
    return pl.pallas_call(
        functools.partial(_merge_kernel, n_prompt_tiles=npt), grid=(n // tm,),
        in_specs=[
            tile_p, tile_s, tile_p, tile_s,
            pl.BlockSpec((tm, d), lambda i: (i, 2)),
            pl.BlockSpec((tm, d), lambda i: (i, 3)),
            tile,
            pl.BlockSpec((None, 6, d), lambda i: (grp(i), 0, 0)),
            pl.BlockSpec((1, d), lambda i: (0, 0)),
            full, full, full,
            pl.BlockSpec((None, ne, d), lambda i: (layer, 0, 0)),
        ],
        out_specs=[tile, pl.BlockSpec((tm * (d // LANES), LANES), lambda i: (i, 0)),
                   pl.BlockSpec((ne, tm), lambda i: (0, i))],
        out_shape=[jax.ShapeDtypeStruct((n, d), F32), jax.ShapeDtypeStruct((n * (d // LANES), LANES), F32),
                   jax.ShapeDtypeStruct((ne, n), F32)],
        scratch_shapes=[pltpu.VMEM((d, d), BF16)] * 3,
        compiler_params=_cparams(("arbitrary",), 56), name="merge_out_router",
    )(hm_p, hm_s, hr_p, hr_s, proj_n, proj_n, x, mod, norm_w.reshape(1, d), w_a, w_b, w_out, router_wt)


ROUTE_GROUP = 4


def _route_kernel(a_ref, at_ref, idx_ref, gate_ref, thr_scr, *, cap, n_tok):
    n_sets = a_ref.shape[0]
    nb = n_tok // LANES
    a_all = a_ref[...]

    def as_f32(bits):
        return lax.bitcast_convert_type(bits, F32)

    def count_ge(cand):
        m = jnp.where(a_all >= cand, 1.0, 0.0)
        return jnp.sum(jnp.sum(m, axis=1, keepdims=True), axis=2, keepdims=True)

    def bit_step(k, thr):
        cand = thr | lax.shift_left(jnp.int32(1), 30 - k)
        return jnp.where(count_ge(as_f32(cand)) >= cap, cand, thr)

    thr_scr[...] = lax.fori_loop(0, 31, bit_step, jnp.zeros((n_sets, 1, 1), jnp.int32))

    r128 = _iota((LANES, LANES), 0)
    c128 = _iota((LANES, LANES), 1)
    upper = jnp.where(r128 <= c128, 1.0, 0.0).astype(BF16)
    lower_t = jnp.where(r128 >= c128, 1.0, 0.0).astype(BF16)
    rb = _iota((nb, nb), 0)
    cb = _iota((nb, nb), 1)
    blk_before_rows = jnp.where(cb < rb, 1.0, 0.0).astype(BF16)
    blk_before_cols = jnp.where(rb < cb, 1.0, 0.0).astype(BF16)

    def incl_counts(mask):
        within = _dot(mask.astype(BF16), upper)
        before = _dot(blk_before_rows, within.astype(BF16))[:, LANES - 1:LANES]
        return within + before

    def incl_counts_t(mask_t):
        within = _dot(lower_t, mask_t.astype(BF16))
        before = _dot(within.astype(BF16), blk_before_cols)[LANES - 1:LANES, :]
        return within + before

    slot = _iota((1, cap), 1).astype(F32)
    blk_col = _iota((nb, 1), 0).astype(F32)
    sub_col = _iota((LANES, 1), 0).astype(F32)

    def per_group(g, carry):
        sets = [g + u * (n_sets // ROUTE_GROUP) for u in range(ROUTE_GROUP)]
        stage = []
        for s in sets:
            thr_bits = thr_scr[s]
            thr = as_f32(thr_bits)
            nxt = as_f32(thr_bits + 1)
            a = a_ref[s]
            a_t = at_ref[s]
            gt = jnp.where(a >= nxt, 1.0, 0.0)
            eq = jnp.where((a >= thr) & (a < nxt), 1.0, 0.0)
            gt_t = jnp.where(a_t >= nxt, 1.0, 0.0)
            eq_t = jnp.where((a_t >= thr) & (a_t < nxt), 1.0, 0.0)
            n_gt = jnp.sum(jnp.sum(gt, axis=1, keepdims=True), axis=0, keepdims=True)
            stage.append((a_t, gt, eq, gt_t, eq_t, cap - n_gt))
        ties = [(incl_counts(eq), incl_counts_t(eq_t)) for _, _, eq, _, eq_t, _ in stage]
        sels = []
        for (a_t, gt, eq, gt_t, eq_t, need), (c_eq, c_eq_t) in zip(stage, ties):
            sels.append((gt + eq * jnp.where(c_eq - eq < need, 1.0, 0.0),
                         gt_t + eq_t * jnp.where(c_eq_t - eq_t < need, 1.0, 0.0)))
        cnts = [(incl_counts(sel), incl_counts_t(sel_t)) for sel, sel_t in sels]
        picks = []
        for (a_t, *_), (cnt, cnt_t) in zip(stage, cnts):
            blk_end = cnt[:, LANES - 1:LANES]
            blk_of_slot = jnp.sum(jnp.where(blk_end <= slot, 1.0, 0.0), axis=0, keepdims=True)
            onehot_blk = jnp.where(blk_col == blk_of_slot, 1.0, 0.0).astype(BF16)
            cnt_rows = _dot3(cnt_t, onehot_blk)
            a_rows = _dot3(a_t, onehot_blk)
            picks.append((blk_of_slot, cnt_rows, a_rows))
        for s, (blk_of_slot, cnt_rows, a_rows) in zip(sets, picks):
            sub_of_slot = jnp.sum(jnp.where(cnt_rows <= slot, 1.0, 0.0), axis=0, keepdims=True)
            gate = jnp.sum(jnp.where(sub_col == sub_of_slot, a_rows, 0.0), axis=0, keepdims=True)
            idx_ref[s] = (blk_of_slot * LANES + sub_of_slot).astype(jnp.int32)
            gate_ref[s] = gate
        return carry

    lax.fori_loop(0, n_sets // ROUTE_GROUP, per_group, 0)


def _route_call(aff_t, n_pass, cap):
    ne, n = aff_t.shape
    n_tok = n // n_pass
    nb = n_tok // LANES
    a4 = aff_t.reshape(ne, n_pass, nb, LANES).transpose(1, 0, 2, 3).reshape(n_pass * ne, nb, LANES)
    a4_t = a4.transpose(0, 2, 1)
    n_sets = n_pass * ne
    return pl.pallas_call(
        functools.partial(_route_kernel, cap=cap, n_tok=n_tok),
        out_shape=[jax.ShapeDtypeStruct((n_sets, 1, cap), jnp.int32),
                   jax.ShapeDtypeStruct((n_sets, 1, cap), F32)],
        scratch_shapes=[pltpu.VMEM((n_sets, 1, 1), jnp.int32)],
        compiler_params=pltpu.CompilerParams(vmem_limit_bytes=32 * MIB), name="expert_choice_route",
    )(a4, a4_t)


ROW_LOOP_UNROLL = 8


def _moe_kernel(idx_ref, gate_ref, h_hbm, w1_ref, w3_ref, w2_ref, out_hbm,
                xe_scr, xb_scr, ye_a, ye_b, acc_scr, gsem, osem, *, rows, rows_pad):
    e = pl.program_id(0)
    f = pl.program_id(1)
    n_e = pl.num_programs(0)
    n_f = pl.num_programs(1)
    chunk = rows_pad // n_f

    n_blk = xb_scr.shape[1] // LANES

    def tile_of(t):
        return pl.ds(pl.multiple_of(t * n_blk, n_blk), n_blk)

    def gather_row(slot, s):
        tok = idx_ref[slot * rows_pad + s]
        pltpu.make_async_copy(h_hbm.at[tile_of(tok), :], xe_scr.at[tile_of(s), :], gsem).start()

    def scatter_row(slot, s, ye_ref):
        tok = idx_ref[slot * rows_pad + s]
        g = gate_ref[slot * rows_pad + s]
        acc_scr[pl.ds(tok, 1), :] += ye_ref[pl.ds(s, 1), :] * g

    def wait_gather():
        pltpu.make_async_copy(h_hbm.at[pl.ds(0, rows_pad * n_blk), :], xe_scr, gsem).wait()

    @pl.when((e == 0) & (f == 0))
    def _():
        acc_scr[...] = jnp.zeros_like(acc_scr)
        ye_a[...] = jnp.zeros_like(ye_a)
        ye_b[...] = jnp.zeros_like(ye_b)

        def issue(s, carry):
            gather_row(1, s)
            return carry
        lax.fori_loop(0, rows_pad, issue, 0, unroll=ROW_LOOP_UNROLL)

    def step(ye_cur, ye_prev):
        @pl.when(f == 0)
        def _():
            wait_gather()
            for fb in range(n_blk):
                xb_scr[:, fb * LANES:(fb + 1) * LANES] = _tile_rows(xe_scr, fb, rows, n_blk)[...].astype(BF16)
            ye_cur[0:rows, :] = jnp.zeros((rows, ye_cur.shape[1]), F32)

        xb = xb_scr[...]
        h1 = _dot(xb, w1_ref[...].astype(BF16))
        h3 = _dot(xb, w3_ref[...].astype(BF16))
        he = (h1 * jax.nn.sigmoid(h1) * h3).astype(BF16)
        ye_cur[0:rows, :] += _dot(he, w2_ref[...].astype(BF16))
        base = f * chunk
        for r in range(chunk):
            gather_row(e + 2, base + r)
        for r in range(chunk):
            scatter_row(e, base + r, ye_prev)

        @pl.when((e == n_e - 1) & (f == n_f - 1))
        def _():
            def scatter(s, carry):
                scatter_row(n_e, s, ye_cur)
                return carry
            lax.fori_loop(0, rows, scatter, 0, unroll=ROW_LOOP_UNROLL)
            wait_gather()
            cp = pltpu.make_async_copy(acc_scr, out_hbm, osem)
            cp.start()
            cp.wait()

    @pl.when(e % 2 == 0)
    def _():
        step(ye_a, ye_b)

    @pl.when(e % 2 == 1)
    def _():
        step(ye_b, ye_a)


def _moe_call(idx, gate, h2, w1, w3, w2, layer, tf=256):
    d = w1.shape[2]
    n_blk = d // LANES
    n = h2.shape[0] // n_blk
    ne, rows = idx.shape
    dff = w1.shape[3]
    nf = dff // tf
    rows_pad = -(-rows // (8 * nf)) * 8 * nf
    pad = ((1, 1), (0, rows_pad - rows))
    idx_all = jnp.pad(idx, pad).reshape(-1)
    gate_all = jnp.pad(gate, pad).reshape(-1)
    grid_spec = pltpu.PrefetchScalarGridSpec(
        num_scalar_prefetch=1,
        grid=(ne, nf),
        in_specs=[
            pl.BlockSpec(memory_space=pltpu.SMEM),
            pl.BlockSpec(memory_space=pl.ANY),
            pl.BlockSpec((None, None, d, tf), lambda e, f, idx: (layer, e, 0, f)),
            pl.BlockSpec((None, None, d, tf), lambda e, f, idx: (layer, e, 0, f)),
            pl.BlockSpec((None, None, tf, d), lambda e, f, idx: (layer, e, f, 0)),
        ],
        out_specs=pl.BlockSpec(memory_space=pl.ANY),
        scratch_shapes=[
            pltpu.VMEM((rows_pad * n_blk, LANES), F32), pltpu.VMEM((rows, d), BF16),
            pltpu.VMEM((rows_pad, d), F32), pltpu.VMEM((rows_pad, d), F32),
            pltpu.VMEM((n, d), F32), pltpu.SemaphoreType.DMA, pltpu.SemaphoreType.DMA,
        ],
    )
    return pl.pallas_call(
        functools.partial(_moe_kernel, rows=rows, rows_pad=rows_pad), grid_spec=grid_spec,
        out_shape=jax.ShapeDtypeStruct((n, d), F32),
        compiler_params=_cparams(("arbitrary", "arbitrary"), 62), name="expert_ffn",
    )(idx_all, gate_all, h2, w1, w3, w2)


def _rope_tables(t):
    rows = t // GRID_W
    row = jnp.repeat(jnp.arange(rows, dtype=F32), GRID_W)
    colp = jnp.tile(jnp.arange(GRID_W, dtype=F32), rows)
    n_freq = HEAD_DIM // 4
    inv = ROPE_BASE ** (-jnp.arange(n_freq, dtype=F32) / n_freq)
    ang = jnp.concatenate([row[:, None] * inv, colp[:, None] * inv], axis=-1)
    return jnp.cos(ang), jnp.sin(ang)


def kernel(x_prompt, x_sample, c, state_mlstm_C, state_mlstm_n, state_mlstm_m, state_ret_S, c_ctx, w_mod, b_mod,
           norm1_w, norm2_w, w_in, mlstm_if_b, mlstm_norm_w, ret_decay_logit, ret_norm_w, w_branch_a, w_branch_b,
           w_out, router_w, ffn_w1, ffn_w3, ffn_w2, final_norm_w):
    bp, seq, d = x_prompt.shape
    db, dec_seq, _ = x_sample.shape
    depth = w_mod.shape[0]
    hm = mlstm_if_b.shape[-1]
    hr = ret_decay_logit.shape[-1]
    ne = router_w.shape[-1]
    n_prompt = bp * seq
    n_sample = db * dec_seq
    assert n_prompt == n_sample and hm * HEAD_DIM == d and hr * HEAD_DIM == d and hm == hr
    cap = EC_FACTOR * n_prompt // ne
    tm = 512
    tp = 2048
    geo = dict(n_prompt=n_prompt, dec_seq=dec_seq)

    cond8 = jnp.concatenate([c_ctx[None, :], c, jnp.zeros((8 - 1 - db, d), F32)], axis=0)
    mod = _mod_call(cond8, w_mod, b_mod)[:, :1 + db].reshape(depth, 1 + db, 6, d)

    n_m = 4 * d
    g0 = n_m + 4 * hm

    w_t = jnp.swapaxes(w_in, 1, 2)
    mq, mk, mv, mo = (k * d for k in range(4))
    rq, rk, rv, rg, ga, gb = (g0 + k * d for k in range(6))
    wg_t = w_t[:, n_m:g0]
    gate_bias = jnp.transpose(mlstm_if_b, (0, 2, 1, 3)).reshape(depth, 4 * hm, 1)
    router_wt = jnp.swapaxes(router_w, 1, 2)
    nw_m = jnp.broadcast_to(mlstm_norm_w[:, :, None], (depth, d, LANES))
    nw_r = jnp.broadcast_to(ret_norm_w[:, :, None], (depth, d, LANES))

    cos, sin = _rope_tables(dec_seq)
    cos_t = cos.T.reshape(HEAD_DIM // 2, dec_seq // tp, tp).transpose(1, 0, 2)
    sin_t = sin.T.reshape(HEAD_DIM // 2, dec_seq // tp, tp).transpose(1, 0, 2)

    x, h = _resid_norm_call(x_prompt.reshape(n_prompt, d), None, None, mod[0], norm1_w[0], gate_row=0,
                            mod_rows=(0, 1), h_dtype=BF16, write_x=True, x_tail=x_sample.reshape(n_sample, d),
                            tm=tm, **geo)
    m_bufs, s_buf = None, None
    y_prompt = y_sample = None
    for l in range(depth):
        proj_t = _proj_t_call(h, w_t, l, (mq, mv, mo, rq, rv, rg), (cos_t, sin_t), 3, tm=tp, **geo)
        proj_n = _proj_call(h, w_t, l, (mk, rk, ga, gb), (cos, sin), 1, tm=tp, **geo)
        gp = _gate_call(h, wg_t[l], gate_bias[l], hm, tm=2 * tm)

        mix = dict(n_heads=hm, layer=l, depth=depth)
        hm_p, *m_bufs = _mlstm_call(proj_t, proj_n, gp, nw_m[l], None, m_bufs, n_seq=bp, seq_len=seq, tok_off=0,
                                    **mix)
        (hm_s,) = _mlstm_call(proj_t, proj_n, gp, nw_m[l], (state_mlstm_C, state_mlstm_n, state_mlstm_m), None,
                              n_seq=db, seq_len=dec_seq, tok_off=n_prompt, **mix)
        hr_p, s_buf = _ret_call(proj_t, proj_n, ret_decay_logit[l], nw_r[l], None, s_buf, n_seq=bp, seq_len=seq,
                                tok_off=0, **mix)
        (hr_s,) = _ret_call(proj_t, proj_n, ret_decay_logit[l], nw_r[l], state_ret_S, None, n_seq=db,
                            seq_len=dec_seq, tok_off=n_prompt, **mix)

        x1, h2, aff_t = _merge_call(hm_p, hm_s, hr_p, hr_s, proj_n, x, mod[l], norm2_w[l], w_branch_a, w_branch_b,
                                    w_out, router_wt, l, tm=tm, **geo)
        idx, gate = _route_call(aff_t, 2, cap)
        idx = idx.reshape(2, ne, cap) + (jnp.arange(2, dtype=jnp.int32) * n_prompt)[:, None, None]
        idx = jnp.swapaxes(idx, 0, 1).reshape(ne, 2 * cap)
        gate = jnp.swapaxes(gate.reshape(2, ne, cap), 0, 1).reshape(ne, 2 * cap)
        moe = _moe_call(idx, gate, h2, ffn_w1, ffn_w3, ffn_w2, l)
        if l + 1 < depth:
            x, h = _resid_norm_call(x1, moe, mod[l], mod[l + 1], norm1_w[l + 1], gate_row=5, mod_rows=(0, 1),
                                    h_dtype=BF16, write_x=True, tm=tm, **geo)
        else:
            fin = dict(gate_row=5, mod_rows=None, h_dtype=F32, write_x=False, **geo)
            _, y_prompt = _resid_norm_call(x1, moe, mod[l], None, final_norm_w, row_off=0, n_rows=n_prompt, **fin)
            _, y_sample = _resid_norm_call(x1, moe, mod[l], None, final_norm_w, row_off=n_prompt, n_rows=n_sample,
                                           **fin)

    c_buf, n_buf, m_buf = m_bufs
    return (y_prompt.reshape(bp, seq, d), y_sample.reshape(db, dec_seq, d), c_buf, n_buf[:, :, :, :, 0, :],
            m_buf[:, :, :, :, 0, 0], s_buf)
```

```python
import functools

import jax
import jax.numpy as jnp
from jax import lax
from jax.experimental import pallas as pl
from jax.experimental.pallas import tpu as pltpu

F32 = jnp.float32
BF16 = jnp.bfloat16

GRID_W = 64
CHUNK = 128
LANES = 128
HEAD_DIM = 256
EC_FACTOR = 2
ROPE_BASE = 10000.0
EPS = 1e-6
V7X_VMEM_BYTES = 64 * 1024 * 1024
MIB = 1024 * 1024


def _cparams(semantics, vmem_mib):
    assert vmem_mib * MIB < V7X_VMEM_BYTES
    return pltpu.CompilerParams(dimension_semantics=semantics, vmem_limit_bytes=vmem_mib * MIB)


def _dot(a, b):
    return jnp.dot(a, b, preferred_element_type=F32)


def _dot_nt(a, b):
    return lax.dot_general(a, b, (((1,), (1,)), ((), ())), preferred_element_type=F32)


def _log_sigmoid(x):
    return -(jnp.maximum(-x, 0.0) + jnp.log1p(jnp.exp(-jnp.abs(x))))


def _split3(x):
    hi = x.astype(BF16)
    r1 = x - hi.astype(F32)
    mid = r1.astype(BF16)
    lo = (r1 - mid.astype(F32)).astype(BF16)
    return hi, mid, lo


def _dot3(x, m):
    hi, mid, lo = _split3(x)
    return _dot(hi, m) + _dot(mid, m) + _dot(lo, m)


def _iota(shape, dim):
    return lax.broadcasted_iota(jnp.int32, shape, dim)


def _tile_rows(ref, fb, n_tok, n_blk):
    return ref.at[pl.ds(fb, n_tok, stride=n_blk), :]


def _mod_kernel(cond_ref, w_ref, b_ref, out_ref):
    c = cond_ref[...]
    s = (c * jax.nn.sigmoid(c)).astype(BF16)
    out_ref[...] = _dot(s, w_ref[...].astype(BF16)) + b_ref[...]


def _mod_call(cond8, w_mod, b_mod):
    depth, d, w6 = w_mod.shape
    tn = 1536
    return pl.pallas_call(
        _mod_kernel,
        grid=(depth, w6 // tn),
        in_specs=[
            pl.BlockSpec((8, d), lambda l, j: (0, 0)),
            pl.BlockSpec((None, d, tn), lambda l, j: (l, 0, j)),
            pl.BlockSpec((None, 1, tn), lambda l, j: (l, 0, j)),
        ],
        out_specs=pl.BlockSpec((None, 8, tn), lambda l, j: (l, 0, j)),
        out_shape=jax.ShapeDtypeStruct((depth, 8, w6), F32),
        compiler_params=_cparams(("parallel", "parallel"), 32),
        name="adaln_mod",
    )(cond8, w_mod, b_mod.reshape(depth, 1, w6))


def _group_of_tile(i, tm, n_prompt, dec_seq):
    return jnp.maximum(i * tm - n_prompt + dec_seq, 0) // dec_seq


def _resid_norm_kernel(*refs, has_delta, has_mod, gate_row, mod_rows, write_x, n_first_tiles, gate_heads):
    it = iter(refs)
    x_ref = next(it)
    x2_ref = next(it) if n_first_tiles else None
    delta_ref = next(it) if has_delta else None
    mod_ref = next(it) if (has_delta or has_mod) else None
    nw_ref = next(it)
    wg_ref, gb_ref = (next(it), next(it)) if gate_heads else (None, None)
    xo_ref = next(it) if write_x else None
    h_ref = next(it)
    gp_ref = next(it) if gate_heads else None
    x = x_ref[...]
    if n_first_tiles:
        x = jnp.where(pl.program_id(0) < n_first_tiles, x, x2_ref[...])
    if has_delta:
        x = x + mod_ref[gate_row:gate_row + 1, :] * delta_ref[...]
    if write_x:
        xo_ref[...] = x
    y = x * lax.rsqrt(jnp.mean(x * x, axis=-1, keepdims=True) + EPS) * nw_ref[...]
    if has_mod:
        sh_row, sc_row = mod_rows
        y = y * (1.0 + mod_ref[sc_row:sc_row + 1, :]) + mod_ref[sh_row:sh_row + 1, :]
    h = y.astype(h_ref.dtype)
    h_ref[...] = h
    if gate_heads:
        _gate_prep(h, wg_ref, gb_ref, gp_ref, gate_heads)


def _resid_norm_call(x, delta, mod_gate, mod_norm, norm_w, *, gate_row, mod_rows, h_dtype, write_x,
                     n_prompt, dec_seq, tm=1024, row_off=0, n_rows=None, x_tail=None, gate_prep=None):
    n, d = x.shape
    has_delta = delta is not None
    has_mod = mod_norm is not None
    grp = functools.partial(_group_of_tile, tm=tm, n_prompt=n_prompt, dec_seq=dec_seq)
    if x_tail is None:
        n_rows = n if n_rows is None else n_rows
        toff = row_off // tm
        nft = 0
        tile_in = pl.BlockSpec((tm, d), lambda i: (i + toff, 0))
        args, specs = [x], [tile_in]
    else:
        assert row_off == 0 and n_rows is None and not has_delta
        n_rows, toff, nft = n + x_tail.shape[0], 0, n // tm
        args = [x, x_tail]
        specs = [pl.BlockSpec((tm, d), lambda i: (jnp.minimum(i, nft - 1), 0)),
                 pl.BlockSpec((tm, d), lambda i: (jnp.maximum(i - nft, 0), 0))]
    tile_out = pl.BlockSpec((tm, d), lambda i: (i, 0))
    if has_delta:
        args.append(delta)
        specs.append(tile_in)
    if has_delta or has_mod:
        mg = mod_gate if has_delta else mod_norm
        mn = mod_norm if has_mod else mod_gate
        args.append(jnp.concatenate([mg, mn], axis=1))
        specs.append(pl.BlockSpec((None, 12, d), lambda i: (grp(i + toff), 0, 0)))
    args.append(norm_w.reshape(1, d))
    specs.append(pl.BlockSpec((1, d), lambda i: (0, 0)))
    n_heads = 0
    if gate_prep is not None:
        wg_t, bias_col, n_heads = gate_prep
        args += [wg_t, bias_col]
        specs += [pl.BlockSpec(wg_t.shape, lambda i: (0, 0)), pl.BlockSpec(bias_col.shape, lambda i: (0, 0))]
    out_shape, out_specs = [], []
    if write_x:
        out_shape.append(jax.ShapeDtypeStruct((n_rows, d), F32))
        out_specs.append(tile_out)
    out_shape.append(jax.ShapeDtypeStruct((n_rows, d), h_dtype))
    out_specs.append(tile_out)
    if gate_prep is not None:
        out_shape.append(jax.ShapeDtypeStruct((n_heads, n_rows // CHUNK, 4, CHUNK), F32))
        out_specs.append(pl.BlockSpec((n_heads, tm // CHUNK, 4, CHUNK), lambda i: (0, i, 0, 0)))
    kern = functools.partial(
        _resid_norm_kernel, has_delta=has_delta, has_mod=has_mod, gate_row=gate_row,
        mod_rows=None if mod_rows is None else (6 + mod_rows[0], 6 + mod_rows[1]), write_x=write_x,
        n_first_tiles=nft, gate_heads=n_heads)
    outs = pl.pallas_call(
        kern, grid=(n_rows // tm,), in_specs=specs, out_specs=out_specs, out_shape=out_shape,
        compiler_params=_cparams(("parallel",), 48), name="resid_norm",
    )(*args)
    return outs if write_x else (None, *outs)


def _rope_pair(x1, x2, cos, sin):
    return x1 * cos - x2 * sin, x1 * sin + x2 * cos


def _weight_rows_spec(layer, row_starts, tn, d):
    def index_map(j, i):
        start = sum(jnp.where(j == k, s, 0) for k, s in enumerate(row_starts))
        return layer, pl.multiple_of(start, 8), 0
    assert all(s % 8 == 0 for s in row_starts)
    return pl.BlockSpec((pl.Element(1), pl.Element(tn), pl.Element(d)), index_map)


def _proj_kernel(h_ref, w_ref, cos_ref, sin_ref, out_ref, wb_scr, *, tm, rope_tile, n_prompt, dec_seq):
    j = pl.program_id(0)
    i = pl.program_id(1)

    @pl.when(i == 0)
    def _():
        wb_scr[...] = w_ref[0].astype(BF16)

    row0 = pl.multiple_of(i * tm, tm)
    use_rope = (j == rope_tile) & (row0 >= n_prompt)
    half = HEAD_DIM // 2

    def heads(rope):
        h = h_ref[...]
        if rope:
            pos0 = pl.multiple_of((row0 - n_prompt) % dec_seq, tm)
            cos = cos_ref[pl.ds(pos0, tm), :]
            sin = sin_ref[pl.ds(pos0, tm), :]
        for hh in range(wb_scr.shape[0] // HEAD_DIM):
            c0 = hh * HEAD_DIM
            acc = _dot_nt(h, wb_scr[c0:c0 + HEAD_DIM, :])
            if rope:
                y1, y2 = _rope_pair(acc[:, :half], acc[:, half:], cos, sin)
                out_ref[:, c0:c0 + half] = y1.astype(out_ref.dtype)
                out_ref[:, c0 + half:c0 + HEAD_DIM] = y2.astype(out_ref.dtype)
            else:
                out_ref[:, c0:c0 + HEAD_DIM] = acc.astype(out_ref.dtype)

    pl.when(use_rope)(functools.partial(heads, True))
    pl.when(jnp.logical_not(use_rope))(functools.partial(heads, False))


def _proj_call(h, w_t, layer, row_starts, rope, rope_tile, *, n_prompt, dec_seq, tm=512, tn=1024):
    n, d = h.shape
    nj = len(row_starts)
    kern = functools.partial(_proj_kernel, tm=tm, rope_tile=rope_tile, n_prompt=n_prompt, dec_seq=dec_seq)
    return pl.pallas_call(
        kern, grid=(nj, n // tm),
        in_specs=[
            pl.BlockSpec((tm, d), lambda j, i: (i, 0)),
            _weight_rows_spec(layer, row_starts, tn, d),
            pl.BlockSpec(rope[0].shape, lambda j, i: (0, 0)),
            pl.BlockSpec(rope[1].shape, lambda j, i: (0, 0)),
        ],
        out_specs=pl.BlockSpec((tm, tn), lambda j, i: (i, j)),
        out_shape=jax.ShapeDtypeStruct((n, nj * tn), BF16),
        scratch_shapes=[pltpu.VMEM((tn, d), BF16)],
        compiler_params=_cparams(("parallel", "arbitrary"), 56), name="in_proj",
    )(h, w_t, *rope)


def _proj_t_kernel(h_ref, w_ref, cos_ref, sin_ref, out_ref, wb_scr, *, tm, rope_tile, n_prompt, dec_seq):
    j = pl.program_id(0)
    i = pl.program_id(1)

    @pl.when(i == 0)
    def _():
        wb_scr[...] = w_ref[0].astype(BF16)

    row0 = pl.multiple_of(i * tm, tm)
    use_rope = (j == rope_tile) & (row0 >= n_prompt)
    half = HEAD_DIM // 2

    def heads(rope):
        h = h_ref[...]
        if rope:
            blk = ((row0 - n_prompt) % dec_seq) // tm
            cos = cos_ref[blk]
            sin = sin_ref[blk]
        for hh in range(wb_scr.shape[0] // HEAD_DIM):
            r0 = hh * HEAD_DIM
            acc = _dot_nt(wb_scr[r0:r0 + HEAD_DIM, :], h)
            if rope:
                y1, y2 = _rope_pair(acc[:half, :], acc[half:, :], cos, sin)
                y = jnp.concatenate([y1, y2], axis=0).astype(out_ref.dtype)
            else:
                y = acc.astype(out_ref.dtype)
            for s in range(tm // CHUNK):
                out_ref[s, r0:r0 + HEAD_DIM, :] = y[:, s * CHUNK:(s + 1) * CHUNK]

    pl.when(use_rope)(functools.partial(heads, True))
    pl.when(jnp.logical_not(use_rope))(functools.partial(heads, False))


def _proj_t_call(h, w_t, layer, row_starts, rope_t, rope_tile, *, n_prompt, dec_seq, tm=512, tn=1024):
    n, d = h.shape
    nj = len(row_starts)
    kern = functools.partial(_proj_t_kernel, tm=tm, rope_tile=rope_tile, n_prompt=n_prompt, dec_seq=dec_seq)
    return pl.pallas_call(
        kern, grid=(nj, n // tm),
        in_specs=[
            pl.BlockSpec((tm, d), lambda j, i: (i, 0)),
            _weight_rows_spec(layer, row_starts, tn, d),
            pl.BlockSpec(rope_t[0].shape, lambda j, i: (0, 0, 0)),
            pl.BlockSpec(rope_t[1].shape, lambda j, i: (0, 0, 0)),
        ],
        out_specs=pl.BlockSpec((tm // CHUNK, tn, CHUNK), lambda j, i: (i, j, 0)),
        out_shape=jax.ShapeDtypeStruct((n // CHUNK, nj * tn, CHUNK), BF16),
        scratch_shapes=[pltpu.VMEM((tn, d), BF16)],
        compiler_params=_cparams(("parallel", "arbitrary"), 56), name="in_proj_t",
    )(h, w_t, *rope_t)


def _gate_prep(h, wg_ref, bias_ref, out_ref, n_heads):
    tm = h.shape[0]
    g = _dot_nt(wg_ref[...].astype(BF16), h) + bias_ref[...]
    nd = 2 * n_heads
    ig = g[0:nd, :]
    lf = _log_sigmoid(g[nd:2 * nd, :])
    r = _iota((CHUNK, CHUNK), 0)
    c = _iota((CHUNK, CHUNK), 1)
    upper = jnp.where(r <= c, 1.0, 0.0).astype(BF16)
    lower = jnp.where(r >= c, 1.0, 0.0).astype(BF16)
    is_fwd = _iota((nd, CHUNK), 0) < n_heads
    for s in range(tm // CHUNK):
        sl = slice(s * CHUNK, (s + 1) * CHUNK)
        lf_c = lf[:, sl]
        b = jnp.where(is_fwd, _dot3(lf_c, upper), _dot3(lf_c, lower))
        ig_c = ig[:, sl]
        for hh in range(n_heads):
            out_ref[hh, s, 0:1, :] = ig_c[hh:hh + 1, :]
            out_ref[hh, s, 1:2, :] = ig_c[n_heads + hh:n_heads + hh + 1, :]
            out_ref[hh, s, 2:3, :] = b[hh:hh + 1, :]
            out_ref[hh, s, 3:4, :] = b[n_heads + hh:n_heads + hh + 1, :]


def _scan_loop(nc, body, init):
    if nc <= 2:
        carry = init
        for ci in range(nc):
            carry = body(ci, carry)
        return carry
    return lax.fori_loop(0, nc, body, init)


MAX_GROUP = 16
BF16_ROWS = 16
STRIP = 32


def _seqs_per_step(n_seq, nc):
    ns = max(1, MAX_GROUP // nc)
    while n_seq % ns:
        ns -= 1
    return ns


def _group_loop(n_groups, body):
    if n_groups == 1:
        body(0, 0)
    else:
        lax.fori_loop(0, n_groups, body, 0)


def _chunk_off(c):
    return c * CHUNK if isinstance(c, int) else pl.multiple_of(c * CHUNK, CHUNK)


def _mlstm_kernel(*refs, nc, ns, has_init, first_layer, write_state):
    it = iter(refs)
    qt_ref, vt_ref, ot_ref, k_ref, gp_ref, nw_ref = (next(it) for _ in range(6))
    if has_init:
        c0_ref, n0_ref, m0_ref = next(it), next(it), next(it)
    if write_state and not first_layer:
        next(it), next(it), next(it)
    hm_ref = next(it)
    if write_state:
        cout_ref, nout_ref, mout_ref = next(it), next(it), next(it)
    ct_scr, n_scr, part_scr, inc_scr, ninc_scr, rows_scr, cst_scr, nst_scr, coef_scr = (next(it) for _ in range(9))

    L = CHUNK
    grp = min(MAX_GROUP, ns * nc)
    row = _iota((L, L), 0)
    col = _iota((L, L), 1)
    k_scale = HEAD_DIM ** -0.5
    b_idx = pl.program_id(0)
    h_idx = pl.program_id(1)
    n_heads = pl.num_programs(1)

    def gates(d, c):
        g = gp_ref[c]
        ig = g[d:d + 1, :]
        brow = g[2 + d:3 + d, :]
        blast = brow[:, L - 1:L] if d == 0 else brow[:, 0:1]
        return ig, brow, blast

    def local_group(g, carry):
        cs = [g * grp + u for u in range(grp)]
        pairs = [(u, d) for u in range(grp) for d in range(2)]
        qt = [qt_ref[c] for c in cs]
        vt = [vt_ref[c] for c in cs]
        k = [k_ref[pl.ds(_chunk_off(c), L), :] for c in cs]
        qk = [_dot(k[u], qt[u]) for u in range(grp)]
        vf = [v.astype(F32) for v in vt]
        gts = {(u, d): gates(d, cs[u]) for u, d in pairs}
        for u, c in enumerate(cs):
            lhs = []
            for d in range(2):
                ig, brow, blast = gts[u, d]
                wj = blast - brow + ig
                mloc2 = jnp.max(wj, axis=1, keepdims=True)
                e = jnp.exp(wj - mloc2) * k_scale
                rows_scr[d, c, 2:3, :] = jnp.broadcast_to(mloc2, (1, L))
                rows_scr[d, c, 4:5, :] = jnp.broadcast_to(blast, (1, L))
                lhs.append(((vf[u] * e).astype(BF16), _split3(jnp.broadcast_to(e, (BF16_ROWS, L)))))
            res = _dot(jnp.concatenate([lhs[0][0], lhs[1][0], *lhs[0][1], *lhs[1][1]], axis=0), k[u])
            for d in range(2):
                inc_scr[d, c] = res[d * HEAD_DIM:(d + 1) * HEAD_DIM, :]
                r0 = 2 * HEAD_DIM + 3 * BF16_ROWS * d
                ninc_scr[d, c] = (res[r0:r0 + 8, :] + res[r0 + BF16_ROWS:r0 + BF16_ROWS + 8, :]
                                  + res[r0 + 2 * BF16_ROWS:r0 + 2 * BF16_ROWS + 8, :])
        for u, d in pairs:
            c = cs[u]
            ig, brow, blast = gts[u, d]
            key_term = jnp.broadcast_to(ig - brow, (L, L)).T
            causal = (row <= col) if d == 0 else (row >= col)
            dm = jnp.where(causal, key_term + brow, -jnp.inf)
            mloc = jnp.max(dm, axis=0, keepdims=True)
            s = qk[u] * (jnp.exp(dm - mloc) * k_scale)
            part_scr[d, c] = _dot(vt[u], s.astype(BF16))
            rows_scr[d, c, 0:1, :] = jnp.sum(s, axis=0, keepdims=True)
            rows_scr[d, c, 1:2, :] = mloc
        return carry

    _group_loop(ns * nc // grp, local_group)

    def m_step(d, c, m):
        blast = rows_scr[d, c, 4:5, :]
        mloc2 = rows_scr[d, c, 2:3, :]
        rows_scr[d, c, 3:4, :] = m
        m_new = jnp.maximum(blast + m, mloc2)
        decay = jnp.exp(blast + m - m_new)
        w_inc = jnp.exp(mloc2 - m_new)
        coef_scr[d, c, 0:1, :] = jnp.concatenate([decay] * (HEAD_DIM // L), axis=1)
        coef_scr[d, c, 1:2, :] = jnp.concatenate([w_inc] * (HEAD_DIM // L), axis=1)
        return m_new

    for sq in range(ns):
        c_lo = sq * nc
        m_init = []
        for d in range(2):
            if has_init:
                ct_scr[d] = c0_ref[d].T
                n_scr[d] = jnp.broadcast_to(n0_ref[d], n_scr.shape[1:])
                m_init.append(jnp.full((1, L), m0_ref[(b_idx * 2 + d) * n_heads + h_idx], F32))
            else:
                ct_scr[d] = jnp.zeros(ct_scr.shape[1:], F32)
                n_scr[d] = jnp.zeros(n_scr.shape[1:], F32)
                m_init.append(jnp.zeros((1, L), F32))

        def m_body(ci, ms, c_lo=c_lo):
            return m_step(0, c_lo + ci, ms[0]), m_step(1, c_lo + nc - 1 - ci, ms[1])

        m_fin = _scan_loop(nc, m_body, tuple(m_init))

        def body(ci, carry, c_lo=c_lo):
            for d, c in ((0, c_lo + ci), (1, c_lo + nc - 1 - ci)):
                decay = coef_scr[d, c, 0:1, :]
                w_inc = coef_scr[d, c, 1:2, :]
                n8 = n_scr[d]
                nst_scr[d, c] = n8
                for r0 in range(0, HEAD_DIM, STRIP):
                    ct = ct_scr[d, r0:r0 + STRIP, :]
                    cst_scr[d, c, r0:r0 + STRIP, :] = ct.astype(BF16)
                    ct_scr[d, r0:r0 + STRIP, :] = decay * ct + w_inc * inc_scr[d, c, r0:r0 + STRIP, :]
                n_scr[d] = decay * n8 + w_inc * ninc_scr[d, c]
            return carry

        _scan_loop(nc, body, 0)

        if write_state:
            lsel = (lambda d, sq=sq: (sq, 0, d)) if first_layer else (lambda d, sq=sq: (sq, d))
            for d in range(2):
                cout_ref[lsel(d)] = ct_scr[d].T
                nout_ref[lsel(d)] = n_scr[d][0:1, :]
                mout_ref[lsel(d)] = m_fin[d]
            if first_layer:
                for l in range(1, cout_ref.shape[1]):
                    cout_ref[sq, l] = jnp.zeros(cout_ref.shape[2:], F32)
                    nout_ref[sq, l] = jnp.zeros(nout_ref.shape[2:], F32)
                    mout_ref[sq, l] = jnp.zeros(mout_ref.shape[2:], F32)

    def finish_group(g, carry):
        cs = [g * grp + u for u in range(grp)]
        qt = [qt_ref[c] for c in cs]
        cq, nq = [], []
        for u, c in enumerate(cs):
            n16 = [jnp.concatenate([nst_scr[d, c]] * (BF16_ROWS // 8), axis=0).astype(BF16) for d in range(2)]
            res = _dot(jnp.concatenate([cst_scr[0, c], cst_scr[1, c], n16[0], n16[1]], axis=0), qt[u])
            cq.append([res[d * HEAD_DIM:(d + 1) * HEAD_DIM, :] for d in range(2)])
            nq.append([res[2 * HEAD_DIM + d * BF16_ROWS:2 * HEAD_DIM + d * BF16_ROWS + 1, :] for d in range(2)])
        for u, c in enumerate(cs):
            coef = []
            for d in range(2):
                _, brow, _ = gates(d, c)
                den_loc = rows_scr[d, c, 0:1, :]
                mloc = rows_scr[d, c, 1:2, :]
                inter = brow + rows_scr[d, c, 3:4, :]
                m_row = jnp.maximum(mloc, inter)
                w_loc = jnp.exp(mloc - m_row)
                w_inter = jnp.exp(inter - m_row)
                den = w_loc * den_loc + w_inter * nq[u][d]
                r_den = 1.0 / jnp.maximum(jnp.abs(den), jnp.exp(-m_row))
                coef.append((w_loc * r_den, w_inter * r_den))
            ssq = jnp.zeros((1, L), F32)
            for r0 in range(0, HEAD_DIM, STRIP):
                h_s = None
                for d in range(2):
                    h_d = coef[d][0] * part_scr[d, c, r0:r0 + STRIP, :] + coef[d][1] * cq[u][d][r0:r0 + STRIP, :]
                    h_s = h_d if h_s is None else h_s + h_d
                part_scr[0, c, r0:r0 + STRIP, :] = h_s
                ssq = ssq + jnp.sum(h_s * h_s, axis=0, keepdims=True)
            r_norm = lax.rsqrt(ssq * (1.0 / HEAD_DIM) + EPS)
            for r0 in range(0, HEAD_DIM, L):
                y = part_scr[0, c, r0:r0 + L, :] * r_norm * nw_ref[r0:r0 + L, :]
                o = jax.nn.sigmoid(ot_ref[c, r0:r0 + L, :].astype(F32)) * y
                hm_ref[pl.ds(_chunk_off(c), L), r0:r0 + L] = o.T.astype(hm_ref.dtype)
        return carry

    _group_loop(ns * nc // grp, finish_group)


def _state_specs(shapes, layer, first_layer, ns):
    specs = []
    for shp in shapes:
        tail = shp[4:]
        zeros = (0,) * len(tail)
        if first_layer:
            specs.append(pl.BlockSpec((ns, shp[1], 2, None) + tail, lambda b, h, z=zeros: (b, 0, 0, h) + z))
        else:
            specs.append(pl.BlockSpec((ns, None, 2, None) + tail, lambda b, h, z=zeros: (b, layer, 0, h) + z))
    return specs


def _mlstm_call(proj_t, proj_n, gp, nw_b, init, state_bufs, *, n_seq, seq_len, tok_off, n_heads, layer, depth):
    dk = HEAD_DIM
    nc = seq_len // CHUNK
    has_init = init is not None
    ns = 1 if has_init else _seqs_per_step(n_seq, nc)
    nct = ns * nc
    boff = tok_off // (ns * seq_len)
    write_state = not has_init
    first_layer = layer == 0
    args = [proj_t, proj_t, proj_t, proj_n, gp, nw_b]
    specs = [
        pl.BlockSpec((nct, dk, CHUNK), lambda b, h: (b + boff, h, 0)),
        pl.BlockSpec((nct, dk, CHUNK), lambda b, h: (b + boff, n_heads + h, 0)),
        pl.BlockSpec((nct, dk, CHUNK), lambda b, h: (b + boff, 2 * n_heads + h, 0)),
        pl.BlockSpec((ns * seq_len, dk), lambda b, h: (b + boff, h)),
        pl.BlockSpec((None, nct, 4, CHUNK), lambda b, h: (h, b + boff, 0, 0)),
        pl.BlockSpec((dk, LANES), lambda b, h: (h, 0)),
    ]
    aliases = {}
    if has_init:
        c0, n0, m0 = init
        args += [c0, n0[:, layer][:, :, :, None, :], m0[:, layer].reshape(-1)]
        specs += [
            pl.BlockSpec((None, None, 2, None, dk, dk), lambda b, h: (b, layer, 0, h, 0, 0)),
            pl.BlockSpec((None, 2, None, 1, dk), lambda b, h: (b, 0, h, 0, 0)),
            pl.BlockSpec(memory_space=pltpu.SMEM),
        ]
    out_shape = [jax.ShapeDtypeStruct((n_seq * seq_len, n_heads * dk), BF16)]
    out_specs = [pl.BlockSpec((ns * seq_len, dk), lambda b, h: (b, h))]
    if write_state:
        shapes = [(n_seq, depth, 2, n_heads, dk, dk), (n_seq, depth, 2, n_heads, 1, dk),
                  (n_seq, depth, 2, n_heads, 1, LANES)]
        if not first_layer:
            for k_, buf in enumerate(state_bufs):
                aliases[len(args)] = 1 + k_
                args.append(buf)
                specs.append(pl.BlockSpec(memory_space=pl.ANY))
        out_shape += [jax.ShapeDtypeStruct(s, F32) for s in shapes]
        out_specs += _state_specs(shapes, layer, first_layer, ns)
    kern = functools.partial(_mlstm_kernel, nc=nc, ns=ns, has_init=has_init, first_layer=first_layer,
                             write_state=write_state)
    return pl.pallas_call(
        kern, grid=(n_seq // ns, n_heads), in_specs=specs, out_specs=out_specs, out_shape=out_shape,
        input_output_aliases=aliases,
        scratch_shapes=[pltpu.VMEM((2, dk, dk), F32), pltpu.VMEM((2, 8, dk), F32),
                        pltpu.VMEM((2, nct, dk, CHUNK), F32), pltpu.VMEM((2, nct, dk, dk), F32),
                        pltpu.VMEM((2, nct, 8, dk), F32), pltpu.VMEM((2, nct, 8, CHUNK), F32),
                        pltpu.VMEM((2, nct, dk, dk), BF16), pltpu.VMEM((2, nct, 8, dk), F32),
                        pltpu.VMEM((2, nct, 8, dk), F32)],
        compiler_params=_cparams(("parallel", "parallel"), 48), name="mlstm_mixer",
    )(*args)


def _ret_kernel(*refs, nc, ns, has_init, first_layer, write_state):
    it = iter(refs)
    dl_ref, qt_ref, vt_ref, gt_ref, k_ref, nw_ref = (next(it) for _ in range(6))
    if has_init:
        s0_ref = next(it)
    if write_state and not first_layer:
        next(it)
    hr_ref = next(it)
    if write_state:
        sout_ref = next(it)
    st_scr, part_scr, inc_scr, sst_scr = next(it), next(it), next(it), next(it)

    L = CHUNK
    grp = min(MAX_GROUP, ns * nc)
    rowi = _iota((L, L), 0)
    coli = _iota((L, L), 1)
    k_scale = HEAD_DIM ** -0.5
    h_idx = pl.program_id(1)
    n_heads = pl.num_programs(1)

    consts = []
    for d in range(2):
        lg = _log_sigmoid(jnp.full((1, 1), dl_ref[d * n_heads + h_idx], F32))
        rel = (coli - rowi if d == 0 else rowi - coli).astype(F32)
        dmat_t = jnp.where(rel >= 0, jnp.exp(jnp.maximum(rel, 0.0) * lg), 0.0) * k_scale
        pos = _iota((1, L), 1).astype(F32)
        if d == 1:
            pos = (L - 1.0) - pos
        q_decay = jnp.exp((pos + 1.0) * lg)
        k_decay = jnp.exp((L - 1.0 - pos) * lg) * k_scale
        chunk_decay = jnp.exp(float(L) * lg)
        consts.append((dmat_t, q_decay, k_decay, chunk_decay))
    dmat_both = consts[0][0] + consts[1][0]

    def local_group(g, carry):
        cs = [g * grp + u for u in range(grp)]
        qt = [qt_ref[c] for c in cs]
        vt = [vt_ref[c] for c in cs]
        k = [k_ref[pl.ds(_chunk_off(c), L), :] for c in cs]
        qk = [_dot(k[u], qt[u]) for u in range(grp)]
        for u, c in enumerate(cs):
            vf = vt[u].astype(F32)
            res = _dot(jnp.concatenate([(vf * consts[d][2]).astype(BF16) for d in range(2)], axis=0), k[u])
            for d in range(2):
                inc_scr[d, c] = res[d * HEAD_DIM:(d + 1) * HEAD_DIM, :]
        for u, c in enumerate(cs):
            part_scr[c] = _dot(vt[u], (qk[u] * dmat_both).astype(BF16))
        return carry

    _group_loop(ns * nc // grp, local_group)

    for sq in range(ns):
        c_lo = sq * nc
        for d in range(2):
            if has_init:
                st_scr[d] = s0_ref[d].T
            else:
                st_scr[d] = jnp.zeros(st_scr.shape[1:], F32)

        def body(ci, carry, c_lo=c_lo):
            for d, c in ((0, c_lo + ci), (1, c_lo + nc - 1 - ci)):
                for r0 in range(0, HEAD_DIM, STRIP):
                    st = st_scr[d, r0:r0 + STRIP, :]
                    sst_scr[d, c, r0:r0 + STRIP, :] = st.astype(BF16)
                    st_scr[d, r0:r0 + STRIP, :] = consts[d][3] * st + inc_scr[d, c, r0:r0 + STRIP, :]
            return carry

        _scan_loop(nc, body, 0)

        if write_state:
            for d in range(2):
                if first_layer:
                    sout_ref[sq, 0, d] = st_scr[d].T
                else:
                    sout_ref[sq, d] = st_scr[d].T
            if first_layer:
                for l in range(1, sout_ref.shape[1]):
                    sout_ref[sq, l] = jnp.zeros(sout_ref.shape[2:], F32)

    def finish_group(g, carry):
        cs = [g * grp + u for u in range(grp)]
        qt = [qt_ref[c] for c in cs]
        sq = [_dot(jnp.concatenate([sst_scr[0, c], sst_scr[1, c]], axis=0), qt[u]) for u, c in enumerate(cs)]
        for u, c in enumerate(cs):
            ssq = jnp.zeros((1, L), F32)
            for r0 in range(0, HEAD_DIM, STRIP):
                sl = slice(r0, r0 + STRIP)
                o_s = part_scr[c, sl, :] + (consts[0][1] * sq[u][r0:r0 + STRIP, :]
                                            + consts[1][1] * sq[u][HEAD_DIM + r0:HEAD_DIM + r0 + STRIP, :])
                part_scr[c, sl, :] = o_s
                ssq = ssq + jnp.sum(o_s * o_s, axis=0, keepdims=True)
            r_norm = lax.rsqrt(ssq * (1.0 / HEAD_DIM) + EPS)
            for r0 in range(0, HEAD_DIM, L):
                y = part_scr[c, r0:r0 + L, :] * r_norm * nw_ref[r0:r0 + L, :]
                rg = gt_ref[c, r0:r0 + L, :].astype(F32)
                hr_ref[pl.ds(_chunk_off(c), L), r0:r0 + L] = (rg * jax.nn.sigmoid(rg) * y).T.astype(hr_ref.dtype)
        return carry

    _group_loop(ns * nc // grp, finish_group)


def _ret_call(proj_t, proj_n, decay_logit, nw_b, init, state_buf, *, n_seq, seq_len, tok_off, n_heads, layer,
              depth):
    dk = HEAD_DIM
    nc = seq_len // CHUNK
    has_init = init is not None
    ns = 1 if has_init else _seqs_per_step(n_seq, nc)
    nct = ns * nc
    boff = tok_off // (ns * seq_len)
    write_state = not has_init
    first_layer = layer == 0
    args = [decay_logit.reshape(-1), proj_t, proj_t, proj_t, proj_n, nw_b]
    specs = [
        pl.BlockSpec(memory_space=pltpu.SMEM),
        pl.BlockSpec((nct, dk, CHUNK), lambda b, h: (b + boff, 3 * n_heads + h, 0)),
        pl.BlockSpec((nct, dk, CHUNK), lambda b, h: (b + boff, 4 * n_heads + h, 0)),
        pl.BlockSpec((nct, dk, CHUNK), lambda b, h: (b + boff, 5 * n_heads + h, 0)),
        pl.BlockSpec((ns * seq_len, dk), lambda b, h: (b + boff, n_heads + h)),
        pl.BlockSpec((dk, LANES), lambda b, h: (h, 0)),
    ]
    aliases = {}
    if has_init:
        args.append(init)
        specs.append(pl.BlockSpec((None, None, 2, None, dk, dk), lambda b, h: (b, layer, 0, h, 0, 0)))
    out_shape = [jax.ShapeDtypeStruct((n_seq * seq_len, n_heads * dk), BF16)]
    out_specs = [pl.BlockSpec((ns * seq_len, dk), lambda b, h: (b, h))]
    if write_state:
        shapes = [(n_seq, depth, 2, n_heads, dk, dk)]
        if not first_layer:
            aliases[len(args)] = 1
            args.append(state_buf)
            specs.append(pl.BlockSpec(memory_space=pl.ANY))
        out_shape += [jax.ShapeDtypeStruct(s, F32) for s in shapes]
        out_specs += _state_specs(shapes, layer, first_layer, ns)
    kern = functools.partial(_ret_kernel, nc=nc, ns=ns, has_init=has_init, first_layer=first_layer,
                             write_state=write_state)
    return pl.pallas_call(
        kern, grid=(n_seq // ns, n_heads), in_specs=specs, out_specs=out_specs, out_shape=out_shape,
        input_output_aliases=aliases,
        scratch_shapes=[pltpu.VMEM((2, dk, dk), F32), pltpu.VMEM((nct, dk, CHUNK), F32),
                        pltpu.VMEM((2, nct, dk, dk), F32), pltpu.VMEM((2, nct, dk, dk), BF16)],
        compiler_params=_cparams(("parallel", "parallel"), 48), name="retention_mixer",
    )(*args)


def _merge_kernel(hmp_ref, hms_ref, hrp_ref, hrs_ref, ga_ref, gb_ref, x_ref, mod_ref, nw_ref, wa_ref, wb_ref,
                  wo_ref, wr_ref, x1_ref, h2_ref, aff_ref, wab_scr, wbb_scr, wob_scr, *, n_prompt_tiles):
    i = pl.program_id(0)

    @pl.when(i == 0)
    def _():
        wab_scr[...] = wa_ref[...].astype(BF16)
        wbb_scr[...] = wb_ref[...].astype(BF16)
        wob_scr[...] = wo_ref[...].astype(BF16)

    is_prompt = i < n_prompt_tiles
    hm = jnp.where(is_prompt, hmp_ref[...], hms_ref[...])
    hr = jnp.where(is_prompt, hrp_ref[...], hrs_ref[...])
    ya = _dot(hm, wab_scr[...])
    yb = _dot(hr, wbb_scr[...])
    merged = (jax.nn.sigmoid(ga_ref[...].astype(F32)) * ya + jax.nn.sigmoid(gb_ref[...].astype(F32)) * yb)
    y = _dot(merged.astype(BF16), wob_scr[...])
    x1 = x_ref[...] + mod_ref[2:3, :] * y
    x1_ref[...] = x1
    h2 = x1 * lax.rsqrt(jnp.mean(x1 * x1, axis=-1, keepdims=True) + EPS) * nw_ref[...]
    h2 = h2 * (1.0 + mod_ref[4:5, :]) + mod_ref[3:4, :]
    n_blk = h2.shape[1] // LANES
    for fb in range(n_blk):
        _tile_rows(h2_ref, fb, h2.shape[0], n_blk)[...] = h2[:, fb * LANES:(fb + 1) * LANES]
    logits = _dot_nt(wr_ref[...].astype(BF16), h2.astype(BF16))
    p = jnp.exp(logits - jnp.max(logits, axis=0, keepdims=True))
    aff_ref[...] = p / jnp.sum(p, axis=0, keepdims=True)


def _merge_call(hm_p, hm_s, hr_p, hr_s, proj_n, x, mod, norm_w, w_a, w_b, w_out, router_wt, layer, *,
                n_prompt, dec_seq, tm=512):
    n, d = x.shape
    ne = router_wt.shape[1]
    npt = n_prompt // tm
    grp = functools.partial(_group_of_tile, tm=tm, n_prompt=n_prompt, dec_seq=dec_seq)
    tile = pl.BlockSpec((tm, d), lambda i: (i, 0))
    tile_p = pl.BlockSpec((tm, d), lambda i: (jnp.minimum(i, npt - 1), 0))
    tile_s = pl.BlockSpec((tm, d), lambda i: (jnp.maximum(i - npt, 0), 0))
    full = pl.BlockSpec((None, d, d), lambda i: (layer, 0, 0))
    return pl.pallas_call(
        functools.partial(_merge_kernel, n_prompt_tiles=npt), grid=(n // tm,),
        in_specs=[
            tile_p, tile_s, tile_p, tile_s,
            pl.BlockSpec((tm, d), lambda i: (i, 2)),
            pl.BlockSpec((tm, d), lambda i: (i, 3)),
            tile,
            pl.BlockSpec((None, 6, d), lambda i: (grp(i), 0, 0)),
            pl.BlockSpec((1, d), lambda i: (0, 0)),
            full, full, full,
            pl.BlockSpec((None, ne, d), lambda i: (layer, 0, 0)),
        ],
        out_specs=[tile, pl.BlockSpec((tm * (d // LANES), LANES), lambda i: (i, 0)),
                   pl.BlockSpec((ne, tm), lambda i: (0, i))],
        out_shape=[jax.ShapeDtypeStruct((n, d), F32), jax.ShapeDtypeStruct((n * (d // LANES), LANES), F32),
                   jax.ShapeDtypeStruct((ne, n), F32)],
        scratch_shapes=[pltpu.VMEM((d, d), BF16)] * 3,
        compiler_params=_cparams(("arbitrary",), 56), name="merge_out_router",
    )(hm_p, hm_s, hr_p, hr_s, proj_n, proj_n, x, mod, norm_w.reshape(1, d), w_a, w_b, w_out, router_wt)


ROUTE_GROUP = 4


def _route_kernel(a_ref, at_ref, idx_ref, gate_ref, thr_scr, *, cap, n_tok):
    n_sets = a_ref.shape[0]
    nb = n_tok // LANES
    a_all = a_ref[...]

    def as_f32(bits):
        return lax.bitcast_convert_type(bits, F32)

    def count_ge(cand):
        m = jnp.where(a_all >= cand, 1.0, 0.0)
        return jnp.sum(jnp.sum(m, axis=1, keepdims=True), axis=2, keepdims=True)

    def bit_step(k, thr):
        cand = thr | lax.shift_left(jnp.int32(1), 30 - k)
        return jnp.where(count_ge(as_f32(cand)) >= cap, cand, thr)

    thr_scr[...] = lax.fori_loop(0, 31, bit_step, jnp.zeros((n_sets, 1, 1), jnp.int32))

    r128 = _iota((LANES, LANES), 0)
    c128 = _iota((LANES, LANES), 1)
    upper = jnp.where(r128 <= c128, 1.0, 0.0).astype(BF16)
    lower_t = jnp.where(r128 >= c128, 1.0, 0.0).astype(BF16)
    rb = _iota((nb, nb), 0)
    cb = _iota((nb, nb), 1)
    blk_before_rows = jnp.where(cb < rb, 1.0, 0.0).astype(BF16)
    blk_before_cols = jnp.where(rb < cb, 1.0, 0.0).astype(BF16)

    def incl_counts(mask):
        within = _dot(mask.astype(BF16), upper)
        before = _dot(blk_before_rows, within.astype(BF16))[:, LANES - 1:LANES]
        return within + before

    def incl_counts_t(mask_t):
        within = _dot(lower_t, mask_t.astype(BF16))
        before = _dot(within.astype(BF16), blk_before_cols)[LANES - 1:LANES, :]
        return within + before

    slot = _iota((1, cap), 1).astype(F32)
    blk_col = _iota((nb, 1), 0).astype(F32)
    sub_col = _iota((LANES, 1), 0).astype(F32)

    def per_group(g, carry):
        sets = [g + u * (n_sets // ROUTE_GROUP) for u in range(ROUTE_GROUP)]
        stage = []
        for s in sets:
            thr_bits = thr_scr[s]
            thr = as_f32(thr_bits)
            nxt = as_f32(thr_bits + 1)
            a = a_ref[s]
            a_t = at_ref[s]
            gt = jnp.where(a >= nxt, 1.0, 0.0)
            eq = jnp.where((a >= thr) & (a < nxt), 1.0, 0.0)
            gt_t = jnp.where(a_t >= nxt, 1.0, 0.0)
            eq_t = jnp.where((a_t >= thr) & (a_t < nxt), 1.0, 0.0)
            n_gt = jnp.sum(jnp.sum(gt, axis=1, keepdims=True), axis=0, keepdims=True)
            stage.append((a_t, gt, eq, gt_t, eq_t, cap - n_gt))
        ties = [(incl_counts(eq), incl_counts_t(eq_t)) for _, _, eq, _, eq_t, _ in stage]
        sels = []
        for (a_t, gt, eq, gt_t, eq_t, need), (c_eq, c_eq_t) in zip(stage, ties):
            sels.append((gt + eq * jnp.where(c_eq - eq < need, 1.0, 0.0),
                         gt_t + eq_t * jnp.where(c_eq_t - eq_t < need, 1.0, 0.0)))
        cnts = [(incl_counts(sel), incl_counts_t(sel_t)) for sel, sel_t in sels]
        picks = []
        for (a_t, *_), (cnt, cnt_t) in zip(stage, cnts):
            blk_end = cnt[:, LANES - 1:LANES]
            blk_of_slot = jnp.sum(jnp.where(blk_end <= slot, 1.0, 0.0), axis=0, keepdims=True)
            onehot_blk = jnp.where(blk_col == blk_of_slot, 1.0, 0.0).astype(BF16)
            cnt_rows = _dot3(cnt_t, onehot_blk)
            a_rows = _dot3(a_t, onehot_blk)
            picks.append((blk_of_slot, cnt_rows, a_rows))
        for s, (blk_of_slot, cnt_rows, a_rows) in zip(sets, picks):
            sub_of_slot = jnp.sum(jnp.where(cnt_rows <= slot, 1.0, 0.0), axis=0, keepdims=True)
            gate = jnp.sum(jnp.where(sub_col == sub_of_slot, a_rows, 0.0), axis=0, keepdims=True)
            idx_ref[s] = (blk_of_slot * LANES + sub_of_slot).astype(jnp.int32)
            gate_ref[s] = gate
        return carry

    lax.fori_loop(0, n_sets // ROUTE_GROUP, per_group, 0)


def _route_call(aff_t, n_pass, cap):
    ne, n = aff_t.shape
    n_tok = n // n_pass
    nb = n_tok // LANES
    a4 = aff_t.reshape(ne, n_pass, nb, LANES).transpose(1, 0, 2, 3).reshape(n_pass * ne, nb, LANES)
    a4_t = a4.transpose(0, 2, 1)
    n_sets = n_pass * ne
    return pl.pallas_call(
        functools.partial(_route_kernel, cap=cap, n_tok=n_tok),
        out_shape=[jax.ShapeDtypeStruct((n_sets, 1, cap), jnp.int32),
                   jax.ShapeDtypeStruct((n_sets, 1, cap), F32)],
        scratch_shapes=[pltpu.VMEM((n_sets, 1, 1), jnp.int32)],
        compiler_params=pltpu.CompilerParams(vmem_limit_bytes=32 * MIB), name="expert_choice_route",
    )(a4, a4_t)


ROW_LOOP_UNROLL = 8


def _moe_kernel(idx_ref, gate_ref, h_hbm, w1_ref, w3_ref, w2_ref, out_hbm,
                xe_scr, xb_scr, ye_a, ye_b, acc_scr, gsem, osem, *, rows, rows_pad):
    e = pl.program_id(0)
    f = pl.program_id(1)
    n_e = pl.num_programs(0)
    n_f = pl.num_programs(1)
    chunk = rows_pad // n_f

    n_blk = xb_scr.shape[1] // LANES

    def tile_of(t):
        return pl.ds(pl.multiple_of(t * n_blk, n_blk), n_blk)

    def gather_row(slot, s):
        tok = idx_ref[slot * rows_pad + s]
        pltpu.make_async_copy(h_hbm.at[tile_of(tok), :], xe_scr.at[tile_of(s), :], gsem).start()

    def scatter_row(slot, s, ye_ref):
        tok = idx_ref[slot * rows_pad + s]
        g = gate_ref[slot * rows_pad + s]
        acc_scr[pl.ds(tok, 1), :] += ye_ref[pl.ds(s, 1), :] * g

    def wait_gather():
        pltpu.make_async_copy(h_hbm.at[pl.ds(0, rows_pad * n_blk), :], xe_scr, gsem).wait()

    @pl.when((e == 0) & (f == 0))
    def _():
        acc_scr[...] = jnp.zeros_like(acc_scr)
        ye_a[...] = jnp.zeros_like(ye_a)
        ye_b[...] = jnp.zeros_like(ye_b)

        def issue(s, carry):
            gather_row(1, s)
            return carry
        lax.fori_loop(0, rows_pad, issue, 0, unroll=ROW_LOOP_UNROLL)

    def step(ye_cur, ye_prev):
        @pl.when(f == 0)
        def _():
            wait_gather()
            for fb in range(n_blk):
                xb_scr[:, fb * LANES:(fb + 1) * LANES] = _tile_rows(xe_scr, fb, rows, n_blk)[...].astype(BF16)
            ye_cur[0:rows, :] = jnp.zeros((rows, ye_cur.shape[1]), F32)

        xb = xb_scr[...]
        h1 = _dot(xb, w1_ref[...].astype(BF16))
        h3 = _dot(xb, w3_ref[...].astype(BF16))
        he = (h1 * jax.nn.sigmoid(h1) * h3).astype(BF16)
        ye_cur[0:rows, :] += _dot(he, w2_ref[...].astype(BF16))
        base = f * chunk
        for r in range(chunk):
            gather_row(e + 2, base + r)
        for r in range(chunk):
            scatter_row(e, base + r, ye_prev)

        @pl.when((e == n_e - 1) & (f == n_f - 1))
        def _():
            def scatter(s, carry):
                scatter_row(n_e, s, ye_cur)
                return carry
            lax.fori_loop(0, rows, scatter, 0, unroll=ROW_LOOP_UNROLL)
            wait_gather()
            cp = pltpu.make_async_copy(acc_scr, out_hbm, osem)
            cp.start()
            cp.wait()

    @pl.when(e % 2 == 0)
    def _():
        step(ye_a, ye_b)

    @pl.when(e % 2 == 1)
    def _():
        step(ye_b, ye_a)


def _moe_call(idx, gate, h2, w1, w3, w2, layer, tf=256):
    d = w1.shape[2]
    n_blk = d // LANES
    n = h2.shape[0] // n_blk
    ne, rows = idx.shape
    dff = w1.shape[3]
    nf = dff // tf
    rows_pad = -(-rows // (8 * nf)) * 8 * nf
    pad = ((1, 1), (0, rows_pad - rows))
    idx_all = jnp.pad(idx, pad).reshape(-1)
    gate_all = jnp.pad(gate, pad).reshape(-1)
    grid_spec = pltpu.PrefetchScalarGridSpec(
        num_scalar_prefetch=1,
        grid=(ne, nf),
        in_specs=[
            pl.BlockSpec(memory_space=pltpu.SMEM),
            pl.BlockSpec(memory_space=pl.ANY),
            pl.BlockSpec((None, None, d, tf), lambda e, f, idx: (layer, e, 0, f)),
            pl.BlockSpec((None, None, d, tf), lambda e, f, idx: (layer, e, 0, f)),
            pl.BlockSpec((None, None, tf, d), lambda e, f, idx: (layer, e, f, 0)),
        ],
        out_specs=pl.BlockSpec(memory_space=pl.ANY),
        scratch_shapes=[
            pltpu.VMEM((rows_pad * n_blk, LANES), F32), pltpu.VMEM((rows, d), BF16),
            pltpu.VMEM((rows_pad, d), F32), pltpu.VMEM((rows_pad, d), F32),
            pltpu.VMEM((n, d), F32), pltpu.SemaphoreType.DMA, pltpu.SemaphoreType.DMA,
        ],
    )
    return pl.pallas_call(
        functools.partial(_moe_kernel, rows=rows, rows_pad=rows_pad), grid_spec=grid_spec,
        out_shape=jax.ShapeDtypeStruct((n, d), F32),
        compiler_params=_cparams(("arbitrary", "arbitrary"), 62), name="expert_ffn",
    )(idx_all, gate_all, h2, w1, w3, w2)


def _rope_tables(t):
    rows = t // GRID_W
    row = jnp.repeat(jnp.arange(rows, dtype=F32), GRID_W)
    colp = jnp.tile(jnp.arange(GRID_W, dtype=F32), rows)
    n_freq = HEAD_DIM // 4
    inv = ROPE_BASE ** (-jnp.arange(n_freq, dtype=F32) / n_freq)
    ang = jnp.concatenate([row[:, None] * inv, colp[:, None] * inv], axis=-1)
    return jnp.cos(ang), jnp.sin(ang)


def kernel(x_prompt, x_sample, c, state_mlstm_C, state_mlstm_n, state_mlstm_m, state_ret_S, c_ctx, w_mod, b_mod,
           norm1_w, norm2_w, w_in, mlstm_if_b, mlstm_norm_w, ret_decay_logit, ret_norm_w, w_branch_a, w_branch_b,
           w_out, router_w, ffn_w1, ffn_w3, ffn_w2, final_norm_w):
    bp, seq, d = x_prompt.shape
    db, dec_seq, _ = x_sample.shape
    depth = w_mod.shape[0]
    hm = mlstm_if_b.shape[-1]
    hr = ret_decay_logit.shape[-1]
    ne = router_w.shape[-1]
    n_prompt = bp * seq
    n_sample = db * dec_seq
    assert n_prompt == n_sample and hm * HEAD_DIM == d and hr * HEAD_DIM == d and hm == hr
    cap = EC_FACTOR * n_prompt // ne
    tm = 512
    tp = 2048
    geo = dict(n_prompt=n_prompt, dec_seq=dec_seq)

    cond8 = jnp.concatenate([c_ctx[None, :], c, jnp.zeros((8 - 1 - db, d), F32)], axis=0)
    mod = _mod_call(cond8, w_mod, b_mod)[:, :1 + db].reshape(depth, 1 + db, 6, d)

    n_m = 4 * d
    g0 = n_m + 4 * hm

    w_t = jnp.swapaxes(w_in, 1, 2)
    mq, mk, mv, mo = (k * d for k in range(4))
    rq, rk, rv, rg, ga, gb = (g0 + k * d for k in range(6))
    wg_t = w_t[:, n_m:g0]
    gate_bias = jnp.transpose(mlstm_if_b, (0, 2, 1, 3)).reshape(depth, 4 * hm, 1)
    router_wt = jnp.swapaxes(router_w, 1, 2)
    nw_m = jnp.broadcast_to(mlstm_norm_w[:, :, None], (depth, d, LANES))
    nw_r = jnp.broadcast_to(ret_norm_w[:, :, None], (depth, d, LANES))

    cos, sin = _rope_tables(dec_seq)
    cos_t = cos.T.reshape(HEAD_DIM // 2, dec_seq // tp, tp).transpose(1, 0, 2)
    sin_t = sin.T.reshape(HEAD_DIM // 2, dec_seq // tp, tp).transpose(1, 0, 2)

    x, h, gp = _resid_norm_call(x_prompt.reshape(n_prompt, d), None, None, mod[0], norm1_w[0], gate_row=0,
                                mod_rows=(0, 1), h_dtype=BF16, write_x=True,
                                x_tail=x_sample.reshape(n_sample, d), gate_prep=(wg_t[0], gate_bias[0], hm), **geo)
    m_bufs, s_buf = None, None
    y_prompt = y_sample = None
    for l in range(depth):
        proj_t = _proj_t_call(h, w_t, l, (mq, mv, mo, rq, rv, rg), (cos_t, sin_t), 3, tm=tp, **geo)
        proj_n = _proj_call(h, w_t, l, (mk, rk, ga, gb), (cos, sin), 1, tm=tp, **geo)

        mix = dict(n_heads=hm, layer=l, depth=depth)
        hm_p, *m_bufs = _mlstm_call(proj_t, proj_n, gp, nw_m[l], None, m_bufs, n_seq=bp, seq_len=seq, tok_off=0,
                                    **mix)
        (hm_s,) = _mlstm_call(proj_t, proj_n, gp, nw_m[l], (state_mlstm_C, state_mlstm_n, state_mlstm_m), None,
                              n_seq=db, seq_len=dec_seq, tok_off=n_prompt, **mix)
        hr_p, s_buf = _ret_call(proj_t, proj_n, ret_decay_logit[l], nw_r[l], None, s_buf, n_seq=bp, seq_len=seq,
                                tok_off=0, **mix)
        (hr_s,) = _ret_call(proj_t, proj_n, ret_decay_logit[l], nw_r[l], state_ret_S, None, n_seq=db,
                            seq_len=dec_seq, tok_off=n_prompt, **mix)

        x1, h2, aff_t = _merge_call(hm_p, hm_s, hr_p, hr_s, proj_n, x, mod[l], norm2_w[l], w_branch_a, w_branch_b,
                                    w_out, router_wt, l, tm=tm, **geo)
        idx, gate = _route_call(aff_t, 2, cap)
        idx = idx.reshape(2, ne, cap) + (jnp.arange(2, dtype=jnp.int32) * n_prompt)[:, None, None]
        idx = jnp.swapaxes(idx, 0, 1).reshape(ne, 2 * cap)
        gate = jnp.swapaxes(gate.reshape(2, ne, cap), 0, 1).reshape(ne, 2 * cap)
        moe = _moe_call(idx, gate, h2, ffn_w1, ffn_w3, ffn_w2, l)
        if l + 1 < depth:
            x, h, gp = _resid_norm_call(x1, moe, mod[l], mod[l + 1], norm1_w[l + 1], gate_row=5, mod_rows=(0, 1),
                                        h_dtype=BF16, write_x=True,
                                        gate_prep=(wg_t[l + 1], gate_bias[l + 1], hm), **geo)
        else:
            fin = dict(gate_row=5, mod_rows=None, h_dtype=F32, write_x=False, **geo)
            _, y_prompt = _resid_norm_call(x1, moe, mod[l], None, final_norm_w, row_off=0, n_rows=n_prompt, **fin)
            _, y_sample = _resid_norm_call(x1, moe, mod[l], None, final_norm_w, row_off=n_prompt, n_rows=n_sample,
                                           **fin)

    c_buf, n_buf, m_buf = m_bufs
    return (y_prompt.reshape(bp, seq, d), y_sample.reshape(db, dec_seq, d), c_buf, n_buf[:, :, :, :, 0, :],
            m_buf[:, :, :, :, 0, 0], s_buf)
```

```python
import functools

import jax
import jax.numpy as jnp
from jax import lax
from jax.experimental import pallas as pl
from jax.experimental.pallas import tpu as pltpu

F32 = jnp.float32
BF16 = jnp.bfloat16

GRID_W = 64
CHUNK = 128
LANES = 128
HEAD_DIM = 256
EC_FACTOR = 2
ROPE_BASE = 10000.0
EPS = 1e-6
V7X_VMEM_BYTES = 64 * 1024 * 1024
MIB = 1024 * 1024


def _cparams(semantics, vmem_mib):
    assert vmem_mib * MIB < V7X_VMEM_BYTES
    return pltpu.CompilerParams(dimension_semantics=semantics, vmem_limit_bytes=vmem_mib * MIB)


def _dot(a, b):
    return jnp.dot(a, b, preferred_element_type=F32)


def _dot_nt(a, b):
    return lax.dot_general(a, b, (((1,), (1,)), ((), ())), preferred_element_type=F32)


def _log_sigmoid(x):
    return -(jnp.maximum(-x, 0.0) + jnp.log1p(jnp.exp(-jnp.abs(x))))


def _split3(x):
    hi = x.astype(BF16)
    r1 = x - hi.astype(F32)
    mid = r1.astype(BF16)
    lo = (r1 - mid.astype(F32)).astype(BF16)
    return hi, mid, lo


def _dot3(x, m):
    hi, mid, lo = _split3(x)
    return _dot(hi, m) + _dot(mid, m) + _dot(lo, m)


def _iota(shape, dim):
    return lax.broadcasted_iota(jnp.int32, shape, dim)


def _tile_rows(ref, fb, n_tok, n_blk):
    return ref.at[pl.ds(fb, n_tok, stride=n_blk), :]


def _mod_kernel(cond_ref, w_ref, b_ref, out_ref):
    c = cond_ref[...]
    s = (c * jax.nn.sigmoid(c)).astype(BF16)
    out_ref[...] = _dot(s, w_ref[...].astype(BF16)) + b_ref[...]


def _mod_call(cond8, w_mod, b_mod):
    depth, d, w6 = w_mod.shape
    tn = 1536
    return pl.pallas_call(
        _mod_kernel,
        grid=(depth, w6 // tn),
        in_specs=[
            pl.BlockSpec((8, d), lambda l, j: (0, 0)),
            pl.BlockSpec((None, d, tn), lambda l, j: (l, 0, j)),
            pl.BlockSpec((None, 1, tn), lambda l, j: (l, 0, j)),
        ],
        out_specs=pl.BlockSpec((None, 8, tn), lambda l, j: (l, 0, j)),
        out_shape=jax.ShapeDtypeStruct((depth, 8, w6), F32),
        compiler_params=_cparams(("parallel", "parallel"), 32),
        name="adaln_mod",
    )(cond8, w_mod, b_mod.reshape(depth, 1, w6))


def _group_of_tile(i, tm, n_prompt, dec_seq):
    return jnp.maximum(i * tm - n_prompt + dec_seq, 0) // dec_seq


def _resid_norm_kernel(*refs, has_delta, has_mod, gate_row, mod_rows, write_x, n_first_tiles, gate_heads):
    it = iter(refs)
    x_ref = next(it)
    x2_ref = next(it) if n_first_tiles else None
    delta_ref = next(it) if has_delta else None
    mod_ref = next(it) if (has_delta or has_mod) else None
    nw_ref = next(it)
    wg_ref, gb_ref = (next(it), next(it)) if gate_heads else (None, None)
    xo_ref = next(it) if write_x else None
    h_ref = next(it)
    gp_ref = next(it) if gate_heads else None
    x = x_ref[...]
    if n_first_tiles:
        x = jnp.where(pl.program_id(0) < n_first_tiles, x, x2_ref[...])
    if has_delta:
        x = x + mod_ref[gate_row:gate_row + 1, :] * delta_ref[...]
    if write_x:
        xo_ref[...] = x
    y = x * lax.rsqrt(jnp.mean(x * x, axis=-1, keepdims=True) + EPS) * nw_ref[...]
    if has_mod:
        sh_row, sc_row = mod_rows
        y = y * (1.0 + mod_ref[sc_row:sc_row + 1, :]) + mod_ref[sh_row:sh_row + 1, :]
    h = y.astype(h_ref.dtype)
    h_ref[...] = h
    if gate_heads:
        _gate_prep(h, wg_ref, gb_ref, gp_ref, gate_heads)


def _resid_norm_call(x, delta, mod_gate, mod_norm, norm_w, *, gate_row, mod_rows, h_dtype, write_x,
                     n_prompt, dec_seq, tm=1024, row_off=0, n_rows=None, x_tail=None, gate_prep=None):
    n, d = x.shape
    has_delta = delta is not None
    has_mod = mod_norm is not None
    grp = functools.partial(_group_of_tile, tm=tm, n_prompt=n_prompt, dec_seq=dec_seq)
    if x_tail is None:
        n_rows = n if n_rows is None else n_rows
        toff = row_off // tm
        nft = 0
        tile_in = pl.BlockSpec((tm, d), lambda i: (i + toff, 0))
        args, specs = [x], [tile_in]
    else:
        assert row_off == 0 and n_rows is None and not has_delta
        n_rows, toff, nft = n + x_tail.shape[0], 0, n // tm
        args = [x, x_tail]
        specs = [pl.BlockSpec((tm, d), lambda i: (jnp.minimum(i, nft - 1), 0)),
                 pl.BlockSpec((tm, d), lambda i: (jnp.maximum(i - nft, 0), 0))]
    tile_out = pl.BlockSpec((tm, d), lambda i: (i, 0))
    if has_delta:
        args.append(delta)
        specs.append(tile_in)
    if has_delta or has_mod:
        mg = mod_gate if has_delta else mod_norm
        mn = mod_norm if has_mod else mod_gate
        args.append(jnp.concatenate([mg, mn], axis=1))
        specs.append(pl.BlockSpec((None, 12, d), lambda i: (grp(i + toff), 0, 0)))
    args.append(norm_w.reshape(1, d))
    specs.append(pl.BlockSpec((1, d), lambda i: (0, 0)))
    n_heads = 0
    if gate_prep is not None:
        wg_t, bias_col, n_heads = gate_prep
        args += [wg_t, bias_col]
        specs += [pl.BlockSpec(wg_t.shape, lambda i: (0, 0)), pl.BlockSpec(bias_col.shape, lambda i: (0, 0))]
    out_shape, out_specs = [], []
    if write_x:
        out_shape.append(jax.ShapeDtypeStruct((n_rows, d), F32))
        out_specs.append(tile_out)
    out_shape.append(jax.ShapeDtypeStruct((n_rows, d), h_dtype))
    out_specs.append(tile_out)
    if gate_prep is not None:
        out_shape.append(jax.ShapeDtypeStruct((n_heads, n_rows // CHUNK, 4, CHUNK), F32))
        out_specs.append(pl.BlockSpec((n_heads, tm // CHUNK, 4, CHUNK), lambda i: (0, i, 0, 0)))
    kern = functools.partial(
        _resid_norm_kernel, has_delta=has_delta, has_mod=has_mod, gate_row=gate_row,
        mod_rows=None if mod_rows is None else (6 + mod_rows[0], 6 + mod_rows[1]), write_x=write_x,
        n_first_tiles=nft, gate_heads=n_heads)
    outs = pl.pallas_call(
        kern, grid=(n_rows // tm,), in_specs=specs, out_specs=out_specs, out_shape=out_shape,
        compiler_params=_cparams(("parallel",), 48), name="resid_norm",
    )(*args)
    return outs if write_x else (None, *outs)


def _rope_pair(x1, x2, cos, sin):
    return x1 * cos - x2 * sin, x1 * sin + x2 * cos


def _weight_rows_spec(layer, row_starts, tn, d):
    def index_map(j, i):
        start = sum(jnp.where(j == k, s, 0) for k, s in enumerate(row_starts))
        return layer, pl.multiple_of(start, 8), 0
    assert all(s % 8 == 0 for s in row_starts)
    return pl.BlockSpec((pl.Element(1), pl.Element(tn), pl.Element(d)), index_map)


def _proj_kernel(h_ref, w_ref, cos_ref, sin_ref, out_ref, wb_scr, *, tm, rope_tile, n_prompt, dec_seq):
    j = pl.program_id(0)
    i = pl.program_id(1)

    @pl.when(i == 0)
    def _():
        wb_scr[...] = w_ref[0].astype(BF16)

    row0 = pl.multiple_of(i * tm, tm)
    use_rope = (j == rope_tile) & (row0 >= n_prompt)
    half = HEAD_DIM // 2

    def heads(rope):
        h = h_ref[...]
        if rope:
            pos0 = pl.multiple_of((row0 - n_prompt) % dec_seq, tm)
            cos = cos_ref[pl.ds(pos0, tm), :]
            sin = sin_ref[pl.ds(pos0, tm), :]
        for hh in range(wb_scr.shape[0] // HEAD_DIM):
            c0 = hh * HEAD_DIM
            acc = _dot_nt(h, wb_scr[c0:c0 + HEAD_DIM, :])
            if rope:
                y1, y2 = _rope_pair(acc[:, :half], acc[:, half:], cos, sin)
                out_ref[:, c0:c0 + half] = y1.astype(out_ref.dtype)
                out_ref[:, c0 + half:c0 + HEAD_DIM] = y2.astype(out_ref.dtype)
            else:
                out_ref[:, c0:c0 + HEAD_DIM] = acc.astype(out_ref.dtype)

    pl.when(use_rope)(functools.partial(heads, True))
    pl.when(jnp.logical_not(use_rope))(functools.partial(heads, False))


def _proj_call(h, w_t, layer, row_starts, rope, rope_tile, *, n_prompt, dec_seq, tm=512, tn=1024):
    n, d = h.shape
    nj = len(row_starts)
    kern = functools.partial(_proj_kernel, tm=tm, rope_tile=rope_tile, n_prompt=n_prompt, dec_seq=dec_seq)
    return pl.pallas_call(
        kern, grid=(nj, n // tm),
        in_specs=[
            pl.BlockSpec((tm, d), lambda j, i: (i, 0)),
            _weight_rows_spec(layer, row_starts, tn, d),
            pl.BlockSpec(rope[0].shape, lambda j, i: (0, 0)),
            pl.BlockSpec(rope[1].shape, lambda j, i: (0, 0)),
        ],
        out_specs=pl.BlockSpec((tm, tn), lambda j, i: (i, j)),
        out_shape=jax.ShapeDtypeStruct((n, nj * tn), BF16),
        scratch_shapes=[pltpu.VMEM((tn, d), BF16)],
        compiler_params=_cparams(("parallel", "arbitrary"), 56), name="in_proj",
    )(h, w_t, *rope)


def _proj_t_kernel(h_ref, w_ref, cos_ref, sin_ref, out_ref, wb_scr, *, tm, rope_tile, n_prompt, dec_seq):
    j = pl.program_id(0)
    i = pl.program_id(1)

    @pl.when(i == 0)
    def _():
        wb_scr[...] = w_ref[0].astype(BF16)

    row0 = pl.multiple_of(i * tm, tm)
    use_rope = (j == rope_tile) & (row0 >= n_prompt)
    half = HEAD_DIM // 2

    def heads(rope):
        h = h_ref[...]
        if rope:
            blk = ((row0 - n_prompt) % dec_seq) // tm
            cos = cos_ref[blk]
            sin = sin_ref[blk]
        for hh in range(wb_scr.shape[0] // HEAD_DIM):
            r0 = hh * HEAD_DIM
            acc = _dot_nt(wb_scr[r0:r0 + HEAD_DIM, :], h)
            if rope:
                y1, y2 = _rope_pair(acc[:half, :], acc[half:, :], cos, sin)
                y = jnp.concatenate([y1, y2], axis=0).astype(out_ref.dtype)
            else:
                y = acc.astype(out_ref.dtype)
            for s in range(tm // CHUNK):
                out_ref[s, r0:r0 + HEAD_DIM, :] = y[:, s * CHUNK:(s + 1) * CHUNK]

    pl.when(use_rope)(functools.partial(heads, True))
    pl.when(jnp.logical_not(use_rope))(functools.partial(heads, False))


def _proj_t_call(h, w_t, layer, row_starts, rope_t, rope_tile, *, n_prompt, dec_seq, tm=512, tn=1024):
    n, d = h.shape
    nj = len(row_starts)
    kern = functools.partial(_proj_t_kernel, tm=tm, rope_tile=rope_tile, n_prompt=n_prompt, dec_seq=dec_seq)
    return pl.pallas_call(
        kern, grid=(nj, n // tm),
        in_specs=[
            pl.BlockSpec((tm, d), lambda j, i: (i, 0)),
            _weight_rows_spec(layer, row_starts, tn, d),
            pl.BlockSpec(rope_t[0].shape, lambda j, i: (0, 0, 0)),
            pl.BlockSpec(rope_t[1].shape, lambda j, i: (0, 0, 0)),
        ],
        out_specs=pl.BlockSpec((tm // CHUNK, tn, CHUNK), lambda j, i: (i, j, 0)),
        out_shape=jax.ShapeDtypeStruct((n // CHUNK, nj * tn, CHUNK), BF16),
        scratch_shapes=[pltpu.VMEM((tn, d), BF16)],
        compiler_params=_cparams(("parallel", "arbitrary"), 56), name="in_proj_t",
    )(h, w_t, *rope_t)


def _gate_prep(h, wg_ref, bias_ref, out_ref, n_heads):
    tm = h.shape[0]
    g = _dot_nt(wg_ref[...].astype(BF16), h) + bias_ref[...]
    nd = 2 * n_heads
    ig = g[0:nd, :]
    lf = _log_sigmoid(g[nd:2 * nd, :])
    r = _iota((CHUNK, CHUNK), 0)
    c = _iota((CHUNK, CHUNK), 1)
    upper = jnp.where(r <= c, 1.0, 0.0).astype(BF16)
    lower = jnp.where(r >= c, 1.0, 0.0).astype(BF16)
    is_fwd = _iota((nd, CHUNK), 0) < n_heads
    for s in range(tm // CHUNK):
        sl = slice(s * CHUNK, (s + 1) * CHUNK)
        lf_c = lf[:, sl]
        b = jnp.where(is_fwd, _dot3(lf_c, upper), _dot3(lf_c, lower))
        ig_c = ig[:, sl]
        for hh in range(n_heads):
            out_ref[hh, s, 0:1, :] = ig_c[hh:hh + 1, :]
            out_ref[hh, s, 1:2, :] = ig_c[n_heads + hh:n_heads + hh + 1, :]
            out_ref[hh, s, 2:3, :] = b[hh:hh + 1, :]
            out_ref[hh, s, 3:4, :] = b[n_heads + hh:n_heads + hh + 1, :]


def _scan_loop(nc, body, init):
    if nc <= 2:
        carry = init
        for ci in range(nc):
            carry = body(ci, carry)
        return carry
    return lax.fori_loop(0, nc, body, init)


MAX_GROUP = 16
BF16_ROWS = 16
STRIP = 32


def _seqs_per_step(n_seq, nc):
    ns = max(1, MAX_GROUP // nc)
    while n_seq % ns:
        ns -= 1
    return ns


def _group_loop(n_groups, body):
    if n_groups == 1:
        body(0, 0)
    else:
        lax.fori_loop(0, n_groups, body, 0)


def _chunk_off(c):
    return c * CHUNK if isinstance(c, int) else pl.multiple_of(c * CHUNK, CHUNK)


def _mlstm_kernel(*refs, nc, ns, has_init, first_layer, write_state):
    it = iter(refs)
    qt_ref, vt_ref, ot_ref, k_ref, gp_ref, nw_ref = (next(it) for _ in range(6))
    if has_init:
        c0_ref, n0_ref, m0_ref = next(it), next(it), next(it)
    if write_state and not first_layer:
        next(it), next(it), next(it)
    hm_ref = next(it)
    if write_state:
        cout_ref, nout_ref, mout_ref = next(it), next(it), next(it)
    ct_scr, n_scr, part_scr, inc_scr, ninc_scr, rows_scr, cst_scr, nst_scr, coef_scr = (next(it) for _ in range(9))

    L = CHUNK
    grp = min(MAX_GROUP, ns * nc)
    row = _iota((L, L), 0)
    col = _iota((L, L), 1)
    k_scale = HEAD_DIM ** -0.5
    b_idx = pl.program_id(0)
    h_idx = pl.program_id(1)
    n_heads = pl.num_programs(1)

    def gates(d, c):
        g = gp_ref[c]
        ig = g[d:d + 1, :]
        brow = g[2 + d:3 + d, :]
        blast = brow[:, L - 1:L] if d == 0 else brow[:, 0:1]
        return ig, brow, blast

    def local_group(g, carry):
        cs = [g * grp + u for u in range(grp)]
        pairs = [(u, d) for u in range(grp) for d in range(2)]
        qt = [qt_ref[c] for c in cs]
        vt = [vt_ref[c] for c in cs]
        k = [k_ref[pl.ds(_chunk_off(c), L), :] for c in cs]
        qk = [_dot(k[u], qt[u]) for u in range(grp)]
        vf = [v.astype(F32) for v in vt]
        gts = {(u, d): gates(d, cs[u]) for u, d in pairs}
        for u, c in enumerate(cs):
            lhs = []
            for d in range(2):
                ig, brow, blast = gts[u, d]
                wj = blast - brow + ig
                mloc2 = jnp.max(wj, axis=1, keepdims=True)
                e = jnp.exp(wj - mloc2) * k_scale
                rows_scr[d, c, 2:3, :] = jnp.broadcast_to(mloc2, (1, L))
                rows_scr[d, c, 4:5, :] = jnp.broadcast_to(blast, (1, L))
                lhs.append(((vf[u] * e).astype(BF16), _split3(jnp.broadcast_to(e, (BF16_ROWS, L)))))
            res = _dot(jnp.concatenate([lhs[0][0], lhs[1][0], *lhs[0][1], *lhs[1][1]], axis=0), k[u])
            for d in range(2):
                inc_scr[d, c] = res[d * HEAD_DIM:(d + 1) * HEAD_DIM, :]
                r0 = 2 * HEAD_DIM + 3 * BF16_ROWS * d
                ninc_scr[d, c] = (res[r0:r0 + 8, :] + res[r0 + BF16_ROWS:r0 + BF16_ROWS + 8, :]
                                  + res[r0 + 2 * BF16_ROWS:r0 + 2 * BF16_ROWS + 8, :])
        for u, d in pairs:
            c = cs[u]
            ig, brow, blast = gts[u, d]
            key_term = jnp.broadcast_to(ig - brow, (L, L)).T
            causal = (row <= col) if d == 0 else (row >= col)
            dm = jnp.where(causal, key_term + brow, -jnp.inf)
            mloc = jnp.max(dm, axis=0, keepdims=True)
            s = qk[u] * (jnp.exp(dm - mloc) * k_scale)
            part_scr[d, c] = _dot(vt[u], s.astype(BF16))
            rows_scr[d, c, 0:1, :] = jnp.sum(s, axis=0, keepdims=True)
            rows_scr[d, c, 1:2, :] = mloc
        return carry

    _group_loop(ns * nc // grp, local_group)

    def m_step(d, c, m):
        blast = rows_scr[d, c, 4:5, :]
        mloc2 = rows_scr[d, c, 2:3, :]
        rows_scr[d, c, 3:4, :] = m
        m_new = jnp.maximum(blast + m, mloc2)
        decay = jnp.exp(blast + m - m_new)
        w_inc = jnp.exp(mloc2 - m_new)
        coef_scr[d, c, 0:1, :] = jnp.concatenate([decay] * (HEAD_DIM // L), axis=1)
        coef_scr[d, c, 1:2, :] = jnp.concatenate([w_inc] * (HEAD_DIM // L), axis=1)
        return m_new

    for sq in range(ns):
        c_lo = sq * nc
        m_init = []
        for d in range(2):
            if has_init:
                ct_scr[d] = c0_ref[d].T
                n_scr[d] = jnp.broadcast_to(n0_ref[d], n_scr.shape[1:])
                m_init.append(jnp.full((1, L), m0_ref[(b_idx * 2 + d) * n_heads + h_idx], F32))
            else:
                ct_scr[d] = jnp.zeros(ct_scr.shape[1:], F32)
                n_scr[d] = jnp.zeros(n_scr.shape[1:], F32)
                m_init.append(jnp.zeros((1, L), F32))

        def m_body(ci, ms, c_lo=c_lo):
            return m_step(0, c_lo + ci, ms[0]), m_step(1, c_lo + nc - 1 - ci, ms[1])

        m_fin = _scan_loop(nc, m_body, tuple(m_init))

        def body(ci, carry, c_lo=c_lo):
            for d, c in ((0, c_lo + ci), (1, c_lo + nc - 1 - ci)):
                decay = coef_scr[d, c, 0:1, :]
                w_inc = coef_scr[d, c, 1:2, :]
                n8 = n_scr[d]
                nst_scr[d, c] = n8
                for r0 in range(0, HEAD_DIM, STRIP):
                    ct = ct_scr[d, r0:r0 + STRIP, :]
                    cst_scr[d, c, r0:r0 + STRIP, :] = ct.astype(BF16)
                    ct_scr[d, r0:r0 + STRIP, :] = decay * ct + w_inc * inc_scr[d, c, r0:r0 + STRIP, :]
                n_scr[d] = decay * n8 + w_inc * ninc_scr[d, c]
            return carry

        _scan_loop(nc, body, 0)

        if write_state:
            lsel = (lambda d, sq=sq: (sq, 0, d)) if first_layer else (lambda d, sq=sq: (sq, d))
            for d in range(2):
                cout_ref[lsel(d)] = ct_scr[d].T
                nout_ref[lsel(d)] = n_scr[d][0:1, :]
                mout_ref[lsel(d)] = m_fin[d]
            if first_layer:
                for l in range(1, cout_ref.shape[1]):
                    cout_ref[sq, l] = jnp.zeros(cout_ref.shape[2:], F32)
                    nout_ref[sq, l] = jnp.zeros(nout_ref.shape[2:], F32)
                    mout_ref[sq, l] = jnp.zeros(mout_ref.shape[2:], F32)

    def finish_group(g, carry):
        cs = [g * grp + u for u in range(grp)]
        qt = [qt_ref[c] for c in cs]
        cq, nq = [], []
        for u, c in enumerate(cs):
            n16 = [jnp.concatenate([nst_scr[d, c]] * (BF16_ROWS // 8), axis=0).astype(BF16) for d in range(2)]
            res = _dot(jnp.concatenate([cst_scr[0, c], cst_scr[1, c], n16[0], n16[1]], axis=0), qt[u])
            cq.append([res[d * HEAD_DIM:(d + 1) * HEAD_DIM, :] for d in range(2)])
            nq.append([res[2 * HEAD_DIM + d * BF16_ROWS:2 * HEAD_DIM + d * BF16_ROWS + 1, :] for d in range(2)])
        for u, c in enumerate(cs):
            coef = []
            for d in range(2):
                _, brow, _ = gates(d, c)
                den_loc = rows_scr[d, c, 0:1, :]
                mloc = rows_scr[d, c, 1:2, :]
                inter = brow + rows_scr[d, c, 3:4, :]
                m_row = jnp.maximum(mloc, inter)
                w_loc = jnp.exp(mloc - m_row)
                w_inter = jnp.exp(inter - m_row)
                den = w_loc * den_loc + w_inter * nq[u][d]
                r_den = 1.0 / jnp.maximum(jnp.abs(den), jnp.exp(-m_row))
                coef.append((w_loc * r_den, w_inter * r_den))
            ssq = jnp.zeros((1, L), F32)
            for r0 in range(0, HEAD_DIM, STRIP):
                h_s = None
                for d in range(2):
                    h_d = coef[d][0] * part_scr[d, c, r0:r0 + STRIP, :] + coef[d][1] * cq[u][d][r0:r0 + STRIP, :]
                    h_s = h_d if h_s is None else h_s + h_d
                part_scr[0, c, r0:r0 + STRIP, :] = h_s
                ssq = ssq + jnp.sum(h_s * h_s, axis=0, keepdims=True)
            r_norm = lax.rsqrt(ssq * (1.0 / HEAD_DIM) + EPS)
            for r0 in range(0, HEAD_DIM, L):
                y = part_scr[0, c, r0:r0 + L, :] * r_norm * nw_ref[r0:r0 + L, :]
                o = jax.nn.sigmoid(ot_ref[c, r0:r0 + L, :].astype(F32)) * y
                hm_ref[pl.ds(_chunk_off(c), L), r0:r0 + L] = o.T.astype(hm_ref.dtype)
        return carry

    _group_loop(ns * nc // grp, finish_group)


def _state_specs(shapes, layer, first_layer, ns):
    specs = []
    for shp in shapes:
        tail = shp[4:]
        zeros = (0,) * len(tail)
        if first_layer:
            specs.append(pl.BlockSpec((ns, shp[1], 2, None) + tail, lambda b, h, z=zeros: (b, 0, 0, h) + z))
        else:
            specs.append(pl.BlockSpec((ns, None, 2, None) + tail, lambda b, h, z=zeros: (b, layer, 0, h) + z))
    return specs


def _mlstm_call(proj_t, proj_n, gp, nw_b, init, state_bufs, *, n_seq, seq_len, tok_off, n_heads, layer, depth):
    dk = HEAD_DIM
    nc = seq_len // CHUNK
    has_init = init is not None
    ns = 1 if has_init else _seqs_per_step(n_seq, nc)
    nct = ns * nc
    boff = tok_off // (ns * seq_len)
    write_state = not has_init
    first_layer = layer == 0
    args = [proj_t, proj_t, proj_t, proj_n, gp, nw_b]
    specs = [
        pl.BlockSpec((nct, dk, CHUNK), lambda b, h: (b + boff, h, 0)),
        pl.BlockSpec((nct, dk, CHUNK), lambda b, h: (b + boff, n_heads + h, 0)),
        pl.BlockSpec((nct, dk, CHUNK), lambda b, h: (b + boff, 2 * n_heads + h, 0)),
        pl.BlockSpec((ns * seq_len, dk), lambda b, h: (b + boff, h)),
        pl.BlockSpec((None, nct, 4, CHUNK), lambda b, h: (h, b + boff, 0, 0)),
        pl.BlockSpec((dk, LANES), lambda b, h: (h, 0)),
    ]
    aliases = {}
    if has_init:
        c0, n0, m0 = init
        args += [c0, n0[:, layer][:, :, :, None, :], m0[:, layer].reshape(-1)]
        specs += [
            pl.BlockSpec((None, None, 2, None, dk, dk), lambda b, h: (b, layer, 0, h, 0, 0)),
            pl.BlockSpec((None, 2, None, 1, dk), lambda b, h: (b, 0, h, 0, 0)),
            pl.BlockSpec(memory_space=pltpu.SMEM),
        ]
    out_shape = [jax.ShapeDtypeStruct((n_seq * seq_len, n_heads * dk), BF16)]
    out_specs = [pl.BlockSpec((ns * seq_len, dk), lambda b, h: (b, h))]
    if write_state:
        shapes = [(n_seq, depth, 2, n_heads, dk, dk), (n_seq, depth, 2, n_heads, 1, dk),
                  (n_seq, depth, 2, n_heads, 1, LANES)]
        if not first_layer:
            for k_, buf in enumerate(state_bufs):
                aliases[len(args)] = 1 + k_
                args.append(buf)
                specs.append(pl.BlockSpec(memory_space=pl.ANY))
        out_shape += [jax.ShapeDtypeStruct(s, F32) for s in shapes]
        out_specs += _state_specs(shapes, layer, first_layer, ns)
    kern = functools.partial(_mlstm_kernel, nc=nc, ns=ns, has_init=has_init, first_layer=first_layer,
                             write_state=write_state)
    return pl.pallas_call(
        kern, grid=(n_seq // ns, n_heads), in_specs=specs, out_specs=out_specs, out_shape=out_shape,
        input_output_aliases=aliases,
        scratch_shapes=[pltpu.VMEM((2, dk, dk), F32), pltpu.VMEM((2, 8, dk), F32),
                        pltpu.VMEM((2, nct, dk, CHUNK), F32), pltpu.VMEM((2, nct, dk, dk), F32),
                        pltpu.VMEM((2, nct, 8, dk), F32), pltpu.VMEM((2, nct, 8, CHUNK), F32),
                        pltpu.VMEM((2, nct, dk, dk), BF16), pltpu.VMEM((2, nct, 8, dk), F32),
                        pltpu.VMEM((2, nct, 8, dk), F32)],
        compiler_params=_cparams(("parallel", "parallel"), 48), name="mlstm_mixer",
    )(*args)


def _ret_kernel(*refs, nc, ns, has_init, first_layer, write_state):
    it = iter(refs)
    dl_ref, qt_ref, vt_ref, gt_ref, k_ref, nw_ref = (next(it) for _ in range(6))
    if has_init:
        s0_ref = next(it)
    if write_state and not first_layer:
        next(it)
    hr_ref = next(it)
    if write_state:
        sout_ref = next(it)
    st_scr, part_scr, inc_scr, sst_scr = next(it), next(it), next(it), next(it)

    L = CHUNK
    grp = min(MAX_GROUP, ns * nc)
    rowi = _iota((L, L), 0)
    coli = _iota((L, L), 1)
    k_scale = HEAD_DIM ** -0.5
    h_idx = pl.program_id(1)
    n_heads = pl.num_programs(1)

    consts = []
    for d in range(2):
        lg = _log_sigmoid(jnp.full((1, 1), dl_ref[d * n_heads + h_idx], F32))
        rel = (coli - rowi if d == 0 else rowi - coli).astype(F32)
        dmat_t = jnp.where(rel >= 0, jnp.exp(jnp.maximum(rel, 0.0) * lg), 0.0) * k_scale
        pos = _iota((1, L), 1).astype(F32)
        if d == 1:
            pos = (L - 1.0) - pos
        q_decay = jnp.exp((pos + 1.0) * lg)
        k_decay = jnp.exp((L - 1.0 - pos) * lg) * k_scale
        chunk_decay = jnp.exp(float(L) * lg)
        consts.append((dmat_t, q_decay, k_decay, chunk_decay))
    dmat_both = consts[0][0] + consts[1][0]

    def local_group(g, carry):
        cs = [g * grp + u for u in range(grp)]
        qt = [qt_ref[c] for c in cs]
        vt = [vt_ref[c] for c in cs]
        k = [k_ref[pl.ds(_chunk_off(c), L), :] for c in cs]
        qk = [_dot(k[u], qt[u]) for u in range(grp)]
        for u, c in enumerate(cs):
            vf = vt[u].astype(F32)
            res = _dot(jnp.concatenate([(vf * consts[d][2]).astype(BF16) for d in range(2)], axis=0), k[u])
            for d in range(2):
                inc_scr[d, c] = res[d * HEAD_DIM:(d + 1) * HEAD_DIM, :]
        for u, c in enumerate(cs):
            part_scr[c] = _dot(vt[u], (qk[u] * dmat_both).astype(BF16))
        return carry

    _group_loop(ns * nc // grp, local_group)

    for sq in range(ns):
        c_lo = sq * nc
        for d in range(2):
            if has_init:
                st_scr[d] = s0_ref[d].T
            else:
                st_scr[d] = jnp.zeros(st_scr.shape[1:], F32)

        def body(ci, carry, c_lo=c_lo):
            for d, c in ((0, c_lo + ci), (1, c_lo + nc - 1 - ci)):
                for r0 in range(0, HEAD_DIM, STRIP):
                    st = st_scr[d, r0:r0 + STRIP, :]
                    sst_scr[d, c, r0:r0 + STRIP, :] = st.astype(BF16)
                    st_scr[d, r0:r0 + STRIP, :] = consts[d][3] * st + inc_scr[d, c, r0:r0 + STRIP, :]
            return carry

        _scan_loop(nc, body, 0)

        if write_state:
            for d in range(2):
                if first_layer:
                    sout_ref[sq, 0, d] = st_scr[d].T
                else:
                    sout_ref[sq, d] = st_scr[d].T
            if first_layer:
                for l in range(1, sout_ref.shape[1]):
                    sout_ref[sq, l] = jnp.zeros(sout_ref.shape[2:], F32)

    def finish_group(g, carry):
        cs = [g * grp + u for u in range(grp)]
        qt = [qt_ref[c] for c in cs]
        sq = [_dot(jnp.concatenate([sst_scr[0, c], sst_scr[1, c]], axis=0), qt[u]) for u, c in enumerate(cs)]
        for u, c in enumerate(cs):
            ssq = jnp.zeros((1, L), F32)
            for r0 in range(0, HEAD_DIM, STRIP):
                sl = slice(r0, r0 + STRIP)
                o_s = part_scr[c, sl, :] + (consts[0][1] * sq[u][r0:r0 + STRIP, :]
                                            + consts[1][1] * sq[u][HEAD_DIM + r0:HEAD_DIM + r0 + STRIP, :])
                part_scr[c, sl, :] = o_s
                ssq = ssq + jnp.sum(o_s * o_s, axis=0, keepdims=True)
            r_norm = lax.rsqrt(ssq * (1.0 / HEAD_DIM) + EPS)
            for r0 in range(0, HEAD_DIM, L):
                y = part_scr[c, r0:r0 + L, :] * r_norm * nw_ref[r0:r0 + L, :]
                rg = gt_ref[c, r0:r0 + L, :].astype(F32)
                hr_ref[pl.ds(_chunk_off(c), L), r0:r0 + L] = (rg * jax.nn.sigmoid(rg) * y).T.astype(hr_ref.dtype)
        return carry

    _group_loop(ns * nc // grp, finish_group)


def _ret_call(proj_t, proj_n, decay_logit, nw_b, init, state_buf, *, n_seq, seq_len, tok_off, n_heads, layer,
              depth):
    dk = HEAD_DIM
    nc = seq_len // CHUNK
    has_init = init is not None
    ns = 1 if has_init else _seqs_per_step(n_seq, nc)
    nct = ns * nc
    boff = tok_off // (ns * seq_len)
    write_state = not has_init
    first_layer = layer == 0
    args = [decay_logit.reshape(-1), proj_t, proj_t, proj_t, proj_n, nw_b]
    specs = [
        pl.BlockSpec(memory_space=pltpu.SMEM),
        pl.BlockSpec((nct, dk, CHUNK), lambda b, h: (b + boff, 3 * n_heads + h, 0)),
        pl.BlockSpec((nct, dk, CHUNK), lambda b, h: (b + boff, 4 * n_heads + h, 0)),
        pl.BlockSpec((nct, dk, CHUNK), lambda b, h: (b + boff, 5 * n_heads + h, 0)),
        pl.BlockSpec((ns * seq_len, dk), lambda b, h: (b + boff, n_heads + h)),
        pl.BlockSpec((dk, LANES), lambda b, h: (h, 0)),
    ]
    aliases = {}
    if has_init:
        args.append(init)
        specs.append(pl.BlockSpec((None, None, 2, None, dk, dk), lambda b, h: (b, layer, 0, h, 0, 0)))
    out_shape = [jax.ShapeDtypeStruct((n_seq * seq_len, n_heads * dk), BF16)]
    out_specs = [pl.BlockSpec((ns * seq_len, dk), lambda b, h: (b, h))]
    if write_state:
        shapes = [(n_seq, depth, 2, n_heads, dk, dk)]
        if not first_layer:
            aliases[len(args)] = 1
            args.append(state_buf)
            specs.append(pl.BlockSpec(memory_space=pl.ANY))
        out_shape += [jax.ShapeDtypeStruct(s, F32) for s in shapes]
        out_specs += _state_specs(shapes, layer, first_layer, ns)
    kern = functools.partial(_ret_kernel, nc=nc, ns=ns, has_init=has_init, first_layer=first_layer,
                             write_state=write_state)
    return pl.pallas_call(
        kern, grid=(n_seq // ns, n_heads), in_specs=specs, out_specs=out_specs, out_shape=out_shape,
        input_output_aliases=aliases,
        scratch_shapes=[pltpu.VMEM((2, dk, dk), F32), pltpu.VMEM((nct, dk, CHUNK), F32),
                        pltpu.VMEM((2, nct, dk, dk), F32), pltpu.VMEM((2, nct, dk, dk), BF16)],
        compiler_params=_cparams(("parallel", "parallel"), 48), name="retention_mixer",
    )(*args)


def _merge_kernel(hmp_ref, hms_ref, hrp_ref, hrs_ref, ga_ref, gb_ref, x_ref, mod_ref, nw_ref, wa_ref, wb_ref,
                  wo_ref, wr_ref, x1_ref, h2_ref, aff_ref, wab_scr, wbb_scr, wob_scr, *, n_prompt_tiles):
    i = pl.program_id(0)

    @pl.when(i == 0)
    def _():
        wab_scr[...] = wa_ref[...].astype(BF16)
        wbb_scr[...] = wb_ref[...].astype(BF16)
        wob_scr[...] = wo_ref[...].astype(BF16)

    is_prompt = i < n_prompt_tiles
    hm = jnp.where(is_prompt, hmp_ref[...], hms_ref[...])
    hr = jnp.where(is_prompt, hrp_ref[...], hrs_ref[...])
    ya = _dot(hm, wab_scr[...])
    yb = _dot(hr, wbb_scr[...])
    merged = (jax.nn.sigmoid(ga_ref[...].astype(F32)) * ya + jax.nn.sigmoid(gb_ref[...].astype(F32)) * yb)
    y = _dot(merged.astype(BF16), wob_scr[...])
    x1 = x_ref[...] + mod_ref[2:3, :] * y
    x1_ref[...] = x1
    h2 = x1 * lax.rsqrt(jnp.mean(x1 * x1, axis=-1, keepdims=True) + EPS) * nw_ref[...]
    h2 = h2 * (1.0 + mod_ref[4:5, :]) + mod_ref[3:4, :]
    n_blk = h2.shape[1] // LANES
    for fb in range(n_blk):
        _tile_rows(h2_ref, fb, h2.shape[0], n_blk)[...] = h2[:, fb * LANES:(fb + 1) * LANES]
    logits = _dot_nt(wr_ref[...].astype(BF16), h2.astype(BF16))
    p = jnp.exp(logits - jnp.max(logits, axis=0, keepdims=True))
    aff_ref[...] = p / jnp.sum(p, axis=0, keepdims=True)


def _merge_call(hm_p, hm_s, hr_p, hr_s, proj_n, x, mod, norm_w, w_a, w_b, w_out, router_wt, layer, *,
                n_prompt, dec_seq, tm=512):
    n, d = x.shape
    ne = router_wt.shape[1]
    npt = n_prompt // tm
    grp = functools.partial(_group_of_tile, tm=tm, n_prompt=n_prompt, dec_seq=dec_seq)
    tile = pl.BlockSpec((tm, d), lambda i: (i, 0))
    tile_p = pl.BlockSpec((tm, d), lambda i: (jnp.minimum(i, npt - 1), 0))
    tile_s = pl.BlockSpec((tm, d), lambda i: (jnp.maximum(i - npt, 0), 0))
    full = pl.BlockSpec((None, d, d), lambda i: (layer, 0, 0))
    return pl.pallas_call(
        functools.partial(_merge_kernel, n_prompt_tiles=npt), grid=(n // tm,),
        in_specs=[
            tile_p, tile_s, tile_p, tile_s,
            pl.BlockSpec((tm, d), lambda i: (i, 2)),
            pl.BlockSpec((tm, d), lambda i: (i, 3)),
            tile,
            pl.BlockSpec((None, 6, d), lambda i: (grp(i), 0, 0)),
            pl.BlockSpec((1, d), lambda i: (0, 0)),
            full, full, full,
            pl.BlockSpec((None, ne, d), lambda i: (layer, 0, 0)),
        ],
        out_specs=[tile, pl.BlockSpec((tm * (d // LANES), LANES), lambda i: (i, 0)),
                   pl.BlockSpec((ne, tm), lambda i: (0, i))],
        out_shape=[jax.ShapeDtypeStruct((n, d), F32), jax.ShapeDtypeStruct((n * (d // LANES), LANES), F32),
                   jax.ShapeDtypeStruct((ne, n), F32)],
        scratch_shapes=[pltpu.VMEM((d, d), BF16)] * 3,
        compiler_params=_cparams(("arbitrary",), 56), name="merge_out_router",
    )(hm_p, hm_s, hr_p, hr_s, proj_n, proj_n, x, mod, norm_w.reshape(1, d), w_a, w_b, w_out, router_wt)


ROUTE_GROUP = 4


def _route_kernel(a_ref, at_ref, idx_ref, gate_ref, thr_scr, *, cap, n_tok):
    n_sets = a_ref.shape[0]
    nb = n_tok // LANES
    a_all = a_ref[...]

    def as_f32(bits):
        return lax.bitcast_convert_type(bits, F32)

    def count_ge(cand):
        m = jnp.where(a_all >= cand, 1.0, 0.0)
        return jnp.sum(jnp.sum(m, axis=1, keepdims=True), axis=2, keepdims=True)

    def bit_step(k, thr):
        cand = thr | lax.shift_left(jnp.int32(1), 30 - k)
        return jnp.where(count_ge(as_f32(cand)) >= cap, cand, thr)

    thr_scr[...] = lax.fori_loop(0, 31, bit_step, jnp.zeros((n_sets, 1, 1), jnp.int32))

    r128 = _iota((LANES, LANES), 0)
    c128 = _iota((LANES, LANES), 1)
    upper = jnp.where(r128 <= c128, 1.0, 0.0).astype(BF16)
    lower_t = jnp.where(r128 >= c128, 1.0, 0.0).astype(BF16)
    rb = _iota((nb, nb), 0)
    cb = _iota((nb, nb), 1)
    blk_before_rows = jnp.where(cb < rb, 1.0, 0.0).astype(BF16)
    blk_before_cols = jnp.where(rb < cb, 1.0, 0.0).astype(BF16)

    def incl_counts(mask):
        within = _dot(mask.astype(BF16), upper)
        before = _dot(blk_before_rows, within.astype(BF16))[:, LANES - 1:LANES]
        return within + before

    def incl_counts_t(mask_t):
        within = _dot(lower_t, mask_t.astype(BF16))
        before = _dot(within.astype(BF16), blk_before_cols)[LANES - 1:LANES, :]
        return within + before

    slot = _iota((1, cap), 1).astype(F32)
    blk_col = _iota((nb, 1), 0).astype(F32)
    sub_col = _iota((LANES, 1), 0).astype(F32)

    def per_group(g, carry):
        sets = [g + u * (n_sets // ROUTE_GROUP) for u in range(ROUTE_GROUP)]
        stage = []
        for s in sets:
            thr_bits = thr_scr[s]
            thr = as_f32(thr_bits)
            nxt = as_f32(thr_bits + 1)
            a = a_ref[s]
            a_t = at_ref[s]
            gt = jnp.where(a >= nxt, 1.0, 0.0)
            eq = jnp.where((a >= thr) & (a < nxt), 1.0, 0.0)
            gt_t = jnp.where(a_t >= nxt, 1.0, 0.0)
            eq_t = jnp.where((a_t >= thr) & (a_t < nxt), 1.0, 0.0)
            n_gt = jnp.sum(jnp.sum(gt, axis=1, keepdims=True), axis=0, keepdims=True)
            stage.append((a_t, gt, eq, gt_t, eq_t, cap - n_gt))
        ties = [(incl_counts(eq), incl_counts_t(eq_t)) for _, _, eq, _, eq_t, _ in stage]
        sels = []
        for (a_t, gt, eq, gt_t, eq_t, need), (c_eq, c_eq_t) in zip(stage, ties):
            sels.append((gt + eq * jnp.where(c_eq - eq < need, 1.0, 0.0),
                         gt_t + eq_t * jnp.where(c_eq_t - eq_t < need, 1.0, 0.0)))
        cnts = [(incl_counts(sel), incl_counts_t(sel_t)) for sel, sel_t in sels]
        picks = []
        for (a_t, *_), (cnt, cnt_t) in zip(stage, cnts):
            blk_end = cnt[:, LANES - 1:LANES]
            blk_of_slot = jnp.sum(jnp.where(blk_end <= slot, 1.0, 0.0), axis=0, keepdims=True)
            onehot_blk = jnp.where(blk_col == blk_of_slot, 1.0, 0.0).astype(BF16)
            cnt_rows = _dot3(cnt_t, onehot_blk)
            a_rows = _dot3(a_t, onehot_blk)
            picks.append((blk_of_slot, cnt_rows, a_rows))
        for s, (blk_of_slot, cnt_rows, a_rows) in zip(sets, picks):
            sub_of_slot = jnp.sum(jnp.where(cnt_rows <= slot, 1.0, 0.0), axis=0, keepdims=True)
            gate = jnp.sum(jnp.where(sub_col == sub_of_slot, a_rows, 0.0), axis=0, keepdims=True)
            idx_ref[s] = (blk_of_slot * LANES + sub_of_slot).astype(jnp.int32)
            gate_ref[s] = gate
        return carry

    lax.fori_loop(0, n_sets // ROUTE_GROUP, per_group, 0)


def _route_call(aff_t, n_pass, cap):
    ne, n = aff_t.shape
    n_tok = n // n_pass
    nb = n_tok // LANES
    a4 = aff_t.reshape(ne, n_pass, nb, LANES).transpose(1, 0, 2, 3).reshape(n_pass * ne, nb, LANES)
    a4_t = a4.transpose(0, 2, 1)
    n_sets = n_pass * ne
    return pl.pallas_call(
        functools.partial(_route_kernel, cap=cap, n_tok=n_tok),
        out_shape=[jax.ShapeDtypeStruct((n_sets, 1, cap), jnp.int32),
                   jax.ShapeDtypeStruct((n_sets, 1, cap), F32)],
        scratch_shapes=[pltpu.VMEM((n_sets, 1, 1), jnp.int32)],
        compiler_params=pltpu.CompilerParams(vmem_limit_bytes=32 * MIB), name="expert_choice_route",
    )(a4, a4_t)


ROW_LOOP_UNROLL = 8


def _moe_kernel(idx_ref, gate_ref, h_hbm, w1_ref, w3_ref, w2_ref, out_hbm,
                xe_scr, xb_scr, ye_a, ye_b, acc_scr, gsem, osem, *, rows, rows_pad):
    e = pl.program_id(0)
    f = pl.program_id(1)
    n_e = pl.num_programs(0)
    n_f = pl.num_programs(1)
    chunk = rows_pad // n_f

    n_blk = xb_scr.shape[1] // LANES

    def tile_of(t):
        return pl.ds(pl.multiple_of(t * n_blk, n_blk), n_blk)

    def gather_row(slot, s, priority=0):
        tok = idx_ref[slot * rows_pad + s]
        pltpu.make_async_copy(h_hbm.at[tile_of(tok), :], xe_scr.at[tile_of(s), :], gsem).start(priority=priority)

    def scatter_row(slot, s, ye_ref):
        tok = idx_ref[slot * rows_pad + s]
        g = gate_ref[slot * rows_pad + s]
        acc_scr[pl.ds(tok, 1), :] += ye_ref[pl.ds(s, 1), :] * g

    def wait_gather():
        pltpu.make_async_copy(h_hbm.at[pl.ds(0, rows_pad * n_blk), :], xe_scr, gsem).wait()

    @pl.when((e == 0) & (f == 0))
    def _():
        acc_scr[...] = jnp.zeros_like(acc_scr)
        ye_a[...] = jnp.zeros_like(ye_a)
        ye_b[...] = jnp.zeros_like(ye_b)

        def issue(s, carry):
            gather_row(1, s)
            return carry
        lax.fori_loop(0, rows_pad, issue, 0, unroll=ROW_LOOP_UNROLL)

    def step(ye_cur, ye_prev):
        @pl.when(f == 0)
        def _():
            wait_gather()
            for fb in range(n_blk):
                xb_scr[:, fb * LANES:(fb + 1) * LANES] = _tile_rows(xe_scr, fb, rows, n_blk)[...].astype(BF16)
            ye_cur[0:rows, :] = jnp.zeros((rows, ye_cur.shape[1]), F32)

        xb = xb_scr[...]
        h1 = _dot(xb, w1_ref[...].astype(BF16))
        h3 = _dot(xb, w3_ref[...].astype(BF16))
        he = (h1 * jax.nn.sigmoid(h1) * h3).astype(BF16)
        ye_cur[0:rows, :] += _dot(he, w2_ref[...].astype(BF16))
        base = f * chunk
        for r in range(chunk):
            gather_row(e + 2, base + r, priority=r % 2)
        for r in range(chunk):
            scatter_row(e, base + r, ye_prev)

        @pl.when((e == n_e - 1) & (f == n_f - 1))
        def _():
            def scatter(s, carry):
                scatter_row(n_e, s, ye_cur)
                return carry
            lax.fori_loop(0, rows, scatter, 0, unroll=ROW_LOOP_UNROLL)
            wait_gather()
            cp = pltpu.make_async_copy(acc_scr, out_hbm, osem)
            cp.start()
            cp.wait()

    @pl.when(e % 2 == 0)
    def _():
        step(ye_a, ye_b)

    @pl.when(e % 2 == 1)
    def _():
        step(ye_b, ye_a)


def _moe_call(idx, gate, h2, w1, w3, w2, layer, tf=256):
    d = w1.shape[2]
    n_blk = d // LANES
    n = h2.shape[0] // n_blk
    ne, rows = idx.shape
    dff = w1.shape[3]
    nf = dff // tf
    rows_pad = -(-rows // (8 * nf)) * 8 * nf
    pad = ((1, 1), (0, rows_pad - rows))
    idx_all = jnp.pad(idx, pad).reshape(-1)
    gate_all = jnp.pad(gate, pad).reshape(-1)
    grid_spec = pltpu.PrefetchScalarGridSpec(
        num_scalar_prefetch=1,
        grid=(ne, nf),
        in_specs=[
            pl.BlockSpec(memory_space=pltpu.SMEM),
            pl.BlockSpec(memory_space=pl.ANY),
            pl.BlockSpec((None, None, d, tf), lambda e, f, idx: (layer, e, 0, f)),
            pl.BlockSpec((None, None, d, tf), lambda e, f, idx: (layer, e, 0, f)),
            pl.BlockSpec((None, None, tf, d), lambda e, f, idx: (layer, e, f, 0)),
        ],
        out_specs=pl.BlockSpec(memory_space=pl.ANY),
        scratch_shapes=[
            pltpu.VMEM((rows_pad * n_blk, LANES), F32), pltpu.VMEM((rows, d), BF16),
            pltpu.VMEM((rows_pad, d), F32), pltpu.VMEM((rows_pad, d), F32),
            pltpu.VMEM((n, d), F32), pltpu.SemaphoreType.DMA, pltpu.SemaphoreType.DMA,
        ],
    )
    return pl.pallas_call(
        functools.partial(_moe_kernel, rows=rows, rows_pad=rows_pad), grid_spec=grid_spec,
        out_shape=jax.ShapeDtypeStruct((n, d), F32),
        compiler_params=_cparams(("arbitrary", "arbitrary"), 62), name="expert_ffn",
    )(idx_all, gate_all, h2, w1, w3, w2)


def _rope_tables(t):
    rows = t // GRID_W
    row = jnp.repeat(jnp.arange(rows, dtype=F32), GRID_W)
    colp = jnp.tile(jnp.arange(GRID_W, dtype=F32), rows)
    n_freq = HEAD_DIM // 4
    inv = ROPE_BASE ** (-jnp.arange(n_freq, dtype=F32) / n_freq)
    ang = jnp.concatenate([row[:, None] * inv, colp[:, None] * inv], axis=-1)
    return jnp.cos(ang), jnp.sin(ang)


def kernel(x_prompt, x_sample, c, state_mlstm_C, state_mlstm_n, state_mlstm_m, state_ret_S, c_ctx, w_mod, b_mod,
           norm1_w, norm2_w, w_in, mlstm_if_b, mlstm_norm_w, ret_decay_logit, ret_norm_w, w_branch_a, w_branch_b,
           w_out, router_w, ffn_w1, ffn_w3, ffn_w2, final_norm_w):
    bp, seq, d = x_prompt.shape
    db, dec_seq, _ = x_sample.shape
    depth = w_mod.shape[0]
    hm = mlstm_if_b.shape[-1]
    hr = ret_decay_logit.shape[-1]
    ne = router_w.shape[-1]
    n_prompt = bp * seq
    n_sample = db * dec_seq
    assert n_prompt == n_sample and hm * HEAD_DIM == d and hr * HEAD_DIM == d and hm == hr
    cap = EC_FACTOR * n_prompt // ne
    tm = 512
    tp = 2048
    geo = dict(n_prompt=n_prompt, dec_seq=dec_seq)

    cond8 = jnp.concatenate([c_ctx[None, :], c, jnp.zeros((8 - 1 - db, d), F32)], axis=0)
    mod = _mod_call(cond8, w_mod, b_mod)[:, :1 + db].reshape(depth, 1 + db, 6, d)

    n_m = 4 * d
    g0 = n_m + 4 * hm

    w_t = jnp.swapaxes(w_in, 1, 2)
    mq, mk, mv, mo = (k * d for k in range(4))
    rq, rk, rv, rg, ga, gb = (g0 + k * d for k in range(6))
    wg_t = w_t[:, n_m:g0]
    gate_bias = jnp.transpose(mlstm_if_b, (0, 2, 1, 3)).reshape(depth, 4 * hm, 1)
    router_wt = jnp.swapaxes(router_w, 1, 2)
    nw_m = jnp.broadcast_to(mlstm_norm_w[:, :, None], (depth, d, LANES))
    nw_r = jnp.broadcast_to(ret_norm_w[:, :, None], (depth, d, LANES))

    cos, sin = _rope_tables(dec_seq)
    cos_t = cos.T.reshape(HEAD_DIM // 2, dec_seq // tp, tp).transpose(1, 0, 2)
    sin_t = sin.T.reshape(HEAD_DIM // 2, dec_seq // tp, tp).transpose(1, 0, 2)

    x, h, gp = _resid_norm_call(x_prompt.reshape(n_prompt, d), None, None, mod[0], norm1_w[0], gate_row=0,
                                mod_rows=(0, 1), h_dtype=BF16, write_x=True,
                                x_tail=x_sample.reshape(n_sample, d), gate_prep=(wg_t[0], gate_bias[0], hm), **geo)
    m_bufs, s_buf = None, None
    y_prompt = y_sample = None
    for l in range(depth):
        proj_t = _proj_t_call(h, w_t, l, (mq, mv, mo, rq, rv, rg), (cos_t, sin_t), 3, tm=tp, **geo)
        proj_n = _proj_call(h, w_t, l, (mk, rk, ga, gb), (cos, sin), 1, tm=tp, **geo)

        mix = dict(n_heads=hm, layer=l, depth=depth)
        hm_p, *m_bufs = _mlstm_call(proj_t, proj_n, gp, nw_m[l], None, m_bufs, n_seq=bp, seq_len=seq, tok_off=0,
                                    **mix)
        (hm_s,) = _mlstm_call(proj_t, proj_n, gp, nw_m[l], (state_mlstm_C, state_mlstm_n, state_mlstm_m), None,
                              n_seq=db, seq_len=dec_seq, tok_off=n_prompt, **mix)
        hr_p, s_buf = _ret_call(proj_t, proj_n, ret_decay_logit[l], nw_r[l], None, s_buf, n_seq=bp, seq_len=seq,
                                tok_off=0, **mix)
        (hr_s,) = _ret_call(proj_t, proj_n, ret_decay_logit[l], nw_r[l], state_ret_S, None, n_seq=db,
                            seq_len=dec_seq, tok_off=n_prompt, **mix)

        x1, h2, aff_t = _merge_call(hm_p, hm_s, hr_p, hr_s, proj_n, x, mod[l], norm2_w[l], w_branch_a, w_branch_b,
                                    w_out, router_wt, l, tm=tm, **geo)
        idx, gate = _route_call(aff_t, 2, cap)
        idx = idx.reshape(2, ne, cap) + (jnp.arange(2, dtype=jnp.int32) * n_prompt)[:, None, None]
        idx = jnp.swapaxes(idx, 0, 1).reshape(ne, 2 * cap)
        gate = jnp.swapaxes(gate.reshape(2, ne, cap), 0, 1).reshape(ne, 2 * cap)
        moe = _moe_call(idx, gate, h2, ffn_w1, ffn_w3, ffn_w2, l)
        if l + 1 < depth:
            x, h, gp = _resid_norm_call(x1, moe, mod[l], mod[l + 1], norm1_w[l + 1], gate_row=5, mod_rows=(0, 1),
                                        h_dtype=BF16, write_x=True,
                                        gate_prep=(wg_t[l + 1], gate_bias[l + 1], hm), **geo)
        else:
            fin = dict(gate_row=5, mod_rows=None, h_dtype=F32, write_x=False, **geo)
            _, y_prompt = _resid_norm_call(x1, moe, mod[l], None, final_norm_w, row_off=0, n_rows=n_prompt, **fin)
            _, y_sample = _resid_norm_call(x1, moe, mod[l], None, final_norm_w, row_off=n_prompt, n_rows=n_sample,
                                           **fin)

    c_buf, n_buf, m_buf = m_bufs
    return (y_prompt.reshape(bp, seq, d), y_sample.reshape(db, dec_seq, d), c_buf, n_buf[:, :, :, :, 0, :],
            m_buf[:, :, :, :, 0, 0], s_buf)
```

```python
import functools

import jax
import jax.numpy as jnp
from jax import lax
from jax.experimental import pallas as pl
from jax.experimental.pallas import tpu as pltpu

F32 = jnp.float32
BF16 = jnp.bfloat16

GRID_W = 64
CHUNK = 128
LANES = 128
HEAD_DIM = 256
EC_FACTOR = 2
ROPE_BASE = 10000.0
EPS = 1e-6
V7X_VMEM_BYTES = 64 * 1024 * 1024
MIB = 1024 * 1024


def _cparams(semantics, vmem_mib):
    assert vmem_mib * MIB < V7X_VMEM_BYTES
    return pltpu.CompilerParams(dimension_semantics=semantics, vmem_limit_bytes=vmem_mib * MIB)


def _dot(a, b):
    return jnp.dot(a, b, preferred_element_type=F32)


def _dot_nt(a, b):
    return lax.dot_general(a, b, (((1,), (1,)), ((), ())), preferred_element_type=F32)


def _log_sigmoid(x):
    return -(jnp.maximum(-x, 0.0) + jnp.log1p(jnp.exp(-jnp.abs(x))))


def _split3(x):
    hi = x.astype(BF16)
    r1 = x - hi.astype(F32)
    mid = r1.astype(BF16)
    lo = (r1 - mid.astype(F32)).astype(BF16)
    return hi, mid, lo


def _dot3(x, m):
    hi, mid, lo = _split3(x)
    return _dot(hi, m) + _dot(mid, m) + _dot(lo, m)


def _iota(shape, dim):
    return lax.broadcasted_iota(jnp.int32, shape, dim)


def _tile_rows(ref, fb, n_tok, n_blk):
    return ref.at[pl.ds(fb, n_tok, stride=n_blk), :]


def _mod_kernel(cond_ref, w_ref, b_ref, out_ref):
    c = cond_ref[...]
    s = (c * jax.nn.sigmoid(c)).astype(BF16)
    out_ref[...] = _dot(s, w_ref[...].astype(BF16)) + b_ref[...]


def _mod_call(cond8, w_mod, b_mod):
    depth, d, w6 = w_mod.shape
    tn = 1536
    return pl.pallas_call(
        _mod_kernel,
        grid=(depth, w6 // tn),
        in_specs=[
            pl.BlockSpec((8, d), lambda l, j: (0, 0)),
            pl.BlockSpec((None, d, tn), lambda l, j: (l, 0, j)),
            pl.BlockSpec((None, 1, tn), lambda l, j: (l, 0, j)),
        ],
        out_specs=pl.BlockSpec((None, 8, tn), lambda l, j: (l, 0, j)),
        out_shape=jax.ShapeDtypeStruct((depth, 8, w6), F32),
        compiler_params=_cparams(("parallel", "parallel"), 32),
        name="adaln_mod",
    )(cond8, w_mod, b_mod.reshape(depth, 1, w6))


def _group_of_tile(i, tm, n_prompt, dec_seq):
    return jnp.maximum(i * tm - n_prompt + dec_seq, 0) // dec_seq


def _resid_norm_kernel(*refs, has_delta, has_mod, gate_row, mod_rows, write_x, n_first_tiles, gate_heads):
    it = iter(refs)
    x_ref = next(it)
    x2_ref = next(it) if n_first_tiles else None
    delta_ref = next(it) if has_delta else None
    mod_ref = next(it) if (has_delta or has_mod) else None
    nw_ref = next(it)
    wg_ref, gb_ref = (next(it), next(it)) if gate_heads else (None, None)
    xo_ref = next(it) if write_x else None
    h_ref = next(it)
    gp_ref = next(it) if gate_heads else None
    x = x_ref[...]
    if n_first_tiles:
        x = jnp.where(pl.program_id(0) < n_first_tiles, x, x2_ref[...])
    if has_delta:
        x = x + mod_ref[gate_row:gate_row + 1, :] * delta_ref[...]
    if write_x:
        xo_ref[...] = x
    y = x * lax.rsqrt(jnp.mean(x * x, axis=-1, keepdims=True) + EPS) * nw_ref[...]
    if has_mod:
        sh_row, sc_row = mod_rows
        y = y * (1.0 + mod_ref[sc_row:sc_row + 1, :]) + mod_ref[sh_row:sh_row + 1, :]
    h = y.astype(h_ref.dtype)
    h_ref[...] = h
    if gate_heads:
        _gate_prep(h, wg_ref, gb_ref, gp_ref, gate_heads)


def _resid_norm_call(x, delta, mod_gate, mod_norm, norm_w, *, gate_row, mod_rows, h_dtype, write_x,
                     n_prompt, dec_seq, tm=1024, row_off=0, n_rows=None, x_tail=None, gate_prep=None):
    n, d = x.shape
    has_delta = delta is not None
    has_mod = mod_norm is not None
    grp = functools.partial(_group_of_tile, tm=tm, n_prompt=n_prompt, dec_seq=dec_seq)
    if x_tail is None:
        n_rows = n if n_rows is None else n_rows
        toff = row_off // tm
        nft = 0
        tile_in = pl.BlockSpec((tm, d), lambda i: (i + toff, 0))
        args, specs = [x], [tile_in]
    else:
        assert row_off == 0 and n_rows is None and not has_delta
        n_rows, toff, nft = n + x_tail.shape[0], 0, n // tm
        args = [x, x_tail]
        specs = [pl.BlockSpec((tm, d), lambda i: (jnp.minimum(i, nft - 1), 0)),
                 pl.BlockSpec((tm, d), lambda i: (jnp.maximum(i - nft, 0), 0))]
    tile_out = pl.BlockSpec((tm, d), lambda i: (i, 0))
    if has_delta:
        args.append(delta)
        specs.append(tile_in)
    if has_delta or has_mod:
        mg = mod_gate if has_delta else mod_norm
        mn = mod_norm if has_mod else mod_gate
        args.append(jnp.concatenate([mg, mn], axis=1))
        specs.append(pl.BlockSpec((None, 12, d), lambda i: (grp(i + toff), 0, 0)))
    args.append(norm_w.reshape(1, d))
    specs.append(pl.BlockSpec((1, d), lambda i: (0, 0)))
    n_heads = 0
    if gate_prep is not None:
        wg_t, bias_col, n_heads = gate_prep
        args += [wg_t, bias_col]
        specs += [pl.BlockSpec(wg_t.shape, lambda i: (0, 0)), pl.BlockSpec(bias_col.shape, lambda i: (0, 0))]
    out_shape, out_specs = [], []
    if write_x:
        out_shape.append(jax.ShapeDtypeStruct((n_rows, d), F32))
        out_specs.append(tile_out)
    out_shape.append(jax.ShapeDtypeStruct((n_rows, d), h_dtype))
    out_specs.append(tile_out)
    if gate_prep is not None:
        out_shape.append(jax.ShapeDtypeStruct((n_heads, n_rows // CHUNK, 4, CHUNK), F32))
        out_specs.append(pl.BlockSpec((n_heads, tm // CHUNK, 4, CHUNK), lambda i: (0, i, 0, 0)))
    kern = functools.partial(
        _resid_norm_kernel, has_delta=has_delta, has_mod=has_mod, gate_row=gate_row,
        mod_rows=None if mod_rows is None else (6 + mod_rows[0], 6 + mod_rows[1]), write_x=write_x,
        n_first_tiles=nft, gate_heads=n_heads)
    outs = pl.pallas_call(
        kern, grid=(n_rows // tm,), in_specs=specs, out_specs=out_specs, out_shape=out_shape,
        compiler_params=_cparams(("parallel",), 48), name="resid_norm",
    )(*args)
    return outs if write_x else (None, *outs)


def _rope_pair(x1, x2, cos, sin):
    return x1 * cos - x2 * sin, x1 * sin + x2 * cos


def _weight_rows_spec(layer, row_starts, tn, d):
    def index_map(j, i):
        start = sum(jnp.where(j == k, s, 0) for k, s in enumerate(row_starts))
        return layer, pl.multiple_of(start, 8), 0
    assert all(s % 8 == 0 for s in row_starts)
    return pl.BlockSpec((pl.Element(1), pl.Element(tn), pl.Element(d)), index_map)


def _proj_kernel(h_ref, w_ref, cos_ref, sin_ref, out_ref, wb_scr, *, tm, rope_tile, n_prompt, dec_seq):
    j = pl.program_id(0)
    i = pl.program_id(1)

    @pl.when(i == 0)
    def _():
        wb_scr[...] = w_ref[0].astype(BF16)

    row0 = pl.multiple_of(i * tm, tm)
    use_rope = (j == rope_tile) & (row0 >= n_prompt)
    half = HEAD_DIM // 2

    def heads(rope):
        h = h_ref[...]
        if rope:
            pos0 = pl.multiple_of((row0 - n_prompt) % dec_seq, tm)
            cos = cos_ref[pl.ds(pos0, tm), :]
            sin = sin_ref[pl.ds(pos0, tm), :]
        for hh in range(wb_scr.shape[0] // HEAD_DIM):
            c0 = hh * HEAD_DIM
            acc = _dot_nt(h, wb_scr[c0:c0 + HEAD_DIM, :])
            if rope:
                y1, y2 = _rope_pair(acc[:, :half], acc[:, half:], cos, sin)
                out_ref[:, c0:c0 + half] = y1.astype(out_ref.dtype)
                out_ref[:, c0 + half:c0 + HEAD_DIM] = y2.astype(out_ref.dtype)
            else:
                out_ref[:, c0:c0 + HEAD_DIM] = acc.astype(out_ref.dtype)

    pl.when(use_rope)(functools.partial(heads, True))
    pl.when(jnp.logical_not(use_rope))(functools.partial(heads, False))


def _proj_call(h, w_t, layer, row_starts, rope, rope_tile, *, n_prompt, dec_seq, tm=512, tn=1024):
    n, d = h.shape
    nj = len(row_starts)
    kern = functools.partial(_proj_kernel, tm=tm, rope_tile=rope_tile, n_prompt=n_prompt, dec_seq=dec_seq)
    return pl.pallas_call(
        kern, grid=(nj, n // tm),
        in_specs=[
            pl.BlockSpec((tm, d), lambda j, i: (i, 0)),
            _weight_rows_spec(layer, row_starts, tn, d),
            pl.BlockSpec(rope[0].shape, lambda j, i: (0, 0)),
            pl.BlockSpec(rope[1].shape, lambda j, i: (0, 0)),
        ],
        out_specs=pl.BlockSpec((tm, tn), lambda j, i: (i, j)),
        out_shape=jax.ShapeDtypeStruct((n, nj * tn), BF16),
        scratch_shapes=[pltpu.VMEM((tn, d), BF16)],
        compiler_params=_cparams(("parallel", "arbitrary"), 56), name="in_proj",
    )(h, w_t, *rope)


def _proj_t_kernel(h_ref, w_ref, cos_ref, sin_ref, out_ref, wb_scr, *, tm, rope_tile, n_prompt, dec_seq):
    j = pl.program_id(0)
    i = pl.program_id(1)

    @pl.when(i == 0)
    def _():
        wb_scr[...] = w_ref[0].astype(BF16)

    row0 = pl.multiple_of(i * tm, tm)
    use_rope = (j == rope_tile) & (row0 >= n_prompt)
    half = HEAD_DIM // 2

    def heads(rope):
        h = h_ref[...]
        if rope:
            blk = ((row0 - n_prompt) % dec_seq) // tm
            cos = cos_ref[blk]
            sin = sin_ref[blk]
        for hh in range(wb_scr.shape[0] // HEAD_DIM):
            r0 = hh * HEAD_DIM
            acc = _dot_nt(wb_scr[r0:r0 + HEAD_DIM, :], h)
            if rope:
                y1, y2 = _rope_pair(acc[:half, :], acc[half:, :], cos, sin)
                y = jnp.concatenate([y1, y2], axis=0).astype(out_ref.dtype)
            else:
                y = acc.astype(out_ref.dtype)
            for s in range(tm // CHUNK):
                out_ref[s, r0:r0 + HEAD_DIM, :] = y[:, s * CHUNK:(s + 1) * CHUNK]

    pl.when(use_rope)(functools.partial(heads, True))
    pl.when(jnp.logical_not(use_rope))(functools.partial(heads, False))


def _proj_t_call(h, w_t, layer, row_starts, rope_t, rope_tile, *, n_prompt, dec_seq, tm=512, tn=1024):
    n, d = h.shape
    nj = len(row_starts)
    kern = functools.partial(_proj_t_kernel, tm=tm, rope_tile=rope_tile, n_prompt=n_prompt, dec_seq=dec_seq)
    return pl.pallas_call(
        kern, grid=(nj, n // tm),
        in_specs=[
            pl.BlockSpec((tm, d), lambda j, i: (i, 0)),
            _weight_rows_spec(layer, row_starts, tn, d),
            pl.BlockSpec(rope_t[0].shape, lambda j, i: (0, 0, 0)),
            pl.BlockSpec(rope_t[1].shape, lambda j, i: (0, 0, 0)),
        ],
        out_specs=pl.BlockSpec((tm // CHUNK, tn, CHUNK), lambda j, i: (i, j, 0)),
        out_shape=jax.ShapeDtypeStruct((n // CHUNK, nj * tn, CHUNK), BF16),
        scratch_shapes=[pltpu.VMEM((tn, d), BF16)],
        compiler_params=_cparams(("parallel", "arbitrary"), 56), name="in_proj_t",
    )(h, w_t, *rope_t)


def _gate_prep(h, wg_ref, bias_ref, out_ref, n_heads):
    tm = h.shape[0]
    g = _dot_nt(wg_ref[...].astype(BF16), h) + bias_ref[...]
    nd = 2 * n_heads
    ig = g[0:nd, :]
    lf = _log_sigmoid(g[nd:2 * nd, :])
    r = _iota((CHUNK, CHUNK), 0)
    c = _iota((CHUNK, CHUNK), 1)
    upper = jnp.where(r <= c, 1.0, 0.0).astype(BF16)
    lower = jnp.where(r >= c, 1.0, 0.0).astype(BF16)
    is_fwd = _iota((nd, CHUNK), 0) < n_heads
    for s in range(tm // CHUNK):
        sl = slice(s * CHUNK, (s + 1) * CHUNK)
        lf_c = lf[:, sl]
        b = jnp.where(is_fwd, _dot3(lf_c, upper), _dot3(lf_c, lower))
        ig_c = ig[:, sl]
        for hh in range(n_heads):
            out_ref[hh, s, 0:1, :] = ig_c[hh:hh + 1, :]
            out_ref[hh, s, 1:2, :] = ig_c[n_heads + hh:n_heads + hh + 1, :]
            out_ref[hh, s, 2:3, :] = b[hh:hh + 1, :]
            out_ref[hh, s, 3:4, :] = b[n_heads + hh:n_heads + hh + 1, :]


def _scan_loop(nc, body, init):
    if nc <= 2:
        carry = init
        for ci in range(nc):
            carry = body(ci, carry)
        return carry
    return lax.fori_loop(0, nc, body, init)


MAX_GROUP = 16
BF16_ROWS = 16
STRIP = 32


def _seqs_per_step(n_seq, nc):
    ns = max(1, MAX_GROUP // nc)
    while n_seq % ns:
        ns -= 1
    return ns


def _group_loop(n_groups, body):
    if n_groups == 1:
        body(0, 0)
    else:
        lax.fori_loop(0, n_groups, body, 0)


def _chunk_off(c):
    return c * CHUNK if isinstance(c, int) else pl.multiple_of(c * CHUNK, CHUNK)


def _mlstm_kernel(*refs, nc, ns, has_init, first_layer, write_state):
    it = iter(refs)
    qt_ref, vt_ref, ot_ref, k_ref, gp_ref, nw_ref = (next(it) for _ in range(6))
    if has_init:
        c0_ref, n0_ref, m0_ref = next(it), next(it), next(it)
    if write_state and not first_layer:
        next(it), next(it), next(it)
    hm_ref = next(it)
    if write_state:
        cout_ref, nout_ref, mout_ref = next(it), next(it), next(it)
    ct_scr, n_scr, part_scr, inc_scr, ninc_scr, rows_scr, cst_scr, nst_scr, coef_scr = (next(it) for _ in range(9))

    L = CHUNK
    grp = min(MAX_GROUP, ns * nc)
    row = _iota((L, L), 0)
    col = _iota((L, L), 1)
    k_scale = HEAD_DIM ** -0.5
    b_idx = pl.program_id(0)
    h_idx = pl.program_id(1)
    n_heads = pl.num_programs(1)

    def gates(d, c):
        g = gp_ref[c]
        ig = g[d:d + 1, :]
        brow = g[2 + d:3 + d, :]
        blast = brow[:, L - 1:L] if d == 0 else brow[:, 0:1]
        return ig, brow, blast

    def local_group(g, carry):
        cs = [g * grp + u for u in range(grp)]
        pairs = [(u, d) for u in range(grp) for d in range(2)]
        qt = [qt_ref[c] for c in cs]
        vt = [vt_ref[c] for c in cs]
        k = [k_ref[pl.ds(_chunk_off(c), L), :] for c in cs]
        qk = [_dot(k[u], qt[u]) for u in range(grp)]
        vf = [v.astype(F32) for v in vt]
        gts = {(u, d): gates(d, cs[u]) for u, d in pairs}
        for u, c in enumerate(cs):
            lhs = []
            for d in range(2):
                ig, brow, blast = gts[u, d]
                wj = blast - brow + ig
                mloc2 = jnp.max(wj, axis=1, keepdims=True)
                e = jnp.exp(wj - mloc2) * k_scale
                rows_scr[d, c, 2:3, :] = jnp.broadcast_to(mloc2, (1, L))
                rows_scr[d, c, 4:5, :] = jnp.broadcast_to(blast, (1, L))
                lhs.append(((vf[u] * e).astype(BF16), _split3(jnp.broadcast_to(e, (BF16_ROWS, L)))))
            res = _dot(jnp.concatenate([lhs[0][0], lhs[1][0], *lhs[0][1], *lhs[1][1]], axis=0), k[u])
            for d in range(2):
                inc_scr[d, c] = res[d * HEAD_DIM:(d + 1) * HEAD_DIM, :]
                r0 = 2 * HEAD_DIM + 3 * BF16_ROWS * d
                ninc_scr[d, c] = (res[r0:r0 + 8, :] + res[r0 + BF16_ROWS:r0 + BF16_ROWS + 8, :]
                                  + res[r0 + 2 * BF16_ROWS:r0 + 2 * BF16_ROWS + 8, :])
        for u, d in pairs:
            c = cs[u]
            ig, brow, blast = gts[u, d]
            key_term = jnp.broadcast_to(ig - brow, (L, L)).T
            causal = (row <= col) if d == 0 else (row >= col)
            dm = jnp.where(causal, key_term + brow, -jnp.inf)
            mloc = jnp.max(dm, axis=0, keepdims=True)
            s = qk[u] * (jnp.exp(dm - mloc) * k_scale)
            part_scr[d, c] = _dot(vt[u], s.astype(BF16))
            rows_scr[d, c, 0:1, :] = jnp.sum(s, axis=0, keepdims=True)
            rows_scr[d, c, 1:2, :] = mloc
        return carry

    _group_loop(ns * nc // grp, local_group)

    def m_step(d, c, m):
        blast = rows_scr[d, c, 4:5, :]
        mloc2 = rows_scr[d, c, 2:3, :]
        rows_scr[d, c, 3:4, :] = m
        m_new = jnp.maximum(blast + m, mloc2)
        decay = jnp.exp(blast + m - m_new)
        w_inc = jnp.exp(mloc2 - m_new)
        coef_scr[d, c, 0:1, :] = jnp.concatenate([decay] * (HEAD_DIM // L), axis=1)
        coef_scr[d, c, 1:2, :] = jnp.concatenate([w_inc] * (HEAD_DIM // L), axis=1)
        return m_new

    for sq in range(ns):
        c_lo = sq * nc
        m_init = []
        for d in range(2):
            if has_init:
                ct_scr[d] = c0_ref[d].T
                n_scr[d] = jnp.broadcast_to(n0_ref[d], n_scr.shape[1:])
                m_init.append(jnp.full((1, L), m0_ref[(b_idx * 2 + d) * n_heads + h_idx], F32))
            else:
                ct_scr[d] = jnp.zeros(ct_scr.shape[1:], F32)
                n_scr[d] = jnp.zeros(n_scr.shape[1:], F32)
                m_init.append(jnp.zeros((1, L), F32))

        def m_body(ci, ms, c_lo=c_lo):
            return m_step(0, c_lo + ci, ms[0]), m_step(1, c_lo + nc - 1 - ci, ms[1])

        m_fin = _scan_loop(nc, m_body, tuple(m_init))

        def body(ci, carry, c_lo=c_lo):
            for d, c in ((0, c_lo + ci), (1, c_lo + nc - 1 - ci)):
                decay = coef_scr[d, c, 0:1, :]
                w_inc = coef_scr[d, c, 1:2, :]
                n8 = n_scr[d]
                nst_scr[d, c] = n8
                for r0 in range(0, HEAD_DIM, STRIP):
                    ct = ct_scr[d, r0:r0 + STRIP, :]
                    cst_scr[d, c, r0:r0 + STRIP, :] = ct.astype(BF16)
                    ct_scr[d, r0:r0 + STRIP, :] = decay * ct + w_inc * inc_scr[d, c, r0:r0 + STRIP, :]
                n_scr[d] = decay * n8 + w_inc * ninc_scr[d, c]
            return carry

        _scan_loop(nc, body, 0)

        if write_state:
            lsel = (lambda d, sq=sq: (sq, 0, d)) if first_layer else (lambda d, sq=sq: (sq, d))
            for d in range(2):
                cout_ref[lsel(d)] = ct_scr[d].T
                nout_ref[lsel(d)] = n_scr[d][0:1, :]
                mout_ref[lsel(d)] = m_fin[d]
            if first_layer:
                for l in range(1, cout_ref.shape[1]):
                    cout_ref[sq, l] = jnp.zeros(cout_ref.shape[2:], F32)
                    nout_ref[sq, l] = jnp.zeros(nout_ref.shape[2:], F32)
                    mout_ref[sq, l] = jnp.zeros(mout_ref.shape[2:], F32)

    def finish_group(g, carry):
        cs = [g * grp + u for u in range(grp)]
        qt = [qt_ref[c] for c in cs]
        cq, nq = [], []
        for u, c in enumerate(cs):
            n16 = [jnp.concatenate([nst_scr[d, c]] * (BF16_ROWS // 8), axis=0).astype(BF16) for d in range(2)]
            res = _dot(jnp.concatenate([cst_scr[0, c], cst_scr[1, c], n16[0], n16[1]], axis=0), qt[u])
            cq.append([res[d * HEAD_DIM:(d + 1) * HEAD_DIM, :] for d in range(2)])
            nq.append([res[2 * HEAD_DIM + d * BF16_ROWS:2 * HEAD_DIM + d * BF16_ROWS + 1, :] for d in range(2)])
        for u, c in enumerate(cs):
            coef = []
            for d in range(2):
                _, brow, _ = gates(d, c)
                den_loc = rows_scr[d, c, 0:1, :]
                mloc = rows_scr[d, c, 1:2, :]
                inter = brow + rows_scr[d, c, 3:4, :]
                m_row = jnp.maximum(mloc, inter)
                w_loc = jnp.exp(mloc - m_row)
                w_inter = jnp.exp(inter - m_row)
                den = w_loc * den_loc + w_inter * nq[u][d]
                r_den = 1.0 / jnp.maximum(jnp.abs(den), jnp.exp(-m_row))
                coef.append((w_loc * r_den, w_inter * r_den))
            ssq = jnp.zeros((1, L), F32)
            for r0 in range(0, HEAD_DIM, STRIP):
                h_s = None
                for d in range(2):
                    h_d = coef[d][0] * part_scr[d, c, r0:r0 + STRIP, :] + coef[d][1] * cq[u][d][r0:r0 + STRIP, :]
                    h_s = h_d if h_s is None else h_s + h_d
                part_scr[0, c, r0:r0 + STRIP, :] = h_s
                ssq = ssq + jnp.sum(h_s * h_s, axis=0, keepdims=True)
            r_norm = lax.rsqrt(ssq * (1.0 / HEAD_DIM) + EPS)
            for r0 in range(0, HEAD_DIM, L):
                y = part_scr[0, c, r0:r0 + L, :] * r_norm * nw_ref[r0:r0 + L, :]
                o = jax.nn.sigmoid(ot_ref[c, r0:r0 + L, :].astype(F32)) * y
                hm_ref[pl.ds(_chunk_off(c), L), r0:r0 + L] = o.T.astype(hm_ref.dtype)
        return carry

    _group_loop(ns * nc // grp, finish_group)


def _state_specs(shapes, layer, first_layer, ns):
    specs = []
    for shp in shapes:
        tail = shp[4:]
        zeros = (0,) * len(tail)
        if first_layer:
            specs.append(pl.BlockSpec((ns, shp[1], 2, None) + tail, lambda b, h, z=zeros: (b, 0, 0, h) + z))
        else:
            specs.append(pl.BlockSpec((ns, None, 2, None) + tail, lambda b, h, z=zeros: (b, layer, 0, h) + z))
    return specs


def _mlstm_call(proj_t, proj_n, gp, nw_b, init, state_bufs, *, n_seq, seq_len, tok_off, n_heads, layer, depth):
    dk = HEAD_DIM
    nc = seq_len // CHUNK
    has_init = init is not None
    ns = 1 if has_init else _seqs_per_step(n_seq, nc)
    nct = ns * nc
    boff = tok_off // (ns * seq_len)
    write_state = not has_init
    first_layer = layer == 0
    args = [proj_t, proj_t, proj_t, proj_n, gp, nw_b]
    specs = [
        pl.BlockSpec((nct, dk, CHUNK), lambda b, h: (b + boff, h, 0)),
        pl.BlockSpec((nct, dk, CHUNK), lambda b, h: (b + boff, n_heads + h, 0)),
        pl.BlockSpec((nct, dk, CHUNK), lambda b, h: (b + boff, 2 * n_heads + h, 0)),
        pl.BlockSpec((ns * seq_len, dk), lambda b, h: (b + boff, h)),
        pl.BlockSpec((None, nct, 4, CHUNK), lambda b, h: (h, b + boff, 0, 0)),
        pl.BlockSpec((dk, LANES), lambda b, h: (h, 0)),
    ]
    aliases = {}
    if has_init:
        c0, n0, m0 = init
        args += [c0, n0[:, layer][:, :, :, None, :], m0[:, layer].reshape(-1)]
        specs += [
            pl.BlockSpec((None, None, 2, None, dk, dk), lambda b, h: (b, layer, 0, h, 0, 0)),
            pl.BlockSpec((None, 2, None, 1, dk), lambda b, h: (b, 0, h, 0, 0)),
            pl.BlockSpec(memory_space=pltpu.SMEM),
        ]
    out_shape = [jax.ShapeDtypeStruct((n_seq * seq_len, n_heads * dk), BF16)]
    out_specs = [pl.BlockSpec((ns * seq_len, dk), lambda b, h: (b, h))]
    if write_state:
        shapes = [(n_seq, depth, 2, n_heads, dk, dk), (n_seq, depth, 2, n_heads, 1, dk),
                  (n_seq, depth, 2, n_heads, 1, LANES)]
        if not first_layer:
            for k_, buf in enumerate(state_bufs):
                aliases[len(args)] = 1 + k_
                args.append(buf)
                specs.append(pl.BlockSpec(memory_space=pl.ANY))
        out_shape += [jax.ShapeDtypeStruct(s, F32) for s in shapes]
        out_specs += _state_specs(shapes, layer, first_layer, ns)
    kern = functools.partial(_mlstm_kernel, nc=nc, ns=ns, has_init=has_init, first_layer=first_layer,
                             write_state=write_state)
    return pl.pallas_call(
        kern, grid=(n_seq // ns, n_heads), in_specs=specs, out_specs=out_specs, out_shape=out_shape,
        input_output_aliases=aliases,
        scratch_shapes=[pltpu.VMEM((2, dk, dk), F32), pltpu.VMEM((2, 8, dk), F32),
                        pltpu.VMEM((2, nct, dk, CHUNK), F32), pltpu.VMEM((2, nct, dk, dk), F32),
                        pltpu.VMEM((2, nct, 8, dk), F32), pltpu.VMEM((2, nct, 8, CHUNK), F32),
                        pltpu.VMEM((2, nct, dk, dk), BF16), pltpu.VMEM((2, nct, 8, dk), F32),
                        pltpu.VMEM((2, nct, 8, dk), F32)],
        compiler_params=_cparams(("parallel", "parallel"), 48), name="mlstm_mixer",
    )(*args)


def _ret_kernel(*refs, nc, ns, has_init, first_layer, write_state):
    it = iter(refs)
    dl_ref, qt_ref, vt_ref, gt_ref, k_ref, nw_ref = (next(it) for _ in range(6))
    if has_init:
        s0_ref = next(it)
    if write_state and not first_layer:
        next(it)
    hr_ref = next(it)
    if write_state:
        sout_ref = next(it)
    st_scr, part_scr, inc_scr, sst_scr = next(it), next(it), next(it), next(it)

    L = CHUNK
    grp = min(MAX_GROUP, ns * nc)
    rowi = _iota((L, L), 0)
    coli = _iota((L, L), 1)
    k_scale = HEAD_DIM ** -0.5
    h_idx = pl.program_id(1)
    n_heads = pl.num_programs(1)

    consts = []
    for d in range(2):
        lg = _log_sigmoid(jnp.full((1, 1), dl_ref[d * n_heads + h_idx], F32))
        rel = (coli - rowi if d == 0 else rowi - coli).astype(F32)
        dmat_t = jnp.where(rel >= 0, jnp.exp(jnp.maximum(rel, 0.0) * lg), 0.0) * k_scale
        pos = _iota((1, L), 1).astype(F32)
        if d == 1:
            pos = (L - 1.0) - pos
        q_decay = jnp.exp((pos + 1.0) * lg)
        k_decay = jnp.exp((L - 1.0 - pos) * lg) * k_scale
        chunk_decay = jnp.exp(float(L) * lg)
        consts.append((dmat_t, q_decay, k_decay, chunk_decay))
    dmat_both = consts[0][0] + consts[1][0]

    def local_group(g, carry):
        cs = [g * grp + u for u in range(grp)]
        qt = [qt_ref[c] for c in cs]
        vt = [vt_ref[c] for c in cs]
        k = [k_ref[pl.ds(_chunk_off(c), L), :] for c in cs]
        qk = [_dot(k[u], qt[u]) for u in range(grp)]
        for u, c in enumerate(cs):
            vf = vt[u].astype(F32)
            res = _dot(jnp.concatenate([(vf * consts[d][2]).astype(BF16) for d in range(2)], axis=0), k[u])
            for d in range(2):
                inc_scr[d, c] = res[d * HEAD_DIM:(d + 1) * HEAD_DIM, :]
        for u, c in enumerate(cs):
            part_scr[c] = _dot(vt[u], (qk[u] * dmat_both).astype(BF16))
        return carry

    _group_loop(ns * nc // grp, local_group)

    for sq in range(ns):
        c_lo = sq * nc
        for d in range(2):
            if has_init:
                st_scr[d] = s0_ref[d].T
            else:
                st_scr[d] = jnp.zeros(st_scr.shape[1:], F32)

        def body(ci, carry, c_lo=c_lo):
            for d, c in ((0, c_lo + ci), (1, c_lo + nc - 1 - ci)):
                for r0 in range(0, HEAD_DIM, STRIP):
                    st = st_scr[d, r0:r0 + STRIP, :]
                    sst_scr[d, c, r0:r0 + STRIP, :] = st.astype(BF16)
                    st_scr[d, r0:r0 + STRIP, :] = consts[d][3] * st + inc_scr[d, c, r0:r0 + STRIP, :]
            return carry

        _scan_loop(nc, body, 0)

        if write_state:
            for d in range(2):
                if first_layer:
                    sout_ref[sq, 0, d] = st_scr[d].T
                else:
                    sout_ref[sq, d] = st_scr[d].T
            if first_layer:
                for l in range(1, sout_ref.shape[1]):
                    sout_ref[sq, l] = jnp.zeros(sout_ref.shape[2:], F32)

    def finish_group(g, carry):
        cs = [g * grp + u for u in range(grp)]
        qt = [qt_ref[c] for c in cs]
        sq = [_dot(jnp.concatenate([sst_scr[0, c], sst_scr[1, c]], axis=0), qt[u]) for u, c in enumerate(cs)]
        for u, c in enumerate(cs):
            ssq = jnp.zeros((1, L), F32)
            for r0 in range(0, HEAD_DIM, STRIP):
                sl = slice(r0, r0 + STRIP)
                o_s = part_scr[c, sl, :] + (consts[0][1] * sq[u][r0:r0 + STRIP, :]
                                            + consts[1][1] * sq[u][HEAD_DIM + r0:HEAD_DIM + r0 + STRIP, :])
                part_scr[c, sl, :] = o_s
                ssq = ssq + jnp.sum(o_s * o_s, axis=0, keepdims=True)
            r_norm = lax.rsqrt(ssq * (1.0 / HEAD_DIM) + EPS)
            for r0 in range(0, HEAD_DIM, L):
                y = part_scr[c, r0:r0 + L, :] * r_norm * nw_ref[r0:r0 + L, :]
                rg = gt_ref[c, r0:r0 + L, :].astype(F32)
                hr_ref[pl.ds(_chunk_off(c), L), r0:r0 + L] = (rg * jax.nn.sigmoid(rg) * y).T.astype(hr_ref.dtype)
        return carry

    _group_loop(ns * nc // grp, finish_group)


def _ret_call(proj_t, proj_n, decay_logit, nw_b, init, state_buf, *, n_seq, seq_len, tok_off, n_heads, layer,
              depth):
    dk = HEAD_DIM
    nc = seq_len // CHUNK
    has_init = init is not None
    ns = 1 if has_init else _seqs_per_step(n_seq, nc)
    nct = ns * nc
    boff = tok_off // (ns * seq_len)
    write_state = not has_init
    first_layer = layer == 0
    args = [decay_logit.reshape(-1), proj_t, proj_t, proj_t, proj_n, nw_b]
    specs = [
        pl.BlockSpec(memory_space=pltpu.SMEM),
        pl.BlockSpec((nct, dk, CHUNK), lambda b, h: (b + boff, 3 * n_heads + h, 0)),
        pl.BlockSpec((nct, dk, CHUNK), lambda b, h: (b + boff, 4 * n_heads + h, 0)),
        pl.BlockSpec((nct, dk, CHUNK), lambda b, h: (b + boff, 5 * n_heads + h, 0)),
        pl.BlockSpec((ns * seq_len, dk), lambda b, h: (b + boff, n_heads + h)),
        pl.BlockSpec((dk, LANES), lambda b, h: (h, 0)),
    ]
    aliases = {}
    if has_init:
        args.append(init)
        specs.append(pl.BlockSpec((None, None, 2, None, dk, dk), lambda b, h: (b, layer, 0, h, 0, 0)))
    out_shape = [jax.ShapeDtypeStruct((n_seq * seq_len, n_heads * dk), BF16)]
    out_specs = [pl.BlockSpec((ns * seq_len, dk), lambda b, h: (b, h))]
    if write_state:
        shapes = [(n_seq, depth, 2, n_heads, dk, dk)]
        if not first_layer:
            aliases[len(args)] = 1
            args.append(state_buf)
            specs.append(pl.BlockSpec(memory_space=pl.ANY))
        out_shape += [jax.ShapeDtypeStruct(s, F32) for s in shapes]
        out_specs += _state_specs(shapes, layer, first_layer, ns)
    kern = functools.partial(_ret_kernel, nc=nc, ns=ns, has_init=has_init, first_layer=first_layer,
                             write_state=write_state)
    return pl.pallas_call(
        kern, grid=(n_seq // ns, n_heads), in_specs=specs, out_specs=out_specs, out_shape=out_shape,
        input_output_aliases=aliases,
        scratch_shapes=[pltpu.VMEM((2, dk, dk), F32), pltpu.VMEM((nct, dk, CHUNK), F32),
                        pltpu.VMEM((2, nct, dk, dk), F32), pltpu.VMEM((2, nct, dk, dk), BF16)],
        compiler_params=_cparams(("parallel", "parallel"), 48), name="retention_mixer",
    )(*args)


def _merge_kernel(hmp_ref, hms_ref, hrp_ref, hrs_ref, ga_ref, gb_ref, x_ref, mod_ref, nw_ref, wa_ref, wb_ref,
                  wo_ref, wr_ref, x1_ref, h2_ref, aff_ref, wab_scr, wbb_scr, wob_scr, *, n_prompt_tiles):
    i = pl.program_id(0)

    @pl.when(i == 0)
    def _():
        wab_scr[...] = wa_ref[...].astype(BF16)
        wbb_scr[...] = wb_ref[...].astype(BF16)
        wob_scr[...] = wo_ref[...].astype(BF16)

    is_prompt = i < n_prompt_tiles
    hm = jnp.where(is_prompt, hmp_ref[...], hms_ref[...])
    hr = jnp.where(is_prompt, hrp_ref[...], hrs_ref[...])
    ya = _dot(hm, wab_scr[...])
    yb = _dot(hr, wbb_scr[...])
    merged = (jax.nn.sigmoid(ga_ref[...].astype(F32)) * ya + jax.nn.sigmoid(gb_ref[...].astype(F32)) * yb)
    y = _dot(merged.astype(BF16), wob_scr[...])
    x1 = x_ref[...] + mod_ref[2:3, :] * y
    x1_ref[...] = x1
    h2 = x1 * lax.rsqrt(jnp.mean(x1 * x1, axis=-1, keepdims=True) + EPS) * nw_ref[...]
    h2 = h2 * (1.0 + mod_ref[4:5, :]) + mod_ref[3:4, :]
    n_blk = h2.shape[1] // LANES
    for fb in range(n_blk):
        _tile_rows(h2_ref, fb, h2.shape[0], n_blk)[...] = h2[:, fb * LANES:(fb + 1) * LANES]
    logits = _dot_nt(wr_ref[...].astype(BF16), h2.astype(BF16))
    p = jnp.exp(logits - jnp.max(logits, axis=0, keepdims=True))
    aff_ref[...] = p / jnp.sum(p, axis=0, keepdims=True)


def _merge_call(hm_p, hm_s, hr_p, hr_s, proj_n, x, mod, norm_w, w_a, w_b, w_out, router_wt, layer, *,
                n_prompt, dec_seq, tm=512):
    n, d = x.shape
    ne = router_wt.shape[1]
    npt = n_prompt // tm
    grp = functools.partial(_group_of_tile, tm=tm, n_prompt=n_prompt, dec_seq=dec_seq)
    tile = pl.BlockSpec((tm, d), lambda i: (i, 0))
    tile_p = pl.BlockSpec((tm, d), lambda i: (jnp.minimum(i, npt - 1), 0))
    tile_s = pl.BlockSpec((tm, d), lambda i: (jnp.maximum(i - npt, 0), 0))
    full = pl.BlockSpec((None, d, d), lambda i: (layer, 0, 0))
    return pl.pallas_call(
        functools.partial(_merge_kernel, n_prompt_tiles=npt), grid=(n // tm,),
        in_specs=[
            tile_p, tile_s, tile_p, tile_s,
            pl.BlockSpec((tm, d), lambda i: (i, 2)),
            pl.BlockSpec((tm, d), lambda i: (i, 3)),
            tile,
            pl.BlockSpec((None, 6, d), lambda i: (grp(i), 0, 0)),
            pl.BlockSpec((1, d), lambda i: (0, 0)),
            full, full, full,
            pl.BlockSpec((None, ne, d), lambda i: (layer, 0, 0)),
        ],
        out_specs=[tile, pl.BlockSpec((tm * (d // LANES), LANES), lambda i: (i, 0)),
                   pl.BlockSpec((ne, tm), lambda i: (0, i))],
        out_shape=[jax.ShapeDtypeStruct((n, d), F32), jax.ShapeDtypeStruct((n * (d // LANES), LANES), F32),
                   jax.ShapeDtypeStruct((ne, n), F32)],
        scratch_shapes=[pltpu.VMEM((d, d), BF16)] * 3,
        compiler_params=_cparams(("arbitrary",), 56), name="merge_out_router",
    )(hm_p, hm_s, hr_p, hr_s, proj_n, proj_n, x, mod, norm_w.reshape(1, d), w_a, w_b, w_out, router_wt)


ROUTE_GROUP = 4


def _route_kernel(a_ref, at_ref, idx_ref, gate_ref, thr_scr, *, cap, n_tok):
    n_sets = a_ref.shape[0]
    nb = n_tok // LANES
    a_all = a_ref[...]

    def as_f32(bits):
        return lax.bitcast_convert_type(bits, F32)

    def count_ge(cand):
        m = jnp.where(a_all >= cand, 1.0, 0.0)
        return jnp.sum(jnp.sum(m, axis=1, keepdims=True), axis=2, keepdims=True)

    def bit_step(k, thr):
        cand = thr | lax.shift_left(jnp.int32(1), 30 - k)
        return jnp.where(count_ge(as_f32(cand)) >= cap, cand, thr)

    thr_scr[...] = lax.fori_loop(0, 31, bit_step, jnp.zeros((n_sets, 1, 1), jnp.int32))

    r128 = _iota((LANES, LANES), 0)
    c128 = _iota((LANES, LANES), 1)
    upper = jnp.where(r128 <= c128, 1.0, 0.0).astype(BF16)
    lower_t = jnp.where(r128 >= c128, 1.0, 0.0).astype(BF16)
    rb = _iota((nb, nb), 0)
    cb = _iota((nb, nb), 1)
    blk_before_rows = jnp.where(cb < rb, 1.0, 0.0).astype(BF16)
    blk_before_cols = jnp.where(rb < cb, 1.0, 0.0).astype(BF16)

    def incl_counts(mask):
        within = _dot(mask.astype(BF16), upper)
        before = _dot(blk_before_rows, within.astype(BF16))[:, LANES - 1:LANES]
        return within + before

    def incl_counts_t(mask_t):
        within = _dot(lower_t, mask_t.astype(BF16))
        before = _dot(within.astype(BF16), blk_before_cols)[LANES - 1:LANES, :]
        return within + before

    slot = _iota((1, cap), 1).astype(F32)
    blk_col = _iota((nb, 1), 0).astype(F32)
    sub_col = _iota((LANES, 1), 0).astype(F32)

    def per_group(g, carry):
        sets = [g + u * (n_sets // ROUTE_GROUP) for u in range(ROUTE_GROUP)]
        stage = []
        for s in sets:
            thr_bits = thr_scr[s]
            thr = as_f32(thr_bits)
            nxt = as_f32(thr_bits + 1)
            a = a_ref[s]
            a_t = at_ref[s]
            gt = jnp.where(a >= nxt, 1.0, 0.0)
            eq = jnp.where((a >= thr) & (a < nxt), 1.0, 0.0)
            gt_t = jnp.where(a_t >= nxt, 1.0, 0.0)
            eq_t = jnp.where((a_t >= thr) & (a_t < nxt), 1.0, 0.0)
            n_gt = jnp.sum(jnp.sum(gt, axis=1, keepdims=True), axis=0, keepdims=True)
            stage.append((a_t, gt, eq, gt_t, eq_t, cap - n_gt))
        ties = [(incl_counts(eq), incl_counts_t(eq_t)) for _, _, eq, _, eq_t, _ in stage]
        sels = []
        for (a_t, gt, eq, gt_t, eq_t, need), (c_eq, c_eq_t) in zip(stage, ties):
            sels.append((gt + eq * jnp.where(c_eq - eq < need, 1.0, 0.0),
                         gt_t + eq_t * jnp.where(c_eq_t - eq_t < need, 1.0, 0.0)))
        cnts = [(incl_counts(sel), incl_counts_t(sel_t)) for sel, sel_t in sels]
        picks = []
        for (a_t, *_), (cnt, cnt_t) in zip(stage, cnts):
            blk_end = cnt[:, LANES - 1:LANES]
            blk_of_slot = jnp.sum(jnp.where(blk_end <= slot, 1.0, 0.0), axis=0, keepdims=True)
            onehot_blk = jnp.where(blk_col == blk_of_slot, 1.0, 0.0).astype(BF16)
            cnt_rows = _dot3(cnt_t, onehot_blk)
            a_rows = _dot3(a_t, onehot_blk)
            picks.append((blk_of_slot, cnt_rows, a_rows))
        for s, (blk_of_slot, cnt_rows, a_rows) in zip(sets, picks):
            sub_of_slot = jnp.sum(jnp.where(cnt_rows <= slot, 1.0, 0.0), axis=0, keepdims=True)
            gate = jnp.sum(jnp.where(sub_col == sub_of_slot, a_rows, 0.0), axis=0, keepdims=True)
            idx_ref[s] = (blk_of_slot * LANES + sub_of_slot).astype(jnp.int32)
            gate_ref[s] = gate
        return carry

    lax.fori_loop(0, n_sets // ROUTE_GROUP, per_group, 0)


def _route_call(aff_t, n_pass, cap):
    ne, n = aff_t.shape
    n_tok = n // n_pass
    nb = n_tok // LANES
    a4 = aff_t.reshape(ne, n_pass, nb, LANES).transpose(1, 0, 2, 3).reshape(n_pass * ne, nb, LANES)
    a4_t = a4.transpose(0, 2, 1)
    n_sets = n_pass * ne
    return pl.pallas_call(
        functools.partial(_route_kernel, cap=cap, n_tok=n_tok),
        out_shape=[jax.ShapeDtypeStruct((n_sets, 1, cap), jnp.int32),
                   jax.ShapeDtypeStruct((n_sets, 1, cap), F32)],
        scratch_shapes=[pltpu.VMEM((n_sets, 1, 1), jnp.int32)],
        compiler_params=pltpu.CompilerParams(vmem_limit_bytes=32 * MIB), name="expert_choice_route",
    )(a4, a4_t)


ROW_LOOP_UNROLL = 8


def _moe_kernel(idx_ref, gate_ref, h_hbm, w1_ref, w3_ref, w2_ref, out_hbm,
                xe_scr, xb_scr, ye_a, ye_b, acc_scr, gsem, osem, *, rows, rows_pad):
    e = pl.program_id(0)
    f = pl.program_id(1)
    n_e = pl.num_programs(0)
    n_f = pl.num_programs(1)
    chunk = rows_pad // n_f

    n_blk = xb_scr.shape[1] // LANES

    def tile_of(t):
        return pl.ds(pl.multiple_of(t * n_blk, n_blk), n_blk)

    def gather_row(slot, s, priority=0):
        tok = idx_ref[slot * rows_pad + s]
        pltpu.make_async_copy(h_hbm.at[tile_of(tok), :], xe_scr.at[tile_of(s), :], gsem).start(priority=priority)

    def scatter_row(slot, s, ye_ref):
        tok = idx_ref[slot * rows_pad + s]
        g = gate_ref[slot * rows_pad + s]
        acc_scr[pl.ds(tok, 1), :] += ye_ref[pl.ds(s, 1), :] * g

    def wait_gather():
        pltpu.make_async_copy(h_hbm.at[pl.ds(0, rows_pad * n_blk), :], xe_scr, gsem).wait()

    @pl.when((e == 0) & (f == 0))
    def _():
        acc_scr[...] = jnp.zeros_like(acc_scr)
        ye_a[...] = jnp.zeros_like(ye_a)
        ye_b[...] = jnp.zeros_like(ye_b)

        def issue(i, carry):
            gather_row(1, 2 * i, priority=0)
            gather_row(1, 2 * i + 1, priority=1)
            return carry
        lax.fori_loop(0, rows_pad // 2, issue, 0, unroll=ROW_LOOP_UNROLL // 2)

    def step(ye_cur, ye_prev):
        @pl.when(f == 0)
        def _():
            wait_gather()
            for fb in range(n_blk):
                xb_scr[:, fb * LANES:(fb + 1) * LANES] = _tile_rows(xe_scr, fb, rows, n_blk)[...].astype(BF16)
            ye_cur[0:rows, :] = jnp.zeros((rows, ye_cur.shape[1]), F32)

        xb = xb_scr[...]
        h1 = _dot(xb, w1_ref[...].astype(BF16))
        h3 = _dot(xb, w3_ref[...].astype(BF16))
        he = (h1 * jax.nn.sigmoid(h1) * h3).astype(BF16)
        ye_cur[0:rows, :] += _dot(he, w2_ref[...].astype(BF16))
        base = f * chunk
        for r in range(chunk):
            gather_row(e + 2, base + r, priority=r % 2)
        for r in range(chunk):
            scatter_row(e, base + r, ye_prev)

        @pl.when((e == n_e - 1) & (f == n_f - 1))
        def _():
            def scatter(s, carry):
                scatter_row(n_e, s, ye_cur)
                return carry
            lax.fori_loop(0, rows, scatter, 0, unroll=ROW_LOOP_UNROLL)
            wait_gather()
            cp = pltpu.make_async_copy(acc_scr, out_hbm, osem)
            cp.start()
            cp.wait()

    @pl.when(e % 2 == 0)
    def _():
        step(ye_a, ye_b)

    @pl.when(e % 2 == 1)
    def _():
        step(ye_b, ye_a)


def _moe_call(idx, gate, h2, w1, w3, w2, layer, tf=256):
    d = w1.shape[2]
    n_blk = d // LANES
    n = h2.shape[0] // n_blk
    ne, rows = idx.shape
    dff = w1.shape[3]
    nf = dff // tf
    rows_pad = -(-rows // (8 * nf)) * 8 * nf
    pad = ((1, 1), (0, rows_pad - rows))
    idx_all = jnp.pad(idx, pad).reshape(-1)
    gate_all = jnp.pad(gate, pad).reshape(-1)
    grid_spec = pltpu.PrefetchScalarGridSpec(
        num_scalar_prefetch=1,
        grid=(ne, nf),
        in_specs=[
            pl.BlockSpec(memory_space=pltpu.SMEM),
            pl.BlockSpec(memory_space=pl.ANY),
            pl.BlockSpec((None, None, d, tf), lambda e, f, idx: (layer, e, 0, f)),
            pl.BlockSpec((None, None, d, tf), lambda e, f, idx: (layer, e, 0, f)),
            pl.BlockSpec((None, None, tf, d), lambda e, f, idx: (layer, e, f, 0)),
        ],
        out_specs=pl.BlockSpec(memory_space=pl.ANY),
        scratch_shapes=[
            pltpu.VMEM((rows_pad * n_blk, LANES), F32), pltpu.VMEM((rows, d), BF16),
            pltpu.VMEM((rows_pad, d), F32), pltpu.VMEM((rows_pad, d), F32),
            pltpu.VMEM((n, d), F32), pltpu.SemaphoreType.DMA, pltpu.SemaphoreType.DMA,
        ],
    )
    return pl.pallas_call(
        functools.partial(_moe_kernel, rows=rows, rows_pad=rows_pad), grid_spec=grid_spec,
        out_shape=jax.ShapeDtypeStruct((n, d), F32),
        compiler_params=_cparams(("arbitrary", "arbitrary"), 62), name="expert_ffn",
    )(idx_all, gate_all, h2, w1, w3, w2)


def _rope_tables(t):
    rows = t // GRID_W
    row = jnp.repeat(jnp.arange(rows, dtype=F32), GRID_W)
    colp = jnp.tile(jnp.arange(GRID_W, dtype=F32), rows)
    n_freq = HEAD_DIM // 4
    inv = ROPE_BASE ** (-jnp.arange(n_freq, dtype=F32) / n_freq)
    ang = jnp.concatenate([row[:, None] * inv, colp[:, None] * inv], axis=-1)
    return jnp.cos(ang), jnp.sin(ang)


def kernel(x_prompt, x_sample, c, state_mlstm_C, state_mlstm_n, state_mlstm_m, state_ret_S, c_ctx, w_mod, b_mod,
           norm1_w, norm2_w, w_in, mlstm_if_b, mlstm_norm_w, ret_decay_logit, ret_norm_w, w_branch_a, w_branch_b,
           w_out, router_w, ffn_w1, ffn_w3, ffn_w2, final_norm_w):
    bp, seq, d = x_prompt.shape
    db, dec_seq, _ = x_sample.shape
    depth = w_mod.shape[0]
    hm = mlstm_if_b.shape[-1]
    hr = ret_decay_logit.shape[-1]
    ne = router_w.shape[-1]
    n_prompt = bp * seq
    n_sample = db * dec_seq
    assert n_prompt == n_sample and hm * HEAD_DIM == d and hr * HEAD_DIM == d and hm == hr
    cap = EC_FACTOR * n_prompt // ne
    tm = 512
    tp = 2048
    geo = dict(n_prompt=n_prompt, dec_seq=dec_seq)

    cond8 = jnp.concatenate([c_ctx[None, :], c, jnp.zeros((8 - 1 - db, d), F32)], axis=0)
    mod = _mod_call(cond8, w_mod, b_mod)[:, :1 + db].reshape(depth, 1 + db, 6, d)

    n_m = 4 * d
    g0 = n_m + 4 * hm

    w_t = jnp.swapaxes(w_in, 1, 2)
    mq, mk, mv, mo = (k * d for k in range(4))
    rq, rk, rv, rg, ga, gb = (g0 + k * d for k in range(6))
    wg_t = w_t[:, n_m:g0]
    gate_bias = jnp.transpose(mlstm_if_b, (0, 2, 1, 3)).reshape(depth, 4 * hm, 1)
    router_wt = jnp.swapaxes(router_w, 1, 2)
    nw_m = jnp.broadcast_to(mlstm_norm_w[:, :, None], (depth, d, LANES))
    nw_r = jnp.broadcast_to(ret_norm_w[:, :, None], (depth, d, LANES))

    cos, sin = _rope_tables(dec_seq)
    cos_t = cos.T.reshape(HEAD_DIM // 2, dec_seq // tp, tp).transpose(1, 0, 2)
    sin_t = sin.T.reshape(HEAD_DIM // 2, dec_seq // tp, tp).transpose(1, 0, 2)

    x, h, gp = _resid_norm_call(x_prompt.reshape(n_prompt, d), None, None, mod[0], norm1_w[0], gate_row=0,
                                mod_rows=(0, 1), h_dtype=BF16, write_x=True,
                                x_tail=x_sample.reshape(n_sample, d), gate_prep=(wg_t[0], gate_bias[0], hm), **geo)
    m_bufs, s_buf = None, None
    y_prompt = y_sample = None
    for l in range(depth):
        proj_t = _proj_t_call(h, w_t, l, (mq, mv, mo, rq, rv, rg), (cos_t, sin_t), 3, tm=tp, **geo)
        proj_n = _proj_call(h, w_t, l, (mk, rk, ga, gb), (cos, sin), 1, tm=tp, **geo)

        mix = dict(n_heads=hm, layer=l, depth=depth)
        hm_p, *m_bufs = _mlstm_call(proj_t, proj_n, gp, nw_m[l], None, m_bufs, n_seq=bp, seq_len=seq, tok_off=0,
                                    **mix)
        (hm_s,) = _mlstm_call(proj_t, proj_n, gp, nw_m[l], (state_mlstm_C, state_mlstm_n, state_mlstm_m), None,
                              n_seq=db, seq_len=dec_seq, tok_off=n_prompt, **mix)
        hr_p, s_buf = _ret_call(proj_t, proj_n, ret_decay_logit[l], nw_r[l], None, s_buf, n_seq=bp, seq_len=seq,
                                tok_off=0, **mix)
        (hr_s,) = _ret_call(proj_t, proj_n, ret_decay_logit[l], nw_r[l], state_ret_S, None, n_seq=db,
                            seq_len=dec_seq, tok_off=n_prompt, **mix)

        x1, h2, aff_t = _merge_call(hm_p, hm_s, hr_p, hr_s, proj_n, x, mod[l], norm2_w[l], w_branch_a, w_branch_b,
                                    w_out, router_wt, l, tm=tm, **geo)
        idx, gate = _route_call(aff_t, 2, cap)
        idx = idx.reshape(2, ne, cap) + (jnp.arange(2, dtype=jnp.int32) * n_prompt)[:, None, None]
        idx = jnp.swapaxes(idx, 0, 1).reshape(ne, 2 * cap)
        gate = jnp.swapaxes(gate.reshape(2, ne, cap), 0, 1).reshape(ne, 2 * cap)
        moe = _moe_call(idx, gate, h2, ffn_w1, ffn_w3, ffn_w2, l)
        if l + 1 < depth:
            x, h, gp = _resid_norm_call(x1, moe, mod[l], mod[l + 1], norm1_w[l + 1], gate_row=5, mod_rows=(0, 1),
                                        h_dtype=BF16, write_x=True,
                                        gate_prep=(wg_t[l + 1], gate_bias[l + 1], hm), **geo)
        else:
            fin = dict(gate_row=5, mod_rows=None, h_dtype=F32, write_x=False, **geo)
            _, y_prompt = _resid_norm_call(x1, moe, mod[l], None, final_norm_w, row_off=0, n_rows=n_prompt, **fin)
            _, y_sample = _resid_norm_call(x1, moe, mod[l], None, final_norm_w, row_off=n_prompt, n_rows=n_sample,
                                           **fin)

    c_buf, n_buf, m_buf = m_bufs
    return (y_prompt.reshape(bp, seq, d), y_sample.reshape(db, dec_seq, d), c_buf, n_buf[:, :, :, :, 0, :],
            m_buf[:, :, :, :, 0, 0], s_buf)
```
